```python
import jax, jax.numpy as jnp
from jax import lax
import numpy as np

D_MODEL = 1024
BATCH = 8
SEQ = 2048
DEPTH = 2

CTX_LEN = 256
GRID_W = 64
D_MIX = D_MODEL
RW_WIDTH = D_MIX // 2
RW_HEAD = 64
RW_HEADS = RW_WIDTH // RW_HEAD
RW_DECAY_RANK = 64
RW_A_RANK = 64
RW_V_RANK = 32
RW_G_RANK = 128
RW_GN_EPS = 64e-5
GLA_WIDTH = D_MIX - RW_WIDTH
GLA_HEADS = 4
GLA_DV = GLA_WIDTH // GLA_HEADS
GLA_DK = GLA_DV // 2
GLA_KW = GLA_HEADS * GLA_DK
GLA_GATE_RANK = 16
GLA_GATE_NORM = 16.0
GLA_CHUNK = 64
GLA_CONV = 3
GLA_NORM_EPS = 1e-5
D_FF = 2816
N_EXPERTS = 8
TOP_K = 2
N_DENSE = (DEPTH + 1) // 2
N_MOE = DEPTH // 2
NORM_EPS = 1e-6
RW_COLS = 3 * RW_WIDTH + 2 * RW_DECAY_RANK + 2 * RW_A_RANK + RW_G_RANK
GLA_QKV = 2 * GLA_KW + GLA_WIDTH
GLA_COLS = GLA_QKV + GLA_WIDTH + 2 * GLA_GATE_RANK
N_IN = RW_COLS + GLA_COLS
RW_SPLITS = [RW_WIDTH, 2 * RW_WIDTH, 3 * RW_WIDTH, 3 * RW_WIDTH + 2 * RW_DECAY_RANK, 3 * RW_WIDTH + 2 * RW_DECAY_RANK + 2 * RW_A_RANK]

kernel_name = 'hymba_rwkv7_gla_moe_prefix_dit'


def rmsnorm(x, g):
    xf = x.astype(jnp.float32)
    return xf * lax.rsqrt(jnp.mean(xf * xf, -1, keepdims=True) + NORM_EPS) * g.astype(jnp.float32)


def modulate(h, shift, scale):
    return h * (1.0 + scale) + shift


def split_heads(t, n_heads):
    return t.reshape(t.shape[:-1] + (n_heads, t.shape[-1] // n_heads))


def grid_qshift(p, rows):
    B, S, C = p.shape
    g = p.reshape(B, rows, GRID_W, C // 4, 4)
    left = jnp.pad(g[:, :, :-1, :, 0], ((0, 0), (0, 0), (1, 0), (0, 0)))
    right = jnp.pad(g[:, :, 1:, :, 1], ((0, 0), (0, 0), (0, 1), (0, 0)))
    up = jnp.pad(g[:, :-1, :, :, 2], ((0, 0), (1, 0), (0, 0), (0, 0)))
    down = jnp.pad(g[:, 1:, :, :, 3], ((0, 0), (0, 1), (0, 0), (0, 0)))
    return jnp.stack([left, right, up, down], -1).reshape(B, S, C)


def seq_bishift(p):
    B, L, C = p.shape
    g = p.reshape(B, L, C // 2, 2)
    prev = jnp.pad(g[:, :-1, :, 0], ((0, 0), (1, 0), (0, 0)))
    nxt = jnp.pad(g[:, 1:, :, 1], ((0, 0), (0, 1), (0, 0)))
    return jnp.stack([prev, nxt], -1).reshape(B, L, C)


def centred_dwconv(u, w):
    K, C = w.shape
    pad = K // 2
    return lax.conv_general_dilated(u, w.astype(jnp.float32)[:, None, :], window_strides=(1,),
                                    padding=[(pad, K - 1 - pad)], dimension_numbers=('NWC', 'WIO', 'NWC'),
                                    feature_group_count=C)


def rwkv_inputs(p, p_shift, mu, w_up, w0, a_up, a0, k_k, k_a, g_up):
    u = (p + mu * (p_shift - p)).astype(jnp.float32)
    r, k, v, wd, ad, gd = jnp.split(u, RW_SPLITS, axis=-1)
    B, T = u.shape[:2]
    wd = wd.reshape(B, T, 2, RW_DECAY_RANK)
    ad = ad.reshape(B, T, 2, RW_A_RANK)
    w_logit = w0 + jnp.einsum('btdr,drc->btdc', jnp.tanh(wd), w_up)
    decay = jnp.exp(-jnp.exp(-jax.nn.softplus(-w_logit) - 0.5))
    a = jax.nn.sigmoid(a0 + jnp.einsum('btdr,drc->btdc', ad, a_up))
    kk = split_heads(k * k_k, RW_HEADS)
    kk = kk / jnp.maximum(jnp.sqrt(jnp.sum(kk * kk, -1, keepdims=True)), 1e-12)
    k_eff = k[:, :, None, :] * (1.0 + (a - 1.0) * k_a)
    g = jax.nn.sigmoid(gd) @ g_up
    return r, v, kk, decay, a, k_eff, g


def value_residual(v, v_first, v_down, v_up, v0):
    return v + (v_first - v) * jax.nn.sigmoid(v0 + (v @ v_down) @ v_up)


def rwkv7_scan(r, w, k, v, kk, a, s0, reverse):
    def step(s, inp):
        r_t, w_t, k_t, v_t, kk_t, a_t = inp
        s = (s * w_t[:, :, None, :]
             - jnp.einsum('bhvk,bhk->bhv', s, kk_t)[..., None] * (kk_t * a_t)[:, :, None, :]
             + v_t[..., None] * k_t[:, :, None, :])
        return s, jnp.einsum('bhvk,bhk->bhv', s, r_t)
    xs = tuple(jnp.swapaxes(t, 0, 1) for t in (r, w, k, v, kk, a))
    s, ys = lax.scan(step, s0, xs, reverse=reverse)
    return jnp.swapaxes(ys, 0, 1), s


def rwkv_bidir(r, v, kk, decay, a, k_eff, s_init, r_k):
    rh, vh = split_heads(r, RW_HEADS), split_heads(v, RW_HEADS)
    y, bonus, finals = 0.0, 0.0, []
    for d in range(2):
        kd = split_heads(k_eff[:, :, d], RW_HEADS)
        yd, sd = rwkv7_scan(rh, split_heads(decay[:, :, d], RW_HEADS), kd, vh, kk,
                            split_heads(a[:, :, d], RW_HEADS), s_init[d], reverse=(d == 1))
        y = y + yd
        bonus = bonus + jnp.sum(rh * kd * r_k, -1, keepdims=True) * vh
        finals.append(sd)
    return y, bonus, finals


def rwkv_readout(y, bonus, g, gn_w, gn_b):
    B, T = y.shape[:2]
    mu = jnp.mean(y, -1, keepdims=True)
    var = jnp.mean(jnp.square(y - mu), -1, keepdims=True)
    yn = ((y - mu) * lax.rsqrt(var + RW_GN_EPS)).reshape(B, T, RW_WIDTH) * gn_w + gn_b
    return (yn + bonus.reshape(B, T, RW_WIDTH)) * g


def gla_inputs(p, conv_w, a_up, a_b):
    p = p.astype(jnp.float32)
    qkv, og, ad = jnp.split(p, [GLA_QKV, GLA_QKV + GLA_WIDTH], axis=-1)
    qkv = jax.nn.silu(centred_dwconv(qkv, conv_w))
    q, k, v = jnp.split(qkv, [GLA_KW, 2 * GLA_KW], axis=-1)
    B, T = p.shape[:2]
    ad = ad.reshape(B, T, 2, GLA_GATE_RANK)
    logg = jax.nn.log_sigmoid(jnp.einsum('btdr,drc->btdc', ad, a_up) + a_b) / GLA_GATE_NORM
    return (split_heads(q * GLA_DK ** -0.5, GLA_HEADS), split_heads(k, GLA_HEADS),
            split_heads(v, GLA_HEADS), og, logg)


def gla_chunked(q, k, v, logg, s0):
    B, T, H, DK = q.shape
    DV = v.shape[-1]
    n = T // GLA_CHUNK

    def blocks(t):
        return t.reshape(B, n, GLA_CHUNK, H, t.shape[-1]).transpose(1, 0, 3, 2, 4)
    qb, kb, vb, gb = blocks(q), blocks(k), blocks(v), blocks(logg)
    cum = jnp.cumsum(gb, axis=3)
    last = cum[:, :, :, -1:, :]
    q_dec = qb * jnp.exp(cum)
    k_inv = kb * jnp.exp(-cum)
    k_end = kb * jnp.exp(last - cum)
    lower = jnp.tril(jnp.ones((GLA_CHUNK, GLA_CHUNK), bool))
    att = jnp.where(lower, jnp.einsum('nbhcd,nbhsd->nbhcs', q_dec, k_inv), 0.0)
    o_intra = jnp.einsum('nbhcs,nbhsv->nbhcv', att, vb)

    def step(s, inp):
        q_c, k_c, v_c, dec_c = inp
        o = jnp.einsum('bhcd,bhdv->bhcv', q_c, s)
        s = dec_c[:, :, 0, :, None] * s + jnp.einsum('bhcd,bhcv->bhdv', k_c, v_c)
        return s, o
    s, o_inter = lax.scan(step, s0, (q_dec, k_end, vb, jnp.exp(last)))
    o = (o_intra + o_inter).transpose(1, 0, 3, 2, 4).reshape(B, T, H, DV)
    return o, s


def gla_bidir(q, k, v, logg, s_init):
    lg_f = split_heads(logg[:, :, 0], GLA_HEADS)
    lg_b = split_heads(logg[:, :, 1], GLA_HEADS)
    o_f, s_f = gla_chunked(q, k, v, lg_f, s_init[0])
    o_b, s_b = gla_chunked(jnp.flip(q, 1), jnp.flip(k, 1), jnp.flip(v, 1), jnp.flip(lg_b, 1), s_init[1])
    return o_f + jnp.flip(o_b, 1), [s_f, s_b]


def gla_readout(o, og, gn_w):
    B, T = o.shape[:2]
    on = o * lax.rsqrt(jnp.mean(o * o, -1, keepdims=True) + GLA_NORM_EPS)
    return on.reshape(B, T, GLA_WIDTH) * gn_w * jax.nn.silu(og)


def token_mixer(hc, hx, rows, v_first_c, v_first_x, vres, need_ctx,
                w_in, shift_mu, rw_w_up, rw_w0, rw_a_up, rw_a0, rw_k_k, rw_k_a, rw_r_k, rw_g_up,
                rw_gn_w, rw_gn_b, gla_conv, gla_a_up, gla_a_b, gla_gn_w, w_out):
    B = hx.shape[0]
    pc, px = hc @ w_in, hx @ w_in
    rc, gc = pc[..., :RW_COLS], pc[..., RW_COLS:]
    rx, gx = px[..., :RW_COLS], px[..., RW_COLS:]
    rw_par = (shift_mu, rw_w_up, rw_w0, rw_a_up, rw_a0, rw_k_k, rw_k_a, rw_g_up)
    r_c, v_c, kk_c, dec_c, a_c, ke_c, g_c = rwkv_inputs(rc, seq_bishift(rc), *rw_par)
    r_x, v_x, kk_x, dec_x, a_x, ke_x, g_x = rwkv_inputs(rx, grid_qshift(rx, rows), *rw_par)
    if vres is None:
        vm_c, vm_x = v_c, v_x
    else:
        vm_c = value_residual(v_c, v_first_c, *vres)
        vm_x = value_residual(v_x, v_first_x, *vres)
    z_rw = jnp.zeros((B, RW_HEADS, RW_HEAD, RW_HEAD), jnp.float32)
    y_c, b_c, s_rw = rwkv_bidir(r_c, vm_c, kk_c, dec_c, a_c, ke_c, [z_rw, z_rw], rw_r_k)
    y_x, b_x, _ = rwkv_bidir(r_x, vm_x, kk_x, dec_x, a_x, ke_x, s_rw, rw_r_k)
    q_c, k_c, gv_c, og_c, lg_c = gla_inputs(gc, gla_conv, gla_a_up, gla_a_b)
    q_x, k_x, gv_x, og_x, lg_x = gla_inputs(gx, gla_conv, gla_a_up, gla_a_b)
    z_gla = jnp.zeros((B, GLA_HEADS, GLA_DK, GLA_DV), jnp.float32)
    o_c, s_gla = gla_bidir(q_c, k_c, gv_c, lg_c, [z_gla, z_gla])
    o_x, _ = gla_bidir(q_x, k_x, gv_x, lg_x, s_gla)
    out_x = jnp.concatenate([rwkv_readout(y_x, b_x, g_x, rw_gn_w, rw_gn_b),
                             gla_readout(o_x, og_x, gla_gn_w)], -1) @ w_out
    out_c = None
    if need_ctx:
        out_c = jnp.concatenate([rwkv_readout(y_c, b_c, g_c, rw_gn_w, rw_gn_b),
                                 gla_readout(o_c, og_c, gla_gn_w)], -1) @ w_out
    return out_c, out_x, v_c, v_x


def swiglu(h, wg, wu, wd):
    return (jax.nn.silu(h @ wg) * (h @ wu)) @ wd


def moe_swiglu(h, router, e_gate, e_up, e_down):
    B, T, D = h.shape
    t = h.reshape(-1, D)
    logits = (t @ router).astype(jnp.float32)
    top_v, top_i = lax.top_k(logits, TOP_K)
    top_w = jax.nn.softmax(top_v, -1)
    comb = jnp.sum(jax.nn.one_hot(top_i, N_EXPERTS, dtype=jnp.float32) * top_w[..., None], axis=1)
    y = jnp.zeros(t.shape, jnp.float32)
    for e in range(N_EXPERTS):
        y = y + comb[:, e:e + 1] * swiglu(t, e_gate[e], e_up[e], e_down[e]).astype(jnp.float32)
    return y.reshape(B, T, D)


def _normal(k, shape, scale):
    return scale * jax.random.normal(k, shape, jnp.float32)


def setup_inputs(seed: int = 0) -> dict:
    key = jax.random.key(seed)
    ks = iter(jax.random.split(key, 48))
    D, L = D_MODEL, DEPTH
    return {
        'x': _normal(next(ks), (BATCH, SEQ, D), 1.0),
        'c': _normal(next(ks), (BATCH, D), 1.0),
        'ctx': _normal(next(ks), (BATCH, CTX_LEN, D), 1.0),
        'c_ctx': _normal(next(ks), (D,), 1.0),
        'ada_w': _normal(next(ks), (L, D, 6 * D), 0.5 * D ** -0.5),
        'ada_b': _normal(next(ks), (L, 6 * D), 0.02),
        'norm_mix_pre': 1.0 + _normal(next(ks), (L, D), 0.05),
        'norm_mix_post': 1.0 + _normal(next(ks), (L, D), 0.05),
        'norm_ffn_pre': 1.0 + _normal(next(ks), (L, D), 0.05),
        'norm_ffn_post': 1.0 + _normal(next(ks), (L, D), 0.05),
        'w_in': _normal(next(ks), (L, D, N_IN), D ** -0.5),
        'shift_mu': jax.random.uniform(next(ks), (L, RW_COLS), jnp.float32),
        'rw_w_up': _normal(next(ks), (L, 2, RW_DECAY_RANK, RW_WIDTH), 0.1),
        'rw_w0': jax.random.uniform(next(ks), (L, 2, RW_WIDTH), jnp.float32, -6.0, -1.0),
        'rw_a_up': _normal(next(ks), (L, 2, RW_A_RANK, RW_WIDTH), 0.1),
        'rw_a0': _normal(next(ks), (L, 2, RW_WIDTH), 0.1),
        'rw_k_k': 0.85 + _normal(next(ks), (L, RW_WIDTH), 0.05),
        'rw_k_a': 1.0 + _normal(next(ks), (L, RW_WIDTH), 0.05),
        'rw_r_k': _normal(next(ks), (L, RW_HEADS, RW_HEAD), 0.1),
        'rw_g_up': _normal(next(ks), (L, RW_G_RANK, RW_WIDTH), RW_G_RANK ** -0.5),
        'rw_gn_w': 1.0 + _normal(next(ks), (L, RW_WIDTH), 0.05),
        'rw_gn_b': _normal(next(ks), (L, RW_WIDTH), 0.02),
        'rw_v_down': _normal(next(ks), (L - 1, RW_WIDTH, RW_V_RANK), RW_WIDTH ** -0.5),
        'rw_v_up': _normal(next(ks), (L - 1, RW_V_RANK, RW_WIDTH), 0.1),
        'rw_v0': _normal(next(ks), (L - 1, RW_WIDTH), 0.5),
        'gla_conv': _normal(next(ks), (L, GLA_CONV, GLA_QKV), 0.5),
        'gla_a_up': _normal(next(ks), (L, 2, GLA_GATE_RANK, GLA_KW), GLA_GATE_RANK ** -0.5),
        'gla_a_b': 1.0 + _normal(next(ks), (L, 2, GLA_KW), 0.5),
        'gla_gn_w': 1.0 + _normal(next(ks), (L, GLA_WIDTH), 0.05),
        'w_out': _normal(next(ks), (L, D_MIX, D), D_MIX ** -0.5),
        'ffn_w_gate': _normal(next(ks), (N_DENSE, D, D_FF), D ** -0.5),
        'ffn_w_up': _normal(next(ks), (N_DENSE, D, D_FF), D ** -0.5),
        'ffn_w_down': _normal(next(ks), (N_DENSE, D_FF, D), D_FF ** -0.5),
        'moe_router': _normal(next(ks), (N_MOE, D, N_EXPERTS), D ** -0.5),
        'moe_w_gate': _normal(next(ks), (N_MOE, N_EXPERTS, D, D_FF), D ** -0.5),
        'moe_w_up': _normal(next(ks), (N_MOE, N_EXPERTS, D, D_FF), D ** -0.5),
        'moe_w_down': _normal(next(ks), (N_MOE, N_EXPERTS, D_FF, D), D_FF ** -0.5),
    }


def reference(x, c, ctx, c_ctx, ada_w, ada_b, norm_mix_pre, norm_mix_post, norm_ffn_pre, norm_ffn_post,
              w_in, shift_mu, rw_w_up, rw_w0, rw_a_up, rw_a0, rw_k_k, rw_k_a, rw_r_k, rw_g_up,
              rw_gn_w, rw_gn_b, rw_v_down, rw_v_up, rw_v0, gla_conv, gla_a_up, gla_a_b, gla_gn_w, w_out,
              ffn_w_gate, ffn_w_up, ffn_w_down, moe_router, moe_w_gate, moe_w_up, moe_w_down):
    rows = x.shape[1] // GRID_W
    silu_c = jax.nn.silu(c)
    silu_cc = jax.nn.silu(c_ctx)
    h_ctx = ctx
    v_first_c = v_first_x = None
    for i in range(DEPTH):
        last = i == DEPTH - 1
        mod_x = jnp.split((silu_c @ ada_w[i] + ada_b[i])[:, None, :], 6, axis=-1)
        mod_c = jnp.split((silu_cc @ ada_w[i] + ada_b[i])[None, None, :], 6, axis=-1)
        hx = modulate(rmsnorm(x, norm_mix_pre[i]), mod_x[0], mod_x[1])
        hc = modulate(rmsnorm(h_ctx, norm_mix_pre[i]), mod_c[0], mod_c[1])
        vres = None if i == 0 else (rw_v_down[i - 1], rw_v_up[i - 1], rw_v0[i - 1])
        mc, mx, v_c, v_x = token_mixer(hc, hx, rows, v_first_c, v_first_x, vres, not last,
                                       w_in[i], shift_mu[i], rw_w_up[i], rw_w0[i], rw_a_up[i], rw_a0[i],
                                       rw_k_k[i], rw_k_a[i], rw_r_k[i], rw_g_up[i], rw_gn_w[i], rw_gn_b[i],
                                       gla_conv[i], gla_a_up[i], gla_a_b[i], gla_gn_w[i], w_out[i])
        if i == 0:
            v_first_c, v_first_x = v_c, v_x
        x = x + (mod_x[2] * rmsnorm(mx, norm_mix_post[i])).astype(x.dtype)
        if not last:
            h_ctx = h_ctx + (mod_c[2] * rmsnorm(mc, norm_mix_post[i])).astype(h_ctx.dtype)
        j = i // 2
        if i % 2 == 0:
            ffn = lambda h: swiglu(h, ffn_w_gate[j], ffn_w_up[j], ffn_w_down[j])
        else:
            ffn = lambda h: moe_swiglu(h, moe_router[j], moe_w_gate[j], moe_w_up[j], moe_w_down[j])
        fx = ffn(modulate(rmsnorm(x, norm_ffn_pre[i]), mod_x[3], mod_x[4]).astype(x.dtype))
        x = x + (mod_x[5] * rmsnorm(fx, norm_ffn_post[i])).astype(x.dtype)
        if not last:
            fc = ffn(modulate(rmsnorm(h_ctx, norm_ffn_pre[i]), mod_c[3], mod_c[4]).astype(h_ctx.dtype))
            h_ctx = h_ctx + (mod_c[5] * rmsnorm(fc, norm_ffn_post[i])).astype(h_ctx.dtype)
    return x
```

```python
import functools
import math

import jax
import jax.numpy as jnp
from jax import lax
from jax.experimental import pallas as pl
from jax.experimental.pallas import tpu as pltpu

F32, BF16 = jnp.float32, jnp.bfloat16

D = 1024
GRID_W = 64
RW_WIDTH = 512
RW_HEAD = 64
RW_RANK = 64
RW_G_RANK = 128
RW_V_RANK = 32
RW_GN_EPS = 64e-5
GLA_WIDTH = 512
GLA_HEADS = 4
GLA_DV = 128
GLA_DK = 64
GLA_KW = 256
GLA_GATE_RANK = 16
GLA_GATE_NORM = 16.0
GLA_NORM_EPS = 1e-5
D_FF = 2816
N_EXPERTS = 8
NORM_EPS = 1e-6
RW_COLS = 3 * RW_WIDTH + 4 * RW_RANK + RW_G_RANK
GLA_QKV = 2 * GLA_KW + GLA_WIDTH
GLA_COLS = GLA_QKV + GLA_WIDTH + 2 * GLA_GATE_RANK
GLA_PAD = 1664

LANES = 128
TB = 256
CH = 64
FCH = 256
VMEM_LIMIT = 56 * 1024 * 1024


def _cparams(sem):
    return pltpu.CompilerParams(dimension_semantics=sem, vmem_limit_bytes=VMEM_LIMIT)


def _bdot(a, b):
    return jnp.dot(a.astype(BF16), b.astype(BF16), preferred_element_type=F32)


def _bdot_nt(a, b):
    return lax.dot_general(a.astype(BF16), b.astype(BF16), (((1,), (1,)), ((), ())),
                           preferred_element_type=F32)


def _bdot_tn(a, b):
    return lax.dot_general(a.astype(BF16), b.astype(BF16), (((0,), (0,)), ((), ())),
                           preferred_element_type=F32)


def _split_dot(a_exact, x):
    h1 = x.astype(BF16)
    r1 = x - h1.astype(F32)
    h2 = r1.astype(BF16)
    h3 = (r1 - h2.astype(F32)).astype(BF16)
    return (jnp.dot(a_exact, h1, preferred_element_type=F32)
            + jnp.dot(a_exact, h2, preferred_element_type=F32)
            + jnp.dot(a_exact, h3, preferred_element_type=F32))


def _seg_sum(x, bd):
    hi = x.astype(BF16)
    lo = (x - hi.astype(F32)).astype(BF16)
    return jnp.dot(hi, bd, preferred_element_type=F32) + jnp.dot(lo, bd, preferred_element_type=F32)


def _sigmoid(x):
    return jax.nn.sigmoid(x)


def _rms(x, g):
    return x * lax.rsqrt(jnp.mean(x * x, axis=-1, keepdims=True) + NORM_EPS) * g


def _pair_stack(z, lo):
    return jnp.concatenate([jnp.where(lo, z, 0.0), jnp.where(lo, 0.0, z)], axis=0)


def _adaln_kernel(c_ref, w_ref, b_ref, o_ref):
    c = c_ref[...]
    s = c * _sigmoid(c)
    o_ref[...] = jnp.dot(s, w_ref[...], precision=lax.Precision.HIGHEST,
                         preferred_element_type=F32) + b_ref[...]


def _adaln(cvec, w, b):
    rows = cvec.shape[0]
    n = w.shape[1]
    return pl.pallas_call(
        _adaln_kernel,
        out_shape=jax.ShapeDtypeStruct((rows, n), F32),
        grid=(n // D,),
        in_specs=[pl.BlockSpec((rows, D), lambda i: (0, 0)),
                  pl.BlockSpec((D, D), lambda i: (0, i)),
                  pl.BlockSpec((1, D), lambda i: (0, i))],
        out_specs=pl.BlockSpec((rows, D), lambda i: (0, i)),
        compiler_params=_cparams(("arbitrary",)),
        name="adaln",
    )(cvec, w, b)


def _inproj_kernel(x_ref, g_ref, mod_ref, w_ref, prw_ref, pgl_ref):
    mod = mod_ref[0, 0]
    h = _rms(x_ref[0], g_ref[...]) * (1.0 + mod[1:2]) + mod[0:1]
    hb = h.astype(BF16)
    prw_ref[0] = jnp.dot(hb, w_ref[:, :RW_COLS], preferred_element_type=F32)
    pgl_ref[0] = jnp.dot(hb, w_ref[:, RW_COLS:], preferred_element_type=F32)


def _const_spec(shape):
    nd = len(shape)
    return pl.BlockSpec(shape, lambda *_: (0,) * nd)


def _tile_spec(c):
    return pl.BlockSpec((1, TB, c), lambda b, j: (b, j, 0))


def _dir_tile_spec(c):
    return pl.BlockSpec((2, 1, TB, c), lambda b, j: (0, b, j, 0))


def _mod_spec():
    return pl.BlockSpec((1, 1, 6, D), lambda b, j: (b, jnp.minimum(j, 1), 0, 0))


def _inproj(xc, g, modtab, w):
    B, N, _ = xc.shape
    return pl.pallas_call(
        _inproj_kernel,
        out_shape=[jax.ShapeDtypeStruct((B, N, RW_COLS), F32),
                   jax.ShapeDtypeStruct((B, N, GLA_PAD), F32)],
        grid=(B, N // TB),
        in_specs=[_tile_spec(D), _const_spec((1, D)), _mod_spec(),
                  _const_spec((D, RW_COLS + GLA_PAD))],
        out_specs=[_tile_spec(RW_COLS), _tile_spec(GLA_PAD)],
        compiler_params=_cparams(("parallel", "parallel")),
        name="inproj",
    )(xc, g, modtab, w)


def _rwprep_kernel(has_vres, n_tiles, p_ref, hu_ref, hd_ref, mu_ref, wup_ref, w0_ref, aup_ref, a0_ref,
                   kk_ref, ka_ref, rk_ref, gup_ref, bd_ref, *rest):
    if has_vres:
        vf_ref, vdn_ref, vup_ref, v0_ref = rest[:4]
        rest = rest[4:]
    r_out, kk_out, v_out, g_out, bon_out, lw_out, b_out, ke_out = rest
    j = pl.program_id(1)
    p = p_ref[0]
    row = lax.broadcasted_iota(jnp.int32, p.shape, 0)
    lane = lax.broadcasted_iota(jnp.int32, p.shape, 1)
    prev1 = pltpu.roll(p, 1, 0)
    next1 = pltpu.roll(p, TB - 1, 0)
    up = jnp.concatenate([hu_ref[0], p[:TB - GRID_W]], axis=0)
    down = jnp.concatenate([p[GRID_W:], hd_ref[0]], axis=0)
    col = row & (GRID_W - 1)
    c4 = lane & 3
    left = jnp.where(col == 0, 0.0, prev1)
    right = jnp.where(col == GRID_W - 1, 0.0, next1)
    upper = jnp.where(jnp.logical_and(j == 1, row < GRID_W), 0.0, up)
    lower = jnp.where(jnp.logical_and(j == n_tiles - 1, row >= TB - GRID_W), 0.0, down)
    sx = jnp.where(c4 == 0, left, jnp.where(c4 == 1, right, jnp.where(c4 == 2, upper, lower)))
    sc = jnp.where((lane & 1) == 0, jnp.where(row == 0, 0.0, prev1), jnp.where(row == TB - 1, 0.0, next1))
    shifted = jnp.where(j == 0, sc, sx)
    u = p + mu_ref[...] * (shifted - p)

    r = u[:, 0:RW_WIDTH]
    k = u[:, RW_WIDTH:2 * RW_WIDTH]
    v = u[:, 2 * RW_WIDTH:3 * RW_WIDTH]
    o = 3 * RW_WIDTH
    wd = u[:, o:o + 2 * RW_RANK]
    ad = u[:, o + 2 * RW_RANK:o + 4 * RW_RANK]
    gd = u[:, o + 4 * RW_RANK:]

    w_logit = w0_ref[...] + _bdot(jnp.tanh(wd), wup_ref[...])
    lw = -math.exp(-0.5) * _sigmoid(w_logit)
    a = _sigmoid(a0_ref[...] + _bdot(ad, aup_ref[...]))
    bd = bd_ref[...]
    kk = k * kk_ref[...]
    kk = kk / jnp.maximum(jnp.sqrt(_seg_sum(kk * kk, bd)), 1e-12)
    g = _bdot(_sigmoid(gd), gup_ref[...])
    if has_vres:
        gate = _sigmoid(v0_ref[...] + _bdot(_bdot(v, vdn_ref[...]), vup_ref[...]))
        vm = v + (vf_ref[0] - v) * gate
    else:
        vm = v
    ke_sum = jnp.zeros_like(k)
    for d in range(2):
        a_d = a[:, d * RW_WIDTH:(d + 1) * RW_WIDTH]
        ke_d = k * (1.0 + (a_d - 1.0) * ka_ref[...])
        lw_out[d, 0] = lw[:, d * RW_WIDTH:(d + 1) * RW_WIDTH]
        b_out[d, 0] = kk * a_d
        ke_out[d, 0] = ke_d
        ke_sum = ke_sum + ke_d
    r_out[0] = r
    kk_out[0] = kk
    v_out[0] = vm
    g_out[0] = g
    bon_out[0] = _seg_sum(r * ke_sum * rk_ref[...], bd) * vm


def _rwprep(p_rw, prm, v_first):
    B, N, _ = p_rw.shape
    nt = N // TB
    hb = TB // GRID_W
    nhb = N // GRID_W
    has_vres = v_first is not None
    W2 = 2 * RW_WIDTH
    in_specs = [
        _tile_spec(RW_COLS),
        pl.BlockSpec((1, GRID_W, RW_COLS), lambda b, j: (b, jnp.maximum(j * hb - 1, 0), 0)),
        pl.BlockSpec((1, GRID_W, RW_COLS), lambda b, j: (b, jnp.minimum(j * hb + hb, nhb - 1), 0)),
        _const_spec((1, RW_COLS)), _const_spec((2 * RW_RANK, W2)), _const_spec((1, W2)),
        _const_spec((2 * RW_RANK, W2)), _const_spec((1, W2)),
        _const_spec((1, RW_WIDTH)), _const_spec((1, RW_WIDTH)), _const_spec((1, RW_WIDTH)),
        _const_spec((RW_G_RANK, RW_WIDTH)), _const_spec((RW_WIDTH, RW_WIDTH)),
    ]
    args = [p_rw, p_rw, p_rw, prm["mu"], prm["w_up"], prm["w0"], prm["a_up"], prm["a0"],
            prm["k_k"], prm["k_a"], prm["r_k"], prm["g_up"], prm["bd64"]]
    if has_vres:
        in_specs += [_tile_spec(RW_WIDTH), _const_spec((RW_WIDTH, LANES)), _const_spec((LANES, RW_WIDTH)),
                     _const_spec((1, RW_WIDTH))]
        args += [v_first, prm["v_down"], prm["v_up"], prm["v0"]]
    tok = jax.ShapeDtypeStruct((B, N, RW_WIDTH), F32)
    dtok = jax.ShapeDtypeStruct((2, B, N, RW_WIDTH), F32)
    return pl.pallas_call(
        functools.partial(_rwprep_kernel, has_vres, nt),
        out_shape=[tok] * 5 + [dtok] * 3,
        grid=(B, nt),
        in_specs=in_specs,
        out_specs=[_tile_spec(RW_WIDTH)] * 5 + [_dir_tile_spec(RW_WIDTH)] * 3,
        compiler_params=_cparams(("parallel", "parallel")),
        name="rwprep",
    )(*args)


def _scan_chunk(d, j, n_ctx_chunks, n_chunks):
    back = jnp.where(j < n_ctx_chunks, n_ctx_chunks - 1 - j, n_chunks + n_ctx_chunks - 1 - j)
    return jnp.where(d == 0, j, back)


def _rwscan_kernel(r_ref, kk_ref, v_ref, lw_ref, b_ref, ke_ref, y_ref, st_ref):
    d = pl.program_id(0)
    j = pl.program_id(2)

    @pl.when(j == 0)
    def _():
        st_ref[...] = jnp.zeros_like(st_ref)

    sgn = 1 - 2 * d
    t64 = lax.broadcasted_iota(jnp.int32, (CH, CH), 0)
    s64 = lax.broadcasted_iota(jnp.int32, (CH, CH), 1)
    tri = jnp.where((t64 - s64) * sgn >= 0, 1.0, 0.0).astype(BF16)
    lw = lw_ref[0, 0]
    cum = _split_dot(tri, lw)
    tot = jnp.sum(lw, axis=0, keepdims=True)
    w_in = jnp.exp(cum)
    w_inv = jnp.exp(-cum)
    w_prev = jnp.exp(cum - lw)
    w_end = jnp.exp(tot - cum)
    w_tot = jnp.exp(tot)
    r, kk, v, b, ke = r_ref[0], kk_ref[0], v_ref[0], b_ref[0, 0], ke_ref[0, 0]
    rh = r * w_in
    ah = -(kk * w_prev)
    bh = b * w_inv
    kh = ke * w_inv
    bt = b * w_end
    kt = ke * w_end

    ti = lax.broadcasted_iota(jnp.int32, (CH, LANES), 0)
    li = lax.broadcasted_iota(jnp.int32, (CH, LANES), 1)
    si = li & (CH - 1)
    dd = (ti - si) * sgn
    m_incl = dd >= 0
    m_strict = dd > 0
    lo = li < CH
    eye = jnp.where(ti == si, 1.0, 0.0)
    rb = lax.broadcasted_iota(jnp.int32, (LANES, LANES), 0) // CH
    cb = lax.broadcasted_iota(jnp.int32, (LANES, LANES), 1) // CH
    bdmask = rb == cb

    for p in range(RW_WIDTH // LANES):
        sl = slice(LANES * p, LANES * (p + 1))
        lhs = jnp.concatenate([ah[:, sl], rh[:, sl]], axis=0).astype(BF16)
        rhs = jnp.concatenate([_pair_stack(bh[:, sl], lo), _pair_stack(kh[:, sl], lo)], axis=0)
        a_all = _bdot_nt(lhs, rhs)
        a_ab = jnp.where(m_strict, a_all[:CH, :LANES], 0.0)
        a_ak = jnp.where(m_strict, a_all[:CH, LANES:], 0.0)
        a_rb = jnp.where(m_incl, a_all[CH:, :LANES], 0.0)
        a_rk = jnp.where(m_incl, a_all[CH:, LANES:], 0.0)
        t_inv = eye + a_ab
        m = _bdot(a_ab, _pair_stack(a_ab, lo))
        for _ in range(int(math.log2(CH)) - 2):
            z = _bdot(jnp.concatenate([m, t_inv], axis=0), _pair_stack(m, lo))
            m = z[:CH]
            t_inv = t_inv + z[CH:]
        t_inv = t_inv + _bdot(t_inv, _pair_stack(m, lo))
        st = st_ref[p]
        z1 = _bdot_nt(lhs, st)
        v_p = v[:, sl]
        v_bd = _pair_stack(v_p, lo).astype(BF16)
        x = z1[:CH] + _bdot(a_ak, v_bd)
        u = _bdot(t_inv, _pair_stack(x, lo))
        y = z1[CH:] + _bdot(jnp.concatenate([a_rb, a_rk], axis=1),
                            jnp.concatenate([_pair_stack(u, lo).astype(BF16), v_bd], axis=0))
        upd = _bdot_tn(jnp.concatenate([u, v_p], axis=0), jnp.concatenate([bt[:, sl], kt[:, sl]], axis=0))
        st_ref[p] = st * w_tot[:, sl] + jnp.where(bdmask, upd, 0.0)
        y_ref[0, 0, :, sl] = y


def _scan_specs(c, n_ctx_chunks, n_chunks):
    def tok_map(d, b, j):
        return (b, _scan_chunk(d, j, n_ctx_chunks, n_chunks), 0)

    def dir_map(d, b, j):
        return (d, b, _scan_chunk(d, j, n_ctx_chunks, n_chunks), 0)

    return pl.BlockSpec((1, CH, c), tok_map), pl.BlockSpec((1, 1, CH, c), dir_map)


def _rwscan(r, kk, v, lw, b, ke, n_ctx):
    B, N, _ = r.shape
    nch = N // CH
    tok, dtok = _scan_specs(RW_WIDTH, n_ctx // CH, nch)
    return pl.pallas_call(
        _rwscan_kernel,
        out_shape=jax.ShapeDtypeStruct((2, B, N, RW_WIDTH), F32),
        grid=(2, B, nch),
        in_specs=[tok, tok, tok, dtok, dtok, dtok],
        out_specs=dtok,
        scratch_shapes=[pltpu.VMEM((RW_WIDTH // LANES, LANES, LANES), F32)],
        compiler_params=_cparams(("parallel", "parallel", "arbitrary")),
        name="rwscan",
    )(r, kk, v, lw, b, ke)


def _glaprep_kernel(n_tiles, p_ref, hp_ref, hn_ref, cw_ref, aup_ref, ab_ref, q_out, k_out, v_out, og_out, lg_out):
    j = pl.program_id(1)
    u = p_ref[0][:, :GLA_QKV]
    row = lax.broadcasted_iota(jnp.int32, u.shape, 0)
    first = jnp.where(j <= 1, 0.0, 1.0) * hp_ref[0, 7:8, :GLA_QKV]
    last = jnp.where(jnp.logical_or(j == 0, j == n_tiles - 1), 0.0, 1.0) * hn_ref[0, 0:1, :GLA_QKV]
    prev1 = jnp.where(row == 0, first, pltpu.roll(u, 1, 0))
    next1 = jnp.where(row == TB - 1, last, pltpu.roll(u, TB - 1, 0))
    cw = cw_ref[...]
    conv = cw[0:1] * prev1 + cw[1:2] * u + cw[2:3] * next1
    qkv = conv * _sigmoid(conv)
    q_out[0] = qkv[:, :GLA_KW] * (GLA_DK ** -0.5)
    k_out[0] = qkv[:, GLA_KW:2 * GLA_KW]
    v_out[0] = qkv[:, 2 * GLA_KW:]
    og_out[0] = p_ref[0][:, GLA_QKV:GLA_QKV + GLA_WIDTH]
    z = _bdot(p_ref[0][:, GLA_QKV + GLA_WIDTH:], aup_ref[...]) + ab_ref[...]
    lg = (jnp.minimum(z, 0.0) - jnp.log1p(jnp.exp(-jnp.abs(z)))) * (1.0 / GLA_GATE_NORM)
    for d in range(2):
        lg_out[d, 0] = lg[:, d * GLA_KW:(d + 1) * GLA_KW]


def _glaprep(p_gl, prm):
    B, N, _ = p_gl.shape
    nt = N // TB
    sub = 8
    hb = TB // sub
    nhb = N // sub
    return pl.pallas_call(
        functools.partial(_glaprep_kernel, nt),
        out_shape=[jax.ShapeDtypeStruct((B, N, GLA_KW), F32), jax.ShapeDtypeStruct((B, N, GLA_KW), F32),
                   jax.ShapeDtypeStruct((B, N, GLA_WIDTH), F32), jax.ShapeDtypeStruct((B, N, GLA_WIDTH), F32),
                   jax.ShapeDtypeStruct((2, B, N, GLA_KW), F32)],
        grid=(B, nt),
        in_specs=[_tile_spec(GLA_PAD),
                  pl.BlockSpec((1, sub, GLA_PAD), lambda b, j: (b, jnp.maximum(j * hb - 1, 0), 0)),
                  pl.BlockSpec((1, sub, GLA_PAD), lambda b, j: (b, jnp.minimum(j * hb + hb, nhb - 1), 0)),
                  _const_spec((3, GLA_QKV)), _const_spec((LANES, 2 * GLA_KW)), _const_spec((1, 2 * GLA_KW))],
        out_specs=[_tile_spec(GLA_KW), _tile_spec(GLA_KW), _tile_spec(GLA_WIDTH), _tile_spec(GLA_WIDTH),
                   _dir_tile_spec(GLA_KW)],
        compiler_params=_cparams(("parallel", "parallel")),
        name="glaprep",
    )(p_gl, p_gl, p_gl, prm["conv"], prm["a_up"], prm["a_b"])


def _glascan_kernel(q_ref, k_ref, v_ref, lg_ref, o_ref, st_ref):
    d = pl.program_id(0)
    j = pl.program_id(2)

    @pl.when(j == 0)
    def _():
        st_ref[...] = jnp.zeros_like(st_ref)

    sgn = 1 - 2 * d
    t64 = lax.broadcasted_iota(jnp.int32, (CH, CH), 0)
    s64 = lax.broadcasted_iota(jnp.int32, (CH, CH), 1)
    tri = jnp.where((t64 - s64) * sgn >= 0, 1.0, 0.0).astype(BF16)
    lg = lg_ref[0, 0]
    cum = _split_dot(tri, lg)
    tot = jnp.sum(lg, axis=0, keepdims=True)
    q, k, v = q_ref[0], k_ref[0], v_ref[0]
    qd = q * jnp.exp(cum)
    ki = k * jnp.exp(-cum)
    kend = k * jnp.exp(tot - cum)
    dec = jnp.exp(tot)

    ti = lax.broadcasted_iota(jnp.int32, (CH, LANES), 0)
    li = lax.broadcasted_iota(jnp.int32, (CH, LANES), 1)
    m_incl = (ti - (li & (CH - 1))) * sgn >= 0
    lo = li < CH
    lo2 = lax.broadcasted_iota(jnp.int32, (LANES, LANES), 1) < CH
    zeros_v = jnp.zeros((CH, GLA_DV), F32)

    for p in range(GLA_KW // LANES):
        sl = slice(LANES * p, LANES * (p + 1))
        v0 = v[:, 2 * LANES * p:2 * LANES * p + LANES]
        v1 = v[:, 2 * LANES * p + LANES:2 * LANES * (p + 1)]
        qd_p = qd[:, sl].astype(BF16)
        att = jnp.where(m_incl, _bdot_nt(qd_p, _pair_stack(ki[:, sl], lo)), 0.0)
        v_bd = jnp.concatenate([jnp.concatenate([v0, zeros_v], axis=1),
                                jnp.concatenate([zeros_v, v1], axis=1)], axis=0)
        st = st_ref[p]
        st_bd = jnp.concatenate([jnp.where(lo2, st, 0.0), jnp.where(lo2, 0.0, st)], axis=0)
        o = _bdot(att, v_bd) + _bdot_nt(qd_p, st_bd)
        upd = _bdot_tn(jnp.concatenate([v0, v1], axis=0), _pair_stack(kend[:, sl], lo))
        st_ref[p] = st * dec[:, sl] + upd
        o_ref[0, 0, :, 2 * LANES * p:2 * LANES * (p + 1)] = o


def _glascan(q, k, v, lg, n_ctx):
    B, N, _ = q.shape
    nch = N // CH
    tok_k, dtok_k = _scan_specs(GLA_KW, n_ctx // CH, nch)
    tok_v, dtok_v = _scan_specs(GLA_WIDTH, n_ctx // CH, nch)
    return pl.pallas_call(
        _glascan_kernel,
        out_shape=jax.ShapeDtypeStruct((2, B, N, GLA_WIDTH), F32),
        grid=(2, B, nch),
        in_specs=[tok_k, tok_k, tok_v, dtok_k],
        out_specs=dtok_v,
        scratch_shapes=[pltpu.VMEM((GLA_KW // LANES, GLA_DV, LANES), F32)],
        compiler_params=_cparams(("parallel", "parallel", "arbitrary")),
        name="glascan",
    )(q, k, v, lg)


def _readout_kernel(y_ref, bon_ref, g_ref, o_ref, og_ref, x_ref, mod_ref, gnw_ref, gnb_ref, ggn_ref,
                    wout_ref, gpost_ref, bd_ref, xo_ref):
    bd = bd_ref[...]
    y = y_ref[0, 0] + y_ref[1, 0]
    mu = _seg_sum(y, bd) * (1.0 / RW_HEAD)
    yc = y - mu
    var = _seg_sum(yc * yc, bd) * (1.0 / RW_HEAD)
    yn = yc * lax.rsqrt(var + RW_GN_EPS) * gnw_ref[...] + gnb_ref[...]
    rw = (yn + bon_ref[0]) * g_ref[0]
    o = o_ref[0, 0] + o_ref[1, 0]
    og = og_ref[0]
    parts = [rw.astype(BF16)]
    for h in range(GLA_HEADS):
        sl = slice(GLA_DV * h, GLA_DV * (h + 1))
        oh = o[:, sl]
        on = oh * lax.rsqrt(jnp.mean(oh * oh, axis=-1, keepdims=True) + GLA_NORM_EPS)
        ogh = og[:, sl]
        parts.append((on * ggn_ref[:, sl] * (ogh * _sigmoid(ogh))).astype(BF16))
    cat = jnp.concatenate(parts, axis=1)
    mx = jnp.dot(cat, wout_ref[...], preferred_element_type=F32)
    xo_ref[0] = x_ref[0] + mod_ref[0, 0][2:3] * _rms(mx, gpost_ref[...])


def _readout(y, bonus, g, o, og, xc, modtab, prm):
    B, N, _ = xc.shape
    return pl.pallas_call(
        _readout_kernel,
        out_shape=jax.ShapeDtypeStruct((B, N, D), F32),
        grid=(B, N // TB),
        in_specs=[_dir_tile_spec(RW_WIDTH), _tile_spec(RW_WIDTH), _tile_spec(RW_WIDTH),
                  _dir_tile_spec(GLA_WIDTH), _tile_spec(GLA_WIDTH), _tile_spec(D), _mod_spec(),
                  _const_spec((1, RW_WIDTH)), _const_spec((1, RW_WIDTH)), _const_spec((1, GLA_WIDTH)),
                  _const_spec((D, D)), _const_spec((1, D)), _const_spec((RW_WIDTH, RW_WIDTH))],
        out_specs=_tile_spec(D),
        compiler_params=_cparams(("parallel", "parallel")),
        name="readout",
    )(y, bonus, g, o, og, xc, modtab, prm["gn_w"], prm["gn_b"], prm["gla_gn_w"], prm["w_out"],
      prm["norm_post"], prm["bd64"])


def _swiglu_acc(hb, wg_ref, wu_ref, wd_ref):
    acc = jnp.zeros((hb.shape[0], D), F32)
    for c in range(D_FF // FCH):
        sl = slice(c * FCH, (c + 1) * FCH)
        gate = jnp.dot(hb, wg_ref[:, sl], preferred_element_type=F32)
        up = jnp.dot(hb, wu_ref[:, sl], preferred_element_type=F32)
        act = (gate * _sigmoid(gate) * up).astype(BF16)
        acc = acc + jnp.dot(act, wd_ref[sl, :], preferred_element_type=F32)
    return acc


def _ffn_kernel(x_ref, mod_ref, gpre_ref, gpost_ref, wg_ref, wu_ref, wd_ref, xo_ref):
    mod = mod_ref[0, 0]
    x = x_ref[0]
    hb = (_rms(x, gpre_ref[...]) * (1.0 + mod[4:5]) + mod[3:4]).astype(BF16)
    fx = _swiglu_acc(hb, wg_ref, wu_ref, wd_ref)
    xo_ref[0] = x + mod[5:6] * _rms(fx, gpost_ref[...])


def _single_buffered(shape):
    nd = len(shape)
    return pl.BlockSpec(shape, lambda *_: (0,) * nd, pipeline_mode=pl.Buffered(1))


def _ffn(xc, modtab, g_pre, g_post, wg, wu, wd):
    B, N, _ = xc.shape
    return pl.pallas_call(
        _ffn_kernel,
        out_shape=jax.ShapeDtypeStruct((B, N, D), F32),
        grid=(B, N // TB),
        in_specs=[_tile_spec(D), _mod_spec(), _const_spec((1, D)), _const_spec((1, D)),
                  _single_buffered((D, D_FF)), _single_buffered((D, D_FF)), _single_buffered((D_FF, D))],
        out_specs=_tile_spec(D),
        compiler_params=_cparams(("parallel", "parallel")),
        name="ffn",
    )(xc, modtab, g_pre, g_post, wg, wu, wd)


MOE_TM = 512


def _moe_kernel(x_ref, mod_ref, gpre_ref, gpost_ref, router_ref, wg_ref, wu_ref, wd_ref, xo_ref,
                h_ref, comb_ref, acc_ref):
    e = pl.program_id(2)
    mod = mod_ref[0, 0]
    lane = lax.broadcasted_iota(jnp.int32, (MOE_TM, LANES), 1)

    @pl.when(e == 0)
    def _():
        h = _rms(x_ref[0], gpre_ref[...]) * (1.0 + mod[4:5]) + mod[3:4]
        h_ref[...] = h.astype(BF16)
        logits = jnp.dot(h, router_ref[...], precision=lax.Precision.HIGHEST, preferred_element_type=F32)
        logits = jnp.where(lane < N_EXPERTS, logits, -jnp.inf)
        v1 = jnp.max(logits, axis=-1, keepdims=True)
        i1 = jnp.min(jnp.where(logits == v1, lane, LANES), axis=-1, keepdims=True)
        rest = jnp.where(lane == i1, -jnp.inf, logits)
        v2 = jnp.max(rest, axis=-1, keepdims=True)
        i2 = jnp.min(jnp.where(rest == v2, lane, LANES), axis=-1, keepdims=True)
        e2 = jnp.exp(v2 - v1)
        w1 = 1.0 / (1.0 + e2)
        comb_ref[...] = jnp.where(lane == i1, w1, 0.0) + jnp.where(lane == i2, e2 * w1, 0.0)
        acc_ref[...] = jnp.zeros_like(acc_ref)

    ce = jnp.sum(jnp.where(lane == e, comb_ref[...], 0.0), axis=-1, keepdims=True)
    acc_ref[...] += ce * _swiglu_acc(h_ref[...], wg_ref.at[0], wu_ref.at[0], wd_ref.at[0])

    @pl.when(e == N_EXPERTS - 1)
    def _():
        xo_ref[0] = x_ref[0] + mod[5:6] * _rms(acc_ref[...], gpost_ref[...])


def _moe(xs, modx, g_pre, g_post, router, wg, wu, wd):
    B, S, _ = xs.shape
    xt = pl.BlockSpec((1, MOE_TM, D), lambda b, t, e: (b, t, 0))
    return pl.pallas_call(
        _moe_kernel,
        out_shape=jax.ShapeDtypeStruct((B, S, D), F32),
        grid=(B, S // MOE_TM, N_EXPERTS),
        in_specs=[xt, pl.BlockSpec((1, 1, 6, D), lambda b, t, e: (b, 1, 0, 0)),
                  _const_spec((1, D)), _const_spec((1, D)), _const_spec((D, LANES)),
                  pl.BlockSpec((1, D, D_FF), lambda b, t, e: (e, 0, 0)),
                  pl.BlockSpec((1, D, D_FF), lambda b, t, e: (e, 0, 0)),
                  pl.BlockSpec((1, D_FF, D), lambda b, t, e: (e, 0, 0))],
        out_specs=xt,
        scratch_shapes=[pltpu.VMEM((MOE_TM, D), BF16), pltpu.VMEM((MOE_TM, LANES), F32),
                        pltpu.VMEM((MOE_TM, D), F32)],
        compiler_params=_cparams(("parallel", "parallel", "arbitrary")),
        name="moe",
    )(xs, modx, g_pre, g_post, router, wg, wu, wd)


def _block_diag2(w):
    z = jnp.zeros_like(w[0])
    return jnp.concatenate([jnp.concatenate([w[0], z], axis=1), jnp.concatenate([z, w[1]], axis=1)], axis=0)


def _row(v):
    return v.reshape(1, -1).astype(F32)


def _head_ones(width, head):
    i = jnp.arange(width) // head
    return (i[:, None] == i[None, :]).astype(BF16)


def kernel(x, c, ctx, c_ctx, ada_w, ada_b, norm_mix_pre, norm_mix_post, norm_ffn_pre, norm_ffn_post, w_in, shift_mu, rw_w_up, rw_w0, rw_a_up, rw_a0, rw_k_k, rw_k_a, rw_r_k, rw_g_up, rw_gn_w, rw_gn_b, rw_v_down, rw_v_up, rw_v0, gla_conv, gla_a_up, gla_a_b, gla_gn_w, w_out, ffn_w_gate, ffn_w_up, ffn_w_down, moe_router, moe_w_gate, moe_w_up, moe_w_down):
    B, S, _ = x.shape
    n_ctx = ctx.shape[1]
    depth = w_in.shape[0]
    assert n_ctx == TB and S % TB == 0 and S % MOE_TM == 0 and depth == 2

    xc = jnp.concatenate([ctx, x], axis=1)
    cvec = jnp.concatenate([c, c_ctx[None, :], jnp.zeros((16 - B - 1, D), F32)], axis=0)
    bd64 = _head_ones(RW_WIDTH, RW_HEAD)
    v_first = None
    out = None
    for i in range(depth):
        last = i == depth - 1
        mods = _adaln(cvec, ada_w[i], _row(ada_b[i]))
        mod_x = mods[:B].reshape(B, 6, D)
        mod_c = jnp.broadcast_to(mods[B].reshape(1, 6, D), (B, 6, D))
        modtab = jnp.stack([mod_c, mod_x], axis=1)

        w_i = jnp.concatenate([w_in[i], jnp.zeros((D, GLA_PAD - GLA_COLS), F32)], axis=1).astype(BF16)
        p_rw, p_gl = _inproj(xc, _row(norm_mix_pre[i]), modtab, w_i)

        prm = dict(
            mu=_row(shift_mu[i]),
            w_up=_block_diag2(rw_w_up[i]).astype(BF16), w0=_row(rw_w0[i]),
            a_up=_block_diag2(rw_a_up[i]).astype(BF16), a0=_row(rw_a0[i]),
            k_k=_row(rw_k_k[i]), k_a=_row(rw_k_a[i]), r_k=_row(rw_r_k[i]),
            g_up=rw_g_up[i].astype(BF16), bd64=bd64,
            gn_w=_row(rw_gn_w[i]), gn_b=_row(rw_gn_b[i]), gla_gn_w=_row(gla_gn_w[i]),
            w_out=w_out[i].astype(BF16), norm_post=_row(norm_mix_post[i]),
        )
        gate_pad = jnp.zeros((LANES - 2 * GLA_GATE_RANK, 2 * GLA_KW), F32)
        gla_prm = dict(conv=gla_conv[i].astype(F32), a_b=_row(gla_a_b[i]),
                       a_up=jnp.concatenate([_block_diag2(gla_a_up[i]), gate_pad], axis=0).astype(BF16))
        if i > 0:
            pad = LANES - RW_V_RANK
            prm["v_down"] = jnp.concatenate([rw_v_down[i - 1], jnp.zeros((RW_WIDTH, pad), F32)], axis=1).astype(BF16)
            prm["v_up"] = jnp.concatenate([rw_v_up[i - 1], jnp.zeros((pad, RW_WIDTH), F32)], axis=0).astype(BF16)
            prm["v0"] = _row(rw_v0[i - 1])

        r, kk, vm, g, bonus, lw, bb, ke = _rwprep(p_rw, prm, v_first if i > 0 else None)
        if i == 0:
            v_first = vm
        y = _rwscan(r, kk, vm, lw, bb, ke, n_ctx)
        q, k, gv, og, lg = _glaprep(p_gl, gla_prm)
        o = _glascan(q, k, gv, lg, n_ctx)
        xc = _readout(y, bonus, g, o, og, xc, modtab, prm)

        jf = i // 2
        if i % 2 == 0:
            xc = _ffn(xc, modtab, _row(norm_ffn_pre[i]), _row(norm_ffn_post[i]),
                      ffn_w_gate[jf].astype(BF16), ffn_w_up[jf].astype(BF16), ffn_w_down[jf].astype(BF16))
        else:
            router = jnp.concatenate([moe_router[jf], jnp.zeros((D, LANES - N_EXPERTS), F32)], axis=1)
            out = _moe(xc[:, n_ctx:], modtab, _row(norm_ffn_pre[i]), _row(norm_ffn_post[i]), router,
                       moe_w_gate[jf].astype(BF16), moe_w_up[jf].astype(BF16), moe_w_down[jf].astype(BF16))
    return out
```

```python
import functools
import math

import jax
import jax.numpy as jnp
from jax import lax
from jax.experimental import pallas as pl
from jax.experimental.pallas import tpu as pltpu

F32, BF16 = jnp.float32, jnp.bfloat16

D = 1024
GRID_W = 64
RW_WIDTH = 512
RW_HEAD = 64
RW_RANK = 64
RW_G_RANK = 128
RW_V_RANK = 32
RW_GN_EPS = 64e-5
GLA_WIDTH = 512
GLA_HEADS = 4
GLA_DV = 128
GLA_DK = 64
GLA_KW = 256
GLA_GATE_RANK = 16
GLA_GATE_NORM = 16.0
GLA_NORM_EPS = 1e-5
D_FF = 2816
N_EXPERTS = 8
NORM_EPS = 1e-6
RW_COLS = 3 * RW_WIDTH + 4 * RW_RANK + RW_G_RANK
GLA_QKV = 2 * GLA_KW + GLA_WIDTH
GLA_COLS = GLA_QKV + GLA_WIDTH + 2 * GLA_GATE_RANK
GLA_PAD = 1664

LANES = 128
TB = 256
CH = 64
FCH = 256
VMEM_LIMIT = 56 * 1024 * 1024


def _cparams(sem):
    return pltpu.CompilerParams(dimension_semantics=sem, vmem_limit_bytes=VMEM_LIMIT)


def _bdot(a, b):
    return jnp.dot(a.astype(BF16), b.astype(BF16), preferred_element_type=F32)


def _bdot_nt(a, b):
    return lax.dot_general(a.astype(BF16), b.astype(BF16), (((1,), (1,)), ((), ())),
                           preferred_element_type=F32)


def _bdot_tn(a, b):
    return lax.dot_general(a.astype(BF16), b.astype(BF16), (((0,), (0,)), ((), ())),
                           preferred_element_type=F32)


def _split_dot(a_exact, x):
    h1 = x.astype(BF16)
    r1 = x - h1.astype(F32)
    h2 = r1.astype(BF16)
    h3 = (r1 - h2.astype(F32)).astype(BF16)
    return (jnp.dot(a_exact, h1, preferred_element_type=F32)
            + jnp.dot(a_exact, h2, preferred_element_type=F32)
            + jnp.dot(a_exact, h3, preferred_element_type=F32))


def _seg_sum(x, bd):
    hi = x.astype(BF16)
    lo = (x - hi.astype(F32)).astype(BF16)
    return jnp.dot(hi, bd, preferred_element_type=F32) + jnp.dot(lo, bd, preferred_element_type=F32)


def _sigmoid(x):
    return jax.nn.sigmoid(x)


def _rms(x, g):
    return x * lax.rsqrt(jnp.mean(x * x, axis=-1, keepdims=True) + NORM_EPS) * g


def _pair_stack(z, lo):
    return jnp.concatenate([jnp.where(lo, z, 0.0), jnp.where(lo, 0.0, z)], axis=0)


def _adaln_kernel(c_ref, w_ref, b_ref, o_ref):
    c = c_ref[...]
    s = c * _sigmoid(c)
    o_ref[...] = jnp.dot(s, w_ref[...], precision=lax.Precision.HIGHEST,
                         preferred_element_type=F32) + b_ref[...]


def _adaln(cvec, w, b):
    rows = cvec.shape[0]
    n = w.shape[1]
    return pl.pallas_call(
        _adaln_kernel,
        out_shape=jax.ShapeDtypeStruct((rows, n), F32),
        grid=(n // D,),
        in_specs=[pl.BlockSpec((rows, D), lambda i: (0, 0)),
                  pl.BlockSpec((D, D), lambda i: (0, i)),
                  pl.BlockSpec((1, D), lambda i: (0, i))],
        out_specs=pl.BlockSpec((rows, D), lambda i: (0, i)),
        compiler_params=_cparams(("arbitrary",)),
        name="adaln",
    )(cvec, w, b)


def _inproj_kernel(x_ref, g_ref, mod_ref, w_ref, prw_ref, pgl_ref):
    mod = mod_ref[0, 0]
    h = _rms(x_ref[0], g_ref[...]) * (1.0 + mod[1:2]) + mod[0:1]
    hb = h.astype(BF16)
    prw_ref[0] = jnp.dot(hb, w_ref[:, :RW_COLS], preferred_element_type=F32)
    pgl_ref[0] = jnp.dot(hb, w_ref[:, RW_COLS:], preferred_element_type=F32)


def _const_spec(shape):
    nd = len(shape)
    return pl.BlockSpec(shape, lambda *_: (0,) * nd)


def _tile_spec(c):
    return pl.BlockSpec((1, TB, c), lambda b, j: (b, j, 0))


def _dir_tile_spec(c):
    return pl.BlockSpec((2, 1, TB, c), lambda b, j: (0, b, j, 0))


def _mod_spec():
    return pl.BlockSpec((1, 1, 6, D), lambda b, j: (b, jnp.minimum(j, 1), 0, 0))


def _inproj(xc, g, modtab, w):
    B, N, _ = xc.shape
    return pl.pallas_call(
        _inproj_kernel,
        out_shape=[jax.ShapeDtypeStruct((B, N, RW_COLS), F32),
                   jax.ShapeDtypeStruct((B, N, GLA_PAD), F32)],
        grid=(B, N // TB),
        in_specs=[_tile_spec(D), _const_spec((1, D)), _mod_spec(),
                  _const_spec((D, RW_COLS + GLA_PAD))],
        out_specs=[_tile_spec(RW_COLS), _tile_spec(GLA_PAD)],
        compiler_params=_cparams(("parallel", "parallel")),
        name="inproj",
    )(xc, g, modtab, w)


def _rwprep_kernel(has_vres, n_tiles, p_ref, hu_ref, hd_ref, mu_ref, wup_ref, w0_ref, aup_ref, a0_ref,
                   kk_ref, ka_ref, rk_ref, gup_ref, bd_ref, *rest):
    if has_vres:
        vf_ref, vdn_ref, vup_ref, v0_ref = rest[:4]
        rest = rest[4:]
    r_out, kk_out, v_out, g_out, bon_out, lw_out, b_out, ke_out = rest
    j = pl.program_id(1)
    p = p_ref[0]
    row = lax.broadcasted_iota(jnp.int32, p.shape, 0)
    lane = lax.broadcasted_iota(jnp.int32, p.shape, 1)
    prev1 = pltpu.roll(p, 1, 0)
    next1 = pltpu.roll(p, TB - 1, 0)
    up = jnp.concatenate([hu_ref[0], p[:TB - GRID_W]], axis=0)
    down = jnp.concatenate([p[GRID_W:], hd_ref[0]], axis=0)
    col = row & (GRID_W - 1)
    c4 = lane & 3
    left = jnp.where(col == 0, 0.0, prev1)
    right = jnp.where(col == GRID_W - 1, 0.0, next1)
    upper = jnp.where(jnp.logical_and(j == 1, row < GRID_W), 0.0, up)
    lower = jnp.where(jnp.logical_and(j == n_tiles - 1, row >= TB - GRID_W), 0.0, down)
    sx = jnp.where(c4 == 0, left, jnp.where(c4 == 1, right, jnp.where(c4 == 2, upper, lower)))
    sc = jnp.where((lane & 1) == 0, jnp.where(row == 0, 0.0, prev1), jnp.where(row == TB - 1, 0.0, next1))
    shifted = jnp.where(j == 0, sc, sx)
    u = p + mu_ref[...] * (shifted - p)

    r = u[:, 0:RW_WIDTH]
    k = u[:, RW_WIDTH:2 * RW_WIDTH]
    v = u[:, 2 * RW_WIDTH:3 * RW_WIDTH]
    o = 3 * RW_WIDTH
    wd = u[:, o:o + 2 * RW_RANK]
    ad = u[:, o + 2 * RW_RANK:o + 4 * RW_RANK]
    gd = u[:, o + 4 * RW_RANK:]

    w_logit = w0_ref[...] + _bdot(jnp.tanh(wd), wup_ref[...])
    lw = -math.exp(-0.5) * _sigmoid(w_logit)
    a = _sigmoid(a0_ref[...] + _bdot(ad, aup_ref[...]))
    bd = bd_ref[...]
    kk = k * kk_ref[...]
    kk = kk / jnp.maximum(jnp.sqrt(_seg_sum(kk * kk, bd)), 1e-12)
    g = _bdot(_sigmoid(gd), gup_ref[...])
    if has_vres:
        gate = _sigmoid(v0_ref[...] + _bdot(_bdot(v, vdn_ref[...]), vup_ref[...]))
        vm = v + (vf_ref[0] - v) * gate
    else:
        vm = v
    ke_sum = jnp.zeros_like(k)
    for d in range(2):
        a_d = a[:, d * RW_WIDTH:(d + 1) * RW_WIDTH]
        ke_d = k * (1.0 + (a_d - 1.0) * ka_ref[...])
        lw_out[d, 0] = lw[:, d * RW_WIDTH:(d + 1) * RW_WIDTH]
        b_out[d, 0] = kk * a_d
        ke_out[d, 0] = ke_d
        ke_sum = ke_sum + ke_d
    r_out[0] = r
    kk_out[0] = kk
    v_out[0] = vm
    g_out[0] = g
    bon_out[0] = _seg_sum(r * ke_sum * rk_ref[...], bd) * vm


def _rwprep(p_rw, prm, v_first):
    B, N, _ = p_rw.shape
    nt = N // TB
    hb = TB // GRID_W
    nhb = N // GRID_W
    has_vres = v_first is not None
    W2 = 2 * RW_WIDTH
    in_specs = [
        _tile_spec(RW_COLS),
        pl.BlockSpec((1, GRID_W, RW_COLS), lambda b, j: (b, jnp.maximum(j * hb - 1, 0), 0)),
        pl.BlockSpec((1, GRID_W, RW_COLS), lambda b, j: (b, jnp.minimum(j * hb + hb, nhb - 1), 0)),
        _const_spec((1, RW_COLS)), _const_spec((2 * RW_RANK, W2)), _const_spec((1, W2)),
        _const_spec((2 * RW_RANK, W2)), _const_spec((1, W2)),
        _const_spec((1, RW_WIDTH)), _const_spec((1, RW_WIDTH)), _const_spec((1, RW_WIDTH)),
        _const_spec((RW_G_RANK, RW_WIDTH)), _const_spec((RW_WIDTH, RW_WIDTH)),
    ]
    args = [p_rw, p_rw, p_rw, prm["mu"], prm["w_up"], prm["w0"], prm["a_up"], prm["a0"],
            prm["k_k"], prm["k_a"], prm["r_k"], prm["g_up"], prm["bd64"]]
    if has_vres:
        in_specs += [_tile_spec(RW_WIDTH), _const_spec((RW_WIDTH, LANES)), _const_spec((LANES, RW_WIDTH)),
                     _const_spec((1, RW_WIDTH))]
        args += [v_first, prm["v_down"], prm["v_up"], prm["v0"]]
    tok = jax.ShapeDtypeStruct((B, N, RW_WIDTH), F32)
    dtok = jax.ShapeDtypeStruct((2, B, N, RW_WIDTH), F32)
    return pl.pallas_call(
        functools.partial(_rwprep_kernel, has_vres, nt),
        out_shape=[tok] * 5 + [dtok] * 3,
        grid=(B, nt),
        in_specs=in_specs,
        out_specs=[_tile_spec(RW_WIDTH)] * 5 + [_dir_tile_spec(RW_WIDTH)] * 3,
        compiler_params=_cparams(("parallel", "parallel")),
        name="rwprep",
    )(*args)


NCH = TB // CH


def _chunk_tri(rev):
    t = lax.broadcasted_iota(jnp.int32, (TB, TB), 0)
    s = lax.broadcasted_iota(jnp.int32, (TB, TB), 1)
    order = (s >= t) if rev else (s <= t)
    return jnp.where(jnp.logical_and(t // CH == s // CH, order), 1.0, 0.0).astype(BF16)


def _chunk_totals(x):
    tots = [jnp.sum(x[c * CH:(c + 1) * CH], axis=0, keepdims=True) for c in range(NCH)]
    full = jnp.concatenate([jnp.broadcast_to(t, (CH, x.shape[1])) for t in tots], axis=0)
    return tots, full


def _scan_masks(rev):
    ti = lax.broadcasted_iota(jnp.int32, (CH, LANES), 0)
    li = lax.broadcasted_iota(jnp.int32, (CH, LANES), 1)
    si = li & (CH - 1)
    incl = (si >= ti) if rev else (si <= ti)
    strict = (si > ti) if rev else (si < ti)
    return incl, strict


def _rwscan_kernel(rf_ref, kkf_ref, vf_ref, lwf_ref, bf_ref, kef_ref,
                   rb_ref, kkb_ref, vb_ref, lwb_ref, bb_ref, keb_ref, yf_ref, yb_ref, st_ref):
    j = pl.program_id(1)

    @pl.when(j == 0)
    def _():
        st_ref[...] = jnp.zeros_like(st_ref)

    li = lax.broadcasted_iota(jnp.int32, (CH, LANES), 1)
    lo = li < CH
    eye = jnp.where(lax.broadcasted_iota(jnp.int32, (CH, LANES), 0) == (li & (CH - 1)), 1.0, 0.0)
    rblk = lax.broadcasted_iota(jnp.int32, (LANES, LANES), 0) // CH
    cblk = lax.broadcasted_iota(jnp.int32, (LANES, LANES), 1) // CH
    bdmask = rblk == cblk
    n_pairs = RW_WIDTH // LANES

    def stack(z):
        return _pair_stack(z, lo).astype(BF16)

    items = {}
    w_tots = {}
    for d, refs in enumerate(((rf_ref, kkf_ref, vf_ref, lwf_ref, bf_ref, kef_ref),
                              (rb_ref, kkb_ref, vb_ref, lwb_ref, bb_ref, keb_ref))):
        rev = d == 1
        r, kk, v = refs[0][0], refs[1][0], refs[2][0]
        lw, b, ke = refs[3][0, 0], refs[4][0, 0], refs[5][0, 0]
        cum = _split_dot(_chunk_tri(rev), lw)
        tots, tot_full = _chunk_totals(lw)
        w_inv = jnp.exp(-cum)
        w_end = jnp.exp(tot_full - cum)
        rh = r * jnp.exp(cum)
        ah = -(kk * jnp.exp(cum - lw))
        bh = b * w_inv
        kh = ke * w_inv
        bt = b * w_end
        kt = ke * w_end
        m_incl, m_strict = _scan_masks(rev)
        for c in range(NCH):
            rows = slice(c * CH, (c + 1) * CH)
            w_tots[d, c] = jnp.exp(tots[c])
            for p in range(n_pairs):
                sl = slice(LANES * p, LANES * (p + 1))
                items[d, c, p] = dict(ah=ah[rows, sl], rh=rh[rows, sl], bh=bh[rows, sl], kh=kh[rows, sl],
                                      bt=bt[rows, sl], kt=kt[rows, sl], v=v[rows, sl],
                                      m_incl=m_incl, m_strict=m_strict)

    for it in items.values():
        it["lhs"] = jnp.concatenate([it["ah"], it["rh"]], axis=0).astype(BF16)
        rhs = jnp.concatenate([stack(it["bh"]), stack(it["kh"])], axis=0)
        it["a_all"] = _bdot_nt(it["lhs"], rhs)
    for it in items.values():
        a_all = it.pop("a_all")
        it["a_ab"] = jnp.where(it["m_strict"], a_all[:CH, :LANES], 0.0)
        a_ak = jnp.where(it["m_strict"], a_all[:CH, LANES:], 0.0)
        a_rb = jnp.where(it["m_incl"], a_all[CH:, :LANES], 0.0)
        a_rk = jnp.where(it["m_incl"], a_all[CH:, LANES:], 0.0)
        it["a_r"] = jnp.concatenate([a_rb, a_rk], axis=1).astype(BF16)
        it["v_bd"] = stack(it["v"])
        it["akv"] = _bdot(a_ak, it["v_bd"])
        it["t"] = eye + it["a_ab"]
        it["m"] = _bdot(it["a_ab"], stack(it["a_ab"]))
    for _ in range(int(math.log2(CH)) - 2):
        for it in items.values():
            z = _bdot(jnp.concatenate([it["m"], it["t"]], axis=0), stack(it["m"]))
            it["m"] = z[:CH]
            it["t"] = it["t"] + z[CH:]
    for it in items.values():
        it["t"] = (it["t"] + _bdot(it["t"], stack(it["m"]))).astype(BF16)
    for it in items.values():
        a_til = jnp.dot(it["t"], stack(it["ah"]), preferred_element_type=F32)
        it["lhs2"] = jnp.concatenate([a_til.astype(BF16), it["lhs"][CH:]], axis=0)
        it["cc"] = jnp.dot(it["t"], stack(it["akv"]), preferred_element_type=F32)
        it["rhs_t"] = jnp.concatenate([it["bt"], it["kt"]], axis=0).astype(BF16)

    st = {(d, p): st_ref[d, p] for d in range(2) for p in range(n_pairs)}
    for step in range(NCH):
        cur = {(d, p): items[d, (NCH - 1 - step) if d == 1 else step, p] for d in range(2) for p in range(n_pairs)}
        z1 = {k: _bdot_nt(it["lhs2"], st[k]) for k, it in cur.items()}
        u = {k: z1[k][:CH] + it["cc"] for k, it in cur.items()}
        upd = {k: _bdot_tn(jnp.concatenate([u[k], it["v"]], axis=0), it["rhs_t"]) for k, it in cur.items()}
        for (d, p), it in cur.items():
            c = (NCH - 1 - step) if d == 1 else step
            sl = slice(LANES * p, LANES * (p + 1))
            y = z1[d, p][CH:] + jnp.dot(it["a_r"], jnp.concatenate([stack(u[d, p]), it["v_bd"]], axis=0),
                                        preferred_element_type=F32)
            (yb_ref if d == 1 else yf_ref)[0, c * CH:(c + 1) * CH, sl] = y
            st[d, p] = st[d, p] * w_tots[d, c][:, sl] + jnp.where(bdmask, upd[d, p], 0.0)
    for (d, p), s in st.items():
        st_ref[d, p] = s


def _bidir_specs(c, n_tiles):
    def back(j):
        return jnp.where(j == 0, 0, n_tiles - j)

    tok_f = pl.BlockSpec((1, TB, c), lambda b, j: (b, j, 0))
    tok_b = pl.BlockSpec((1, TB, c), lambda b, j: (b, back(j), 0))
    dir_f = pl.BlockSpec((1, 1, TB, c), lambda b, j: (0, b, j, 0))
    dir_b = pl.BlockSpec((1, 1, TB, c), lambda b, j: (1, b, back(j), 0))
    return tok_f, tok_b, dir_f, dir_b


def _rwscan(r, kk, v, lw, b, ke):
    B, N, _ = r.shape
    tok_f, tok_b, dir_f, dir_b = _bidir_specs(RW_WIDTH, N // TB)
    out = jax.ShapeDtypeStruct((B, N, RW_WIDTH), F32)
    return pl.pallas_call(
        _rwscan_kernel,
        out_shape=[out, out],
        grid=(B, N // TB),
        in_specs=[tok_f, tok_f, tok_f, dir_f, dir_f, dir_f, tok_b, tok_b, tok_b, dir_b, dir_b, dir_b],
        out_specs=[tok_f, tok_b],
        scratch_shapes=[pltpu.VMEM((2, RW_WIDTH // LANES, LANES, LANES), F32)],
        compiler_params=_cparams(("parallel", "arbitrary")),
        name="rwscan",
    )(r, kk, v, lw, b, ke, r, kk, v, lw, b, ke)


def _glaprep_kernel(n_tiles, p_ref, hp_ref, hn_ref, cw_ref, aup_ref, ab_ref, q_out, k_out, v_out, og_out, lg_out):
    j = pl.program_id(1)
    u = p_ref[0][:, :GLA_QKV]
    row = lax.broadcasted_iota(jnp.int32, u.shape, 0)
    first = jnp.where(j <= 1, 0.0, 1.0) * hp_ref[0, 7:8, :GLA_QKV]
    last = jnp.where(jnp.logical_or(j == 0, j == n_tiles - 1), 0.0, 1.0) * hn_ref[0, 0:1, :GLA_QKV]
    prev1 = jnp.where(row == 0, first, pltpu.roll(u, 1, 0))
    next1 = jnp.where(row == TB - 1, last, pltpu.roll(u, TB - 1, 0))
    cw = cw_ref[...]
    conv = cw[0:1] * prev1 + cw[1:2] * u + cw[2:3] * next1
    qkv = conv * _sigmoid(conv)
    q_out[0] = qkv[:, :GLA_KW] * (GLA_DK ** -0.5)
    k_out[0] = qkv[:, GLA_KW:2 * GLA_KW]
    v_out[0] = qkv[:, 2 * GLA_KW:]
    og_out[0] = p_ref[0][:, GLA_QKV:GLA_QKV + GLA_WIDTH]
    z = _bdot(p_ref[0][:, GLA_QKV + GLA_WIDTH:], aup_ref[...]) + ab_ref[...]
    lg = (jnp.minimum(z, 0.0) - jnp.log1p(jnp.exp(-jnp.abs(z)))) * (1.0 / GLA_GATE_NORM)
    for d in range(2):
        lg_out[d, 0] = lg[:, d * GLA_KW:(d + 1) * GLA_KW]


def _glaprep(p_gl, prm):
    B, N, _ = p_gl.shape
    nt = N // TB
    sub = 8
    hb = TB // sub
    nhb = N // sub
    return pl.pallas_call(
        functools.partial(_glaprep_kernel, nt),
        out_shape=[jax.ShapeDtypeStruct((B, N, GLA_KW), F32), jax.ShapeDtypeStruct((B, N, GLA_KW), F32),
                   jax.ShapeDtypeStruct((B, N, GLA_WIDTH), F32), jax.ShapeDtypeStruct((B, N, GLA_WIDTH), F32),
                   jax.ShapeDtypeStruct((2, B, N, GLA_KW), F32)],
        grid=(B, nt),
        in_specs=[_tile_spec(GLA_PAD),
                  pl.BlockSpec((1, sub, GLA_PAD), lambda b, j: (b, jnp.maximum(j * hb - 1, 0), 0)),
                  pl.BlockSpec((1, sub, GLA_PAD), lambda b, j: (b, jnp.minimum(j * hb + hb, nhb - 1), 0)),
                  _const_spec((3, GLA_QKV)), _const_spec((LANES, 2 * GLA_KW)), _const_spec((1, 2 * GLA_KW))],
        out_specs=[_tile_spec(GLA_KW), _tile_spec(GLA_KW), _tile_spec(GLA_WIDTH), _tile_spec(GLA_WIDTH),
                   _dir_tile_spec(GLA_KW)],
        compiler_params=_cparams(("parallel", "parallel")),
        name="glaprep",
    )(p_gl, p_gl, p_gl, prm["conv"], prm["a_up"], prm["a_b"])


def _glascan_kernel(qf_ref, kf_ref, vf_ref, lgf_ref, qb_ref, kb_ref, vb_ref, lgb_ref, of_ref, ob_ref, st_ref):
    j = pl.program_id(1)

    @pl.when(j == 0)
    def _():
        st_ref[...] = jnp.zeros_like(st_ref)

    li = lax.broadcasted_iota(jnp.int32, (CH, LANES), 1)
    lo = li < CH
    lo2 = lax.broadcasted_iota(jnp.int32, (LANES, LANES), 1) < CH
    zeros_v = jnp.zeros((CH, GLA_DV), F32)
    n_pairs = GLA_KW // LANES

    items = {}
    decs = {}
    for d, refs in enumerate(((qf_ref, kf_ref, vf_ref, lgf_ref), (qb_ref, kb_ref, vb_ref, lgb_ref))):
        rev = d == 1
        q, k, v, lg = refs[0][0], refs[1][0], refs[2][0], refs[3][0, 0]
        cum = _split_dot(_chunk_tri(rev), lg)
        tots, tot_full = _chunk_totals(lg)
        qd = q * jnp.exp(cum)
        ki = k * jnp.exp(-cum)
        kend = k * jnp.exp(tot_full - cum)
        m_incl, _ = _scan_masks(rev)
        for c in range(NCH):
            rows = slice(c * CH, (c + 1) * CH)
            decs[d, c] = jnp.exp(tots[c])
            for p in range(n_pairs):
                sl = slice(LANES * p, LANES * (p + 1))
                items[d, c, p] = dict(qd=qd[rows, sl].astype(BF16), ki=ki[rows, sl], kend=kend[rows, sl],
                                      v0=v[rows, 2 * LANES * p:2 * LANES * p + LANES],
                                      v1=v[rows, 2 * LANES * p + LANES:2 * LANES * (p + 1)], m_incl=m_incl)

    for it in items.values():
        it["att"] = _bdot_nt(it["qd"], _pair_stack(it["ki"], lo))
    for it in items.values():
        att = jnp.where(it["m_incl"], it["att"], 0.0)
        v_bd = jnp.concatenate([jnp.concatenate([it["v0"], zeros_v], axis=1),
                                jnp.concatenate([zeros_v, it["v1"]], axis=1)], axis=0)
        it["o"] = _bdot(att, v_bd)
        it["upd"] = _bdot_tn(jnp.concatenate([it["v0"], it["v1"]], axis=0), _pair_stack(it["kend"], lo))

    for d in range(2):
        for p in range(n_pairs):
            st = st_ref[d, p]
            for step in range(NCH):
                c = (NCH - 1 - step) if d == 1 else step
                it = items[d, c, p]
                st_bd = jnp.concatenate([jnp.where(lo2, st, 0.0), jnp.where(lo2, 0.0, st)], axis=0)
                o = it["o"] + _bdot_nt(it["qd"], st_bd)
                (ob_ref if d == 1 else of_ref)[0, c * CH:(c + 1) * CH, 2 * LANES * p:2 * LANES * (p + 1)] = o
                st = st * decs[d, c][:, LANES * p:LANES * (p + 1)] + it["upd"]
            st_ref[d, p] = st


def _glascan(q, k, v, lg):
    B, N, _ = q.shape
    kf, kb, dkf, dkb = _bidir_specs(GLA_KW, N // TB)
    vf, vb, _, _ = _bidir_specs(GLA_WIDTH, N // TB)
    out = jax.ShapeDtypeStruct((B, N, GLA_WIDTH), F32)
    return pl.pallas_call(
        _glascan_kernel,
        out_shape=[out, out],
        grid=(B, N // TB),
        in_specs=[kf, kf, vf, dkf, kb, kb, vb, dkb],
        out_specs=[vf, vb],
        scratch_shapes=[pltpu.VMEM((2, GLA_KW // LANES, GLA_DV, LANES), F32)],
        compiler_params=_cparams(("parallel", "arbitrary")),
        name="glascan",
    )(q, k, v, lg, q, k, v, lg)


def _readout_kernel(yf_ref, yb_ref, bon_ref, g_ref, of_ref, ob_ref, og_ref, x_ref, mod_ref, gnw_ref, gnb_ref,
                    ggn_ref, wout_ref, gpost_ref, bd_ref, xo_ref):
    bd = bd_ref[...]
    y = yf_ref[0] + yb_ref[0]
    mu = _seg_sum(y, bd) * (1.0 / RW_HEAD)
    yc = y - mu
    var = _seg_sum(yc * yc, bd) * (1.0 / RW_HEAD)
    yn = yc * lax.rsqrt(var + RW_GN_EPS) * gnw_ref[...] + gnb_ref[...]
    rw = (yn + bon_ref[0]) * g_ref[0]
    o = of_ref[0] + ob_ref[0]
    og = og_ref[0]
    parts = [rw.astype(BF16)]
    for h in range(GLA_HEADS):
        sl = slice(GLA_DV * h, GLA_DV * (h + 1))
        oh = o[:, sl]
        on = oh * lax.rsqrt(jnp.mean(oh * oh, axis=-1, keepdims=True) + GLA_NORM_EPS)
        ogh = og[:, sl]
        parts.append((on * ggn_ref[:, sl] * (ogh * _sigmoid(ogh))).astype(BF16))
    cat = jnp.concatenate(parts, axis=1)
    mx = jnp.dot(cat, wout_ref[...], preferred_element_type=F32)
    xo_ref[0] = x_ref[0] + mod_ref[0, 0][2:3] * _rms(mx, gpost_ref[...])


def _readout(y, bonus, g, o, og, xc, modtab, prm):
    B, N, _ = xc.shape
    return pl.pallas_call(
        _readout_kernel,
        out_shape=jax.ShapeDtypeStruct((B, N, D), F32),
        grid=(B, N // TB),
        in_specs=[_tile_spec(RW_WIDTH)] * 4 + [_tile_spec(GLA_WIDTH)] * 3 + [_tile_spec(D), _mod_spec(),
                  _const_spec((1, RW_WIDTH)), _const_spec((1, RW_WIDTH)), _const_spec((1, GLA_WIDTH)),
                  _const_spec((D, D)), _const_spec((1, D)), _const_spec((RW_WIDTH, RW_WIDTH))],
        out_specs=_tile_spec(D),
        compiler_params=_cparams(("parallel", "parallel")),
        name="readout",
    )(y[0], y[1], bonus, g, o[0], o[1], og, xc, modtab, prm["gn_w"], prm["gn_b"], prm["gla_gn_w"], prm["w_out"],
      prm["norm_post"], prm["bd64"])


def _swiglu_acc(hb, wg_ref, wu_ref, wd_ref):
    acc = jnp.zeros((hb.shape[0], D), F32)
    for c in range(D_FF // FCH):
        sl = slice(c * FCH, (c + 1) * FCH)
        gate = jnp.dot(hb, wg_ref[:, sl], preferred_element_type=F32)
        up = jnp.dot(hb, wu_ref[:, sl], preferred_element_type=F32)
        act = (gate * _sigmoid(gate) * up).astype(BF16)
        acc = acc + jnp.dot(act, wd_ref[sl, :], preferred_element_type=F32)
    return acc


def _ffn_kernel(x_ref, mod_ref, gpre_ref, gpost_ref, wg_ref, wu_ref, wd_ref, xo_ref):
    mod = mod_ref[0, 0]
    x = x_ref[0]
    hb = (_rms(x, gpre_ref[...]) * (1.0 + mod[4:5]) + mod[3:4]).astype(BF16)
    fx = _swiglu_acc(hb, wg_ref, wu_ref, wd_ref)
    xo_ref[0] = x + mod[5:6] * _rms(fx, gpost_ref[...])


def _single_buffered(shape):
    nd = len(shape)
    return pl.BlockSpec(shape, lambda *_: (0,) * nd, pipeline_mode=pl.Buffered(1))


def _ffn(xc, modtab, g_pre, g_post, wg, wu, wd):
    B, N, _ = xc.shape
    return pl.pallas_call(
        _ffn_kernel,
        out_shape=jax.ShapeDtypeStruct((B, N, D), F32),
        grid=(B, N // TB),
        in_specs=[_tile_spec(D), _mod_spec(), _const_spec((1, D)), _const_spec((1, D)),
                  _single_buffered((D, D_FF)), _single_buffered((D, D_FF)), _single_buffered((D_FF, D))],
        out_specs=_tile_spec(D),
        compiler_params=_cparams(("parallel", "parallel")),
        name="ffn",
    )(xc, modtab, g_pre, g_post, wg, wu, wd)


MOE_TM = 512


def _moe_kernel(x_ref, mod_ref, gpre_ref, gpost_ref, router_ref, wg_ref, wu_ref, wd_ref, xo_ref,
                h_ref, comb_ref, acc_ref):
    e = pl.program_id(2)
    mod = mod_ref[0, 0]
    lane = lax.broadcasted_iota(jnp.int32, (MOE_TM, LANES), 1)

    @pl.when(e == 0)
    def _():
        h = _rms(x_ref[0], gpre_ref[...]) * (1.0 + mod[4:5]) + mod[3:4]
        h_ref[...] = h.astype(BF16)
        logits = jnp.dot(h, router_ref[...], precision=lax.Precision.HIGHEST, preferred_element_type=F32)
        logits = jnp.where(lane < N_EXPERTS, logits, -jnp.inf)
        v1 = jnp.max(logits, axis=-1, keepdims=True)
        i1 = jnp.min(jnp.where(logits == v1, lane, LANES), axis=-1, keepdims=True)
        rest = jnp.where(lane == i1, -jnp.inf, logits)
        v2 = jnp.max(rest, axis=-1, keepdims=True)
        i2 = jnp.min(jnp.where(rest == v2, lane, LANES), axis=-1, keepdims=True)
        e2 = jnp.exp(v2 - v1)
        w1 = 1.0 / (1.0 + e2)
        comb_ref[...] = jnp.where(lane == i1, w1, 0.0) + jnp.where(lane == i2, e2 * w1, 0.0)
        acc_ref[...] = jnp.zeros_like(acc_ref)

    ce = jnp.sum(jnp.where(lane == e, comb_ref[...], 0.0), axis=-1, keepdims=True)
    acc_ref[...] += ce * _swiglu_acc(h_ref[...], wg_ref.at[0], wu_ref.at[0], wd_ref.at[0])

    @pl.when(e == N_EXPERTS - 1)
    def _():
        xo_ref[0] = x_ref[0] + mod[5:6] * _rms(acc_ref[...], gpost_ref[...])


def _moe(xs, modx, g_pre, g_post, router, wg, wu, wd):
    B, S, _ = xs.shape
    xt = pl.BlockSpec((1, MOE_TM, D), lambda b, t, e: (b, t, 0))
    return pl.pallas_call(
        _moe_kernel,
        out_shape=jax.ShapeDtypeStruct((B, S, D), F32),
        grid=(B, S // MOE_TM, N_EXPERTS),
        in_specs=[xt, pl.BlockSpec((1, 1, 6, D), lambda b, t, e: (b, 1, 0, 0)),
                  _const_spec((1, D)), _const_spec((1, D)), _const_spec((D, LANES)),
                  pl.BlockSpec((1, D, D_FF), lambda b, t, e: (e, 0, 0)),
                  pl.BlockSpec((1, D, D_FF), lambda b, t, e: (e, 0, 0)),
                  pl.BlockSpec((1, D_FF, D), lambda b, t, e: (e, 0, 0))],
        out_specs=xt,
        scratch_shapes=[pltpu.VMEM((MOE_TM, D), BF16), pltpu.VMEM((MOE_TM, LANES), F32),
                        pltpu.VMEM((MOE_TM, D), F32)],
        compiler_params=_cparams(("parallel", "parallel", "arbitrary")),
        name="moe",
    )(xs, modx, g_pre, g_post, router, wg, wu, wd)


def _block_diag2(w):
    z = jnp.zeros_like(w[0])
    return jnp.concatenate([jnp.concatenate([w[0], z], axis=1), jnp.concatenate([z, w[1]], axis=1)], axis=0)


def _row(v):
    return v.reshape(1, -1).astype(F32)


def _head_ones(width, head):
    i = jnp.arange(width) // head
    return (i[:, None] == i[None, :]).astype(BF16)


def kernel(x, c, ctx, c_ctx, ada_w, ada_b, norm_mix_pre, norm_mix_post, norm_ffn_pre, norm_ffn_post, w_in, shift_mu, rw_w_up, rw_w0, rw_a_up, rw_a0, rw_k_k, rw_k_a, rw_r_k, rw_g_up, rw_gn_w, rw_gn_b, rw_v_down, rw_v_up, rw_v0, gla_conv, gla_a_up, gla_a_b, gla_gn_w, w_out, ffn_w_gate, ffn_w_up, ffn_w_down, moe_router, moe_w_gate, moe_w_up, moe_w_down):
    B, S, _ = x.shape
    n_ctx = ctx.shape[1]
    depth = w_in.shape[0]
    assert n_ctx == TB and S % TB == 0 and S % MOE_TM == 0 and depth == 2

    xc = jnp.concatenate([ctx, x], axis=1)
    cvec = jnp.concatenate([c, c_ctx[None, :], jnp.zeros((16 - B - 1, D), F32)], axis=0)
    bd64 = _head_ones(RW_WIDTH, RW_HEAD)
    v_first = None
    out = None
    for i in range(depth):
        last = i == depth - 1
        mods = _adaln(cvec, ada_w[i], _row(ada_b[i]))
        mod_x = mods[:B].reshape(B, 6, D)
        mod_c = jnp.broadcast_to(mods[B].reshape(1, 6, D), (B, 6, D))
        modtab = jnp.stack([mod_c, mod_x], axis=1)

        w_i = jnp.concatenate([w_in[i], jnp.zeros((D, GLA_PAD - GLA_COLS), F32)], axis=1).astype(BF16)
        p_rw, p_gl = _inproj(xc, _row(norm_mix_pre[i]), modtab, w_i)

        prm = dict(
            mu=_row(shift_mu[i]),
            w_up=_block_diag2(rw_w_up[i]).astype(BF16), w0=_row(rw_w0[i]),
            a_up=_block_diag2(rw_a_up[i]).astype(BF16), a0=_row(rw_a0[i]),
            k_k=_row(rw_k_k[i]), k_a=_row(rw_k_a[i]), r_k=_row(rw_r_k[i]),
            g_up=rw_g_up[i].astype(BF16), bd64=bd64,
            gn_w=_row(rw_gn_w[i]), gn_b=_row(rw_gn_b[i]), gla_gn_w=_row(gla_gn_w[i]),
            w_out=w_out[i].astype(BF16), norm_post=_row(norm_mix_post[i]),
        )
        gate_pad = jnp.zeros((LANES - 2 * GLA_GATE_RANK, 2 * GLA_KW), F32)
        gla_prm = dict(conv=gla_conv[i].astype(F32), a_b=_row(gla_a_b[i]),
                       a_up=jnp.concatenate([_block_diag2(gla_a_up[i]), gate_pad], axis=0).astype(BF16))
        if i > 0:
            pad = LANES - RW_V_RANK
            prm["v_down"] = jnp.concatenate([rw_v_down[i - 1], jnp.zeros((RW_WIDTH, pad), F32)], axis=1).astype(BF16)
            prm["v_up"] = jnp.concatenate([rw_v_up[i - 1], jnp.zeros((pad, RW_WIDTH), F32)], axis=0).astype(BF16)
            prm["v0"] = _row(rw_v0[i - 1])

        r, kk, vm, g, bonus, lw, bb, ke = _rwprep(p_rw, prm, v_first if i > 0 else None)
        if i == 0:
            v_first = vm
        y = _rwscan(r, kk, vm, lw, bb, ke)
        q, k, gv, og, lg = _glaprep(p_gl, gla_prm)
        o = _glascan(q, k, gv, lg)
        xc = _readout(y, bonus, g, o, og, xc, modtab, prm)

        jf = i // 2
        if i % 2 == 0:
            xc = _ffn(xc, modtab, _row(norm_ffn_pre[i]), _row(norm_ffn_post[i]),
                      ffn_w_gate[jf].astype(BF16), ffn_w_up[jf].astype(BF16), ffn_w_down[jf].astype(BF16))
        else:
            router = jnp.concatenate([moe_router[jf], jnp.zeros((D, LANES - N_EXPERTS), F32)], axis=1)
            out = _moe(xc[:, n_ctx:], modtab, _row(norm_ffn_pre[i]), _row(norm_ffn_post[i]), router,
                       moe_w_gate[jf].astype(BF16), moe_w_up[jf].astype(BF16), moe_w_down[jf].astype(BF16))
    return out
```

```python
import functools
import math

import jax
import jax.numpy as jnp
from jax import lax
from jax.experimental import pallas as pl
from jax.experimental.pallas import tpu as pltpu

F32, BF16 = jnp.float32, jnp.bfloat16

D = 1024
GRID_W = 64
RW_WIDTH = 512
RW_HEAD = 64
RW_RANK = 64
RW_G_RANK = 128
RW_V_RANK = 32
RW_GN_EPS = 64e-5
GLA_WIDTH = 512
GLA_HEADS = 4
GLA_DV = 128
GLA_DK = 64
GLA_KW = 256
GLA_GATE_RANK = 16
GLA_GATE_NORM = 16.0
GLA_NORM_EPS = 1e-5
D_FF = 2816
N_EXPERTS = 8
NORM_EPS = 1e-6
RW_COLS = 3 * RW_WIDTH + 4 * RW_RANK + RW_G_RANK
GLA_QKV = 2 * GLA_KW + GLA_WIDTH
GLA_COLS = GLA_QKV + GLA_WIDTH + 2 * GLA_GATE_RANK
GLA_PAD = 1664

LANES = 128
TB = 256
CH = 64
FCH = 256
VMEM_LIMIT = 56 * 1024 * 1024


def _cparams(sem):
    return pltpu.CompilerParams(dimension_semantics=sem, vmem_limit_bytes=VMEM_LIMIT)


def _bdot(a, b):
    return jnp.dot(a.astype(BF16), b.astype(BF16), preferred_element_type=F32)


def _bdot_nt(a, b):
    return lax.dot_general(a.astype(BF16), b.astype(BF16), (((1,), (1,)), ((), ())),
                           preferred_element_type=F32)


def _bdot_tn(a, b):
    return lax.dot_general(a.astype(BF16), b.astype(BF16), (((0,), (0,)), ((), ())),
                           preferred_element_type=F32)


def _split_dot(a_exact, x):
    h1 = x.astype(BF16)
    r1 = x - h1.astype(F32)
    h2 = r1.astype(BF16)
    h3 = (r1 - h2.astype(F32)).astype(BF16)
    return (jnp.dot(a_exact, h1, preferred_element_type=F32)
            + jnp.dot(a_exact, h2, preferred_element_type=F32)
            + jnp.dot(a_exact, h3, preferred_element_type=F32))


def _seg_sum(x, bd):
    hi = x.astype(BF16)
    lo = (x - hi.astype(F32)).astype(BF16)
    return jnp.dot(hi, bd, preferred_element_type=F32) + jnp.dot(lo, bd, preferred_element_type=F32)


def _sigmoid(x):
    return jax.nn.sigmoid(x)


def _rms(x, g):
    return x * lax.rsqrt(jnp.mean(x * x, axis=-1, keepdims=True) + NORM_EPS) * g


def _pair_stack(z, lo):
    return jnp.concatenate([jnp.where(lo, z, 0.0), jnp.where(lo, 0.0, z)], axis=0)


def _adaln_kernel(c_ref, w_ref, b_ref, o_ref):
    c = c_ref[...]
    s = c * _sigmoid(c)
    o_ref[...] = jnp.dot(s, w_ref[...], precision=lax.Precision.HIGHEST,
                         preferred_element_type=F32) + b_ref[...]


def _adaln(cvec, w, b):
    rows = cvec.shape[0]
    n = w.shape[1]
    return pl.pallas_call(
        _adaln_kernel,
        out_shape=jax.ShapeDtypeStruct((rows, n), F32),
        grid=(n // D,),
        in_specs=[pl.BlockSpec((rows, D), lambda i: (0, 0)),
                  pl.BlockSpec((D, D), lambda i: (0, i)),
                  pl.BlockSpec((1, D), lambda i: (0, i))],
        out_specs=pl.BlockSpec((rows, D), lambda i: (0, i)),
        compiler_params=_cparams(("arbitrary",)),
        name="adaln",
    )(cvec, w, b)


def _inproj_kernel(x_ref, g_ref, mod_ref, w_ref, prw_ref, pgl_ref):
    mod = mod_ref[0, 0]
    h = _rms(x_ref[0], g_ref[...]) * (1.0 + mod[1:2]) + mod[0:1]
    hb = h.astype(BF16)
    prw_ref[0] = jnp.dot(hb, w_ref[:, :RW_COLS], preferred_element_type=F32)
    pgl_ref[0] = jnp.dot(hb, w_ref[:, RW_COLS:], preferred_element_type=F32)


def _const_spec(shape):
    nd = len(shape)
    return pl.BlockSpec(shape, lambda *_: (0,) * nd)


def _tile_spec(c, first=0):
    return pl.BlockSpec((1, TB, c), lambda b, j: (b, j + first, 0))


def _dir_tile_spec(c):
    return pl.BlockSpec((2, 1, TB, c), lambda b, j: (0, b, j, 0))


def _mod_spec(first=0):
    return pl.BlockSpec((1, 1, 6, D), lambda b, j: (b, jnp.minimum(j + first, 1), 0, 0))


def _inproj(xc, g, modtab, w):
    B, N, _ = xc.shape
    return pl.pallas_call(
        _inproj_kernel,
        out_shape=[jax.ShapeDtypeStruct((B, N, RW_COLS), F32),
                   jax.ShapeDtypeStruct((B, N, GLA_PAD), F32)],
        grid=(B, N // TB),
        in_specs=[_tile_spec(D), _const_spec((1, D)), _mod_spec(),
                  _const_spec((D, RW_COLS + GLA_PAD))],
        out_specs=[_tile_spec(RW_COLS), _tile_spec(GLA_PAD)],
        compiler_params=_cparams(("parallel", "parallel")),
        name="inproj",
    )(xc, g, modtab, w)


def _rwprep_kernel(has_vres, n_tiles, p_ref, hu_ref, hd_ref, mu_ref, wup_ref, w0_ref, aup_ref, a0_ref,
                   kk_ref, ka_ref, rk_ref, gup_ref, bd_ref, *rest):
    if has_vres:
        vf_ref, vdn_ref, vup_ref, v0_ref = rest[:4]
        rest = rest[4:]
    r_out, kk_out, v_out, g_out, bon_out, lw_out, b_out, ke_out = rest
    j = pl.program_id(1)
    p = p_ref[0]
    row = lax.broadcasted_iota(jnp.int32, p.shape, 0)
    lane = lax.broadcasted_iota(jnp.int32, p.shape, 1)
    prev1 = pltpu.roll(p, 1, 0)
    next1 = pltpu.roll(p, TB - 1, 0)
    up = jnp.concatenate([hu_ref[0], p[:TB - GRID_W]], axis=0)
    down = jnp.concatenate([p[GRID_W:], hd_ref[0]], axis=0)
    col = row & (GRID_W - 1)
    c4 = lane & 3
    left = jnp.where(col == 0, 0.0, prev1)
    right = jnp.where(col == GRID_W - 1, 0.0, next1)
    upper = jnp.where(jnp.logical_and(j == 1, row < GRID_W), 0.0, up)
    lower = jnp.where(jnp.logical_and(j == n_tiles - 1, row >= TB - GRID_W), 0.0, down)
    sx = jnp.where(c4 == 0, left, jnp.where(c4 == 1, right, jnp.where(c4 == 2, upper, lower)))
    sc = jnp.where((lane & 1) == 0, jnp.where(row == 0, 0.0, prev1), jnp.where(row == TB - 1, 0.0, next1))
    shifted = jnp.where(j == 0, sc, sx)
    u = p + mu_ref[...] * (shifted - p)

    r = u[:, 0:RW_WIDTH]
    k = u[:, RW_WIDTH:2 * RW_WIDTH]
    v = u[:, 2 * RW_WIDTH:3 * RW_WIDTH]
    o = 3 * RW_WIDTH
    wd = u[:, o:o + 2 * RW_RANK]
    ad = u[:, o + 2 * RW_RANK:o + 4 * RW_RANK]
    gd = u[:, o + 4 * RW_RANK:]

    w_logit = w0_ref[...] + _bdot(jnp.tanh(wd), wup_ref[...])
    lw = -math.exp(-0.5) * _sigmoid(w_logit)
    a = _sigmoid(a0_ref[...] + _bdot(ad, aup_ref[...]))
    bd = bd_ref[...]
    kk = k * kk_ref[...]
    kk = kk / jnp.maximum(jnp.sqrt(_seg_sum(kk * kk, bd)), 1e-12)
    g = _bdot(_sigmoid(gd), gup_ref[...])
    if has_vres:
        gate = _sigmoid(v0_ref[...] + _bdot(_bdot(v, vdn_ref[...]), vup_ref[...]))
        vm = v + (vf_ref[0] - v) * gate
    else:
        vm = v
    ke_sum = jnp.zeros_like(k)
    for d in range(2):
        a_d = a[:, d * RW_WIDTH:(d + 1) * RW_WIDTH]
        ke_d = k * (1.0 + (a_d - 1.0) * ka_ref[...])
        lw_out[d, 0] = lw[:, d * RW_WIDTH:(d + 1) * RW_WIDTH]
        b_out[d, 0] = kk * a_d
        ke_out[d, 0] = ke_d
        ke_sum = ke_sum + ke_d
    r_out[0] = r
    kk_out[0] = kk
    v_out[0] = vm
    g_out[0] = g
    bon_out[0] = _seg_sum(r * ke_sum * rk_ref[...], bd) * vm


def _rwprep(p_rw, prm, v_first):
    B, N, _ = p_rw.shape
    nt = N // TB
    hb = TB // GRID_W
    nhb = N // GRID_W
    has_vres = v_first is not None
    W2 = 2 * RW_WIDTH
    in_specs = [
        _tile_spec(RW_COLS),
        pl.BlockSpec((1, GRID_W, RW_COLS), lambda b, j: (b, jnp.maximum(j * hb - 1, 0), 0)),
        pl.BlockSpec((1, GRID_W, RW_COLS), lambda b, j: (b, jnp.minimum(j * hb + hb, nhb - 1), 0)),
        _const_spec((1, RW_COLS)), _const_spec((2 * RW_RANK, W2)), _const_spec((1, W2)),
        _const_spec((2 * RW_RANK, W2)), _const_spec((1, W2)),
        _const_spec((1, RW_WIDTH)), _const_spec((1, RW_WIDTH)), _const_spec((1, RW_WIDTH)),
        _const_spec((RW_G_RANK, RW_WIDTH)), _const_spec((RW_WIDTH, RW_WIDTH)),
    ]
    args = [p_rw, p_rw, p_rw, prm["mu"], prm["w_up"], prm["w0"], prm["a_up"], prm["a0"],
            prm["k_k"], prm["k_a"], prm["r_k"], prm["g_up"], prm["bd64"]]
    if has_vres:
        in_specs += [_tile_spec(RW_WIDTH), _const_spec((RW_WIDTH, LANES)), _const_spec((LANES, RW_WIDTH)),
                     _const_spec((1, RW_WIDTH))]
        args += [v_first, prm["v_down"], prm["v_up"], prm["v0"]]
    tok = jax.ShapeDtypeStruct((B, N, RW_WIDTH), F32)
    dtok = jax.ShapeDtypeStruct((2, B, N, RW_WIDTH), F32)
    return pl.pallas_call(
        functools.partial(_rwprep_kernel, has_vres, nt),
        out_shape=[tok] * 5 + [dtok] * 3,
        grid=(B, nt),
        in_specs=in_specs,
        out_specs=[_tile_spec(RW_WIDTH)] * 5 + [_dir_tile_spec(RW_WIDTH)] * 3,
        compiler_params=_cparams(("parallel", "parallel")),
        name="rwprep",
    )(*args)


NCH = TB // CH


def _chunk_tri(rev):
    t = lax.broadcasted_iota(jnp.int32, (TB, TB), 0)
    s = lax.broadcasted_iota(jnp.int32, (TB, TB), 1)
    order = (s >= t) if rev else (s <= t)
    return jnp.where(jnp.logical_and(t // CH == s // CH, order), 1.0, 0.0).astype(BF16)


def _chunk_totals(x):
    tots = [jnp.sum(x[c * CH:(c + 1) * CH], axis=0, keepdims=True) for c in range(NCH)]
    full = jnp.concatenate([jnp.broadcast_to(t, (CH, x.shape[1])) for t in tots], axis=0)
    return tots, full


def _scan_masks(rev):
    ti = lax.broadcasted_iota(jnp.int32, (CH, LANES), 0)
    li = lax.broadcasted_iota(jnp.int32, (CH, LANES), 1)
    si = li & (CH - 1)
    incl = (si >= ti) if rev else (si <= ti)
    strict = (si > ti) if rev else (si < ti)
    return incl, strict


def _rwscan_kernel(rf_ref, kkf_ref, vf_ref, lwf_ref, bf_ref, kef_ref,
                   rb_ref, kkb_ref, vb_ref, lwb_ref, bb_ref, keb_ref, yf_ref, yb_ref, st_ref):
    j = pl.program_id(1)

    @pl.when(j == 0)
    def _():
        st_ref[...] = jnp.zeros_like(st_ref)

    li = lax.broadcasted_iota(jnp.int32, (CH, LANES), 1)
    lo = li < CH
    eye = jnp.where(lax.broadcasted_iota(jnp.int32, (CH, LANES), 0) == (li & (CH - 1)), 1.0, 0.0)
    rblk = lax.broadcasted_iota(jnp.int32, (LANES, LANES), 0) // CH
    cblk = lax.broadcasted_iota(jnp.int32, (LANES, LANES), 1) // CH
    bdmask = rblk == cblk
    n_pairs = RW_WIDTH // LANES

    def stack(z):
        return _pair_stack(z, lo).astype(BF16)

    items = {}
    w_tots = {}
    for d, refs in enumerate(((rf_ref, kkf_ref, vf_ref, lwf_ref, bf_ref, kef_ref),
                              (rb_ref, kkb_ref, vb_ref, lwb_ref, bb_ref, keb_ref))):
        rev = d == 1
        r, kk, v = refs[0][0], refs[1][0], refs[2][0]
        lw, b, ke = refs[3][0, 0], refs[4][0, 0], refs[5][0, 0]
        cum = _split_dot(_chunk_tri(rev), lw)
        tots, tot_full = _chunk_totals(lw)
        w_inv = jnp.exp(-cum)
        w_end = jnp.exp(tot_full - cum)
        rh = r * jnp.exp(cum)
        ah = -(kk * jnp.exp(cum - lw))
        bh = b * w_inv
        kh = ke * w_inv
        bt = b * w_end
        kt = ke * w_end
        m_incl, m_strict = _scan_masks(rev)
        for c in range(NCH):
            rows = slice(c * CH, (c + 1) * CH)
            w_tots[d, c] = jnp.exp(tots[c])
            for p in range(n_pairs):
                sl = slice(LANES * p, LANES * (p + 1))
                items[d, c, p] = dict(ah=ah[rows, sl], rh=rh[rows, sl], bh=bh[rows, sl], kh=kh[rows, sl],
                                      bt=bt[rows, sl], kt=kt[rows, sl], v=v[rows, sl],
                                      m_incl=m_incl, m_strict=m_strict)

    for it in items.values():
        it["lhs"] = jnp.concatenate([it["ah"], it["rh"]], axis=0).astype(BF16)
        rhs = jnp.concatenate([stack(it["bh"]), stack(it["kh"])], axis=0)
        it["a_all"] = _bdot_nt(it["lhs"], rhs)
    for it in items.values():
        a_all = it.pop("a_all")
        it["a_ab"] = jnp.where(it["m_strict"], a_all[:CH, :LANES], 0.0)
        a_ak = jnp.where(it["m_strict"], a_all[:CH, LANES:], 0.0)
        a_rb = jnp.where(it["m_incl"], a_all[CH:, :LANES], 0.0)
        a_rk = jnp.where(it["m_incl"], a_all[CH:, LANES:], 0.0)
        it["a_r"] = jnp.concatenate([a_rb, a_rk], axis=1).astype(BF16)
        it["v_bd"] = stack(it["v"])
        it["akv"] = _bdot(a_ak, it["v_bd"])
        it["t"] = eye + it["a_ab"]
        it["m"] = _bdot(it["a_ab"], stack(it["a_ab"]))
    for _ in range(int(math.log2(CH)) - 2):
        for it in items.values():
            z = _bdot(jnp.concatenate([it["m"], it["t"]], axis=0), stack(it["m"]))
            it["m"] = z[:CH]
            it["t"] = it["t"] + z[CH:]
    for it in items.values():
        it["t"] = (it["t"] + _bdot(it["t"], stack(it["m"]))).astype(BF16)
    for it in items.values():
        a_til = jnp.dot(it["t"], stack(it["ah"]), preferred_element_type=F32)
        it["lhs2"] = jnp.concatenate([a_til.astype(BF16), it["lhs"][CH:]], axis=0)
        it["cc"] = jnp.dot(it["t"], stack(it["akv"]), preferred_element_type=F32)
        it["rhs_t"] = jnp.concatenate([it["bt"], it["kt"]], axis=0).astype(BF16)

    st = {(d, p): st_ref[d, p] for d in range(2) for p in range(n_pairs)}
    for step in range(NCH):
        cur = {(d, p): items[d, (NCH - 1 - step) if d == 1 else step, p] for d in range(2) for p in range(n_pairs)}
        z1 = {k: _bdot_nt(it["lhs2"], st[k]) for k, it in cur.items()}
        u = {k: z1[k][:CH] + it["cc"] for k, it in cur.items()}
        upd = {k: _bdot_tn(jnp.concatenate([u[k], it["v"]], axis=0), it["rhs_t"]) for k, it in cur.items()}
        for (d, p), it in cur.items():
            c = (NCH - 1 - step) if d == 1 else step
            sl = slice(LANES * p, LANES * (p + 1))
            y = z1[d, p][CH:] + jnp.dot(it["a_r"], jnp.concatenate([stack(u[d, p]), it["v_bd"]], axis=0),
                                        preferred_element_type=F32)
            (yb_ref if d == 1 else yf_ref)[0, c * CH:(c + 1) * CH, sl] = y
            st[d, p] = st[d, p] * w_tots[d, c][:, sl] + jnp.where(bdmask, upd[d, p], 0.0)
    for (d, p), s in st.items():
        st_ref[d, p] = s


def _bidir_specs(c, n_tiles):
    def back(j):
        return jnp.where(j == 0, 0, n_tiles - j)

    tok_f = pl.BlockSpec((1, TB, c), lambda b, j: (b, j, 0))
    tok_b = pl.BlockSpec((1, TB, c), lambda b, j: (b, back(j), 0))
    dir_f = pl.BlockSpec((1, 1, TB, c), lambda b, j: (0, b, j, 0))
    dir_b = pl.BlockSpec((1, 1, TB, c), lambda b, j: (1, b, back(j), 0))
    return tok_f, tok_b, dir_f, dir_b


def _rwscan(r, kk, v, lw, b, ke):
    B, N, _ = r.shape
    tok_f, tok_b, dir_f, dir_b = _bidir_specs(RW_WIDTH, N // TB)
    out = jax.ShapeDtypeStruct((B, N, RW_WIDTH), F32)
    return pl.pallas_call(
        _rwscan_kernel,
        out_shape=[out, out],
        grid=(B, N // TB),
        in_specs=[tok_f, tok_f, tok_f, dir_f, dir_f, dir_f, tok_b, tok_b, tok_b, dir_b, dir_b, dir_b],
        out_specs=[tok_f, tok_b],
        scratch_shapes=[pltpu.VMEM((2, RW_WIDTH // LANES, LANES, LANES), F32)],
        compiler_params=_cparams(("parallel", "arbitrary")),
        name="rwscan",
    )(r, kk, v, lw, b, ke, r, kk, v, lw, b, ke)


def _glaprep_kernel(n_tiles, p_ref, hp_ref, hn_ref, cw_ref, aup_ref, ab_ref, q_out, k_out, v_out, og_out, lg_out):
    j = pl.program_id(1)
    u = p_ref[0][:, :GLA_QKV]
    row = lax.broadcasted_iota(jnp.int32, u.shape, 0)
    first = jnp.where(j <= 1, 0.0, 1.0) * hp_ref[0, 7:8, :GLA_QKV]
    last = jnp.where(jnp.logical_or(j == 0, j == n_tiles - 1), 0.0, 1.0) * hn_ref[0, 0:1, :GLA_QKV]
    prev1 = jnp.where(row == 0, first, pltpu.roll(u, 1, 0))
    next1 = jnp.where(row == TB - 1, last, pltpu.roll(u, TB - 1, 0))
    cw = cw_ref[...]
    conv = cw[0:1] * prev1 + cw[1:2] * u + cw[2:3] * next1
    qkv = conv * _sigmoid(conv)
    q_out[0] = qkv[:, :GLA_KW] * (GLA_DK ** -0.5)
    k_out[0] = qkv[:, GLA_KW:2 * GLA_KW]
    v_out[0] = qkv[:, 2 * GLA_KW:]
    og_out[0] = p_ref[0][:, GLA_QKV:GLA_QKV + GLA_WIDTH]
    z = _bdot(p_ref[0][:, GLA_QKV + GLA_WIDTH:], aup_ref[...]) + ab_ref[...]
    lg = (jnp.minimum(z, 0.0) - jnp.log1p(jnp.exp(-jnp.abs(z)))) * (1.0 / GLA_GATE_NORM)
    for d in range(2):
        lg_out[d, 0] = lg[:, d * GLA_KW:(d + 1) * GLA_KW]


def _glaprep(p_gl, prm):
    B, N, _ = p_gl.shape
    nt = N // TB
    sub = 8
    hb = TB // sub
    nhb = N // sub
    return pl.pallas_call(
        functools.partial(_glaprep_kernel, nt),
        out_shape=[jax.ShapeDtypeStruct((B, N, GLA_KW), F32), jax.ShapeDtypeStruct((B, N, GLA_KW), F32),
                   jax.ShapeDtypeStruct((B, N, GLA_WIDTH), F32), jax.ShapeDtypeStruct((B, N, GLA_WIDTH), F32),
                   jax.ShapeDtypeStruct((2, B, N, GLA_KW), F32)],
        grid=(B, nt),
        in_specs=[_tile_spec(GLA_PAD),
                  pl.BlockSpec((1, sub, GLA_PAD), lambda b, j: (b, jnp.maximum(j * hb - 1, 0), 0)),
                  pl.BlockSpec((1, sub, GLA_PAD), lambda b, j: (b, jnp.minimum(j * hb + hb, nhb - 1), 0)),
                  _const_spec((3, GLA_QKV)), _const_spec((LANES, 2 * GLA_KW)), _const_spec((1, 2 * GLA_KW))],
        out_specs=[_tile_spec(GLA_KW), _tile_spec(GLA_KW), _tile_spec(GLA_WIDTH), _tile_spec(GLA_WIDTH),
                   _dir_tile_spec(GLA_KW)],
        compiler_params=_cparams(("parallel", "parallel")),
        name="glaprep",
    )(p_gl, p_gl, p_gl, prm["conv"], prm["a_up"], prm["a_b"])


def _glascan_kernel(qf_ref, kf_ref, vf_ref, lgf_ref, qb_ref, kb_ref, vb_ref, lgb_ref, of_ref, ob_ref, st_ref):
    j = pl.program_id(1)

    @pl.when(j == 0)
    def _():
        st_ref[...] = jnp.zeros_like(st_ref)

    li = lax.broadcasted_iota(jnp.int32, (CH, LANES), 1)
    lo = li < CH
    lo2 = lax.broadcasted_iota(jnp.int32, (LANES, LANES), 1) < CH
    zeros_v = jnp.zeros((CH, GLA_DV), F32)
    n_pairs = GLA_KW // LANES

    items = {}
    decs = {}
    for d, refs in enumerate(((qf_ref, kf_ref, vf_ref, lgf_ref), (qb_ref, kb_ref, vb_ref, lgb_ref))):
        rev = d == 1
        q, k, v, lg = refs[0][0], refs[1][0], refs[2][0], refs[3][0, 0]
        cum = _split_dot(_chunk_tri(rev), lg)
        tots, tot_full = _chunk_totals(lg)
        qd = q * jnp.exp(cum)
        ki = k * jnp.exp(-cum)
        kend = k * jnp.exp(tot_full - cum)
        m_incl, _ = _scan_masks(rev)
        for c in range(NCH):
            rows = slice(c * CH, (c + 1) * CH)
            decs[d, c] = jnp.exp(tots[c])
            for p in range(n_pairs):
                sl = slice(LANES * p, LANES * (p + 1))
                items[d, c, p] = dict(qd=qd[rows, sl].astype(BF16), ki=ki[rows, sl], kend=kend[rows, sl],
                                      v0=v[rows, 2 * LANES * p:2 * LANES * p + LANES],
                                      v1=v[rows, 2 * LANES * p + LANES:2 * LANES * (p + 1)], m_incl=m_incl)

    for it in items.values():
        it["att"] = _bdot_nt(it["qd"], _pair_stack(it["ki"], lo))
    for it in items.values():
        att = jnp.where(it["m_incl"], it["att"], 0.0)
        v_bd = jnp.concatenate([jnp.concatenate([it["v0"], zeros_v], axis=1),
                                jnp.concatenate([zeros_v, it["v1"]], axis=1)], axis=0)
        it["o"] = _bdot(att, v_bd)
        it["upd"] = _bdot_tn(jnp.concatenate([it["v0"], it["v1"]], axis=0), _pair_stack(it["kend"], lo))

    for d in range(2):
        for p in range(n_pairs):
            st = st_ref[d, p]
            for step in range(NCH):
                c = (NCH - 1 - step) if d == 1 else step
                it = items[d, c, p]
                st_bd = jnp.concatenate([jnp.where(lo2, st, 0.0), jnp.where(lo2, 0.0, st)], axis=0)
                o = it["o"] + _bdot_nt(it["qd"], st_bd)
                (ob_ref if d == 1 else of_ref)[0, c * CH:(c + 1) * CH, 2 * LANES * p:2 * LANES * (p + 1)] = o
                st = st * decs[d, c][:, LANES * p:LANES * (p + 1)] + it["upd"]
            st_ref[d, p] = st


def _glascan(q, k, v, lg):
    B, N, _ = q.shape
    kf, kb, dkf, dkb = _bidir_specs(GLA_KW, N // TB)
    vf, vb, _, _ = _bidir_specs(GLA_WIDTH, N // TB)
    out = jax.ShapeDtypeStruct((B, N, GLA_WIDTH), F32)
    return pl.pallas_call(
        _glascan_kernel,
        out_shape=[out, out],
        grid=(B, N // TB),
        in_specs=[kf, kf, vf, dkf, kb, kb, vb, dkb],
        out_specs=[vf, vb],
        scratch_shapes=[pltpu.VMEM((2, GLA_KW // LANES, GLA_DV, LANES), F32)],
        compiler_params=_cparams(("parallel", "arbitrary")),
        name="glascan",
    )(q, k, v, lg, q, k, v, lg)


def _readout_kernel(yf_ref, yb_ref, bon_ref, g_ref, of_ref, ob_ref, og_ref, x_ref, mod_ref, gnw_ref, gnb_ref,
                    ggn_ref, wout_ref, gpost_ref, bd_ref, xo_ref):
    bd = bd_ref[...]
    y = yf_ref[0] + yb_ref[0]
    mu = _seg_sum(y, bd) * (1.0 / RW_HEAD)
    yc = y - mu
    var = _seg_sum(yc * yc, bd) * (1.0 / RW_HEAD)
    yn = yc * lax.rsqrt(var + RW_GN_EPS) * gnw_ref[...] + gnb_ref[...]
    rw = (yn + bon_ref[0]) * g_ref[0]
    o = of_ref[0] + ob_ref[0]
    og = og_ref[0]
    parts = [rw.astype(BF16)]
    for h in range(GLA_HEADS):
        sl = slice(GLA_DV * h, GLA_DV * (h + 1))
        oh = o[:, sl]
        on = oh * lax.rsqrt(jnp.mean(oh * oh, axis=-1, keepdims=True) + GLA_NORM_EPS)
        ogh = og[:, sl]
        parts.append((on * ggn_ref[:, sl] * (ogh * _sigmoid(ogh))).astype(BF16))
    cat = jnp.concatenate(parts, axis=1)
    mx = jnp.dot(cat, wout_ref[...], preferred_element_type=F32)
    xo_ref[0] = x_ref[0] + mod_ref[0, 0][2:3] * _rms(mx, gpost_ref[...])


def _readout(y, bonus, g, o, og, xc, modtab, prm, latents_only):
    B, N, _ = xc.shape
    f = 1 if latents_only else 0
    return pl.pallas_call(
        _readout_kernel,
        out_shape=jax.ShapeDtypeStruct((B, N - f * TB, D), F32),
        grid=(B, N // TB - f),
        in_specs=[_tile_spec(RW_WIDTH, f)] * 4 + [_tile_spec(GLA_WIDTH, f)] * 3 + [_tile_spec(D, f), _mod_spec(f),
                  _const_spec((1, RW_WIDTH)), _const_spec((1, RW_WIDTH)), _const_spec((1, GLA_WIDTH)),
                  _const_spec((D, D)), _const_spec((1, D)), _const_spec((RW_WIDTH, RW_WIDTH))],
        out_specs=_tile_spec(D),
        compiler_params=_cparams(("parallel", "parallel")),
        name="readout",
    )(y[0], y[1], bonus, g, o[0], o[1], og, xc, modtab, prm["gn_w"], prm["gn_b"], prm["gla_gn_w"], prm["w_out"],
      prm["norm_post"], prm["bd64"])


def _swiglu_acc(hb, wg_ref, wu_ref, wd_ref):
    acc = jnp.zeros((hb.shape[0], D), F32)
    for c in range(D_FF // FCH):
        sl = slice(c * FCH, (c + 1) * FCH)
        gate = jnp.dot(hb, wg_ref[:, sl], preferred_element_type=F32)
        up = jnp.dot(hb, wu_ref[:, sl], preferred_element_type=F32)
        act = (gate * _sigmoid(gate) * up).astype(BF16)
        acc = acc + jnp.dot(act, wd_ref[sl, :], preferred_element_type=F32)
    return acc


def _ffn_kernel(x_ref, mod_ref, gpre_ref, gpost_ref, wg_ref, wu_ref, wd_ref, xo_ref):
    mod = mod_ref[0, 0]
    x = x_ref[0]
    hb = (_rms(x, gpre_ref[...]) * (1.0 + mod[4:5]) + mod[3:4]).astype(BF16)
    fx = _swiglu_acc(hb, wg_ref, wu_ref, wd_ref)
    xo_ref[0] = x + mod[5:6] * _rms(fx, gpost_ref[...])


def _single_buffered(shape):
    nd = len(shape)
    return pl.BlockSpec(shape, lambda *_: (0,) * nd, pipeline_mode=pl.Buffered(1))


def _ffn(xc, modtab, g_pre, g_post, wg, wu, wd):
    B, N, _ = xc.shape
    return pl.pallas_call(
        _ffn_kernel,
        out_shape=jax.ShapeDtypeStruct((B, N, D), F32),
        grid=(B, N // TB),
        in_specs=[_tile_spec(D), _mod_spec(), _const_spec((1, D)), _const_spec((1, D)),
                  _single_buffered((D, D_FF)), _single_buffered((D, D_FF)), _single_buffered((D_FF, D))],
        out_specs=_tile_spec(D),
        compiler_params=_cparams(("parallel", "parallel")),
        name="ffn",
    )(xc, modtab, g_pre, g_post, wg, wu, wd)


MOE_TB = 1024
MOE_SEG = 32
MOE_TM = 512
MOE_R = 2 * MOE_TB + N_EXPERTS * MOE_SEG
MOE_NP = MOE_R // MOE_SEG


def _moe_route_kernel(x_ref, mod_ref, gpre_ref, router_ref, h_ref, info_ref, infot_ref, cnt_ref):
    mod = mod_ref[0, 0]
    h = _rms(x_ref[0], gpre_ref[...]) * (1.0 + mod[4:5]) + mod[3:4]
    h_ref[0] = h.astype(BF16)
    lane = lax.broadcasted_iota(jnp.int32, (MOE_TB, LANES), 1)
    logits = jnp.dot(h, router_ref[...], precision=lax.Precision.HIGHEST, preferred_element_type=F32)
    logits = jnp.where(lane < N_EXPERTS, logits, -jnp.inf)
    v1 = jnp.max(logits, axis=-1, keepdims=True)
    i1 = jnp.min(jnp.where(logits == v1, lane, LANES), axis=-1, keepdims=True)
    rest = jnp.where(lane == i1, -jnp.inf, logits)
    v2 = jnp.max(rest, axis=-1, keepdims=True)
    i2 = jnp.min(jnp.where(rest == v2, lane, LANES), axis=-1, keepdims=True)
    ex = jnp.exp(v2 - v1)
    w1 = 1.0 / (1.0 + ex)
    w2 = ex * w1
    e1 = jnp.where(lane == i1, 1.0, 0.0)
    e2 = jnp.where(lane == i2, 1.0, 0.0)
    es = e1 + e2
    t = lax.broadcasted_iota(jnp.int32, (MOE_TB, MOE_TB), 0)
    s = lax.broadcasted_iota(jnp.int32, (MOE_TB, MOE_TB), 1)
    before = jnp.where(s < t, 1.0, 0.0).astype(BF16)
    rank = jnp.dot(before, es.astype(BF16), preferred_element_type=F32)
    cnt = jnp.sum(es, axis=0, keepdims=True)
    segs = jnp.floor((cnt + (MOE_SEG - 1)) * (1.0 / MOE_SEG))
    ea = lax.broadcasted_iota(jnp.int32, (LANES, LANES), 0)
    eb = lax.broadcasted_iota(jnp.int32, (LANES, LANES), 1)
    earlier = jnp.where(ea < eb, 1.0, 0.0).astype(BF16)
    start = jnp.dot(jnp.broadcast_to(segs, (8, LANES)).astype(BF16), earlier,
                    preferred_element_type=F32)[0:1] * MOE_SEG
    pos = rank + start
    d1 = jnp.sum(e1 * pos, axis=-1, keepdims=True)
    d2 = jnp.sum(e2 * pos, axis=-1, keepdims=True)
    info = jnp.where(lane == 0, d1, jnp.where(lane == 1, d2, jnp.where(lane == 2, w1, jnp.where(lane == 3, w2, 0.0))))
    info_ref[0] = info
    infot_ref[0] = jnp.transpose(info)[0:8]
    cnt_ref[0] = jnp.broadcast_to(cnt, (8, LANES))


def _moe_route(xs, modtab, g_pre, router):
    B, S, _ = xs.shape
    per = S // MOE_TB
    nb = B * per
    blk = lambda c: pl.BlockSpec((1, MOE_TB, c), lambda i: (i // per, i % per, 0))
    flat = lambda r, c: pl.BlockSpec((1, r, c), lambda i: (i, 0, 0))
    return pl.pallas_call(
        _moe_route_kernel,
        out_shape=[jax.ShapeDtypeStruct((nb, MOE_TB, D), BF16), jax.ShapeDtypeStruct((nb, MOE_TB, LANES), F32),
                   jax.ShapeDtypeStruct((nb, 8, MOE_TB), F32), jax.ShapeDtypeStruct((nb, 8, LANES), F32)],
        grid=(nb,),
        in_specs=[blk(D), pl.BlockSpec((1, 1, 6, D), lambda i: (i // per, 1, 0, 0)),
                  _const_spec((1, D)), _const_spec((D, LANES))],
        out_specs=[flat(MOE_TB, D), flat(MOE_TB, LANES), flat(8, MOE_TB), flat(8, LANES)],
        compiler_params=_cparams(("parallel",)),
        name="moe_route",
    )(xs, modtab, g_pre, router)


def _moe_plan(cnt, n_tiles):
    pc = (cnt + MOE_SEG - 1) // MOE_SEG * MOE_SEG
    inc = jnp.cumsum(pc, axis=1)
    loff = inc - pc
    reg = (jnp.sum(pc, axis=0) + MOE_TM - 1) // MOE_TM * MOE_TM
    gend = jnp.cumsum(reg)
    goff = (gend - reg)[None, :] + jnp.cumsum(pc, axis=0) - pc
    rows = jnp.arange(MOE_NP, dtype=jnp.int32) * MOE_SEG
    e_p = jnp.sum((rows[None, :, None] >= inc[:, None, :]).astype(jnp.int32), axis=-1)
    e_c = jnp.minimum(e_p, N_EXPERTS - 1)
    dst = jnp.take_along_axis(goff, e_c, axis=1) + rows[None, :] - jnp.take_along_axis(loff, e_c, axis=1)
    dst = jnp.where(e_p < N_EXPERTS, dst, 0).astype(jnp.int32)
    n_valid = (inc[:, -1] // MOE_SEG).astype(jnp.int32)
    trow = jnp.arange(n_tiles, dtype=jnp.int32) * MOE_TM
    te = jnp.sum((trow[:, None] >= gend[None, :]).astype(jnp.int32), axis=-1)
    valid = te < N_EXPERTS
    last = gend[-1] // MOE_TM - 1
    te = jnp.where(valid, te, te[last]).astype(jnp.int32)
    src = jnp.where(valid, jnp.arange(n_tiles, dtype=jnp.int32), last).astype(jnp.int32)
    return dst, n_valid, te, src, valid.astype(jnp.int32)


def _piece_copy(src_ref, src_row, dst_ref, dst_row, sem):
    return pltpu.make_async_copy(src_ref.at[pl.ds(src_row, MOE_SEG)], dst_ref.at[pl.ds(dst_row, MOE_SEG)], sem)


def _moe_gather_kernel(dst_ref, nv_ref, h_ref, infot_ref, xg_in_ref, xg_ref, buf_ref, sem_ref):
    del xg_in_ref
    i = pl.program_id(0)
    it = infot_ref[0]
    rr = lax.broadcasted_iota(jnp.int32, (MOE_R, MOE_TB), 0).astype(F32)
    onehot = jnp.where(rr == it[0:1], 1.0, jnp.where(rr == it[1:2], 1.0, 0.0)).astype(BF16)
    buf_ref[...] = jnp.dot(onehot, h_ref[0], preferred_element_type=F32).astype(BF16)
    nv = nv_ref[i]

    def piece(p):
        return _piece_copy(buf_ref, pl.multiple_of(p * MOE_SEG, MOE_SEG),
                           xg_ref, pl.multiple_of(dst_ref[i, p], MOE_SEG), sem_ref.at[p])

    def start(p, c):
        piece(p).start()
        return c

    def wait(p, c):
        piece(p).wait()
        return c

    lax.fori_loop(0, nv, start, 0)
    lax.fori_loop(0, nv, wait, 0)


def _moe_gather(dst, n_valid, h, info_t, n_rows):
    nb = h.shape[0]
    flat = lambda r, c: pl.BlockSpec((1, r, c), lambda i, *_: (i, 0, 0))
    return pl.pallas_call(
        _moe_gather_kernel,
        out_shape=jax.ShapeDtypeStruct((n_rows, D), BF16),
        grid_spec=pltpu.PrefetchScalarGridSpec(
            num_scalar_prefetch=2, grid=(nb,),
            in_specs=[flat(MOE_TB, D), flat(8, MOE_TB), pl.BlockSpec(memory_space=pl.ANY)],
            out_specs=pl.BlockSpec(memory_space=pl.ANY),
            scratch_shapes=[pltpu.VMEM((MOE_R, D), BF16), pltpu.SemaphoreType.DMA((MOE_NP,))]),
        input_output_aliases={4: 0},
        compiler_params=_cparams(("arbitrary",)),
        name="moe_gather",
    )(dst, n_valid, h, info_t, jnp.zeros((n_rows, D), BF16))


def _moe_ffn_kernel(te_ref, src_ref, valid_ref, x_ref, wg_ref, wu_ref, wd_ref, y_ref):
    i = pl.program_id(0)

    @pl.when(valid_ref[i] == 1)
    def _():
        y_ref[...] = _swiglu_acc(x_ref[...], wg_ref.at[0], wu_ref.at[0], wd_ref.at[0]).astype(BF16)

    @pl.when(valid_ref[i] == 0)
    def _():
        y_ref[...] = jnp.zeros_like(y_ref)


def _moe_ffn(te, src, valid, xg, wg, wu, wd):
    n_rows = xg.shape[0]
    wspec = lambda r, c: pl.BlockSpec((1, r, c), lambda i, te, src, valid: (te[i], 0, 0))
    return pl.pallas_call(
        _moe_ffn_kernel,
        out_shape=jax.ShapeDtypeStruct((n_rows, D), BF16),
        grid_spec=pltpu.PrefetchScalarGridSpec(
            num_scalar_prefetch=3, grid=(n_rows // MOE_TM,),
            in_specs=[pl.BlockSpec((MOE_TM, D), lambda i, te, src, valid: (src[i], 0)),
                      wspec(D, D_FF), wspec(D, D_FF), wspec(D_FF, D)],
            out_specs=pl.BlockSpec((MOE_TM, D), lambda i, te, src, valid: (i, 0))),
        compiler_params=_cparams(("arbitrary",)),
        name="moe_ffn",
    )(te, src, valid, xg, wg, wu, wd)


def _moe_combine_kernel(dst_ref, nv_ref, x_ref, mod_ref, gpost_ref, info_ref, yg_ref, xo_ref, buf_ref, sem_ref):
    i = pl.program_id(0)
    nv = nv_ref[i]

    def piece(p):
        return _piece_copy(yg_ref, pl.multiple_of(dst_ref[i, p], MOE_SEG),
                           buf_ref, pl.multiple_of(p * MOE_SEG, MOE_SEG), sem_ref.at[p])

    def start(p, c):
        piece(p).start()
        return c

    def clear(p, c):
        buf_ref[pl.ds(pl.multiple_of(p * MOE_SEG, MOE_SEG), MOE_SEG), :] = jnp.zeros((MOE_SEG, D), BF16)
        return c

    def wait(p, c):
        piece(p).wait()
        return c

    lax.fori_loop(0, nv, start, 0)
    lax.fori_loop(nv, MOE_NP, clear, 0)
    info = info_ref[0]
    rr = lax.broadcasted_iota(jnp.int32, (MOE_TB, MOE_R), 1).astype(F32)
    comb = jnp.where(rr == info[:, 0:1], info[:, 2:3], jnp.where(rr == info[:, 1:2], info[:, 3:4], 0.0)).astype(BF16)
    lax.fori_loop(0, nv, wait, 0)
    fx = jnp.dot(comb, buf_ref[...], preferred_element_type=F32)
    xo_ref[0] = x_ref[0] + mod_ref[0, 0][5:6] * _rms(fx, gpost_ref[...])


def _moe_combine(dst, n_valid, xs, modtab, g_post, info, yg):
    B, S, _ = xs.shape
    per = S // MOE_TB
    blk = pl.BlockSpec((1, MOE_TB, D), lambda i, *_: (i // per, i % per, 0))
    return pl.pallas_call(
        _moe_combine_kernel,
        out_shape=jax.ShapeDtypeStruct((B, S, D), F32),
        grid_spec=pltpu.PrefetchScalarGridSpec(
            num_scalar_prefetch=2, grid=(B * per,),
            in_specs=[blk, pl.BlockSpec((1, 1, 6, D), lambda i, *_: (i // per, 1, 0, 0)),
                      pl.BlockSpec((1, D), lambda i, *_: (0, 0)),
                      pl.BlockSpec((1, MOE_TB, LANES), lambda i, *_: (i, 0, 0)),
                      pl.BlockSpec(memory_space=pl.ANY)],
            out_specs=blk,
            scratch_shapes=[pltpu.VMEM((MOE_R, D), BF16), pltpu.SemaphoreType.DMA((MOE_NP,))]),
        compiler_params=_cparams(("arbitrary",)),
        name="moe_combine",
    )(dst, n_valid, xs, modtab, g_post, info, yg)


def _moe(xs, modtab, g_pre, g_post, router, wg, wu, wd):
    B, S, _ = xs.shape
    nb = B * S // MOE_TB
    h, info, info_t, cnt = _moe_route(xs, modtab, g_pre, router)
    worst = 2 * B * S + nb * N_EXPERTS * (MOE_SEG - 1) + N_EXPERTS * (MOE_TM - 1)
    n_tiles = -(-worst // MOE_TM)
    dst, n_valid, te, src, valid = _moe_plan(cnt[:, 0, :N_EXPERTS].astype(jnp.int32), n_tiles)
    xg = _moe_gather(dst, n_valid, h, info_t, n_tiles * MOE_TM)
    yg = _moe_ffn(te, src, valid, xg, wg, wu, wd)
    return _moe_combine(dst, n_valid, xs, modtab, g_post, info, yg)


def _block_diag2(w):
    z = jnp.zeros_like(w[0])
    return jnp.concatenate([jnp.concatenate([w[0], z], axis=1), jnp.concatenate([z, w[1]], axis=1)], axis=0)


def _row(v):
    return v.reshape(1, -1).astype(F32)


def _head_ones(width, head):
    i = jnp.arange(width) // head
    return (i[:, None] == i[None, :]).astype(BF16)


def kernel(x, c, ctx, c_ctx, ada_w, ada_b, norm_mix_pre, norm_mix_post, norm_ffn_pre, norm_ffn_post, w_in, shift_mu, rw_w_up, rw_w0, rw_a_up, rw_a0, rw_k_k, rw_k_a, rw_r_k, rw_g_up, rw_gn_w, rw_gn_b, rw_v_down, rw_v_up, rw_v0, gla_conv, gla_a_up, gla_a_b, gla_gn_w, w_out, ffn_w_gate, ffn_w_up, ffn_w_down, moe_router, moe_w_gate, moe_w_up, moe_w_down):
    B, S, _ = x.shape
    n_ctx = ctx.shape[1]
    depth = w_in.shape[0]
    assert n_ctx == TB and S % MOE_TB == 0 and depth == 2

    xc = jnp.concatenate([ctx, x], axis=1)
    cvec = jnp.concatenate([c, c_ctx[None, :], jnp.zeros((16 - B - 1, D), F32)], axis=0)
    bd64 = _head_ones(RW_WIDTH, RW_HEAD)
    v_first = None
    out = None
    for i in range(depth):
        last = i == depth - 1
        mods = _adaln(cvec, ada_w[i], _row(ada_b[i]))
        mod_x = mods[:B].reshape(B, 6, D)
        mod_c = jnp.broadcast_to(mods[B].reshape(1, 6, D), (B, 6, D))
        modtab = jnp.stack([mod_c, mod_x], axis=1)

        w_i = jnp.concatenate([w_in[i], jnp.zeros((D, GLA_PAD - GLA_COLS), F32)], axis=1).astype(BF16)
        p_rw, p_gl = _inproj(xc, _row(norm_mix_pre[i]), modtab, w_i)

        prm = dict(
            mu=_row(shift_mu[i]),
            w_up=_block_diag2(rw_w_up[i]).astype(BF16), w0=_row(rw_w0[i]),
            a_up=_block_diag2(rw_a_up[i]).astype(BF16), a0=_row(rw_a0[i]),
            k_k=_row(rw_k_k[i]), k_a=_row(rw_k_a[i]), r_k=_row(rw_r_k[i]),
            g_up=rw_g_up[i].astype(BF16), bd64=bd64,
            gn_w=_row(rw_gn_w[i]), gn_b=_row(rw_gn_b[i]), gla_gn_w=_row(gla_gn_w[i]),
            w_out=w_out[i].astype(BF16), norm_post=_row(norm_mix_post[i]),
        )
        gate_pad = jnp.zeros((LANES - 2 * GLA_GATE_RANK, 2 * GLA_KW), F32)
        gla_prm = dict(conv=gla_conv[i].astype(F32), a_b=_row(gla_a_b[i]),
                       a_up=jnp.concatenate([_block_diag2(gla_a_up[i]), gate_pad], axis=0).astype(BF16))
        if i > 0:
            pad = LANES - RW_V_RANK
            prm["v_down"] = jnp.concatenate([rw_v_down[i - 1], jnp.zeros((RW_WIDTH, pad), F32)], axis=1).astype(BF16)
            prm["v_up"] = jnp.concatenate([rw_v_up[i - 1], jnp.zeros((pad, RW_WIDTH), F32)], axis=0).astype(BF16)
            prm["v0"] = _row(rw_v0[i - 1])

        r, kk, vm, g, bonus, lw, bb, ke = _rwprep(p_rw, prm, v_first if i > 0 else None)
        if i == 0:
            v_first = vm
        y = _rwscan(r, kk, vm, lw, bb, ke)
        q, k, gv, og, lg = _glaprep(p_gl, gla_prm)
        o = _glascan(q, k, gv, lg)
        xc = _readout(y, bonus, g, o, og, xc, modtab, prm, latents_only=last)

        jf = i // 2
        if i % 2 == 0:
            xc = _ffn(xc, modtab, _row(norm_ffn_pre[i]), _row(norm_ffn_post[i]),
                      ffn_w_gate[jf].astype(BF16), ffn_w_up[jf].astype(BF16), ffn_w_down[jf].astype(BF16))
        else:
            router = jnp.concatenate([moe_router[jf], jnp.zeros((D, LANES - N_EXPERTS), F32)], axis=1)
            out = _moe(xc, modtab, _row(norm_ffn_pre[i]), _row(norm_ffn_post[i]), router,
                       moe_w_gate[jf].astype(BF16), moe_w_up[jf].astype(BF16), moe_w_down[jf].astype(BF16))
    return out
```

```python
import functools
import math

import jax
import jax.numpy as jnp
from jax import lax
from jax.experimental import pallas as pl
from jax.experimental.pallas import tpu as pltpu

F32, BF16 = jnp.float32, jnp.bfloat16
ACT = BF16

D = 1024
GRID_W = 64
RW_WIDTH = 512
RW_HEAD = 64
RW_RANK = 64
RW_G_RANK = 128
RW_V_RANK = 32
RW_GN_EPS = 64e-5
GLA_WIDTH = 512
GLA_HEADS = 4
GLA_DV = 128
GLA_DK = 64
GLA_KW = 256
GLA_GATE_RANK = 16
GLA_GATE_NORM = 16.0
GLA_NORM_EPS = 1e-5
D_FF = 2816
N_EXPERTS = 8
NORM_EPS = 1e-6
RW_COLS = 3 * RW_WIDTH + 4 * RW_RANK + RW_G_RANK
GLA_QKV = 2 * GLA_KW + GLA_WIDTH
GLA_COLS = GLA_QKV + GLA_WIDTH + 2 * GLA_GATE_RANK
GLA_PAD = 1664

LANES = 128
TB = 256
CH = 64
FCH = 256
HALO = 16
VMEM_LIMIT = 56 * 1024 * 1024


def _cparams(sem):
    return pltpu.CompilerParams(dimension_semantics=sem, vmem_limit_bytes=VMEM_LIMIT)


def _bdot(a, b):
    return jnp.dot(a.astype(BF16), b.astype(BF16), preferred_element_type=F32)


def _bdot_nt(a, b):
    return lax.dot_general(a.astype(BF16), b.astype(BF16), (((1,), (1,)), ((), ())),
                           preferred_element_type=F32)


def _bdot_tn(a, b):
    return lax.dot_general(a.astype(BF16), b.astype(BF16), (((0,), (0,)), ((), ())),
                           preferred_element_type=F32)


def _split_dot(a_exact, x):
    h1 = x.astype(BF16)
    r1 = x - h1.astype(F32)
    h2 = r1.astype(BF16)
    h3 = (r1 - h2.astype(F32)).astype(BF16)
    return (jnp.dot(a_exact, h1, preferred_element_type=F32)
            + jnp.dot(a_exact, h2, preferred_element_type=F32)
            + jnp.dot(a_exact, h3, preferred_element_type=F32))


def _seg_sum(x, bd):
    hi = x.astype(BF16)
    lo = (x - hi.astype(F32)).astype(BF16)
    return jnp.dot(hi, bd, preferred_element_type=F32) + jnp.dot(lo, bd, preferred_element_type=F32)


def _sigmoid(x):
    return jax.nn.sigmoid(x)


def _rms(x, g):
    return x * lax.rsqrt(jnp.mean(x * x, axis=-1, keepdims=True) + NORM_EPS) * g


def _pair_stack(z, lo):
    return jnp.concatenate([jnp.where(lo, z, 0.0), jnp.where(lo, 0.0, z)], axis=0)


def _adaln_kernel(c_ref, w_ref, b_ref, o_ref):
    c = c_ref[...]
    s = c * _sigmoid(c)
    o_ref[...] = jnp.dot(s, w_ref[...], precision=lax.Precision.HIGHEST,
                         preferred_element_type=F32) + b_ref[...]


def _adaln(cvec, w, b):
    rows = cvec.shape[0]
    n = w.shape[1]
    return pl.pallas_call(
        _adaln_kernel,
        out_shape=jax.ShapeDtypeStruct((rows, n), F32),
        grid=(n // D,),
        in_specs=[pl.BlockSpec((rows, D), lambda i: (0, 0)),
                  pl.BlockSpec((D, D), lambda i: (0, i)),
                  pl.BlockSpec((1, D), lambda i: (0, i))],
        out_specs=pl.BlockSpec((rows, D), lambda i: (0, i)),
        compiler_params=_cparams(("arbitrary",)),
        name="adaln",
    )(cvec, w, b)


def _inproj_kernel(x_ref, g_ref, mod_ref, w_ref, prw_ref, pgl_ref):
    mod = mod_ref[0, 0]
    h = _rms(x_ref[0], g_ref[...]) * (1.0 + mod[1:2]) + mod[0:1]
    hb = h.astype(BF16)
    prw_ref[0] = jnp.dot(hb, w_ref[:, :RW_COLS], preferred_element_type=F32).astype(ACT)
    pgl_ref[0] = jnp.dot(hb, w_ref[:, RW_COLS:], preferred_element_type=F32).astype(ACT)


def _const_spec(shape):
    nd = len(shape)
    return pl.BlockSpec(shape, lambda *_: (0,) * nd)


def _tile_spec(c, first=0):
    return pl.BlockSpec((1, TB, c), lambda b, j: (b, j + first, 0))


def _dir_tile_spec(c):
    return pl.BlockSpec((2, 1, TB, c), lambda b, j: (0, b, j, 0))


def _mod_spec(first=0):
    return pl.BlockSpec((1, 1, 6, D), lambda b, j: (b, jnp.minimum(j + first, 1), 0, 0))


def _inproj(xc, g, modtab, w):
    B, N, _ = xc.shape
    return pl.pallas_call(
        _inproj_kernel,
        out_shape=[jax.ShapeDtypeStruct((B, N, RW_COLS), ACT),
                   jax.ShapeDtypeStruct((B, N, GLA_PAD), ACT)],
        grid=(B, N // TB),
        in_specs=[_tile_spec(D), _const_spec((1, D)), _mod_spec(),
                  _const_spec((D, RW_COLS + GLA_PAD))],
        out_specs=[_tile_spec(RW_COLS), _tile_spec(GLA_PAD)],
        compiler_params=_cparams(("parallel", "parallel")),
        name="inproj",
    )(xc, g, modtab, w)


def _rwprep_kernel(has_vres, n_tiles, p_ref, hu_ref, hd_ref, mu_ref, wup_ref, w0_ref, aup_ref, a0_ref,
                   kk_ref, ka_ref, rk_ref, gup_ref, bd_ref, *rest):
    if has_vres:
        vf_ref, vdn_ref, vup_ref, v0_ref = rest[:4]
        rest = rest[4:]
    r_out, kk_out, v_out, g_out, bon_out, lw_out, b_out, ke_out = rest
    j = pl.program_id(1)
    p = p_ref[0].astype(F32)
    row = lax.broadcasted_iota(jnp.int32, p.shape, 0)
    lane = lax.broadcasted_iota(jnp.int32, p.shape, 1)
    prev1 = pltpu.roll(p, 1, 0)
    next1 = pltpu.roll(p, TB - 1, 0)
    up = jnp.concatenate([hu_ref[0].astype(F32), p[:TB - GRID_W]], axis=0)
    down = jnp.concatenate([p[GRID_W:], hd_ref[0].astype(F32)], axis=0)
    col = row & (GRID_W - 1)
    c4 = lane & 3
    left = jnp.where(col == 0, 0.0, prev1)
    right = jnp.where(col == GRID_W - 1, 0.0, next1)
    upper = jnp.where(jnp.logical_and(j == 1, row < GRID_W), 0.0, up)
    lower = jnp.where(jnp.logical_and(j == n_tiles - 1, row >= TB - GRID_W), 0.0, down)
    sx = jnp.where(c4 == 0, left, jnp.where(c4 == 1, right, jnp.where(c4 == 2, upper, lower)))
    sc = jnp.where((lane & 1) == 0, jnp.where(row == 0, 0.0, prev1), jnp.where(row == TB - 1, 0.0, next1))
    shifted = jnp.where(j == 0, sc, sx)
    u = p + mu_ref[...] * (shifted - p)

    r = u[:, 0:RW_WIDTH]
    k = u[:, RW_WIDTH:2 * RW_WIDTH]
    v = u[:, 2 * RW_WIDTH:3 * RW_WIDTH]
    o = 3 * RW_WIDTH
    wd = u[:, o:o + 2 * RW_RANK]
    ad = u[:, o + 2 * RW_RANK:o + 4 * RW_RANK]
    gd = u[:, o + 4 * RW_RANK:]

    w_logit = w0_ref[...] + _bdot(jnp.tanh(wd), wup_ref[...])
    lw = -math.exp(-0.5) * _sigmoid(w_logit)
    a = _sigmoid(a0_ref[...] + _bdot(ad, aup_ref[...]))
    bd = bd_ref[...]
    kk = k * kk_ref[...]
    kk = kk / jnp.maximum(jnp.sqrt(_seg_sum(kk * kk, bd)), 1e-12)
    g = _bdot(_sigmoid(gd), gup_ref[...])
    if has_vres:
        gate = _sigmoid(v0_ref[...] + _bdot(_bdot(v, vdn_ref[...]), vup_ref[...]))
        vm = v + (vf_ref[0].astype(F32) - v) * gate
    else:
        vm = v
    ke_sum = jnp.zeros_like(k)
    for d in range(2):
        a_d = a[:, d * RW_WIDTH:(d + 1) * RW_WIDTH]
        ke_d = k * (1.0 + (a_d - 1.0) * ka_ref[...])
        lw_out[d, 0] = lw[:, d * RW_WIDTH:(d + 1) * RW_WIDTH]
        b_out[d, 0] = (kk * a_d).astype(ACT)
        ke_out[d, 0] = ke_d.astype(ACT)
        ke_sum = ke_sum + ke_d
    r_out[0] = r.astype(ACT)
    kk_out[0] = kk.astype(ACT)
    v_out[0] = vm.astype(ACT)
    g_out[0] = g.astype(ACT)
    bon_out[0] = (_seg_sum(r * ke_sum * rk_ref[...], bd) * vm).astype(ACT)


def _rwprep(p_rw, prm, v_first):
    B, N, _ = p_rw.shape
    nt = N // TB
    hb = TB // GRID_W
    nhb = N // GRID_W
    has_vres = v_first is not None
    W2 = 2 * RW_WIDTH
    in_specs = [
        _tile_spec(RW_COLS),
        pl.BlockSpec((1, GRID_W, RW_COLS), lambda b, j: (b, jnp.maximum(j * hb - 1, 0), 0)),
        pl.BlockSpec((1, GRID_W, RW_COLS), lambda b, j: (b, jnp.minimum(j * hb + hb, nhb - 1), 0)),
        _const_spec((1, RW_COLS)), _const_spec((2 * RW_RANK, W2)), _const_spec((1, W2)),
        _const_spec((2 * RW_RANK, W2)), _const_spec((1, W2)),
        _const_spec((1, RW_WIDTH)), _const_spec((1, RW_WIDTH)), _const_spec((1, RW_WIDTH)),
        _const_spec((RW_G_RANK, RW_WIDTH)), _const_spec((RW_WIDTH, RW_WIDTH)),
    ]
    args = [p_rw, p_rw, p_rw, prm["mu"], prm["w_up"], prm["w0"], prm["a_up"], prm["a0"],
            prm["k_k"], prm["k_a"], prm["r_k"], prm["g_up"], prm["bd64"]]
    if has_vres:
        in_specs += [_tile_spec(RW_WIDTH), _const_spec((RW_WIDTH, LANES)), _const_spec((LANES, RW_WIDTH)),
                     _const_spec((1, RW_WIDTH))]
        args += [v_first, prm["v_down"], prm["v_up"], prm["v0"]]
    tok = jax.ShapeDtypeStruct((B, N, RW_WIDTH), ACT)
    dtok = jax.ShapeDtypeStruct((2, B, N, RW_WIDTH), ACT)
    return pl.pallas_call(
        functools.partial(_rwprep_kernel, has_vres, nt),
        out_shape=[tok] * 5 + [jax.ShapeDtypeStruct((2, B, N, RW_WIDTH), F32), dtok, dtok],
        grid=(B, nt),
        in_specs=in_specs,
        out_specs=[_tile_spec(RW_WIDTH)] * 5 + [_dir_tile_spec(RW_WIDTH)] * 3,
        compiler_params=_cparams(("parallel", "parallel")),
        name="rwprep",
    )(*args)


NCH = TB // CH


def _chunk_tri(rev):
    t = lax.broadcasted_iota(jnp.int32, (TB, TB), 0)
    s = lax.broadcasted_iota(jnp.int32, (TB, TB), 1)
    order = (s >= t) if rev else (s <= t)
    return jnp.where(jnp.logical_and(t // CH == s // CH, order), 1.0, 0.0).astype(BF16)


def _chunk_totals(x):
    tots = [jnp.sum(x[c * CH:(c + 1) * CH], axis=0, keepdims=True) for c in range(NCH)]
    full = jnp.concatenate([jnp.broadcast_to(t, (CH, x.shape[1])) for t in tots], axis=0)
    return tots, full


def _scan_masks(rev):
    ti = lax.broadcasted_iota(jnp.int32, (CH, LANES), 0)
    li = lax.broadcasted_iota(jnp.int32, (CH, LANES), 1)
    si = li & (CH - 1)
    incl = (si >= ti) if rev else (si <= ti)
    strict = (si > ti) if rev else (si < ti)
    return incl, strict


def _rwscan_kernel(rf_ref, kkf_ref, vf_ref, lwf_ref, bf_ref, kef_ref,
                   rb_ref, kkb_ref, vb_ref, lwb_ref, bb_ref, keb_ref, yf_ref, yb_ref, st_ref):
    j = pl.program_id(1)

    @pl.when(j == 0)
    def _():
        st_ref[...] = jnp.zeros_like(st_ref)

    li = lax.broadcasted_iota(jnp.int32, (CH, LANES), 1)
    lo = li < CH
    eye = jnp.where(lax.broadcasted_iota(jnp.int32, (CH, LANES), 0) == (li & (CH - 1)), 1.0, 0.0)
    rblk = lax.broadcasted_iota(jnp.int32, (LANES, LANES), 0) // CH
    cblk = lax.broadcasted_iota(jnp.int32, (LANES, LANES), 1) // CH
    bdmask = rblk == cblk
    n_pairs = RW_WIDTH // LANES

    def stack(z):
        return _pair_stack(z, lo).astype(BF16)

    items = {}
    w_tots = {}
    for d, refs in enumerate(((rf_ref, kkf_ref, vf_ref, lwf_ref, bf_ref, kef_ref),
                              (rb_ref, kkb_ref, vb_ref, lwb_ref, bb_ref, keb_ref))):
        rev = d == 1
        r, kk, v = refs[0][0].astype(F32), refs[1][0].astype(F32), refs[2][0].astype(F32)
        lw, b, ke = refs[3][0, 0], refs[4][0, 0].astype(F32), refs[5][0, 0].astype(F32)
        cum = _split_dot(_chunk_tri(rev), lw)
        tots, tot_full = _chunk_totals(lw)
        w_inv = jnp.exp(-cum)
        w_end = jnp.exp(tot_full - cum)
        rh = r * jnp.exp(cum)
        ah = -(kk * jnp.exp(cum - lw))
        bh = b * w_inv
        kh = ke * w_inv
        bt = b * w_end
        kt = ke * w_end
        m_incl, m_strict = _scan_masks(rev)
        for c in range(NCH):
            rows = slice(c * CH, (c + 1) * CH)
            w_tots[d, c] = jnp.exp(tots[c])
            for p in range(n_pairs):
                sl = slice(LANES * p, LANES * (p + 1))
                items[d, c, p] = dict(ah=ah[rows, sl], rh=rh[rows, sl], bh=bh[rows, sl], kh=kh[rows, sl],
                                      bt=bt[rows, sl], kt=kt[rows, sl], v=v[rows, sl],
                                      m_incl=m_incl, m_strict=m_strict)

    for it in items.values():
        it["lhs"] = jnp.concatenate([it["ah"], it["rh"]], axis=0).astype(BF16)
        rhs = jnp.concatenate([stack(it["bh"]), stack(it["kh"])], axis=0)
        it["a_all"] = _bdot_nt(it["lhs"], rhs)
    for it in items.values():
        a_all = it.pop("a_all")
        it["a_ab"] = jnp.where(it["m_strict"], a_all[:CH, :LANES], 0.0)
        a_ak = jnp.where(it["m_strict"], a_all[:CH, LANES:], 0.0)
        a_rb = jnp.where(it["m_incl"], a_all[CH:, :LANES], 0.0)
        a_rk = jnp.where(it["m_incl"], a_all[CH:, LANES:], 0.0)
        it["a_r"] = jnp.concatenate([a_rb, a_rk], axis=1).astype(BF16)
        it["v_bd"] = stack(it["v"])
        it["akv"] = _bdot(a_ak, it["v_bd"])
        it["t"] = eye + it["a_ab"]
        it["m"] = _bdot(it["a_ab"], stack(it["a_ab"]))
    for _ in range(int(math.log2(CH)) - 2):
        for it in items.values():
            z = _bdot(jnp.concatenate([it["m"], it["t"]], axis=0), stack(it["m"]))
            it["m"] = z[:CH]
            it["t"] = it["t"] + z[CH:]
    for it in items.values():
        it["t"] = (it["t"] + _bdot(it["t"], stack(it["m"]))).astype(BF16)
    for it in items.values():
        a_til = jnp.dot(it["t"], stack(it["ah"]), preferred_element_type=F32)
        it["lhs2"] = jnp.concatenate([a_til.astype(BF16), it["lhs"][CH:]], axis=0)
        it["cc"] = jnp.dot(it["t"], stack(it["akv"]), preferred_element_type=F32)
        it["rhs_t"] = jnp.concatenate([it["bt"], it["kt"]], axis=0).astype(BF16)

    st = {(d, p): st_ref[d, p] for d in range(2) for p in range(n_pairs)}
    for step in range(NCH):
        cur = {(d, p): items[d, (NCH - 1 - step) if d == 1 else step, p] for d in range(2) for p in range(n_pairs)}
        z1 = {k: _bdot_nt(it["lhs2"], st[k]) for k, it in cur.items()}
        u = {k: z1[k][:CH] + it["cc"] for k, it in cur.items()}
        upd = {k: _bdot_tn(jnp.concatenate([u[k], it["v"]], axis=0), it["rhs_t"]) for k, it in cur.items()}
        for (d, p), it in cur.items():
            c = (NCH - 1 - step) if d == 1 else step
            sl = slice(LANES * p, LANES * (p + 1))
            y = z1[d, p][CH:] + jnp.dot(it["a_r"], jnp.concatenate([stack(u[d, p]), it["v_bd"]], axis=0),
                                        preferred_element_type=F32)
            (yb_ref if d == 1 else yf_ref)[0, c * CH:(c + 1) * CH, sl] = y.astype(ACT)
            st[d, p] = st[d, p] * w_tots[d, c][:, sl] + jnp.where(bdmask, upd[d, p], 0.0)
    for (d, p), s in st.items():
        st_ref[d, p] = s


def _bidir_specs(c, n_tiles):
    def back(j):
        return jnp.where(j == 0, 0, n_tiles - j)

    tok_f = pl.BlockSpec((1, TB, c), lambda b, j: (b, j, 0))
    tok_b = pl.BlockSpec((1, TB, c), lambda b, j: (b, back(j), 0))
    dir_f = pl.BlockSpec((1, 1, TB, c), lambda b, j: (0, b, j, 0))
    dir_b = pl.BlockSpec((1, 1, TB, c), lambda b, j: (1, b, back(j), 0))
    return tok_f, tok_b, dir_f, dir_b


def _rwscan(r, kk, v, lw, b, ke):
    B, N, _ = r.shape
    tok_f, tok_b, dir_f, dir_b = _bidir_specs(RW_WIDTH, N // TB)
    out = jax.ShapeDtypeStruct((B, N, RW_WIDTH), ACT)
    return pl.pallas_call(
        _rwscan_kernel,
        out_shape=[out, out],
        grid=(B, N // TB),
        in_specs=[tok_f, tok_f, tok_f, dir_f, dir_f, dir_f, tok_b, tok_b, tok_b, dir_b, dir_b, dir_b],
        out_specs=[tok_f, tok_b],
        scratch_shapes=[pltpu.VMEM((2, RW_WIDTH // LANES, LANES, LANES), F32)],
        compiler_params=_cparams(("parallel", "arbitrary")),
        name="rwscan",
    )(r, kk, v, lw, b, ke, r, kk, v, lw, b, ke)


def _glaprep_kernel(n_tiles, p_ref, hp_ref, hn_ref, cw_ref, aup_ref, ab_ref, q_out, k_out, v_out, og_out, lg_out):
    j = pl.program_id(1)
    u = p_ref[0][:, :GLA_QKV].astype(F32)
    row = lax.broadcasted_iota(jnp.int32, u.shape, 0)
    first = jnp.where(j <= 1, 0.0, 1.0) * hp_ref[0][:, :GLA_QKV].astype(F32)[HALO - 1:HALO]
    last = (jnp.where(jnp.logical_or(j == 0, j == n_tiles - 1), 0.0, 1.0)
            * hn_ref[0][:, :GLA_QKV].astype(F32)[0:1])
    prev1 = jnp.where(row == 0, first, pltpu.roll(u, 1, 0))
    next1 = jnp.where(row == TB - 1, last, pltpu.roll(u, TB - 1, 0))
    cw = cw_ref[...]
    conv = cw[0:1] * prev1 + cw[1:2] * u + cw[2:3] * next1
    qkv = conv * _sigmoid(conv)
    q_out[0] = (qkv[:, :GLA_KW] * (GLA_DK ** -0.5)).astype(ACT)
    k_out[0] = qkv[:, GLA_KW:2 * GLA_KW].astype(ACT)
    v_out[0] = qkv[:, 2 * GLA_KW:].astype(ACT)
    og_out[0] = p_ref[0][:, GLA_QKV:GLA_QKV + GLA_WIDTH]
    z = _bdot(p_ref[0][:, GLA_QKV + GLA_WIDTH:], aup_ref[...]) + ab_ref[...]
    lg = (jnp.minimum(z, 0.0) - jnp.log1p(jnp.exp(-jnp.abs(z)))) * (1.0 / GLA_GATE_NORM)
    for d in range(2):
        lg_out[d, 0] = lg[:, d * GLA_KW:(d + 1) * GLA_KW]


def _glaprep(p_gl, prm):
    B, N, _ = p_gl.shape
    nt = N // TB
    sub = HALO
    hb = TB // sub
    nhb = N // sub
    return pl.pallas_call(
        functools.partial(_glaprep_kernel, nt),
        out_shape=[jax.ShapeDtypeStruct((B, N, GLA_KW), ACT), jax.ShapeDtypeStruct((B, N, GLA_KW), ACT),
                   jax.ShapeDtypeStruct((B, N, GLA_WIDTH), ACT), jax.ShapeDtypeStruct((B, N, GLA_WIDTH), ACT),
                   jax.ShapeDtypeStruct((2, B, N, GLA_KW), F32)],
        grid=(B, nt),
        in_specs=[_tile_spec(GLA_PAD),
                  pl.BlockSpec((1, sub, GLA_PAD), lambda b, j: (b, jnp.maximum(j * hb - 1, 0), 0)),
                  pl.BlockSpec((1, sub, GLA_PAD), lambda b, j: (b, jnp.minimum(j * hb + hb, nhb - 1), 0)),
                  _const_spec((3, GLA_QKV)), _const_spec((LANES, 2 * GLA_KW)), _const_spec((1, 2 * GLA_KW))],
        out_specs=[_tile_spec(GLA_KW), _tile_spec(GLA_KW), _tile_spec(GLA_WIDTH), _tile_spec(GLA_WIDTH),
                   _dir_tile_spec(GLA_KW)],
        compiler_params=_cparams(("parallel", "parallel")),
        name="glaprep",
    )(p_gl, p_gl, p_gl, prm["conv"], prm["a_up"], prm["a_b"])


def _glascan_kernel(qf_ref, kf_ref, vf_ref, lgf_ref, qb_ref, kb_ref, vb_ref, lgb_ref, of_ref, ob_ref, st_ref):
    j = pl.program_id(1)

    @pl.when(j == 0)
    def _():
        st_ref[...] = jnp.zeros_like(st_ref)

    li = lax.broadcasted_iota(jnp.int32, (CH, LANES), 1)
    lo = li < CH
    lo2 = lax.broadcasted_iota(jnp.int32, (LANES, LANES), 1) < CH
    zeros_v = jnp.zeros((CH, GLA_DV), F32)
    n_pairs = GLA_KW // LANES

    items = {}
    decs = {}
    for d, refs in enumerate(((qf_ref, kf_ref, vf_ref, lgf_ref), (qb_ref, kb_ref, vb_ref, lgb_ref))):
        rev = d == 1
        q, k, v, lg = refs[0][0].astype(F32), refs[1][0].astype(F32), refs[2][0].astype(F32), refs[3][0, 0]
        cum = _split_dot(_chunk_tri(rev), lg)
        tots, tot_full = _chunk_totals(lg)
        qd = q * jnp.exp(cum)
        ki = k * jnp.exp(-cum)
        kend = k * jnp.exp(tot_full - cum)
        m_incl, _ = _scan_masks(rev)
        for c in range(NCH):
            rows = slice(c * CH, (c + 1) * CH)
            decs[d, c] = jnp.exp(tots[c])
            for p in range(n_pairs):
                sl = slice(LANES * p, LANES * (p + 1))
                items[d, c, p] = dict(qd=qd[rows, sl].astype(BF16), ki=ki[rows, sl], kend=kend[rows, sl],
                                      v0=v[rows, 2 * LANES * p:2 * LANES * p + LANES],
                                      v1=v[rows, 2 * LANES * p + LANES:2 * LANES * (p + 1)], m_incl=m_incl)

    for it in items.values():
        it["att"] = _bdot_nt(it["qd"], _pair_stack(it["ki"], lo))
    for it in items.values():
        att = jnp.where(it["m_incl"], it["att"], 0.0)
        v_bd = jnp.concatenate([jnp.concatenate([it["v0"], zeros_v], axis=1),
                                jnp.concatenate([zeros_v, it["v1"]], axis=1)], axis=0)
        it["o"] = _bdot(att, v_bd)
        it["upd"] = _bdot_tn(jnp.concatenate([it["v0"], it["v1"]], axis=0), _pair_stack(it["kend"], lo))

    for d in range(2):
        for p in range(n_pairs):
            st = st_ref[d, p]
            for step in range(NCH):
                c = (NCH - 1 - step) if d == 1 else step
                it = items[d, c, p]
                st_bd = jnp.concatenate([jnp.where(lo2, st, 0.0), jnp.where(lo2, 0.0, st)], axis=0)
                o = it["o"] + _bdot_nt(it["qd"], st_bd)
                (ob_ref if d == 1 else of_ref)[0, c * CH:(c + 1) * CH,
                                               2 * LANES * p:2 * LANES * (p + 1)] = o.astype(ACT)
                st = st * decs[d, c][:, LANES * p:LANES * (p + 1)] + it["upd"]
            st_ref[d, p] = st


def _glascan(q, k, v, lg):
    B, N, _ = q.shape
    kf, kb, dkf, dkb = _bidir_specs(GLA_KW, N // TB)
    vf, vb, _, _ = _bidir_specs(GLA_WIDTH, N // TB)
    out = jax.ShapeDtypeStruct((B, N, GLA_WIDTH), ACT)
    return pl.pallas_call(
        _glascan_kernel,
        out_shape=[out, out],
        grid=(B, N // TB),
        in_specs=[kf, kf, vf, dkf, kb, kb, vb, dkb],
        out_specs=[vf, vb],
        scratch_shapes=[pltpu.VMEM((2, GLA_KW // LANES, GLA_DV, LANES), F32)],
        compiler_params=_cparams(("parallel", "arbitrary")),
        name="glascan",
    )(q, k, v, lg, q, k, v, lg)


def _readout_kernel(yf_ref, yb_ref, bon_ref, g_ref, of_ref, ob_ref, og_ref, x_ref, mod_ref, gnw_ref, gnb_ref,
                    ggn_ref, wout_ref, gpost_ref, bd_ref, xo_ref):
    bd = bd_ref[...]
    y = yf_ref[0].astype(F32) + yb_ref[0].astype(F32)
    mu = _seg_sum(y, bd) * (1.0 / RW_HEAD)
    yc = y - mu
    var = _seg_sum(yc * yc, bd) * (1.0 / RW_HEAD)
    yn = yc * lax.rsqrt(var + RW_GN_EPS) * gnw_ref[...] + gnb_ref[...]
    rw = (yn + bon_ref[0].astype(F32)) * g_ref[0].astype(F32)
    o = of_ref[0].astype(F32) + ob_ref[0].astype(F32)
    og = og_ref[0].astype(F32)
    parts = [rw.astype(BF16)]
    for h in range(GLA_HEADS):
        sl = slice(GLA_DV * h, GLA_DV * (h + 1))
        oh = o[:, sl]
        on = oh * lax.rsqrt(jnp.mean(oh * oh, axis=-1, keepdims=True) + GLA_NORM_EPS)
        ogh = og[:, sl]
        parts.append((on * ggn_ref[:, sl] * (ogh * _sigmoid(ogh))).astype(BF16))
    cat = jnp.concatenate(parts, axis=1)
    mx = jnp.dot(cat, wout_ref[...], preferred_element_type=F32)
    xo_ref[0] = x_ref[0] + mod_ref[0, 0][2:3] * _rms(mx, gpost_ref[...])


def _readout(y, bonus, g, o, og, xc, modtab, prm, latents_only):
    B, N, _ = xc.shape
    f = 1 if latents_only else 0
    return pl.pallas_call(
        _readout_kernel,
        out_shape=jax.ShapeDtypeStruct((B, N - f * TB, D), F32),
        grid=(B, N // TB - f),
        in_specs=[_tile_spec(RW_WIDTH, f)] * 4 + [_tile_spec(GLA_WIDTH, f)] * 3 + [_tile_spec(D, f), _mod_spec(f),
                  _const_spec((1, RW_WIDTH)), _const_spec((1, RW_WIDTH)), _const_spec((1, GLA_WIDTH)),
                  _const_spec((D, D)), _const_spec((1, D)), _const_spec((RW_WIDTH, RW_WIDTH))],
        out_specs=_tile_spec(D),
        compiler_params=_cparams(("parallel", "parallel")),
        name="readout",
    )(y[0], y[1], bonus, g, o[0], o[1], og, xc, modtab, prm["gn_w"], prm["gn_b"], prm["gla_gn_w"], prm["w_out"],
      prm["norm_post"], prm["bd64"])


def _swiglu_acc(hb, wg_ref, wu_ref, wd_ref):
    acc = jnp.zeros((hb.shape[0], D), F32)
    for c in range(D_FF // FCH):
        sl = slice(c * FCH, (c + 1) * FCH)
        gate = jnp.dot(hb, wg_ref[:, sl], preferred_element_type=F32)
        up = jnp.dot(hb, wu_ref[:, sl], preferred_element_type=F32)
        act = (gate * _sigmoid(gate) * up).astype(BF16)
        acc = acc + jnp.dot(act, wd_ref[sl, :], preferred_element_type=F32)
    return acc


def _ffn_kernel(x_ref, mod_ref, gpre_ref, gpost_ref, wg_ref, wu_ref, wd_ref, xo_ref):
    mod = mod_ref[0, 0]
    x = x_ref[0]
    hb = (_rms(x, gpre_ref[...]) * (1.0 + mod[4:5]) + mod[3:4]).astype(BF16)
    fx = _swiglu_acc(hb, wg_ref, wu_ref, wd_ref)
    xo_ref[0] = x + mod[5:6] * _rms(fx, gpost_ref[...])


def _single_buffered(shape):
    nd = len(shape)
    return pl.BlockSpec(shape, lambda *_: (0,) * nd, pipeline_mode=pl.Buffered(1))


def _ffn(xc, modtab, g_pre, g_post, wg, wu, wd):
    B, N, _ = xc.shape
    return pl.pallas_call(
        _ffn_kernel,
        out_shape=jax.ShapeDtypeStruct((B, N, D), F32),
        grid=(B, N // TB),
        in_specs=[_tile_spec(D), _mod_spec(), _const_spec((1, D)), _const_spec((1, D)),
                  _single_buffered((D, D_FF)), _single_buffered((D, D_FF)), _single_buffered((D_FF, D))],
        out_specs=_tile_spec(D),
        compiler_params=_cparams(("parallel", "parallel")),
        name="ffn",
    )(xc, modtab, g_pre, g_post, wg, wu, wd)


MOE_TB = 1024
MOE_SEG = 32
MOE_TM = 512
MOE_R = 2 * MOE_TB + N_EXPERTS * MOE_SEG
MOE_NP = MOE_R // MOE_SEG


def _moe_route_kernel(x_ref, mod_ref, gpre_ref, router_ref, h_ref, info_ref, infot_ref, cnt_ref):
    mod = mod_ref[0, 0]
    h = _rms(x_ref[0], gpre_ref[...]) * (1.0 + mod[4:5]) + mod[3:4]
    h_ref[0] = h.astype(BF16)
    lane = lax.broadcasted_iota(jnp.int32, (MOE_TB, LANES), 1)
    logits = jnp.dot(h, router_ref[...], precision=lax.Precision.HIGHEST, preferred_element_type=F32)
    logits = jnp.where(lane < N_EXPERTS, logits, -jnp.inf)
    v1 = jnp.max(logits, axis=-1, keepdims=True)
    i1 = jnp.min(jnp.where(logits == v1, lane, LANES), axis=-1, keepdims=True)
    rest = jnp.where(lane == i1, -jnp.inf, logits)
    v2 = jnp.max(rest, axis=-1, keepdims=True)
    i2 = jnp.min(jnp.where(rest == v2, lane, LANES), axis=-1, keepdims=True)
    ex = jnp.exp(v2 - v1)
    w1 = 1.0 / (1.0 + ex)
    w2 = ex * w1
    e1 = jnp.where(lane == i1, 1.0, 0.0)
    e2 = jnp.where(lane == i2, 1.0, 0.0)
    es = e1 + e2
    t = lax.broadcasted_iota(jnp.int32, (MOE_TB, MOE_TB), 0)
    s = lax.broadcasted_iota(jnp.int32, (MOE_TB, MOE_TB), 1)
    before = jnp.where(s < t, 1.0, 0.0).astype(BF16)
    rank = jnp.dot(before, es.astype(BF16), preferred_element_type=F32)
    cnt = jnp.sum(es, axis=0, keepdims=True)
    segs = jnp.floor((cnt + (MOE_SEG - 1)) * (1.0 / MOE_SEG))
    ea = lax.broadcasted_iota(jnp.int32, (LANES, LANES), 0)
    eb = lax.broadcasted_iota(jnp.int32, (LANES, LANES), 1)
    earlier = jnp.where(ea < eb, 1.0, 0.0).astype(BF16)
    start = jnp.dot(jnp.broadcast_to(segs, (8, LANES)).astype(BF16), earlier,
                    preferred_element_type=F32)[0:1] * MOE_SEG
    pos = rank + start
    d1 = jnp.sum(e1 * pos, axis=-1, keepdims=True)
    d2 = jnp.sum(e2 * pos, axis=-1, keepdims=True)
    info = jnp.where(lane == 0, d1, jnp.where(lane == 1, d2, jnp.where(lane == 2, w1, jnp.where(lane == 3, w2, 0.0))))
    info_ref[0] = info
    infot_ref[0] = jnp.transpose(info)[0:8]
    cnt_ref[0] = jnp.broadcast_to(cnt, (8, LANES))


def _moe_route(xs, modtab, g_pre, router):
    B, S, _ = xs.shape
    per = S // MOE_TB
    nb = B * per
    blk = lambda c: pl.BlockSpec((1, MOE_TB, c), lambda i: (i // per, i % per, 0))
    flat = lambda r, c: pl.BlockSpec((1, r, c), lambda i: (i, 0, 0))
    return pl.pallas_call(
        _moe_route_kernel,
        out_shape=[jax.ShapeDtypeStruct((nb, MOE_TB, D), BF16), jax.ShapeDtypeStruct((nb, MOE_TB, LANES), F32),
                   jax.ShapeDtypeStruct((nb, 8, MOE_TB), F32), jax.ShapeDtypeStruct((nb, 8, LANES), F32)],
        grid=(nb,),
        in_specs=[blk(D), pl.BlockSpec((1, 1, 6, D), lambda i: (i // per, 1, 0, 0)),
                  _const_spec((1, D)), _const_spec((D, LANES))],
        out_specs=[flat(MOE_TB, D), flat(MOE_TB, LANES), flat(8, MOE_TB), flat(8, LANES)],
        compiler_params=_cparams(("parallel",)),
        name="moe_route",
    )(xs, modtab, g_pre, router)


def _moe_plan(cnt, n_tiles):
    pc = (cnt + MOE_SEG - 1) // MOE_SEG * MOE_SEG
    inc = jnp.cumsum(pc, axis=1)
    loff = inc - pc
    reg = (jnp.sum(pc, axis=0) + MOE_TM - 1) // MOE_TM * MOE_TM
    gend = jnp.cumsum(reg)
    goff = (gend - reg)[None, :] + jnp.cumsum(pc, axis=0) - pc
    rows = jnp.arange(MOE_NP, dtype=jnp.int32) * MOE_SEG
    e_p = jnp.sum((rows[None, :, None] >= inc[:, None, :]).astype(jnp.int32), axis=-1)
    e_c = jnp.minimum(e_p, N_EXPERTS - 1)
    dst = jnp.take_along_axis(goff, e_c, axis=1) + rows[None, :] - jnp.take_along_axis(loff, e_c, axis=1)
    dst = jnp.where(e_p < N_EXPERTS, dst, 0).astype(jnp.int32)
    n_valid = (inc[:, -1] // MOE_SEG).astype(jnp.int32)
    trow = jnp.arange(n_tiles, dtype=jnp.int32) * MOE_TM
    te = jnp.sum((trow[:, None] >= gend[None, :]).astype(jnp.int32), axis=-1)
    valid = te < N_EXPERTS
    last = gend[-1] // MOE_TM - 1
    te = jnp.where(valid, te, te[last]).astype(jnp.int32)
    src = jnp.where(valid, jnp.arange(n_tiles, dtype=jnp.int32), last).astype(jnp.int32)
    return dst, n_valid, te, src, valid.astype(jnp.int32)


def _piece_copy(src_ref, src_row, dst_ref, dst_row, sem):
    return pltpu.make_async_copy(src_ref.at[pl.ds(src_row, MOE_SEG)], dst_ref.at[pl.ds(dst_row, MOE_SEG)], sem)


def _moe_gather_kernel(dst_ref, nv_ref, h_ref, infot_ref, xg_in_ref, xg_ref, buf_ref, sem_ref):
    del xg_in_ref
    i = pl.program_id(0)
    it = infot_ref[0]
    rr = lax.broadcasted_iota(jnp.int32, (MOE_R, MOE_TB), 0).astype(F32)
    onehot = jnp.where(rr == it[0:1], 1.0, jnp.where(rr == it[1:2], 1.0, 0.0)).astype(BF16)
    buf_ref[...] = jnp.dot(onehot, h_ref[0], preferred_element_type=F32).astype(BF16)
    nv = nv_ref[i]

    def piece(p):
        return _piece_copy(buf_ref, pl.multiple_of(p * MOE_SEG, MOE_SEG),
                           xg_ref, pl.multiple_of(dst_ref[i, p], MOE_SEG), sem_ref.at[p])

    def start(p, c):
        piece(p).start()
        return c

    def wait(p, c):
        piece(p).wait()
        return c

    lax.fori_loop(0, nv, start, 0)
    lax.fori_loop(0, nv, wait, 0)


def _moe_gather(dst, n_valid, h, info_t, n_rows):
    nb = h.shape[0]
    flat = lambda r, c: pl.BlockSpec((1, r, c), lambda i, *_: (i, 0, 0))
    return pl.pallas_call(
        _moe_gather_kernel,
        out_shape=jax.ShapeDtypeStruct((n_rows, D), BF16),
        grid_spec=pltpu.PrefetchScalarGridSpec(
            num_scalar_prefetch=2, grid=(nb,),
            in_specs=[flat(MOE_TB, D), flat(8, MOE_TB), pl.BlockSpec(memory_space=pl.ANY)],
            out_specs=pl.BlockSpec(memory_space=pl.ANY),
            scratch_shapes=[pltpu.VMEM((MOE_R, D), BF16), pltpu.SemaphoreType.DMA((MOE_NP,))]),
        input_output_aliases={4: 0},
        compiler_params=_cparams(("arbitrary",)),
        name="moe_gather",
    )(dst, n_valid, h, info_t, jnp.zeros((n_rows, D), BF16))


def _moe_ffn_kernel(te_ref, src_ref, valid_ref, x_ref, wg_ref, wu_ref, wd_ref, y_ref):
    i = pl.program_id(0)

    @pl.when(valid_ref[i] == 1)
    def _():
        y_ref[...] = _swiglu_acc(x_ref[...], wg_ref.at[0], wu_ref.at[0], wd_ref.at[0]).astype(BF16)

    @pl.when(valid_ref[i] == 0)
    def _():
        y_ref[...] = jnp.zeros_like(y_ref)


def _moe_ffn(te, src, valid, xg, wg, wu, wd):
    n_rows = xg.shape[0]
    wspec = lambda r, c: pl.BlockSpec((1, r, c), lambda i, te, src, valid: (te[i], 0, 0))
    return pl.pallas_call(
        _moe_ffn_kernel,
        out_shape=jax.ShapeDtypeStruct((n_rows, D), BF16),
        grid_spec=pltpu.PrefetchScalarGridSpec(
            num_scalar_prefetch=3, grid=(n_rows // MOE_TM,),
            in_specs=[pl.BlockSpec((MOE_TM, D), lambda i, te, src, valid: (src[i], 0)),
                      wspec(D, D_FF), wspec(D, D_FF), wspec(D_FF, D)],
            out_specs=pl.BlockSpec((MOE_TM, D), lambda i, te, src, valid: (i, 0))),
        compiler_params=_cparams(("arbitrary",)),
        name="moe_ffn",
    )(te, src, valid, xg, wg, wu, wd)


def _moe_combine_kernel(dst_ref, nv_ref, x_ref, mod_ref, gpost_ref, info_ref, yg_ref, xo_ref, buf_ref, sem_ref):
    i = pl.program_id(0)
    nv = nv_ref[i]

    def piece(p):
        return _piece_copy(yg_ref, pl.multiple_of(dst_ref[i, p], MOE_SEG),
                           buf_ref, pl.multiple_of(p * MOE_SEG, MOE_SEG), sem_ref.at[p])

    def start(p, c):
        piece(p).start()
        return c

    def clear(p, c):
        buf_ref[pl.ds(pl.multiple_of(p * MOE_SEG, MOE_SEG), MOE_SEG), :] = jnp.zeros((MOE_SEG, D), BF16)
        return c

    def wait(p, c):
        piece(p).wait()
        return c

    lax.fori_loop(0, nv, start, 0)
    lax.fori_loop(nv, MOE_NP, clear, 0)
    info = info_ref[0]
    rr = lax.broadcasted_iota(jnp.int32, (MOE_TB, MOE_R), 1).astype(F32)
    comb = jnp.where(rr == info[:, 0:1], info[:, 2:3], jnp.where(rr == info[:, 1:2], info[:, 3:4], 0.0)).astype(BF16)
    lax.fori_loop(0, nv, wait, 0)
    fx = jnp.dot(comb, buf_ref[...], preferred_element_type=F32)
    xo_ref[0] = x_ref[0] + mod_ref[0, 0][5:6] * _rms(fx, gpost_ref[...])


def _moe_combine(dst, n_valid, xs, modtab, g_post, info, yg):
    B, S, _ = xs.shape
    per = S // MOE_TB
    blk = pl.BlockSpec((1, MOE_TB, D), lambda i, *_: (i // per, i % per, 0))
    return pl.pallas_call(
        _moe_combine_kernel,
        out_shape=jax.ShapeDtypeStruct((B, S, D), F32),
        grid_spec=pltpu.PrefetchScalarGridSpec(
            num_scalar_prefetch=2, grid=(B * per,),
            in_specs=[blk, pl.BlockSpec((1, 1, 6, D), lambda i, *_: (i // per, 1, 0, 0)),
                      pl.BlockSpec((1, D), lambda i, *_: (0, 0)),
                      pl.BlockSpec((1, MOE_TB, LANES), lambda i, *_: (i, 0, 0)),
                      pl.BlockSpec(memory_space=pl.ANY)],
            out_specs=blk,
            scratch_shapes=[pltpu.VMEM((MOE_R, D), BF16), pltpu.SemaphoreType.DMA((MOE_NP,))]),
        compiler_params=_cparams(("arbitrary",)),
        name="moe_combine",
    )(dst, n_valid, xs, modtab, g_post, info, yg)


def _moe(xs, modtab, g_pre, g_post, router, wg, wu, wd):
    B, S, _ = xs.shape
    nb = B * S // MOE_TB
    h, info, info_t, cnt = _moe_route(xs, modtab, g_pre, router)
    worst = 2 * B * S + nb * N_EXPERTS * (MOE_SEG - 1) + N_EXPERTS * (MOE_TM - 1)
    n_tiles = -(-worst // MOE_TM)
    dst, n_valid, te, src, valid = _moe_plan(cnt[:, 0, :N_EXPERTS].astype(jnp.int32), n_tiles)
    xg = _moe_gather(dst, n_valid, h, info_t, n_tiles * MOE_TM)
    yg = _moe_ffn(te, src, valid, xg, wg, wu, wd)
    return _moe_combine(dst, n_valid, xs, modtab, g_post, info, yg)


def _block_diag2(w):
    z = jnp.zeros_like(w[0])
    return jnp.concatenate([jnp.concatenate([w[0], z], axis=1), jnp.concatenate([z, w[1]], axis=1)], axis=0)


def _row(v):
    return v.reshape(1, -1).astype(F32)


def _head_ones(width, head):
    i = jnp.arange(width) // head
    return (i[:, None] == i[None, :]).astype(BF16)


def kernel(x, c, ctx, c_ctx, ada_w, ada_b, norm_mix_pre, norm_mix_post, norm_ffn_pre, norm_ffn_post, w_in, shift_mu, rw_w_up, rw_w0, rw_a_up, rw_a0, rw_k_k, rw_k_a, rw_r_k, rw_g_up, rw_gn_w, rw_gn_b, rw_v_down, rw_v_up, rw_v0, gla_conv, gla_a_up, gla_a_b, gla_gn_w, w_out, ffn_w_gate, ffn_w_up, ffn_w_down, moe_router, moe_w_gate, moe_w_up, moe_w_down):
    B, S, _ = x.shape
    n_ctx = ctx.shape[1]
    depth = w_in.shape[0]
    assert n_ctx == TB and S % MOE_TB == 0 and depth == 2

    xc = jnp.concatenate([ctx, x], axis=1)
    cvec = jnp.concatenate([c, c_ctx[None, :], jnp.zeros((16 - B - 1, D), F32)], axis=0)
    bd64 = _head_ones(RW_WIDTH, RW_HEAD)
    v_first = None
    out = None
    for i in range(depth):
        last = i == depth - 1
        mods = _adaln(cvec, ada_w[i], _row(ada_b[i]))
        mod_x = mods[:B].reshape(B, 6, D)
        mod_c = jnp.broadcast_to(mods[B].reshape(1, 6, D), (B, 6, D))
        modtab = jnp.stack([mod_c, mod_x], axis=1)

        w_i = jnp.concatenate([w_in[i], jnp.zeros((D, GLA_PAD - GLA_COLS), F32)], axis=1).astype(BF16)
        p_rw, p_gl = _inproj(xc, _row(norm_mix_pre[i]), modtab, w_i)

        prm = dict(
            mu=_row(shift_mu[i]),
            w_up=_block_diag2(rw_w_up[i]).astype(BF16), w0=_row(rw_w0[i]),
            a_up=_block_diag2(rw_a_up[i]).astype(BF16), a0=_row(rw_a0[i]),
            k_k=_row(rw_k_k[i]), k_a=_row(rw_k_a[i]), r_k=_row(rw_r_k[i]),
            g_up=rw_g_up[i].astype(BF16), bd64=bd64,
            gn_w=_row(rw_gn_w[i]), gn_b=_row(rw_gn_b[i]), gla_gn_w=_row(gla_gn_w[i]),
            w_out=w_out[i].astype(BF16), norm_post=_row(norm_mix_post[i]),
        )
        gate_pad = jnp.zeros((LANES - 2 * GLA_GATE_RANK, 2 * GLA_KW), F32)
        gla_prm = dict(conv=gla_conv[i].astype(F32), a_b=_row(gla_a_b[i]),
                       a_up=jnp.concatenate([_block_diag2(gla_a_up[i]), gate_pad], axis=0).astype(BF16))
        if i > 0:
            pad = LANES - RW_V_RANK
            prm["v_down"] = jnp.concatenate([rw_v_down[i - 1], jnp.zeros((RW_WIDTH, pad), F32)], axis=1).astype(BF16)
            prm["v_up"] = jnp.concatenate([rw_v_up[i - 1], jnp.zeros((pad, RW_WIDTH), F32)], axis=0).astype(BF16)
            prm["v0"] = _row(rw_v0[i - 1])

        r, kk, vm, g, bonus, lw, bb, ke = _rwprep(p_rw, prm, v_first if i > 0 else None)
        if i == 0:
            v_first = vm
        y = _rwscan(r, kk, vm, lw, bb, ke)
        q, k, gv, og, lg = _glaprep(p_gl, gla_prm)
        o = _glascan(q, k, gv, lg)
        xc = _readout(y, bonus, g, o, og, xc, modtab, prm, latents_only=last)

        jf = i // 2
        if i % 2 == 0:
            xc = _ffn(xc, modtab, _row(norm_ffn_pre[i]), _row(norm_ffn_post[i]),
                      ffn_w_gate[jf].astype(BF16), ffn_w_up[jf].astype(BF16), ffn_w_down[jf].astype(BF16))
        else:
            router = jnp.concatenate([moe_router[jf], jnp.zeros((D, LANES - N_EXPERTS), F32)], axis=1)
            out = _moe(xc, modtab, _row(norm_ffn_pre[i]), _row(norm_ffn_post[i]), router,
                       moe_w_gate[jf].astype(BF16), moe_w_up[jf].astype(BF16), moe_w_down[jf].astype(BF16))
    return out
```

```python
import functools
import math

import jax
import jax.numpy as jnp
from jax import lax
from jax.experimental import pallas as pl
from jax.experimental.pallas import tpu as pltpu

F32, BF16 = jnp.float32, jnp.bfloat16
ACT = BF16

D = 1024
GRID_W = 64
RW_WIDTH = 512
RW_HEAD = 64
RW_RANK = 64
RW_G_RANK = 128
RW_V_RANK = 32
RW_GN_EPS = 64e-5
GLA_WIDTH = 512
GLA_HEADS = 4
GLA_DV = 128
GLA_DK = 64
GLA_KW = 256
GLA_GATE_RANK = 16
GLA_GATE_NORM = 16.0
GLA_NORM_EPS = 1e-5
D_FF = 2816
N_EXPERTS = 8
NORM_EPS = 1e-6
RW_COLS = 3 * RW_WIDTH + 4 * RW_RANK + RW_G_RANK
GLA_QKV = 2 * GLA_KW + GLA_WIDTH
GLA_COLS = GLA_QKV + GLA_WIDTH + 2 * GLA_GATE_RANK
GLA_PAD = 1664

LANES = 128
TB = 256
TW = 3 * TB
CH = 64
FCH = 256
HALO = 16
VMEM_LIMIT = 56 * 1024 * 1024


def _cparams(sem):
    return pltpu.CompilerParams(dimension_semantics=sem, vmem_limit_bytes=VMEM_LIMIT)


def _bdot(a, b):
    return jnp.dot(a.astype(BF16), b.astype(BF16), preferred_element_type=F32)


def _bdot_nt(a, b):
    return lax.dot_general(a.astype(BF16), b.astype(BF16), (((1,), (1,)), ((), ())),
                           preferred_element_type=F32)


def _bdot_tn(a, b):
    return lax.dot_general(a.astype(BF16), b.astype(BF16), (((0,), (0,)), ((), ())),
                           preferred_element_type=F32)


def _split_dot(a_exact, x):
    h1 = x.astype(BF16)
    r1 = x - h1.astype(F32)
    h2 = r1.astype(BF16)
    h3 = (r1 - h2.astype(F32)).astype(BF16)
    return (jnp.dot(a_exact, h1, preferred_element_type=F32)
            + jnp.dot(a_exact, h2, preferred_element_type=F32)
            + jnp.dot(a_exact, h3, preferred_element_type=F32))


def _seg_sum(x, bd):
    hi = x.astype(BF16)
    lo = (x - hi.astype(F32)).astype(BF16)
    return jnp.dot(hi, bd, preferred_element_type=F32) + jnp.dot(lo, bd, preferred_element_type=F32)


def _sigmoid(x):
    return jax.nn.sigmoid(x)


def _rms(x, g):
    return x * lax.rsqrt(jnp.mean(x * x, axis=-1, keepdims=True) + NORM_EPS) * g


def _pair_stack(z, lo):
    return jnp.concatenate([jnp.where(lo, z, 0.0), jnp.where(lo, 0.0, z)], axis=0)


def _adaln_kernel(c_ref, w_ref, b_ref, o_ref):
    c = c_ref[...]
    s = c * _sigmoid(c)
    o_ref[...] = jnp.dot(s, w_ref[0], precision=lax.Precision.HIGHEST,
                         preferred_element_type=F32) + b_ref[0]


def _adaln(cvec, w, b, layer):
    rows = cvec.shape[0]
    n = w.shape[2]
    return pl.pallas_call(
        _adaln_kernel,
        out_shape=jax.ShapeDtypeStruct((rows, n), F32),
        grid=(n // D,),
        in_specs=[pl.BlockSpec((rows, D), lambda i: (0, 0)),
                  pl.BlockSpec((1, D, D), lambda i: (layer, 0, i)),
                  pl.BlockSpec((1, 1, D), lambda i: (layer, 0, i))],
        out_specs=pl.BlockSpec((rows, D), lambda i: (0, i)),
        compiler_params=_cparams(("arbitrary",)),
        name="adaln",
    )(cvec, w, b)


def _wide_mod(mod_ref, k):
    is_ctx = jnp.logical_and(pl.program_id(1) == 0, lax.broadcasted_iota(jnp.int32, (TW, 1), 0) < TB)
    return jnp.where(is_ctx, mod_ref[0, 0, k:k + 1], mod_ref[0, 1, k:k + 1])


def _wide_rows(refs):
    if len(refs) == 1:
        return refs[0][0]
    first = jnp.where(pl.program_id(1) == 0, refs[0][0], refs[1][0])
    return jnp.concatenate([first] + [r[0] for r in refs[2:]], axis=0)


def _inproj_kernel(n_src, *refs):
    g_ref, mod_ref, w_ref, prw_ref, pgl_ref = refs[n_src:]
    x = _wide_rows(refs[:n_src])
    h = _rms(x, g_ref[...]) * (1.0 + _wide_mod(mod_ref, 1)) + _wide_mod(mod_ref, 0)
    hb = h.astype(BF16)
    prw_ref[0] = jnp.dot(hb, w_ref[:, :RW_COLS], preferred_element_type=F32).astype(ACT)
    pgl_ref[0] = jnp.dot(hb, w_ref[:, RW_COLS:], preferred_element_type=F32).astype(ACT)


def _wide_src(ctx, x, xc):
    if xc is not None:
        return [xc], [pl.BlockSpec((1, TW, D), lambda b, j: (b, j, 0))]
    per = TW // TB
    specs = [pl.BlockSpec((1, TB, D), lambda b, j: (b, 0, 0))]
    for k in range(per):
        specs.append(pl.BlockSpec((1, TB, D), lambda b, j, k=k: (b, jnp.maximum(j * per + k - 1, 0), 0)))
    return [ctx] + [x] * per, specs


def _wide_mod_spec():
    return pl.BlockSpec((1, 2, 6, D), lambda b, j: (b, 0, 0, 0))


def _const_spec(shape):
    nd = len(shape)
    return pl.BlockSpec(shape, lambda *_: (0,) * nd)


def _tile_spec(c, first=0):
    return pl.BlockSpec((1, TB, c), lambda b, j: (b, j + first, 0))


def _dir_tile_spec(c):
    return pl.BlockSpec((2, 1, TB, c), lambda b, j: (0, b, j, 0))


def _mod_spec(first=0):
    return pl.BlockSpec((1, 1, 6, D), lambda b, j: (b, jnp.minimum(j + first, 1), 0, 0))


def _inproj(ctx, x, xc, g, modtab, w):
    B = modtab.shape[0]
    N = xc.shape[1] if xc is not None else ctx.shape[1] + x.shape[1]
    srcs, src_specs = _wide_src(ctx, x, xc)
    wide = lambda c: pl.BlockSpec((1, TW, c), lambda b, j: (b, j, 0))
    return pl.pallas_call(
        functools.partial(_inproj_kernel, len(srcs)),
        out_shape=[jax.ShapeDtypeStruct((B, N, RW_COLS), ACT),
                   jax.ShapeDtypeStruct((B, N, GLA_PAD), ACT)],
        grid=(B, N // TW),
        in_specs=src_specs + [_const_spec((1, D)), _wide_mod_spec(), _const_spec((D, RW_COLS + GLA_PAD))],
        out_specs=[wide(RW_COLS), wide(GLA_PAD)],
        compiler_params=_cparams(("parallel", "parallel")),
        name="inproj",
    )(*srcs, g, modtab, w)


def _rwprep_kernel(has_vres, n_tiles, p_ref, hu_ref, hd_ref, mu_ref, wup_ref, w0_ref, aup_ref, a0_ref,
                   kk_ref, ka_ref, rk_ref, gup_ref, bd_ref, *rest):
    if has_vres:
        vf_ref, vdn_ref, vup_ref, v0_ref = rest[:4]
        rest = rest[4:]
    r_out, kk_out, v_out, g_out, bon_out, lw_out, b_out, ke_out = rest
    j = pl.program_id(1)
    p = p_ref[0].astype(F32)
    row = lax.broadcasted_iota(jnp.int32, p.shape, 0)
    lane = lax.broadcasted_iota(jnp.int32, p.shape, 1)
    prev1 = pltpu.roll(p, 1, 0)
    next1 = pltpu.roll(p, TB - 1, 0)
    up = jnp.concatenate([hu_ref[0].astype(F32), p[:TB - GRID_W]], axis=0)
    down = jnp.concatenate([p[GRID_W:], hd_ref[0].astype(F32)], axis=0)
    col = row & (GRID_W - 1)
    c4 = lane & 3
    left = jnp.where(col == 0, 0.0, prev1)
    right = jnp.where(col == GRID_W - 1, 0.0, next1)
    upper = jnp.where(jnp.logical_and(j == 1, row < GRID_W), 0.0, up)
    lower = jnp.where(jnp.logical_and(j == n_tiles - 1, row >= TB - GRID_W), 0.0, down)
    sx = jnp.where(c4 == 0, left, jnp.where(c4 == 1, right, jnp.where(c4 == 2, upper, lower)))
    sc = jnp.where((lane & 1) == 0, jnp.where(row == 0, 0.0, prev1), jnp.where(row == TB - 1, 0.0, next1))
    shifted = jnp.where(j == 0, sc, sx)
    u = p + mu_ref[...] * (shifted - p)

    r = u[:, 0:RW_WIDTH]
    k = u[:, RW_WIDTH:2 * RW_WIDTH]
    v = u[:, 2 * RW_WIDTH:3 * RW_WIDTH]
    o = 3 * RW_WIDTH
    wd = u[:, o:o + 2 * RW_RANK]
    ad = u[:, o + 2 * RW_RANK:o + 4 * RW_RANK]
    gd = u[:, o + 4 * RW_RANK:]

    w_logit = w0_ref[...] + _bdot(jnp.tanh(wd), wup_ref[...])
    lw = -math.exp(-0.5) * _sigmoid(w_logit)
    a = _sigmoid(a0_ref[...] + _bdot(ad, aup_ref[...]))
    bd = bd_ref[...]
    kk = k * kk_ref[...]
    kk = kk / jnp.maximum(jnp.sqrt(_seg_sum(kk * kk, bd)), 1e-12)
    g = _bdot(_sigmoid(gd), gup_ref[...])
    if has_vres:
        gate = _sigmoid(v0_ref[...] + _bdot(_bdot(v, vdn_ref[...]), vup_ref[...]))
        vm = v + (vf_ref[0].astype(F32) - v) * gate
    else:
        vm = v
    ke_sum = jnp.zeros_like(k)
    for d in range(2):
        a_d = a[:, d * RW_WIDTH:(d + 1) * RW_WIDTH]
        ke_d = k * (1.0 + (a_d - 1.0) * ka_ref[...])
        lw_out[d, 0] = lw[:, d * RW_WIDTH:(d + 1) * RW_WIDTH]
        b_out[d, 0] = (kk * a_d).astype(ACT)
        ke_out[d, 0] = ke_d.astype(ACT)
        ke_sum = ke_sum + ke_d
    r_out[0] = r.astype(ACT)
    kk_out[0] = kk.astype(ACT)
    v_out[0] = vm.astype(ACT)
    g_out[0] = g.astype(ACT)
    bon_out[0] = (_seg_sum(r * ke_sum * rk_ref[...], bd) * vm).astype(ACT)


def _rwprep(p_rw, prm, v_first):
    B, N, _ = p_rw.shape
    nt = N // TB
    hb = TB // GRID_W
    nhb = N // GRID_W
    has_vres = v_first is not None
    W2 = 2 * RW_WIDTH
    in_specs = [
        _tile_spec(RW_COLS),
        pl.BlockSpec((1, GRID_W, RW_COLS), lambda b, j: (b, jnp.maximum(j * hb - 1, 0), 0)),
        pl.BlockSpec((1, GRID_W, RW_COLS), lambda b, j: (b, jnp.minimum(j * hb + hb, nhb - 1), 0)),
        _const_spec((1, RW_COLS)), _const_spec((2 * RW_RANK, W2)), _const_spec((1, W2)),
        _const_spec((2 * RW_RANK, W2)), _const_spec((1, W2)),
        _const_spec((1, RW_WIDTH)), _const_spec((1, RW_WIDTH)), _const_spec((1, RW_WIDTH)),
        _const_spec((RW_G_RANK, RW_WIDTH)), _const_spec((RW_WIDTH, RW_WIDTH)),
    ]
    args = [p_rw, p_rw, p_rw, prm["mu"], prm["w_up"], prm["w0"], prm["a_up"], prm["a0"],
            prm["k_k"], prm["k_a"], prm["r_k"], prm["g_up"], prm["bd64"]]
    if has_vres:
        in_specs += [_tile_spec(RW_WIDTH), _const_spec((RW_WIDTH, LANES)), _const_spec((LANES, RW_WIDTH)),
                     _const_spec((1, RW_WIDTH))]
        args += [v_first, prm["v_down"], prm["v_up"], prm["v0"]]
    tok = jax.ShapeDtypeStruct((B, N, RW_WIDTH), ACT)
    dtok = jax.ShapeDtypeStruct((2, B, N, RW_WIDTH), ACT)
    return pl.pallas_call(
        functools.partial(_rwprep_kernel, has_vres, nt),
        out_shape=[tok] * 5 + [jax.ShapeDtypeStruct((2, B, N, RW_WIDTH), F32), dtok, dtok],
        grid=(B, nt),
        in_specs=in_specs,
        out_specs=[_tile_spec(RW_WIDTH)] * 5 + [_dir_tile_spec(RW_WIDTH)] * 3,
        compiler_params=_cparams(("parallel", "parallel")),
        name="rwprep",
    )(*args)


NCH = TB // CH


def _chunk_tri(rev):
    t = lax.broadcasted_iota(jnp.int32, (TB, TB), 0)
    s = lax.broadcasted_iota(jnp.int32, (TB, TB), 1)
    order = (s >= t) if rev else (s <= t)
    return jnp.where(jnp.logical_and(t // CH == s // CH, order), 1.0, 0.0).astype(BF16)


def _chunk_totals(x):
    tots = [jnp.sum(x[c * CH:(c + 1) * CH], axis=0, keepdims=True) for c in range(NCH)]
    full = jnp.concatenate([jnp.broadcast_to(t, (CH, x.shape[1])) for t in tots], axis=0)
    return tots, full


def _scan_masks(rev):
    ti = lax.broadcasted_iota(jnp.int32, (CH, LANES), 0)
    li = lax.broadcasted_iota(jnp.int32, (CH, LANES), 1)
    si = li & (CH - 1)
    incl = (si >= ti) if rev else (si <= ti)
    strict = (si > ti) if rev else (si < ti)
    return incl, strict


def _rwscan_kernel(rf_ref, kkf_ref, vf_ref, lwf_ref, bf_ref, kef_ref,
                   rb_ref, kkb_ref, vb_ref, lwb_ref, bb_ref, keb_ref, yf_ref, yb_ref, st_ref):
    j = pl.program_id(1)

    @pl.when(j == 0)
    def _():
        st_ref[...] = jnp.zeros_like(st_ref)

    li = lax.broadcasted_iota(jnp.int32, (CH, LANES), 1)
    lo = li < CH
    eye = jnp.where(lax.broadcasted_iota(jnp.int32, (CH, LANES), 0) == (li & (CH - 1)), 1.0, 0.0)
    rblk = lax.broadcasted_iota(jnp.int32, (LANES, LANES), 0) // CH
    cblk = lax.broadcasted_iota(jnp.int32, (LANES, LANES), 1) // CH
    bdmask = rblk == cblk
    n_pairs = RW_WIDTH // LANES

    def stack(z):
        return _pair_stack(z, lo).astype(BF16)

    items = {}
    w_tots = {}
    for d, refs in enumerate(((rf_ref, kkf_ref, vf_ref, lwf_ref, bf_ref, kef_ref),
                              (rb_ref, kkb_ref, vb_ref, lwb_ref, bb_ref, keb_ref))):
        rev = d == 1
        r, kk, v = refs[0][0].astype(F32), refs[1][0].astype(F32), refs[2][0].astype(F32)
        lw, b, ke = refs[3][0, 0], refs[4][0, 0].astype(F32), refs[5][0, 0].astype(F32)
        cum = _split_dot(_chunk_tri(rev), lw)
        tots, tot_full = _chunk_totals(lw)
        w_inv = jnp.exp(-cum)
        w_end = jnp.exp(tot_full - cum)
        rh = r * jnp.exp(cum)
        ah = -(kk * jnp.exp(cum - lw))
        bh = b * w_inv
        kh = ke * w_inv
        bt = b * w_end
        kt = ke * w_end
        m_incl, m_strict = _scan_masks(rev)
        for c in range(NCH):
            rows = slice(c * CH, (c + 1) * CH)
            w_tots[d, c] = jnp.exp(tots[c])
            for p in range(n_pairs):
                sl = slice(LANES * p, LANES * (p + 1))
                items[d, c, p] = dict(ah=ah[rows, sl], rh=rh[rows, sl], bh=bh[rows, sl], kh=kh[rows, sl],
                                      bt=bt[rows, sl], kt=kt[rows, sl], v=v[rows, sl],
                                      m_incl=m_incl, m_strict=m_strict)

    for it in items.values():
        it["lhs"] = jnp.concatenate([it["ah"], it["rh"]], axis=0).astype(BF16)
        rhs = jnp.concatenate([stack(it["bh"]), stack(it["kh"])], axis=0)
        it["a_all"] = _bdot_nt(it["lhs"], rhs)
    for it in items.values():
        a_all = it.pop("a_all")
        it["a_ab"] = jnp.where(it["m_strict"], a_all[:CH, :LANES], 0.0)
        a_ak = jnp.where(it["m_strict"], a_all[:CH, LANES:], 0.0)
        a_rb = jnp.where(it["m_incl"], a_all[CH:, :LANES], 0.0)
        a_rk = jnp.where(it["m_incl"], a_all[CH:, LANES:], 0.0)
        it["a_r"] = jnp.concatenate([a_rb, a_rk], axis=1).astype(BF16)
        it["v_bd"] = stack(it["v"])
        it["akv"] = _bdot(a_ak, it["v_bd"])
        it["t"] = eye + it["a_ab"]
        it["m"] = _bdot(it["a_ab"], stack(it["a_ab"]))
    for _ in range(int(math.log2(CH)) - 2):
        for it in items.values():
            z = _bdot(jnp.concatenate([it["m"], it["t"]], axis=0), stack(it["m"]))
            it["m"] = z[:CH]
            it["t"] = it["t"] + z[CH:]
    for it in items.values():
        it["t"] = (it["t"] + _bdot(it["t"], stack(it["m"]))).astype(BF16)
    for it in items.values():
        a_til = jnp.dot(it["t"], stack(it["ah"]), preferred_element_type=F32)
        it["lhs2"] = jnp.concatenate([a_til.astype(BF16), it["lhs"][CH:]], axis=0)
        it["cc"] = jnp.dot(it["t"], stack(it["akv"]), preferred_element_type=F32)
        it["rhs_t"] = jnp.concatenate([it["bt"], it["kt"]], axis=0).astype(BF16)

    st = {(d, p): st_ref[d, p] for d in range(2) for p in range(n_pairs)}
    for step in range(NCH):
        cur = {(d, p): items[d, (NCH - 1 - step) if d == 1 else step, p] for d in range(2) for p in range(n_pairs)}
        z1 = {k: _bdot_nt(it["lhs2"], st[k]) for k, it in cur.items()}
        u = {k: z1[k][:CH] + it["cc"] for k, it in cur.items()}
        upd = {k: _bdot_tn(jnp.concatenate([u[k], it["v"]], axis=0), it["rhs_t"]) for k, it in cur.items()}
        for (d, p), it in cur.items():
            c = (NCH - 1 - step) if d == 1 else step
            sl = slice(LANES * p, LANES * (p + 1))
            y = z1[d, p][CH:] + jnp.dot(it["a_r"], jnp.concatenate([stack(u[d, p]), it["v_bd"]], axis=0),
                                        preferred_element_type=F32)
            (yb_ref if d == 1 else yf_ref)[0, c * CH:(c + 1) * CH, sl] = y.astype(ACT)
            st[d, p] = st[d, p] * w_tots[d, c][:, sl] + jnp.where(bdmask, upd[d, p], 0.0)
    for (d, p), s in st.items():
        st_ref[d, p] = s


def _bidir_specs(c, n_tiles):
    def back(j):
        return jnp.where(j == 0, 0, n_tiles - j)

    tok_f = pl.BlockSpec((1, TB, c), lambda b, j: (b, j, 0))
    tok_b = pl.BlockSpec((1, TB, c), lambda b, j: (b, back(j), 0))
    dir_f = pl.BlockSpec((1, 1, TB, c), lambda b, j: (0, b, j, 0))
    dir_b = pl.BlockSpec((1, 1, TB, c), lambda b, j: (1, b, back(j), 0))
    return tok_f, tok_b, dir_f, dir_b


def _rwscan(r, kk, v, lw, b, ke):
    B, N, _ = r.shape
    tok_f, tok_b, dir_f, dir_b = _bidir_specs(RW_WIDTH, N // TB)
    out = jax.ShapeDtypeStruct((B, N, RW_WIDTH), ACT)
    return pl.pallas_call(
        _rwscan_kernel,
        out_shape=[out, out],
        grid=(B, N // TB),
        in_specs=[tok_f, tok_f, tok_f, dir_f, dir_f, dir_f, tok_b, tok_b, tok_b, dir_b, dir_b, dir_b],
        out_specs=[tok_f, tok_b],
        scratch_shapes=[pltpu.VMEM((2, RW_WIDTH // LANES, LANES, LANES), F32)],
        compiler_params=_cparams(("parallel", "arbitrary")),
        name="rwscan",
    )(r, kk, v, lw, b, ke, r, kk, v, lw, b, ke)


def _glaprep_kernel(n_tiles, p_ref, hp_ref, hn_ref, cw_ref, aup_ref, ab_ref, q_out, k_out, v_out, og_out, lg_out):
    j = pl.program_id(1)
    u = p_ref[0][:, :GLA_QKV].astype(F32)
    row = lax.broadcasted_iota(jnp.int32, u.shape, 0)
    first = jnp.where(j <= 1, 0.0, 1.0) * hp_ref[0][:, :GLA_QKV].astype(F32)[HALO - 1:HALO]
    last = (jnp.where(jnp.logical_or(j == 0, j == n_tiles - 1), 0.0, 1.0)
            * hn_ref[0][:, :GLA_QKV].astype(F32)[0:1])
    prev1 = jnp.where(row == 0, first, pltpu.roll(u, 1, 0))
    next1 = jnp.where(row == TB - 1, last, pltpu.roll(u, TB - 1, 0))
    cw = cw_ref[...]
    conv = cw[0:1] * prev1 + cw[1:2] * u + cw[2:3] * next1
    qkv = conv * _sigmoid(conv)
    q_out[0] = (qkv[:, :GLA_KW] * (GLA_DK ** -0.5)).astype(ACT)
    k_out[0] = qkv[:, GLA_KW:2 * GLA_KW].astype(ACT)
    v_out[0] = qkv[:, 2 * GLA_KW:].astype(ACT)
    og_out[0] = p_ref[0][:, GLA_QKV:GLA_QKV + GLA_WIDTH]
    z = _bdot(p_ref[0][:, GLA_QKV + GLA_WIDTH:], aup_ref[...]) + ab_ref[...]
    lg = (jnp.minimum(z, 0.0) - jnp.log1p(jnp.exp(-jnp.abs(z)))) * (1.0 / GLA_GATE_NORM)
    for d in range(2):
        lg_out[d, 0] = lg[:, d * GLA_KW:(d + 1) * GLA_KW]


def _glaprep(p_gl, prm):
    B, N, _ = p_gl.shape
    nt = N // TB
    sub = HALO
    hb = TB // sub
    nhb = N // sub
    return pl.pallas_call(
        functools.partial(_glaprep_kernel, nt),
        out_shape=[jax.ShapeDtypeStruct((B, N, GLA_KW), ACT), jax.ShapeDtypeStruct((B, N, GLA_KW), ACT),
                   jax.ShapeDtypeStruct((B, N, GLA_WIDTH), ACT), jax.ShapeDtypeStruct((B, N, GLA_WIDTH), ACT),
                   jax.ShapeDtypeStruct((2, B, N, GLA_KW), F32)],
        grid=(B, nt),
        in_specs=[_tile_spec(GLA_PAD),
                  pl.BlockSpec((1, sub, GLA_PAD), lambda b, j: (b, jnp.maximum(j * hb - 1, 0), 0)),
                  pl.BlockSpec((1, sub, GLA_PAD), lambda b, j: (b, jnp.minimum(j * hb + hb, nhb - 1), 0)),
                  _const_spec((3, GLA_QKV)), _const_spec((LANES, 2 * GLA_KW)), _const_spec((1, 2 * GLA_KW))],
        out_specs=[_tile_spec(GLA_KW), _tile_spec(GLA_KW), _tile_spec(GLA_WIDTH), _tile_spec(GLA_WIDTH),
                   _dir_tile_spec(GLA_KW)],
        compiler_params=_cparams(("parallel", "parallel")),
        name="glaprep",
    )(p_gl, p_gl, p_gl, prm["conv"], prm["a_up"], prm["a_b"])


def _glascan_kernel(qf_ref, kf_ref, vf_ref, lgf_ref, qb_ref, kb_ref, vb_ref, lgb_ref, of_ref, ob_ref, st_ref):
    j = pl.program_id(1)

    @pl.when(j == 0)
    def _():
        st_ref[...] = jnp.zeros_like(st_ref)

    li = lax.broadcasted_iota(jnp.int32, (CH, LANES), 1)
    lo = li < CH
    lo2 = lax.broadcasted_iota(jnp.int32, (LANES, LANES), 1) < CH
    zeros_v = jnp.zeros((CH, GLA_DV), F32)
    n_pairs = GLA_KW // LANES

    items = {}
    decs = {}
    for d, refs in enumerate(((qf_ref, kf_ref, vf_ref, lgf_ref), (qb_ref, kb_ref, vb_ref, lgb_ref))):
        rev = d == 1
        q, k, v, lg = refs[0][0].astype(F32), refs[1][0].astype(F32), refs[2][0].astype(F32), refs[3][0, 0]
        cum = _split_dot(_chunk_tri(rev), lg)
        tots, tot_full = _chunk_totals(lg)
        qd = q * jnp.exp(cum)
        ki = k * jnp.exp(-cum)
        kend = k * jnp.exp(tot_full - cum)
        m_incl, _ = _scan_masks(rev)
        for c in range(NCH):
            rows = slice(c * CH, (c + 1) * CH)
            decs[d, c] = jnp.exp(tots[c])
            for p in range(n_pairs):
                sl = slice(LANES * p, LANES * (p + 1))
                items[d, c, p] = dict(qd=qd[rows, sl].astype(BF16), ki=ki[rows, sl], kend=kend[rows, sl],
                                      v0=v[rows, 2 * LANES * p:2 * LANES * p + LANES],
                                      v1=v[rows, 2 * LANES * p + LANES:2 * LANES * (p + 1)], m_incl=m_incl)

    for it in items.values():
        it["att"] = _bdot_nt(it["qd"], _pair_stack(it["ki"], lo))
    for it in items.values():
        att = jnp.where(it["m_incl"], it["att"], 0.0)
        v_bd = jnp.concatenate([jnp.concatenate([it["v0"], zeros_v], axis=1),
                                jnp.concatenate([zeros_v, it["v1"]], axis=1)], axis=0)
        it["o"] = _bdot(att, v_bd)
        it["upd"] = _bdot_tn(jnp.concatenate([it["v0"], it["v1"]], axis=0), _pair_stack(it["kend"], lo))

    for d in range(2):
        for p in range(n_pairs):
            st = st_ref[d, p]
            for step in range(NCH):
                c = (NCH - 1 - step) if d == 1 else step
                it = items[d, c, p]
                st_bd = jnp.concatenate([jnp.where(lo2, st, 0.0), jnp.where(lo2, 0.0, st)], axis=0)
                o = it["o"] + _bdot_nt(it["qd"], st_bd)
                (ob_ref if d == 1 else of_ref)[0, c * CH:(c + 1) * CH,
                                               2 * LANES * p:2 * LANES * (p + 1)] = o.astype(ACT)
                st = st * decs[d, c][:, LANES * p:LANES * (p + 1)] + it["upd"]
            st_ref[d, p] = st


def _glascan(q, k, v, lg):
    B, N, _ = q.shape
    kf, kb, dkf, dkb = _bidir_specs(GLA_KW, N // TB)
    vf, vb, _, _ = _bidir_specs(GLA_WIDTH, N // TB)
    out = jax.ShapeDtypeStruct((B, N, GLA_WIDTH), ACT)
    return pl.pallas_call(
        _glascan_kernel,
        out_shape=[out, out],
        grid=(B, N // TB),
        in_specs=[kf, kf, vf, dkf, kb, kb, vb, dkb],
        out_specs=[vf, vb],
        scratch_shapes=[pltpu.VMEM((2, GLA_KW // LANES, GLA_DV, LANES), F32)],
        compiler_params=_cparams(("parallel", "arbitrary")),
        name="glascan",
    )(q, k, v, lg, q, k, v, lg)


def _readout_kernel(split_src, yf_ref, yb_ref, bon_ref, g_ref, of_ref, ob_ref, og_ref, *refs):
    if split_src:
        x_res = jnp.where(pl.program_id(1) == 0, refs[0][0], refs[1][0])
        refs = refs[2:]
    else:
        x_res = refs[0][0]
        refs = refs[1:]
    mod_ref, gnw_ref, gnb_ref, ggn_ref, wout_ref, gpost_ref, bd_ref, xo_ref = refs
    bd = bd_ref[...]
    y = yf_ref[0].astype(F32) + yb_ref[0].astype(F32)
    mu = _seg_sum(y, bd) * (1.0 / RW_HEAD)
    yc = y - mu
    var = _seg_sum(yc * yc, bd) * (1.0 / RW_HEAD)
    yn = yc * lax.rsqrt(var + RW_GN_EPS) * gnw_ref[...] + gnb_ref[...]
    rw = (yn + bon_ref[0].astype(F32)) * g_ref[0].astype(F32)
    o = of_ref[0].astype(F32) + ob_ref[0].astype(F32)
    og = og_ref[0].astype(F32)
    parts = [rw.astype(BF16)]
    for h in range(GLA_HEADS):
        sl = slice(GLA_DV * h, GLA_DV * (h + 1))
        oh = o[:, sl]
        on = oh * lax.rsqrt(jnp.mean(oh * oh, axis=-1, keepdims=True) + GLA_NORM_EPS)
        ogh = og[:, sl]
        parts.append((on * ggn_ref[:, sl] * (ogh * _sigmoid(ogh))).astype(BF16))
    cat = jnp.concatenate(parts, axis=1)
    mx = jnp.dot(cat, wout_ref[...], preferred_element_type=F32)
    xo_ref[0] = x_res + mod_ref[0, 0][2:3] * _rms(mx, gpost_ref[...])


def _readout(y, bonus, g, o, og, ctx, x, xc, modtab, prm, latents_only):
    B, N, _ = bonus.shape
    f = 1 if latents_only else 0
    if xc is None:
        res = [ctx, x]
        res_specs = [pl.BlockSpec((1, TB, D), lambda b, j: (b, 0, 0)),
                     pl.BlockSpec((1, TB, D), lambda b, j: (b, jnp.maximum(j + f - 1, 0), 0))]
    else:
        res, res_specs = [xc], [_tile_spec(D, f)]
    return pl.pallas_call(
        functools.partial(_readout_kernel, xc is None),
        out_shape=jax.ShapeDtypeStruct((B, N - f * TB, D), F32),
        grid=(B, N // TB - f),
        in_specs=[_tile_spec(RW_WIDTH, f)] * 4 + [_tile_spec(GLA_WIDTH, f)] * 3 + res_specs + [_mod_spec(f),
                  _const_spec((1, RW_WIDTH)), _const_spec((1, RW_WIDTH)), _const_spec((1, GLA_WIDTH)),
                  _const_spec((D, D)), _const_spec((1, D)), _const_spec((RW_WIDTH, RW_WIDTH))],
        out_specs=_tile_spec(D),
        compiler_params=_cparams(("parallel", "parallel")),
        name="readout",
    )(y[0], y[1], bonus, g, o[0], o[1], og, *res, modtab, prm["gn_w"], prm["gn_b"], prm["gla_gn_w"],
      prm["w_out"], prm["norm_post"], prm["bd64"])


def _swiglu_acc(hb, wg_ref, wu_ref, wd_ref):
    acc = jnp.zeros((hb.shape[0], D), F32)
    for c in range(D_FF // FCH):
        sl = slice(c * FCH, (c + 1) * FCH)
        gate = jnp.dot(hb, wg_ref[:, sl], preferred_element_type=F32)
        up = jnp.dot(hb, wu_ref[:, sl], preferred_element_type=F32)
        act = (gate * _sigmoid(gate) * up).astype(BF16)
        acc = acc + jnp.dot(act, wd_ref[sl, :], preferred_element_type=F32)
    return acc


def _ffn_kernel(x_ref, mod_ref, gpre_ref, gpost_ref, wg_ref, wu_ref, wd_ref, xo_ref):
    x = x_ref[0]
    hb = (_rms(x, gpre_ref[...]) * (1.0 + _wide_mod(mod_ref, 4)) + _wide_mod(mod_ref, 3)).astype(BF16)
    fx = _swiglu_acc(hb, wg_ref, wu_ref, wd_ref)
    xo_ref[0] = x + _wide_mod(mod_ref, 5) * _rms(fx, gpost_ref[...])


def _single_buffered(shape):
    nd = len(shape)
    return pl.BlockSpec(shape, lambda *_: (0,) * nd, pipeline_mode=pl.Buffered(1))


def _ffn(xc, modtab, g_pre, g_post, wg, wu, wd):
    B, N, _ = xc.shape
    wide = pl.BlockSpec((1, TW, D), lambda b, j: (b, j, 0))
    return pl.pallas_call(
        _ffn_kernel,
        out_shape=jax.ShapeDtypeStruct((B, N, D), F32),
        grid=(B, N // TW),
        in_specs=[wide, _wide_mod_spec(), _const_spec((1, D)), _const_spec((1, D)),
                  _single_buffered((D, D_FF)), _single_buffered((D, D_FF)), _single_buffered((D_FF, D))],
        out_specs=wide,
        compiler_params=_cparams(("parallel", "parallel")),
        name="ffn",
    )(xc, modtab, g_pre, g_post, wg, wu, wd)


MOE_TB = 1024
MOE_SEG = 32
MOE_TM = 512
MOE_R = 2 * MOE_TB + N_EXPERTS * MOE_SEG
MOE_NP = MOE_R // MOE_SEG


def _moe_route_kernel(x_ref, mod_ref, gpre_ref, router_ref, h_ref, info_ref, infot_ref, cnt_ref):
    mod = mod_ref[0, 0]
    h = _rms(x_ref[0], gpre_ref[...]) * (1.0 + mod[4:5]) + mod[3:4]
    h_ref[0] = h.astype(BF16)
    lane = lax.broadcasted_iota(jnp.int32, (MOE_TB, LANES), 1)
    logits = jnp.dot(h, router_ref[...], precision=lax.Precision.HIGHEST, preferred_element_type=F32)
    logits = jnp.where(lane < N_EXPERTS, logits, -jnp.inf)
    v1 = jnp.max(logits, axis=-1, keepdims=True)
    i1 = jnp.min(jnp.where(logits == v1, lane, LANES), axis=-1, keepdims=True)
    rest = jnp.where(lane == i1, -jnp.inf, logits)
    v2 = jnp.max(rest, axis=-1, keepdims=True)
    i2 = jnp.min(jnp.where(rest == v2, lane, LANES), axis=-1, keepdims=True)
    ex = jnp.exp(v2 - v1)
    w1 = 1.0 / (1.0 + ex)
    w2 = ex * w1
    e1 = jnp.where(lane == i1, 1.0, 0.0)
    e2 = jnp.where(lane == i2, 1.0, 0.0)
    es = e1 + e2
    t = lax.broadcasted_iota(jnp.int32, (MOE_TB, MOE_TB), 0)
    s = lax.broadcasted_iota(jnp.int32, (MOE_TB, MOE_TB), 1)
    before = jnp.where(s < t, 1.0, 0.0).astype(BF16)
    rank = jnp.dot(before, es.astype(BF16), preferred_element_type=F32)
    cnt = jnp.sum(es, axis=0, keepdims=True)
    segs = jnp.floor((cnt + (MOE_SEG - 1)) * (1.0 / MOE_SEG))
    ea = lax.broadcasted_iota(jnp.int32, (LANES, LANES), 0)
    eb = lax.broadcasted_iota(jnp.int32, (LANES, LANES), 1)
    earlier = jnp.where(ea < eb, 1.0, 0.0).astype(BF16)
    start = jnp.dot(jnp.broadcast_to(segs, (8, LANES)).astype(BF16), earlier,
                    preferred_element_type=F32)[0:1] * MOE_SEG
    pos = rank + start
    d1 = jnp.sum(e1 * pos, axis=-1, keepdims=True)
    d2 = jnp.sum(e2 * pos, axis=-1, keepdims=True)
    info = jnp.where(lane == 0, d1, jnp.where(lane == 1, d2, jnp.where(lane == 2, w1, jnp.where(lane == 3, w2, 0.0))))
    info_ref[0] = info
    infot_ref[0] = jnp.transpose(info)[0:8]
    cnt_ref[0] = jnp.broadcast_to(cnt, (8, LANES))


def _moe_route(xs, modtab, g_pre, router):
    B, S, _ = xs.shape
    per = S // MOE_TB
    nb = B * per
    blk = lambda c: pl.BlockSpec((1, MOE_TB, c), lambda i: (i // per, i % per, 0))
    flat = lambda r, c: pl.BlockSpec((1, r, c), lambda i: (i, 0, 0))
    return pl.pallas_call(
        _moe_route_kernel,
        out_shape=[jax.ShapeDtypeStruct((nb, MOE_TB, D), BF16), jax.ShapeDtypeStruct((nb, MOE_TB, LANES), F32),
                   jax.ShapeDtypeStruct((nb, 8, MOE_TB), F32), jax.ShapeDtypeStruct((nb, 8, LANES), F32)],
        grid=(nb,),
        in_specs=[blk(D), pl.BlockSpec((1, 1, 6, D), lambda i: (i // per, 1, 0, 0)),
                  _const_spec((1, D)), _const_spec((D, LANES))],
        out_specs=[flat(MOE_TB, D), flat(MOE_TB, LANES), flat(8, MOE_TB), flat(8, LANES)],
        compiler_params=_cparams(("parallel",)),
        name="moe_route",
    )(xs, modtab, g_pre, router)


def _moe_plan(cnt, n_tiles):
    pc = (cnt + MOE_SEG - 1) // MOE_SEG * MOE_SEG
    inc = jnp.cumsum(pc, axis=1)
    loff = inc - pc
    reg = (jnp.sum(pc, axis=0) + MOE_TM - 1) // MOE_TM * MOE_TM
    gend = jnp.cumsum(reg)
    goff = (gend - reg)[None, :] + jnp.cumsum(pc, axis=0) - pc
    rows = jnp.arange(MOE_NP, dtype=jnp.int32) * MOE_SEG
    e_p = jnp.sum((rows[None, :, None] >= inc[:, None, :]).astype(jnp.int32), axis=-1)
    e_c = jnp.minimum(e_p, N_EXPERTS - 1)
    dst = jnp.take_along_axis(goff, e_c, axis=1) + rows[None, :] - jnp.take_along_axis(loff, e_c, axis=1)
    dst = jnp.where(e_p < N_EXPERTS, dst, 0).astype(jnp.int32)
    n_valid = (inc[:, -1] // MOE_SEG).astype(jnp.int32)
    trow = jnp.arange(n_tiles, dtype=jnp.int32) * MOE_TM
    te = jnp.sum((trow[:, None] >= gend[None, :]).astype(jnp.int32), axis=-1)
    valid = te < N_EXPERTS
    last = gend[-1] // MOE_TM - 1
    te = jnp.where(valid, te, te[last]).astype(jnp.int32)
    src = jnp.where(valid, jnp.arange(n_tiles, dtype=jnp.int32), last).astype(jnp.int32)
    return dst, n_valid, te, src, valid.astype(jnp.int32)


def _piece_copy(src_ref, src_row, dst_ref, dst_row, sem):
    return pltpu.make_async_copy(src_ref.at[pl.ds(src_row, MOE_SEG)], dst_ref.at[pl.ds(dst_row, MOE_SEG)], sem)


def _moe_gather_kernel(dst_ref, nv_ref, h_ref, infot_ref, xg_in_ref, xg_ref, buf_ref, sem_ref):
    del xg_in_ref
    i = pl.program_id(0)
    it = infot_ref[0]
    rr = lax.broadcasted_iota(jnp.int32, (MOE_R, MOE_TB), 0).astype(F32)
    onehot = jnp.where(rr == it[0:1], 1.0, jnp.where(rr == it[1:2], 1.0, 0.0)).astype(BF16)
    buf_ref[...] = jnp.dot(onehot, h_ref[0], preferred_element_type=F32).astype(BF16)
    nv = nv_ref[i]

    def piece(p):
        return _piece_copy(buf_ref, pl.multiple_of(p * MOE_SEG, MOE_SEG),
                           xg_ref, pl.multiple_of(dst_ref[i, p], MOE_SEG), sem_ref.at[p])

    def start(p, c):
        piece(p).start()
        return c

    def wait(p, c):
        piece(p).wait()
        return c

    lax.fori_loop(0, nv, start, 0)
    lax.fori_loop(0, nv, wait, 0)


def _moe_gather(dst, n_valid, h, info_t, n_rows):
    nb = h.shape[0]
    flat = lambda r, c: pl.BlockSpec((1, r, c), lambda i, *_: (i, 0, 0))
    return pl.pallas_call(
        _moe_gather_kernel,
        out_shape=jax.ShapeDtypeStruct((n_rows, D), BF16),
        grid_spec=pltpu.PrefetchScalarGridSpec(
            num_scalar_prefetch=2, grid=(nb,),
            in_specs=[flat(MOE_TB, D), flat(8, MOE_TB), pl.BlockSpec(memory_space=pl.ANY)],
            out_specs=pl.BlockSpec(memory_space=pl.ANY),
            scratch_shapes=[pltpu.VMEM((MOE_R, D), BF16), pltpu.SemaphoreType.DMA((MOE_NP,))]),
        input_output_aliases={4: 0},
        compiler_params=_cparams(("arbitrary",)),
        name="moe_gather",
    )(dst, n_valid, h, info_t, jnp.zeros((n_rows, D), BF16))


def _moe_ffn_kernel(te_ref, src_ref, valid_ref, x_ref, wg_ref, wu_ref, wd_ref, y_ref):
    i = pl.program_id(0)

    @pl.when(valid_ref[i] == 1)
    def _():
        y_ref[...] = _swiglu_acc(x_ref[...], wg_ref.at[0], wu_ref.at[0], wd_ref.at[0]).astype(BF16)

    @pl.when(valid_ref[i] == 0)
    def _():
        y_ref[...] = jnp.zeros_like(y_ref)


def _moe_ffn(te, src, valid, xg, wg, wu, wd):
    n_rows = xg.shape[0]
    wspec = lambda r, c: pl.BlockSpec((1, r, c), lambda i, te, src, valid: (te[i], 0, 0))
    return pl.pallas_call(
        _moe_ffn_kernel,
        out_shape=jax.ShapeDtypeStruct((n_rows, D), BF16),
        grid_spec=pltpu.PrefetchScalarGridSpec(
            num_scalar_prefetch=3, grid=(n_rows // MOE_TM,),
            in_specs=[pl.BlockSpec((MOE_TM, D), lambda i, te, src, valid: (src[i], 0)),
                      wspec(D, D_FF), wspec(D, D_FF), wspec(D_FF, D)],
            out_specs=pl.BlockSpec((MOE_TM, D), lambda i, te, src, valid: (i, 0))),
        compiler_params=_cparams(("arbitrary",)),
        name="moe_ffn",
    )(te, src, valid, xg, wg, wu, wd)


def _moe_combine_kernel(dst_ref, nv_ref, x_ref, mod_ref, gpost_ref, info_ref, yg_ref, xo_ref, buf_ref, sem_ref):
    i = pl.program_id(0)
    nv = nv_ref[i]

    def piece(p):
        return _piece_copy(yg_ref, pl.multiple_of(dst_ref[i, p], MOE_SEG),
                           buf_ref, pl.multiple_of(p * MOE_SEG, MOE_SEG), sem_ref.at[p])

    def start(p, c):
        piece(p).start()
        return c

    def clear(p, c):
        buf_ref[pl.ds(pl.multiple_of(p * MOE_SEG, MOE_SEG), MOE_SEG), :] = jnp.zeros((MOE_SEG, D), BF16)
        return c

    def wait(p, c):
        piece(p).wait()
        return c

    lax.fori_loop(0, nv, start, 0)
    lax.fori_loop(nv, MOE_NP, clear, 0)
    info = info_ref[0]
    rr = lax.broadcasted_iota(jnp.int32, (MOE_TB, MOE_R), 1).astype(F32)
    comb = jnp.where(rr == info[:, 0:1], info[:, 2:3], jnp.where(rr == info[:, 1:2], info[:, 3:4], 0.0)).astype(BF16)
    lax.fori_loop(0, nv, wait, 0)
    fx = jnp.dot(comb, buf_ref[...], preferred_element_type=F32)
    xo_ref[0] = x_ref[0] + mod_ref[0, 0][5:6] * _rms(fx, gpost_ref[...])


def _moe_combine(dst, n_valid, xs, modtab, g_post, info, yg):
    B, S, _ = xs.shape
    per = S // MOE_TB
    blk = pl.BlockSpec((1, MOE_TB, D), lambda i, *_: (i // per, i % per, 0))
    return pl.pallas_call(
        _moe_combine_kernel,
        out_shape=jax.ShapeDtypeStruct((B, S, D), F32),
        grid_spec=pltpu.PrefetchScalarGridSpec(
            num_scalar_prefetch=2, grid=(B * per,),
            in_specs=[blk, pl.BlockSpec((1, 1, 6, D), lambda i, *_: (i // per, 1, 0, 0)),
                      pl.BlockSpec((1, D), lambda i, *_: (0, 0)),
                      pl.BlockSpec((1, MOE_TB, LANES), lambda i, *_: (i, 0, 0)),
                      pl.BlockSpec(memory_space=pl.ANY)],
            out_specs=blk,
            scratch_shapes=[pltpu.VMEM((MOE_R, D), BF16), pltpu.SemaphoreType.DMA((MOE_NP,))]),
        compiler_params=_cparams(("arbitrary",)),
        name="moe_combine",
    )(dst, n_valid, xs, modtab, g_post, info, yg)


def _moe(xs, modtab, g_pre, g_post, router, wg, wu, wd):
    B, S, _ = xs.shape
    nb = B * S // MOE_TB
    h, info, info_t, cnt = _moe_route(xs, modtab, g_pre, router)
    worst = 2 * B * S + nb * N_EXPERTS * (MOE_SEG - 1) + N_EXPERTS * (MOE_TM - 1)
    n_tiles = -(-worst // MOE_TM)
    dst, n_valid, te, src, valid = _moe_plan(cnt[:, 0, :N_EXPERTS].astype(jnp.int32), n_tiles)
    xg = _moe_gather(dst, n_valid, h, info_t, n_tiles * MOE_TM)
    yg = _moe_ffn(te, src, valid, xg, wg, wu, wd)
    return _moe_combine(dst, n_valid, xs, modtab, g_post, info, yg)


def _block_diag2(w):
    z = jnp.zeros_like(w[0])
    return jnp.concatenate([jnp.concatenate([w[0], z], axis=1), jnp.concatenate([z, w[1]], axis=1)], axis=0)


def _row(v):
    return v.reshape(1, -1).astype(F32)


def _head_ones(width, head):
    i = jnp.arange(width) // head
    return (i[:, None] == i[None, :]).astype(BF16)


def kernel(x, c, ctx, c_ctx, ada_w, ada_b, norm_mix_pre, norm_mix_post, norm_ffn_pre, norm_ffn_post, w_in, shift_mu, rw_w_up, rw_w0, rw_a_up, rw_a0, rw_k_k, rw_k_a, rw_r_k, rw_g_up, rw_gn_w, rw_gn_b, rw_v_down, rw_v_up, rw_v0, gla_conv, gla_a_up, gla_a_b, gla_gn_w, w_out, ffn_w_gate, ffn_w_up, ffn_w_down, moe_router, moe_w_gate, moe_w_up, moe_w_down):
    B, S, _ = x.shape
    n_ctx = ctx.shape[1]
    depth = w_in.shape[0]
    assert n_ctx == TB and S % MOE_TB == 0 and (n_ctx + S) % TW == 0 and depth == 2

    xc = None
    cvec = jnp.concatenate([c, c_ctx[None, :], jnp.zeros((16 - B - 1, D), F32)], axis=0)
    bd64 = _head_ones(RW_WIDTH, RW_HEAD)
    ada_b3 = ada_b.reshape(depth, 1, 6 * D)
    v_first = None
    out = None
    for i in range(depth):
        last = i == depth - 1
        mods = _adaln(cvec, ada_w, ada_b3, i)
        mod_x = mods[:B].reshape(B, 6, D)
        mod_c = jnp.broadcast_to(mods[B].reshape(1, 6, D), (B, 6, D))
        modtab = jnp.stack([mod_c, mod_x], axis=1)

        w_i = jnp.concatenate([w_in[i], jnp.zeros((D, GLA_PAD - GLA_COLS), F32)], axis=1).astype(BF16)
        p_rw, p_gl = _inproj(ctx, x, xc, _row(norm_mix_pre[i]), modtab, w_i)

        prm = dict(
            mu=_row(shift_mu[i]),
            w_up=_block_diag2(rw_w_up[i]).astype(BF16), w0=_row(rw_w0[i]),
            a_up=_block_diag2(rw_a_up[i]).astype(BF16), a0=_row(rw_a0[i]),
            k_k=_row(rw_k_k[i]), k_a=_row(rw_k_a[i]), r_k=_row(rw_r_k[i]),
            g_up=rw_g_up[i].astype(BF16), bd64=bd64,
            gn_w=_row(rw_gn_w[i]), gn_b=_row(rw_gn_b[i]), gla_gn_w=_row(gla_gn_w[i]),
            w_out=w_out[i].astype(BF16), norm_post=_row(norm_mix_post[i]),
        )
        gate_pad = jnp.zeros((LANES - 2 * GLA_GATE_RANK, 2 * GLA_KW), F32)
        gla_prm = dict(conv=gla_conv[i].astype(F32), a_b=_row(gla_a_b[i]),
                       a_up=jnp.concatenate([_block_diag2(gla_a_up[i]), gate_pad], axis=0).astype(BF16))
        if i > 0:
            pad = LANES - RW_V_RANK
            prm["v_down"] = jnp.concatenate([rw_v_down[i - 1], jnp.zeros((RW_WIDTH, pad), F32)], axis=1).astype(BF16)
            prm["v_up"] = jnp.concatenate([rw_v_up[i - 1], jnp.zeros((pad, RW_WIDTH), F32)], axis=0).astype(BF16)
            prm["v0"] = _row(rw_v0[i - 1])

        r, kk, vm, g, bonus, lw, bb, ke = _rwprep(p_rw, prm, v_first if i > 0 else None)
        if i == 0:
            v_first = vm
        y = _rwscan(r, kk, vm, lw, bb, ke)
        q, k, gv, og, lg = _glaprep(p_gl, gla_prm)
        o = _glascan(q, k, gv, lg)
        xc = _readout(y, bonus, g, o, og, ctx, x, xc, modtab, prm, latents_only=last)

        jf = i // 2
        if i % 2 == 0:
            xc = _ffn(xc, modtab, _row(norm_ffn_pre[i]), _row(norm_ffn_post[i]),
                      ffn_w_gate[jf].astype(BF16), ffn_w_up[jf].astype(BF16), ffn_w_down[jf].astype(BF16))
        else:
            router = jnp.concatenate([moe_router[jf], jnp.zeros((D, LANES - N_EXPERTS), F32)], axis=1)
            out = _moe(xc, modtab, _row(norm_ffn_pre[i]), _row(norm_ffn_post[i]), router,
                       moe_w_gate[jf].astype(BF16), moe_w_up[jf].astype(BF16), moe_w_down[jf].astype(BF16))
    return out
```

```python
import functools
import math

import jax
import jax.numpy as jnp
from jax import lax
from jax.experimental import pallas as pl
from jax.experimental.pallas import tpu as pltpu

F32, BF16 = jnp.float32, jnp.bfloat16
ACT = BF16

D = 1024
GRID_W = 64
RW_WIDTH = 512
RW_HEAD = 64
RW_RANK = 64
RW_G_RANK = 128
RW_V_RANK = 32
RW_GN_EPS = 64e-5
GLA_WIDTH = 512
GLA_HEADS = 4
GLA_DV = 128
GLA_DK = 64
GLA_KW = 256
GLA_GATE_RANK = 16
GLA_GATE_NORM = 16.0
GLA_NORM_EPS = 1e-5
D_FF = 2816
N_EXPERTS = 8
NORM_EPS = 1e-6
RW_COLS = 3 * RW_WIDTH + 4 * RW_RANK + RW_G_RANK
GLA_QKV = 2 * GLA_KW + GLA_WIDTH
GLA_COLS = GLA_QKV + GLA_WIDTH + 2 * GLA_GATE_RANK
GLA_PAD = 1664

LANES = 128
TB = 256
TW = 3 * TB
CH = 64
FCH = 256
HALO = 16
VMEM_LIMIT = 56 * 1024 * 1024


def _cparams(sem):
    return pltpu.CompilerParams(dimension_semantics=sem, vmem_limit_bytes=VMEM_LIMIT)


def _bdot(a, b):
    return jnp.dot(a.astype(BF16), b.astype(BF16), preferred_element_type=F32)


def _bdot_nt(a, b):
    return lax.dot_general(a.astype(BF16), b.astype(BF16), (((1,), (1,)), ((), ())),
                           preferred_element_type=F32)


def _bdot_tn(a, b):
    return lax.dot_general(a.astype(BF16), b.astype(BF16), (((0,), (0,)), ((), ())),
                           preferred_element_type=F32)


def _split_dot(a_exact, x):
    h1 = x.astype(BF16)
    r1 = x - h1.astype(F32)
    h2 = r1.astype(BF16)
    h3 = (r1 - h2.astype(F32)).astype(BF16)
    return (jnp.dot(a_exact, h1, preferred_element_type=F32)
            + jnp.dot(a_exact, h2, preferred_element_type=F32)
            + jnp.dot(a_exact, h3, preferred_element_type=F32))


def _seg_sum(x, bd, exact=True):
    hi = x.astype(BF16)
    out = jnp.dot(hi, bd, preferred_element_type=F32)
    if exact:
        lo = (x - hi.astype(F32)).astype(BF16)
        out = out + jnp.dot(lo, bd, preferred_element_type=F32)
    return out


def _sigmoid(x):
    return jax.nn.sigmoid(x)


def _rms(x, g):
    return x * lax.rsqrt(jnp.mean(x * x, axis=-1, keepdims=True) + NORM_EPS) * g


def _pair_stack(z, lo):
    return jnp.concatenate([jnp.where(lo, z, 0.0), jnp.where(lo, 0.0, z)], axis=0)


def _adaln_kernel(c_ref, w_ref, b_ref, o_ref):
    c = c_ref[...]
    s = c * _sigmoid(c)
    o_ref[...] = jnp.dot(s, w_ref[0], precision=lax.Precision.HIGHEST,
                         preferred_element_type=F32) + b_ref[0]


def _adaln(cvec, w, b, layer):
    rows = cvec.shape[0]
    n = w.shape[2]
    return pl.pallas_call(
        _adaln_kernel,
        out_shape=jax.ShapeDtypeStruct((rows, n), F32),
        grid=(n // D,),
        in_specs=[pl.BlockSpec((rows, D), lambda i: (0, 0)),
                  pl.BlockSpec((1, D, D), lambda i: (layer, 0, i)),
                  pl.BlockSpec((1, 1, D), lambda i: (layer, 0, i))],
        out_specs=pl.BlockSpec((rows, D), lambda i: (0, i)),
        compiler_params=_cparams(("arbitrary",)),
        name="adaln",
    )(cvec, w, b)


def _wide_mod(mod_ref, k):
    is_ctx = jnp.logical_and(pl.program_id(1) == 0, lax.broadcasted_iota(jnp.int32, (TW, 1), 0) < TB)
    return jnp.where(is_ctx, mod_ref[0, 0, k:k + 1], mod_ref[0, 1, k:k + 1])


def _wide_rows(refs):
    if len(refs) == 1:
        return refs[0][0]
    first = jnp.where(pl.program_id(1) == 0, refs[0][0], refs[1][0])
    return jnp.concatenate([first] + [r[0] for r in refs[2:]], axis=0)


def _inproj_kernel(n_src, *refs):
    g_ref, mod_ref, w_ref, prw_ref, pgl_ref = refs[n_src:]
    x = _wide_rows(refs[:n_src])
    h = _rms(x, g_ref[...]) * (1.0 + _wide_mod(mod_ref, 1)) + _wide_mod(mod_ref, 0)
    hb = h.astype(BF16)
    prw_ref[0] = jnp.dot(hb, w_ref[:, :RW_COLS], preferred_element_type=F32).astype(ACT)
    pgl_ref[0] = jnp.dot(hb, w_ref[:, RW_COLS:], preferred_element_type=F32).astype(ACT)


def _wide_src(ctx, x, xc):
    if xc is not None:
        return [xc], [pl.BlockSpec((1, TW, D), lambda b, j: (b, j, 0))]
    per = TW // TB
    specs = [pl.BlockSpec((1, TB, D), lambda b, j: (b, 0, 0))]
    for k in range(per):
        specs.append(pl.BlockSpec((1, TB, D), lambda b, j, k=k: (b, jnp.maximum(j * per + k - 1, 0), 0)))
    return [ctx] + [x] * per, specs


def _wide_mod_spec():
    return pl.BlockSpec((1, 2, 6, D), lambda b, j: (b, 0, 0, 0))


def _const_spec(shape):
    nd = len(shape)
    return pl.BlockSpec(shape, lambda *_: (0,) * nd)


def _tile_spec(c, first=0):
    return pl.BlockSpec((1, TB, c), lambda b, j: (b, j + first, 0))


def _dir_tile_spec(c):
    return pl.BlockSpec((2, 1, TB, c), lambda b, j: (0, b, j, 0))


def _mod_spec(first=0):
    return pl.BlockSpec((1, 1, 6, D), lambda b, j: (b, jnp.minimum(j + first, 1), 0, 0))


def _inproj(ctx, x, xc, g, modtab, w):
    B = modtab.shape[0]
    N = xc.shape[1] if xc is not None else ctx.shape[1] + x.shape[1]
    srcs, src_specs = _wide_src(ctx, x, xc)
    wide = lambda c: pl.BlockSpec((1, TW, c), lambda b, j: (b, j, 0))
    return pl.pallas_call(
        functools.partial(_inproj_kernel, len(srcs)),
        out_shape=[jax.ShapeDtypeStruct((B, N, RW_COLS), ACT),
                   jax.ShapeDtypeStruct((B, N, GLA_PAD), ACT)],
        grid=(B, N // TW),
        in_specs=src_specs + [_const_spec((1, D)), _wide_mod_spec(), _const_spec((D, RW_COLS + GLA_PAD))],
        out_specs=[wide(RW_COLS), wide(GLA_PAD)],
        compiler_params=_cparams(("parallel", "parallel")),
        name="inproj",
    )(*srcs, g, modtab, w)


def _shift_tables():
    t = jnp.arange(TB)
    prev = t[:, None] - 1 == t[None, :]
    nxt = t[:, None] + 1 == t[None, :]
    col = (t % GRID_W)[:, None]
    shifts = jnp.stack([jnp.stack([prev, nxt]),
                        jnp.stack([prev & (col != 0), nxt & (col != GRID_W - 1)])]).astype(BF16)
    c = jnp.arange(RW_COLS)
    zero = jnp.zeros_like(c, dtype=bool)
    classes = jnp.stack([jnp.stack([c % 2 == 0, c % 2 == 1, zero, zero]),
                         jnp.stack([c % 4 == k for k in range(4)])]).astype(F32)
    return shifts, classes


def _rwprep_kernel(has_vres, n_tiles, p_ref, hu_ref, hd_ref, sh_ref, cls_ref, mu_ref, wup_ref, w0_ref, aup_ref,
                   a0_ref, kk_ref, ka_ref, rk_ref, gup_ref, bd_ref, *rest):
    if has_vres:
        vf_ref, vdn_ref, vup_ref, v0_ref = rest[:4]
        rest = rest[4:]
    r_out, kk_out, v_out, g_out, bon_out, lw_out, b_out, ke_out = rest
    j = pl.program_id(1)
    pb = p_ref[0]
    p = pb.astype(F32)
    prev1 = jnp.dot(sh_ref[0, 0], pb, preferred_element_type=F32)
    next1 = jnp.dot(sh_ref[0, 1], pb, preferred_element_type=F32)
    has_upper = jnp.where(j == 1, 0.0, 1.0)
    has_lower = jnp.where(j == n_tiles - 1, 0.0, 1.0)
    up = jnp.concatenate([hu_ref[0].astype(F32) * has_upper, p[:TB - GRID_W]], axis=0)
    down = jnp.concatenate([p[GRID_W:], hd_ref[0].astype(F32) * has_lower], axis=0)
    cls = cls_ref[0] > 0.5
    shifted = jnp.where(cls[0:1], prev1, jnp.where(cls[1:2], next1, jnp.where(cls[2:3], up, down)))
    u = p + mu_ref[...] * (shifted - p)

    r = u[:, 0:RW_WIDTH]
    k = u[:, RW_WIDTH:2 * RW_WIDTH]
    v = u[:, 2 * RW_WIDTH:3 * RW_WIDTH]
    o = 3 * RW_WIDTH
    wd = u[:, o:o + 2 * RW_RANK]
    ad = u[:, o + 2 * RW_RANK:o + 4 * RW_RANK]
    gd = u[:, o + 4 * RW_RANK:]

    w_logit = w0_ref[...] + _bdot(jnp.tanh(wd), wup_ref[...])
    lw = -math.exp(-0.5) * _sigmoid(w_logit)
    a = _sigmoid(a0_ref[...] + _bdot(ad, aup_ref[...]))
    bd = bd_ref[...]
    kk = k * kk_ref[...]
    kk = kk * lax.rsqrt(jnp.maximum(_seg_sum(kk * kk, bd), 1e-24))
    g = _bdot(_sigmoid(gd), gup_ref[...])
    if has_vres:
        gate = _sigmoid(v0_ref[...] + _bdot(_bdot(v, vdn_ref[...]), vup_ref[...]))
        vm = v + (vf_ref[0].astype(F32) - v) * gate
    else:
        vm = v
    ke_sum = jnp.zeros_like(k)
    for d in range(2):
        a_d = a[:, d * RW_WIDTH:(d + 1) * RW_WIDTH]
        ke_d = k * (1.0 + (a_d - 1.0) * ka_ref[...])
        lw_out[d, 0] = lw[:, d * RW_WIDTH:(d + 1) * RW_WIDTH]
        b_out[d, 0] = (kk * a_d).astype(ACT)
        ke_out[d, 0] = ke_d.astype(ACT)
        ke_sum = ke_sum + ke_d
    r_out[0] = r.astype(ACT)
    kk_out[0] = kk.astype(ACT)
    v_out[0] = vm.astype(ACT)
    g_out[0] = g.astype(ACT)
    bon_out[0] = (_seg_sum(r * ke_sum * rk_ref[...], bd, exact=False) * vm).astype(ACT)


def _rwprep(p_rw, prm, v_first):
    B, N, _ = p_rw.shape
    nt = N // TB
    hb = TB // GRID_W
    nhb = N // GRID_W
    has_vres = v_first is not None
    W2 = 2 * RW_WIDTH
    in_specs = [
        _tile_spec(RW_COLS),
        pl.BlockSpec((1, GRID_W, RW_COLS), lambda b, j: (b, jnp.maximum(j * hb - 1, 0), 0)),
        pl.BlockSpec((1, GRID_W, RW_COLS), lambda b, j: (b, jnp.minimum(j * hb + hb, nhb - 1), 0)),
        pl.BlockSpec((1, 2, TB, TB), lambda b, j: (jnp.minimum(j, 1), 0, 0, 0)),
        pl.BlockSpec((1, 4, RW_COLS), lambda b, j: (jnp.minimum(j, 1), 0, 0)),
        _const_spec((1, RW_COLS)), _const_spec((2 * RW_RANK, W2)), _const_spec((1, W2)),
        _const_spec((2 * RW_RANK, W2)), _const_spec((1, W2)),
        _const_spec((1, RW_WIDTH)), _const_spec((1, RW_WIDTH)), _const_spec((1, RW_WIDTH)),
        _const_spec((RW_G_RANK, RW_WIDTH)), _const_spec((RW_WIDTH, RW_WIDTH)),
    ]
    shifts, classes = _shift_tables()
    args = [p_rw, p_rw, p_rw, shifts, classes, prm["mu"], prm["w_up"], prm["w0"], prm["a_up"], prm["a0"],
            prm["k_k"], prm["k_a"], prm["r_k"], prm["g_up"], prm["bd64"]]
    if has_vres:
        in_specs += [_tile_spec(RW_WIDTH), _const_spec((RW_WIDTH, LANES)), _const_spec((LANES, RW_WIDTH)),
                     _const_spec((1, RW_WIDTH))]
        args += [v_first, prm["v_down"], prm["v_up"], prm["v0"]]
    tok = jax.ShapeDtypeStruct((B, N, RW_WIDTH), ACT)
    dtok = jax.ShapeDtypeStruct((2, B, N, RW_WIDTH), ACT)
    return pl.pallas_call(
        functools.partial(_rwprep_kernel, has_vres, nt),
        out_shape=[tok] * 5 + [jax.ShapeDtypeStruct((2, B, N, RW_WIDTH), F32), dtok, dtok],
        grid=(B, nt),
        in_specs=in_specs,
        out_specs=[_tile_spec(RW_WIDTH)] * 5 + [_dir_tile_spec(RW_WIDTH)] * 3,
        compiler_params=_cparams(("parallel", "parallel")),
        name="rwprep",
    )(*args)


NCH = TB // CH


def _chunk_tri(rev):
    t = lax.broadcasted_iota(jnp.int32, (TB, TB), 0)
    s = lax.broadcasted_iota(jnp.int32, (TB, TB), 1)
    order = (s >= t) if rev else (s <= t)
    return jnp.where(jnp.logical_and(t // CH == s // CH, order), 1.0, 0.0).astype(BF16)


def _chunk_totals(x):
    tots = [jnp.sum(x[c * CH:(c + 1) * CH], axis=0, keepdims=True) for c in range(NCH)]
    full = jnp.concatenate([jnp.broadcast_to(t, (CH, x.shape[1])) for t in tots], axis=0)
    return tots, full


def _scan_masks(rev):
    ti = lax.broadcasted_iota(jnp.int32, (CH, LANES), 0)
    li = lax.broadcasted_iota(jnp.int32, (CH, LANES), 1)
    si = li & (CH - 1)
    incl = (si >= ti) if rev else (si <= ti)
    strict = (si > ti) if rev else (si < ti)
    return incl, strict


def _rwscan_kernel(rf_ref, kkf_ref, vf_ref, lwf_ref, bf_ref, kef_ref,
                   rb_ref, kkb_ref, vb_ref, lwb_ref, bb_ref, keb_ref, yf_ref, yb_ref, st_ref):
    j = pl.program_id(1)

    @pl.when(j == 0)
    def _():
        st_ref[...] = jnp.zeros_like(st_ref)

    ti = lax.broadcasted_iota(jnp.int32, (CH, LANES), 0)
    li = lax.broadcasted_iota(jnp.int32, (CH, LANES), 1)
    si = li & (CH - 1)
    lo = li < CH
    eye = jnp.where(ti == si, 1.0, 0.0)
    rblk = lax.broadcasted_iota(jnp.int32, (LANES, LANES), 0) // CH
    cblk = lax.broadcasted_iota(jnp.int32, (LANES, LANES), 1) // CH
    bdmask = rblk == cblk
    t64 = lax.broadcasted_iota(jnp.int32, (CH, CH), 0)
    s64 = lax.broadcasted_iota(jnp.int32, (CH, CH), 1)
    tri = [jnp.where(s64 <= t64, 1.0, 0.0).astype(BF16), jnp.where(s64 >= t64, 1.0, 0.0).astype(BF16)]
    m_incl = [si <= ti, si >= ti]
    m_strict = [si < ti, si > ti]
    n_pairs = RW_WIDTH // LANES
    in_refs = ((rf_ref, kkf_ref, vf_ref, lwf_ref, bf_ref, kef_ref),
               (rb_ref, kkb_ref, vb_ref, lwb_ref, bb_ref, keb_ref))
    y_refs = (yf_ref, yb_ref)

    def stack(z):
        return _pair_stack(z, lo).astype(BF16)

    st = {(d, p): st_ref[d, p] for d in range(2) for p in range(n_pairs)}

    def scan_step(step):
        cur = {}
        w_tot = {}
        for d in range(2):
            c = (NCH - 1 - step) if d == 1 else step
            rows = slice(c * CH, (c + 1) * CH)
            refs = in_refs[d]
            r, kk, v = (refs[i][0, rows, :].astype(F32) for i in range(3))
            lw = refs[3][0, 0, rows, :]
            b, ke = refs[4][0, 0, rows, :].astype(F32), refs[5][0, 0, rows, :].astype(F32)
            cum = _split_dot(tri[d], lw)
            tot = jnp.sum(lw, axis=0, keepdims=True)
            w_inv = jnp.exp(-cum)
            w_end = jnp.exp(tot - cum)
            w_tot[d] = jnp.exp(tot)
            rh = r * jnp.exp(cum)
            ah = -(kk * jnp.exp(cum - lw))
            bh, kh, bt, kt = b * w_inv, ke * w_inv, b * w_end, ke * w_end
            for p in range(n_pairs):
                sl = slice(LANES * p, LANES * (p + 1))
                cur[d, p] = dict(ah=ah[:, sl], rh=rh[:, sl], bh=bh[:, sl], kh=kh[:, sl], bt=bt[:, sl],
                                 kt=kt[:, sl], v=v[:, sl], rows=rows, sl=sl)
        yield
        for it in cur.values():
            it["lhs"] = jnp.concatenate([it["ah"], it["rh"]], axis=0).astype(BF16)
            rhs = jnp.concatenate([stack(it["bh"]), stack(it["kh"])], axis=0)
            it["a_all"] = _bdot_nt(it["lhs"], rhs)
        yield
        for (d, _), it in cur.items():
            a_all = it.pop("a_all")
            it["a_ab"] = jnp.where(m_strict[d], a_all[:CH, :LANES], 0.0)
            a_ak = jnp.where(m_strict[d], a_all[:CH, LANES:], 0.0)
            a_rb = jnp.where(m_incl[d], a_all[CH:, :LANES], 0.0)
            a_rk = jnp.where(m_incl[d], a_all[CH:, LANES:], 0.0)
            it["a_r"] = jnp.concatenate([a_rb, a_rk], axis=1).astype(BF16)
            it["v_bd"] = stack(it["v"])
            it["akv"] = _bdot(a_ak, it["v_bd"])
            it["t"] = eye + it["a_ab"]
            it["m"] = _bdot(it["a_ab"], stack(it["a_ab"]))
        yield
        for _ in range(int(math.log2(CH)) - 2):
            for it in cur.values():
                z = _bdot(jnp.concatenate([it["m"], it["t"]], axis=0), stack(it["m"]))
                it["m"] = z[:CH]
                it["t"] = it["t"] + z[CH:]
            yield
        for it in cur.values():
            it["t"] = (it["t"] + _bdot(it["t"], stack(it["m"]))).astype(BF16)
        yield
        for it in cur.values():
            a_til = jnp.dot(it["t"], stack(it["ah"]), preferred_element_type=F32)
            it["lhs2"] = jnp.concatenate([a_til.astype(BF16), it["lhs"][CH:]], axis=0)
            it["cc"] = jnp.dot(it["t"], stack(it["akv"]), preferred_element_type=F32)
            it["rhs_t"] = jnp.concatenate([it["bt"], it["kt"]], axis=0).astype(BF16)
        yield
        z1 = {k: _bdot_nt(it["lhs2"], st[k]) for k, it in cur.items()}
        yield
        u = {k: z1[k][:CH] + it["cc"] for k, it in cur.items()}
        upd = {k: _bdot_tn(jnp.concatenate([u[k], it["v"]], axis=0), it["rhs_t"]) for k, it in cur.items()}
        yield
        for (d, p), it in cur.items():
            y = z1[d, p][CH:] + jnp.dot(it["a_r"], jnp.concatenate([stack(u[d, p]), it["v_bd"]], axis=0),
                                        preferred_element_type=F32)
            y_refs[d][0, it["rows"], it["sl"]] = y.astype(ACT)
            st[d, p] = st[d, p] * w_tot[d][:, it["sl"]] + jnp.where(bdmask, upd[d, p], 0.0)
        yield

    n_stages = 12
    n_dep = 3
    pipeline = [scan_step(s) for s in range(NCH)]
    for slot in range(n_stages + n_dep * (NCH - 1)):
        for s, gen in enumerate(pipeline):
            if 0 <= slot - n_dep * s < n_stages:
                next(gen)
    for (d, p), s in st.items():
        st_ref[d, p] = s


def _bidir_specs(c, n_tiles):
    def back(j):
        return jnp.where(j == 0, 0, n_tiles - j)

    tok_f = pl.BlockSpec((1, TB, c), lambda b, j: (b, j, 0))
    tok_b = pl.BlockSpec((1, TB, c), lambda b, j: (b, back(j), 0))
    dir_f = pl.BlockSpec((1, 1, TB, c), lambda b, j: (0, b, j, 0))
    dir_b = pl.BlockSpec((1, 1, TB, c), lambda b, j: (1, b, back(j), 0))
    return tok_f, tok_b, dir_f, dir_b


def _rwscan(r, kk, v, lw, b, ke):
    B, N, _ = r.shape
    tok_f, tok_b, dir_f, dir_b = _bidir_specs(RW_WIDTH, N // TB)
    out = jax.ShapeDtypeStruct((B, N, RW_WIDTH), ACT)
    return pl.pallas_call(
        _rwscan_kernel,
        out_shape=[out, out],
        grid=(B, N // TB),
        in_specs=[tok_f, tok_f, tok_f, dir_f, dir_f, dir_f, tok_b, tok_b, tok_b, dir_b, dir_b, dir_b],
        out_specs=[tok_f, tok_b],
        scratch_shapes=[pltpu.VMEM((2, RW_WIDTH // LANES, LANES, LANES), F32)],
        compiler_params=_cparams(("parallel", "arbitrary")),
        name="rwscan",
    )(r, kk, v, lw, b, ke, r, kk, v, lw, b, ke)


CONV_K = TB + LANES


def _conv_shift_table():
    t = jnp.arange(TB)[:, None]
    s = jnp.arange(CONV_K)[None, :]
    prev = jnp.where(t == 0, s == TB + 2 * HALO - 1, s == t - 1)
    nxt = jnp.where(t == TB - 1, s == TB, s == t + 1)
    return jnp.stack([prev, nxt]).astype(BF16)


def _glaprep_kernel(n_tiles, p_ref, hp_ref, hn_ref, sh_ref, cw_ref, aup_ref, ab_ref, q_out, k_out, v_out,
                    og_out, lg_out):
    j = pl.program_id(1)
    ub = p_ref[0][:, :GLA_QKV]
    u = ub.astype(F32)
    has_prev = jnp.where(j <= 1, 0.0, 1.0)
    has_next = jnp.where(jnp.logical_or(j == 0, j == n_tiles - 1), 0.0, 1.0)
    ext = jnp.concatenate([ub,
                           (hn_ref[0][:, :GLA_QKV].astype(F32) * has_next).astype(BF16),
                           (hp_ref[0][:, :GLA_QKV].astype(F32) * has_prev).astype(BF16),
                           jnp.zeros((CONV_K - TB - 2 * HALO, GLA_QKV), BF16)], axis=0)
    prev1 = jnp.dot(sh_ref[0], ext, preferred_element_type=F32)
    next1 = jnp.dot(sh_ref[1], ext, preferred_element_type=F32)
    cw = cw_ref[...]
    conv = cw[0:1] * prev1 + cw[1:2] * u + cw[2:3] * next1
    qkv = conv * _sigmoid(conv)
    q_out[0] = (qkv[:, :GLA_KW] * (GLA_DK ** -0.5)).astype(ACT)
    k_out[0] = qkv[:, GLA_KW:2 * GLA_KW].astype(ACT)
    v_out[0] = qkv[:, 2 * GLA_KW:].astype(ACT)
    og_out[0] = p_ref[0][:, GLA_QKV:GLA_QKV + GLA_WIDTH]
    z = _bdot(p_ref[0][:, GLA_QKV + GLA_WIDTH:], aup_ref[...]) + ab_ref[...]
    lg = (jnp.minimum(z, 0.0) - jnp.log1p(jnp.exp(-jnp.abs(z)))) * (1.0 / GLA_GATE_NORM)
    for d in range(2):
        lg_out[d, 0] = lg[:, d * GLA_KW:(d + 1) * GLA_KW]


def _glaprep(p_gl, prm):
    B, N, _ = p_gl.shape
    nt = N // TB
    sub = HALO
    hb = TB // sub
    nhb = N // sub
    return pl.pallas_call(
        functools.partial(_glaprep_kernel, nt),
        out_shape=[jax.ShapeDtypeStruct((B, N, GLA_KW), ACT), jax.ShapeDtypeStruct((B, N, GLA_KW), ACT),
                   jax.ShapeDtypeStruct((B, N, GLA_WIDTH), ACT), jax.ShapeDtypeStruct((B, N, GLA_WIDTH), ACT),
                   jax.ShapeDtypeStruct((2, B, N, GLA_KW), F32)],
        grid=(B, nt),
        in_specs=[_tile_spec(GLA_PAD),
                  pl.BlockSpec((1, sub, GLA_PAD), lambda b, j: (b, jnp.maximum(j * hb - 1, 0), 0)),
                  pl.BlockSpec((1, sub, GLA_PAD), lambda b, j: (b, jnp.minimum(j * hb + hb, nhb - 1), 0)),
                  _const_spec((2, TB, CONV_K)),
                  _const_spec((3, GLA_QKV)), _const_spec((LANES, 2 * GLA_KW)), _const_spec((1, 2 * GLA_KW))],
        out_specs=[_tile_spec(GLA_KW), _tile_spec(GLA_KW), _tile_spec(GLA_WIDTH), _tile_spec(GLA_WIDTH),
                   _dir_tile_spec(GLA_KW)],
        compiler_params=_cparams(("parallel", "parallel")),
        name="glaprep",
    )(p_gl, p_gl, p_gl, _conv_shift_table(), prm["conv"], prm["a_up"], prm["a_b"])


def _glascan_kernel(qf_ref, kf_ref, vf_ref, lgf_ref, qb_ref, kb_ref, vb_ref, lgb_ref, of_ref, ob_ref, st_ref):
    j = pl.program_id(1)

    @pl.when(j == 0)
    def _():
        st_ref[...] = jnp.zeros_like(st_ref)

    li = lax.broadcasted_iota(jnp.int32, (CH, LANES), 1)
    lo = li < CH
    lo2 = lax.broadcasted_iota(jnp.int32, (LANES, LANES), 1) < CH
    zeros_v = jnp.zeros((CH, GLA_DV), F32)
    n_pairs = GLA_KW // LANES

    items = {}
    decs = {}
    for d, refs in enumerate(((qf_ref, kf_ref, vf_ref, lgf_ref), (qb_ref, kb_ref, vb_ref, lgb_ref))):
        rev = d == 1
        q, k, v, lg = refs[0][0].astype(F32), refs[1][0].astype(F32), refs[2][0].astype(F32), refs[3][0, 0]
        cum = _split_dot(_chunk_tri(rev), lg)
        tots, tot_full = _chunk_totals(lg)
        qd = q * jnp.exp(cum)
        ki = k * jnp.exp(-cum)
        kend = k * jnp.exp(tot_full - cum)
        m_incl, _ = _scan_masks(rev)
        for c in range(NCH):
            rows = slice(c * CH, (c + 1) * CH)
            decs[d, c] = jnp.exp(tots[c])
            for p in range(n_pairs):
                sl = slice(LANES * p, LANES * (p + 1))
                items[d, c, p] = dict(qd=qd[rows, sl].astype(BF16), ki=ki[rows, sl], kend=kend[rows, sl],
                                      v0=v[rows, 2 * LANES * p:2 * LANES * p + LANES],
                                      v1=v[rows, 2 * LANES * p + LANES:2 * LANES * (p + 1)], m_incl=m_incl)

    for it in items.values():
        it["att"] = _bdot_nt(it["qd"], _pair_stack(it["ki"], lo))
    for it in items.values():
        att = jnp.where(it["m_incl"], it["att"], 0.0)
        v_bd = jnp.concatenate([jnp.concatenate([it["v0"], zeros_v], axis=1),
                                jnp.concatenate([zeros_v, it["v1"]], axis=1)], axis=0)
        it["o"] = _bdot(att, v_bd)
        it["upd"] = _bdot_tn(jnp.concatenate([it["v0"], it["v1"]], axis=0), _pair_stack(it["kend"], lo))

    for d in range(2):
        for p in range(n_pairs):
            st = st_ref[d, p]
            for step in range(NCH):
                c = (NCH - 1 - step) if d == 1 else step
                it = items[d, c, p]
                st_bd = jnp.concatenate([jnp.where(lo2, st, 0.0), jnp.where(lo2, 0.0, st)], axis=0)
                o = it["o"] + _bdot_nt(it["qd"], st_bd)
                (ob_ref if d == 1 else of_ref)[0, c * CH:(c + 1) * CH,
                                               2 * LANES * p:2 * LANES * (p + 1)] = o.astype(ACT)
                st = st * decs[d, c][:, LANES * p:LANES * (p + 1)] + it["upd"]
            st_ref[d, p] = st


def _glascan(q, k, v, lg):
    B, N, _ = q.shape
    kf, kb, dkf, dkb = _bidir_specs(GLA_KW, N // TB)
    vf, vb, _, _ = _bidir_specs(GLA_WIDTH, N // TB)
    out = jax.ShapeDtypeStruct((B, N, GLA_WIDTH), ACT)
    return pl.pallas_call(
        _glascan_kernel,
        out_shape=[out, out],
        grid=(B, N // TB),
        in_specs=[kf, kf, vf, dkf, kb, kb, vb, dkb],
        out_specs=[vf, vb],
        scratch_shapes=[pltpu.VMEM((2, GLA_KW // LANES, GLA_DV, LANES), F32)],
        compiler_params=_cparams(("parallel", "arbitrary")),
        name="glascan",
    )(q, k, v, lg, q, k, v, lg)


def _readout_kernel(split_src, yf_ref, yb_ref, bon_ref, g_ref, of_ref, ob_ref, og_ref, *refs):
    if split_src:
        x_res = jnp.where(pl.program_id(1) == 0, refs[0][0], refs[1][0])
        refs = refs[2:]
    else:
        x_res = refs[0][0]
        refs = refs[1:]
    mod_ref, gnw_ref, gnb_ref, ggn_ref, wout_ref, gpost_ref, bd_ref, xo_ref = refs
    bd = bd_ref[...]
    y = yf_ref[0].astype(F32) + yb_ref[0].astype(F32)
    mu = _seg_sum(y, bd, exact=False) * (1.0 / RW_HEAD)
    yc = y - mu
    var = _seg_sum(yc * yc, bd, exact=False) * (1.0 / RW_HEAD)
    yn = yc * lax.rsqrt(var + RW_GN_EPS) * gnw_ref[...] + gnb_ref[...]
    rw = (yn + bon_ref[0].astype(F32)) * g_ref[0].astype(F32)
    o = of_ref[0].astype(F32) + ob_ref[0].astype(F32)
    og = og_ref[0].astype(F32)
    parts = [rw.astype(BF16)]
    for h in range(GLA_HEADS):
        sl = slice(GLA_DV * h, GLA_DV * (h + 1))
        oh = o[:, sl]
        on = oh * lax.rsqrt(jnp.mean(oh * oh, axis=-1, keepdims=True) + GLA_NORM_EPS)
        ogh = og[:, sl]
        parts.append((on * ggn_ref[:, sl] * (ogh * _sigmoid(ogh))).astype(BF16))
    cat = jnp.concatenate(parts, axis=1)
    mx = jnp.dot(cat, wout_ref[...], preferred_element_type=F32)
    xo_ref[0] = x_res + mod_ref[0, 0][2:3] * _rms(mx, gpost_ref[...])


def _readout(y, bonus, g, o, og, ctx, x, xc, modtab, prm, latents_only):
    B, N, _ = bonus.shape
    f = 1 if latents_only else 0
    if xc is None:
        res = [ctx, x]
        res_specs = [pl.BlockSpec((1, TB, D), lambda b, j: (b, 0, 0)),
                     pl.BlockSpec((1, TB, D), lambda b, j: (b, jnp.maximum(j + f - 1, 0), 0))]
    else:
        res, res_specs = [xc], [_tile_spec(D, f)]
    return pl.pallas_call(
        functools.partial(_readout_kernel, xc is None),
        out_shape=jax.ShapeDtypeStruct((B, N - f * TB, D), F32),
        grid=(B, N // TB - f),
        in_specs=[_tile_spec(RW_WIDTH, f)] * 4 + [_tile_spec(GLA_WIDTH, f)] * 3 + res_specs + [_mod_spec(f),
                  _const_spec((1, RW_WIDTH)), _const_spec((1, RW_WIDTH)), _const_spec((1, GLA_WIDTH)),
                  _const_spec((D, D)), _const_spec((1, D)), _const_spec((RW_WIDTH, RW_WIDTH))],
        out_specs=_tile_spec(D),
        compiler_params=_cparams(("parallel", "parallel")),
        name="readout",
    )(y[0], y[1], bonus, g, o[0], o[1], og, *res, modtab, prm["gn_w"], prm["gn_b"], prm["gla_gn_w"],
      prm["w_out"], prm["norm_post"], prm["bd64"])


def _swiglu_acc(hb, wg_ref, wu_ref, wd_ref):
    acc = jnp.zeros((hb.shape[0], D), F32)
    for c in range(D_FF // FCH):
        sl = slice(c * FCH, (c + 1) * FCH)
        gate = jnp.dot(hb, wg_ref[:, sl], preferred_element_type=F32)
        up = jnp.dot(hb, wu_ref[:, sl], preferred_element_type=F32)
        act = (gate * _sigmoid(gate) * up).astype(BF16)
        acc = acc + jnp.dot(act, wd_ref[sl, :], preferred_element_type=F32)
    return acc


def _ffn_kernel(x_ref, mod_ref, gpre_ref, gpost_ref, wg_ref, wu_ref, wd_ref, xo_ref):
    x = x_ref[0]
    hb = (_rms(x, gpre_ref[...]) * (1.0 + _wide_mod(mod_ref, 4)) + _wide_mod(mod_ref, 3)).astype(BF16)
    fx = _swiglu_acc(hb, wg_ref, wu_ref, wd_ref)
    xo_ref[0] = x + _wide_mod(mod_ref, 5) * _rms(fx, gpost_ref[...])


def _single_buffered(shape):
    nd = len(shape)
    return pl.BlockSpec(shape, lambda *_: (0,) * nd, pipeline_mode=pl.Buffered(1))


def _ffn(xc, modtab, g_pre, g_post, wg, wu, wd):
    B, N, _ = xc.shape
    wide = pl.BlockSpec((1, TW, D), lambda b, j: (b, j, 0))
    return pl.pallas_call(
        _ffn_kernel,
        out_shape=jax.ShapeDtypeStruct((B, N, D), F32),
        grid=(B, N // TW),
        in_specs=[wide, _wide_mod_spec(), _const_spec((1, D)), _const_spec((1, D)),
                  _single_buffered((D, D_FF)), _single_buffered((D, D_FF)), _single_buffered((D_FF, D))],
        out_specs=wide,
        compiler_params=_cparams(("parallel", "parallel")),
        name="ffn",
    )(xc, modtab, g_pre, g_post, wg, wu, wd)


MOE_TB = 1024
MOE_SEG = 32
MOE_TM = 512
MOE_R = 2 * MOE_TB + N_EXPERTS * MOE_SEG
MOE_NP = MOE_R // MOE_SEG


def _moe_route_kernel(x_ref, mod_ref, gpre_ref, router_ref, h_ref, info_ref, infot_ref, cnt_ref):
    mod = mod_ref[0, 0]
    h = _rms(x_ref[0], gpre_ref[...]) * (1.0 + mod[4:5]) + mod[3:4]
    h_ref[0] = h.astype(BF16)
    lane = lax.broadcasted_iota(jnp.int32, (MOE_TB, LANES), 1)
    logits = jnp.dot(h, router_ref[...], precision=lax.Precision.HIGHEST, preferred_element_type=F32)
    logits = jnp.where(lane < N_EXPERTS, logits, -jnp.inf)
    v1 = jnp.max(logits, axis=-1, keepdims=True)
    i1 = jnp.min(jnp.where(logits == v1, lane, LANES), axis=-1, keepdims=True)
    rest = jnp.where(lane == i1, -jnp.inf, logits)
    v2 = jnp.max(rest, axis=-1, keepdims=True)
    i2 = jnp.min(jnp.where(rest == v2, lane, LANES), axis=-1, keepdims=True)
    ex = jnp.exp(v2 - v1)
    w1 = 1.0 / (1.0 + ex)
    w2 = ex * w1
    e1 = jnp.where(lane == i1, 1.0, 0.0)
    e2 = jnp.where(lane == i2, 1.0, 0.0)
    es = e1 + e2
    t = lax.broadcasted_iota(jnp.int32, (MOE_TB, MOE_TB), 0)
    s = lax.broadcasted_iota(jnp.int32, (MOE_TB, MOE_TB), 1)
    before = jnp.where(s < t, 1.0, 0.0).astype(BF16)
    rank = jnp.dot(before, es.astype(BF16), preferred_element_type=F32)
    cnt = jnp.sum(es, axis=0, keepdims=True)
    segs = jnp.floor((cnt + (MOE_SEG - 1)) * (1.0 / MOE_SEG))
    ea = lax.broadcasted_iota(jnp.int32, (LANES, LANES), 0)
    eb = lax.broadcasted_iota(jnp.int32, (LANES, LANES), 1)
    earlier = jnp.where(ea < eb, 1.0, 0.0).astype(BF16)
    start = jnp.dot(jnp.broadcast_to(segs, (8, LANES)).astype(BF16), earlier,
                    preferred_element_type=F32)[0:1] * MOE_SEG
    pos = rank + start
    d1 = jnp.sum(e1 * pos, axis=-1, keepdims=True)
    d2 = jnp.sum(e2 * pos, axis=-1, keepdims=True)
    info = jnp.where(lane == 0, d1, jnp.where(lane == 1, d2, jnp.where(lane == 2, w1, jnp.where(lane == 3, w2, 0.0))))
    info_ref[0] = info
    infot_ref[0] = jnp.transpose(info)[0:8]
    cnt_ref[0] = jnp.broadcast_to(cnt, (8, LANES))


def _moe_route(xs, modtab, g_pre, router):
    B, S, _ = xs.shape
    per = S // MOE_TB
    nb = B * per
    blk = lambda c: pl.BlockSpec((1, MOE_TB, c), lambda i: (i // per, i % per, 0))
    flat = lambda r, c: pl.BlockSpec((1, r, c), lambda i: (i, 0, 0))
    return pl.pallas_call(
        _moe_route_kernel,
        out_shape=[jax.ShapeDtypeStruct((nb, MOE_TB, D), BF16), jax.ShapeDtypeStruct((nb, MOE_TB, LANES), F32),
                   jax.ShapeDtypeStruct((nb, 8, MOE_TB), F32), jax.ShapeDtypeStruct((nb, 8, LANES), F32)],
        grid=(nb,),
        in_specs=[blk(D), pl.BlockSpec((1, 1, 6, D), lambda i: (i // per, 1, 0, 0)),
                  _const_spec((1, D)), _const_spec((D, LANES))],
        out_specs=[flat(MOE_TB, D), flat(MOE_TB, LANES), flat(8, MOE_TB), flat(8, LANES)],
        compiler_params=_cparams(("parallel",)),
        name="moe_route",
    )(xs, modtab, g_pre, router)


def _moe_plan(cnt, n_tiles):
    pc = (cnt + MOE_SEG - 1) // MOE_SEG * MOE_SEG
    inc = jnp.cumsum(pc, axis=1)
    loff = inc - pc
    reg = (jnp.sum(pc, axis=0) + MOE_TM - 1) // MOE_TM * MOE_TM
    gend = jnp.cumsum(reg)
    goff = (gend - reg)[None, :] + jnp.cumsum(pc, axis=0) - pc
    rows = jnp.arange(MOE_NP, dtype=jnp.int32) * MOE_SEG
    e_p = jnp.sum((rows[None, :, None] >= inc[:, None, :]).astype(jnp.int32), axis=-1)
    e_c = jnp.minimum(e_p, N_EXPERTS - 1)
    dst = jnp.take_along_axis(goff, e_c, axis=1) + rows[None, :] - jnp.take_along_axis(loff, e_c, axis=1)
    dst = jnp.where(e_p < N_EXPERTS, dst, 0).astype(jnp.int32)
    n_valid = (inc[:, -1] // MOE_SEG).astype(jnp.int32)
    trow = jnp.arange(n_tiles, dtype=jnp.int32) * MOE_TM
    te = jnp.sum((trow[:, None] >= gend[None, :]).astype(jnp.int32), axis=-1)
    valid = te < N_EXPERTS
    last = gend[-1] // MOE_TM - 1
    te = jnp.where(valid, te, te[last]).astype(jnp.int32)
    src = jnp.where(valid, jnp.arange(n_tiles, dtype=jnp.int32), last).astype(jnp.int32)
    return dst, n_valid, te, src, valid.astype(jnp.int32)


def _piece_copy(src_ref, src_row, dst_ref, dst_row, sem):
    return pltpu.make_async_copy(src_ref.at[pl.ds(src_row, MOE_SEG)], dst_ref.at[pl.ds(dst_row, MOE_SEG)], sem)


def _moe_gather_kernel(dst_ref, nv_ref, h_ref, infot_ref, xg_in_ref, xg_ref, buf_ref, sem_ref):
    del xg_in_ref
    i = pl.program_id(0)
    it = infot_ref[0]
    rr = lax.broadcasted_iota(jnp.int32, (MOE_R, MOE_TB), 0).astype(F32)
    onehot = jnp.where(rr == it[0:1], 1.0, jnp.where(rr == it[1:2], 1.0, 0.0)).astype(BF16)
    buf_ref[...] = jnp.dot(onehot, h_ref[0], preferred_element_type=F32).astype(BF16)
    nv = nv_ref[i]

    def piece(p):
        return _piece_copy(buf_ref, pl.multiple_of(p * MOE_SEG, MOE_SEG),
                           xg_ref, pl.multiple_of(dst_ref[i, p], MOE_SEG), sem_ref.at[p])

    def start(p, c):
        piece(p).start()
        return c

    def wait(p, c):
        piece(p).wait()
        return c

    lax.fori_loop(0, nv, start, 0)
    lax.fori_loop(0, nv, wait, 0)


def _moe_gather(dst, n_valid, h, info_t, n_rows):
    nb = h.shape[0]
    flat = lambda r, c: pl.BlockSpec((1, r, c), lambda i, *_: (i, 0, 0))
    return pl.pallas_call(
        _moe_gather_kernel,
        out_shape=jax.ShapeDtypeStruct((n_rows, D), BF16),
        grid_spec=pltpu.PrefetchScalarGridSpec(
            num_scalar_prefetch=2, grid=(nb,),
            in_specs=[flat(MOE_TB, D), flat(8, MOE_TB), pl.BlockSpec(memory_space=pl.ANY)],
            out_specs=pl.BlockSpec(memory_space=pl.ANY),
            scratch_shapes=[pltpu.VMEM((MOE_R, D), BF16), pltpu.SemaphoreType.DMA((MOE_NP,))]),
        input_output_aliases={4: 0},
        compiler_params=_cparams(("arbitrary",)),
        name="moe_gather",
    )(dst, n_valid, h, info_t, jnp.zeros((n_rows, D), BF16))


def _moe_ffn_kernel(te_ref, src_ref, valid_ref, x_ref, wg_ref, wu_ref, wd_ref, y_ref):
    i = pl.program_id(0)

    @pl.when(valid_ref[i] == 1)
    def _():
        y_ref[...] = _swiglu_acc(x_ref[...], wg_ref.at[0], wu_ref.at[0], wd_ref.at[0]).astype(BF16)

    @pl.when(valid_ref[i] == 0)
    def _():
        y_ref[...] = jnp.zeros_like(y_ref)


def _moe_ffn(te, src, valid, xg, wg, wu, wd):
    n_rows = xg.shape[0]
    wspec = lambda r, c: pl.BlockSpec((1, r, c), lambda i, te, src, valid: (te[i], 0, 0))
    return pl.pallas_call(
        _moe_ffn_kernel,
        out_shape=jax.ShapeDtypeStruct((n_rows, D), BF16),
        grid_spec=pltpu.PrefetchScalarGridSpec(
            num_scalar_prefetch=3, grid=(n_rows // MOE_TM,),
            in_specs=[pl.BlockSpec((MOE_TM, D), lambda i, te, src, valid: (src[i], 0)),
                      wspec(D, D_FF), wspec(D, D_FF), wspec(D_FF, D)],
            out_specs=pl.BlockSpec((MOE_TM, D), lambda i, te, src, valid: (i, 0))),
        compiler_params=_cparams(("arbitrary",)),
        name="moe_ffn",
    )(te, src, valid, xg, wg, wu, wd)


def _moe_combine_kernel(dst_ref, nv_ref, x_ref, mod_ref, gpost_ref, info_ref, yg_ref, xo_ref, buf_ref, sem_ref):
    i = pl.program_id(0)
    nv = nv_ref[i]

    def piece(p):
        return _piece_copy(yg_ref, pl.multiple_of(dst_ref[i, p], MOE_SEG),
                           buf_ref, pl.multiple_of(p * MOE_SEG, MOE_SEG), sem_ref.at[p])

    def start(p, c):
        piece(p).start()
        return c

    def clear(p, c):
        buf_ref[pl.ds(pl.multiple_of(p * MOE_SEG, MOE_SEG), MOE_SEG), :] = jnp.zeros((MOE_SEG, D), BF16)
        return c

    def wait(p, c):
        piece(p).wait()
        return c

    lax.fori_loop(0, nv, start, 0)
    lax.fori_loop(nv, MOE_NP, clear, 0)
    info = info_ref[0]
    rr = lax.broadcasted_iota(jnp.int32, (MOE_TB, MOE_R), 1).astype(F32)
    comb = jnp.where(rr == info[:, 0:1], info[:, 2:3], jnp.where(rr == info[:, 1:2], info[:, 3:4], 0.0)).astype(BF16)
    lax.fori_loop(0, nv, wait, 0)
    fx = jnp.dot(comb, buf_ref[...], preferred_element_type=F32)
    xo_ref[0] = x_ref[0] + mod_ref[0, 0][5:6] * _rms(fx, gpost_ref[...])


def _moe_combine(dst, n_valid, xs, modtab, g_post, info, yg):
    B, S, _ = xs.shape
    per = S // MOE_TB
    blk = pl.BlockSpec((1, MOE_TB, D), lambda i, *_: (i // per, i % per, 0))
    return pl.pallas_call(
        _moe_combine_kernel,
        out_shape=jax.ShapeDtypeStruct((B, S, D), F32),
        grid_spec=pltpu.PrefetchScalarGridSpec(
            num_scalar_prefetch=2, grid=(B * per,),
            in_specs=[blk, pl.BlockSpec((1, 1, 6, D), lambda i, *_: (i // per, 1, 0, 0)),
                      pl.BlockSpec((1, D), lambda i, *_: (0, 0)),
                      pl.BlockSpec((1, MOE_TB, LANES), lambda i, *_: (i, 0, 0)),
                      pl.BlockSpec(memory_space=pl.ANY)],
            out_specs=blk,
            scratch_shapes=[pltpu.VMEM((MOE_R, D), BF16), pltpu.SemaphoreType.DMA((MOE_NP,))]),
        compiler_params=_cparams(("arbitrary",)),
        name="moe_combine",
    )(dst, n_valid, xs, modtab, g_post, info, yg)


def _moe(xs, modtab, g_pre, g_post, router, wg, wu, wd):
    B, S, _ = xs.shape
    nb = B * S // MOE_TB
    h, info, info_t, cnt = _moe_route(xs, modtab, g_pre, router)
    worst = 2 * B * S + nb * N_EXPERTS * (MOE_SEG - 1) + N_EXPERTS * (MOE_TM - 1)
    n_tiles = -(-worst // MOE_TM)
    dst, n_valid, te, src, valid = _moe_plan(cnt[:, 0, :N_EXPERTS].astype(jnp.int32), n_tiles)
    xg = _moe_gather(dst, n_valid, h, info_t, n_tiles * MOE_TM)
    yg = _moe_ffn(te, src, valid, xg, wg, wu, wd)
    return _moe_combine(dst, n_valid, xs, modtab, g_post, info, yg)


def _block_diag2(w):
    z = jnp.zeros_like(w[0])
    return jnp.concatenate([jnp.concatenate([w[0], z], axis=1), jnp.concatenate([z, w[1]], axis=1)], axis=0)


def _row(v):
    return v.reshape(1, -1).astype(F32)


def _head_ones(width, head):
    i = jnp.arange(width) // head
    return (i[:, None] == i[None, :]).astype(BF16)


def kernel(x, c, ctx, c_ctx, ada_w, ada_b, norm_mix_pre, norm_mix_post, norm_ffn_pre, norm_ffn_post, w_in, shift_mu, rw_w_up, rw_w0, rw_a_up, rw_a0, rw_k_k, rw_k_a, rw_r_k, rw_g_up, rw_gn_w, rw_gn_b, rw_v_down, rw_v_up, rw_v0, gla_conv, gla_a_up, gla_a_b, gla_gn_w, w_out, ffn_w_gate, ffn_w_up, ffn_w_down, moe_router, moe_w_gate, moe_w_up, moe_w_down):
    B, S, _ = x.shape
    n_ctx = ctx.shape[1]
    depth = w_in.shape[0]
    assert n_ctx == TB and S % MOE_TB == 0 and (n_ctx + S) % TW == 0 and depth == 2

    xc = None
    cvec = jnp.concatenate([c, c_ctx[None, :], jnp.zeros((16 - B - 1, D), F32)], axis=0)
    bd64 = _head_ones(RW_WIDTH, RW_HEAD)
    ada_b3 = ada_b.reshape(depth, 1, 6 * D)
    v_first = None
    out = None
    for i in range(depth):
        last = i == depth - 1
        mods = _adaln(cvec, ada_w, ada_b3, i)
        mod_x = mods[:B].reshape(B, 6, D)
        mod_c = jnp.broadcast_to(mods[B].reshape(1, 6, D), (B, 6, D))
        modtab = jnp.stack([mod_c, mod_x], axis=1)

        w_i = jnp.concatenate([w_in[i], jnp.zeros((D, GLA_PAD - GLA_COLS), F32)], axis=1).astype(BF16)
        p_rw, p_gl = _inproj(ctx, x, xc, _row(norm_mix_pre[i]), modtab, w_i)

        prm = dict(
            mu=_row(shift_mu[i]),
            w_up=_block_diag2(rw_w_up[i]).astype(BF16), w0=_row(rw_w0[i]),
            a_up=_block_diag2(rw_a_up[i]).astype(BF16), a0=_row(rw_a0[i]),
            k_k=_row(rw_k_k[i]), k_a=_row(rw_k_a[i]), r_k=_row(rw_r_k[i]),
            g_up=rw_g_up[i].astype(BF16), bd64=bd64,
            gn_w=_row(rw_gn_w[i]), gn_b=_row(rw_gn_b[i]), gla_gn_w=_row(gla_gn_w[i]),
            w_out=w_out[i].astype(BF16), norm_post=_row(norm_mix_post[i]),
        )
        gate_pad = jnp.zeros((LANES - 2 * GLA_GATE_RANK, 2 * GLA_KW), F32)
        gla_prm = dict(conv=gla_conv[i].astype(F32), a_b=_row(gla_a_b[i]),
                       a_up=jnp.concatenate([_block_diag2(gla_a_up[i]), gate_pad], axis=0).astype(BF16))
        if i > 0:
            pad = LANES - RW_V_RANK
            prm["v_down"] = jnp.concatenate([rw_v_down[i - 1], jnp.zeros((RW_WIDTH, pad), F32)], axis=1).astype(BF16)
            prm["v_up"] = jnp.concatenate([rw_v_up[i - 1], jnp.zeros((pad, RW_WIDTH), F32)], axis=0).astype(BF16)
            prm["v0"] = _row(rw_v0[i - 1])

        r, kk, vm, g, bonus, lw, bb, ke = _rwprep(p_rw, prm, v_first if i > 0 else None)
        if i == 0:
            v_first = vm
        y = _rwscan(r, kk, vm, lw, bb, ke)
        q, k, gv, og, lg = _glaprep(p_gl, gla_prm)
        o = _glascan(q, k, gv, lg)
        xc = _readout(y, bonus, g, o, og, ctx, x, xc, modtab, prm, latents_only=last)

        jf = i // 2
        if i % 2 == 0:
            xc = _ffn(xc, modtab, _row(norm_ffn_pre[i]), _row(norm_ffn_post[i]),
                      ffn_w_gate[jf].astype(BF16), ffn_w_up[jf].astype(BF16), ffn_w_down[jf].astype(BF16))
        else:
            router = jnp.concatenate([moe_router[jf], jnp.zeros((D, LANES - N_EXPERTS), F32)], axis=1)
            out = _moe(xc, modtab, _row(norm_ffn_pre[i]), _row(norm_ffn_post[i]), router,
                       moe_w_gate[jf].astype(BF16), moe_w_up[jf].astype(BF16), moe_w_down[jf].astype(BF16))
    return out
```

```python
import functools
import math

import jax
import jax.numpy as jnp
from jax import lax
from jax.experimental import pallas as pl
from jax.experimental.pallas import tpu as pltpu

F32, BF16 = jnp.float32, jnp.bfloat16
ACT = BF16

D = 1024
GRID_W = 64
RW_WIDTH = 512
RW_HEAD = 64
RW_RANK = 64
RW_G_RANK = 128
RW_V_RANK = 32
RW_GN_EPS = 64e-5
GLA_WIDTH = 512
GLA_HEADS = 4
GLA_DV = 128
GLA_DK = 64
GLA_KW = 256
GLA_GATE_RANK = 16
GLA_GATE_NORM = 16.0
GLA_NORM_EPS = 1e-5
D_FF = 2816
N_EXPERTS = 8
NORM_EPS = 1e-6
RW_COLS = 3 * RW_WIDTH + 4 * RW_RANK + RW_G_RANK
GLA_QKV = 2 * GLA_KW + GLA_WIDTH
GLA_COLS = GLA_QKV + GLA_WIDTH + 2 * GLA_GATE_RANK
GLA_PAD = 1664

LANES = 128
TB = 256
TW = 3 * TB
CH = 64
FCH = 256
HALO = 16
VMEM_LIMIT = 56 * 1024 * 1024


def _cparams(sem):
    return pltpu.CompilerParams(dimension_semantics=sem, vmem_limit_bytes=VMEM_LIMIT)


def _bdot(a, b):
    return jnp.dot(a.astype(BF16), b.astype(BF16), preferred_element_type=F32)


def _bdot_nt(a, b):
    return lax.dot_general(a.astype(BF16), b.astype(BF16), (((1,), (1,)), ((), ())),
                           preferred_element_type=F32)


def _bdot_tn(a, b):
    return lax.dot_general(a.astype(BF16), b.astype(BF16), (((0,), (0,)), ((), ())),
                           preferred_element_type=F32)


def _split_dot(a_exact, x):
    h1 = x.astype(BF16)
    r1 = x - h1.astype(F32)
    h2 = r1.astype(BF16)
    h3 = (r1 - h2.astype(F32)).astype(BF16)
    return (jnp.dot(a_exact, h1, preferred_element_type=F32)
            + jnp.dot(a_exact, h2, preferred_element_type=F32)
            + jnp.dot(a_exact, h3, preferred_element_type=F32))


def _seg_sum(x, bd, exact=True):
    hi = x.astype(BF16)
    out = jnp.dot(hi, bd, preferred_element_type=F32)
    if exact:
        lo = (x - hi.astype(F32)).astype(BF16)
        out = out + jnp.dot(lo, bd, preferred_element_type=F32)
    return out


def _sigmoid(x):
    return jax.nn.sigmoid(x)


def _rms(x, g):
    return x * lax.rsqrt(jnp.mean(x * x, axis=-1, keepdims=True) + NORM_EPS) * g


def _pair_stack(z, lo):
    return jnp.concatenate([jnp.where(lo, z, 0.0), jnp.where(lo, 0.0, z)], axis=0)


def _adaln_kernel(c_ref, w_ref, b_ref, o_ref):
    c = c_ref[...]
    s = c * _sigmoid(c)
    o_ref[...] = jnp.dot(s, w_ref[0], precision=lax.Precision.HIGHEST,
                         preferred_element_type=F32) + b_ref[0]


def _adaln(cvec, w, b, layer):
    rows = cvec.shape[0]
    n = w.shape[2]
    return pl.pallas_call(
        _adaln_kernel,
        out_shape=jax.ShapeDtypeStruct((rows, n), F32),
        grid=(n // D,),
        in_specs=[pl.BlockSpec((rows, D), lambda i: (0, 0)),
                  pl.BlockSpec((1, D, D), lambda i: (layer, 0, i)),
                  pl.BlockSpec((1, 1, D), lambda i: (layer, 0, i))],
        out_specs=pl.BlockSpec((rows, D), lambda i: (0, i)),
        compiler_params=_cparams(("arbitrary",)),
        name="adaln",
    )(cvec, w, b)


def _wide_mod(mod_ref, k):
    is_ctx = jnp.logical_and(pl.program_id(1) == 0, lax.broadcasted_iota(jnp.int32, (TW, 1), 0) < TB)
    return jnp.where(is_ctx, mod_ref[0, 0, k:k + 1], mod_ref[0, 1, k:k + 1])


def _wide_rows(refs):
    if len(refs) == 1:
        return refs[0][0]
    first = jnp.where(pl.program_id(1) == 0, refs[0][0], refs[1][0])
    return jnp.concatenate([first] + [r[0] for r in refs[2:]], axis=0)


def _inproj_kernel(n_src, *refs):
    g_ref, mod_ref, w_ref, prw_ref, pgl_ref = refs[n_src:]
    x = _wide_rows(refs[:n_src])
    h = _rms(x, g_ref[...]) * (1.0 + _wide_mod(mod_ref, 1)) + _wide_mod(mod_ref, 0)
    hb = h.astype(BF16)
    prw_ref[0] = jnp.dot(hb, w_ref[:, :RW_COLS], preferred_element_type=F32).astype(ACT)
    pgl_ref[0] = jnp.dot(hb, w_ref[:, RW_COLS:], preferred_element_type=F32).astype(ACT)


def _wide_src(ctx, x, xc):
    if xc is not None:
        return [xc], [pl.BlockSpec((1, TW, D), lambda b, j: (b, j, 0))]
    per = TW // TB
    specs = [pl.BlockSpec((1, TB, D), lambda b, j: (b, 0, 0))]
    for k in range(per):
        specs.append(pl.BlockSpec((1, TB, D), lambda b, j, k=k: (b, jnp.maximum(j * per + k - 1, 0), 0)))
    return [ctx] + [x] * per, specs


def _wide_mod_spec():
    return pl.BlockSpec((1, 2, 6, D), lambda b, j: (b, 0, 0, 0))


def _const_spec(shape):
    nd = len(shape)
    return pl.BlockSpec(shape, lambda *_: (0,) * nd)


def _tile_spec(c, first=0):
    return pl.BlockSpec((1, TB, c), lambda b, j: (b, j + first, 0))


def _dir_tile_spec(c):
    return pl.BlockSpec((2, 1, TB, c), lambda b, j: (0, b, j, 0))


def _mod_spec(first=0):
    return pl.BlockSpec((1, 1, 6, D), lambda b, j: (b, jnp.minimum(j + first, 1), 0, 0))


def _inproj(ctx, x, xc, g, modtab, w):
    B = modtab.shape[0]
    N = xc.shape[1] if xc is not None else ctx.shape[1] + x.shape[1]
    srcs, src_specs = _wide_src(ctx, x, xc)
    wide = lambda c: pl.BlockSpec((1, TW, c), lambda b, j: (b, j, 0))
    return pl.pallas_call(
        functools.partial(_inproj_kernel, len(srcs)),
        out_shape=[jax.ShapeDtypeStruct((B, N, RW_COLS), ACT),
                   jax.ShapeDtypeStruct((B, N, GLA_PAD), ACT)],
        grid=(B, N // TW),
        in_specs=src_specs + [_const_spec((1, D)), _wide_mod_spec(), _const_spec((D, RW_COLS + GLA_PAD))],
        out_specs=[wide(RW_COLS), wide(GLA_PAD)],
        compiler_params=_cparams(("parallel", "parallel")),
        name="inproj",
    )(*srcs, g, modtab, w)


def _shift_table():
    t = jnp.arange(TB)
    prev = t[:, None] - 1 == t[None, :]
    nxt = t[:, None] + 1 == t[None, :]
    col = (t % GRID_W)[:, None]
    return jnp.stack([jnp.stack([prev, nxt]),
                      jnp.stack([prev & (col != 0), nxt & (col != GRID_W - 1)])]).astype(BF16)


def _rwprep_kernel(has_vres, n_tiles, p_ref, hu_ref, hd_ref, sh_ref, mu_ref, wup_ref, w0_ref, aup_ref,
                   a0_ref, kk_ref, ka_ref, rk_ref, gup_ref, bd_ref, *rest):
    if has_vres:
        vf_ref, vdn_ref, vup_ref, v0_ref = rest[:4]
        rest = rest[4:]
    r_out, kk_out, v_out, g_out, bon_out, lw_out, b_out, ke_out = rest
    j = pl.program_id(1)
    pb = p_ref[0]
    p = pb.astype(F32)
    prev1 = jnp.dot(sh_ref[0, 0], pb, preferred_element_type=F32)
    next1 = jnp.dot(sh_ref[0, 1], pb, preferred_element_type=F32)
    has_upper = jnp.where(j == 1, 0.0, 1.0)
    has_lower = jnp.where(j == n_tiles - 1, 0.0, 1.0)
    up = jnp.concatenate([hu_ref[0].astype(F32) * has_upper, p[:TB - GRID_W]], axis=0)
    down = jnp.concatenate([p[GRID_W:], hd_ref[0].astype(F32) * has_lower], axis=0)
    cls = lax.broadcasted_iota(jnp.int32, p.shape, 1) & jnp.where(j == 0, 1, 3)
    shifted = jnp.where(cls == 0, prev1, jnp.where(cls == 1, next1, jnp.where(cls == 2, up, down)))
    u = p + mu_ref[...] * (shifted - p)

    r = u[:, 0:RW_WIDTH]
    k = u[:, RW_WIDTH:2 * RW_WIDTH]
    v = u[:, 2 * RW_WIDTH:3 * RW_WIDTH]
    o = 3 * RW_WIDTH
    wd = u[:, o:o + 2 * RW_RANK]
    ad = u[:, o + 2 * RW_RANK:o + 4 * RW_RANK]
    gd = u[:, o + 4 * RW_RANK:]

    w_logit = w0_ref[...] + _bdot(jnp.tanh(wd), wup_ref[...])
    lw = -math.exp(-0.5) * _sigmoid(w_logit)
    a = _sigmoid(a0_ref[...] + _bdot(ad, aup_ref[...]))
    bd = bd_ref[...]
    kk = k * kk_ref[...]
    kk = kk * lax.rsqrt(jnp.maximum(_seg_sum(kk * kk, bd), 1e-24))
    g = _bdot(_sigmoid(gd), gup_ref[...])
    if has_vres:
        gate = _sigmoid(v0_ref[...] + _bdot(_bdot(v, vdn_ref[...]), vup_ref[...]))
        vm = v + (vf_ref[0].astype(F32) - v) * gate
    else:
        vm = v
    ke_sum = jnp.zeros_like(k)
    for d in range(2):
        a_d = a[:, d * RW_WIDTH:(d + 1) * RW_WIDTH]
        ke_d = k * (1.0 + (a_d - 1.0) * ka_ref[...])
        lw_out[d, 0] = lw[:, d * RW_WIDTH:(d + 1) * RW_WIDTH]
        b_out[d, 0] = (kk * a_d).astype(ACT)
        ke_out[d, 0] = ke_d.astype(ACT)
        ke_sum = ke_sum + ke_d
    r_out[0] = r.astype(ACT)
    kk_out[0] = kk.astype(ACT)
    v_out[0] = vm.astype(ACT)
    g_out[0] = g.astype(ACT)
    bon_out[0] = (_seg_sum(r * ke_sum * rk_ref[...], bd, exact=False) * vm).astype(ACT)


def _rwprep(p_rw, prm, v_first):
    B, N, _ = p_rw.shape
    nt = N // TB
    hb = TB // GRID_W
    nhb = N // GRID_W
    has_vres = v_first is not None
    W2 = 2 * RW_WIDTH
    in_specs = [
        _tile_spec(RW_COLS),
        pl.BlockSpec((1, GRID_W, RW_COLS), lambda b, j: (b, jnp.maximum(j * hb - 1, 0), 0)),
        pl.BlockSpec((1, GRID_W, RW_COLS), lambda b, j: (b, jnp.minimum(j * hb + hb, nhb - 1), 0)),
        pl.BlockSpec((1, 2, TB, TB), lambda b, j: (jnp.minimum(j, 1), 0, 0, 0)),
        _const_spec((1, RW_COLS)), _const_spec((2 * RW_RANK, W2)), _const_spec((1, W2)),
        _const_spec((2 * RW_RANK, W2)), _const_spec((1, W2)),
        _const_spec((1, RW_WIDTH)), _const_spec((1, RW_WIDTH)), _const_spec((1, RW_WIDTH)),
        _const_spec((RW_G_RANK, RW_WIDTH)), _const_spec((RW_WIDTH, RW_WIDTH)),
    ]
    args = [p_rw, p_rw, p_rw, _shift_table(), prm["mu"], prm["w_up"], prm["w0"], prm["a_up"], prm["a0"],
            prm["k_k"], prm["k_a"], prm["r_k"], prm["g_up"], prm["bd64"]]
    if has_vres:
        in_specs += [_tile_spec(RW_WIDTH), _const_spec((RW_WIDTH, LANES)), _const_spec((LANES, RW_WIDTH)),
                     _const_spec((1, RW_WIDTH))]
        args += [v_first, prm["v_down"], prm["v_up"], prm["v0"]]
    tok = jax.ShapeDtypeStruct((B, N, RW_WIDTH), ACT)
    dtok = jax.ShapeDtypeStruct((2, B, N, RW_WIDTH), ACT)
    return pl.pallas_call(
        functools.partial(_rwprep_kernel, has_vres, nt),
        out_shape=[tok] * 5 + [jax.ShapeDtypeStruct((2, B, N, RW_WIDTH), F32), dtok, dtok],
        grid=(B, nt),
        in_specs=in_specs,
        out_specs=[_tile_spec(RW_WIDTH)] * 5 + [_dir_tile_spec(RW_WIDTH)] * 3,
        compiler_params=_cparams(("parallel", "parallel")),
        name="rwprep",
    )(*args)


NCH = TB // CH


def _chunk_tri(rev):
    t = lax.broadcasted_iota(jnp.int32, (TB, TB), 0)
    s = lax.broadcasted_iota(jnp.int32, (TB, TB), 1)
    order = (s >= t) if rev else (s <= t)
    return jnp.where(jnp.logical_and(t // CH == s // CH, order), 1.0, 0.0).astype(BF16)


def _chunk_totals(x):
    tots = [jnp.sum(x[c * CH:(c + 1) * CH], axis=0, keepdims=True) for c in range(NCH)]
    full = jnp.concatenate([jnp.broadcast_to(t, (CH, x.shape[1])) for t in tots], axis=0)
    return tots, full


def _scan_masks(rev):
    ti = lax.broadcasted_iota(jnp.int32, (CH, LANES), 0)
    li = lax.broadcasted_iota(jnp.int32, (CH, LANES), 1)
    si = li & (CH - 1)
    incl = (si >= ti) if rev else (si <= ti)
    strict = (si > ti) if rev else (si < ti)
    return incl, strict


def _rwscan_kernel(rf_ref, kkf_ref, vf_ref, lwf_ref, bf_ref, kef_ref,
                   rb_ref, kkb_ref, vb_ref, lwb_ref, bb_ref, keb_ref, yf_ref, yb_ref, st_ref):
    j = pl.program_id(1)

    @pl.when(j == 0)
    def _():
        st_ref[...] = jnp.zeros_like(st_ref)

    ti = lax.broadcasted_iota(jnp.int32, (CH, LANES), 0)
    li = lax.broadcasted_iota(jnp.int32, (CH, LANES), 1)
    si = li & (CH - 1)
    lo = li < CH
    eye = jnp.where(ti == si, 1.0, 0.0)
    rblk = lax.broadcasted_iota(jnp.int32, (LANES, LANES), 0) // CH
    cblk = lax.broadcasted_iota(jnp.int32, (LANES, LANES), 1) // CH
    bdmask = rblk == cblk
    t64 = lax.broadcasted_iota(jnp.int32, (CH, CH), 0)
    s64 = lax.broadcasted_iota(jnp.int32, (CH, CH), 1)
    tri = [jnp.where(s64 <= t64, 1.0, 0.0).astype(BF16), jnp.where(s64 >= t64, 1.0, 0.0).astype(BF16)]
    m_incl = [si <= ti, si >= ti]
    m_strict = [si < ti, si > ti]
    n_pairs = RW_WIDTH // LANES
    in_refs = ((rf_ref, kkf_ref, vf_ref, lwf_ref, bf_ref, kef_ref),
               (rb_ref, kkb_ref, vb_ref, lwb_ref, bb_ref, keb_ref))
    y_refs = (yf_ref, yb_ref)

    def stack(z):
        return _pair_stack(z, lo).astype(BF16)

    st = {(d, p): st_ref[d, p] for d in range(2) for p in range(n_pairs)}

    def scan_step(step):
        cur = {}
        w_tot = {}
        for d in range(2):
            c = (NCH - 1 - step) if d == 1 else step
            rows = slice(c * CH, (c + 1) * CH)
            refs = in_refs[d]
            r, kk, v = (refs[i][0, rows, :].astype(F32) for i in range(3))
            lw = refs[3][0, 0, rows, :]
            b, ke = refs[4][0, 0, rows, :].astype(F32), refs[5][0, 0, rows, :].astype(F32)
            cum = _split_dot(tri[d], lw)
            tot = jnp.sum(lw, axis=0, keepdims=True)
            w_inv = jnp.exp(-cum)
            w_end = jnp.exp(tot - cum)
            w_tot[d] = jnp.exp(tot)
            rh = r * jnp.exp(cum)
            ah = -(kk * jnp.exp(cum - lw))
            bh, kh, bt, kt = b * w_inv, ke * w_inv, b * w_end, ke * w_end
            for p in range(n_pairs):
                sl = slice(LANES * p, LANES * (p + 1))
                cur[d, p] = dict(ah=ah[:, sl], rh=rh[:, sl], bh=bh[:, sl], kh=kh[:, sl], bt=bt[:, sl],
                                 kt=kt[:, sl], v=v[:, sl], rows=rows, sl=sl)
        yield
        for it in cur.values():
            it["lhs"] = jnp.concatenate([it["ah"], it["rh"]], axis=0).astype(BF16)
            rhs = jnp.concatenate([stack(it["bh"]), stack(it["kh"])], axis=0)
            it["a_all"] = _bdot_nt(it["lhs"], rhs)
        yield
        for (d, _), it in cur.items():
            a_all = it.pop("a_all")
            it["a_ab"] = jnp.where(m_strict[d], a_all[:CH, :LANES], 0.0)
            a_ak = jnp.where(m_strict[d], a_all[:CH, LANES:], 0.0)
            a_rb = jnp.where(m_incl[d], a_all[CH:, :LANES], 0.0)
            a_rk = jnp.where(m_incl[d], a_all[CH:, LANES:], 0.0)
            it["a_r"] = jnp.concatenate([a_rb, a_rk], axis=1).astype(BF16)
            it["v_bd"] = stack(it["v"])
            it["akv"] = _bdot(a_ak, it["v_bd"])
            it["t"] = eye + it["a_ab"]
            it["m"] = _bdot(it["a_ab"], stack(it["a_ab"]))
        yield
        for _ in range(int(math.log2(CH)) - 2):
            for it in cur.values():
                z = _bdot(jnp.concatenate([it["m"], it["t"]], axis=0), stack(it["m"]))
                it["m"] = z[:CH]
                it["t"] = it["t"] + z[CH:]
            yield
        for it in cur.values():
            it["t"] = (it["t"] + _bdot(it["t"], stack(it["m"]))).astype(BF16)
        yield
        for it in cur.values():
            a_til = jnp.dot(it["t"], stack(it["ah"]), preferred_element_type=F32)
            it["lhs2"] = jnp.concatenate([a_til.astype(BF16), it["lhs"][CH:]], axis=0)
            it["cc"] = jnp.dot(it["t"], stack(it["akv"]), preferred_element_type=F32)
            it["rhs_t"] = jnp.concatenate([it["bt"], it["kt"]], axis=0).astype(BF16)
        yield
        z1 = {k: _bdot_nt(it["lhs2"], st[k]) for k, it in cur.items()}
        yield
        u = {k: z1[k][:CH] + it["cc"] for k, it in cur.items()}
        upd = {k: _bdot_tn(jnp.concatenate([u[k], it["v"]], axis=0), it["rhs_t"]) for k, it in cur.items()}
        yield
        for (d, p), it in cur.items():
            y = z1[d, p][CH:] + jnp.dot(it["a_r"], jnp.concatenate([stack(u[d, p]), it["v_bd"]], axis=0),
                                        preferred_element_type=F32)
            y_refs[d][0, it["rows"], it["sl"]] = y.astype(ACT)
            st[d, p] = st[d, p] * w_tot[d][:, it["sl"]] + jnp.where(bdmask, upd[d, p], 0.0)
        yield

    n_stages = 12
    n_dep = 3
    pipeline = [scan_step(s) for s in range(NCH)]
    for slot in range(n_stages + n_dep * (NCH - 1)):
        for s, gen in enumerate(pipeline):
            if 0 <= slot - n_dep * s < n_stages:
                next(gen)
    for (d, p), s in st.items():
        st_ref[d, p] = s


def _bidir_specs(c, n_tiles):
    def back(j):
        return jnp.where(j == 0, 0, n_tiles - j)

    tok_f = pl.BlockSpec((1, TB, c), lambda b, j: (b, j, 0))
    tok_b = pl.BlockSpec((1, TB, c), lambda b, j: (b, back(j), 0))
    dir_f = pl.BlockSpec((1, 1, TB, c), lambda b, j: (0, b, j, 0))
    dir_b = pl.BlockSpec((1, 1, TB, c), lambda b, j: (1, b, back(j), 0))
    return tok_f, tok_b, dir_f, dir_b


def _rwscan(r, kk, v, lw, b, ke):
    B, N, _ = r.shape
    tok_f, tok_b, dir_f, dir_b = _bidir_specs(RW_WIDTH, N // TB)
    out = jax.ShapeDtypeStruct((B, N, RW_WIDTH), ACT)
    return pl.pallas_call(
        _rwscan_kernel,
        out_shape=[out, out],
        grid=(B, N // TB),
        in_specs=[tok_f, tok_f, tok_f, dir_f, dir_f, dir_f, tok_b, tok_b, tok_b, dir_b, dir_b, dir_b],
        out_specs=[tok_f, tok_b],
        scratch_shapes=[pltpu.VMEM((2, RW_WIDTH // LANES, LANES, LANES), F32)],
        compiler_params=_cparams(("parallel", "arbitrary")),
        name="rwscan",
    )(r, kk, v, lw, b, ke, r, kk, v, lw, b, ke)


CONV_K = TB + LANES


def _conv_shift_table():
    t = jnp.arange(TB)[:, None]
    s = jnp.arange(CONV_K)[None, :]
    prev = jnp.where(t == 0, s == TB + 2 * HALO - 1, s == t - 1)
    nxt = jnp.where(t == TB - 1, s == TB, s == t + 1)
    return jnp.stack([prev, nxt]).astype(BF16)


def _glaprep_kernel(n_tiles, p_ref, hp_ref, hn_ref, sh_ref, cw_ref, aup_ref, ab_ref, q_out, k_out, v_out,
                    og_out, lg_out):
    j = pl.program_id(1)
    ub = p_ref[0][:, :GLA_QKV]
    u = ub.astype(F32)
    has_prev = jnp.where(j <= 1, 0.0, 1.0)
    has_next = jnp.where(jnp.logical_or(j == 0, j == n_tiles - 1), 0.0, 1.0)
    ext = jnp.concatenate([ub,
                           (hn_ref[0][:, :GLA_QKV].astype(F32) * has_next).astype(BF16),
                           (hp_ref[0][:, :GLA_QKV].astype(F32) * has_prev).astype(BF16),
                           jnp.zeros((CONV_K - TB - 2 * HALO, GLA_QKV), BF16)], axis=0)
    prev1 = jnp.dot(sh_ref[0], ext, preferred_element_type=F32)
    next1 = jnp.dot(sh_ref[1], ext, preferred_element_type=F32)
    cw = cw_ref[...]
    conv = cw[0:1] * prev1 + cw[1:2] * u + cw[2:3] * next1
    qkv = conv * _sigmoid(conv)
    q_out[0] = (qkv[:, :GLA_KW] * (GLA_DK ** -0.5)).astype(ACT)
    k_out[0] = qkv[:, GLA_KW:2 * GLA_KW].astype(ACT)
    v_out[0] = qkv[:, 2 * GLA_KW:].astype(ACT)
    og_out[0] = p_ref[0][:, GLA_QKV:GLA_QKV + GLA_WIDTH]
    z = _bdot(p_ref[0][:, GLA_QKV + GLA_WIDTH:], aup_ref[...]) + ab_ref[...]
    lg = (jnp.minimum(z, 0.0) - jnp.log1p(jnp.exp(-jnp.abs(z)))) * (1.0 / GLA_GATE_NORM)
    for d in range(2):
        lg_out[d, 0] = lg[:, d * GLA_KW:(d + 1) * GLA_KW]


def _glaprep(p_gl, prm):
    B, N, _ = p_gl.shape
    nt = N // TB
    sub = HALO
    hb = TB // sub
    nhb = N // sub
    return pl.pallas_call(
        functools.partial(_glaprep_kernel, nt),
        out_shape=[jax.ShapeDtypeStruct((B, N, GLA_KW), ACT), jax.ShapeDtypeStruct((B, N, GLA_KW), ACT),
                   jax.ShapeDtypeStruct((B, N, GLA_WIDTH), ACT), jax.ShapeDtypeStruct((B, N, GLA_WIDTH), ACT),
                   jax.ShapeDtypeStruct((2, B, N, GLA_KW), F32)],
        grid=(B, nt),
        in_specs=[_tile_spec(GLA_PAD),
                  pl.BlockSpec((1, sub, GLA_PAD), lambda b, j: (b, jnp.maximum(j * hb - 1, 0), 0)),
                  pl.BlockSpec((1, sub, GLA_PAD), lambda b, j: (b, jnp.minimum(j * hb + hb, nhb - 1), 0)),
                  _const_spec((2, TB, CONV_K)),
                  _const_spec((3, GLA_QKV)), _const_spec((LANES, 2 * GLA_KW)), _const_spec((1, 2 * GLA_KW))],
        out_specs=[_tile_spec(GLA_KW), _tile_spec(GLA_KW), _tile_spec(GLA_WIDTH), _tile_spec(GLA_WIDTH),
                   _dir_tile_spec(GLA_KW)],
        compiler_params=_cparams(("parallel", "parallel")),
        name="glaprep",
    )(p_gl, p_gl, p_gl, _conv_shift_table(), prm["conv"], prm["a_up"], prm["a_b"])


def _glascan_kernel(qf_ref, kf_ref, vf_ref, lgf_ref, qb_ref, kb_ref, vb_ref, lgb_ref, of_ref, ob_ref, st_ref):
    j = pl.program_id(1)

    @pl.when(j == 0)
    def _():
        st_ref[...] = jnp.zeros_like(st_ref)

    li = lax.broadcasted_iota(jnp.int32, (CH, LANES), 1)
    lo = li < CH
    lo2 = lax.broadcasted_iota(jnp.int32, (LANES, LANES), 1) < CH
    zeros_v = jnp.zeros((CH, GLA_DV), F32)
    n_pairs = GLA_KW // LANES

    items = {}
    decs = {}
    for d, refs in enumerate(((qf_ref, kf_ref, vf_ref, lgf_ref), (qb_ref, kb_ref, vb_ref, lgb_ref))):
        rev = d == 1
        q, k, v, lg = refs[0][0].astype(F32), refs[1][0].astype(F32), refs[2][0].astype(F32), refs[3][0, 0]
        cum = _split_dot(_chunk_tri(rev), lg)
        tots, tot_full = _chunk_totals(lg)
        qd = q * jnp.exp(cum)
        ki = k * jnp.exp(-cum)
        kend = k * jnp.exp(tot_full - cum)
        m_incl, _ = _scan_masks(rev)
        for c in range(NCH):
            rows = slice(c * CH, (c + 1) * CH)
            decs[d, c] = jnp.exp(tots[c])
            for p in range(n_pairs):
                sl = slice(LANES * p, LANES * (p + 1))
                items[d, c, p] = dict(qd=qd[rows, sl].astype(BF16), ki=ki[rows, sl], kend=kend[rows, sl],
                                      v0=v[rows, 2 * LANES * p:2 * LANES * p + LANES],
                                      v1=v[rows, 2 * LANES * p + LANES:2 * LANES * (p + 1)], m_incl=m_incl)

    for it in items.values():
        it["att"] = _bdot_nt(it["qd"], _pair_stack(it["ki"], lo))
    for it in items.values():
        att = jnp.where(it["m_incl"], it["att"], 0.0)
        v_bd = jnp.concatenate([jnp.concatenate([it["v0"], zeros_v], axis=1),
                                jnp.concatenate([zeros_v, it["v1"]], axis=1)], axis=0)
        it["o"] = _bdot(att, v_bd)
        it["upd"] = _bdot_tn(jnp.concatenate([it["v0"], it["v1"]], axis=0), _pair_stack(it["kend"], lo))

    for d in range(2):
        for p in range(n_pairs):
            st = st_ref[d, p]
            for step in range(NCH):
                c = (NCH - 1 - step) if d == 1 else step
                it = items[d, c, p]
                st_bd = jnp.concatenate([jnp.where(lo2, st, 0.0), jnp.where(lo2, 0.0, st)], axis=0)
                o = it["o"] + _bdot_nt(it["qd"], st_bd)
                (ob_ref if d == 1 else of_ref)[0, c * CH:(c + 1) * CH,
                                               2 * LANES * p:2 * LANES * (p + 1)] = o.astype(ACT)
                st = st * decs[d, c][:, LANES * p:LANES * (p + 1)] + it["upd"]
            st_ref[d, p] = st


def _glascan(q, k, v, lg):
    B, N, _ = q.shape
    kf, kb, dkf, dkb = _bidir_specs(GLA_KW, N // TB)
    vf, vb, _, _ = _bidir_specs(GLA_WIDTH, N // TB)
    out = jax.ShapeDtypeStruct((B, N, GLA_WIDTH), ACT)
    return pl.pallas_call(
        _glascan_kernel,
        out_shape=[out, out],
        grid=(B, N // TB),
        in_specs=[kf, kf, vf, dkf, kb, kb, vb, dkb],
        out_specs=[vf, vb],
        scratch_shapes=[pltpu.VMEM((2, GLA_KW // LANES, GLA_DV, LANES), F32)],
        compiler_params=_cparams(("parallel", "arbitrary")),
        name="glascan",
    )(q, k, v, lg, q, k, v, lg)


def _readout_kernel(split_src, yf_ref, yb_ref, bon_ref, g_ref, of_ref, ob_ref, og_ref, *refs):
    if split_src:
        x_res = jnp.where(pl.program_id(1) == 0, refs[0][0], refs[1][0])
        refs = refs[2:]
    else:
        x_res = refs[0][0]
        refs = refs[1:]
    mod_ref, gnw_ref, gnb_ref, ggn_ref, wout_ref, gpost_ref, bd_ref, xo_ref = refs
    bd = bd_ref[...]
    y = yf_ref[0].astype(F32) + yb_ref[0].astype(F32)
    mu = _seg_sum(y, bd, exact=False) * (1.0 / RW_HEAD)
    yc = y - mu
    var = _seg_sum(yc * yc, bd, exact=False) * (1.0 / RW_HEAD)
    yn = yc * lax.rsqrt(var + RW_GN_EPS) * gnw_ref[...] + gnb_ref[...]
    rw = (yn + bon_ref[0].astype(F32)) * g_ref[0].astype(F32)
    o = of_ref[0].astype(F32) + ob_ref[0].astype(F32)
    og = og_ref[0].astype(F32)
    parts = [rw.astype(BF16)]
    for h in range(GLA_HEADS):
        sl = slice(GLA_DV * h, GLA_DV * (h + 1))
        oh = o[:, sl]
        on = oh * lax.rsqrt(jnp.mean(oh * oh, axis=-1, keepdims=True) + GLA_NORM_EPS)
        ogh = og[:, sl]
        parts.append((on * ggn_ref[:, sl] * (ogh * _sigmoid(ogh))).astype(BF16))
    cat = jnp.concatenate(parts, axis=1)
    mx = jnp.dot(cat, wout_ref[...], preferred_element_type=F32)
    xo_ref[0] = x_res + mod_ref[0, 0][2:3] * _rms(mx, gpost_ref[...])


def _readout(y, bonus, g, o, og, ctx, x, xc, modtab, prm, latents_only):
    B, N, _ = bonus.shape
    f = 1 if latents_only else 0
    if xc is None:
        res = [ctx, x]
        res_specs = [pl.BlockSpec((1, TB, D), lambda b, j: (b, 0, 0)),
                     pl.BlockSpec((1, TB, D), lambda b, j: (b, jnp.maximum(j + f - 1, 0), 0))]
    else:
        res, res_specs = [xc], [_tile_spec(D, f)]
    return pl.pallas_call(
        functools.partial(_readout_kernel, xc is None),
        out_shape=jax.ShapeDtypeStruct((B, N - f * TB, D), F32),
        grid=(B, N // TB - f),
        in_specs=[_tile_spec(RW_WIDTH, f)] * 4 + [_tile_spec(GLA_WIDTH, f)] * 3 + res_specs + [_mod_spec(f),
                  _const_spec((1, RW_WIDTH)), _const_spec((1, RW_WIDTH)), _const_spec((1, GLA_WIDTH)),
                  _const_spec((D, D)), _const_spec((1, D)), _const_spec((RW_WIDTH, RW_WIDTH))],
        out_specs=_tile_spec(D),
        compiler_params=_cparams(("parallel", "parallel")),
        name="readout",
    )(y[0], y[1], bonus, g, o[0], o[1], og, *res, modtab, prm["gn_w"], prm["gn_b"], prm["gla_gn_w"],
      prm["w_out"], prm["norm_post"], prm["bd64"])


def _swiglu_acc(hb, wg_ref, wu_ref, wd_ref):
    acc = jnp.zeros((hb.shape[0], D), F32)
    for c in range(D_FF // FCH):
        sl = slice(c * FCH, (c + 1) * FCH)
        gate = jnp.dot(hb, wg_ref[:, sl], preferred_element_type=F32)
        up = jnp.dot(hb, wu_ref[:, sl], preferred_element_type=F32)
        act = (gate * _sigmoid(gate) * up).astype(BF16)
        acc = acc + jnp.dot(act, wd_ref[sl, :], preferred_element_type=F32)
    return acc


def _ffn_kernel(x_ref, mod_ref, gpre_ref, gpost_ref, wg_ref, wu_ref, wd_ref, xo_ref):
    x = x_ref[0]
    hb = (_rms(x, gpre_ref[...]) * (1.0 + _wide_mod(mod_ref, 4)) + _wide_mod(mod_ref, 3)).astype(BF16)
    fx = _swiglu_acc(hb, wg_ref, wu_ref, wd_ref)
    xo_ref[0] = x + _wide_mod(mod_ref, 5) * _rms(fx, gpost_ref[...])


def _single_buffered(shape):
    nd = len(shape)
    return pl.BlockSpec(shape, lambda *_: (0,) * nd, pipeline_mode=pl.Buffered(1))


def _ffn(xc, modtab, g_pre, g_post, wg, wu, wd):
    B, N, _ = xc.shape
    wide = pl.BlockSpec((1, TW, D), lambda b, j: (b, j, 0))
    return pl.pallas_call(
        _ffn_kernel,
        out_shape=jax.ShapeDtypeStruct((B, N, D), F32),
        grid=(B, N // TW),
        in_specs=[wide, _wide_mod_spec(), _const_spec((1, D)), _const_spec((1, D)),
                  _single_buffered((D, D_FF)), _single_buffered((D, D_FF)), _single_buffered((D_FF, D))],
        out_specs=wide,
        compiler_params=_cparams(("parallel", "parallel")),
        name="ffn",
    )(xc, modtab, g_pre, g_post, wg, wu, wd)


MOE_TB = 1024
MOE_SEG = 32
MOE_TM = 512
MOE_R = 2 * MOE_TB + N_EXPERTS * MOE_SEG
MOE_NP = MOE_R // MOE_SEG


def _moe_route_kernel(x_ref, mod_ref, gpre_ref, router_ref, h_ref, info_ref, infot_ref, cnt_ref):
    mod = mod_ref[0, 0]
    h = _rms(x_ref[0], gpre_ref[...]) * (1.0 + mod[4:5]) + mod[3:4]
    h_ref[0] = h.astype(BF16)
    lane = lax.broadcasted_iota(jnp.int32, (MOE_TB, LANES), 1)
    logits = jnp.dot(h, router_ref[...], precision=lax.Precision.HIGHEST, preferred_element_type=F32)
    logits = jnp.where(lane < N_EXPERTS, logits, -jnp.inf)
    v1 = jnp.max(logits, axis=-1, keepdims=True)
    i1 = jnp.min(jnp.where(logits == v1, lane, LANES), axis=-1, keepdims=True)
    rest = jnp.where(lane == i1, -jnp.inf, logits)
    v2 = jnp.max(rest, axis=-1, keepdims=True)
    i2 = jnp.min(jnp.where(rest == v2, lane, LANES), axis=-1, keepdims=True)
    ex = jnp.exp(v2 - v1)
    w1 = 1.0 / (1.0 + ex)
    w2 = ex * w1
    e1 = jnp.where(lane == i1, 1.0, 0.0)
    e2 = jnp.where(lane == i2, 1.0, 0.0)
    es = e1 + e2
    t = lax.broadcasted_iota(jnp.int32, (MOE_TB, MOE_TB), 0)
    s = lax.broadcasted_iota(jnp.int32, (MOE_TB, MOE_TB), 1)
    before = jnp.where(s < t, 1.0, 0.0).astype(BF16)
    rank = jnp.dot(before, es.astype(BF16), preferred_element_type=F32)
    cnt = jnp.sum(es, axis=0, keepdims=True)
    segs = jnp.floor((cnt + (MOE_SEG - 1)) * (1.0 / MOE_SEG))
    ea = lax.broadcasted_iota(jnp.int32, (LANES, LANES), 0)
    eb = lax.broadcasted_iota(jnp.int32, (LANES, LANES), 1)
    earlier = jnp.where(ea < eb, 1.0, 0.0).astype(BF16)
    start = jnp.dot(jnp.broadcast_to(segs, (8, LANES)).astype(BF16), earlier,
                    preferred_element_type=F32)[0:1] * MOE_SEG
    pos = rank + start
    d1 = jnp.sum(e1 * pos, axis=-1, keepdims=True)
    d2 = jnp.sum(e2 * pos, axis=-1, keepdims=True)
    info = jnp.where(lane == 0, d1, jnp.where(lane == 1, d2, jnp.where(lane == 2, w1, jnp.where(lane == 3, w2, 0.0))))
    info_ref[0] = info
    infot_ref[0] = jnp.transpose(info)[0:8]
    cnt_ref[0] = jnp.broadcast_to(cnt, (8, LANES))


def _moe_route(xs, modtab, g_pre, router):
    B, S, _ = xs.shape
    per = S // MOE_TB
    nb = B * per
    blk = lambda c: pl.BlockSpec((1, MOE_TB, c), lambda i: (i // per, i % per, 0))
    flat = lambda r, c: pl.BlockSpec((1, r, c), lambda i: (i, 0, 0))
    return pl.pallas_call(
        _moe_route_kernel,
        out_shape=[jax.ShapeDtypeStruct((nb, MOE_TB, D), BF16), jax.ShapeDtypeStruct((nb, MOE_TB, LANES), F32),
                   jax.ShapeDtypeStruct((nb, 8, MOE_TB), F32), jax.ShapeDtypeStruct((nb, 8, LANES), F32)],
        grid=(nb,),
        in_specs=[blk(D), pl.BlockSpec((1, 1, 6, D), lambda i: (i // per, 1, 0, 0)),
                  _const_spec((1, D)), _const_spec((D, LANES))],
        out_specs=[flat(MOE_TB, D), flat(MOE_TB, LANES), flat(8, MOE_TB), flat(8, LANES)],
        compiler_params=_cparams(("parallel",)),
        name="moe_route",
    )(xs, modtab, g_pre, router)


def _moe_plan(cnt, n_tiles):
    pc = (cnt + MOE_SEG - 1) // MOE_SEG * MOE_SEG
    inc = jnp.cumsum(pc, axis=1)
    loff = inc - pc
    reg = (jnp.sum(pc, axis=0) + MOE_TM - 1) // MOE_TM * MOE_TM
    gend = jnp.cumsum(reg)
    goff = (gend - reg)[None, :] + jnp.cumsum(pc, axis=0) - pc
    rows = jnp.arange(MOE_NP, dtype=jnp.int32) * MOE_SEG
    e_p = jnp.sum((rows[None, :, None] >= inc[:, None, :]).astype(jnp.int32), axis=-1)
    e_c = jnp.minimum(e_p, N_EXPERTS - 1)
    dst = jnp.take_along_axis(goff, e_c, axis=1) + rows[None, :] - jnp.take_along_axis(loff, e_c, axis=1)
    dst = jnp.where(e_p < N_EXPERTS, dst, 0).astype(jnp.int32)
    n_valid = (inc[:, -1] // MOE_SEG).astype(jnp.int32)
    trow = jnp.arange(n_tiles, dtype=jnp.int32) * MOE_TM
    te = jnp.sum((trow[:, None] >= gend[None, :]).astype(jnp.int32), axis=-1)
    valid = te < N_EXPERTS
    last = gend[-1] // MOE_TM - 1
    te = jnp.where(valid, te, te[last]).astype(jnp.int32)
    src = jnp.where(valid, jnp.arange(n_tiles, dtype=jnp.int32), last).astype(jnp.int32)
    return dst, n_valid, te, src, valid.astype(jnp.int32)


def _piece_copy(src_ref, src_row, dst_ref, dst_row, sem):
    return pltpu.make_async_copy(src_ref.at[pl.ds(src_row, MOE_SEG)], dst_ref.at[pl.ds(dst_row, MOE_SEG)], sem)


def _moe_gather_kernel(dst_ref, nv_ref, h_ref, infot_ref, xg_in_ref, xg_ref, buf_ref, sem_ref):
    del xg_in_ref
    i = pl.program_id(0)
    it = infot_ref[0]
    rr = lax.broadcasted_iota(jnp.int32, (MOE_R, MOE_TB), 0).astype(F32)
    onehot = jnp.where(rr == it[0:1], 1.0, jnp.where(rr == it[1:2], 1.0, 0.0)).astype(BF16)
    buf_ref[...] = jnp.dot(onehot, h_ref[0], preferred_element_type=F32).astype(BF16)
    nv = nv_ref[i]

    def piece(p):
        return _piece_copy(buf_ref, pl.multiple_of(p * MOE_SEG, MOE_SEG),
                           xg_ref, pl.multiple_of(dst_ref[i, p], MOE_SEG), sem_ref.at[p])

    def start(p, c):
        piece(p).start()
        return c

    def wait(p, c):
        piece(p).wait()
        return c

    lax.fori_loop(0, nv, start, 0)
    lax.fori_loop(0, nv, wait, 0)


def _moe_gather(dst, n_valid, h, info_t, n_rows):
    nb = h.shape[0]
    flat = lambda r, c: pl.BlockSpec((1, r, c), lambda i, *_: (i, 0, 0))
    return pl.pallas_call(
        _moe_gather_kernel,
        out_shape=jax.ShapeDtypeStruct((n_rows, D), BF16),
        grid_spec=pltpu.PrefetchScalarGridSpec(
            num_scalar_prefetch=2, grid=(nb,),
            in_specs=[flat(MOE_TB, D), flat(8, MOE_TB), pl.BlockSpec(memory_space=pl.ANY)],
            out_specs=pl.BlockSpec(memory_space=pl.ANY),
            scratch_shapes=[pltpu.VMEM((MOE_R, D), BF16), pltpu.SemaphoreType.DMA((MOE_NP,))]),
        input_output_aliases={4: 0},
        compiler_params=_cparams(("arbitrary",)),
        name="moe_gather",
    )(dst, n_valid, h, info_t, jnp.zeros((n_rows, D), BF16))


def _moe_ffn_kernel(te_ref, src_ref, valid_ref, x_ref, wg_ref, wu_ref, wd_ref, y_ref):
    i = pl.program_id(0)

    @pl.when(valid_ref[i] == 1)
    def _():
        y_ref[...] = _swiglu_acc(x_ref[...], wg_ref.at[0], wu_ref.at[0], wd_ref.at[0]).astype(BF16)

    @pl.when(valid_ref[i] == 0)
    def _():
        y_ref[...] = jnp.zeros_like(y_ref)


def _moe_ffn(te, src, valid, xg, wg, wu, wd):
    n_rows = xg.shape[0]
    wspec = lambda r, c: pl.BlockSpec((1, r, c), lambda i, te, src, valid: (te[i], 0, 0))
    return pl.pallas_call(
        _moe_ffn_kernel,
        out_shape=jax.ShapeDtypeStruct((n_rows, D), BF16),
        grid_spec=pltpu.PrefetchScalarGridSpec(
            num_scalar_prefetch=3, grid=(n_rows // MOE_TM,),
            in_specs=[pl.BlockSpec((MOE_TM, D), lambda i, te, src, valid: (src[i], 0)),
                      wspec(D, D_FF), wspec(D, D_FF), wspec(D_FF, D)],
            out_specs=pl.BlockSpec((MOE_TM, D), lambda i, te, src, valid: (i, 0))),
        compiler_params=_cparams(("arbitrary",)),
        name="moe_ffn",
    )(te, src, valid, xg, wg, wu, wd)


def _moe_combine_kernel(dst_ref, nv_ref, x_ref, mod_ref, gpost_ref, info_ref, yg_ref, xo_ref, buf_ref, sem_ref):
    i = pl.program_id(0)
    nv = nv_ref[i]

    def piece(p):
        return _piece_copy(yg_ref, pl.multiple_of(dst_ref[i, p], MOE_SEG),
                           buf_ref, pl.multiple_of(p * MOE_SEG, MOE_SEG), sem_ref.at[p])

    def start(p, c):
        piece(p).start()
        return c

    def clear(p, c):
        buf_ref[pl.ds(pl.multiple_of(p * MOE_SEG, MOE_SEG), MOE_SEG), :] = jnp.zeros((MOE_SEG, D), BF16)
        return c

    def wait(p, c):
        piece(p).wait()
        return c

    lax.fori_loop(0, nv, start, 0)
    lax.fori_loop(nv, MOE_NP, clear, 0)
    info = info_ref[0]
    rr = lax.broadcasted_iota(jnp.int32, (MOE_TB, MOE_R), 1).astype(F32)
    comb = jnp.where(rr == info[:, 0:1], info[:, 2:3], jnp.where(rr == info[:, 1:2], info[:, 3:4], 0.0)).astype(BF16)
    lax.fori_loop(0, nv, wait, 0)
    fx = jnp.dot(comb, buf_ref[...], preferred_element_type=F32)
    xo_ref[0] = x_ref[0] + mod_ref[0, 0][5:6] * _rms(fx, gpost_ref[...])


def _moe_combine(dst, n_valid, xs, modtab, g_post, info, yg):
    B, S, _ = xs.shape
    per = S // MOE_TB
    blk = pl.BlockSpec((1, MOE_TB, D), lambda i, *_: (i // per, i % per, 0))
    return pl.pallas_call(
        _moe_combine_kernel,
        out_shape=jax.ShapeDtypeStruct((B, S, D), F32),
        grid_spec=pltpu.PrefetchScalarGridSpec(
            num_scalar_prefetch=2, grid=(B * per,),
            in_specs=[blk, pl.BlockSpec((1, 1, 6, D), lambda i, *_: (i // per, 1, 0, 0)),
                      pl.BlockSpec((1, D), lambda i, *_: (0, 0)),
                      pl.BlockSpec((1, MOE_TB, LANES), lambda i, *_: (i, 0, 0)),
                      pl.BlockSpec(memory_space=pl.ANY)],
            out_specs=blk,
            scratch_shapes=[pltpu.VMEM((MOE_R, D), BF16), pltpu.SemaphoreType.DMA((MOE_NP,))]),
        compiler_params=_cparams(("arbitrary",)),
        name="moe_combine",
    )(dst, n_valid, xs, modtab, g_post, info, yg)


def _moe(xs, modtab, g_pre, g_post, router, wg, wu, wd):
    B, S, _ = xs.shape
    nb = B * S // MOE_TB
    h, info, info_t, cnt = _moe_route(xs, modtab, g_pre, router)
    worst = 2 * B * S + nb * N_EXPERTS * (MOE_SEG - 1) + N_EXPERTS * (MOE_TM - 1)
    n_tiles = -(-worst // MOE_TM)
    dst, n_valid, te, src, valid = _moe_plan(cnt[:, 0, :N_EXPERTS].astype(jnp.int32), n_tiles)
    xg = _moe_gather(dst, n_valid, h, info_t, n_tiles * MOE_TM)
    yg = _moe_ffn(te, src, valid, xg, wg, wu, wd)
    return _moe_combine(dst, n_valid, xs, modtab, g_post, info, yg)


def _block_diag2(w):
    z = jnp.zeros_like(w[0])
    return jnp.concatenate([jnp.concatenate([w[0], z], axis=1), jnp.concatenate([z, w[1]], axis=1)], axis=0)


def _row(v):
    return v.reshape(1, -1).astype(F32)


def _head_ones(width, head):
    i = jnp.arange(width) // head
    return (i[:, None] == i[None, :]).astype(BF16)


def kernel(x, c, ctx, c_ctx, ada_w, ada_b, norm_mix_pre, norm_mix_post, norm_ffn_pre, norm_ffn_post, w_in, shift_mu, rw_w_up, rw_w0, rw_a_up, rw_a0, rw_k_k, rw_k_a, rw_r_k, rw_g_up, rw_gn_w, rw_gn_b, rw_v_down, rw_v_up, rw_v0, gla_conv, gla_a_up, gla_a_b, gla_gn_w, w_out, ffn_w_gate, ffn_w_up, ffn_w_down, moe_router, moe_w_gate, moe_w_up, moe_w_down):
    B, S, _ = x.shape
    n_ctx = ctx.shape[1]
    depth = w_in.shape[0]
    assert n_ctx == TB and S % MOE_TB == 0 and (n_ctx + S) % TW == 0 and depth == 2

    xc = None
    cvec = jnp.concatenate([c, c_ctx[None, :], jnp.zeros((16 - B - 1, D), F32)], axis=0)
    bd64 = _head_ones(RW_WIDTH, RW_HEAD)
    ada_b3 = ada_b.reshape(depth, 1, 6 * D)
    v_first = None
    out = None
    for i in range(depth):
        last = i == depth - 1
        mods = _adaln(cvec, ada_w, ada_b3, i)
        mod_x = mods[:B].reshape(B, 6, D)
        mod_c = jnp.broadcast_to(mods[B].reshape(1, 6, D), (B, 6, D))
        modtab = jnp.stack([mod_c, mod_x], axis=1)

        w_i = jnp.concatenate([w_in[i], jnp.zeros((D, GLA_PAD - GLA_COLS), F32)], axis=1).astype(BF16)
        p_rw, p_gl = _inproj(ctx, x, xc, _row(norm_mix_pre[i]), modtab, w_i)

        prm = dict(
            mu=_row(shift_mu[i]),
            w_up=_block_diag2(rw_w_up[i]).astype(BF16), w0=_row(rw_w0[i]),
            a_up=_block_diag2(rw_a_up[i]).astype(BF16), a0=_row(rw_a0[i]),
            k_k=_row(rw_k_k[i]), k_a=_row(rw_k_a[i]), r_k=_row(rw_r_k[i]),
            g_up=rw_g_up[i].astype(BF16), bd64=bd64,
            gn_w=_row(rw_gn_w[i]), gn_b=_row(rw_gn_b[i]), gla_gn_w=_row(gla_gn_w[i]),
            w_out=w_out[i].astype(BF16), norm_post=_row(norm_mix_post[i]),
        )
        gate_pad = jnp.zeros((LANES - 2 * GLA_GATE_RANK, 2 * GLA_KW), F32)
        gla_prm = dict(conv=gla_conv[i].astype(F32), a_b=_row(gla_a_b[i]),
                       a_up=jnp.concatenate([_block_diag2(gla_a_up[i]), gate_pad], axis=0).astype(BF16))
        if i > 0:
            pad = LANES - RW_V_RANK
            prm["v_down"] = jnp.concatenate([rw_v_down[i - 1], jnp.zeros((RW_WIDTH, pad), F32)], axis=1).astype(BF16)
            prm["v_up"] = jnp.concatenate([rw_v_up[i - 1], jnp.zeros((pad, RW_WIDTH), F32)], axis=0).astype(BF16)
            prm["v0"] = _row(rw_v0[i - 1])

        r, kk, vm, g, bonus, lw, bb, ke = _rwprep(p_rw, prm, v_first if i > 0 else None)
        if i == 0:
            v_first = vm
        y = _rwscan(r, kk, vm, lw, bb, ke)
        q, k, gv, og, lg = _glaprep(p_gl, gla_prm)
        o = _glascan(q, k, gv, lg)
        xc = _readout(y, bonus, g, o, og, ctx, x, xc, modtab, prm, latents_only=last)

        jf = i // 2
        if i % 2 == 0:
            xc = _ffn(xc, modtab, _row(norm_ffn_pre[i]), _row(norm_ffn_post[i]),
                      ffn_w_gate[jf].astype(BF16), ffn_w_up[jf].astype(BF16), ffn_w_down[jf].astype(BF16))
        else:
            router = jnp.concatenate([moe_router[jf], jnp.zeros((D, LANES - N_EXPERTS), F32)], axis=1)
            out = _moe(xc, modtab, _row(norm_ffn_pre[i]), _row(norm_ffn_post[i]), router,
                       moe_w_gate[jf].astype(BF16), moe_w_up[jf].astype(BF16), moe_w_down[jf].astype(BF16))
    return out
```

```python
import functools
import math

import jax
import jax.numpy as jnp
from jax import lax
from jax.experimental import pallas as pl
from jax.experimental.pallas import tpu as pltpu

F32, BF16 = jnp.float32, jnp.bfloat16
ACT = BF16

D = 1024
GRID_W = 64
RW_WIDTH = 512
RW_HEAD = 64
RW_RANK = 64
RW_G_RANK = 128
RW_V_RANK = 32
RW_GN_EPS = 64e-5
GLA_WIDTH = 512
GLA_HEADS = 4
GLA_DV = 128
GLA_DK = 64
GLA_KW = 256
GLA_GATE_RANK = 16
GLA_GATE_NORM = 16.0
GLA_NORM_EPS = 1e-5
D_FF = 2816
N_EXPERTS = 8
NORM_EPS = 1e-6
RW_COLS = 3 * RW_WIDTH + 4 * RW_RANK + RW_G_RANK
GLA_QKV = 2 * GLA_KW + GLA_WIDTH
GLA_COLS = GLA_QKV + GLA_WIDTH + 2 * GLA_GATE_RANK
GLA_PAD = 1664

LANES = 128
SUBLANES = 8
TB = 256
TW = 3 * TB
CH = 64
FCH = 256
HALO = 16
VMEM_LIMIT = 56 * 1024 * 1024


def _cparams(sem):
    return pltpu.CompilerParams(dimension_semantics=sem, vmem_limit_bytes=VMEM_LIMIT)


def _bdot(a, b):
    return jnp.dot(a.astype(BF16), b.astype(BF16), preferred_element_type=F32)


def _bdot_nt(a, b):
    return lax.dot_general(a.astype(BF16), b.astype(BF16), (((1,), (1,)), ((), ())),
                           preferred_element_type=F32)


def _bdot_tn(a, b):
    return lax.dot_general(a.astype(BF16), b.astype(BF16), (((0,), (0,)), ((), ())),
                           preferred_element_type=F32)


def _split_dot(a_exact, x):
    h1 = x.astype(BF16)
    r1 = x - h1.astype(F32)
    h2 = r1.astype(BF16)
    h3 = (r1 - h2.astype(F32)).astype(BF16)
    return (jnp.dot(a_exact, h1, preferred_element_type=F32)
            + jnp.dot(a_exact, h2, preferred_element_type=F32)
            + jnp.dot(a_exact, h3, preferred_element_type=F32))


def _seg_sum(x, bd, exact=True):
    hi = x.astype(BF16)
    out = jnp.dot(hi, bd, preferred_element_type=F32)
    if exact:
        lo = (x - hi.astype(F32)).astype(BF16)
        out = out + jnp.dot(lo, bd, preferred_element_type=F32)
    return out


def _sigmoid(x):
    return jax.nn.sigmoid(x)


def _rms(x, g):
    return x * lax.rsqrt(jnp.mean(x * x, axis=-1, keepdims=True) + NORM_EPS) * g


def _pair_stack(z, lo):
    return jnp.concatenate([jnp.where(lo, z, 0.0), jnp.where(lo, 0.0, z)], axis=0)


def _adaln_kernel(c_ref, w_ref, b_ref, o_ref):
    c = c_ref[...]
    s = c * _sigmoid(c)
    o_ref[...] = jnp.dot(s, w_ref[0], precision=lax.Precision.HIGHEST,
                         preferred_element_type=F32) + b_ref[0]


def _adaln(cvec, w, b, layer):
    rows = cvec.shape[0]
    n = w.shape[2]
    return pl.pallas_call(
        _adaln_kernel,
        out_shape=jax.ShapeDtypeStruct((rows, n), F32),
        grid=(n // D,),
        in_specs=[pl.BlockSpec((rows, D), lambda i: (0, 0)),
                  pl.BlockSpec((1, D, D), lambda i: (layer, 0, i)),
                  pl.BlockSpec((1, 1, D), lambda i: (layer, 0, i))],
        out_specs=pl.BlockSpec((rows, D), lambda i: (0, i)),
        compiler_params=_cparams(("arbitrary",)),
        name="adaln",
    )(cvec, w, b)


def _wide_mod(mod_ref, k):
    is_ctx = jnp.logical_and(pl.program_id(1) == 0, lax.broadcasted_iota(jnp.int32, (TW, 1), 0) < TB)
    return jnp.where(is_ctx, mod_ref[0, 0, k:k + 1], mod_ref[0, 1, k:k + 1])


def _wide_rows(refs):
    if len(refs) == 1:
        return refs[0][0]
    first = jnp.where(pl.program_id(1) == 0, refs[0][0], refs[1][0])
    return jnp.concatenate([first] + [r[0] for r in refs[2:]], axis=0)


def _inproj_kernel(n_src, *refs):
    g_ref, mod_ref, w_ref, prw_ref, pgl_ref = refs[n_src:]
    x = _wide_rows(refs[:n_src])
    h = _rms(x, g_ref[...]) * (1.0 + _wide_mod(mod_ref, 1)) + _wide_mod(mod_ref, 0)
    hb = h.astype(BF16)
    prw_ref[0] = jnp.dot(hb, w_ref[:, :RW_COLS], preferred_element_type=F32).astype(ACT)
    pgl_ref[0] = jnp.dot(hb, w_ref[:, RW_COLS:], preferred_element_type=F32).astype(ACT)


def _wide_src(ctx, x, xc):
    if xc is not None:
        return [xc], [pl.BlockSpec((1, TW, D), lambda b, j: (b, j, 0))]
    per = TW // TB
    specs = [pl.BlockSpec((1, TB, D), lambda b, j: (b, 0, 0))]
    for k in range(per):
        specs.append(pl.BlockSpec((1, TB, D), lambda b, j, k=k: (b, jnp.maximum(j * per + k - 1, 0), 0)))
    return [ctx] + [x] * per, specs


def _wide_mod_spec():
    return pl.BlockSpec((1, 2, 6, D), lambda b, j: (b, 0, 0, 0))


def _const_spec(shape):
    nd = len(shape)
    return pl.BlockSpec(shape, lambda *_: (0,) * nd)


def _tile_spec(c):
    return pl.BlockSpec((1, TB, c), lambda b, j: (b, j, 0))


def _dir_tile_spec(c):
    return pl.BlockSpec((2, 1, TB, c), lambda b, j: (0, b, j, 0))


def _inproj(ctx, x, xc, g, modtab, w):
    B = modtab.shape[0]
    N = xc.shape[1] if xc is not None else ctx.shape[1] + x.shape[1]
    srcs, src_specs = _wide_src(ctx, x, xc)
    wide = lambda c: pl.BlockSpec((1, TW, c), lambda b, j: (b, j, 0))
    return pl.pallas_call(
        functools.partial(_inproj_kernel, len(srcs)),
        out_shape=[jax.ShapeDtypeStruct((B, N, RW_COLS), ACT),
                   jax.ShapeDtypeStruct((B, N, GLA_PAD), ACT)],
        grid=(B, N // TW),
        in_specs=src_specs + [_const_spec((1, D)), _wide_mod_spec(), _const_spec((D, RW_COLS + GLA_PAD))],
        out_specs=[wide(RW_COLS), wide(GLA_PAD)],
        compiler_params=_cparams(("parallel", "parallel")),
        name="inproj",
    )(*srcs, g, modtab, w)


def _shift_table():
    t = jnp.arange(TB)
    prev = t[:, None] - 1 == t[None, :]
    nxt = t[:, None] + 1 == t[None, :]
    col = (t % GRID_W)[:, None]
    return jnp.stack([jnp.stack([prev, nxt]),
                      jnp.stack([prev & (col != 0), nxt & (col != GRID_W - 1)])]).astype(BF16)


def _rwprep_kernel(has_vres, n_tiles, p_ref, hu_ref, hd_ref, sh_ref, mu_ref, wup_ref, w0_ref, aup_ref,
                   a0_ref, kk_ref, ka_ref, rk_ref, gup_ref, bd_ref, *rest):
    if has_vres:
        vf_ref, vdn_ref, vup_ref, v0_ref = rest[:4]
        rest = rest[4:]
    r_out, kk_out, v_out, g_out, bon_out, lw_out, b_out, ke_out = rest
    j = pl.program_id(1)
    pb = p_ref[0]
    p = pb.astype(F32)
    prev1 = jnp.dot(sh_ref[0, 0], pb, preferred_element_type=F32)
    next1 = jnp.dot(sh_ref[0, 1], pb, preferred_element_type=F32)
    has_upper = jnp.where(j == 1, 0.0, 1.0)
    has_lower = jnp.where(j == n_tiles - 1, 0.0, 1.0)
    up = jnp.concatenate([hu_ref[0].astype(F32) * has_upper, p[:TB - GRID_W]], axis=0)
    down = jnp.concatenate([p[GRID_W:], hd_ref[0].astype(F32) * has_lower], axis=0)
    cls = lax.broadcasted_iota(jnp.int32, p.shape, 1) & jnp.where(j == 0, 1, 3)
    shifted = jnp.where(cls == 0, prev1, jnp.where(cls == 1, next1, jnp.where(cls == 2, up, down)))
    u = p + mu_ref[...] * (shifted - p)

    r = u[:, 0:RW_WIDTH]
    k = u[:, RW_WIDTH:2 * RW_WIDTH]
    v = u[:, 2 * RW_WIDTH:3 * RW_WIDTH]
    o = 3 * RW_WIDTH
    wd = u[:, o:o + 2 * RW_RANK]
    ad = u[:, o + 2 * RW_RANK:o + 4 * RW_RANK]
    gd = u[:, o + 4 * RW_RANK:]

    w_logit = w0_ref[...] + _bdot(jnp.tanh(wd), wup_ref[...])
    lw = -math.exp(-0.5) * _sigmoid(w_logit)
    a = _sigmoid(a0_ref[...] + _bdot(ad, aup_ref[...]))
    bd = bd_ref[...]
    kk = k * kk_ref[...]
    kk = kk * lax.rsqrt(jnp.maximum(_seg_sum(kk * kk, bd), 1e-24))
    g = _bdot(_sigmoid(gd), gup_ref[...])
    if has_vres:
        gate = _sigmoid(v0_ref[...] + _bdot(_bdot(v, vdn_ref[...]), vup_ref[...]))
        vm = v + (vf_ref[0].astype(F32) - v) * gate
    else:
        vm = v
    ke_sum = jnp.zeros_like(k)
    for d in range(2):
        a_d = a[:, d * RW_WIDTH:(d + 1) * RW_WIDTH]
        ke_d = k * (1.0 + (a_d - 1.0) * ka_ref[...])
        lw_out[d, 0] = lw[:, d * RW_WIDTH:(d + 1) * RW_WIDTH]
        b_out[d, 0] = (kk * a_d).astype(ACT)
        ke_out[d, 0] = ke_d.astype(ACT)
        ke_sum = ke_sum + ke_d
    r_out[0] = r.astype(ACT)
    kk_out[0] = kk.astype(ACT)
    v_out[0] = vm.astype(ACT)
    g_out[0] = g.astype(ACT)
    bon_out[0] = (_seg_sum(r * ke_sum * rk_ref[...], bd, exact=False) * vm).astype(ACT)


def _rwprep(p_rw, prm, v_first):
    B, N, _ = p_rw.shape
    nt = N // TB
    hb = TB // GRID_W
    nhb = N // GRID_W
    has_vres = v_first is not None
    W2 = 2 * RW_WIDTH
    in_specs = [
        _tile_spec(RW_COLS),
        pl.BlockSpec((1, GRID_W, RW_COLS), lambda b, j: (b, jnp.maximum(j * hb - 1, 0), 0)),
        pl.BlockSpec((1, GRID_W, RW_COLS), lambda b, j: (b, jnp.minimum(j * hb + hb, nhb - 1), 0)),
        pl.BlockSpec((1, 2, TB, TB), lambda b, j: (jnp.minimum(j, 1), 0, 0, 0)),
        _const_spec((1, RW_COLS)), _const_spec((2 * RW_RANK, W2)), _const_spec((1, W2)),
        _const_spec((2 * RW_RANK, W2)), _const_spec((1, W2)),
        _const_spec((1, RW_WIDTH)), _const_spec((1, RW_WIDTH)), _const_spec((1, RW_WIDTH)),
        _const_spec((RW_G_RANK, RW_WIDTH)), _const_spec((RW_WIDTH, RW_WIDTH)),
    ]
    args = [p_rw, p_rw, p_rw, _shift_table(), prm["mu"], prm["w_up"], prm["w0"], prm["a_up"], prm["a0"],
            prm["k_k"], prm["k_a"], prm["r_k"], prm["g_up"], prm["bd64"]]
    if has_vres:
        in_specs += [_tile_spec(RW_WIDTH), _const_spec((RW_WIDTH, LANES)), _const_spec((LANES, RW_WIDTH)),
                     _const_spec((1, RW_WIDTH))]
        args += [v_first, prm["v_down"], prm["v_up"], prm["v0"]]
    tok = jax.ShapeDtypeStruct((B, N, RW_WIDTH), ACT)
    dtok = jax.ShapeDtypeStruct((2, B, N, RW_WIDTH), ACT)
    return pl.pallas_call(
        functools.partial(_rwprep_kernel, has_vres, nt),
        out_shape=[tok] * 5 + [jax.ShapeDtypeStruct((2, B, N, RW_WIDTH), F32), dtok, dtok],
        grid=(B, nt),
        in_specs=in_specs,
        out_specs=[_tile_spec(RW_WIDTH)] * 5 + [_dir_tile_spec(RW_WIDTH)] * 3,
        compiler_params=_cparams(("parallel", "parallel")),
        name="rwprep",
    )(*args)


NCH = TB // CH


def _scan_kernel(rf_ref, kkf_ref, vf_ref, lwf_ref, bf_ref, kef_ref,
                 rb_ref, kkb_ref, vb_ref, lwb_ref, bb_ref, keb_ref,
                 gqf_ref, gkf_ref, gvf_ref, lgf_ref, gqb_ref, gkb_ref, gvb_ref, lgb_ref,
                 yf_ref, yb_ref, of_ref, ob_ref, st_ref, gst_ref):
    j = pl.program_id(1)

    @pl.when(j == 0)
    def _():
        st_ref[...] = jnp.zeros_like(st_ref)
        gst_ref[...] = jnp.zeros_like(gst_ref)

    ti = lax.broadcasted_iota(jnp.int32, (CH, LANES), 0)
    li = lax.broadcasted_iota(jnp.int32, (CH, LANES), 1)
    si = li & (CH - 1)
    lo = li < CH
    eye = jnp.where(ti == si, 1.0, 0.0)
    rblk = lax.broadcasted_iota(jnp.int32, (LANES, LANES), 0) // CH
    cblk = lax.broadcasted_iota(jnp.int32, (LANES, LANES), 1) // CH
    bdmask = rblk == cblk
    t64 = lax.broadcasted_iota(jnp.int32, (CH, CH), 0)
    s64 = lax.broadcasted_iota(jnp.int32, (CH, CH), 1)
    tri = [jnp.where(s64 <= t64, 1.0, 0.0).astype(BF16), jnp.where(s64 >= t64, 1.0, 0.0).astype(BF16)]
    m_incl = [si <= ti, si >= ti]
    m_strict = [si < ti, si > ti]
    n_pairs = RW_WIDTH // LANES
    in_refs = ((rf_ref, kkf_ref, vf_ref, lwf_ref, bf_ref, kef_ref),
               (rb_ref, kkb_ref, vb_ref, lwb_ref, bb_ref, keb_ref))
    y_refs = (yf_ref, yb_ref)

    def stack(z):
        return _pair_stack(z, lo).astype(BF16)

    st = {(d, p): st_ref[d, p] for d in range(2) for p in range(n_pairs)}

    def scan_step(step):
        cur = {}
        w_tot = {}
        for d in range(2):
            c = (NCH - 1 - step) if d == 1 else step
            rows = slice(c * CH, (c + 1) * CH)
            refs = in_refs[d]
            r, kk, v = (refs[i][0, rows, :].astype(F32) for i in range(3))
            lw = refs[3][0, 0, rows, :]
            b, ke = refs[4][0, 0, rows, :].astype(F32), refs[5][0, 0, rows, :].astype(F32)
            cum = _split_dot(tri[d], lw)
            tot = jnp.sum(lw, axis=0, keepdims=True)
            w_inv = jnp.exp(-cum)
            w_end = jnp.exp(tot - cum)
            w_tot[d] = jnp.exp(tot)
            rh = r * jnp.exp(cum)
            ah = -(kk * jnp.exp(cum - lw))
            bh, kh, bt, kt = b * w_inv, ke * w_inv, b * w_end, ke * w_end
            for p in range(n_pairs):
                sl = slice(LANES * p, LANES * (p + 1))
                cur[d, p] = dict(ah=ah[:, sl], rh=rh[:, sl], bh=bh[:, sl], kh=kh[:, sl], bt=bt[:, sl],
                                 kt=kt[:, sl], v=v[:, sl], rows=rows, sl=sl)
        yield
        for it in cur.values():
            it["lhs"] = jnp.concatenate([it["ah"], it["rh"]], axis=0).astype(BF16)
            rhs = jnp.concatenate([stack(it["bh"]), stack(it["kh"])], axis=0)
            it["a_all"] = _bdot_nt(it["lhs"], rhs)
        yield
        for (d, _), it in cur.items():
            a_all = it.pop("a_all")
            it["a_ab"] = jnp.where(m_strict[d], a_all[:CH, :LANES], 0.0)
            a_ak = jnp.where(m_strict[d], a_all[:CH, LANES:], 0.0)
            a_rb = jnp.where(m_incl[d], a_all[CH:, :LANES], 0.0)
            a_rk = jnp.where(m_incl[d], a_all[CH:, LANES:], 0.0)
            it["a_r"] = jnp.concatenate([a_rb, a_rk], axis=1).astype(BF16)
            it["v_bd"] = stack(it["v"])
            it["akv"] = _bdot(a_ak, it["v_bd"])
            it["t"] = eye + it["a_ab"]
            it["m"] = _bdot(it["a_ab"], stack(it["a_ab"]))
        yield
        for _ in range(int(math.log2(CH)) - 2):
            for it in cur.values():
                z = _bdot(jnp.concatenate([it["m"], it["t"]], axis=0), stack(it["m"]))
                it["m"] = z[:CH]
                it["t"] = it["t"] + z[CH:]
            yield
        for it in cur.values():
            it["t"] = (it["t"] + _bdot(it["t"], stack(it["m"]))).astype(BF16)
        yield
        for it in cur.values():
            a_til = jnp.dot(it["t"], stack(it["ah"]), preferred_element_type=F32)
            it["lhs2"] = jnp.concatenate([a_til.astype(BF16), it["lhs"][CH:]], axis=0)
            it["cc"] = jnp.dot(it["t"], stack(it["akv"]), preferred_element_type=F32)
            it["rhs_t"] = jnp.concatenate([it["bt"], it["kt"]], axis=0).astype(BF16)
        yield
        z1 = {k: _bdot_nt(it["lhs2"], st[k]) for k, it in cur.items()}
        yield
        u = {k: z1[k][:CH] + it["cc"] for k, it in cur.items()}
        upd = {k: _bdot_tn(jnp.concatenate([u[k], it["v"]], axis=0), it["rhs_t"]) for k, it in cur.items()}
        yield
        for (d, p), it in cur.items():
            y = z1[d, p][CH:] + jnp.dot(it["a_r"], jnp.concatenate([stack(u[d, p]), it["v_bd"]], axis=0),
                                        preferred_element_type=F32)
            y_refs[d][0, it["rows"], it["sl"]] = y.astype(ACT)
            st[d, p] = st[d, p] * w_tot[d][:, it["sl"]] + jnp.where(bdmask, upd[d, p], 0.0)
        yield

    gla_refs = ((gqf_ref, gkf_ref, gvf_ref, lgf_ref), (gqb_ref, gkb_ref, gvb_ref, lgb_ref))
    o_refs = (of_ref, ob_ref)
    g_pairs = GLA_KW // LANES
    gst = {(d, p): gst_ref[d, p] for d in range(2) for p in range(g_pairs)}
    lo2 = lax.broadcasted_iota(jnp.int32, (LANES, LANES), 1) < CH
    zeros_v = jnp.zeros((CH, GLA_DV), F32)

    def gla_step(step):
        cur = {}
        dec = {}
        for d in range(2):
            c = (NCH - 1 - step) if d == 1 else step
            rows = slice(c * CH, (c + 1) * CH)
            refs = gla_refs[d]
            q, k, v = (refs[i][0, rows, :].astype(F32) for i in range(3))
            lg = refs[3][0, 0, rows, :]
            cum = _split_dot(tri[d], lg)
            tot = jnp.sum(lg, axis=0, keepdims=True)
            dec[d] = jnp.exp(tot)
            qd = q * jnp.exp(cum)
            ki = k * jnp.exp(-cum)
            kend = k * jnp.exp(tot - cum)
            for p in range(g_pairs):
                sl = slice(LANES * p, LANES * (p + 1))
                cur[d, p] = dict(qd=qd[:, sl].astype(BF16), ki=ki[:, sl], kend=kend[:, sl], rows=rows, sl=sl,
                                 v0=v[:, 2 * LANES * p:2 * LANES * p + LANES],
                                 v1=v[:, 2 * LANES * p + LANES:2 * LANES * (p + 1)])
        yield
        for it in cur.values():
            it["att"] = _bdot_nt(it["qd"], _pair_stack(it["ki"], lo))
        yield
        for (d, _), it in cur.items():
            att = jnp.where(m_incl[d], it["att"], 0.0)
            v_bd = jnp.concatenate([jnp.concatenate([it["v0"], zeros_v], axis=1),
                                    jnp.concatenate([zeros_v, it["v1"]], axis=1)], axis=0)
            it["o"] = _bdot(att, v_bd)
            it["upd"] = _bdot_tn(jnp.concatenate([it["v0"], it["v1"]], axis=0), _pair_stack(it["kend"], lo))
        yield
        for (d, p), it in cur.items():
            s = gst[d, p]
            s_bd = jnp.concatenate([jnp.where(lo2, s, 0.0), jnp.where(lo2, 0.0, s)], axis=0)
            o = it["o"] + _bdot_nt(it["qd"], s_bd)
            o_refs[d][0, it["rows"], 2 * LANES * p:2 * LANES * (p + 1)] = o.astype(ACT)
            gst[d, p] = s * dec[d][:, it["sl"]] + it["upd"]
        yield

    n_stages = 12
    n_dep = 3
    gla_stages = (1, 4, 7, 10)
    pipeline = [(scan_step(s), gla_step(s)) for s in range(NCH)]
    for slot in range(n_stages + n_dep * (NCH - 1)):
        for s, (rw_gen, gla_gen) in enumerate(pipeline):
            stage = slot - n_dep * s
            if 0 <= stage < n_stages:
                next(rw_gen)
                if stage in gla_stages:
                    next(gla_gen)
    for (d, p), s in st.items():
        st_ref[d, p] = s
    for (d, p), s in gst.items():
        gst_ref[d, p] = s


def _bidir_specs(c, n_tiles):
    def back(j):
        return jnp.where(j == 0, 0, n_tiles - j)

    tok_f = pl.BlockSpec((1, TB, c), lambda b, j: (b, j, 0))
    tok_b = pl.BlockSpec((1, TB, c), lambda b, j: (b, back(j), 0))
    dir_f = pl.BlockSpec((1, 1, TB, c), lambda b, j: (0, b, j, 0))
    dir_b = pl.BlockSpec((1, 1, TB, c), lambda b, j: (1, b, back(j), 0))
    return tok_f, tok_b, dir_f, dir_b


def _scans(r, kk, v, lw, b, ke, q, k, gv, lg):
    B, N, _ = r.shape
    nt = N // TB
    tok_f, tok_b, dir_f, dir_b = _bidir_specs(RW_WIDTH, nt)
    kf, kb, dkf, dkb = _bidir_specs(GLA_KW, nt)
    out = jax.ShapeDtypeStruct((B, N, RW_WIDTH), ACT)
    assert GLA_WIDTH == RW_WIDTH
    yf, yb, of, ob = pl.pallas_call(
        _scan_kernel,
        out_shape=[out] * 4,
        grid=(B, nt),
        in_specs=[tok_f, tok_f, tok_f, dir_f, dir_f, dir_f, tok_b, tok_b, tok_b, dir_b, dir_b, dir_b,
                  kf, kf, tok_f, dkf, kb, kb, tok_b, dkb],
        out_specs=[tok_f, tok_b, tok_f, tok_b],
        scratch_shapes=[pltpu.VMEM((2, RW_WIDTH // LANES, LANES, LANES), F32),
                        pltpu.VMEM((2, GLA_KW // LANES, GLA_DV, LANES), F32)],
        compiler_params=_cparams(("parallel", "arbitrary")),
        name="scans",
    )(r, kk, v, lw, b, ke, r, kk, v, lw, b, ke, q, k, gv, lg, q, k, gv, lg)
    return (yf, yb), (of, ob)


CONV_K = TB + LANES


def _conv_shift_table():
    t = jnp.arange(TB)[:, None]
    s = jnp.arange(CONV_K)[None, :]
    prev = jnp.where(t == 0, s == TB + 2 * HALO - 1, s == t - 1)
    nxt = jnp.where(t == TB - 1, s == TB, s == t + 1)
    return jnp.stack([prev, nxt]).astype(BF16)


def _glaprep_kernel(n_tiles, p_ref, hp_ref, hn_ref, sh_ref, cw_ref, aup_ref, ab_ref, q_out, k_out, v_out,
                    og_out, lg_out):
    j = pl.program_id(1)
    ub = p_ref[0][:, :GLA_QKV]
    u = ub.astype(F32)
    has_prev = jnp.where(j <= 1, 0.0, 1.0)
    has_next = jnp.where(jnp.logical_or(j == 0, j == n_tiles - 1), 0.0, 1.0)
    ext = jnp.concatenate([ub,
                           (hn_ref[0][:, :GLA_QKV].astype(F32) * has_next).astype(BF16),
                           (hp_ref[0][:, :GLA_QKV].astype(F32) * has_prev).astype(BF16),
                           jnp.zeros((CONV_K - TB - 2 * HALO, GLA_QKV), BF16)], axis=0)
    prev1 = jnp.dot(sh_ref[0], ext, preferred_element_type=F32)
    next1 = jnp.dot(sh_ref[1], ext, preferred_element_type=F32)
    cw = cw_ref[...]
    conv = cw[0:1] * prev1 + cw[1:2] * u + cw[2:3] * next1
    qkv = conv * _sigmoid(conv)
    q_out[0] = (qkv[:, :GLA_KW] * (GLA_DK ** -0.5)).astype(ACT)
    k_out[0] = qkv[:, GLA_KW:2 * GLA_KW].astype(ACT)
    v_out[0] = qkv[:, 2 * GLA_KW:].astype(ACT)
    og_out[0] = p_ref[0][:, GLA_QKV:GLA_QKV + GLA_WIDTH]
    z = _bdot(p_ref[0][:, GLA_QKV + GLA_WIDTH:], aup_ref[...]) + ab_ref[...]
    lg = (jnp.minimum(z, 0.0) - jnp.log1p(jnp.exp(-jnp.abs(z)))) * (1.0 / GLA_GATE_NORM)
    for d in range(2):
        lg_out[d, 0] = lg[:, d * GLA_KW:(d + 1) * GLA_KW]


def _glaprep(p_gl, prm):
    B, N, _ = p_gl.shape
    nt = N // TB
    sub = HALO
    hb = TB // sub
    nhb = N // sub
    return pl.pallas_call(
        functools.partial(_glaprep_kernel, nt),
        out_shape=[jax.ShapeDtypeStruct((B, N, GLA_KW), ACT), jax.ShapeDtypeStruct((B, N, GLA_KW), ACT),
                   jax.ShapeDtypeStruct((B, N, GLA_WIDTH), ACT), jax.ShapeDtypeStruct((B, N, GLA_WIDTH), ACT),
                   jax.ShapeDtypeStruct((2, B, N, GLA_KW), F32)],
        grid=(B, nt),
        in_specs=[_tile_spec(GLA_PAD),
                  pl.BlockSpec((1, sub, GLA_PAD), lambda b, j: (b, jnp.maximum(j * hb - 1, 0), 0)),
                  pl.BlockSpec((1, sub, GLA_PAD), lambda b, j: (b, jnp.minimum(j * hb + hb, nhb - 1), 0)),
                  _const_spec((2, TB, CONV_K)),
                  _const_spec((3, GLA_QKV)), _const_spec((LANES, 2 * GLA_KW)), _const_spec((1, 2 * GLA_KW))],
        out_specs=[_tile_spec(GLA_KW), _tile_spec(GLA_KW), _tile_spec(GLA_WIDTH), _tile_spec(GLA_WIDTH),
                   _dir_tile_spec(GLA_KW)],
        compiler_params=_cparams(("parallel", "parallel")),
        name="glaprep",
    )(p_gl, p_gl, p_gl, _conv_shift_table(), prm["conv"], prm["a_up"], prm["a_b"])


def _readout_kernel(group, first, split_src, *refs):
    tok = [refs[i * group:(i + 1) * group] for i in range(7)]
    refs = refs[7 * group:]
    n_res = group + 1 if split_src else group
    res_refs, refs = refs[:n_res], refs[n_res:]
    mod_ref, gnw_ref, gnb_ref, ggn_ref, wout_ref, gpost_ref, bd_ref, xo_ref = refs
    bd = bd_ref[...]
    starts_with_ctx = jnp.logical_and(first == 0, pl.program_id(1) == 0)
    for k in range(group):
        yf_ref, yb_ref, bon_ref, g_ref, of_ref, ob_ref, og_ref = (t[k] for t in tok)
        gate = mod_ref[0, 1, 2:3]
        if split_src:
            x_res = res_refs[k + 1][0]
            if k == 0:
                x_res = jnp.where(starts_with_ctx, res_refs[0][0], x_res)
        else:
            x_res = res_refs[k][0]
        if k == 0 and first == 0:
            gate = jnp.where(starts_with_ctx, mod_ref[0, 0, 2:3], gate)
        y = yf_ref[0].astype(F32) + yb_ref[0].astype(F32)
        mu = _seg_sum(y, bd, exact=False) * (1.0 / RW_HEAD)
        yc = y - mu
        var = _seg_sum(yc * yc, bd, exact=False) * (1.0 / RW_HEAD)
        yn = yc * lax.rsqrt(var + RW_GN_EPS) * gnw_ref[...] + gnb_ref[...]
        rw = (yn + bon_ref[0].astype(F32)) * g_ref[0].astype(F32)
        o = of_ref[0].astype(F32) + ob_ref[0].astype(F32)
        og = og_ref[0].astype(F32)
        parts = [rw.astype(BF16)]
        for h in range(GLA_HEADS):
            sl = slice(GLA_DV * h, GLA_DV * (h + 1))
            oh = o[:, sl]
            on = oh * lax.rsqrt(jnp.mean(oh * oh, axis=-1, keepdims=True) + GLA_NORM_EPS)
            ogh = og[:, sl]
            parts.append((on * ggn_ref[:, sl] * (ogh * _sigmoid(ogh))).astype(BF16))
        cat = jnp.concatenate(parts, axis=1)
        mx = jnp.dot(cat, wout_ref[...], preferred_element_type=F32)
        xo_ref[0, k * TB:(k + 1) * TB, :] = x_res + gate * _rms(mx, gpost_ref[...])


def _readout(y, bonus, g, o, og, ctx, x, xc, modtab, prm, latents_only):
    B, N, _ = bonus.shape
    first = 1 if latents_only else 0
    n_tiles = N // TB - first
    group = max(k for k in (4, 3, 2, 1) if n_tiles % k == 0)

    def tiles(c, shift=0):
        return [pl.BlockSpec((1, TB, c), lambda b, j, k=k: (b, jnp.maximum(first + group * j + k + shift, 0), 0))
                for k in range(group)]

    if xc is None:
        res, res_specs = [ctx] + [x] * group, [pl.BlockSpec((1, TB, D), lambda b, j: (b, 0, 0))] + tiles(D, -1)
    else:
        res, res_specs = [xc] * group, tiles(D)
    tok_args, tok_specs = [], []
    for arr in (y[0], y[1], bonus, g, o[0], o[1], og):
        tok_args += [arr] * group
        tok_specs += tiles(arr.shape[-1])
    return pl.pallas_call(
        functools.partial(_readout_kernel, group, first, xc is None),
        out_shape=jax.ShapeDtypeStruct((B, n_tiles * TB, D), F32),
        grid=(B, n_tiles // group),
        in_specs=tok_specs + res_specs + [_wide_mod_spec(),
                  _const_spec((1, RW_WIDTH)), _const_spec((1, RW_WIDTH)), _const_spec((1, GLA_WIDTH)),
                  _const_spec((D, D)), _const_spec((1, D)), _const_spec((RW_WIDTH, RW_WIDTH))],
        out_specs=pl.BlockSpec((1, group * TB, D), lambda b, j: (b, j, 0)),
        compiler_params=_cparams(("parallel", "parallel")),
        name="readout",
    )(*tok_args, *res, modtab, prm["gn_w"], prm["gn_b"], prm["gla_gn_w"],
      prm["w_out"], prm["norm_post"], prm["bd64"])


def _swiglu_acc(hb, wg_ref, wu_ref, wd_ref):
    acc = jnp.zeros((hb.shape[0], D), F32)
    for c in range(D_FF // FCH):
        sl = slice(c * FCH, (c + 1) * FCH)
        gate = jnp.dot(hb, wg_ref[:, sl], preferred_element_type=F32)
        up = jnp.dot(hb, wu_ref[:, sl], preferred_element_type=F32)
        act = (gate * _sigmoid(gate) * up).astype(BF16)
        acc = acc + jnp.dot(act, wd_ref[sl, :], preferred_element_type=F32)
    return acc


def _ffn_kernel(x_ref, mod_ref, gpre_ref, gpost_ref, wg_ref, wu_ref, wd_ref, xo_ref):
    x = x_ref[0]
    hb = (_rms(x, gpre_ref[...]) * (1.0 + _wide_mod(mod_ref, 4)) + _wide_mod(mod_ref, 3)).astype(BF16)
    fx = _swiglu_acc(hb, wg_ref, wu_ref, wd_ref)
    xo_ref[0] = x + _wide_mod(mod_ref, 5) * _rms(fx, gpost_ref[...])


def _single_buffered(shape):
    nd = len(shape)
    return pl.BlockSpec(shape, lambda *_: (0,) * nd, pipeline_mode=pl.Buffered(1))


def _ffn(xc, modtab, g_pre, g_post, wg, wu, wd):
    B, N, _ = xc.shape
    wide = pl.BlockSpec((1, TW, D), lambda b, j: (b, j, 0))
    return pl.pallas_call(
        _ffn_kernel,
        out_shape=jax.ShapeDtypeStruct((B, N, D), F32),
        grid=(B, N // TW),
        in_specs=[wide, _wide_mod_spec(), _const_spec((1, D)), _const_spec((1, D)),
                  _single_buffered((D, D_FF)), _single_buffered((D, D_FF)), _single_buffered((D_FF, D))],
        out_specs=wide,
        compiler_params=_cparams(("parallel", "parallel")),
        name="ffn",
    )(xc, modtab, g_pre, g_post, wg, wu, wd)


MOE_TB = 1024
MOE_SEG = 32
MOE_TM = 512
MOE_R = 2 * MOE_TB + N_EXPERTS * MOE_SEG
MOE_NP = MOE_R // MOE_SEG


def _moe_route_kernel(x_ref, mod_ref, gpre_ref, router_ref, h_ref, info_ref, infot_ref, cnt_ref):
    mod = mod_ref[0, 0]
    h = _rms(x_ref[0], gpre_ref[...]) * (1.0 + mod[4:5]) + mod[3:4]
    hb = h.astype(BF16)
    h_ref[0] = hb
    lane = lax.broadcasted_iota(jnp.int32, (MOE_TB, LANES), 1)
    h_lo = (h - hb.astype(F32)).astype(BF16)
    logits = (jnp.dot(hb, router_ref[0], preferred_element_type=F32)
              + jnp.dot(hb, router_ref[1], preferred_element_type=F32)
              + jnp.dot(h_lo, router_ref[0], preferred_element_type=F32))
    logits = jnp.where(lane < N_EXPERTS, logits, -jnp.inf)
    v1 = jnp.max(logits, axis=-1, keepdims=True)
    i1 = jnp.min(jnp.where(logits == v1, lane, LANES), axis=-1, keepdims=True)
    rest = jnp.where(lane == i1, -jnp.inf, logits)
    v2 = jnp.max(rest, axis=-1, keepdims=True)
    i2 = jnp.min(jnp.where(rest == v2, lane, LANES), axis=-1, keepdims=True)
    ex = jnp.exp(v2 - v1)
    w1 = 1.0 / (1.0 + ex)
    w2 = ex * w1
    e1 = jnp.where(lane == i1, 1.0, 0.0)
    e2 = jnp.where(lane == i2, 1.0, 0.0)
    es = e1 + e2
    t = lax.broadcasted_iota(jnp.int32, (MOE_TB, MOE_TB), 0)
    s = lax.broadcasted_iota(jnp.int32, (MOE_TB, MOE_TB), 1)
    before = jnp.where(s < t, 1.0, 0.0).astype(BF16)
    rank = jnp.dot(before, es.astype(BF16), preferred_element_type=F32)
    cnt = jnp.sum(es, axis=0, keepdims=True)
    segs = jnp.floor((cnt + (MOE_SEG - 1)) * (1.0 / MOE_SEG))
    ea = lax.broadcasted_iota(jnp.int32, (LANES, LANES), 0)
    eb = lax.broadcasted_iota(jnp.int32, (LANES, LANES), 1)
    earlier = jnp.where(ea < eb, 1.0, 0.0).astype(BF16)
    start = jnp.dot(jnp.broadcast_to(segs, (SUBLANES, LANES)).astype(BF16), earlier,
                    preferred_element_type=F32)[0:1] * MOE_SEG
    pos = rank + start
    d1 = jnp.sum(e1 * pos, axis=-1, keepdims=True)
    d2 = jnp.sum(e2 * pos, axis=-1, keepdims=True)
    info = jnp.where(lane == 0, d1, jnp.where(lane == 1, d2, jnp.where(lane == 2, w1, jnp.where(lane == 3, w2, 0.0))))
    info_ref[0] = info
    infot_ref[0] = jnp.transpose(info)[0:SUBLANES]
    cnt_ref[0] = jnp.broadcast_to(cnt, (SUBLANES, LANES))


def _moe_route(xs, modtab, g_pre, router):
    B, S, _ = xs.shape
    per = S // MOE_TB
    nb = B * per
    blk = lambda c: pl.BlockSpec((1, MOE_TB, c), lambda i: (i // per, i % per, 0))
    flat = lambda r, c: pl.BlockSpec((1, r, c), lambda i: (i, 0, 0))
    return pl.pallas_call(
        _moe_route_kernel,
        out_shape=[jax.ShapeDtypeStruct((nb, MOE_TB, D), BF16), jax.ShapeDtypeStruct((nb, MOE_TB, LANES), F32),
                   jax.ShapeDtypeStruct((nb, SUBLANES, MOE_TB), F32),
                   jax.ShapeDtypeStruct((nb, SUBLANES, LANES), F32)],
        grid=(nb,),
        in_specs=[blk(D), pl.BlockSpec((1, 1, 6, D), lambda i: (i // per, 1, 0, 0)),
                  _const_spec((1, D)), _const_spec((2, D, LANES))],
        out_specs=[flat(MOE_TB, D), flat(MOE_TB, LANES), flat(SUBLANES, MOE_TB), flat(SUBLANES, LANES)],
        compiler_params=_cparams(("parallel",)),
        name="moe_route",
    )(xs, modtab, g_pre, router)


def _moe_plan(cnt, n_tiles):
    pc = (cnt + MOE_SEG - 1) // MOE_SEG * MOE_SEG
    inc = jnp.cumsum(pc, axis=1)
    loff = inc - pc
    reg = (jnp.sum(pc, axis=0) + MOE_TM - 1) // MOE_TM * MOE_TM
    gend = jnp.cumsum(reg)
    goff = (gend - reg)[None, :] + jnp.cumsum(pc, axis=0) - pc
    rows = jnp.arange(MOE_NP, dtype=jnp.int32) * MOE_SEG
    e_p = jnp.sum((rows[None, :, None] >= inc[:, None, :]).astype(jnp.int32), axis=-1)
    e_c = jnp.minimum(e_p, N_EXPERTS - 1)
    dst = jnp.take_along_axis(goff, e_c, axis=1) + rows[None, :] - jnp.take_along_axis(loff, e_c, axis=1)
    dst = jnp.where(e_p < N_EXPERTS, dst, 0).astype(jnp.int32)
    n_valid = (inc[:, -1] // MOE_SEG).astype(jnp.int32)
    trow = jnp.arange(n_tiles, dtype=jnp.int32) * MOE_TM
    te = jnp.sum((trow[:, None] >= gend[None, :]).astype(jnp.int32), axis=-1)
    valid = te < N_EXPERTS
    last = gend[-1] // MOE_TM - 1
    te = jnp.where(valid, te, te[last]).astype(jnp.int32)
    src = jnp.where(valid, jnp.arange(n_tiles, dtype=jnp.int32), last).astype(jnp.int32)
    return dst, n_valid, te, src, valid.astype(jnp.int32)


def _piece_copy(src_ref, src_row, dst_ref, dst_row, sem):
    return pltpu.make_async_copy(src_ref.at[pl.ds(src_row, MOE_SEG)], dst_ref.at[pl.ds(dst_row, MOE_SEG)], sem)


def _moe_gather_kernel(dst_ref, nv_ref, h_ref, infot_ref, xg_in_ref, xg_ref, buf_ref, sem_ref):
    del xg_in_ref
    i = pl.program_id(0)
    it = infot_ref[0]
    rr = lax.broadcasted_iota(jnp.int32, (MOE_R, MOE_TB), 0).astype(F32)
    onehot = jnp.where(rr == it[0:1], 1.0, jnp.where(rr == it[1:2], 1.0, 0.0)).astype(BF16)
    buf_ref[...] = jnp.dot(onehot, h_ref[0], preferred_element_type=F32).astype(BF16)
    nv = nv_ref[i]

    def piece(p):
        return _piece_copy(buf_ref, pl.multiple_of(p * MOE_SEG, MOE_SEG),
                           xg_ref, pl.multiple_of(dst_ref[i, p], MOE_SEG), sem_ref.at[p])

    def start(p, c):
        piece(p).start()
        return c

    def wait(p, c):
        piece(p).wait()
        return c

    lax.fori_loop(0, nv, start, 0)
    lax.fori_loop(0, nv, wait, 0)


def _moe_gather(dst, n_valid, h, info_t, n_rows):
    nb = h.shape[0]
    flat = lambda r, c: pl.BlockSpec((1, r, c), lambda i, *_: (i, 0, 0))
    return pl.pallas_call(
        _moe_gather_kernel,
        out_shape=jax.ShapeDtypeStruct((n_rows, D), BF16),
        grid_spec=pltpu.PrefetchScalarGridSpec(
            num_scalar_prefetch=2, grid=(nb,),
            in_specs=[flat(MOE_TB, D), flat(SUBLANES, MOE_TB), pl.BlockSpec(memory_space=pl.ANY)],
            out_specs=pl.BlockSpec(memory_space=pl.ANY),
            scratch_shapes=[pltpu.VMEM((MOE_R, D), BF16), pltpu.SemaphoreType.DMA((MOE_NP,))]),
        input_output_aliases={4: 0},
        compiler_params=_cparams(("arbitrary",)),
        name="moe_gather",
    )(dst, n_valid, h, info_t, jnp.zeros((n_rows, D), BF16))


def _moe_ffn_kernel(te_ref, src_ref, valid_ref, x_ref, wg_ref, wu_ref, wd_ref, y_ref):
    i = pl.program_id(0)

    @pl.when(valid_ref[i] == 1)
    def _():
        y_ref[...] = _swiglu_acc(x_ref[...], wg_ref.at[0], wu_ref.at[0], wd_ref.at[0]).astype(BF16)

    @pl.when(valid_ref[i] == 0)
    def _():
        y_ref[...] = jnp.zeros_like(y_ref)


def _moe_ffn(te, src, valid, xg, wg, wu, wd):
    n_rows = xg.shape[0]
    wspec = lambda r, c: pl.BlockSpec((1, r, c), lambda i, te, src, valid: (te[i], 0, 0))
    return pl.pallas_call(
        _moe_ffn_kernel,
        out_shape=jax.ShapeDtypeStruct((n_rows, D), BF16),
        grid_spec=pltpu.PrefetchScalarGridSpec(
            num_scalar_prefetch=3, grid=(n_rows // MOE_TM,),
            in_specs=[pl.BlockSpec((MOE_TM, D), lambda i, te, src, valid: (src[i], 0)),
                      wspec(D, D_FF), wspec(D, D_FF), wspec(D_FF, D)],
            out_specs=pl.BlockSpec((MOE_TM, D), lambda i, te, src, valid: (i, 0))),
        compiler_params=_cparams(("arbitrary",)),
        name="moe_ffn",
    )(te, src, valid, xg, wg, wu, wd)


def _moe_combine_kernel(dst_ref, nv_ref, x_ref, mod_ref, gpost_ref, info_ref, yg_ref, xo_ref, buf_ref, sem_ref):
    i = pl.program_id(0)
    nv = nv_ref[i]

    def piece(p):
        return _piece_copy(yg_ref, pl.multiple_of(dst_ref[i, p], MOE_SEG),
                           buf_ref, pl.multiple_of(p * MOE_SEG, MOE_SEG), sem_ref.at[p])

    def start(p, c):
        piece(p).start()
        return c

    def clear(p, c):
        buf_ref[pl.ds(pl.multiple_of(p * MOE_SEG, MOE_SEG), MOE_SEG), :] = jnp.zeros((MOE_SEG, D), BF16)
        return c

    def wait(p, c):
        piece(p).wait()
        return c

    lax.fori_loop(0, nv, start, 0)
    lax.fori_loop(nv, MOE_NP, clear, 0)
    info = info_ref[0]
    rr = lax.broadcasted_iota(jnp.int32, (MOE_TB, MOE_R), 1).astype(F32)
    comb = jnp.where(rr == info[:, 0:1], info[:, 2:3], jnp.where(rr == info[:, 1:2], info[:, 3:4], 0.0)).astype(BF16)
    lax.fori_loop(0, nv, wait, 0)
    fx = jnp.dot(comb, buf_ref[...], preferred_element_type=F32)
    xo_ref[0] = x_ref[0] + mod_ref[0, 0][5:6] * _rms(fx, gpost_ref[...])


def _moe_combine(dst, n_valid, xs, modtab, g_post, info, yg):
    B, S, _ = xs.shape
    per = S // MOE_TB
    blk = pl.BlockSpec((1, MOE_TB, D), lambda i, *_: (i // per, i % per, 0))
    return pl.pallas_call(
        _moe_combine_kernel,
        out_shape=jax.ShapeDtypeStruct((B, S, D), F32),
        grid_spec=pltpu.PrefetchScalarGridSpec(
            num_scalar_prefetch=2, grid=(B * per,),
            in_specs=[blk, pl.BlockSpec((1, 1, 6, D), lambda i, *_: (i // per, 1, 0, 0)),
                      pl.BlockSpec((1, D), lambda i, *_: (0, 0)),
                      pl.BlockSpec((1, MOE_TB, LANES), lambda i, *_: (i, 0, 0)),
                      pl.BlockSpec(memory_space=pl.ANY)],
            out_specs=blk,
            scratch_shapes=[pltpu.VMEM((MOE_R, D), BF16), pltpu.SemaphoreType.DMA((MOE_NP,))]),
        compiler_params=_cparams(("arbitrary",)),
        name="moe_combine",
    )(dst, n_valid, xs, modtab, g_post, info, yg)


def _moe(xs, modtab, g_pre, g_post, router, wg, wu, wd):
    B, S, _ = xs.shape
    nb = B * S // MOE_TB
    h, info, info_t, cnt = _moe_route(xs, modtab, g_pre, router)
    worst = 2 * B * S + nb * N_EXPERTS * (MOE_SEG - 1) + N_EXPERTS * (MOE_TM - 1)
    n_tiles = -(-worst // MOE_TM)
    dst, n_valid, te, src, valid = _moe_plan(cnt[:, 0, :N_EXPERTS].astype(jnp.int32), n_tiles)
    xg = _moe_gather(dst, n_valid, h, info_t, n_tiles * MOE_TM)
    yg = _moe_ffn(te, src, valid, xg, wg, wu, wd)
    return _moe_combine(dst, n_valid, xs, modtab, g_post, info, yg)


def _block_diag2(w):
    z = jnp.zeros_like(w[0])
    return jnp.concatenate([jnp.concatenate([w[0], z], axis=1), jnp.concatenate([z, w[1]], axis=1)], axis=0)


def _row(v):
    return v.reshape(1, -1).astype(F32)


def _head_ones(width, head):
    i = jnp.arange(width) // head
    return (i[:, None] == i[None, :]).astype(BF16)


def kernel(x, c, ctx, c_ctx, ada_w, ada_b, norm_mix_pre, norm_mix_post, norm_ffn_pre, norm_ffn_post, w_in, shift_mu, rw_w_up, rw_w0, rw_a_up, rw_a0, rw_k_k, rw_k_a, rw_r_k, rw_g_up, rw_gn_w, rw_gn_b, rw_v_down, rw_v_up, rw_v0, gla_conv, gla_a_up, gla_a_b, gla_gn_w, w_out, ffn_w_gate, ffn_w_up, ffn_w_down, moe_router, moe_w_gate, moe_w_up, moe_w_down):
    B, S, _ = x.shape
    n_ctx = ctx.shape[1]
    depth = w_in.shape[0]
    assert n_ctx == TB and S % MOE_TB == 0 and (n_ctx + S) % TW == 0 and depth == 2

    xc = None
    pad_rows = -(B + 1) % SUBLANES
    cvec = jnp.concatenate([c, c_ctx[None, :], jnp.zeros((pad_rows, D), F32)], axis=0)
    bd64 = _head_ones(RW_WIDTH, RW_HEAD)
    ada_b3 = ada_b.reshape(depth, 1, 6 * D)
    v_first = None
    out = None
    for i in range(depth):
        last = i == depth - 1
        mods = _adaln(cvec, ada_w, ada_b3, i)
        mod_x = mods[:B].reshape(B, 6, D)
        mod_c = jnp.broadcast_to(mods[B].reshape(1, 6, D), (B, 6, D))
        modtab = jnp.stack([mod_c, mod_x], axis=1)

        w_i = jnp.concatenate([w_in[i], jnp.zeros((D, GLA_PAD - GLA_COLS), F32)], axis=1).astype(BF16)
        p_rw, p_gl = _inproj(ctx, x, xc, _row(norm_mix_pre[i]), modtab, w_i)

        prm = dict(
            mu=_row(shift_mu[i]),
            w_up=_block_diag2(rw_w_up[i]).astype(BF16), w0=_row(rw_w0[i]),
            a_up=_block_diag2(rw_a_up[i]).astype(BF16), a0=_row(rw_a0[i]),
            k_k=_row(rw_k_k[i]), k_a=_row(rw_k_a[i]), r_k=_row(rw_r_k[i]),
            g_up=rw_g_up[i].astype(BF16), bd64=bd64,
            gn_w=_row(rw_gn_w[i]), gn_b=_row(rw_gn_b[i]), gla_gn_w=_row(gla_gn_w[i]),
            w_out=w_out[i].astype(BF16), norm_post=_row(norm_mix_post[i]),
        )
        gate_pad = jnp.zeros((LANES - 2 * GLA_GATE_RANK, 2 * GLA_KW), F32)
        gla_prm = dict(conv=gla_conv[i].astype(F32), a_b=_row(gla_a_b[i]),
                       a_up=jnp.concatenate([_block_diag2(gla_a_up[i]), gate_pad], axis=0).astype(BF16))
        if i > 0:
            pad = LANES - RW_V_RANK
            prm["v_down"] = jnp.concatenate([rw_v_down[i - 1], jnp.zeros((RW_WIDTH, pad), F32)], axis=1).astype(BF16)
            prm["v_up"] = jnp.concatenate([rw_v_up[i - 1], jnp.zeros((pad, RW_WIDTH), F32)], axis=0).astype(BF16)
            prm["v0"] = _row(rw_v0[i - 1])

        r, kk, vm, g, bonus, lw, bb, ke = _rwprep(p_rw, prm, v_first if i > 0 else None)
        if i == 0:
            v_first = vm
        q, k, gv, og, lg = _glaprep(p_gl, gla_prm)
        y, o = _scans(r, kk, vm, lw, bb, ke, q, k, gv, lg)
        xc = _readout(y, bonus, g, o, og, ctx, x, xc, modtab, prm, latents_only=last)

        jf = i // 2
        if i % 2 == 0:
            xc = _ffn(xc, modtab, _row(norm_ffn_pre[i]), _row(norm_ffn_post[i]),
                      ffn_w_gate[jf].astype(BF16), ffn_w_up[jf].astype(BF16), ffn_w_down[jf].astype(BF16))
        else:
            router = jnp.concatenate([moe_router[jf], jnp.zeros((D, LANES - N_EXPERTS), F32)], axis=1)
            r_hi = router.astype(BF16)
            router = jnp.stack([r_hi, (router - r_hi.astype(F32)).astype(BF16)])
            out = _moe(xc, modtab, _row(norm_ffn_pre[i]), _row(norm_ffn_post[i]), router,
                       moe_w_gate[jf].astype(BF16), moe_w_up[jf].astype(BF16), moe_w_down[jf].astype(BF16))
    return out
```

```python
import functools
import math

import jax
import jax.numpy as jnp
from jax import lax
from jax.experimental import pallas as pl
from jax.experimental.pallas import tpu as pltpu

F32, BF16 = jnp.float32, jnp.bfloat16
ACT = BF16

D = 1024
GRID_W = 64
RW_WIDTH = 512
RW_HEAD = 64
RW_RANK = 64
RW_G_RANK = 128
RW_V_RANK = 32
RW_GN_EPS = 64e-5
GLA_WIDTH = 512
GLA_HEADS = 4
GLA_DV = 128
GLA_DK = 64
GLA_KW = 256
GLA_GATE_RANK = 16
GLA_GATE_NORM = 16.0
GLA_NORM_EPS = 1e-5
D_FF = 2816
N_EXPERTS = 8
NORM_EPS = 1e-6
RW_COLS = 3 * RW_WIDTH + 4 * RW_RANK + RW_G_RANK
GLA_QKV = 2 * GLA_KW + GLA_WIDTH
GLA_COLS = GLA_QKV + GLA_WIDTH + 2 * GLA_GATE_RANK
GLA_PAD = 1664

LANES = 128
SUBLANES = 8
TB = 256
TW = 3 * TB
CH = 64
FCH = 256
HALO = 16
VMEM_LIMIT = 56 * 1024 * 1024


def _cparams(sem):
    return pltpu.CompilerParams(dimension_semantics=sem, vmem_limit_bytes=VMEM_LIMIT)


def _bdot(a, b):
    return jnp.dot(a.astype(BF16), b.astype(BF16), preferred_element_type=F32)


def _bdot_nt(a, b):
    return lax.dot_general(a.astype(BF16), b.astype(BF16), (((1,), (1,)), ((), ())),
                           preferred_element_type=F32)


def _bdot_tn(a, b):
    return lax.dot_general(a.astype(BF16), b.astype(BF16), (((0,), (0,)), ((), ())),
                           preferred_element_type=F32)


def _split_dot(a_exact, x):
    h1 = x.astype(BF16)
    r1 = x - h1.astype(F32)
    h2 = r1.astype(BF16)
    h3 = (r1 - h2.astype(F32)).astype(BF16)
    return (jnp.dot(a_exact, h1, preferred_element_type=F32)
            + jnp.dot(a_exact, h2, preferred_element_type=F32)
            + jnp.dot(a_exact, h3, preferred_element_type=F32))


def _seg_sum(x, bd, exact=True):
    hi = x.astype(BF16)
    out = jnp.dot(hi, bd, preferred_element_type=F32)
    if exact:
        lo = (x - hi.astype(F32)).astype(BF16)
        out = out + jnp.dot(lo, bd, preferred_element_type=F32)
    return out


def _sigmoid(x):
    return jax.nn.sigmoid(x)


def _rms(x, g):
    return x * lax.rsqrt(jnp.mean(x * x, axis=-1, keepdims=True) + NORM_EPS) * g


def _pair_stack(z, lo):
    return jnp.concatenate([jnp.where(lo, z, 0.0), jnp.where(lo, 0.0, z)], axis=0)


def _adaln_kernel(c_ref, w_ref, b_ref, o_ref):
    c = c_ref[...]
    s = c * _sigmoid(c)
    o_ref[...] = jnp.dot(s, w_ref[0], precision=lax.Precision.HIGHEST,
                         preferred_element_type=F32) + b_ref[0]


def _adaln(cvec, w, b, layer):
    rows = cvec.shape[0]
    n = w.shape[2]
    return pl.pallas_call(
        _adaln_kernel,
        out_shape=jax.ShapeDtypeStruct((rows, n), F32),
        grid=(n // D,),
        in_specs=[pl.BlockSpec((rows, D), lambda i: (0, 0)),
                  pl.BlockSpec((1, D, D), lambda i: (layer, 0, i)),
                  pl.BlockSpec((1, 1, D), lambda i: (layer, 0, i))],
        out_specs=pl.BlockSpec((rows, D), lambda i: (0, i)),
        compiler_params=_cparams(("arbitrary",)),
        name="adaln",
    )(cvec, w, b)


def _wide_mod(mod_ref, k):
    is_ctx = jnp.logical_and(pl.program_id(1) == 0, lax.broadcasted_iota(jnp.int32, (TW, 1), 0) < TB)
    return jnp.where(is_ctx, mod_ref[0, 0, k:k + 1], mod_ref[0, 1, k:k + 1])


def _wide_rows(refs):
    if len(refs) == 1:
        return refs[0][0]
    first = jnp.where(pl.program_id(1) == 0, refs[0][0], refs[1][0])
    return jnp.concatenate([first] + [r[0] for r in refs[2:]], axis=0)


def _inproj_kernel(n_src, *refs):
    g_ref, mod_ref, w_ref, prw_ref, pgl_ref = refs[n_src:]
    x = _wide_rows(refs[:n_src])
    h = _rms(x, g_ref[...]) * (1.0 + _wide_mod(mod_ref, 1)) + _wide_mod(mod_ref, 0)
    hb = h.astype(BF16)
    prw_ref[0] = jnp.dot(hb, w_ref[:, :RW_COLS], preferred_element_type=F32).astype(ACT)
    pgl_ref[0] = jnp.dot(hb, w_ref[:, RW_COLS:], preferred_element_type=F32).astype(ACT)


def _wide_src(ctx, x, xc):
    if xc is not None:
        return [xc], [pl.BlockSpec((1, TW, D), lambda b, j: (b, j, 0))]
    per = TW // TB
    specs = [pl.BlockSpec((1, TB, D), lambda b, j: (b, 0, 0))]
    for k in range(per):
        specs.append(pl.BlockSpec((1, TB, D), lambda b, j, k=k: (b, jnp.maximum(j * per + k - 1, 0), 0)))
    return [ctx] + [x] * per, specs


def _wide_mod_spec():
    return pl.BlockSpec((1, 2, 6, D), lambda b, j: (b, 0, 0, 0))


def _const_spec(shape):
    nd = len(shape)
    return pl.BlockSpec(shape, lambda *_: (0,) * nd)


def _tile_spec(c):
    return pl.BlockSpec((1, TB, c), lambda b, j: (b, j, 0))


def _dir_tile_spec(c):
    return pl.BlockSpec((2, 1, TB, c), lambda b, j: (0, b, j, 0))


def _inproj(ctx, x, xc, g, modtab, w):
    B = modtab.shape[0]
    N = xc.shape[1] if xc is not None else ctx.shape[1] + x.shape[1]
    srcs, src_specs = _wide_src(ctx, x, xc)
    wide = lambda c: pl.BlockSpec((1, TW, c), lambda b, j: (b, j, 0))
    return pl.pallas_call(
        functools.partial(_inproj_kernel, len(srcs)),
        out_shape=[jax.ShapeDtypeStruct((B, N, RW_COLS), ACT),
                   jax.ShapeDtypeStruct((B, N, GLA_PAD), ACT)],
        grid=(B, N // TW),
        in_specs=src_specs + [_const_spec((1, D)), _wide_mod_spec(), _const_spec((D, RW_COLS + GLA_PAD))],
        out_specs=[wide(RW_COLS), wide(GLA_PAD)],
        compiler_params=_cparams(("parallel", "parallel")),
        name="inproj",
    )(*srcs, g, modtab, w)


def _shift_table():
    t = jnp.arange(TB)
    prev = t[:, None] - 1 == t[None, :]
    nxt = t[:, None] + 1 == t[None, :]
    col = (t % GRID_W)[:, None]
    return jnp.stack([jnp.stack([prev, nxt]),
                      jnp.stack([prev & (col != 0), nxt & (col != GRID_W - 1)])]).astype(BF16)


def _rwprep_kernel(has_vres, n_tiles, p_ref, hu_ref, hd_ref, sh_ref, mu_ref, wup_ref, w0_ref, aup_ref,
                   a0_ref, kk_ref, ka_ref, rk_ref, gup_ref, bd_ref, *rest):
    if has_vres:
        vf_ref, vdn_ref, vup_ref, v0_ref = rest[:4]
        rest = rest[4:]
    r_out, kk_out, v_out, g_out, bon_out, lw_out, b_out, ke_out = rest
    j = pl.program_id(1)
    pb = p_ref[0]
    p = pb.astype(F32)
    prev1 = jnp.dot(sh_ref[0, 0], pb, preferred_element_type=F32)
    next1 = jnp.dot(sh_ref[0, 1], pb, preferred_element_type=F32)
    has_upper = jnp.where(j == 1, 0.0, 1.0)
    has_lower = jnp.where(j == n_tiles - 1, 0.0, 1.0)
    up = jnp.concatenate([hu_ref[0].astype(F32) * has_upper, p[:TB - GRID_W]], axis=0)
    down = jnp.concatenate([p[GRID_W:], hd_ref[0].astype(F32) * has_lower], axis=0)
    cls = lax.broadcasted_iota(jnp.int32, p.shape, 1) & jnp.where(j == 0, 1, 3)
    shifted = jnp.where(cls == 0, prev1, jnp.where(cls == 1, next1, jnp.where(cls == 2, up, down)))
    u = p + mu_ref[...] * (shifted - p)

    r = u[:, 0:RW_WIDTH]
    k = u[:, RW_WIDTH:2 * RW_WIDTH]
    v = u[:, 2 * RW_WIDTH:3 * RW_WIDTH]
    o = 3 * RW_WIDTH
    wd = u[:, o:o + 2 * RW_RANK]
    ad = u[:, o + 2 * RW_RANK:o + 4 * RW_RANK]
    gd = u[:, o + 4 * RW_RANK:]

    w_logit = w0_ref[...] + _bdot(jnp.tanh(wd), wup_ref[...])
    lw = -math.exp(-0.5) * _sigmoid(w_logit)
    a = _sigmoid(a0_ref[...] + _bdot(ad, aup_ref[...]))
    bd = bd_ref[...]
    kk = k * kk_ref[...]
    kk = kk * lax.rsqrt(jnp.maximum(_seg_sum(kk * kk, bd), 1e-24))
    g = _bdot(_sigmoid(gd), gup_ref[...])
    if has_vres:
        gate = _sigmoid(v0_ref[...] + _bdot(_bdot(v, vdn_ref[...]), vup_ref[...]))
        vm = v + (vf_ref[0].astype(F32) - v) * gate
    else:
        vm = v
    ke_sum = jnp.zeros_like(k)
    for d in range(2):
        a_d = a[:, d * RW_WIDTH:(d + 1) * RW_WIDTH]
        ke_d = k * (1.0 + (a_d - 1.0) * ka_ref[...])
        lw_out[d, 0] = lw[:, d * RW_WIDTH:(d + 1) * RW_WIDTH]
        b_out[d, 0] = (kk * a_d).astype(ACT)
        ke_out[d, 0] = ke_d.astype(ACT)
        ke_sum = ke_sum + ke_d
    r_out[0] = r.astype(ACT)
    kk_out[0] = kk.astype(ACT)
    v_out[0] = vm.astype(ACT)
    g_out[0] = g.astype(ACT)
    bon_out[0] = (_seg_sum(r * ke_sum * rk_ref[...], bd, exact=False) * vm).astype(ACT)


def _rwprep(p_rw, prm, v_first):
    B, N, _ = p_rw.shape
    nt = N // TB
    hb = TB // GRID_W
    nhb = N // GRID_W
    has_vres = v_first is not None
    W2 = 2 * RW_WIDTH
    in_specs = [
        _tile_spec(RW_COLS),
        pl.BlockSpec((1, GRID_W, RW_COLS), lambda b, j: (b, jnp.maximum(j * hb - 1, 0), 0)),
        pl.BlockSpec((1, GRID_W, RW_COLS), lambda b, j: (b, jnp.minimum(j * hb + hb, nhb - 1), 0)),
        pl.BlockSpec((1, 2, TB, TB), lambda b, j: (jnp.minimum(j, 1), 0, 0, 0)),
        _const_spec((1, RW_COLS)), _const_spec((2 * RW_RANK, W2)), _const_spec((1, W2)),
        _const_spec((2 * RW_RANK, W2)), _const_spec((1, W2)),
        _const_spec((1, RW_WIDTH)), _const_spec((1, RW_WIDTH)), _const_spec((1, RW_WIDTH)),
        _const_spec((RW_G_RANK, RW_WIDTH)), _const_spec((RW_WIDTH, RW_WIDTH)),
    ]
    args = [p_rw, p_rw, p_rw, _shift_table(), prm["mu"], prm["w_up"], prm["w0"], prm["a_up"], prm["a0"],
            prm["k_k"], prm["k_a"], prm["r_k"], prm["g_up"], prm["bd64"]]
    if has_vres:
        in_specs += [_tile_spec(RW_WIDTH), _const_spec((RW_WIDTH, LANES)), _const_spec((LANES, RW_WIDTH)),
                     _const_spec((1, RW_WIDTH))]
        args += [v_first, prm["v_down"], prm["v_up"], prm["v0"]]
    tok = jax.ShapeDtypeStruct((B, N, RW_WIDTH), ACT)
    dtok = jax.ShapeDtypeStruct((2, B, N, RW_WIDTH), ACT)
    return pl.pallas_call(
        functools.partial(_rwprep_kernel, has_vres, nt),
        out_shape=[tok] * 5 + [jax.ShapeDtypeStruct((2, B, N, RW_WIDTH), F32), dtok, dtok],
        grid=(B, nt),
        in_specs=in_specs,
        out_specs=[_tile_spec(RW_WIDTH)] * 5 + [_dir_tile_spec(RW_WIDTH)] * 3,
        compiler_params=_cparams(("parallel", "parallel")),
        name="rwprep",
    )(*args)


NCH = TB // CH


def _scan_kernel(rf_ref, kkf_ref, vf_ref, lwf_ref, bf_ref, kef_ref,
                 rb_ref, kkb_ref, vb_ref, lwb_ref, bb_ref, keb_ref,
                 gqf_ref, gkf_ref, gvf_ref, lgf_ref, gqb_ref, gkb_ref, gvb_ref, lgb_ref,
                 yf_ref, yb_ref, of_ref, ob_ref, st_ref, gst_ref):
    j = pl.program_id(1)

    @pl.when(j == 0)
    def _():
        st_ref[...] = jnp.zeros_like(st_ref)
        gst_ref[...] = jnp.zeros_like(gst_ref)

    ti = lax.broadcasted_iota(jnp.int32, (CH, LANES), 0)
    li = lax.broadcasted_iota(jnp.int32, (CH, LANES), 1)
    si = li & (CH - 1)
    lo = li < CH
    eye = jnp.where(ti == si, 1.0, 0.0)
    rblk = lax.broadcasted_iota(jnp.int32, (LANES, LANES), 0) // CH
    cblk = lax.broadcasted_iota(jnp.int32, (LANES, LANES), 1) // CH
    bdmask = rblk == cblk
    t64 = lax.broadcasted_iota(jnp.int32, (CH, CH), 0)
    s64 = lax.broadcasted_iota(jnp.int32, (CH, CH), 1)
    tri = [jnp.where(s64 <= t64, 1.0, 0.0).astype(BF16), jnp.where(s64 >= t64, 1.0, 0.0).astype(BF16)]
    m_incl = [si <= ti, si >= ti]
    m_strict = [si < ti, si > ti]
    n_pairs = RW_WIDTH // LANES
    in_refs = ((rf_ref, kkf_ref, vf_ref, lwf_ref, bf_ref, kef_ref),
               (rb_ref, kkb_ref, vb_ref, lwb_ref, bb_ref, keb_ref))
    y_refs = (yf_ref, yb_ref)

    def stack(z):
        return _pair_stack(z, lo).astype(BF16)

    st = {(d, p): st_ref[d, p] for d in range(2) for p in range(n_pairs)}

    def scan_step(step):
        cur = {}
        w_tot = {}
        for d in range(2):
            c = (NCH - 1 - step) if d == 1 else step
            rows = slice(c * CH, (c + 1) * CH)
            refs = in_refs[d]
            r, kk, v = (refs[i][0, rows, :].astype(F32) for i in range(3))
            lw = refs[3][0, 0, rows, :]
            b, ke = refs[4][0, 0, rows, :].astype(F32), refs[5][0, 0, rows, :].astype(F32)
            cum = _split_dot(tri[d], lw)
            tot = jnp.sum(lw, axis=0, keepdims=True)
            w_inv = jnp.exp(-cum)
            w_end = jnp.exp(tot - cum)
            w_tot[d] = jnp.exp(tot)
            rh = r * jnp.exp(cum)
            ah = -(kk * jnp.exp(cum - lw))
            bh, kh, bt, kt = b * w_inv, ke * w_inv, b * w_end, ke * w_end
            for p in range(n_pairs):
                sl = slice(LANES * p, LANES * (p + 1))
                cur[d, p] = dict(ah=ah[:, sl], rh=rh[:, sl], bh=bh[:, sl], kh=kh[:, sl], bt=bt[:, sl],
                                 kt=kt[:, sl], v=v[:, sl], rows=rows, sl=sl)
        yield
        for it in cur.values():
            it["lhs"] = jnp.concatenate([it["ah"], it["rh"]], axis=0).astype(BF16)
            rhs = jnp.concatenate([stack(it["bh"]), stack(it["kh"])], axis=0)
            it["a_all"] = _bdot_nt(it["lhs"], rhs)
        yield
        for (d, _), it in cur.items():
            a_all = it.pop("a_all")
            it["a_ab"] = jnp.where(m_strict[d], a_all[:CH, :LANES], 0.0)
            a_ak = jnp.where(m_strict[d], a_all[:CH, LANES:], 0.0)
            a_rb = jnp.where(m_incl[d], a_all[CH:, :LANES], 0.0)
            a_rk = jnp.where(m_incl[d], a_all[CH:, LANES:], 0.0)
            it["a_r"] = jnp.concatenate([a_rb, a_rk], axis=1).astype(BF16)
            it["v_bd"] = stack(it["v"])
            it["akv"] = _bdot(a_ak, it["v_bd"])
            it["t"] = eye + it["a_ab"]
            it["m"] = _bdot(it["a_ab"], stack(it["a_ab"]))
        yield
        for _ in range(int(math.log2(CH)) - 2):
            for it in cur.values():
                z = _bdot(jnp.concatenate([it["m"], it["t"]], axis=0), stack(it["m"]))
                it["m"] = z[:CH]
                it["t"] = it["t"] + z[CH:]
            yield
        for it in cur.values():
            it["t"] = (it["t"] + _bdot(it["t"], stack(it["m"]))).astype(BF16)
        yield
        for it in cur.values():
            a_til = jnp.dot(it["t"], stack(it["ah"]), preferred_element_type=F32)
            it["lhs2"] = jnp.concatenate([a_til.astype(BF16), it["lhs"][CH:]], axis=0)
            it["cc"] = jnp.dot(it["t"], stack(it["akv"]), preferred_element_type=F32)
            it["rhs_t"] = jnp.concatenate([it["bt"], it["kt"]], axis=0).astype(BF16)
        yield
        z1 = {k: _bdot_nt(it["lhs2"], st[k]) for k, it in cur.items()}
        yield
        u = {k: z1[k][:CH] + it["cc"] for k, it in cur.items()}
        upd = {k: _bdot_tn(jnp.concatenate([u[k], it["v"]], axis=0), it["rhs_t"]) for k, it in cur.items()}
        yield
        for (d, p), it in cur.items():
            y = z1[d, p][CH:] + jnp.dot(it["a_r"], jnp.concatenate([stack(u[d, p]), it["v_bd"]], axis=0),
                                        preferred_element_type=F32)
            y_refs[d][0, it["rows"], it["sl"]] = y.astype(ACT)
            st[d, p] = st[d, p] * w_tot[d][:, it["sl"]] + jnp.where(bdmask, upd[d, p], 0.0)
        yield

    gla_refs = ((gqf_ref, gkf_ref, gvf_ref, lgf_ref), (gqb_ref, gkb_ref, gvb_ref, lgb_ref))
    o_refs = (of_ref, ob_ref)
    g_pairs = GLA_KW // LANES
    gst = {(d, p): gst_ref[d, p] for d in range(2) for p in range(g_pairs)}
    lo2 = lax.broadcasted_iota(jnp.int32, (LANES, LANES), 1) < CH
    zeros_v = jnp.zeros((CH, GLA_DV), F32)

    def gla_step(step):
        cur = {}
        dec = {}
        for d in range(2):
            c = (NCH - 1 - step) if d == 1 else step
            rows = slice(c * CH, (c + 1) * CH)
            refs = gla_refs[d]
            q, k, v = (refs[i][0, rows, :].astype(F32) for i in range(3))
            lg = refs[3][0, 0, rows, :]
            cum = _split_dot(tri[d], lg)
            tot = jnp.sum(lg, axis=0, keepdims=True)
            dec[d] = jnp.exp(tot)
            qd = q * jnp.exp(cum)
            ki = k * jnp.exp(-cum)
            kend = k * jnp.exp(tot - cum)
            for p in range(g_pairs):
                sl = slice(LANES * p, LANES * (p + 1))
                cur[d, p] = dict(qd=qd[:, sl].astype(BF16), ki=ki[:, sl], kend=kend[:, sl], rows=rows, sl=sl,
                                 v0=v[:, 2 * LANES * p:2 * LANES * p + LANES],
                                 v1=v[:, 2 * LANES * p + LANES:2 * LANES * (p + 1)])
        yield
        for it in cur.values():
            it["att"] = _bdot_nt(it["qd"], _pair_stack(it["ki"], lo))
        yield
        for (d, _), it in cur.items():
            att = jnp.where(m_incl[d], it["att"], 0.0)
            v_bd = jnp.concatenate([jnp.concatenate([it["v0"], zeros_v], axis=1),
                                    jnp.concatenate([zeros_v, it["v1"]], axis=1)], axis=0)
            it["o"] = _bdot(att, v_bd)
            it["upd"] = _bdot_tn(jnp.concatenate([it["v0"], it["v1"]], axis=0), _pair_stack(it["kend"], lo))
        yield
        for (d, p), it in cur.items():
            s = gst[d, p]
            s_bd = jnp.concatenate([jnp.where(lo2, s, 0.0), jnp.where(lo2, 0.0, s)], axis=0)
            o = it["o"] + _bdot_nt(it["qd"], s_bd)
            o_refs[d][0, it["rows"], 2 * LANES * p:2 * LANES * (p + 1)] = o.astype(ACT)
            gst[d, p] = s * dec[d][:, it["sl"]] + it["upd"]
        yield

    n_stages = 12
    n_dep = 3
    gla_stages = (1, 4, 7, 10)
    pipeline = [(scan_step(s), gla_step(s)) for s in range(NCH)]
    for slot in range(n_stages + n_dep * (NCH - 1)):
        for s, (rw_gen, gla_gen) in enumerate(pipeline):
            stage = slot - n_dep * s
            if 0 <= stage < n_stages:
                next(rw_gen)
                if stage in gla_stages:
                    next(gla_gen)
    for (d, p), s in st.items():
        st_ref[d, p] = s
    for (d, p), s in gst.items():
        gst_ref[d, p] = s


def _bidir_specs(c, n_tiles):
    def back(j):
        return jnp.where(j == 0, 0, n_tiles - j)

    tok_f = pl.BlockSpec((1, TB, c), lambda b, j: (b, j, 0))
    tok_b = pl.BlockSpec((1, TB, c), lambda b, j: (b, back(j), 0))
    dir_f = pl.BlockSpec((1, 1, TB, c), lambda b, j: (0, b, j, 0))
    dir_b = pl.BlockSpec((1, 1, TB, c), lambda b, j: (1, b, back(j), 0))
    return tok_f, tok_b, dir_f, dir_b


def _scans(r, kk, v, lw, b, ke, q, k, gv, lg):
    B, N, _ = r.shape
    nt = N // TB
    tok_f, tok_b, dir_f, dir_b = _bidir_specs(RW_WIDTH, nt)
    kf, kb, dkf, dkb = _bidir_specs(GLA_KW, nt)
    out = jax.ShapeDtypeStruct((B, N, RW_WIDTH), ACT)
    assert GLA_WIDTH == RW_WIDTH
    yf, yb, of, ob = pl.pallas_call(
        _scan_kernel,
        out_shape=[out] * 4,
        grid=(B, nt),
        in_specs=[tok_f, tok_f, tok_f, dir_f, dir_f, dir_f, tok_b, tok_b, tok_b, dir_b, dir_b, dir_b,
                  kf, kf, tok_f, dkf, kb, kb, tok_b, dkb],
        out_specs=[tok_f, tok_b, tok_f, tok_b],
        scratch_shapes=[pltpu.VMEM((2, RW_WIDTH // LANES, LANES, LANES), F32),
                        pltpu.VMEM((2, GLA_KW // LANES, GLA_DV, LANES), F32)],
        compiler_params=_cparams(("parallel", "arbitrary")),
        name="scans",
    )(r, kk, v, lw, b, ke, r, kk, v, lw, b, ke, q, k, gv, lg, q, k, gv, lg)
    return (yf, yb), (of, ob)


CONV_K = TB + LANES


def _conv_shift_table():
    t = jnp.arange(TB)[:, None]
    s = jnp.arange(CONV_K)[None, :]
    prev = jnp.where(t == 0, s == TB + 2 * HALO - 1, s == t - 1)
    nxt = jnp.where(t == TB - 1, s == TB, s == t + 1)
    return jnp.stack([prev, nxt]).astype(BF16)


def _glaprep_kernel(n_tiles, p_ref, hp_ref, hn_ref, sh_ref, cw_ref, aup_ref, ab_ref, q_out, k_out, v_out,
                    og_out, lg_out):
    j = pl.program_id(1)
    ub = p_ref[0][:, :GLA_QKV]
    u = ub.astype(F32)
    has_prev = jnp.where(j <= 1, 0.0, 1.0)
    has_next = jnp.where(jnp.logical_or(j == 0, j == n_tiles - 1), 0.0, 1.0)
    ext = jnp.concatenate([ub,
                           (hn_ref[0][:, :GLA_QKV].astype(F32) * has_next).astype(BF16),
                           (hp_ref[0][:, :GLA_QKV].astype(F32) * has_prev).astype(BF16),
                           jnp.zeros((CONV_K - TB - 2 * HALO, GLA_QKV), BF16)], axis=0)
    prev1 = jnp.dot(sh_ref[0], ext, preferred_element_type=F32)
    next1 = jnp.dot(sh_ref[1], ext, preferred_element_type=F32)
    cw = cw_ref[...]
    conv = cw[0:1] * prev1 + cw[1:2] * u + cw[2:3] * next1
    qkv = conv * _sigmoid(conv)
    q_out[0] = (qkv[:, :GLA_KW] * (GLA_DK ** -0.5)).astype(ACT)
    k_out[0] = qkv[:, GLA_KW:2 * GLA_KW].astype(ACT)
    v_out[0] = qkv[:, 2 * GLA_KW:].astype(ACT)
    og_out[0] = p_ref[0][:, GLA_QKV:GLA_QKV + GLA_WIDTH]
    z = _bdot(p_ref[0][:, GLA_QKV + GLA_WIDTH:], aup_ref[...]) + ab_ref[...]
    lg = (jnp.minimum(z, 0.0) - jnp.log1p(jnp.exp(-jnp.abs(z)))) * (1.0 / GLA_GATE_NORM)
    for d in range(2):
        lg_out[d, 0] = lg[:, d * GLA_KW:(d + 1) * GLA_KW]


def _glaprep(p_gl, prm):
    B, N, _ = p_gl.shape
    nt = N // TB
    sub = HALO
    hb = TB // sub
    nhb = N // sub
    return pl.pallas_call(
        functools.partial(_glaprep_kernel, nt),
        out_shape=[jax.ShapeDtypeStruct((B, N, GLA_KW), ACT), jax.ShapeDtypeStruct((B, N, GLA_KW), ACT),
                   jax.ShapeDtypeStruct((B, N, GLA_WIDTH), ACT), jax.ShapeDtypeStruct((B, N, GLA_WIDTH), ACT),
                   jax.ShapeDtypeStruct((2, B, N, GLA_KW), F32)],
        grid=(B, nt),
        in_specs=[_tile_spec(GLA_PAD),
                  pl.BlockSpec((1, sub, GLA_PAD), lambda b, j: (b, jnp.maximum(j * hb - 1, 0), 0)),
                  pl.BlockSpec((1, sub, GLA_PAD), lambda b, j: (b, jnp.minimum(j * hb + hb, nhb - 1), 0)),
                  _const_spec((2, TB, CONV_K)),
                  _const_spec((3, GLA_QKV)), _const_spec((LANES, 2 * GLA_KW)), _const_spec((1, 2 * GLA_KW))],
        out_specs=[_tile_spec(GLA_KW), _tile_spec(GLA_KW), _tile_spec(GLA_WIDTH), _tile_spec(GLA_WIDTH),
                   _dir_tile_spec(GLA_KW)],
        compiler_params=_cparams(("parallel", "parallel")),
        name="glaprep",
    )(p_gl, p_gl, p_gl, _conv_shift_table(), prm["conv"], prm["a_up"], prm["a_b"])


def _readout_kernel(group, first, split_src, *refs):
    tok = [refs[i * group:(i + 1) * group] for i in range(7)]
    refs = refs[7 * group:]
    n_res = group + 1 if split_src else group
    res_refs, refs = refs[:n_res], refs[n_res:]
    mod_ref, gnw_ref, gnb_ref, ggn_ref, wout_ref, gpost_ref, bd_ref, xo_ref = refs
    bd = bd_ref[...]
    starts_with_ctx = jnp.logical_and(first == 0, pl.program_id(1) == 0)
    for k in range(group):
        yf_ref, yb_ref, bon_ref, g_ref, of_ref, ob_ref, og_ref = (t[k] for t in tok)
        gate = mod_ref[0, 1, 2:3]
        if split_src:
            x_res = res_refs[k + 1][0]
            if k == 0:
                x_res = jnp.where(starts_with_ctx, res_refs[0][0], x_res)
        else:
            x_res = res_refs[k][0]
        if k == 0 and first == 0:
            gate = jnp.where(starts_with_ctx, mod_ref[0, 0, 2:3], gate)
        y = yf_ref[0].astype(F32) + yb_ref[0].astype(F32)
        mu = _seg_sum(y, bd, exact=False) * (1.0 / RW_HEAD)
        yc = y - mu
        var = _seg_sum(yc * yc, bd, exact=False) * (1.0 / RW_HEAD)
        yn = yc * lax.rsqrt(var + RW_GN_EPS) * gnw_ref[...] + gnb_ref[...]
        rw = (yn + bon_ref[0].astype(F32)) * g_ref[0].astype(F32)
        o = of_ref[0].astype(F32) + ob_ref[0].astype(F32)
        og = og_ref[0].astype(F32)
        parts = [rw.astype(BF16)]
        for h in range(GLA_HEADS):
            sl = slice(GLA_DV * h, GLA_DV * (h + 1))
            oh = o[:, sl]
            on = oh * lax.rsqrt(jnp.mean(oh * oh, axis=-1, keepdims=True) + GLA_NORM_EPS)
            ogh = og[:, sl]
            parts.append((on * ggn_ref[:, sl] * (ogh * _sigmoid(ogh))).astype(BF16))
        cat = jnp.concatenate(parts, axis=1)
        mx = jnp.dot(cat, wout_ref[...], preferred_element_type=F32)
        xo_ref[0, k * TB:(k + 1) * TB, :] = x_res + gate * _rms(mx, gpost_ref[...])


def _readout(y, bonus, g, o, og, ctx, x, xc, modtab, prm, latents_only):
    B, N, _ = bonus.shape
    first = 1 if latents_only else 0
    n_tiles = N // TB - first
    group = max(k for k in (4, 3, 2, 1) if n_tiles % k == 0)

    def tiles(c, shift=0):
        return [pl.BlockSpec((1, TB, c), lambda b, j, k=k: (b, jnp.maximum(first + group * j + k + shift, 0), 0))
                for k in range(group)]

    if xc is None:
        res, res_specs = [ctx] + [x] * group, [pl.BlockSpec((1, TB, D), lambda b, j: (b, 0, 0))] + tiles(D, -1)
    else:
        res, res_specs = [xc] * group, tiles(D)
    tok_args, tok_specs = [], []
    for arr in (y[0], y[1], bonus, g, o[0], o[1], og):
        tok_args += [arr] * group
        tok_specs += tiles(arr.shape[-1])
    return pl.pallas_call(
        functools.partial(_readout_kernel, group, first, xc is None),
        out_shape=jax.ShapeDtypeStruct((B, n_tiles * TB, D), F32),
        grid=(B, n_tiles // group),
        in_specs=tok_specs + res_specs + [_wide_mod_spec(),
                  _const_spec((1, RW_WIDTH)), _const_spec((1, RW_WIDTH)), _const_spec((1, GLA_WIDTH)),
                  _const_spec((D, D)), _const_spec((1, D)), _const_spec((RW_WIDTH, RW_WIDTH))],
        out_specs=pl.BlockSpec((1, group * TB, D), lambda b, j: (b, j, 0)),
        compiler_params=_cparams(("parallel", "parallel")),
        name="readout",
    )(*tok_args, *res, modtab, prm["gn_w"], prm["gn_b"], prm["gla_gn_w"],
      prm["w_out"], prm["norm_post"], prm["bd64"])


def _swiglu_acc(hb, wg_ref, wu_ref, wd_ref):
    acc = jnp.zeros((hb.shape[0], D), F32)
    for c in range(D_FF // FCH):
        sl = slice(c * FCH, (c + 1) * FCH)
        gate = jnp.dot(hb, wg_ref[:, sl].astype(BF16), preferred_element_type=F32)
        up = jnp.dot(hb, wu_ref[:, sl].astype(BF16), preferred_element_type=F32)
        act = (gate * _sigmoid(gate) * up).astype(BF16)
        acc = acc + jnp.dot(act, wd_ref[sl, :].astype(BF16), preferred_element_type=F32)
    return acc


def _ffn_kernel(x_ref, mod_ref, gpre_ref, gpost_ref, wg_ref, wu_ref, wd_ref, xo_ref):
    x = x_ref[0]
    hb = (_rms(x, gpre_ref[...]) * (1.0 + _wide_mod(mod_ref, 4)) + _wide_mod(mod_ref, 3)).astype(BF16)
    fx = _swiglu_acc(hb, wg_ref, wu_ref, wd_ref)
    xo_ref[0] = x + _wide_mod(mod_ref, 5) * _rms(fx, gpost_ref[...])


def _single_buffered(shape):
    nd = len(shape)
    return pl.BlockSpec(shape, lambda *_: (0,) * nd, pipeline_mode=pl.Buffered(1))


def _ffn(xc, modtab, g_pre, g_post, wg, wu, wd):
    B, N, _ = xc.shape
    wide = pl.BlockSpec((1, TW, D), lambda b, j: (b, j, 0))
    return pl.pallas_call(
        _ffn_kernel,
        out_shape=jax.ShapeDtypeStruct((B, N, D), F32),
        grid=(B, N // TW),
        in_specs=[wide, _wide_mod_spec(), _const_spec((1, D)), _const_spec((1, D)),
                  _single_buffered((D, D_FF)), _single_buffered((D, D_FF)), _single_buffered((D_FF, D))],
        out_specs=wide,
        compiler_params=_cparams(("parallel", "parallel")),
        name="ffn",
    )(xc, modtab, g_pre, g_post, wg, wu, wd)


MOE_TB = 1024
MOE_SEG = 32
MOE_TM = 512
MOE_R = 2 * MOE_TB + N_EXPERTS * MOE_SEG
MOE_NP = MOE_R // MOE_SEG


def _moe_route_kernel(x_ref, mod_ref, gpre_ref, router_ref, h_ref, info_ref, infot_ref, cnt_ref):
    mod = mod_ref[0, 0]
    h = _rms(x_ref[0], gpre_ref[...]) * (1.0 + mod[4:5]) + mod[3:4]
    hb = h.astype(BF16)
    h_ref[0] = hb
    lane = lax.broadcasted_iota(jnp.int32, (MOE_TB, LANES), 1)
    h_lo = (h - hb.astype(F32)).astype(BF16)
    logits = (jnp.dot(hb, router_ref[0], preferred_element_type=F32)
              + jnp.dot(hb, router_ref[1], preferred_element_type=F32)
              + jnp.dot(h_lo, router_ref[0], preferred_element_type=F32))
    logits = jnp.where(lane < N_EXPERTS, logits, -jnp.inf)
    v1 = jnp.max(logits, axis=-1, keepdims=True)
    i1 = jnp.min(jnp.where(logits == v1, lane, LANES), axis=-1, keepdims=True)
    rest = jnp.where(lane == i1, -jnp.inf, logits)
    v2 = jnp.max(rest, axis=-1, keepdims=True)
    i2 = jnp.min(jnp.where(rest == v2, lane, LANES), axis=-1, keepdims=True)
    ex = jnp.exp(v2 - v1)
    w1 = 1.0 / (1.0 + ex)
    w2 = ex * w1
    e1 = jnp.where(lane == i1, 1.0, 0.0)
    e2 = jnp.where(lane == i2, 1.0, 0.0)
    es = e1 + e2
    t = lax.broadcasted_iota(jnp.int32, (MOE_TB, MOE_TB), 0)
    s = lax.broadcasted_iota(jnp.int32, (MOE_TB, MOE_TB), 1)
    before = jnp.where(s < t, 1.0, 0.0).astype(BF16)
    rank = jnp.dot(before, es.astype(BF16), preferred_element_type=F32)
    cnt = jnp.sum(es, axis=0, keepdims=True)
    segs = jnp.floor((cnt + (MOE_SEG - 1)) * (1.0 / MOE_SEG))
    ea = lax.broadcasted_iota(jnp.int32, (LANES, LANES), 0)
    eb = lax.broadcasted_iota(jnp.int32, (LANES, LANES), 1)
    earlier = jnp.where(ea < eb, 1.0, 0.0).astype(BF16)
    start = jnp.dot(jnp.broadcast_to(segs, (SUBLANES, LANES)).astype(BF16), earlier,
                    preferred_element_type=F32)[0:1] * MOE_SEG
    pos = rank + start
    d1 = jnp.sum(e1 * pos, axis=-1, keepdims=True)
    d2 = jnp.sum(e2 * pos, axis=-1, keepdims=True)
    info = jnp.where(lane == 0, d1, jnp.where(lane == 1, d2, jnp.where(lane == 2, w1, jnp.where(lane == 3, w2, 0.0))))
    info_ref[0] = info
    infot_ref[0] = jnp.transpose(info)[0:SUBLANES]
    cnt_ref[0] = jnp.broadcast_to(cnt, (SUBLANES, LANES))


def _moe_route(xs, modtab, g_pre, router):
    B, S, _ = xs.shape
    per = S // MOE_TB
    nb = B * per
    blk = lambda c: pl.BlockSpec((1, MOE_TB, c), lambda i: (i // per, i % per, 0))
    flat = lambda r, c: pl.BlockSpec((1, r, c), lambda i: (i, 0, 0))
    return pl.pallas_call(
        _moe_route_kernel,
        out_shape=[jax.ShapeDtypeStruct((nb, MOE_TB, D), BF16), jax.ShapeDtypeStruct((nb, MOE_TB, LANES), F32),
                   jax.ShapeDtypeStruct((nb, SUBLANES, MOE_TB), F32),
                   jax.ShapeDtypeStruct((nb, SUBLANES, LANES), F32)],
        grid=(nb,),
        in_specs=[blk(D), pl.BlockSpec((1, 1, 6, D), lambda i: (i // per, 1, 0, 0)),
                  _const_spec((1, D)), _const_spec((2, D, LANES))],
        out_specs=[flat(MOE_TB, D), flat(MOE_TB, LANES), flat(SUBLANES, MOE_TB), flat(SUBLANES, LANES)],
        compiler_params=_cparams(("parallel",)),
        name="moe_route",
    )(xs, modtab, g_pre, router)


def _moe_plan(cnt, n_tiles):
    pc = (cnt + MOE_SEG - 1) // MOE_SEG * MOE_SEG
    inc = jnp.cumsum(pc, axis=1)
    loff = inc - pc
    reg = (jnp.sum(pc, axis=0) + MOE_TM - 1) // MOE_TM * MOE_TM
    gend = jnp.cumsum(reg)
    goff = (gend - reg)[None, :] + jnp.cumsum(pc, axis=0) - pc
    rows = jnp.arange(MOE_NP, dtype=jnp.int32) * MOE_SEG
    e_p = jnp.sum((rows[None, :, None] >= inc[:, None, :]).astype(jnp.int32), axis=-1)
    e_c = jnp.minimum(e_p, N_EXPERTS - 1)
    dst = jnp.take_along_axis(goff, e_c, axis=1) + rows[None, :] - jnp.take_along_axis(loff, e_c, axis=1)
    dst = jnp.where(e_p < N_EXPERTS, dst, 0).astype(jnp.int32)
    n_valid = (inc[:, -1] // MOE_SEG).astype(jnp.int32)
    trow = jnp.arange(n_tiles, dtype=jnp.int32) * MOE_TM
    te = jnp.sum((trow[:, None] >= gend[None, :]).astype(jnp.int32), axis=-1)
    valid = te < N_EXPERTS
    last = gend[-1] // MOE_TM - 1
    te = jnp.where(valid, te, te[last]).astype(jnp.int32)
    src = jnp.where(valid, jnp.arange(n_tiles, dtype=jnp.int32), last).astype(jnp.int32)
    return dst, n_valid, te, src, valid.astype(jnp.int32)


def _piece_copy(src_ref, src_row, dst_ref, dst_row, sem):
    return pltpu.make_async_copy(src_ref.at[pl.ds(src_row, MOE_SEG)], dst_ref.at[pl.ds(dst_row, MOE_SEG)], sem)


def _moe_gather_kernel(dst_ref, nv_ref, h_ref, infot_ref, xg_in_ref, xg_ref, buf_ref, sem_ref):
    del xg_in_ref
    i = pl.program_id(0)
    it = infot_ref[0]
    rr = lax.broadcasted_iota(jnp.int32, (MOE_R, MOE_TB), 0).astype(F32)
    onehot = jnp.where(rr == it[0:1], 1.0, jnp.where(rr == it[1:2], 1.0, 0.0)).astype(BF16)
    buf_ref[...] = jnp.dot(onehot, h_ref[0], preferred_element_type=F32).astype(BF16)
    nv = nv_ref[i]

    def piece(p):
        return _piece_copy(buf_ref, pl.multiple_of(p * MOE_SEG, MOE_SEG),
                           xg_ref, pl.multiple_of(dst_ref[i, p], MOE_SEG), sem_ref.at[p])

    def start(p, c):
        piece(p).start()
        return c

    def wait(p, c):
        piece(p).wait()
        return c

    lax.fori_loop(0, nv, start, 0)
    lax.fori_loop(0, nv, wait, 0)


def _moe_gather(dst, n_valid, h, info_t, n_rows):
    nb = h.shape[0]
    flat = lambda r, c: pl.BlockSpec((1, r, c), lambda i, *_: (i, 0, 0))
    return pl.pallas_call(
        _moe_gather_kernel,
        out_shape=jax.ShapeDtypeStruct((n_rows, D), BF16),
        grid_spec=pltpu.PrefetchScalarGridSpec(
            num_scalar_prefetch=2, grid=(nb,),
            in_specs=[flat(MOE_TB, D), flat(SUBLANES, MOE_TB), pl.BlockSpec(memory_space=pl.ANY)],
            out_specs=pl.BlockSpec(memory_space=pl.ANY),
            scratch_shapes=[pltpu.VMEM((MOE_R, D), BF16), pltpu.SemaphoreType.DMA((MOE_NP,))]),
        input_output_aliases={4: 0},
        compiler_params=_cparams(("arbitrary",)),
        name="moe_gather",
    )(dst, n_valid, h, info_t, jnp.zeros((n_rows, D), BF16))


def _moe_ffn_kernel(te_ref, src_ref, valid_ref, x_ref, wg_ref, wu_ref, wd_ref, y_ref):
    i = pl.program_id(0)

    @pl.when(valid_ref[i] == 1)
    def _():
        y_ref[...] = _swiglu_acc(x_ref[...], wg_ref.at[0], wu_ref.at[0], wd_ref.at[0]).astype(BF16)

    @pl.when(valid_ref[i] == 0)
    def _():
        y_ref[...] = jnp.zeros_like(y_ref)


def _moe_ffn(te, src, valid, xg, wg, wu, wd):
    n_rows = xg.shape[0]
    wspec = lambda r, c: pl.BlockSpec((1, r, c), lambda i, te, src, valid: (te[i], 0, 0),
                                      pipeline_mode=pl.Buffered(1))
    return pl.pallas_call(
        _moe_ffn_kernel,
        out_shape=jax.ShapeDtypeStruct((n_rows, D), BF16),
        grid_spec=pltpu.PrefetchScalarGridSpec(
            num_scalar_prefetch=3, grid=(n_rows // MOE_TM,),
            in_specs=[pl.BlockSpec((MOE_TM, D), lambda i, te, src, valid: (src[i], 0)),
                      wspec(D, D_FF), wspec(D, D_FF), wspec(D_FF, D)],
            out_specs=pl.BlockSpec((MOE_TM, D), lambda i, te, src, valid: (i, 0))),
        compiler_params=_cparams(("arbitrary",)),
        name="moe_ffn",
    )(te, src, valid, xg, wg, wu, wd)


def _moe_combine_kernel(dst_ref, nv_ref, x_ref, mod_ref, gpost_ref, info_ref, yg_ref, xo_ref, buf_ref, sem_ref):
    i = pl.program_id(0)
    nv = nv_ref[i]

    def piece(p):
        return _piece_copy(yg_ref, pl.multiple_of(dst_ref[i, p], MOE_SEG),
                           buf_ref, pl.multiple_of(p * MOE_SEG, MOE_SEG), sem_ref.at[p])

    def start(p, c):
        piece(p).start()
        return c

    def clear(p, c):
        buf_ref[pl.ds(pl.multiple_of(p * MOE_SEG, MOE_SEG), MOE_SEG), :] = jnp.zeros((MOE_SEG, D), BF16)
        return c

    def wait(p, c):
        piece(p).wait()
        return c

    lax.fori_loop(0, nv, start, 0)
    lax.fori_loop(nv, MOE_NP, clear, 0)
    info = info_ref[0]
    rr = lax.broadcasted_iota(jnp.int32, (MOE_TB, MOE_R), 1).astype(F32)
    comb = jnp.where(rr == info[:, 0:1], info[:, 2:3], jnp.where(rr == info[:, 1:2], info[:, 3:4], 0.0)).astype(BF16)
    lax.fori_loop(0, nv, wait, 0)
    fx = jnp.dot(comb, buf_ref[...], preferred_element_type=F32)
    xo_ref[0] = x_ref[0] + mod_ref[0, 0][5:6] * _rms(fx, gpost_ref[...])


def _moe_combine(dst, n_valid, xs, modtab, g_post, info, yg):
    B, S, _ = xs.shape
    per = S // MOE_TB
    blk = pl.BlockSpec((1, MOE_TB, D), lambda i, *_: (i // per, i % per, 0))
    return pl.pallas_call(
        _moe_combine_kernel,
        out_shape=jax.ShapeDtypeStruct((B, S, D), F32),
        grid_spec=pltpu.PrefetchScalarGridSpec(
            num_scalar_prefetch=2, grid=(B * per,),
            in_specs=[blk, pl.BlockSpec((1, 1, 6, D), lambda i, *_: (i // per, 1, 0, 0)),
                      pl.BlockSpec((1, D), lambda i, *_: (0, 0)),
                      pl.BlockSpec((1, MOE_TB, LANES), lambda i, *_: (i, 0, 0)),
                      pl.BlockSpec(memory_space=pl.ANY)],
            out_specs=blk,
            scratch_shapes=[pltpu.VMEM((MOE_R, D), BF16), pltpu.SemaphoreType.DMA((MOE_NP,))]),
        compiler_params=_cparams(("arbitrary",)),
        name="moe_combine",
    )(dst, n_valid, xs, modtab, g_post, info, yg)


def _moe(xs, modtab, g_pre, g_post, router, wg, wu, wd):
    B, S, _ = xs.shape
    nb = B * S // MOE_TB
    h, info, info_t, cnt = _moe_route(xs, modtab, g_pre, router)
    worst = 2 * B * S + nb * N_EXPERTS * (MOE_SEG - 1) + N_EXPERTS * (MOE_TM - 1)
    n_tiles = -(-worst // MOE_TM)
    dst, n_valid, te, src, valid = _moe_plan(cnt[:, 0, :N_EXPERTS].astype(jnp.int32), n_tiles)
    xg = _moe_gather(dst, n_valid, h, info_t, n_tiles * MOE_TM)
    yg = _moe_ffn(te, src, valid, xg, wg, wu, wd)
    return _moe_combine(dst, n_valid, xs, modtab, g_post, info, yg)


def _block_diag2(w):
    z = jnp.zeros_like(w[0])
    return jnp.concatenate([jnp.concatenate([w[0], z], axis=1), jnp.concatenate([z, w[1]], axis=1)], axis=0)


def _row(v):
    return v.reshape(1, -1).astype(F32)


def _head_ones(width, head):
    i = jnp.arange(width) // head
    return (i[:, None] == i[None, :]).astype(BF16)


def kernel(x, c, ctx, c_ctx, ada_w, ada_b, norm_mix_pre, norm_mix_post, norm_ffn_pre, norm_ffn_post, w_in, shift_mu, rw_w_up, rw_w0, rw_a_up, rw_a0, rw_k_k, rw_k_a, rw_r_k, rw_g_up, rw_gn_w, rw_gn_b, rw_v_down, rw_v_up, rw_v0, gla_conv, gla_a_up, gla_a_b, gla_gn_w, w_out, ffn_w_gate, ffn_w_up, ffn_w_down, moe_router, moe_w_gate, moe_w_up, moe_w_down):
    B, S, _ = x.shape
    n_ctx = ctx.shape[1]
    depth = w_in.shape[0]
    assert n_ctx == TB and S % MOE_TB == 0 and (n_ctx + S) % TW == 0 and depth == 2

    xc = None
    pad_rows = -(B + 1) % SUBLANES
    cvec = jnp.concatenate([c, c_ctx[None, :], jnp.zeros((pad_rows, D), F32)], axis=0)
    bd64 = _head_ones(RW_WIDTH, RW_HEAD)
    ada_b3 = ada_b.reshape(depth, 1, 6 * D)
    v_first = None
    out = None
    for i in range(depth):
        last = i == depth - 1
        mods = _adaln(cvec, ada_w, ada_b3, i)
        mod_x = mods[:B].reshape(B, 6, D)
        mod_c = jnp.broadcast_to(mods[B].reshape(1, 6, D), (B, 6, D))
        modtab = jnp.stack([mod_c, mod_x], axis=1)

        w_i = jnp.concatenate([w_in[i], jnp.zeros((D, GLA_PAD - GLA_COLS), F32)], axis=1).astype(BF16)
        p_rw, p_gl = _inproj(ctx, x, xc, _row(norm_mix_pre[i]), modtab, w_i)

        prm = dict(
            mu=_row(shift_mu[i]),
            w_up=_block_diag2(rw_w_up[i]).astype(BF16), w0=_row(rw_w0[i]),
            a_up=_block_diag2(rw_a_up[i]).astype(BF16), a0=_row(rw_a0[i]),
            k_k=_row(rw_k_k[i]), k_a=_row(rw_k_a[i]), r_k=_row(rw_r_k[i]),
            g_up=rw_g_up[i].astype(BF16), bd64=bd64,
            gn_w=_row(rw_gn_w[i]), gn_b=_row(rw_gn_b[i]), gla_gn_w=_row(gla_gn_w[i]),
            w_out=w_out[i].astype(BF16), norm_post=_row(norm_mix_post[i]),
        )
        gate_pad = jnp.zeros((LANES - 2 * GLA_GATE_RANK, 2 * GLA_KW), F32)
        gla_prm = dict(conv=gla_conv[i].astype(F32), a_b=_row(gla_a_b[i]),
                       a_up=jnp.concatenate([_block_diag2(gla_a_up[i]), gate_pad], axis=0).astype(BF16))
        if i > 0:
            pad = LANES - RW_V_RANK
            prm["v_down"] = jnp.concatenate([rw_v_down[i - 1], jnp.zeros((RW_WIDTH, pad), F32)], axis=1).astype(BF16)
            prm["v_up"] = jnp.concatenate([rw_v_up[i - 1], jnp.zeros((pad, RW_WIDTH), F32)], axis=0).astype(BF16)
            prm["v0"] = _row(rw_v0[i - 1])

        r, kk, vm, g, bonus, lw, bb, ke = _rwprep(p_rw, prm, v_first if i > 0 else None)
        if i == 0:
            v_first = vm
        q, k, gv, og, lg = _glaprep(p_gl, gla_prm)
        y, o = _scans(r, kk, vm, lw, bb, ke, q, k, gv, lg)
        xc = _readout(y, bonus, g, o, og, ctx, x, xc, modtab, prm, latents_only=last)

        jf = i // 2
        if i % 2 == 0:
            xc = _ffn(xc, modtab, _row(norm_ffn_pre[i]), _row(norm_ffn_post[i]),
                      ffn_w_gate[jf].astype(BF16), ffn_w_up[jf].astype(BF16), ffn_w_down[jf].astype(BF16))
        else:
            router = jnp.concatenate([moe_router[jf], jnp.zeros((D, LANES - N_EXPERTS), F32)], axis=1)
            r_hi = router.astype(BF16)
            router = jnp.stack([r_hi, (router - r_hi.astype(F32)).astype(BF16)])
            out = _moe(xc, modtab, _row(norm_ffn_pre[i]), _row(norm_ffn_post[i]), router,
                       moe_w_gate[jf], moe_w_up[jf], moe_w_down[jf])
    return out
```

```python
import functools
import math

import jax
import jax.numpy as jnp
from jax import lax
from jax.experimental import pallas as pl
from jax.experimental.pallas import tpu as pltpu

F32, BF16 = jnp.float32, jnp.bfloat16
ACT = BF16

D = 1024
GRID_W = 64
RW_WIDTH = 512
RW_HEAD = 64
RW_RANK = 64
RW_G_RANK = 128
RW_V_RANK = 32
RW_GN_EPS = 64e-5
GLA_WIDTH = 512
GLA_HEADS = 4
GLA_DV = 128
GLA_DK = 64
GLA_KW = 256
GLA_GATE_RANK = 16
GLA_GATE_NORM = 16.0
GLA_NORM_EPS = 1e-5
D_FF = 2816
N_EXPERTS = 8
NORM_EPS = 1e-6
RW_COLS = 3 * RW_WIDTH + 4 * RW_RANK + RW_G_RANK
GLA_QKV = 2 * GLA_KW + GLA_WIDTH
GLA_COLS = GLA_QKV + GLA_WIDTH + 2 * GLA_GATE_RANK
GLA_PAD = 1664

LANES = 128
SUBLANES = 8
TB = 256
TW = 3 * TB
CH = 64
FCH = 256
HALO = 16
VMEM_LIMIT = 56 * 1024 * 1024


def _cparams(sem):
    return pltpu.CompilerParams(dimension_semantics=sem, vmem_limit_bytes=VMEM_LIMIT)


def _bdot(a, b):
    return jnp.dot(a.astype(BF16), b.astype(BF16), preferred_element_type=F32)


def _bdot_nt(a, b):
    return lax.dot_general(a.astype(BF16), b.astype(BF16), (((1,), (1,)), ((), ())),
                           preferred_element_type=F32)


def _bdot_tn(a, b):
    return lax.dot_general(a.astype(BF16), b.astype(BF16), (((0,), (0,)), ((), ())),
                           preferred_element_type=F32)


def _split_dot(a_exact, x):
    h1 = x.astype(BF16)
    r1 = x - h1.astype(F32)
    h2 = r1.astype(BF16)
    h3 = (r1 - h2.astype(F32)).astype(BF16)
    return (jnp.dot(a_exact, h1, preferred_element_type=F32)
            + jnp.dot(a_exact, h2, preferred_element_type=F32)
            + jnp.dot(a_exact, h3, preferred_element_type=F32))


def _seg_sum(x, bd, exact=True):
    hi = x.astype(BF16)
    out = jnp.dot(hi, bd, preferred_element_type=F32)
    if exact:
        lo = (x - hi.astype(F32)).astype(BF16)
        out = out + jnp.dot(lo, bd, preferred_element_type=F32)
    return out


def _sigmoid(x):
    return jax.nn.sigmoid(x)


def _rms(x, g):
    return x * lax.rsqrt(jnp.mean(x * x, axis=-1, keepdims=True) + NORM_EPS) * g


def _pair_stack(z, lo):
    return jnp.concatenate([jnp.where(lo, z, 0.0), jnp.where(lo, 0.0, z)], axis=0)


def _adaln_kernel(c_ref, w_ref, b_ref, o_ref):
    c = c_ref[...]
    s = c * _sigmoid(c)
    o_ref[...] = jnp.dot(s, w_ref[0], precision=lax.Precision.HIGHEST,
                         preferred_element_type=F32) + b_ref[0]


def _adaln(cvec, w, b, layer):
    rows = cvec.shape[0]
    n = w.shape[2]
    return pl.pallas_call(
        _adaln_kernel,
        out_shape=jax.ShapeDtypeStruct((rows, n), F32),
        grid=(n // D,),
        in_specs=[pl.BlockSpec((rows, D), lambda i: (0, 0)),
                  pl.BlockSpec((1, D, D), lambda i: (layer, 0, i)),
                  pl.BlockSpec((1, 1, D), lambda i: (layer, 0, i))],
        out_specs=pl.BlockSpec((rows, D), lambda i: (0, i)),
        compiler_params=_cparams(("arbitrary",)),
        name="adaln",
    )(cvec, w, b)


def _wide_mod(mod_ref, k):
    is_ctx = jnp.logical_and(pl.program_id(1) == 0, lax.broadcasted_iota(jnp.int32, (TW, 1), 0) < TB)
    return jnp.where(is_ctx, mod_ref[0, 0, k:k + 1], mod_ref[0, 1, k:k + 1])


def _wide_rows(refs):
    if len(refs) == 1:
        return refs[0][0]
    first = jnp.where(pl.program_id(1) == 0, refs[0][0], refs[1][0])
    return jnp.concatenate([first] + [r[0] for r in refs[2:]], axis=0)


def _inproj_kernel(n_src, *refs):
    g_ref, mod_ref, w_ref, prw_ref, pgl_ref = refs[n_src:]
    x = _wide_rows(refs[:n_src])
    h = _rms(x, g_ref[...]) * (1.0 + _wide_mod(mod_ref, 1)) + _wide_mod(mod_ref, 0)
    hb = h.astype(BF16)
    prw_ref[0] = jnp.dot(hb, w_ref[:, :RW_COLS], preferred_element_type=F32).astype(ACT)
    pgl_ref[0] = jnp.dot(hb, w_ref[:, RW_COLS:], preferred_element_type=F32).astype(ACT)


def _wide_src(ctx, x, xc):
    if xc is not None:
        return [xc], [pl.BlockSpec((1, TW, D), lambda b, j: (b, j, 0))]
    per = TW // TB
    specs = [pl.BlockSpec((1, TB, D), lambda b, j: (b, 0, 0))]
    for k in range(per):
        specs.append(pl.BlockSpec((1, TB, D), lambda b, j, k=k: (b, jnp.maximum(j * per + k - 1, 0), 0)))
    return [ctx] + [x] * per, specs


def _wide_mod_spec():
    return pl.BlockSpec((1, 2, 6, D), lambda b, j: (b, 0, 0, 0))


def _const_spec(shape):
    nd = len(shape)
    return pl.BlockSpec(shape, lambda *_: (0,) * nd)


def _tile_spec(c):
    return pl.BlockSpec((1, TB, c), lambda b, j: (b, j, 0))


def _dir_tile_spec(c):
    return pl.BlockSpec((2, 1, TB, c), lambda b, j: (0, b, j, 0))


def _inproj(ctx, x, xc, g, modtab, w):
    B = modtab.shape[0]
    N = xc.shape[1] if xc is not None else ctx.shape[1] + x.shape[1]
    srcs, src_specs = _wide_src(ctx, x, xc)
    wide = lambda c: pl.BlockSpec((1, TW, c), lambda b, j: (b, j, 0))
    return pl.pallas_call(
        functools.partial(_inproj_kernel, len(srcs)),
        out_shape=[jax.ShapeDtypeStruct((B, N, RW_COLS), ACT),
                   jax.ShapeDtypeStruct((B, N, GLA_PAD), ACT)],
        grid=(B, N // TW),
        in_specs=src_specs + [_const_spec((1, D)), _wide_mod_spec(), _const_spec((D, RW_COLS + GLA_PAD))],
        out_specs=[wide(RW_COLS), wide(GLA_PAD)],
        compiler_params=_cparams(("parallel", "parallel")),
        name="inproj",
    )(*srcs, g, modtab, w)


def _shift_table():
    t = jnp.arange(TB)
    prev = t[:, None] - 1 == t[None, :]
    nxt = t[:, None] + 1 == t[None, :]
    col = (t % GRID_W)[:, None]
    return jnp.stack([jnp.stack([prev, nxt]),
                      jnp.stack([prev & (col != 0), nxt & (col != GRID_W - 1)])]).astype(BF16)


def _rwprep_kernel(has_vres, n_tiles, p_ref, hu_ref, hd_ref, sh_ref, mu_ref, wup_ref, w0_ref, aup_ref,
                   a0_ref, kk_ref, ka_ref, rk_ref, gup_ref, bd_ref, *rest):
    if has_vres:
        vf_ref, vdn_ref, vup_ref, v0_ref = rest[:4]
        rest = rest[4:]
    r_out, kk_out, v_out, g_out, bon_out, lw_out, b_out, ke_out = rest
    j = pl.program_id(1)
    pb = p_ref[0]
    p = pb.astype(F32)
    prev1 = jnp.dot(sh_ref[0, 0], pb, preferred_element_type=F32)
    next1 = jnp.dot(sh_ref[0, 1], pb, preferred_element_type=F32)
    has_upper = jnp.where(j == 1, 0.0, 1.0)
    has_lower = jnp.where(j == n_tiles - 1, 0.0, 1.0)
    up = jnp.concatenate([hu_ref[0].astype(F32) * has_upper, p[:TB - GRID_W]], axis=0)
    down = jnp.concatenate([p[GRID_W:], hd_ref[0].astype(F32) * has_lower], axis=0)
    cls = lax.broadcasted_iota(jnp.int32, p.shape, 1) & jnp.where(j == 0, 1, 3)
    shifted = jnp.where(cls == 0, prev1, jnp.where(cls == 1, next1, jnp.where(cls == 2, up, down)))
    u = p + mu_ref[...] * (shifted - p)

    r = u[:, 0:RW_WIDTH]
    k = u[:, RW_WIDTH:2 * RW_WIDTH]
    v = u[:, 2 * RW_WIDTH:3 * RW_WIDTH]
    o = 3 * RW_WIDTH
    wd = u[:, o:o + 2 * RW_RANK]
    ad = u[:, o + 2 * RW_RANK:o + 4 * RW_RANK]
    gd = u[:, o + 4 * RW_RANK:]

    w_logit = w0_ref[...] + _bdot(jnp.tanh(wd), wup_ref[...])
    lw = -math.exp(-0.5) * _sigmoid(w_logit)
    a = _sigmoid(a0_ref[...] + _bdot(ad, aup_ref[...]))
    bd = bd_ref[...]
    kk = k * kk_ref[...]
    kk = kk * lax.rsqrt(jnp.maximum(_seg_sum(kk * kk, bd), 1e-24))
    g = _bdot(_sigmoid(gd), gup_ref[...])
    if has_vres:
        gate = _sigmoid(v0_ref[...] + _bdot(_bdot(v, vdn_ref[...]), vup_ref[...]))
        vm = v + (vf_ref[0].astype(F32) - v) * gate
    else:
        vm = v
    ke_sum = jnp.zeros_like(k)
    for d in range(2):
        a_d = a[:, d * RW_WIDTH:(d + 1) * RW_WIDTH]
        ke_d = k * (1.0 + (a_d - 1.0) * ka_ref[...])
        lw_out[d, 0] = lw[:, d * RW_WIDTH:(d + 1) * RW_WIDTH]
        b_out[d, 0] = (kk * a_d).astype(ACT)
        ke_out[d, 0] = ke_d.astype(ACT)
        ke_sum = ke_sum + ke_d
    r_out[0] = r.astype(ACT)
    kk_out[0] = kk.astype(ACT)
    v_out[0] = vm.astype(ACT)
    g_out[0] = g.astype(ACT)
    bon_out[0] = (_seg_sum(r * ke_sum * rk_ref[...], bd, exact=False) * vm).astype(ACT)


def _rwprep(p_rw, prm, v_first):
    B, N, _ = p_rw.shape
    nt = N // TB
    hb = TB // GRID_W
    nhb = N // GRID_W
    has_vres = v_first is not None
    W2 = 2 * RW_WIDTH
    in_specs = [
        _tile_spec(RW_COLS),
        pl.BlockSpec((1, GRID_W, RW_COLS), lambda b, j: (b, jnp.maximum(j * hb - 1, 0), 0)),
        pl.BlockSpec((1, GRID_W, RW_COLS), lambda b, j: (b, jnp.minimum(j * hb + hb, nhb - 1), 0)),
        pl.BlockSpec((1, 2, TB, TB), lambda b, j: (jnp.minimum(j, 1), 0, 0, 0)),
        _const_spec((1, RW_COLS)), _const_spec((2 * RW_RANK, W2)), _const_spec((1, W2)),
        _const_spec((2 * RW_RANK, W2)), _const_spec((1, W2)),
        _const_spec((1, RW_WIDTH)), _const_spec((1, RW_WIDTH)), _const_spec((1, RW_WIDTH)),
        _const_spec((RW_G_RANK, RW_WIDTH)), _const_spec((RW_WIDTH, RW_WIDTH)),
    ]
    args = [p_rw, p_rw, p_rw, _shift_table(), prm["mu"], prm["w_up"], prm["w0"], prm["a_up"], prm["a0"],
            prm["k_k"], prm["k_a"], prm["r_k"], prm["g_up"], prm["bd64"]]
    if has_vres:
        in_specs += [_tile_spec(RW_WIDTH), _const_spec((RW_WIDTH, LANES)), _const_spec((LANES, RW_WIDTH)),
                     _const_spec((1, RW_WIDTH))]
        args += [v_first, prm["v_down"], prm["v_up"], prm["v0"]]
    tok = jax.ShapeDtypeStruct((B, N, RW_WIDTH), ACT)
    dtok = jax.ShapeDtypeStruct((2, B, N, RW_WIDTH), ACT)
    return pl.pallas_call(
        functools.partial(_rwprep_kernel, has_vres, nt),
        out_shape=[tok] * 5 + [jax.ShapeDtypeStruct((2, B, N, RW_WIDTH), F32), dtok, dtok],
        grid=(B, nt),
        in_specs=in_specs,
        out_specs=[_tile_spec(RW_WIDTH)] * 5 + [_dir_tile_spec(RW_WIDTH)] * 3,
        compiler_params=_cparams(("parallel", "parallel")),
        name="rwprep",
    )(*args)


NCH = TB // CH


def _scan_kernel(rf_ref, kkf_ref, vf_ref, lwf_ref, bf_ref, kef_ref,
                 rb_ref, kkb_ref, vb_ref, lwb_ref, bb_ref, keb_ref,
                 gqf_ref, gkf_ref, gvf_ref, lgf_ref, gqb_ref, gkb_ref, gvb_ref, lgb_ref,
                 yf_ref, yb_ref, of_ref, ob_ref, st_ref, gst_ref):
    j = pl.program_id(1)

    @pl.when(j == 0)
    def _():
        st_ref[...] = jnp.zeros_like(st_ref)
        gst_ref[...] = jnp.zeros_like(gst_ref)

    ti = lax.broadcasted_iota(jnp.int32, (CH, LANES), 0)
    li = lax.broadcasted_iota(jnp.int32, (CH, LANES), 1)
    si = li & (CH - 1)
    lo = li < CH
    eye = jnp.where(ti == si, 1.0, 0.0)
    rblk = lax.broadcasted_iota(jnp.int32, (LANES, LANES), 0) // CH
    cblk = lax.broadcasted_iota(jnp.int32, (LANES, LANES), 1) // CH
    bdmask = rblk == cblk
    t64 = lax.broadcasted_iota(jnp.int32, (CH, CH), 0)
    s64 = lax.broadcasted_iota(jnp.int32, (CH, CH), 1)
    tri = [jnp.where(s64 <= t64, 1.0, 0.0).astype(BF16), jnp.where(s64 >= t64, 1.0, 0.0).astype(BF16)]
    m_incl = [si <= ti, si >= ti]
    m_strict = [si < ti, si > ti]
    n_pairs = RW_WIDTH // LANES
    in_refs = ((rf_ref, kkf_ref, vf_ref, lwf_ref, bf_ref, kef_ref),
               (rb_ref, kkb_ref, vb_ref, lwb_ref, bb_ref, keb_ref))
    y_refs = (yf_ref, yb_ref)

    def stack(z):
        return _pair_stack(z, lo).astype(BF16)

    st = {(d, p): st_ref[d, p] for d in range(2) for p in range(n_pairs)}

    def scan_step(step):
        cur = {}
        w_tot = {}
        for d in range(2):
            c = (NCH - 1 - step) if d == 1 else step
            rows = slice(c * CH, (c + 1) * CH)
            refs = in_refs[d]
            r, kk, v = (refs[i][0, rows, :].astype(F32) for i in range(3))
            lw = refs[3][0, 0, rows, :]
            b, ke = refs[4][0, 0, rows, :].astype(F32), refs[5][0, 0, rows, :].astype(F32)
            cum = _split_dot(tri[d], lw)
            tot = jnp.sum(lw, axis=0, keepdims=True)
            w_inv = jnp.exp(-cum)
            w_end = jnp.exp(tot - cum)
            w_tot[d] = jnp.exp(tot)
            rh = r * jnp.exp(cum)
            ah = -(kk * jnp.exp(cum - lw))
            bh, kh, bt, kt = b * w_inv, ke * w_inv, b * w_end, ke * w_end
            for p in range(n_pairs):
                sl = slice(LANES * p, LANES * (p + 1))
                cur[d, p] = dict(ah=ah[:, sl], rh=rh[:, sl], bh=bh[:, sl], kh=kh[:, sl], bt=bt[:, sl],
                                 kt=kt[:, sl], v=v[:, sl], rows=rows, sl=sl)
        yield
        for it in cur.values():
            it["lhs"] = jnp.concatenate([it["ah"], it["rh"]], axis=0).astype(BF16)
            rhs = jnp.concatenate([stack(it["bh"]), stack(it["kh"])], axis=0)
            it["a_all"] = _bdot_nt(it["lhs"], rhs)
        yield
        for (d, _), it in cur.items():
            a_all = it.pop("a_all")
            it["a_ab"] = jnp.where(m_strict[d], a_all[:CH, :LANES], 0.0)
            a_ak = jnp.where(m_strict[d], a_all[:CH, LANES:], 0.0)
            a_rb = jnp.where(m_incl[d], a_all[CH:, :LANES], 0.0)
            a_rk = jnp.where(m_incl[d], a_all[CH:, LANES:], 0.0)
            it["a_r"] = jnp.concatenate([a_rb, a_rk], axis=1).astype(BF16)
            it["v_bd"] = stack(it["v"])
            it["akv"] = _bdot(a_ak, it["v_bd"])
            it["t"] = eye + it["a_ab"]
            it["m"] = _bdot(it["a_ab"], stack(it["a_ab"]))
        yield
        for _ in range(int(math.log2(CH)) - 2):
            for it in cur.values():
                z = _bdot(jnp.concatenate([it["m"], it["t"]], axis=0), stack(it["m"]))
                it["m"] = z[:CH]
                it["t"] = it["t"] + z[CH:]
            yield
        for it in cur.values():
            it["t"] = (it["t"] + _bdot(it["t"], stack(it["m"]))).astype(BF16)
        yield
        for it in cur.values():
            a_til = jnp.dot(it["t"], stack(it["ah"]), preferred_element_type=F32)
            it["lhs2"] = jnp.concatenate([a_til.astype(BF16), it["lhs"][CH:]], axis=0)
            it["cc"] = jnp.dot(it["t"], stack(it["akv"]), preferred_element_type=F32)
            it["rhs_t"] = jnp.concatenate([it["bt"], it["kt"]], axis=0).astype(BF16)
        yield
        z1 = {k: _bdot_nt(it["lhs2"], st[k]) for k, it in cur.items()}
        yield
        u = {k: z1[k][:CH] + it["cc"] for k, it in cur.items()}
        upd = {k: _bdot_tn(jnp.concatenate([u[k], it["v"]], axis=0), it["rhs_t"]) for k, it in cur.items()}
        yield
        for (d, p), it in cur.items():
            y = z1[d, p][CH:] + jnp.dot(it["a_r"], jnp.concatenate([stack(u[d, p]), it["v_bd"]], axis=0),
                                        preferred_element_type=F32)
            y_refs[d][0, it["rows"], it["sl"]] = y.astype(ACT)
            st[d, p] = st[d, p] * w_tot[d][:, it["sl"]] + jnp.where(bdmask, upd[d, p], 0.0)
        yield

    gla_refs = ((gqf_ref, gkf_ref, gvf_ref, lgf_ref), (gqb_ref, gkb_ref, gvb_ref, lgb_ref))
    o_refs = (of_ref, ob_ref)
    g_pairs = GLA_KW // LANES
    gst = {(d, p): gst_ref[d, p] for d in range(2) for p in range(g_pairs)}
    lo2 = lax.broadcasted_iota(jnp.int32, (LANES, LANES), 1) < CH
    zeros_v = jnp.zeros((CH, GLA_DV), F32)

    def gla_step(step):
        cur = {}
        dec = {}
        for d in range(2):
            c = (NCH - 1 - step) if d == 1 else step
            rows = slice(c * CH, (c + 1) * CH)
            refs = gla_refs[d]
            q, k, v = (refs[i][0, rows, :].astype(F32) for i in range(3))
            lg = refs[3][0, 0, rows, :]
            cum = _split_dot(tri[d], lg)
            tot = jnp.sum(lg, axis=0, keepdims=True)
            dec[d] = jnp.exp(tot)
            qd = q * jnp.exp(cum)
            ki = k * jnp.exp(-cum)
            kend = k * jnp.exp(tot - cum)
            for p in range(g_pairs):
                sl = slice(LANES * p, LANES * (p + 1))
                cur[d, p] = dict(qd=qd[:, sl].astype(BF16), ki=ki[:, sl], kend=kend[:, sl], rows=rows, sl=sl,
                                 v0=v[:, 2 * LANES * p:2 * LANES * p + LANES],
                                 v1=v[:, 2 * LANES * p + LANES:2 * LANES * (p + 1)])
        yield
        for it in cur.values():
            it["att"] = _bdot_nt(it["qd"], _pair_stack(it["ki"], lo))
        yield
        for (d, _), it in cur.items():
            att = jnp.where(m_incl[d], it["att"], 0.0)
            v_bd = jnp.concatenate([jnp.concatenate([it["v0"], zeros_v], axis=1),
                                    jnp.concatenate([zeros_v, it["v1"]], axis=1)], axis=0)
            it["o"] = _bdot(att, v_bd)
            it["upd"] = _bdot_tn(jnp.concatenate([it["v0"], it["v1"]], axis=0), _pair_stack(it["kend"], lo))
        yield
        for (d, p), it in cur.items():
            s = gst[d, p]
            s_bd = jnp.concatenate([jnp.where(lo2, s, 0.0), jnp.where(lo2, 0.0, s)], axis=0)
            o = it["o"] + _bdot_nt(it["qd"], s_bd)
            o_refs[d][0, it["rows"], 2 * LANES * p:2 * LANES * (p + 1)] = o.astype(ACT)
            gst[d, p] = s * dec[d][:, it["sl"]] + it["upd"]
        yield

    n_stages = 12
    n_dep = 3
    gla_stages = (1, 4, 7, 10)
    pipeline = [(scan_step(s), gla_step(s)) for s in range(NCH)]
    for slot in range(n_stages + n_dep * (NCH - 1)):
        for s, (rw_gen, gla_gen) in enumerate(pipeline):
            stage = slot - n_dep * s
            if 0 <= stage < n_stages:
                next(rw_gen)
                if stage in gla_stages:
                    next(gla_gen)
    for (d, p), s in st.items():
        st_ref[d, p] = s
    for (d, p), s in gst.items():
        gst_ref[d, p] = s


def _bidir_specs(c, n_tiles):
    def back(j):
        return jnp.where(j == 0, 0, n_tiles - j)

    tok_f = pl.BlockSpec((1, TB, c), lambda b, j: (b, j, 0))
    tok_b = pl.BlockSpec((1, TB, c), lambda b, j: (b, back(j), 0))
    dir_f = pl.BlockSpec((1, 1, TB, c), lambda b, j: (0, b, j, 0))
    dir_b = pl.BlockSpec((1, 1, TB, c), lambda b, j: (1, b, back(j), 0))
    return tok_f, tok_b, dir_f, dir_b


def _scans(r, kk, v, lw, b, ke, q, k, gv, lg):
    B, N, _ = r.shape
    nt = N // TB
    tok_f, tok_b, dir_f, dir_b = _bidir_specs(RW_WIDTH, nt)
    kf, kb, dkf, dkb = _bidir_specs(GLA_KW, nt)
    out = jax.ShapeDtypeStruct((B, N, RW_WIDTH), ACT)
    assert GLA_WIDTH == RW_WIDTH
    yf, yb, of, ob = pl.pallas_call(
        _scan_kernel,
        out_shape=[out] * 4,
        grid=(B, nt),
        in_specs=[tok_f, tok_f, tok_f, dir_f, dir_f, dir_f, tok_b, tok_b, tok_b, dir_b, dir_b, dir_b,
                  kf, kf, tok_f, dkf, kb, kb, tok_b, dkb],
        out_specs=[tok_f, tok_b, tok_f, tok_b],
        scratch_shapes=[pltpu.VMEM((2, RW_WIDTH // LANES, LANES, LANES), F32),
                        pltpu.VMEM((2, GLA_KW // LANES, GLA_DV, LANES), F32)],
        compiler_params=_cparams(("parallel", "arbitrary")),
        name="scans",
    )(r, kk, v, lw, b, ke, r, kk, v, lw, b, ke, q, k, gv, lg, q, k, gv, lg)
    return (yf, yb), (of, ob)


CONV_K = TB + LANES


def _conv_shift_table():
    t = jnp.arange(TB)[:, None]
    s = jnp.arange(CONV_K)[None, :]
    prev = jnp.where(t == 0, s == TB + 2 * HALO - 1, s == t - 1)
    nxt = jnp.where(t == TB - 1, s == TB, s == t + 1)
    return jnp.stack([prev, nxt]).astype(BF16)


def _glaprep_kernel(n_tiles, p_ref, hp_ref, hn_ref, sh_ref, cw_ref, aup_ref, ab_ref, q_out, k_out, v_out,
                    og_out, lg_out):
    j = pl.program_id(1)
    ub = p_ref[0][:, :GLA_QKV]
    u = ub.astype(F32)
    has_prev = jnp.where(j <= 1, 0.0, 1.0)
    has_next = jnp.where(jnp.logical_or(j == 0, j == n_tiles - 1), 0.0, 1.0)
    ext = jnp.concatenate([ub,
                           (hn_ref[0][:, :GLA_QKV].astype(F32) * has_next).astype(BF16),
                           (hp_ref[0][:, :GLA_QKV].astype(F32) * has_prev).astype(BF16),
                           jnp.zeros((CONV_K - TB - 2 * HALO, GLA_QKV), BF16)], axis=0)
    prev1 = jnp.dot(sh_ref[0], ext, preferred_element_type=F32)
    next1 = jnp.dot(sh_ref[1], ext, preferred_element_type=F32)
    cw = cw_ref[...]
    conv = cw[0:1] * prev1 + cw[1:2] * u + cw[2:3] * next1
    qkv = conv * _sigmoid(conv)
    q_out[0] = (qkv[:, :GLA_KW] * (GLA_DK ** -0.5)).astype(ACT)
    k_out[0] = qkv[:, GLA_KW:2 * GLA_KW].astype(ACT)
    v_out[0] = qkv[:, 2 * GLA_KW:].astype(ACT)
    og_out[0] = p_ref[0][:, GLA_QKV:GLA_QKV + GLA_WIDTH]
    z = _bdot(p_ref[0][:, GLA_QKV + GLA_WIDTH:], aup_ref[...]) + ab_ref[...]
    lg = (jnp.minimum(z, 0.0) - jnp.log1p(jnp.exp(-jnp.abs(z)))) * (1.0 / GLA_GATE_NORM)
    for d in range(2):
        lg_out[d, 0] = lg[:, d * GLA_KW:(d + 1) * GLA_KW]


def _glaprep(p_gl, prm):
    B, N, _ = p_gl.shape
    nt = N // TB
    sub = HALO
    hb = TB // sub
    nhb = N // sub
    return pl.pallas_call(
        functools.partial(_glaprep_kernel, nt),
        out_shape=[jax.ShapeDtypeStruct((B, N, GLA_KW), ACT), jax.ShapeDtypeStruct((B, N, GLA_KW), ACT),
                   jax.ShapeDtypeStruct((B, N, GLA_WIDTH), ACT), jax.ShapeDtypeStruct((B, N, GLA_WIDTH), ACT),
                   jax.ShapeDtypeStruct((2, B, N, GLA_KW), F32)],
        grid=(B, nt),
        in_specs=[_tile_spec(GLA_PAD),
                  pl.BlockSpec((1, sub, GLA_PAD), lambda b, j: (b, jnp.maximum(j * hb - 1, 0), 0)),
                  pl.BlockSpec((1, sub, GLA_PAD), lambda b, j: (b, jnp.minimum(j * hb + hb, nhb - 1), 0)),
                  _const_spec((2, TB, CONV_K)),
                  _const_spec((3, GLA_QKV)), _const_spec((LANES, 2 * GLA_KW)), _const_spec((1, 2 * GLA_KW))],
        out_specs=[_tile_spec(GLA_KW), _tile_spec(GLA_KW), _tile_spec(GLA_WIDTH), _tile_spec(GLA_WIDTH),
                   _dir_tile_spec(GLA_KW)],
        compiler_params=_cparams(("parallel", "parallel")),
        name="glaprep",
    )(p_gl, p_gl, p_gl, _conv_shift_table(), prm["conv"], prm["a_up"], prm["a_b"])


def _readout_kernel(group, first, split_src, *refs):
    tok = [refs[i * group:(i + 1) * group] for i in range(7)]
    refs = refs[7 * group:]
    n_res = group + 1 if split_src else group
    res_refs, refs = refs[:n_res], refs[n_res:]
    mod_ref, gnw_ref, gnb_ref, ggn_ref, wout_ref, gpost_ref, bd_ref, xo_ref = refs
    bd = bd_ref[...]
    starts_with_ctx = jnp.logical_and(first == 0, pl.program_id(1) == 0)
    for k in range(group):
        yf_ref, yb_ref, bon_ref, g_ref, of_ref, ob_ref, og_ref = (t[k] for t in tok)
        gate = mod_ref[0, 1, 2:3]
        if split_src:
            x_res = res_refs[k + 1][0]
            if k == 0:
                x_res = jnp.where(starts_with_ctx, res_refs[0][0], x_res)
        else:
            x_res = res_refs[k][0]
        if k == 0 and first == 0:
            gate = jnp.where(starts_with_ctx, mod_ref[0, 0, 2:3], gate)
        y = yf_ref[0].astype(F32) + yb_ref[0].astype(F32)
        mu = _seg_sum(y, bd, exact=False) * (1.0 / RW_HEAD)
        yc = y - mu
        var = _seg_sum(yc * yc, bd, exact=False) * (1.0 / RW_HEAD)
        yn = yc * lax.rsqrt(var + RW_GN_EPS) * gnw_ref[...] + gnb_ref[...]
        rw = (yn + bon_ref[0].astype(F32)) * g_ref[0].astype(F32)
        o = of_ref[0].astype(F32) + ob_ref[0].astype(F32)
        og = og_ref[0].astype(F32)
        parts = [rw.astype(BF16)]
        for h in range(GLA_HEADS):
            sl = slice(GLA_DV * h, GLA_DV * (h + 1))
            oh = o[:, sl]
            on = oh * lax.rsqrt(jnp.mean(oh * oh, axis=-1, keepdims=True) + GLA_NORM_EPS)
            ogh = og[:, sl]
            parts.append((on * ggn_ref[:, sl] * (ogh * _sigmoid(ogh))).astype(BF16))
        cat = jnp.concatenate(parts, axis=1)
        mx = jnp.dot(cat, wout_ref[...], preferred_element_type=F32)
        xo_ref[0, k * TB:(k + 1) * TB, :] = x_res + gate * _rms(mx, gpost_ref[...])


def _readout(y, bonus, g, o, og, ctx, x, xc, modtab, prm, latents_only):
    B, N, _ = bonus.shape
    first = 1 if latents_only else 0
    n_tiles = N // TB - first
    group = max(k for k in (4, 3, 2, 1) if n_tiles % k == 0)

    def tiles(c, shift=0):
        return [pl.BlockSpec((1, TB, c), lambda b, j, k=k: (b, jnp.maximum(first + group * j + k + shift, 0), 0))
                for k in range(group)]

    if xc is None:
        res, res_specs = [ctx] + [x] * group, [pl.BlockSpec((1, TB, D), lambda b, j: (b, 0, 0))] + tiles(D, -1)
    else:
        res, res_specs = [xc] * group, tiles(D)
    tok_args, tok_specs = [], []
    for arr in (y[0], y[1], bonus, g, o[0], o[1], og):
        tok_args += [arr] * group
        tok_specs += tiles(arr.shape[-1])
    return pl.pallas_call(
        functools.partial(_readout_kernel, group, first, xc is None),
        out_shape=jax.ShapeDtypeStruct((B, n_tiles * TB, D), F32),
        grid=(B, n_tiles // group),
        in_specs=tok_specs + res_specs + [_wide_mod_spec(),
                  _const_spec((1, RW_WIDTH)), _const_spec((1, RW_WIDTH)), _const_spec((1, GLA_WIDTH)),
                  _const_spec((D, D)), _const_spec((1, D)), _const_spec((RW_WIDTH, RW_WIDTH))],
        out_specs=pl.BlockSpec((1, group * TB, D), lambda b, j: (b, j, 0)),
        compiler_params=_cparams(("parallel", "parallel")),
        name="readout",
    )(*tok_args, *res, modtab, prm["gn_w"], prm["gn_b"], prm["gla_gn_w"],
      prm["w_out"], prm["norm_post"], prm["bd64"])


def _swiglu_acc(hb, wg_ref, wu_ref, wd_ref):
    acc = jnp.zeros((hb.shape[0], D), F32)
    for c in range(D_FF // FCH):
        sl = slice(c * FCH, (c + 1) * FCH)
        gate = jnp.dot(hb, wg_ref[:, sl].astype(BF16), preferred_element_type=F32)
        up = jnp.dot(hb, wu_ref[:, sl].astype(BF16), preferred_element_type=F32)
        act = (gate * _sigmoid(gate) * up).astype(BF16)
        acc = acc + jnp.dot(act, wd_ref[sl, :].astype(BF16), preferred_element_type=F32)
    return acc


def _ffn_kernel(x_ref, mod_ref, gpre_ref, gpost_ref, wg_ref, wu_ref, wd_ref, xo_ref):
    x = x_ref[0]
    hb = (_rms(x, gpre_ref[...]) * (1.0 + _wide_mod(mod_ref, 4)) + _wide_mod(mod_ref, 3)).astype(BF16)
    fx = _swiglu_acc(hb, wg_ref, wu_ref, wd_ref)
    xo_ref[0] = x + _wide_mod(mod_ref, 5) * _rms(fx, gpost_ref[...])


def _single_buffered(shape):
    nd = len(shape)
    return pl.BlockSpec(shape, lambda *_: (0,) * nd, pipeline_mode=pl.Buffered(1))


def _ffn(xc, modtab, g_pre, g_post, wg, wu, wd):
    B, N, _ = xc.shape
    wide = pl.BlockSpec((1, TW, D), lambda b, j: (b, j, 0))
    return pl.pallas_call(
        _ffn_kernel,
        out_shape=jax.ShapeDtypeStruct((B, N, D), F32),
        grid=(B, N // TW),
        in_specs=[wide, _wide_mod_spec(), _const_spec((1, D)), _const_spec((1, D)),
                  _single_buffered((D, D_FF)), _single_buffered((D, D_FF)), _single_buffered((D_FF, D))],
        out_specs=wide,
        compiler_params=_cparams(("parallel", "parallel")),
        name="ffn",
    )(xc, modtab, g_pre, g_post, wg, wu, wd)


MOE_TB = 1024
MOE_SEG = 32
MOE_TM = 512
MOE_R = 2 * MOE_TB + N_EXPERTS * MOE_SEG
MOE_NP = MOE_R // MOE_SEG


def _moe_route_kernel(x_ref, mod_ref, gpre_ref, router_ref, h_ref, info_ref, infot_ref, cnt_ref):
    mod = mod_ref[0, 0]
    h = _rms(x_ref[0], gpre_ref[...]) * (1.0 + mod[4:5]) + mod[3:4]
    hb = h.astype(BF16)
    h_ref[0] = hb
    lane = lax.broadcasted_iota(jnp.int32, (MOE_TB, LANES), 1)
    h_lo = (h - hb.astype(F32)).astype(BF16)
    logits = (jnp.dot(hb, router_ref[0], preferred_element_type=F32)
              + jnp.dot(hb, router_ref[1], preferred_element_type=F32)
              + jnp.dot(h_lo, router_ref[0], preferred_element_type=F32))
    logits = jnp.where(lane < N_EXPERTS, logits, -jnp.inf)
    v1 = jnp.max(logits, axis=-1, keepdims=True)
    i1 = jnp.min(jnp.where(logits == v1, lane, LANES), axis=-1, keepdims=True)
    rest = jnp.where(lane == i1, -jnp.inf, logits)
    v2 = jnp.max(rest, axis=-1, keepdims=True)
    i2 = jnp.min(jnp.where(rest == v2, lane, LANES), axis=-1, keepdims=True)
    ex = jnp.exp(v2 - v1)
    w1 = 1.0 / (1.0 + ex)
    w2 = ex * w1
    e1 = jnp.where(lane == i1, 1.0, 0.0)
    e2 = jnp.where(lane == i2, 1.0, 0.0)
    es = e1 + e2
    t = lax.broadcasted_iota(jnp.int32, (MOE_TB, MOE_TB), 0)
    s = lax.broadcasted_iota(jnp.int32, (MOE_TB, MOE_TB), 1)
    before = jnp.where(s < t, 1.0, 0.0).astype(BF16)
    rank = jnp.dot(before, es.astype(BF16), preferred_element_type=F32)
    cnt = jnp.sum(es, axis=0, keepdims=True)
    segs = jnp.floor((cnt + (MOE_SEG - 1)) * (1.0 / MOE_SEG))
    ea = lax.broadcasted_iota(jnp.int32, (LANES, LANES), 0)
    eb = lax.broadcasted_iota(jnp.int32, (LANES, LANES), 1)
    earlier = jnp.where(ea < eb, 1.0, 0.0).astype(BF16)
    start = jnp.dot(jnp.broadcast_to(segs, (SUBLANES, LANES)).astype(BF16), earlier,
                    preferred_element_type=F32)[0:1] * MOE_SEG
    pos = rank + start
    d1 = jnp.sum(e1 * pos, axis=-1, keepdims=True)
    d2 = jnp.sum(e2 * pos, axis=-1, keepdims=True)
    info = jnp.where(lane == 0, d1, jnp.where(lane == 1, d2, jnp.where(lane == 2, w1, jnp.where(lane == 3, w2, 0.0))))
    info_ref[0] = info
    infot_ref[0] = jnp.transpose(info)[0:SUBLANES]
    cnt_ref[0] = jnp.broadcast_to(cnt, (SUBLANES, LANES))


def _moe_route(xs, modtab, g_pre, router):
    B, S, _ = xs.shape
    per = S // MOE_TB
    nb = B * per
    blk = lambda c: pl.BlockSpec((1, MOE_TB, c), lambda i: (i // per, i % per, 0))
    flat = lambda r, c: pl.BlockSpec((1, r, c), lambda i: (i, 0, 0))
    return pl.pallas_call(
        _moe_route_kernel,
        out_shape=[jax.ShapeDtypeStruct((nb, MOE_TB, D), BF16), jax.ShapeDtypeStruct((nb, MOE_TB, LANES), F32),
                   jax.ShapeDtypeStruct((nb, SUBLANES, MOE_TB), F32),
                   jax.ShapeDtypeStruct((nb, SUBLANES, LANES), F32)],
        grid=(nb,),
        in_specs=[blk(D), pl.BlockSpec((1, 1, 6, D), lambda i: (i // per, 1, 0, 0)),
                  _const_spec((1, D)), _const_spec((2, D, LANES))],
        out_specs=[flat(MOE_TB, D), flat(MOE_TB, LANES), flat(SUBLANES, MOE_TB), flat(SUBLANES, LANES)],
        compiler_params=_cparams(("parallel",)),
        name="moe_route",
    )(xs, modtab, g_pre, router)


def _moe_plan(cnt, n_tiles):
    pc = (cnt + MOE_SEG - 1) // MOE_SEG * MOE_SEG
    inc = jnp.cumsum(pc, axis=1)
    loff = inc - pc
    reg = (jnp.sum(pc, axis=0) + MOE_TM - 1) // MOE_TM * MOE_TM
    gend = jnp.cumsum(reg)
    goff = (gend - reg)[None, :] + jnp.cumsum(pc, axis=0) - pc
    rows = jnp.arange(MOE_NP, dtype=jnp.int32) * MOE_SEG
    e_p = jnp.sum((rows[None, :, None] >= inc[:, None, :]).astype(jnp.int32), axis=-1)
    e_c = jnp.minimum(e_p, N_EXPERTS - 1)
    dst = jnp.take_along_axis(goff, e_c, axis=1) + rows[None, :] - jnp.take_along_axis(loff, e_c, axis=1)
    dst = jnp.where(e_p < N_EXPERTS, dst, 0).astype(jnp.int32)
    n_valid = (inc[:, -1] // MOE_SEG).astype(jnp.int32)
    trow = jnp.arange(n_tiles, dtype=jnp.int32) * MOE_TM
    te = jnp.sum((trow[:, None] >= gend[None, :]).astype(jnp.int32), axis=-1)
    valid = te < N_EXPERTS
    last = gend[-1] // MOE_TM - 1
    te = jnp.where(valid, te, te[last]).astype(jnp.int32)
    src = jnp.where(valid, jnp.arange(n_tiles, dtype=jnp.int32), last).astype(jnp.int32)
    fresh = valid & jnp.concatenate([jnp.ones((1,), bool), te[1:] != te[:-1]])
    return dst, n_valid, te, src, valid.astype(jnp.int32), fresh.astype(jnp.int32)


def _piece_copy(src_ref, src_row, dst_ref, dst_row, sem):
    return pltpu.make_async_copy(src_ref.at[pl.ds(src_row, MOE_SEG)], dst_ref.at[pl.ds(dst_row, MOE_SEG)], sem)


def _moe_gather_kernel(dst_ref, nv_ref, h_ref, infot_ref, xg_in_ref, xg_ref, buf_ref, sem_ref):
    del xg_in_ref
    i = pl.program_id(0)
    it = infot_ref[0]
    rr = lax.broadcasted_iota(jnp.int32, (MOE_R, MOE_TB), 0).astype(F32)
    onehot = jnp.where(rr == it[0:1], 1.0, jnp.where(rr == it[1:2], 1.0, 0.0)).astype(BF16)
    buf_ref[...] = jnp.dot(onehot, h_ref[0], preferred_element_type=F32).astype(BF16)
    nv = nv_ref[i]

    def piece(p):
        return _piece_copy(buf_ref, pl.multiple_of(p * MOE_SEG, MOE_SEG),
                           xg_ref, pl.multiple_of(dst_ref[i, p], MOE_SEG), sem_ref.at[p])

    def start(p, c):
        piece(p).start()
        return c

    def wait(p, c):
        piece(p).wait()
        return c

    lax.fori_loop(0, nv, start, 0)
    lax.fori_loop(0, nv, wait, 0)


def _moe_gather(dst, n_valid, h, info_t, n_rows):
    nb = h.shape[0]
    flat = lambda r, c: pl.BlockSpec((1, r, c), lambda i, *_: (i, 0, 0))
    return pl.pallas_call(
        _moe_gather_kernel,
        out_shape=jax.ShapeDtypeStruct((n_rows, D), BF16),
        grid_spec=pltpu.PrefetchScalarGridSpec(
            num_scalar_prefetch=2, grid=(nb,),
            in_specs=[flat(MOE_TB, D), flat(SUBLANES, MOE_TB), pl.BlockSpec(memory_space=pl.ANY)],
            out_specs=pl.BlockSpec(memory_space=pl.ANY),
            scratch_shapes=[pltpu.VMEM((MOE_R, D), BF16), pltpu.SemaphoreType.DMA((MOE_NP,))]),
        input_output_aliases={4: 0},
        compiler_params=_cparams(("arbitrary",)),
        name="moe_gather",
    )(dst, n_valid, h, info_t, jnp.zeros((n_rows, D), BF16))


N_FCH = D_FF // FCH


def _moe_ffn_kernel(te_ref, src_ref, valid_ref, fresh_ref, x_ref, wg_hbm, wu_hbm, wd_hbm, y_ref,
                    wg_ref, wu_ref, wd_ref, sem_ref):
    i = pl.program_id(0)
    e = te_ref[i]
    fresh = fresh_ref[i] == 1

    def slice_copies(c):
        cols = pl.ds(c * FCH, FCH)
        return (pltpu.make_async_copy(wg_hbm.at[e, :, cols], wg_ref.at[:, cols], sem_ref.at[0, c]),
                pltpu.make_async_copy(wu_hbm.at[e, :, cols], wu_ref.at[:, cols], sem_ref.at[1, c]),
                pltpu.make_async_copy(wd_hbm.at[e, cols, :], wd_ref.at[cols, :], sem_ref.at[2, c]))

    @pl.when(fresh)
    def _():
        for c in range(N_FCH):
            for cp in slice_copies(c):
                cp.start()

    @pl.when(valid_ref[i] == 1)
    def _():
        hb = x_ref[...]
        acc = jnp.zeros((MOE_TM, D), F32)
        for c in range(N_FCH):
            @pl.when(fresh)
            def _():
                for cp in slice_copies(c):
                    cp.wait()

            sl = slice(c * FCH, (c + 1) * FCH)
            gate = jnp.dot(hb, wg_ref[:, sl].astype(BF16), preferred_element_type=F32)
            up = jnp.dot(hb, wu_ref[:, sl].astype(BF16), preferred_element_type=F32)
            act = (gate * _sigmoid(gate) * up).astype(BF16)
            acc = acc + jnp.dot(act, wd_ref[sl, :].astype(BF16), preferred_element_type=F32)
        y_ref[...] = acc.astype(BF16)

    @pl.when(valid_ref[i] == 0)
    def _():
        y_ref[...] = jnp.zeros_like(y_ref)


def _moe_ffn(te, src, valid, fresh, xg, wg, wu, wd):
    n_rows = xg.shape[0]
    hbm = pl.BlockSpec(memory_space=pl.ANY)
    return pl.pallas_call(
        _moe_ffn_kernel,
        out_shape=jax.ShapeDtypeStruct((n_rows, D), BF16),
        grid_spec=pltpu.PrefetchScalarGridSpec(
            num_scalar_prefetch=4, grid=(n_rows // MOE_TM,),
            in_specs=[pl.BlockSpec((MOE_TM, D), lambda i, te, src, *_: (src[i], 0)), hbm, hbm, hbm],
            out_specs=pl.BlockSpec((MOE_TM, D), lambda i, *_: (i, 0)),
            scratch_shapes=[pltpu.VMEM((D, D_FF), F32), pltpu.VMEM((D, D_FF), F32), pltpu.VMEM((D_FF, D), F32),
                            pltpu.SemaphoreType.DMA((3, N_FCH))]),
        compiler_params=_cparams(("arbitrary",)),
        name="moe_ffn",
    )(te, src, valid, fresh, xg, wg, wu, wd)


def _moe_combine_kernel(dst_ref, nv_ref, x_ref, mod_ref, gpost_ref, info_ref, yg_ref, xo_ref, buf_ref, sem_ref):
    i = pl.program_id(0)
    nv = nv_ref[i]

    def piece(p):
        return _piece_copy(yg_ref, pl.multiple_of(dst_ref[i, p], MOE_SEG),
                           buf_ref, pl.multiple_of(p * MOE_SEG, MOE_SEG), sem_ref.at[p])

    def start(p, c):
        piece(p).start()
        return c

    def clear(p, c):
        buf_ref[pl.ds(pl.multiple_of(p * MOE_SEG, MOE_SEG), MOE_SEG), :] = jnp.zeros((MOE_SEG, D), BF16)
        return c

    def wait(p, c):
        piece(p).wait()
        return c

    lax.fori_loop(0, nv, start, 0)
    lax.fori_loop(nv, MOE_NP, clear, 0)
    info = info_ref[0]
    rr = lax.broadcasted_iota(jnp.int32, (MOE_TB, MOE_R), 1).astype(F32)
    comb = jnp.where(rr == info[:, 0:1], info[:, 2:3], jnp.where(rr == info[:, 1:2], info[:, 3:4], 0.0)).astype(BF16)
    lax.fori_loop(0, nv, wait, 0)
    fx = jnp.dot(comb, buf_ref[...], preferred_element_type=F32)
    xo_ref[0] = x_ref[0] + mod_ref[0, 0][5:6] * _rms(fx, gpost_ref[...])


def _moe_combine(dst, n_valid, xs, modtab, g_post, info, yg):
    B, S, _ = xs.shape
    per = S // MOE_TB
    blk = pl.BlockSpec((1, MOE_TB, D), lambda i, *_: (i // per, i % per, 0))
    return pl.pallas_call(
        _moe_combine_kernel,
        out_shape=jax.ShapeDtypeStruct((B, S, D), F32),
        grid_spec=pltpu.PrefetchScalarGridSpec(
            num_scalar_prefetch=2, grid=(B * per,),
            in_specs=[blk, pl.BlockSpec((1, 1, 6, D), lambda i, *_: (i // per, 1, 0, 0)),
                      pl.BlockSpec((1, D), lambda i, *_: (0, 0)),
                      pl.BlockSpec((1, MOE_TB, LANES), lambda i, *_: (i, 0, 0)),
                      pl.BlockSpec(memory_space=pl.ANY)],
            out_specs=blk,
            scratch_shapes=[pltpu.VMEM((MOE_R, D), BF16), pltpu.SemaphoreType.DMA((MOE_NP,))]),
        compiler_params=_cparams(("arbitrary",)),
        name="moe_combine",
    )(dst, n_valid, xs, modtab, g_post, info, yg)


def _moe(xs, modtab, g_pre, g_post, router, wg, wu, wd):
    B, S, _ = xs.shape
    nb = B * S // MOE_TB
    h, info, info_t, cnt = _moe_route(xs, modtab, g_pre, router)
    worst = 2 * B * S + nb * N_EXPERTS * (MOE_SEG - 1) + N_EXPERTS * (MOE_TM - 1)
    n_tiles = -(-worst // MOE_TM)
    dst, n_valid, te, src, valid, fresh = _moe_plan(cnt[:, 0, :N_EXPERTS].astype(jnp.int32), n_tiles)
    xg = _moe_gather(dst, n_valid, h, info_t, n_tiles * MOE_TM)
    yg = _moe_ffn(te, src, valid, fresh, xg, wg, wu, wd)
    return _moe_combine(dst, n_valid, xs, modtab, g_post, info, yg)


def _block_diag2(w):
    z = jnp.zeros_like(w[0])
    return jnp.concatenate([jnp.concatenate([w[0], z], axis=1), jnp.concatenate([z, w[1]], axis=1)], axis=0)


def _row(v):
    return v.reshape(1, -1).astype(F32)


def _head_ones(width, head):
    i = jnp.arange(width) // head
    return (i[:, None] == i[None, :]).astype(BF16)


def kernel(x, c, ctx, c_ctx, ada_w, ada_b, norm_mix_pre, norm_mix_post, norm_ffn_pre, norm_ffn_post, w_in, shift_mu, rw_w_up, rw_w0, rw_a_up, rw_a0, rw_k_k, rw_k_a, rw_r_k, rw_g_up, rw_gn_w, rw_gn_b, rw_v_down, rw_v_up, rw_v0, gla_conv, gla_a_up, gla_a_b, gla_gn_w, w_out, ffn_w_gate, ffn_w_up, ffn_w_down, moe_router, moe_w_gate, moe_w_up, moe_w_down):
    B, S, _ = x.shape
    n_ctx = ctx.shape[1]
    depth = w_in.shape[0]
    assert n_ctx == TB and S % MOE_TB == 0 and (n_ctx + S) % TW == 0 and depth == 2

    xc = None
    pad_rows = -(B + 1) % SUBLANES
    cvec = jnp.concatenate([c, c_ctx[None, :], jnp.zeros((pad_rows, D), F32)], axis=0)
    bd64 = _head_ones(RW_WIDTH, RW_HEAD)
    ada_b3 = ada_b.reshape(depth, 1, 6 * D)
    v_first = None
    out = None
    for i in range(depth):
        last = i == depth - 1
        mods = _adaln(cvec, ada_w, ada_b3, i)
        mod_x = mods[:B].reshape(B, 6, D)
        mod_c = jnp.broadcast_to(mods[B].reshape(1, 6, D), (B, 6, D))
        modtab = jnp.stack([mod_c, mod_x], axis=1)

        w_i = jnp.concatenate([w_in[i], jnp.zeros((D, GLA_PAD - GLA_COLS), F32)], axis=1).astype(BF16)
        p_rw, p_gl = _inproj(ctx, x, xc, _row(norm_mix_pre[i]), modtab, w_i)

        prm = dict(
            mu=_row(shift_mu[i]),
            w_up=_block_diag2(rw_w_up[i]).astype(BF16), w0=_row(rw_w0[i]),
            a_up=_block_diag2(rw_a_up[i]).astype(BF16), a0=_row(rw_a0[i]),
            k_k=_row(rw_k_k[i]), k_a=_row(rw_k_a[i]), r_k=_row(rw_r_k[i]),
            g_up=rw_g_up[i].astype(BF16), bd64=bd64,
            gn_w=_row(rw_gn_w[i]), gn_b=_row(rw_gn_b[i]), gla_gn_w=_row(gla_gn_w[i]),
            w_out=w_out[i].astype(BF16), norm_post=_row(norm_mix_post[i]),
        )
        gate_pad = jnp.zeros((LANES - 2 * GLA_GATE_RANK, 2 * GLA_KW), F32)
        gla_prm = dict(conv=gla_conv[i].astype(F32), a_b=_row(gla_a_b[i]),
                       a_up=jnp.concatenate([_block_diag2(gla_a_up[i]), gate_pad], axis=0).astype(BF16))
        if i > 0:
            pad = LANES - RW_V_RANK
            prm["v_down"] = jnp.concatenate([rw_v_down[i - 1], jnp.zeros((RW_WIDTH, pad), F32)], axis=1).astype(BF16)
            prm["v_up"] = jnp.concatenate([rw_v_up[i - 1], jnp.zeros((pad, RW_WIDTH), F32)], axis=0).astype(BF16)
            prm["v0"] = _row(rw_v0[i - 1])

        r, kk, vm, g, bonus, lw, bb, ke = _rwprep(p_rw, prm, v_first if i > 0 else None)
        if i == 0:
            v_first = vm
        q, k, gv, og, lg = _glaprep(p_gl, gla_prm)
        y, o = _scans(r, kk, vm, lw, bb, ke, q, k, gv, lg)
        xc = _readout(y, bonus, g, o, og, ctx, x, xc, modtab, prm, latents_only=last)

        jf = i // 2
        if i % 2 == 0:
            xc = _ffn(xc, modtab, _row(norm_ffn_pre[i]), _row(norm_ffn_post[i]),
                      ffn_w_gate[jf].astype(BF16), ffn_w_up[jf].astype(BF16), ffn_w_down[jf].astype(BF16))
        else:
            router = jnp.concatenate([moe_router[jf], jnp.zeros((D, LANES - N_EXPERTS), F32)], axis=1)
            r_hi = router.astype(BF16)
            router = jnp.stack([r_hi, (router - r_hi.astype(F32)).astype(BF16)])
            out = _moe(xc, modtab, _row(norm_ffn_pre[i]), _row(norm_ffn_post[i]), router,
                       moe_w_gate[jf], moe_w_up[jf], moe_w_down[jf])
    return out
```

```python
import functools
import math

import jax
import jax.numpy as jnp
from jax import lax
from jax.experimental import pallas as pl
from jax.experimental.pallas import tpu as pltpu

F32, BF16 = jnp.float32, jnp.bfloat16
ACT = BF16

D = 1024
GRID_W = 64
RW_WIDTH = 512
RW_HEAD = 64
RW_RANK = 64
RW_G_RANK = 128
RW_V_RANK = 32
RW_GN_EPS = 64e-5
GLA_WIDTH = 512
GLA_HEADS = 4
GLA_DV = 128
GLA_DK = 64
GLA_KW = 256
GLA_GATE_RANK = 16
GLA_GATE_NORM = 16.0
GLA_NORM_EPS = 1e-5
D_FF = 2816
N_EXPERTS = 8
NORM_EPS = 1e-6
RW_COLS = 3 * RW_WIDTH + 4 * RW_RANK + RW_G_RANK
GLA_QKV = 2 * GLA_KW + GLA_WIDTH
GLA_COLS = GLA_QKV + GLA_WIDTH + 2 * GLA_GATE_RANK
GLA_PAD = 1664

LANES = 128
SUBLANES = 8
TB = 256
TW = 3 * TB
CH = 64
FCH = 256
HALO = 16
VMEM_LIMIT = 56 * 1024 * 1024


def _cparams(sem):
    return pltpu.CompilerParams(dimension_semantics=sem, vmem_limit_bytes=VMEM_LIMIT)


def _bdot(a, b):
    return jnp.dot(a.astype(BF16), b.astype(BF16), preferred_element_type=F32)


def _bdot_nt(a, b):
    return lax.dot_general(a.astype(BF16), b.astype(BF16), (((1,), (1,)), ((), ())),
                           preferred_element_type=F32)


def _bdot_tn(a, b):
    return lax.dot_general(a.astype(BF16), b.astype(BF16), (((0,), (0,)), ((), ())),
                           preferred_element_type=F32)


def _split_dot(a_exact, x):
    h1 = x.astype(BF16)
    r1 = x - h1.astype(F32)
    h2 = r1.astype(BF16)
    h3 = (r1 - h2.astype(F32)).astype(BF16)
    return (jnp.dot(a_exact, h1, preferred_element_type=F32)
            + jnp.dot(a_exact, h2, preferred_element_type=F32)
            + jnp.dot(a_exact, h3, preferred_element_type=F32))


def _seg_sum(x, bd, exact=True):
    hi = x.astype(BF16)
    out = jnp.dot(hi, bd, preferred_element_type=F32)
    if exact:
        lo = (x - hi.astype(F32)).astype(BF16)
        out = out + jnp.dot(lo, bd, preferred_element_type=F32)
    return out


def _sigmoid(x):
    return jax.nn.sigmoid(x)


def _rms(x, g):
    return x * lax.rsqrt(jnp.mean(x * x, axis=-1, keepdims=True) + NORM_EPS) * g


def _pair_stack(z, lo):
    return jnp.concatenate([jnp.where(lo, z, 0.0), jnp.where(lo, 0.0, z)], axis=0)


def _adaln_kernel(c_ref, w_ref, b_ref, o_ref):
    c = c_ref[...]
    s = c * _sigmoid(c)
    o_ref[...] = jnp.dot(s, w_ref[0], precision=lax.Precision.HIGHEST,
                         preferred_element_type=F32) + b_ref[0]


def _adaln(cvec, w, b, layer):
    rows = cvec.shape[0]
    n = w.shape[2]
    return pl.pallas_call(
        _adaln_kernel,
        out_shape=jax.ShapeDtypeStruct((rows, n), F32),
        grid=(n // D,),
        in_specs=[pl.BlockSpec((rows, D), lambda i: (0, 0)),
                  pl.BlockSpec((1, D, D), lambda i: (layer, 0, i)),
                  pl.BlockSpec((1, 1, D), lambda i: (layer, 0, i))],
        out_specs=pl.BlockSpec((rows, D), lambda i: (0, i)),
        compiler_params=_cparams(("arbitrary",)),
        name="adaln",
    )(cvec, w, b)


def _wide_mod(mod_ref, k):
    is_ctx = jnp.logical_and(pl.program_id(1) == 0, lax.broadcasted_iota(jnp.int32, (TW, 1), 0) < TB)
    return jnp.where(is_ctx, mod_ref[0, 0, k:k + 1], mod_ref[0, 1, k:k + 1])


def _wide_rows(refs):
    if len(refs) == 1:
        return refs[0][0]
    first = jnp.where(pl.program_id(1) == 0, refs[0][0], refs[1][0])
    return jnp.concatenate([first] + [r[0] for r in refs[2:]], axis=0)


def _inproj_kernel(n_src, *refs):
    g_ref, mod_ref, w_ref, prw_ref, pgl_ref = refs[n_src:]
    x = _wide_rows(refs[:n_src])
    h = _rms(x, g_ref[...]) * (1.0 + _wide_mod(mod_ref, 1)) + _wide_mod(mod_ref, 0)
    hb = h.astype(BF16)
    prw_ref[0] = jnp.dot(hb, w_ref[:, :RW_COLS], preferred_element_type=F32).astype(ACT)
    pgl_ref[0] = jnp.dot(hb, w_ref[:, RW_COLS:], preferred_element_type=F32).astype(ACT)


def _wide_src(ctx, x, xc):
    if xc is not None:
        return [xc], [pl.BlockSpec((1, TW, D), lambda b, j: (b, j, 0))]
    per = TW // TB
    specs = [pl.BlockSpec((1, TB, D), lambda b, j: (b, 0, 0))]
    for k in range(per):
        specs.append(pl.BlockSpec((1, TB, D), lambda b, j, k=k: (b, jnp.maximum(j * per + k - 1, 0), 0)))
    return [ctx] + [x] * per, specs


def _wide_mod_spec():
    return pl.BlockSpec((1, 2, 6, D), lambda b, j: (b, 0, 0, 0))


def _const_spec(shape):
    nd = len(shape)
    return pl.BlockSpec(shape, lambda *_: (0,) * nd)


def _tile_spec(c):
    return pl.BlockSpec((1, TB, c), lambda b, j: (b, j, 0))


def _dir_tile_spec(c):
    return pl.BlockSpec((2, 1, TB, c), lambda b, j: (0, b, j, 0))


def _inproj(ctx, x, xc, g, modtab, w):
    B = modtab.shape[0]
    N = xc.shape[1] if xc is not None else ctx.shape[1] + x.shape[1]
    srcs, src_specs = _wide_src(ctx, x, xc)
    wide = lambda c: pl.BlockSpec((1, TW, c), lambda b, j: (b, j, 0))
    return pl.pallas_call(
        functools.partial(_inproj_kernel, len(srcs)),
        out_shape=[jax.ShapeDtypeStruct((B, N, RW_COLS), ACT),
                   jax.ShapeDtypeStruct((B, N, GLA_PAD), ACT)],
        grid=(B, N // TW),
        in_specs=src_specs + [_const_spec((1, D)), _wide_mod_spec(), _const_spec((D, RW_COLS + GLA_PAD))],
        out_specs=[wide(RW_COLS), wide(GLA_PAD)],
        compiler_params=_cparams(("parallel", "parallel")),
        name="inproj",
    )(*srcs, g, modtab, w)


def _shift_table():
    t = jnp.arange(TB)
    prev = t[:, None] - 1 == t[None, :]
    nxt = t[:, None] + 1 == t[None, :]
    col = (t % GRID_W)[:, None]
    return jnp.stack([jnp.stack([prev, nxt]),
                      jnp.stack([prev & (col != 0), nxt & (col != GRID_W - 1)])]).astype(BF16)


def _rwprep_kernel(has_vres, n_tiles, p_ref, hu_ref, hd_ref, sh_ref, mu_ref, wup_ref, w0_ref, aup_ref,
                   a0_ref, kk_ref, ka_ref, rk_ref, gup_ref, bd_ref, *rest):
    if has_vres:
        vf_ref, vdn_ref, vup_ref, v0_ref = rest[:4]
        rest = rest[4:]
    r_out, kk_out, v_out, g_out, bon_out, lw_out, b_out, ke_out = rest
    j = pl.program_id(1)
    pb = p_ref[0]
    p = pb.astype(F32)
    prev1 = jnp.dot(sh_ref[0, 0], pb, preferred_element_type=F32)
    next1 = jnp.dot(sh_ref[0, 1], pb, preferred_element_type=F32)
    has_upper = jnp.where(j == 1, 0.0, 1.0)
    has_lower = jnp.where(j == n_tiles - 1, 0.0, 1.0)
    up = jnp.concatenate([hu_ref[0].astype(F32) * has_upper, p[:TB - GRID_W]], axis=0)
    down = jnp.concatenate([p[GRID_W:], hd_ref[0].astype(F32) * has_lower], axis=0)
    cls = lax.broadcasted_iota(jnp.int32, p.shape, 1) & jnp.where(j == 0, 1, 3)
    shifted = jnp.where(cls == 0, prev1, jnp.where(cls == 1, next1, jnp.where(cls == 2, up, down)))
    u = p + mu_ref[...] * (shifted - p)

    r = u[:, 0:RW_WIDTH]
    k = u[:, RW_WIDTH:2 * RW_WIDTH]
    v = u[:, 2 * RW_WIDTH:3 * RW_WIDTH]
    o = 3 * RW_WIDTH
    wd = u[:, o:o + 2 * RW_RANK]
    ad = u[:, o + 2 * RW_RANK:o + 4 * RW_RANK]
    gd = u[:, o + 4 * RW_RANK:]

    w_logit = w0_ref[...] + _bdot(jnp.tanh(wd), wup_ref[...])
    lw = -math.exp(-0.5) * _sigmoid(w_logit)
    a = _sigmoid(a0_ref[...] + _bdot(ad, aup_ref[...]))
    bd = bd_ref[...]
    kk = k * kk_ref[...]
    kk = kk * lax.rsqrt(jnp.maximum(_seg_sum(kk * kk, bd), 1e-24))
    g = _bdot(_sigmoid(gd), gup_ref[...])
    if has_vres:
        gate = _sigmoid(v0_ref[...] + _bdot(_bdot(v, vdn_ref[...]), vup_ref[...]))
        vm = v + (vf_ref[0].astype(F32) - v) * gate
    else:
        vm = v
    ke_sum = jnp.zeros_like(k)
    for d in range(2):
        a_d = a[:, d * RW_WIDTH:(d + 1) * RW_WIDTH]
        ke_d = k * (1.0 + (a_d - 1.0) * ka_ref[...])
        lw_out[d, 0] = lw[:, d * RW_WIDTH:(d + 1) * RW_WIDTH]
        b_out[d, 0] = (kk * a_d).astype(ACT)
        ke_out[d, 0] = ke_d.astype(ACT)
        ke_sum = ke_sum + ke_d
    r_out[0] = r.astype(ACT)
    kk_out[0] = kk.astype(ACT)
    v_out[0] = vm.astype(ACT)
    g_out[0] = g.astype(ACT)
    bon_out[0] = (_seg_sum(r * ke_sum * rk_ref[...], bd, exact=False) * vm).astype(ACT)


def _rwprep(p_rw, prm, v_first):
    B, N, _ = p_rw.shape
    nt = N // TB
    hb = TB // GRID_W
    nhb = N // GRID_W
    has_vres = v_first is not None
    W2 = 2 * RW_WIDTH
    in_specs = [
        _tile_spec(RW_COLS),
        pl.BlockSpec((1, GRID_W, RW_COLS), lambda b, j: (b, jnp.maximum(j * hb - 1, 0), 0)),
        pl.BlockSpec((1, GRID_W, RW_COLS), lambda b, j: (b, jnp.minimum(j * hb + hb, nhb - 1), 0)),
        pl.BlockSpec((1, 2, TB, TB), lambda b, j: (jnp.minimum(j, 1), 0, 0, 0)),
        _const_spec((1, RW_COLS)), _const_spec((2 * RW_RANK, W2)), _const_spec((1, W2)),
        _const_spec((2 * RW_RANK, W2)), _const_spec((1, W2)),
        _const_spec((1, RW_WIDTH)), _const_spec((1, RW_WIDTH)), _const_spec((1, RW_WIDTH)),
        _const_spec((RW_G_RANK, RW_WIDTH)), _const_spec((RW_WIDTH, RW_WIDTH)),
    ]
    args = [p_rw, p_rw, p_rw, _shift_table(), prm["mu"], prm["w_up"], prm["w0"], prm["a_up"], prm["a0"],
            prm["k_k"], prm["k_a"], prm["r_k"], prm["g_up"], prm["bd64"]]
    if has_vres:
        in_specs += [_tile_spec(RW_WIDTH), _const_spec((RW_WIDTH, LANES)), _const_spec((LANES, RW_WIDTH)),
                     _const_spec((1, RW_WIDTH))]
        args += [v_first, prm["v_down"], prm["v_up"], prm["v0"]]
    tok = jax.ShapeDtypeStruct((B, N, RW_WIDTH), ACT)
    dtok = jax.ShapeDtypeStruct((2, B, N, RW_WIDTH), ACT)
    return pl.pallas_call(
        functools.partial(_rwprep_kernel, has_vres, nt),
        out_shape=[tok] * 5 + [jax.ShapeDtypeStruct((2, B, N, RW_WIDTH), F32), dtok, dtok],
        grid=(B, nt),
        in_specs=in_specs,
        out_specs=[_tile_spec(RW_WIDTH)] * 5 + [_dir_tile_spec(RW_WIDTH)] * 3,
        compiler_params=_cparams(("parallel", "parallel")),
        name="rwprep",
    )(*args)


NCH = TB // CH


def _scan_kernel(rf_ref, kkf_ref, vf_ref, lwf_ref, bf_ref, kef_ref,
                 rb_ref, kkb_ref, vb_ref, lwb_ref, bb_ref, keb_ref,
                 gqf_ref, gkf_ref, gvf_ref, lgf_ref, gqb_ref, gkb_ref, gvb_ref, lgb_ref,
                 yf_ref, yb_ref, of_ref, ob_ref, st_ref, gst_ref):
    j = pl.program_id(1)

    @pl.when(j == 0)
    def _():
        st_ref[...] = jnp.zeros_like(st_ref)
        gst_ref[...] = jnp.zeros_like(gst_ref)

    ti = lax.broadcasted_iota(jnp.int32, (CH, LANES), 0)
    li = lax.broadcasted_iota(jnp.int32, (CH, LANES), 1)
    si = li & (CH - 1)
    lo = li < CH
    eye = jnp.where(ti == si, 1.0, 0.0)
    rblk = lax.broadcasted_iota(jnp.int32, (LANES, LANES), 0) // CH
    cblk = lax.broadcasted_iota(jnp.int32, (LANES, LANES), 1) // CH
    bdmask = rblk == cblk
    t64 = lax.broadcasted_iota(jnp.int32, (CH, CH), 0)
    s64 = lax.broadcasted_iota(jnp.int32, (CH, CH), 1)
    tri = [jnp.where(s64 <= t64, 1.0, 0.0).astype(BF16), jnp.where(s64 >= t64, 1.0, 0.0).astype(BF16)]
    m_incl = [si <= ti, si >= ti]
    m_strict = [si < ti, si > ti]
    n_pairs = RW_WIDTH // LANES
    in_refs = ((rf_ref, kkf_ref, vf_ref, lwf_ref, bf_ref, kef_ref),
               (rb_ref, kkb_ref, vb_ref, lwb_ref, bb_ref, keb_ref))
    y_refs = (yf_ref, yb_ref)

    def stack(z):
        return _pair_stack(z, lo).astype(BF16)

    st = {(d, p): st_ref[d, p] for d in range(2) for p in range(n_pairs)}

    def scan_step(step):
        cur = {}
        w_tot = {}
        for d in range(2):
            c = (NCH - 1 - step) if d == 1 else step
            rows = slice(c * CH, (c + 1) * CH)
            refs = in_refs[d]
            r, kk, v = (refs[i][0, rows, :].astype(F32) for i in range(3))
            lw = refs[3][0, 0, rows, :]
            b, ke = refs[4][0, 0, rows, :].astype(F32), refs[5][0, 0, rows, :].astype(F32)
            cum = _split_dot(tri[d], lw)
            tot = jnp.sum(lw, axis=0, keepdims=True)
            w_inv = jnp.exp(-cum)
            w_end = jnp.exp(tot - cum)
            w_tot[d] = jnp.exp(tot)
            rh = r * jnp.exp(cum)
            ah = -(kk * jnp.exp(cum - lw))
            bh, kh, bt, kt = b * w_inv, ke * w_inv, b * w_end, ke * w_end
            for p in range(n_pairs):
                sl = slice(LANES * p, LANES * (p + 1))
                cur[d, p] = dict(ah=ah[:, sl], rh=rh[:, sl], bh=bh[:, sl], kh=kh[:, sl], bt=bt[:, sl],
                                 kt=kt[:, sl], v=v[:, sl], rows=rows, sl=sl)
        yield
        for it in cur.values():
            it["lhs"] = jnp.concatenate([it["ah"], it["rh"]], axis=0).astype(BF16)
            rhs = jnp.concatenate([stack(it["bh"]), stack(it["kh"])], axis=0)
            it["a_all"] = _bdot_nt(it["lhs"], rhs)
        yield
        for (d, _), it in cur.items():
            a_all = it.pop("a_all")
            it["a_ab"] = jnp.where(m_strict[d], a_all[:CH, :LANES], 0.0)
            a_ak = jnp.where(m_strict[d], a_all[:CH, LANES:], 0.0)
            a_rb = jnp.where(m_incl[d], a_all[CH:, :LANES], 0.0)
            a_rk = jnp.where(m_incl[d], a_all[CH:, LANES:], 0.0)
            it["a_r"] = jnp.concatenate([a_rb, a_rk], axis=1).astype(BF16)
            it["v_bd"] = stack(it["v"])
            it["akv"] = _bdot(a_ak, it["v_bd"])
            it["t"] = eye + it["a_ab"]
            it["m"] = _bdot(it["a_ab"], stack(it["a_ab"]))
        yield
        for _ in range(int(math.log2(CH)) - 2):
            for it in cur.values():
                z = _bdot(jnp.concatenate([it["m"], it["t"]], axis=0), stack(it["m"]))
                it["m"] = z[:CH]
                it["t"] = it["t"] + z[CH:]
            yield
        for it in cur.values():
            it["t"] = (it["t"] + _bdot(it["t"], stack(it["m"]))).astype(BF16)
        yield
        for it in cur.values():
            a_til = jnp.dot(it["t"], stack(it["ah"]), preferred_element_type=F32)
            it["lhs2"] = jnp.concatenate([a_til.astype(BF16), it["lhs"][CH:]], axis=0)
            it["cc"] = jnp.dot(it["t"], stack(it["akv"]), preferred_element_type=F32)
            it["rhs_t"] = jnp.concatenate([it["bt"], it["kt"]], axis=0).astype(BF16)
        yield
        z1 = {k: _bdot_nt(it["lhs2"], st[k]) for k, it in cur.items()}
        yield
        u = {k: z1[k][:CH] + it["cc"] for k, it in cur.items()}
        upd = {k: _bdot_tn(jnp.concatenate([u[k], it["v"]], axis=0), it["rhs_t"]) for k, it in cur.items()}
        yield
        for (d, p), it in cur.items():
            y = z1[d, p][CH:] + jnp.dot(it["a_r"], jnp.concatenate([stack(u[d, p]), it["v_bd"]], axis=0),
                                        preferred_element_type=F32)
            y_refs[d][0, it["rows"], it["sl"]] = y.astype(ACT)
            st[d, p] = st[d, p] * w_tot[d][:, it["sl"]] + jnp.where(bdmask, upd[d, p], 0.0)
        yield

    gla_refs = ((gqf_ref, gkf_ref, gvf_ref, lgf_ref), (gqb_ref, gkb_ref, gvb_ref, lgb_ref))
    o_refs = (of_ref, ob_ref)
    g_pairs = GLA_KW // LANES
    gst = {(d, p): gst_ref[d, p] for d in range(2) for p in range(g_pairs)}
    lo2 = lax.broadcasted_iota(jnp.int32, (LANES, LANES), 1) < CH
    zeros_v = jnp.zeros((CH, GLA_DV), F32)

    def gla_step(step):
        cur = {}
        dec = {}
        for d in range(2):
            c = (NCH - 1 - step) if d == 1 else step
            rows = slice(c * CH, (c + 1) * CH)
            refs = gla_refs[d]
            q, k, v = (refs[i][0, rows, :].astype(F32) for i in range(3))
            lg = refs[3][0, 0, rows, :]
            cum = _split_dot(tri[d], lg)
            tot = jnp.sum(lg, axis=0, keepdims=True)
            dec[d] = jnp.exp(tot)
            qd = q * jnp.exp(cum)
            ki = k * jnp.exp(-cum)
            kend = k * jnp.exp(tot - cum)
            for p in range(g_pairs):
                sl = slice(LANES * p, LANES * (p + 1))
                cur[d, p] = dict(qd=qd[:, sl].astype(BF16), ki=ki[:, sl], kend=kend[:, sl], rows=rows, sl=sl,
                                 v0=v[:, 2 * LANES * p:2 * LANES * p + LANES],
                                 v1=v[:, 2 * LANES * p + LANES:2 * LANES * (p + 1)])
        yield
        for it in cur.values():
            it["att"] = _bdot_nt(it["qd"], _pair_stack(it["ki"], lo))
        yield
        for (d, _), it in cur.items():
            att = jnp.where(m_incl[d], it["att"], 0.0)
            v_bd = jnp.concatenate([jnp.concatenate([it["v0"], zeros_v], axis=1),
                                    jnp.concatenate([zeros_v, it["v1"]], axis=1)], axis=0)
            it["o"] = _bdot(att, v_bd)
            it["upd"] = _bdot_tn(jnp.concatenate([it["v0"], it["v1"]], axis=0), _pair_stack(it["kend"], lo))
        yield
        for (d, p), it in cur.items():
            s = gst[d, p]
            s_bd = jnp.concatenate([jnp.where(lo2, s, 0.0), jnp.where(lo2, 0.0, s)], axis=0)
            o = it["o"] + _bdot_nt(it["qd"], s_bd)
            o_refs[d][0, it["rows"], 2 * LANES * p:2 * LANES * (p + 1)] = o.astype(ACT)
            gst[d, p] = s * dec[d][:, it["sl"]] + it["upd"]
        yield

    n_stages = 12
    n_dep = 3
    gla_stages = (1, 4, 7, 10)
    pipeline = [(scan_step(s), gla_step(s)) for s in range(NCH)]
    for slot in range(n_stages + n_dep * (NCH - 1)):
        for s, (rw_gen, gla_gen) in enumerate(pipeline):
            stage = slot - n_dep * s
            if 0 <= stage < n_stages:
                next(rw_gen)
                if stage in gla_stages:
                    next(gla_gen)
    for (d, p), s in st.items():
        st_ref[d, p] = s
    for (d, p), s in gst.items():
        gst_ref[d, p] = s


def _bidir_specs(c, n_tiles):
    def back(j):
        return jnp.where(j == 0, 0, n_tiles - j)

    tok_f = pl.BlockSpec((1, TB, c), lambda b, j: (b, j, 0))
    tok_b = pl.BlockSpec((1, TB, c), lambda b, j: (b, back(j), 0))
    dir_f = pl.BlockSpec((1, 1, TB, c), lambda b, j: (0, b, j, 0))
    dir_b = pl.BlockSpec((1, 1, TB, c), lambda b, j: (1, b, back(j), 0))
    return tok_f, tok_b, dir_f, dir_b


def _scans(r, kk, v, lw, b, ke, q, k, gv, lg):
    B, N, _ = r.shape
    nt = N // TB
    tok_f, tok_b, dir_f, dir_b = _bidir_specs(RW_WIDTH, nt)
    kf, kb, dkf, dkb = _bidir_specs(GLA_KW, nt)
    out = jax.ShapeDtypeStruct((B, N, RW_WIDTH), ACT)
    assert GLA_WIDTH == RW_WIDTH
    yf, yb, of, ob = pl.pallas_call(
        _scan_kernel,
        out_shape=[out] * 4,
        grid=(B, nt),
        in_specs=[tok_f, tok_f, tok_f, dir_f, dir_f, dir_f, tok_b, tok_b, tok_b, dir_b, dir_b, dir_b,
                  kf, kf, tok_f, dkf, kb, kb, tok_b, dkb],
        out_specs=[tok_f, tok_b, tok_f, tok_b],
        scratch_shapes=[pltpu.VMEM((2, RW_WIDTH // LANES, LANES, LANES), F32),
                        pltpu.VMEM((2, GLA_KW // LANES, GLA_DV, LANES), F32)],
        compiler_params=_cparams(("parallel", "arbitrary")),
        name="scans",
    )(r, kk, v, lw, b, ke, r, kk, v, lw, b, ke, q, k, gv, lg, q, k, gv, lg)
    return (yf, yb), (of, ob)


CONV_K = TB + LANES


def _conv_shift_table():
    t = jnp.arange(TB)[:, None]
    s = jnp.arange(CONV_K)[None, :]
    prev = jnp.where(t == 0, s == TB + 2 * HALO - 1, s == t - 1)
    nxt = jnp.where(t == TB - 1, s == TB, s == t + 1)
    return jnp.stack([prev, nxt]).astype(BF16)


def _glaprep_kernel(n_tiles, p_ref, hp_ref, hn_ref, sh_ref, cw_ref, aup_ref, ab_ref, q_out, k_out, v_out,
                    og_out, lg_out):
    j = pl.program_id(1)
    ub = p_ref[0][:, :GLA_QKV]
    u = ub.astype(F32)
    has_prev = jnp.where(j <= 1, 0.0, 1.0)
    has_next = jnp.where(jnp.logical_or(j == 0, j == n_tiles - 1), 0.0, 1.0)
    ext = jnp.concatenate([ub,
                           (hn_ref[0][:, :GLA_QKV].astype(F32) * has_next).astype(BF16),
                           (hp_ref[0][:, :GLA_QKV].astype(F32) * has_prev).astype(BF16),
                           jnp.zeros((CONV_K - TB - 2 * HALO, GLA_QKV), BF16)], axis=0)
    prev1 = jnp.dot(sh_ref[0], ext, preferred_element_type=F32)
    next1 = jnp.dot(sh_ref[1], ext, preferred_element_type=F32)
    cw = cw_ref[...]
    conv = cw[0:1] * prev1 + cw[1:2] * u + cw[2:3] * next1
    qkv = conv * _sigmoid(conv)
    q_out[0] = (qkv[:, :GLA_KW] * (GLA_DK ** -0.5)).astype(ACT)
    k_out[0] = qkv[:, GLA_KW:2 * GLA_KW].astype(ACT)
    v_out[0] = qkv[:, 2 * GLA_KW:].astype(ACT)
    og_out[0] = p_ref[0][:, GLA_QKV:GLA_QKV + GLA_WIDTH]
    z = _bdot(p_ref[0][:, GLA_QKV + GLA_WIDTH:], aup_ref[...]) + ab_ref[...]
    lg = (jnp.minimum(z, 0.0) - jnp.log1p(jnp.exp(-jnp.abs(z)))) * (1.0 / GLA_GATE_NORM)
    for d in range(2):
        lg_out[d, 0] = lg[:, d * GLA_KW:(d + 1) * GLA_KW]


def _glaprep(p_gl, prm):
    B, N, _ = p_gl.shape
    nt = N // TB
    sub = HALO
    hb = TB // sub
    nhb = N // sub
    return pl.pallas_call(
        functools.partial(_glaprep_kernel, nt),
        out_shape=[jax.ShapeDtypeStruct((B, N, GLA_KW), ACT), jax.ShapeDtypeStruct((B, N, GLA_KW), ACT),
                   jax.ShapeDtypeStruct((B, N, GLA_WIDTH), ACT), jax.ShapeDtypeStruct((B, N, GLA_WIDTH), ACT),
                   jax.ShapeDtypeStruct((2, B, N, GLA_KW), F32)],
        grid=(B, nt),
        in_specs=[_tile_spec(GLA_PAD),
                  pl.BlockSpec((1, sub, GLA_PAD), lambda b, j: (b, jnp.maximum(j * hb - 1, 0), 0)),
                  pl.BlockSpec((1, sub, GLA_PAD), lambda b, j: (b, jnp.minimum(j * hb + hb, nhb - 1), 0)),
                  _const_spec((2, TB, CONV_K)),
                  _const_spec((3, GLA_QKV)), _const_spec((LANES, 2 * GLA_KW)), _const_spec((1, 2 * GLA_KW))],
        out_specs=[_tile_spec(GLA_KW), _tile_spec(GLA_KW), _tile_spec(GLA_WIDTH), _tile_spec(GLA_WIDTH),
                   _dir_tile_spec(GLA_KW)],
        compiler_params=_cparams(("parallel", "parallel")),
        name="glaprep",
    )(p_gl, p_gl, p_gl, _conv_shift_table(), prm["conv"], prm["a_up"], prm["a_b"])


def _readout_kernel(group, first, split_src, *refs):
    tok = [refs[i * group:(i + 1) * group] for i in range(7)]
    refs = refs[7 * group:]
    n_res = group + 1 if split_src else group
    res_refs, refs = refs[:n_res], refs[n_res:]
    mod_ref, gnw_ref, gnb_ref, ggn_ref, wout_ref, gpost_ref, bd_ref, xo_ref = refs
    bd = bd_ref[...]
    starts_with_ctx = jnp.logical_and(first == 0, pl.program_id(1) == 0)
    for k in range(group):
        yf_ref, yb_ref, bon_ref, g_ref, of_ref, ob_ref, og_ref = (t[k] for t in tok)
        gate = mod_ref[0, 1, 2:3]
        if split_src:
            x_res = res_refs[k + 1][0]
            if k == 0:
                x_res = jnp.where(starts_with_ctx, res_refs[0][0], x_res)
        else:
            x_res = res_refs[k][0]
        if k == 0 and first == 0:
            gate = jnp.where(starts_with_ctx, mod_ref[0, 0, 2:3], gate)
        y = yf_ref[0].astype(F32) + yb_ref[0].astype(F32)
        mu = _seg_sum(y, bd, exact=False) * (1.0 / RW_HEAD)
        yc = y - mu
        var = _seg_sum(yc * yc, bd, exact=False) * (1.0 / RW_HEAD)
        yn = yc * lax.rsqrt(var + RW_GN_EPS) * gnw_ref[...] + gnb_ref[...]
        rw = (yn + bon_ref[0].astype(F32)) * g_ref[0].astype(F32)
        o = of_ref[0].astype(F32) + ob_ref[0].astype(F32)
        og = og_ref[0].astype(F32)
        parts = [rw.astype(BF16)]
        for h in range(GLA_HEADS):
            sl = slice(GLA_DV * h, GLA_DV * (h + 1))
            oh = o[:, sl]
            on = oh * lax.rsqrt(jnp.mean(oh * oh, axis=-1, keepdims=True) + GLA_NORM_EPS)
            ogh = og[:, sl]
            parts.append((on * ggn_ref[:, sl] * (ogh * _sigmoid(ogh))).astype(BF16))
        cat = jnp.concatenate(parts, axis=1)
        mx = jnp.dot(cat, wout_ref[...], preferred_element_type=F32)
        xo_ref[0, k * TB:(k + 1) * TB, :] = x_res + gate * _rms(mx, gpost_ref[...])


def _readout(y, bonus, g, o, og, ctx, x, xc, modtab, prm, latents_only):
    B, N, _ = bonus.shape
    first = 1 if latents_only else 0
    n_tiles = N // TB - first
    group = max(k for k in (4, 3, 2, 1) if n_tiles % k == 0)

    def tiles(c, shift=0):
        return [pl.BlockSpec((1, TB, c), lambda b, j, k=k: (b, jnp.maximum(first + group * j + k + shift, 0), 0))
                for k in range(group)]

    if xc is None:
        res, res_specs = [ctx] + [x] * group, [pl.BlockSpec((1, TB, D), lambda b, j: (b, 0, 0))] + tiles(D, -1)
    else:
        res, res_specs = [xc] * group, tiles(D)
    tok_args, tok_specs = [], []
    for arr in (y[0], y[1], bonus, g, o[0], o[1], og):
        tok_args += [arr] * group
        tok_specs += tiles(arr.shape[-1])
    return pl.pallas_call(
        functools.partial(_readout_kernel, group, first, xc is None),
        out_shape=jax.ShapeDtypeStruct((B, n_tiles * TB, D), F32),
        grid=(B, n_tiles // group),
        in_specs=tok_specs + res_specs + [_wide_mod_spec(),
                  _const_spec((1, RW_WIDTH)), _const_spec((1, RW_WIDTH)), _const_spec((1, GLA_WIDTH)),
                  _const_spec((D, D)), _const_spec((1, D)), _const_spec((RW_WIDTH, RW_WIDTH))],
        out_specs=pl.BlockSpec((1, group * TB, D), lambda b, j: (b, j, 0)),
        compiler_params=_cparams(("parallel", "parallel")),
        name="readout",
    )(*tok_args, *res, modtab, prm["gn_w"], prm["gn_b"], prm["gla_gn_w"],
      prm["w_out"], prm["norm_post"], prm["bd64"])


def _swiglu_acc(hb, wg_ref, wu_ref, wd_ref):
    acc = jnp.zeros((hb.shape[0], D), F32)
    for c in range(D_FF // FCH):
        sl = slice(c * FCH, (c + 1) * FCH)
        gate = jnp.dot(hb, wg_ref[:, sl].astype(BF16), preferred_element_type=F32)
        up = jnp.dot(hb, wu_ref[:, sl].astype(BF16), preferred_element_type=F32)
        act = (gate * _sigmoid(gate) * up).astype(BF16)
        acc = acc + jnp.dot(act, wd_ref[sl, :].astype(BF16), preferred_element_type=F32)
    return acc


def _ffn_kernel(x_ref, mod_ref, gpre_ref, gpost_ref, wg_ref, wu_ref, wd_ref, xo_ref):
    x = x_ref[0]
    hb = (_rms(x, gpre_ref[...]) * (1.0 + _wide_mod(mod_ref, 4)) + _wide_mod(mod_ref, 3)).astype(BF16)
    fx = _swiglu_acc(hb, wg_ref, wu_ref, wd_ref)
    xo_ref[0] = x + _wide_mod(mod_ref, 5) * _rms(fx, gpost_ref[...])


def _single_buffered(shape):
    nd = len(shape)
    return pl.BlockSpec(shape, lambda *_: (0,) * nd, pipeline_mode=pl.Buffered(1))


def _ffn(xc, modtab, g_pre, g_post, wg, wu, wd):
    B, N, _ = xc.shape
    wide = pl.BlockSpec((1, TW, D), lambda b, j: (b, j, 0))
    return pl.pallas_call(
        _ffn_kernel,
        out_shape=jax.ShapeDtypeStruct((B, N, D), F32),
        grid=(B, N // TW),
        in_specs=[wide, _wide_mod_spec(), _const_spec((1, D)), _const_spec((1, D)),
                  _single_buffered((D, D_FF)), _single_buffered((D, D_FF)), _single_buffered((D_FF, D))],
        out_specs=wide,
        compiler_params=_cparams(("parallel", "parallel")),
        name="ffn",
    )(xc, modtab, g_pre, g_post, wg, wu, wd)


MOE_TB = 1024
MOE_SEG = 32
MOE_TM = 512
MOE_R = 2 * MOE_TB + N_EXPERTS * MOE_SEG
MOE_NP = MOE_R // MOE_SEG


def _moe_route_kernel(x_ref, mod_ref, gpre_ref, router_ref, h_ref, info_ref, infot_ref, cnt_ref):
    mod = mod_ref[0, 0]
    h = _rms(x_ref[0], gpre_ref[...]) * (1.0 + mod[4:5]) + mod[3:4]
    hb = h.astype(BF16)
    h_ref[0] = hb
    lane = lax.broadcasted_iota(jnp.int32, (MOE_TB, LANES), 1)
    h_lo = (h - hb.astype(F32)).astype(BF16)
    logits = (jnp.dot(hb, router_ref[0], preferred_element_type=F32)
              + jnp.dot(hb, router_ref[1], preferred_element_type=F32)
              + jnp.dot(h_lo, router_ref[0], preferred_element_type=F32))
    logits = jnp.where(lane < N_EXPERTS, logits, -jnp.inf)
    v1 = jnp.max(logits, axis=-1, keepdims=True)
    i1 = jnp.min(jnp.where(logits == v1, lane, LANES), axis=-1, keepdims=True)
    rest = jnp.where(lane == i1, -jnp.inf, logits)
    v2 = jnp.max(rest, axis=-1, keepdims=True)
    i2 = jnp.min(jnp.where(rest == v2, lane, LANES), axis=-1, keepdims=True)
    ex = jnp.exp(v2 - v1)
    w1 = 1.0 / (1.0 + ex)
    w2 = ex * w1
    e1 = jnp.where(lane == i1, 1.0, 0.0)
    e2 = jnp.where(lane == i2, 1.0, 0.0)
    es = e1 + e2
    t = lax.broadcasted_iota(jnp.int32, (MOE_TB, MOE_TB), 0)
    s = lax.broadcasted_iota(jnp.int32, (MOE_TB, MOE_TB), 1)
    before = jnp.where(s < t, 1.0, 0.0).astype(BF16)
    rank = jnp.dot(before, es.astype(BF16), preferred_element_type=F32)
    cnt = jnp.sum(es, axis=0, keepdims=True)
    segs = jnp.floor((cnt + (MOE_SEG - 1)) * (1.0 / MOE_SEG))
    ea = lax.broadcasted_iota(jnp.int32, (LANES, LANES), 0)
    eb = lax.broadcasted_iota(jnp.int32, (LANES, LANES), 1)
    earlier = jnp.where(ea < eb, 1.0, 0.0).astype(BF16)
    start = jnp.dot(jnp.broadcast_to(segs, (SUBLANES, LANES)).astype(BF16), earlier,
                    preferred_element_type=F32)[0:1] * MOE_SEG
    pos = rank + start
    d1 = jnp.sum(e1 * pos, axis=-1, keepdims=True)
    d2 = jnp.sum(e2 * pos, axis=-1, keepdims=True)
    info = jnp.where(lane == 0, d1, jnp.where(lane == 1, d2, jnp.where(lane == 2, w1, jnp.where(lane == 3, w2, 0.0))))
    info_ref[0] = info
    infot_ref[0] = jnp.transpose(info)[0:SUBLANES]
    cnt_ref[0] = jnp.broadcast_to(cnt, (SUBLANES, LANES))


def _moe_route(xs, modtab, g_pre, router):
    B, S, _ = xs.shape
    per = S // MOE_TB
    nb = B * per
    blk = lambda c: pl.BlockSpec((1, MOE_TB, c), lambda i: (i // per, i % per, 0))
    flat = lambda r, c: pl.BlockSpec((1, r, c), lambda i: (i, 0, 0))
    return pl.pallas_call(
        _moe_route_kernel,
        out_shape=[jax.ShapeDtypeStruct((nb, MOE_TB, D), BF16), jax.ShapeDtypeStruct((nb, MOE_TB, LANES), F32),
                   jax.ShapeDtypeStruct((nb, SUBLANES, MOE_TB), F32),
                   jax.ShapeDtypeStruct((nb, SUBLANES, LANES), F32)],
        grid=(nb,),
        in_specs=[blk(D), pl.BlockSpec((1, 1, 6, D), lambda i: (i // per, 1, 0, 0)),
                  _const_spec((1, D)), _const_spec((2, D, LANES))],
        out_specs=[flat(MOE_TB, D), flat(MOE_TB, LANES), flat(SUBLANES, MOE_TB), flat(SUBLANES, LANES)],
        compiler_params=_cparams(("parallel",)),
        name="moe_route",
    )(xs, modtab, g_pre, router)


def _moe_plan(cnt, n_tiles):
    pc = (cnt + MOE_SEG - 1) // MOE_SEG * MOE_SEG
    inc = jnp.cumsum(pc, axis=1)
    loff = inc - pc
    reg = (jnp.sum(pc, axis=0) + MOE_TM - 1) // MOE_TM * MOE_TM
    gend = jnp.cumsum(reg)
    goff = (gend - reg)[None, :] + jnp.cumsum(pc, axis=0) - pc
    rows = jnp.arange(MOE_NP, dtype=jnp.int32) * MOE_SEG
    e_p = jnp.sum((rows[None, :, None] >= inc[:, None, :]).astype(jnp.int32), axis=-1)
    e_c = jnp.minimum(e_p, N_EXPERTS - 1)
    dst = jnp.take_along_axis(goff, e_c, axis=1) + rows[None, :] - jnp.take_along_axis(loff, e_c, axis=1)
    dst = jnp.where(e_p < N_EXPERTS, dst, 0).astype(jnp.int32)
    n_valid = (inc[:, -1] // MOE_SEG).astype(jnp.int32)
    trow = jnp.arange(n_tiles, dtype=jnp.int32) * MOE_TM
    te = jnp.sum((trow[:, None] >= gend[None, :]).astype(jnp.int32), axis=-1)
    valid = te < N_EXPERTS
    last = gend[-1] // MOE_TM - 1
    te = jnp.where(valid, te, te[last]).astype(jnp.int32)
    src = jnp.where(valid, jnp.arange(n_tiles, dtype=jnp.int32), last).astype(jnp.int32)
    fresh = valid & jnp.concatenate([jnp.ones((1,), bool), te[1:] != te[:-1]])
    return dst, n_valid, te, src, valid.astype(jnp.int32), fresh.astype(jnp.int32)


def _piece_copy(src_ref, src_row, dst_ref, dst_row, sem):
    return pltpu.make_async_copy(src_ref.at[pl.ds(src_row, MOE_SEG)], dst_ref.at[pl.ds(dst_row, MOE_SEG)], sem)


def _moe_gather_kernel(dst_ref, nv_ref, h_ref, infot_ref, xg_in_ref, xg_ref, buf_ref, sem_ref):
    del xg_in_ref
    i = pl.program_id(0)
    it = infot_ref[0]
    rr = lax.broadcasted_iota(jnp.int32, (MOE_R, MOE_TB), 0).astype(F32)
    onehot = jnp.where(rr == it[0:1], 1.0, jnp.where(rr == it[1:2], 1.0, 0.0)).astype(BF16)
    buf_ref[...] = jnp.dot(onehot, h_ref[0], preferred_element_type=F32).astype(BF16)
    nv = nv_ref[i]

    def piece(p):
        return _piece_copy(buf_ref, pl.multiple_of(p * MOE_SEG, MOE_SEG),
                           xg_ref, pl.multiple_of(dst_ref[i, p], MOE_SEG), sem_ref.at[p])

    def start(p, c):
        piece(p).start()
        return c

    def wait(p, c):
        piece(p).wait()
        return c

    lax.fori_loop(0, nv, start, 0)
    lax.fori_loop(0, nv, wait, 0)


def _moe_gather(dst, n_valid, h, info_t, n_rows):
    nb = h.shape[0]
    flat = lambda r, c: pl.BlockSpec((1, r, c), lambda i, *_: (i, 0, 0))
    return pl.pallas_call(
        _moe_gather_kernel,
        out_shape=jax.ShapeDtypeStruct((n_rows, D), BF16),
        grid_spec=pltpu.PrefetchScalarGridSpec(
            num_scalar_prefetch=2, grid=(nb,),
            in_specs=[flat(MOE_TB, D), flat(SUBLANES, MOE_TB), pl.BlockSpec(memory_space=pl.ANY)],
            out_specs=pl.BlockSpec(memory_space=pl.ANY),
            scratch_shapes=[pltpu.VMEM((MOE_R, D), BF16), pltpu.SemaphoreType.DMA((MOE_NP,))]),
        input_output_aliases={4: 0},
        compiler_params=_cparams(("arbitrary",)),
        name="moe_gather",
    )(dst, n_valid, h, info_t, jnp.zeros((n_rows, D), BF16))


N_FCH = D_FF // FCH


def _moe_ffn_kernel(te_ref, src_ref, valid_ref, fresh_ref, x_ref, wg_hbm, wu_hbm, wd_hbm, y_ref,
                    wg_ref, wu_ref, wd_ref, sem_ref):
    i = pl.program_id(0)
    e = te_ref[i]
    fresh = fresh_ref[i] == 1

    def slice_copies(c):
        cols = pl.ds(c * FCH, FCH)
        return (pltpu.make_async_copy(wg_hbm.at[e, :, cols], wg_ref.at[:, cols], sem_ref.at[0, c]),
                pltpu.make_async_copy(wu_hbm.at[e, :, cols], wu_ref.at[:, cols], sem_ref.at[1, c]),
                pltpu.make_async_copy(wd_hbm.at[e, cols, :], wd_ref.at[cols, :], sem_ref.at[2, c]))

    @pl.when(fresh)
    def _():
        for c in range(N_FCH):
            for cp in slice_copies(c):
                cp.start()
        hb = x_ref[...]
        acc = jnp.zeros((MOE_TM, D), F32)
        for c in range(N_FCH):
            for cp in slice_copies(c):
                cp.wait()
            sl = slice(c * FCH, (c + 1) * FCH)
            gate = jnp.dot(hb, wg_ref[:, sl].astype(BF16), preferred_element_type=F32)
            up = jnp.dot(hb, wu_ref[:, sl].astype(BF16), preferred_element_type=F32)
            act = (gate * _sigmoid(gate) * up).astype(BF16)
            acc = acc + jnp.dot(act, wd_ref[sl, :].astype(BF16), preferred_element_type=F32)
        y_ref[...] = acc.astype(BF16)

    @pl.when(jnp.logical_and(valid_ref[i] == 1, jnp.logical_not(fresh)))
    def _():
        y_ref[...] = _swiglu_acc(x_ref[...], wg_ref, wu_ref, wd_ref).astype(BF16)

    @pl.when(valid_ref[i] == 0)
    def _():
        y_ref[...] = jnp.zeros_like(y_ref)


def _moe_ffn(te, src, valid, fresh, xg, wg, wu, wd):
    n_rows = xg.shape[0]
    hbm = pl.BlockSpec(memory_space=pl.ANY)
    return pl.pallas_call(
        _moe_ffn_kernel,
        out_shape=jax.ShapeDtypeStruct((n_rows, D), BF16),
        grid_spec=pltpu.PrefetchScalarGridSpec(
            num_scalar_prefetch=4, grid=(n_rows // MOE_TM,),
            in_specs=[pl.BlockSpec((MOE_TM, D), lambda i, te, src, *_: (src[i], 0)), hbm, hbm, hbm],
            out_specs=pl.BlockSpec((MOE_TM, D), lambda i, *_: (i, 0)),
            scratch_shapes=[pltpu.VMEM((D, D_FF), F32), pltpu.VMEM((D, D_FF), F32), pltpu.VMEM((D_FF, D), F32),
                            pltpu.SemaphoreType.DMA((3, N_FCH))]),
        compiler_params=_cparams(("arbitrary",)),
        name="moe_ffn",
    )(te, src, valid, fresh, xg, wg, wu, wd)


def _moe_combine_kernel(dst_ref, nv_ref, x_ref, mod_ref, gpost_ref, info_ref, yg_ref, xo_ref, buf_ref, sem_ref):
    i = pl.program_id(0)
    nv = nv_ref[i]

    def piece(p):
        return _piece_copy(yg_ref, pl.multiple_of(dst_ref[i, p], MOE_SEG),
                           buf_ref, pl.multiple_of(p * MOE_SEG, MOE_SEG), sem_ref.at[p])

    def start(p, c):
        piece(p).start()
        return c

    def clear(p, c):
        buf_ref[pl.ds(pl.multiple_of(p * MOE_SEG, MOE_SEG), MOE_SEG), :] = jnp.zeros((MOE_SEG, D), BF16)
        return c

    def wait(p, c):
        piece(p).wait()
        return c

    lax.fori_loop(0, nv, start, 0)
    lax.fori_loop(nv, MOE_NP, clear, 0)
    info = info_ref[0]
    rr = lax.broadcasted_iota(jnp.int32, (MOE_TB, MOE_R), 1).astype(F32)
    comb = jnp.where(rr == info[:, 0:1], info[:, 2:3], jnp.where(rr == info[:, 1:2], info[:, 3:4], 0.0)).astype(BF16)
    lax.fori_loop(0, nv, wait, 0)
    fx = jnp.dot(comb, buf_ref[...], preferred_element_type=F32)
    xo_ref[0] = x_ref[0] + mod_ref[0, 0][5:6] * _rms(fx, gpost_ref[...])


def _moe_combine(dst, n_valid, xs, modtab, g_post, info, yg):
    B, S, _ = xs.shape
    per = S // MOE_TB
    blk = pl.BlockSpec((1, MOE_TB, D), lambda i, *_: (i // per, i % per, 0))
    return pl.pallas_call(
        _moe_combine_kernel,
        out_shape=jax.ShapeDtypeStruct((B, S, D), F32),
        grid_spec=pltpu.PrefetchScalarGridSpec(
            num_scalar_prefetch=2, grid=(B * per,),
            in_specs=[blk, pl.BlockSpec((1, 1, 6, D), lambda i, *_: (i // per, 1, 0, 0)),
                      pl.BlockSpec((1, D), lambda i, *_: (0, 0)),
                      pl.BlockSpec((1, MOE_TB, LANES), lambda i, *_: (i, 0, 0)),
                      pl.BlockSpec(memory_space=pl.ANY)],
            out_specs=blk,
            scratch_shapes=[pltpu.VMEM((MOE_R, D), BF16), pltpu.SemaphoreType.DMA((MOE_NP,))]),
        compiler_params=_cparams(("arbitrary",)),
        name="moe_combine",
    )(dst, n_valid, xs, modtab, g_post, info, yg)


def _moe(xs, modtab, g_pre, g_post, router, wg, wu, wd):
    B, S, _ = xs.shape
    nb = B * S // MOE_TB
    h, info, info_t, cnt = _moe_route(xs, modtab, g_pre, router)
    worst = 2 * B * S + nb * N_EXPERTS * (MOE_SEG - 1) + N_EXPERTS * (MOE_TM - 1)
    n_tiles = -(-worst // MOE_TM)
    dst, n_valid, te, src, valid, fresh = _moe_plan(cnt[:, 0, :N_EXPERTS].astype(jnp.int32), n_tiles)
    xg = _moe_gather(dst, n_valid, h, info_t, n_tiles * MOE_TM)
    yg = _moe_ffn(te, src, valid, fresh, xg, wg, wu, wd)
    return _moe_combine(dst, n_valid, xs, modtab, g_post, info, yg)


def _block_diag2(w):
    z = jnp.zeros_like(w[0])
    return jnp.concatenate([jnp.concatenate([w[0], z], axis=1), jnp.concatenate([z, w[1]], axis=1)], axis=0)


def _row(v):
    return v.reshape(1, -1).astype(F32)


def _head_ones(width, head):
    i = jnp.arange(width) // head
    return (i[:, None] == i[None, :]).astype(BF16)


def kernel(x, c, ctx, c_ctx, ada_w, ada_b, norm_mix_pre, norm_mix_post, norm_ffn_pre, norm_ffn_post, w_in, shift_mu, rw_w_up, rw_w0, rw_a_up, rw_a0, rw_k_k, rw_k_a, rw_r_k, rw_g_up, rw_gn_w, rw_gn_b, rw_v_down, rw_v_up, rw_v0, gla_conv, gla_a_up, gla_a_b, gla_gn_w, w_out, ffn_w_gate, ffn_w_up, ffn_w_down, moe_router, moe_w_gate, moe_w_up, moe_w_down):
    B, S, _ = x.shape
    n_ctx = ctx.shape[1]
    depth = w_in.shape[0]
    assert n_ctx == TB and S % MOE_TB == 0 and (n_ctx + S) % TW == 0 and depth == 2

    xc = None
    pad_rows = -(B + 1) % SUBLANES
    cvec = jnp.concatenate([c, c_ctx[None, :], jnp.zeros((pad_rows, D), F32)], axis=0)
    bd64 = _head_ones(RW_WIDTH, RW_HEAD)
    ada_b3 = ada_b.reshape(depth, 1, 6 * D)
    v_first = None
    out = None
    for i in range(depth):
        last = i == depth - 1
        mods = _adaln(cvec, ada_w, ada_b3, i)
        mod_x = mods[:B].reshape(B, 6, D)
        mod_c = jnp.broadcast_to(mods[B].reshape(1, 6, D), (B, 6, D))
        modtab = jnp.stack([mod_c, mod_x], axis=1)

        w_i = jnp.concatenate([w_in[i], jnp.zeros((D, GLA_PAD - GLA_COLS), F32)], axis=1).astype(BF16)
        p_rw, p_gl = _inproj(ctx, x, xc, _row(norm_mix_pre[i]), modtab, w_i)

        prm = dict(
            mu=_row(shift_mu[i]),
            w_up=_block_diag2(rw_w_up[i]).astype(BF16), w0=_row(rw_w0[i]),
            a_up=_block_diag2(rw_a_up[i]).astype(BF16), a0=_row(rw_a0[i]),
            k_k=_row(rw_k_k[i]), k_a=_row(rw_k_a[i]), r_k=_row(rw_r_k[i]),
            g_up=rw_g_up[i].astype(BF16), bd64=bd64,
            gn_w=_row(rw_gn_w[i]), gn_b=_row(rw_gn_b[i]), gla_gn_w=_row(gla_gn_w[i]),
            w_out=w_out[i].astype(BF16), norm_post=_row(norm_mix_post[i]),
        )
        gate_pad = jnp.zeros((LANES - 2 * GLA_GATE_RANK, 2 * GLA_KW), F32)
        gla_prm = dict(conv=gla_conv[i].astype(F32), a_b=_row(gla_a_b[i]),
                       a_up=jnp.concatenate([_block_diag2(gla_a_up[i]), gate_pad], axis=0).astype(BF16))
        if i > 0:
            pad = LANES - RW_V_RANK
            prm["v_down"] = jnp.concatenate([rw_v_down[i - 1], jnp.zeros((RW_WIDTH, pad), F32)], axis=1).astype(BF16)
            prm["v_up"] = jnp.concatenate([rw_v_up[i - 1], jnp.zeros((pad, RW_WIDTH), F32)], axis=0).astype(BF16)
            prm["v0"] = _row(rw_v0[i - 1])

        r, kk, vm, g, bonus, lw, bb, ke = _rwprep(p_rw, prm, v_first if i > 0 else None)
        if i == 0:
            v_first = vm
        q, k, gv, og, lg = _glaprep(p_gl, gla_prm)
        y, o = _scans(r, kk, vm, lw, bb, ke, q, k, gv, lg)
        xc = _readout(y, bonus, g, o, og, ctx, x, xc, modtab, prm, latents_only=last)

        jf = i // 2
        if i % 2 == 0:
            xc = _ffn(xc, modtab, _row(norm_ffn_pre[i]), _row(norm_ffn_post[i]),
                      ffn_w_gate[jf].astype(BF16), ffn_w_up[jf].astype(BF16), ffn_w_down[jf].astype(BF16))
        else:
            router = jnp.concatenate([moe_router[jf], jnp.zeros((D, LANES - N_EXPERTS), F32)], axis=1)
            r_hi = router.astype(BF16)
            router = jnp.stack([r_hi, (router - r_hi.astype(F32)).astype(BF16)])
            out = _moe(xc, modtab, _row(norm_ffn_pre[i]), _row(norm_ffn_post[i]), router,
                       moe_w_gate[jf], moe_w_up[jf], moe_w_down[jf])
    return out
```

```python
import functools
import math

import jax
import jax.numpy as jnp
from jax import lax
from jax.experimental import pallas as pl
from jax.experimental.pallas import tpu as pltpu

F32, BF16 = jnp.float32, jnp.bfloat16
ACT = BF16

D = 1024
GRID_W = 64
RW_WIDTH = 512
RW_HEAD = 64
RW_RANK = 64
RW_G_RANK = 128
RW_V_RANK = 32
RW_GN_EPS = 64e-5
GLA_WIDTH = 512
GLA_HEADS = 4
GLA_DV = 128
GLA_DK = 64
GLA_KW = 256
GLA_GATE_RANK = 16
GLA_GATE_NORM = 16.0
GLA_NORM_EPS = 1e-5
D_FF = 2816
N_EXPERTS = 8
NORM_EPS = 1e-6
RW_COLS = 3 * RW_WIDTH + 4 * RW_RANK + RW_G_RANK
GLA_QKV = 2 * GLA_KW + GLA_WIDTH
GLA_COLS = GLA_QKV + GLA_WIDTH + 2 * GLA_GATE_RANK
GLA_PAD = 1664

LANES = 128
SUBLANES = 8
TB = 256
TW = 3 * TB
CH = 64
FCH = 256
HALO = 16
VMEM_LIMIT = 56 * 1024 * 1024


def _cparams(sem):
    return pltpu.CompilerParams(dimension_semantics=sem, vmem_limit_bytes=VMEM_LIMIT)


def _bdot(a, b):
    return jnp.dot(a.astype(BF16), b.astype(BF16), preferred_element_type=F32)


def _bdot_nt(a, b):
    return lax.dot_general(a.astype(BF16), b.astype(BF16), (((1,), (1,)), ((), ())),
                           preferred_element_type=F32)


def _bdot_tn(a, b):
    return lax.dot_general(a.astype(BF16), b.astype(BF16), (((0,), (0,)), ((), ())),
                           preferred_element_type=F32)


def _split_dot(a_exact, x):
    h1 = x.astype(BF16)
    r1 = x - h1.astype(F32)
    h2 = r1.astype(BF16)
    h3 = (r1 - h2.astype(F32)).astype(BF16)
    return (jnp.dot(a_exact, h1, preferred_element_type=F32)
            + jnp.dot(a_exact, h2, preferred_element_type=F32)
            + jnp.dot(a_exact, h3, preferred_element_type=F32))


def _seg_sum(x, bd, exact=True):
    hi = x.astype(BF16)
    out = jnp.dot(hi, bd, preferred_element_type=F32)
    if exact:
        lo = (x - hi.astype(F32)).astype(BF16)
        out = out + jnp.dot(lo, bd, preferred_element_type=F32)
    return out


def _sigmoid(x):
    return jax.nn.sigmoid(x)


def _rms(x, g):
    return x * lax.rsqrt(jnp.mean(x * x, axis=-1, keepdims=True) + NORM_EPS) * g


def _pair_stack(z, lo):
    return jnp.concatenate([jnp.where(lo, z, 0.0), jnp.where(lo, 0.0, z)], axis=0)


def _adaln_kernel(c_ref, w_ref, b_ref, o_ref):
    c = c_ref[...]
    s = c * _sigmoid(c)
    o_ref[...] = jnp.dot(s, w_ref[0], precision=lax.Precision.HIGHEST,
                         preferred_element_type=F32) + b_ref[0]


def _adaln(cvec, w, b, layer):
    rows = cvec.shape[0]
    n = w.shape[2]
    return pl.pallas_call(
        _adaln_kernel,
        out_shape=jax.ShapeDtypeStruct((rows, n), F32),
        grid=(n // D,),
        in_specs=[pl.BlockSpec((rows, D), lambda i: (0, 0)),
                  pl.BlockSpec((1, D, D), lambda i: (layer, 0, i)),
                  pl.BlockSpec((1, 1, D), lambda i: (layer, 0, i))],
        out_specs=pl.BlockSpec((rows, D), lambda i: (0, i)),
        compiler_params=_cparams(("arbitrary",)),
        name="adaln",
    )(cvec, w, b)


def _wide_mod(mod_ref, k):
    is_ctx = jnp.logical_and(pl.program_id(1) == 0, lax.broadcasted_iota(jnp.int32, (TW, 1), 0) < TB)
    return jnp.where(is_ctx, mod_ref[0, 0, k:k + 1], mod_ref[0, 1, k:k + 1])


def _wide_rows(refs):
    if len(refs) == 1:
        return refs[0][0]
    first = jnp.where(pl.program_id(1) == 0, refs[0][0], refs[1][0])
    return jnp.concatenate([first] + [r[0] for r in refs[2:]], axis=0)


def _inproj_kernel(n_src, *refs):
    g_ref, mod_ref, w_ref, prw_ref, pgl_ref = refs[n_src:]
    x = _wide_rows(refs[:n_src])
    h = _rms(x, g_ref[...]) * (1.0 + _wide_mod(mod_ref, 1)) + _wide_mod(mod_ref, 0)
    hb = h.astype(BF16)
    prw_ref[0] = jnp.dot(hb, w_ref[0, :, :RW_COLS].astype(BF16), preferred_element_type=F32).astype(ACT)
    pgl_ref[0, :, :GLA_COLS] = jnp.dot(hb, w_ref[0, :, RW_COLS:].astype(BF16),
                                       preferred_element_type=F32).astype(ACT)
    pgl_ref[0, :, GLA_COLS:] = jnp.zeros((TW, GLA_PAD - GLA_COLS), ACT)


def _wide_src(ctx, x, xc):
    if xc is not None:
        return [xc], [pl.BlockSpec((1, TW, D), lambda b, j: (b, j, 0))]
    per = TW // TB
    specs = [pl.BlockSpec((1, TB, D), lambda b, j: (b, 0, 0))]
    for k in range(per):
        specs.append(pl.BlockSpec((1, TB, D), lambda b, j, k=k: (b, jnp.maximum(j * per + k - 1, 0), 0)))
    return [ctx] + [x] * per, specs


def _wide_mod_spec():
    return pl.BlockSpec((1, 2, 6, D), lambda b, j: (b, 0, 0, 0))


def _const_spec(shape):
    nd = len(shape)
    return pl.BlockSpec(shape, lambda *_: (0,) * nd)


def _tile_spec(c):
    return pl.BlockSpec((1, TB, c), lambda b, j: (b, j, 0))


def _dir_tile_spec(c):
    return pl.BlockSpec((2, 1, TB, c), lambda b, j: (0, b, j, 0))


def _inproj(ctx, x, xc, g, modtab, w, layer):
    B = modtab.shape[0]
    N = xc.shape[1] if xc is not None else ctx.shape[1] + x.shape[1]
    srcs, src_specs = _wide_src(ctx, x, xc)
    wide = lambda c: pl.BlockSpec((1, TW, c), lambda b, j: (b, j, 0))
    return pl.pallas_call(
        functools.partial(_inproj_kernel, len(srcs)),
        out_shape=[jax.ShapeDtypeStruct((B, N, RW_COLS), ACT),
                   jax.ShapeDtypeStruct((B, N, GLA_PAD), ACT)],
        grid=(B, N // TW),
        in_specs=src_specs + [_const_spec((1, D)), _wide_mod_spec(),
                              pl.BlockSpec((1, D, RW_COLS + GLA_COLS), lambda b, j: (layer, 0, 0),
                                           pipeline_mode=pl.Buffered(1))],
        out_specs=[wide(RW_COLS), wide(GLA_PAD)],
        compiler_params=_cparams(("parallel", "parallel")),
        name="inproj",
    )(*srcs, g, modtab, w)


def _shift_table():
    t = jnp.arange(TB)
    prev = t[:, None] - 1 == t[None, :]
    nxt = t[:, None] + 1 == t[None, :]
    col = (t % GRID_W)[:, None]
    return jnp.stack([jnp.stack([prev, nxt]),
                      jnp.stack([prev & (col != 0), nxt & (col != GRID_W - 1)])]).astype(BF16)


def _rwprep_kernel(has_vres, n_tiles, p_ref, hu_ref, hd_ref, sh_ref, mu_ref, wup_ref, w0_ref, aup_ref,
                   a0_ref, kk_ref, ka_ref, rk_ref, gup_ref, bd_ref, *rest):
    if has_vres:
        vf_ref, vdn_ref, vup_ref, v0_ref = rest[:4]
        rest = rest[4:]
    r_out, kk_out, v_out, g_out, bon_out, lw_out, b_out, ke_out = rest
    j = pl.program_id(1)
    pb = p_ref[0]
    p = pb.astype(F32)
    prev1 = jnp.dot(sh_ref[0, 0], pb, preferred_element_type=F32)
    next1 = jnp.dot(sh_ref[0, 1], pb, preferred_element_type=F32)
    has_upper = jnp.where(j == 1, 0.0, 1.0)
    has_lower = jnp.where(j == n_tiles - 1, 0.0, 1.0)
    up = jnp.concatenate([hu_ref[0].astype(F32) * has_upper, p[:TB - GRID_W]], axis=0)
    down = jnp.concatenate([p[GRID_W:], hd_ref[0].astype(F32) * has_lower], axis=0)
    cls = lax.broadcasted_iota(jnp.int32, p.shape, 1) & jnp.where(j == 0, 1, 3)
    shifted = jnp.where(cls == 0, prev1, jnp.where(cls == 1, next1, jnp.where(cls == 2, up, down)))
    u = p + mu_ref[...] * (shifted - p)

    r = u[:, 0:RW_WIDTH]
    k = u[:, RW_WIDTH:2 * RW_WIDTH]
    v = u[:, 2 * RW_WIDTH:3 * RW_WIDTH]
    o = 3 * RW_WIDTH
    wd = u[:, o:o + 2 * RW_RANK]
    ad = u[:, o + 2 * RW_RANK:o + 4 * RW_RANK]
    gd = u[:, o + 4 * RW_RANK:]

    w_logit = w0_ref[...] + _bdot(jnp.tanh(wd), wup_ref[...])
    lw = -math.exp(-0.5) * _sigmoid(w_logit)
    a = _sigmoid(a0_ref[...] + _bdot(ad, aup_ref[...]))
    bd = bd_ref[...]
    kk = k * kk_ref[...]
    kk = kk * lax.rsqrt(jnp.maximum(_seg_sum(kk * kk, bd), 1e-24))
    g = _bdot(_sigmoid(gd), gup_ref[...])
    if has_vres:
        gate = _sigmoid(v0_ref[...] + _bdot(_bdot(v, vdn_ref[...]), vup_ref[...]))
        vm = v + (vf_ref[0].astype(F32) - v) * gate
    else:
        vm = v
    ke_sum = jnp.zeros_like(k)
    for d in range(2):
        a_d = a[:, d * RW_WIDTH:(d + 1) * RW_WIDTH]
        ke_d = k * (1.0 + (a_d - 1.0) * ka_ref[...])
        lw_out[d, 0] = lw[:, d * RW_WIDTH:(d + 1) * RW_WIDTH]
        b_out[d, 0] = (kk * a_d).astype(ACT)
        ke_out[d, 0] = ke_d.astype(ACT)
        ke_sum = ke_sum + ke_d
    r_out[0] = r.astype(ACT)
    kk_out[0] = kk.astype(ACT)
    v_out[0] = vm.astype(ACT)
    g_out[0] = g.astype(ACT)
    bon_out[0] = (_seg_sum(r * ke_sum * rk_ref[...], bd, exact=False) * vm).astype(ACT)


def _rwprep(p_rw, prm, v_first):
    B, N, _ = p_rw.shape
    nt = N // TB
    hb = TB // GRID_W
    nhb = N // GRID_W
    has_vres = v_first is not None
    W2 = 2 * RW_WIDTH
    in_specs = [
        _tile_spec(RW_COLS),
        pl.BlockSpec((1, GRID_W, RW_COLS), lambda b, j: (b, jnp.maximum(j * hb - 1, 0), 0)),
        pl.BlockSpec((1, GRID_W, RW_COLS), lambda b, j: (b, jnp.minimum(j * hb + hb, nhb - 1), 0)),
        pl.BlockSpec((1, 2, TB, TB), lambda b, j: (jnp.minimum(j, 1), 0, 0, 0)),
        _const_spec((1, RW_COLS)), _const_spec((2 * RW_RANK, W2)), _const_spec((1, W2)),
        _const_spec((2 * RW_RANK, W2)), _const_spec((1, W2)),
        _const_spec((1, RW_WIDTH)), _const_spec((1, RW_WIDTH)), _const_spec((1, RW_WIDTH)),
        _const_spec((RW_G_RANK, RW_WIDTH)), _const_spec((RW_WIDTH, RW_WIDTH)),
    ]
    args = [p_rw, p_rw, p_rw, _shift_table(), prm["mu"], prm["w_up"], prm["w0"], prm["a_up"], prm["a0"],
            prm["k_k"], prm["k_a"], prm["r_k"], prm["g_up"], prm["bd64"]]
    if has_vres:
        in_specs += [_tile_spec(RW_WIDTH), _const_spec((RW_WIDTH, LANES)), _const_spec((LANES, RW_WIDTH)),
                     _const_spec((1, RW_WIDTH))]
        args += [v_first, prm["v_down"], prm["v_up"], prm["v0"]]
    tok = jax.ShapeDtypeStruct((B, N, RW_WIDTH), ACT)
    dtok = jax.ShapeDtypeStruct((2, B, N, RW_WIDTH), ACT)
    return pl.pallas_call(
        functools.partial(_rwprep_kernel, has_vres, nt),
        out_shape=[tok] * 5 + [jax.ShapeDtypeStruct((2, B, N, RW_WIDTH), F32), dtok, dtok],
        grid=(B, nt),
        in_specs=in_specs,
        out_specs=[_tile_spec(RW_WIDTH)] * 5 + [_dir_tile_spec(RW_WIDTH)] * 3,
        compiler_params=_cparams(("parallel", "parallel")),
        name="rwprep",
    )(*args)


NCH = TB // CH


def _scan_kernel(rf_ref, kkf_ref, vf_ref, lwf_ref, bf_ref, kef_ref,
                 rb_ref, kkb_ref, vb_ref, lwb_ref, bb_ref, keb_ref,
                 gqf_ref, gkf_ref, gvf_ref, lgf_ref, gqb_ref, gkb_ref, gvb_ref, lgb_ref,
                 yf_ref, yb_ref, of_ref, ob_ref, st_ref, gst_ref):
    j = pl.program_id(1)

    @pl.when(j == 0)
    def _():
        st_ref[...] = jnp.zeros_like(st_ref)
        gst_ref[...] = jnp.zeros_like(gst_ref)

    ti = lax.broadcasted_iota(jnp.int32, (CH, LANES), 0)
    li = lax.broadcasted_iota(jnp.int32, (CH, LANES), 1)
    si = li & (CH - 1)
    lo = li < CH
    eye = jnp.where(ti == si, 1.0, 0.0)
    rblk = lax.broadcasted_iota(jnp.int32, (LANES, LANES), 0) // CH
    cblk = lax.broadcasted_iota(jnp.int32, (LANES, LANES), 1) // CH
    bdmask = rblk == cblk
    t64 = lax.broadcasted_iota(jnp.int32, (CH, CH), 0)
    s64 = lax.broadcasted_iota(jnp.int32, (CH, CH), 1)
    tri = [jnp.where(s64 <= t64, 1.0, 0.0).astype(BF16), jnp.where(s64 >= t64, 1.0, 0.0).astype(BF16)]
    m_incl = [si <= ti, si >= ti]
    m_strict = [si < ti, si > ti]
    n_pairs = RW_WIDTH // LANES
    in_refs = ((rf_ref, kkf_ref, vf_ref, lwf_ref, bf_ref, kef_ref),
               (rb_ref, kkb_ref, vb_ref, lwb_ref, bb_ref, keb_ref))
    y_refs = (yf_ref, yb_ref)

    def stack(z):
        return _pair_stack(z, lo).astype(BF16)

    st = {(d, p): st_ref[d, p] for d in range(2) for p in range(n_pairs)}

    def scan_step(step):
        cur = {}
        w_tot = {}
        for d in range(2):
            c = (NCH - 1 - step) if d == 1 else step
            rows = slice(c * CH, (c + 1) * CH)
            refs = in_refs[d]
            r, kk, v = (refs[i][0, rows, :].astype(F32) for i in range(3))
            lw = refs[3][0, 0, rows, :]
            b, ke = refs[4][0, 0, rows, :].astype(F32), refs[5][0, 0, rows, :].astype(F32)
            cum = _split_dot(tri[d], lw)
            tot = jnp.sum(lw, axis=0, keepdims=True)
            w_inv = jnp.exp(-cum)
            w_end = jnp.exp(tot - cum)
            w_tot[d] = jnp.exp(tot)
            rh = r * jnp.exp(cum)
            ah = -(kk * jnp.exp(cum - lw))
            bh, kh, bt, kt = b * w_inv, ke * w_inv, b * w_end, ke * w_end
            for p in range(n_pairs):
                sl = slice(LANES * p, LANES * (p + 1))
                cur[d, p] = dict(ah=ah[:, sl], rh=rh[:, sl], bh=bh[:, sl], kh=kh[:, sl], bt=bt[:, sl],
                                 kt=kt[:, sl], v=v[:, sl], rows=rows, sl=sl)
        yield
        for it in cur.values():
            it["lhs"] = jnp.concatenate([it["ah"], it["rh"]], axis=0).astype(BF16)
            rhs = jnp.concatenate([stack(it["bh"]), stack(it["kh"])], axis=0)
            it["a_all"] = _bdot_nt(it["lhs"], rhs)
        yield
        for (d, _), it in cur.items():
            a_all = it.pop("a_all")
            it["a_ab"] = jnp.where(m_strict[d], a_all[:CH, :LANES], 0.0)
            a_ak = jnp.where(m_strict[d], a_all[:CH, LANES:], 0.0)
            a_rb = jnp.where(m_incl[d], a_all[CH:, :LANES], 0.0)
            a_rk = jnp.where(m_incl[d], a_all[CH:, LANES:], 0.0)
            it["a_r"] = jnp.concatenate([a_rb, a_rk], axis=1).astype(BF16)
            it["v_bd"] = stack(it["v"])
            it["akv"] = _bdot(a_ak, it["v_bd"])
            it["t"] = eye + it["a_ab"]
            it["m"] = _bdot(it["a_ab"], stack(it["a_ab"]))
        yield
        for _ in range(int(math.log2(CH)) - 2):
            for it in cur.values():
                z = _bdot(jnp.concatenate([it["m"], it["t"]], axis=0), stack(it["m"]))
                it["m"] = z[:CH]
                it["t"] = it["t"] + z[CH:]
            yield
        for it in cur.values():
            it["t"] = (it["t"] + _bdot(it["t"], stack(it["m"]))).astype(BF16)
        yield
        for it in cur.values():
            a_til = jnp.dot(it["t"], stack(it["ah"]), preferred_element_type=F32)
            it["lhs2"] = jnp.concatenate([a_til.astype(BF16), it["lhs"][CH:]], axis=0)
            it["cc"] = jnp.dot(it["t"], stack(it["akv"]), preferred_element_type=F32)
            it["rhs_t"] = jnp.concatenate([it["bt"], it["kt"]], axis=0).astype(BF16)
        yield
        z1 = {k: _bdot_nt(it["lhs2"], st[k]) for k, it in cur.items()}
        yield
        u = {k: z1[k][:CH] + it["cc"] for k, it in cur.items()}
        upd = {k: _bdot_tn(jnp.concatenate([u[k], it["v"]], axis=0), it["rhs_t"]) for k, it in cur.items()}
        yield
        for (d, p), it in cur.items():
            y = z1[d, p][CH:] + jnp.dot(it["a_r"], jnp.concatenate([stack(u[d, p]), it["v_bd"]], axis=0),
                                        preferred_element_type=F32)
            y_refs[d][0, it["rows"], it["sl"]] = y.astype(ACT)
            st[d, p] = st[d, p] * w_tot[d][:, it["sl"]] + jnp.where(bdmask, upd[d, p], 0.0)
        yield

    gla_refs = ((gqf_ref, gkf_ref, gvf_ref, lgf_ref), (gqb_ref, gkb_ref, gvb_ref, lgb_ref))
    o_refs = (of_ref, ob_ref)
    g_pairs = GLA_KW // LANES
    gst = {(d, p): gst_ref[d, p] for d in range(2) for p in range(g_pairs)}
    lo2 = lax.broadcasted_iota(jnp.int32, (LANES, LANES), 1) < CH
    zeros_v = jnp.zeros((CH, GLA_DV), F32)

    def gla_step(step):
        cur = {}
        dec = {}
        for d in range(2):
            c = (NCH - 1 - step) if d == 1 else step
            rows = slice(c * CH, (c + 1) * CH)
            refs = gla_refs[d]
            q, k, v = (refs[i][0, rows, :].astype(F32) for i in range(3))
            lg = refs[3][0, 0, rows, :]
            cum = _split_dot(tri[d], lg)
            tot = jnp.sum(lg, axis=0, keepdims=True)
            dec[d] = jnp.exp(tot)
            qd = q * jnp.exp(cum)
            ki = k * jnp.exp(-cum)
            kend = k * jnp.exp(tot - cum)
            for p in range(g_pairs):
                sl = slice(LANES * p, LANES * (p + 1))
                cur[d, p] = dict(qd=qd[:, sl].astype(BF16), ki=ki[:, sl], kend=kend[:, sl], rows=rows, sl=sl,
                                 v0=v[:, 2 * LANES * p:2 * LANES * p + LANES],
                                 v1=v[:, 2 * LANES * p + LANES:2 * LANES * (p + 1)])
        yield
        for it in cur.values():
            it["att"] = _bdot_nt(it["qd"], _pair_stack(it["ki"], lo))
        yield
        for (d, _), it in cur.items():
            att = jnp.where(m_incl[d], it["att"], 0.0)
            v_bd = jnp.concatenate([jnp.concatenate([it["v0"], zeros_v], axis=1),
                                    jnp.concatenate([zeros_v, it["v1"]], axis=1)], axis=0)
            it["o"] = _bdot(att, v_bd)
            it["upd"] = _bdot_tn(jnp.concatenate([it["v0"], it["v1"]], axis=0), _pair_stack(it["kend"], lo))
        yield
        for (d, p), it in cur.items():
            s = gst[d, p]
            s_bd = jnp.concatenate([jnp.where(lo2, s, 0.0), jnp.where(lo2, 0.0, s)], axis=0)
            o = it["o"] + _bdot_nt(it["qd"], s_bd)
            o_refs[d][0, it["rows"], 2 * LANES * p:2 * LANES * (p + 1)] = o.astype(ACT)
            gst[d, p] = s * dec[d][:, it["sl"]] + it["upd"]
        yield

    n_stages = 12
    n_dep = 3
    gla_stages = (1, 4, 7, 10)
    pipeline = [(scan_step(s), gla_step(s)) for s in range(NCH)]
    for slot in range(n_stages + n_dep * (NCH - 1)):
        for s, (rw_gen, gla_gen) in enumerate(pipeline):
            stage = slot - n_dep * s
            if 0 <= stage < n_stages:
                next(rw_gen)
                if stage in gla_stages:
                    next(gla_gen)
    for (d, p), s in st.items():
        st_ref[d, p] = s
    for (d, p), s in gst.items():
        gst_ref[d, p] = s


def _bidir_specs(c, n_tiles):
    def back(j):
        return jnp.where(j == 0, 0, n_tiles - j)

    tok_f = pl.BlockSpec((1, TB, c), lambda b, j: (b, j, 0))
    tok_b = pl.BlockSpec((1, TB, c), lambda b, j: (b, back(j), 0))
    dir_f = pl.BlockSpec((1, 1, TB, c), lambda b, j: (0, b, j, 0))
    dir_b = pl.BlockSpec((1, 1, TB, c), lambda b, j: (1, b, back(j), 0))
    return tok_f, tok_b, dir_f, dir_b


def _scans(r, kk, v, lw, b, ke, q, k, gv, lg):
    B, N, _ = r.shape
    nt = N // TB
    tok_f, tok_b, dir_f, dir_b = _bidir_specs(RW_WIDTH, nt)
    kf, kb, dkf, dkb = _bidir_specs(GLA_KW, nt)
    out = jax.ShapeDtypeStruct((B, N, RW_WIDTH), ACT)
    assert GLA_WIDTH == RW_WIDTH
    yf, yb, of, ob = pl.pallas_call(
        _scan_kernel,
        out_shape=[out] * 4,
        grid=(B, nt),
        in_specs=[tok_f, tok_f, tok_f, dir_f, dir_f, dir_f, tok_b, tok_b, tok_b, dir_b, dir_b, dir_b,
                  kf, kf, tok_f, dkf, kb, kb, tok_b, dkb],
        out_specs=[tok_f, tok_b, tok_f, tok_b],
        scratch_shapes=[pltpu.VMEM((2, RW_WIDTH // LANES, LANES, LANES), F32),
                        pltpu.VMEM((2, GLA_KW // LANES, GLA_DV, LANES), F32)],
        compiler_params=_cparams(("parallel", "arbitrary")),
        name="scans",
    )(r, kk, v, lw, b, ke, r, kk, v, lw, b, ke, q, k, gv, lg, q, k, gv, lg)
    return (yf, yb), (of, ob)


CONV_K = TB + LANES


def _conv_shift_table():
    t = jnp.arange(TB)[:, None]
    s = jnp.arange(CONV_K)[None, :]
    prev = jnp.where(t == 0, s == TB + 2 * HALO - 1, s == t - 1)
    nxt = jnp.where(t == TB - 1, s == TB, s == t + 1)
    return jnp.stack([prev, nxt]).astype(BF16)


def _glaprep_kernel(n_tiles, p_ref, hp_ref, hn_ref, sh_ref, cw_ref, aup_ref, ab_ref, q_out, k_out, v_out,
                    og_out, lg_out):
    j = pl.program_id(1)
    ub = p_ref[0][:, :GLA_QKV]
    u = ub.astype(F32)
    has_prev = jnp.where(j <= 1, 0.0, 1.0)
    has_next = jnp.where(jnp.logical_or(j == 0, j == n_tiles - 1), 0.0, 1.0)
    ext = jnp.concatenate([ub,
                           (hn_ref[0][:, :GLA_QKV].astype(F32) * has_next).astype(BF16),
                           (hp_ref[0][:, :GLA_QKV].astype(F32) * has_prev).astype(BF16),
                           jnp.zeros((CONV_K - TB - 2 * HALO, GLA_QKV), BF16)], axis=0)
    prev1 = jnp.dot(sh_ref[0], ext, preferred_element_type=F32)
    next1 = jnp.dot(sh_ref[1], ext, preferred_element_type=F32)
    cw = cw_ref[...]
    conv = cw[0:1] * prev1 + cw[1:2] * u + cw[2:3] * next1
    qkv = conv * _sigmoid(conv)
    q_out[0] = (qkv[:, :GLA_KW] * (GLA_DK ** -0.5)).astype(ACT)
    k_out[0] = qkv[:, GLA_KW:2 * GLA_KW].astype(ACT)
    v_out[0] = qkv[:, 2 * GLA_KW:].astype(ACT)
    og_out[0] = p_ref[0][:, GLA_QKV:GLA_QKV + GLA_WIDTH]
    z = _bdot(p_ref[0][:, GLA_QKV + GLA_WIDTH:], aup_ref[...]) + ab_ref[...]
    lg = (jnp.minimum(z, 0.0) - jnp.log1p(jnp.exp(-jnp.abs(z)))) * (1.0 / GLA_GATE_NORM)
    for d in range(2):
        lg_out[d, 0] = lg[:, d * GLA_KW:(d + 1) * GLA_KW]


def _glaprep(p_gl, prm):
    B, N, _ = p_gl.shape
    nt = N // TB
    sub = HALO
    hb = TB // sub
    nhb = N // sub
    return pl.pallas_call(
        functools.partial(_glaprep_kernel, nt),
        out_shape=[jax.ShapeDtypeStruct((B, N, GLA_KW), ACT), jax.ShapeDtypeStruct((B, N, GLA_KW), ACT),
                   jax.ShapeDtypeStruct((B, N, GLA_WIDTH), ACT), jax.ShapeDtypeStruct((B, N, GLA_WIDTH), ACT),
                   jax.ShapeDtypeStruct((2, B, N, GLA_KW), F32)],
        grid=(B, nt),
        in_specs=[_tile_spec(GLA_PAD),
                  pl.BlockSpec((1, sub, GLA_PAD), lambda b, j: (b, jnp.maximum(j * hb - 1, 0), 0)),
                  pl.BlockSpec((1, sub, GLA_PAD), lambda b, j: (b, jnp.minimum(j * hb + hb, nhb - 1), 0)),
                  _const_spec((2, TB, CONV_K)),
                  _const_spec((3, GLA_QKV)), _const_spec((LANES, 2 * GLA_KW)), _const_spec((1, 2 * GLA_KW))],
        out_specs=[_tile_spec(GLA_KW), _tile_spec(GLA_KW), _tile_spec(GLA_WIDTH), _tile_spec(GLA_WIDTH),
                   _dir_tile_spec(GLA_KW)],
        compiler_params=_cparams(("parallel", "parallel")),
        name="glaprep",
    )(p_gl, p_gl, p_gl, _conv_shift_table(), prm["conv"], prm["a_up"], prm["a_b"])


def _readout_kernel(group, first, split_src, *refs):
    tok = [refs[i * group:(i + 1) * group] for i in range(7)]
    refs = refs[7 * group:]
    n_res = group + 1 if split_src else group
    res_refs, refs = refs[:n_res], refs[n_res:]
    mod_ref, gnw_ref, gnb_ref, ggn_ref, wout_ref, gpost_ref, bd_ref, xo_ref = refs
    bd = bd_ref[...]
    starts_with_ctx = jnp.logical_and(first == 0, pl.program_id(1) == 0)
    for k in range(group):
        yf_ref, yb_ref, bon_ref, g_ref, of_ref, ob_ref, og_ref = (t[k] for t in tok)
        gate = mod_ref[0, 1, 2:3]
        if split_src:
            x_res = res_refs[k + 1][0]
            if k == 0:
                x_res = jnp.where(starts_with_ctx, res_refs[0][0], x_res)
        else:
            x_res = res_refs[k][0]
        if k == 0 and first == 0:
            gate = jnp.where(starts_with_ctx, mod_ref[0, 0, 2:3], gate)
        y = yf_ref[0].astype(F32) + yb_ref[0].astype(F32)
        mu = _seg_sum(y, bd, exact=False) * (1.0 / RW_HEAD)
        yc = y - mu
        var = _seg_sum(yc * yc, bd, exact=False) * (1.0 / RW_HEAD)
        yn = yc * lax.rsqrt(var + RW_GN_EPS) * gnw_ref[...] + gnb_ref[...]
        rw = (yn + bon_ref[0].astype(F32)) * g_ref[0].astype(F32)
        o = of_ref[0].astype(F32) + ob_ref[0].astype(F32)
        og = og_ref[0].astype(F32)
        parts = [rw.astype(BF16)]
        for h in range(GLA_HEADS):
            sl = slice(GLA_DV * h, GLA_DV * (h + 1))
            oh = o[:, sl]
            on = oh * lax.rsqrt(jnp.mean(oh * oh, axis=-1, keepdims=True) + GLA_NORM_EPS)
            ogh = og[:, sl]
            parts.append((on * ggn_ref[:, sl] * (ogh * _sigmoid(ogh))).astype(BF16))
        cat = jnp.concatenate(parts, axis=1)
        mx = jnp.dot(cat, wout_ref[...], preferred_element_type=F32)
        xo_ref[0, k * TB:(k + 1) * TB, :] = x_res + gate * _rms(mx, gpost_ref[...])


def _readout(y, bonus, g, o, og, ctx, x, xc, modtab, prm, latents_only):
    B, N, _ = bonus.shape
    first = 1 if latents_only else 0
    n_tiles = N // TB - first
    group = max(k for k in (4, 3, 2, 1) if n_tiles % k == 0)

    def tiles(c, shift=0):
        return [pl.BlockSpec((1, TB, c), lambda b, j, k=k: (b, jnp.maximum(first + group * j + k + shift, 0), 0))
                for k in range(group)]

    if xc is None:
        res, res_specs = [ctx] + [x] * group, [pl.BlockSpec((1, TB, D), lambda b, j: (b, 0, 0))] + tiles(D, -1)
    else:
        res, res_specs = [xc] * group, tiles(D)
    tok_args, tok_specs = [], []
    for arr in (y[0], y[1], bonus, g, o[0], o[1], og):
        tok_args += [arr] * group
        tok_specs += tiles(arr.shape[-1])
    return pl.pallas_call(
        functools.partial(_readout_kernel, group, first, xc is None),
        out_shape=jax.ShapeDtypeStruct((B, n_tiles * TB, D), F32),
        grid=(B, n_tiles // group),
        in_specs=tok_specs + res_specs + [_wide_mod_spec(),
                  _const_spec((1, RW_WIDTH)), _const_spec((1, RW_WIDTH)), _const_spec((1, GLA_WIDTH)),
                  _const_spec((D, D)), _const_spec((1, D)), _const_spec((RW_WIDTH, RW_WIDTH))],
        out_specs=pl.BlockSpec((1, group * TB, D), lambda b, j: (b, j, 0)),
        compiler_params=_cparams(("parallel", "parallel")),
        name="readout",
    )(*tok_args, *res, modtab, prm["gn_w"], prm["gn_b"], prm["gla_gn_w"],
      prm["w_out"], prm["norm_post"], prm["bd64"])


def _swiglu_acc(hb, wg_ref, wu_ref, wd_ref):
    acc = jnp.zeros((hb.shape[0], D), F32)
    for c in range(D_FF // FCH):
        sl = slice(c * FCH, (c + 1) * FCH)
        gate = jnp.dot(hb, wg_ref[:, sl].astype(BF16), preferred_element_type=F32)
        up = jnp.dot(hb, wu_ref[:, sl].astype(BF16), preferred_element_type=F32)
        act = (gate * _sigmoid(gate) * up).astype(BF16)
        acc = acc + jnp.dot(act, wd_ref[sl, :].astype(BF16), preferred_element_type=F32)
    return acc


def _ffn_kernel(x_ref, mod_ref, gpre_ref, gpost_ref, wg_ref, wu_ref, wd_ref, xo_ref):
    x = x_ref[0]
    hb = (_rms(x, gpre_ref[...]) * (1.0 + _wide_mod(mod_ref, 4)) + _wide_mod(mod_ref, 3)).astype(BF16)
    fx = _swiglu_acc(hb, wg_ref, wu_ref, wd_ref)
    xo_ref[0] = x + _wide_mod(mod_ref, 5) * _rms(fx, gpost_ref[...])


def _single_buffered(shape):
    nd = len(shape)
    return pl.BlockSpec(shape, lambda *_: (0,) * nd, pipeline_mode=pl.Buffered(1))


def _ffn(xc, modtab, g_pre, g_post, wg, wu, wd):
    B, N, _ = xc.shape
    wide = pl.BlockSpec((1, TW, D), lambda b, j: (b, j, 0))
    return pl.pallas_call(
        _ffn_kernel,
        out_shape=jax.ShapeDtypeStruct((B, N, D), F32),
        grid=(B, N // TW),
        in_specs=[wide, _wide_mod_spec(), _const_spec((1, D)), _const_spec((1, D)),
                  _single_buffered((D, D_FF)), _single_buffered((D, D_FF)), _single_buffered((D_FF, D))],
        out_specs=wide,
        compiler_params=_cparams(("parallel", "parallel")),
        name="ffn",
    )(xc, modtab, g_pre, g_post, wg, wu, wd)


MOE_TB = 1024
MOE_SEG = 32
MOE_TM = 512
MOE_R = 2 * MOE_TB + N_EXPERTS * MOE_SEG
MOE_NP = MOE_R // MOE_SEG


def _moe_route_kernel(x_ref, mod_ref, gpre_ref, router_ref, h_ref, info_ref, infot_ref, cnt_ref):
    mod = mod_ref[0, 0]
    h = _rms(x_ref[0], gpre_ref[...]) * (1.0 + mod[4:5]) + mod[3:4]
    hb = h.astype(BF16)
    h_ref[0] = hb
    lane = lax.broadcasted_iota(jnp.int32, (MOE_TB, LANES), 1)
    h_lo = (h - hb.astype(F32)).astype(BF16)
    logits = (jnp.dot(hb, router_ref[0], preferred_element_type=F32)
              + jnp.dot(hb, router_ref[1], preferred_element_type=F32)
              + jnp.dot(h_lo, router_ref[0], preferred_element_type=F32))
    logits = jnp.where(lane < N_EXPERTS, logits, -jnp.inf)
    v1 = jnp.max(logits, axis=-1, keepdims=True)
    i1 = jnp.min(jnp.where(logits == v1, lane, LANES), axis=-1, keepdims=True)
    rest = jnp.where(lane == i1, -jnp.inf, logits)
    v2 = jnp.max(rest, axis=-1, keepdims=True)
    i2 = jnp.min(jnp.where(rest == v2, lane, LANES), axis=-1, keepdims=True)
    ex = jnp.exp(v2 - v1)
    w1 = 1.0 / (1.0 + ex)
    w2 = ex * w1
    e1 = jnp.where(lane == i1, 1.0, 0.0)
    e2 = jnp.where(lane == i2, 1.0, 0.0)
    es = e1 + e2
    t = lax.broadcasted_iota(jnp.int32, (MOE_TB, MOE_TB), 0)
    s = lax.broadcasted_iota(jnp.int32, (MOE_TB, MOE_TB), 1)
    before = jnp.where(s < t, 1.0, 0.0).astype(BF16)
    rank = jnp.dot(before, es.astype(BF16), preferred_element_type=F32)
    cnt = jnp.sum(es, axis=0, keepdims=True)
    segs = jnp.floor((cnt + (MOE_SEG - 1)) * (1.0 / MOE_SEG))
    ea = lax.broadcasted_iota(jnp.int32, (LANES, LANES), 0)
    eb = lax.broadcasted_iota(jnp.int32, (LANES, LANES), 1)
    earlier = jnp.where(ea < eb, 1.0, 0.0).astype(BF16)
    start = jnp.dot(jnp.broadcast_to(segs, (SUBLANES, LANES)).astype(BF16), earlier,
                    preferred_element_type=F32)[0:1] * MOE_SEG
    pos = rank + start
    d1 = jnp.sum(e1 * pos, axis=-1, keepdims=True)
    d2 = jnp.sum(e2 * pos, axis=-1, keepdims=True)
    info = jnp.where(lane == 0, d1, jnp.where(lane == 1, d2, jnp.where(lane == 2, w1, jnp.where(lane == 3, w2, 0.0))))
    info_ref[0] = info
    infot_ref[0] = jnp.transpose(info)[0:SUBLANES]
    cnt_ref[0] = jnp.broadcast_to(cnt, (SUBLANES, LANES))


def _moe_route(xs, modtab, g_pre, router):
    B, S, _ = xs.shape
    per = S // MOE_TB
    nb = B * per
    blk = lambda c: pl.BlockSpec((1, MOE_TB, c), lambda i: (i // per, i % per, 0))
    flat = lambda r, c: pl.BlockSpec((1, r, c), lambda i: (i, 0, 0))
    return pl.pallas_call(
        _moe_route_kernel,
        out_shape=[jax.ShapeDtypeStruct((nb, MOE_TB, D), BF16), jax.ShapeDtypeStruct((nb, MOE_TB, LANES), F32),
                   jax.ShapeDtypeStruct((nb, SUBLANES, MOE_TB), F32),
                   jax.ShapeDtypeStruct((nb, SUBLANES, LANES), F32)],
        grid=(nb,),
        in_specs=[blk(D), pl.BlockSpec((1, 1, 6, D), lambda i: (i // per, 1, 0, 0)),
                  _const_spec((1, D)), _const_spec((2, D, LANES))],
        out_specs=[flat(MOE_TB, D), flat(MOE_TB, LANES), flat(SUBLANES, MOE_TB), flat(SUBLANES, LANES)],
        compiler_params=_cparams(("parallel",)),
        name="moe_route",
    )(xs, modtab, g_pre, router)


def _moe_plan(cnt, n_tiles):
    pc = (cnt + MOE_SEG - 1) // MOE_SEG * MOE_SEG
    inc = jnp.cumsum(pc, axis=1)
    loff = inc - pc
    reg = (jnp.sum(pc, axis=0) + MOE_TM - 1) // MOE_TM * MOE_TM
    gend = jnp.cumsum(reg)
    goff = (gend - reg)[None, :] + jnp.cumsum(pc, axis=0) - pc
    rows = jnp.arange(MOE_NP, dtype=jnp.int32) * MOE_SEG
    e_p = jnp.sum((rows[None, :, None] >= inc[:, None, :]).astype(jnp.int32), axis=-1)
    e_c = jnp.minimum(e_p, N_EXPERTS - 1)
    dst = jnp.take_along_axis(goff, e_c, axis=1) + rows[None, :] - jnp.take_along_axis(loff, e_c, axis=1)
    dst = jnp.where(e_p < N_EXPERTS, dst, 0).astype(jnp.int32)
    n_valid = (inc[:, -1] // MOE_SEG).astype(jnp.int32)
    trow = jnp.arange(n_tiles, dtype=jnp.int32) * MOE_TM
    te = jnp.sum((trow[:, None] >= gend[None, :]).astype(jnp.int32), axis=-1)
    valid = te < N_EXPERTS
    last = gend[-1] // MOE_TM - 1
    te = jnp.where(valid, te, te[last]).astype(jnp.int32)
    src = jnp.where(valid, jnp.arange(n_tiles, dtype=jnp.int32), last).astype(jnp.int32)
    fresh = valid & jnp.concatenate([jnp.ones((1,), bool), te[1:] != te[:-1]])
    return dst, n_valid, te, src, valid.astype(jnp.int32), fresh.astype(jnp.int32)


def _piece_copy(src_ref, src_row, dst_ref, dst_row, sem):
    return pltpu.make_async_copy(src_ref.at[pl.ds(src_row, MOE_SEG)], dst_ref.at[pl.ds(dst_row, MOE_SEG)], sem)


def _moe_gather_kernel(dst_ref, nv_ref, h_ref, infot_ref, xg_in_ref, xg_ref, buf_ref, sem_ref):
    del xg_in_ref
    i = pl.program_id(0)
    it = infot_ref[0]
    rr = lax.broadcasted_iota(jnp.int32, (MOE_R, MOE_TB), 0).astype(F32)
    onehot = jnp.where(rr == it[0:1], 1.0, jnp.where(rr == it[1:2], 1.0, 0.0)).astype(BF16)
    buf_ref[...] = jnp.dot(onehot, h_ref[0], preferred_element_type=F32).astype(BF16)
    nv = nv_ref[i]

    def piece(p):
        return _piece_copy(buf_ref, pl.multiple_of(p * MOE_SEG, MOE_SEG),
                           xg_ref, pl.multiple_of(dst_ref[i, p], MOE_SEG), sem_ref.at[p])

    def start(p, c):
        piece(p).start()
        return c

    def wait(p, c):
        piece(p).wait()
        return c

    lax.fori_loop(0, nv, start, 0)
    lax.fori_loop(0, nv, wait, 0)


def _moe_gather(dst, n_valid, h, info_t, n_rows):
    nb = h.shape[0]
    flat = lambda r, c: pl.BlockSpec((1, r, c), lambda i, *_: (i, 0, 0))
    return pl.pallas_call(
        _moe_gather_kernel,
        out_shape=jax.ShapeDtypeStruct((n_rows, D), BF16),
        grid_spec=pltpu.PrefetchScalarGridSpec(
            num_scalar_prefetch=2, grid=(nb,),
            in_specs=[flat(MOE_TB, D), flat(SUBLANES, MOE_TB), pl.BlockSpec(memory_space=pl.ANY)],
            out_specs=pl.BlockSpec(memory_space=pl.ANY),
            scratch_shapes=[pltpu.VMEM((MOE_R, D), BF16), pltpu.SemaphoreType.DMA((MOE_NP,))]),
        input_output_aliases={4: 0},
        compiler_params=_cparams(("arbitrary",)),
        name="moe_gather",
    )(dst, n_valid, h, info_t, jnp.zeros((n_rows, D), BF16))


N_FCH = D_FF // FCH


def _moe_ffn_kernel(te_ref, src_ref, valid_ref, fresh_ref, x_ref, wg_hbm, wu_hbm, wd_hbm, y_ref,
                    wg_ref, wu_ref, wd_ref, sem_ref):
    i = pl.program_id(0)
    e = te_ref[i]
    fresh = fresh_ref[i] == 1

    def slice_copies(c):
        cols = pl.ds(c * FCH, FCH)
        return (pltpu.make_async_copy(wg_hbm.at[e, :, cols], wg_ref.at[:, cols], sem_ref.at[0, c]),
                pltpu.make_async_copy(wu_hbm.at[e, :, cols], wu_ref.at[:, cols], sem_ref.at[1, c]),
                pltpu.make_async_copy(wd_hbm.at[e, cols, :], wd_ref.at[cols, :], sem_ref.at[2, c]))

    @pl.when(fresh)
    def _():
        for c in range(N_FCH):
            for cp in slice_copies(c):
                cp.start()
        hb = x_ref[...]
        acc = jnp.zeros((MOE_TM, D), F32)
        for c in range(N_FCH):
            for cp in slice_copies(c):
                cp.wait()
            sl = slice(c * FCH, (c + 1) * FCH)
            gate = jnp.dot(hb, wg_ref[:, sl].astype(BF16), preferred_element_type=F32)
            up = jnp.dot(hb, wu_ref[:, sl].astype(BF16), preferred_element_type=F32)
            act = (gate * _sigmoid(gate) * up).astype(BF16)
            acc = acc + jnp.dot(act, wd_ref[sl, :].astype(BF16), preferred_element_type=F32)
        y_ref[...] = acc.astype(BF16)

    @pl.when(jnp.logical_and(valid_ref[i] == 1, jnp.logical_not(fresh)))
    def _():
        y_ref[...] = _swiglu_acc(x_ref[...], wg_ref, wu_ref, wd_ref).astype(BF16)

    @pl.when(valid_ref[i] == 0)
    def _():
        y_ref[...] = jnp.zeros_like(y_ref)


def _moe_ffn(te, src, valid, fresh, xg, wg, wu, wd):
    n_rows = xg.shape[0]
    hbm = pl.BlockSpec(memory_space=pl.ANY)
    return pl.pallas_call(
        _moe_ffn_kernel,
        out_shape=jax.ShapeDtypeStruct((n_rows, D), BF16),
        grid_spec=pltpu.PrefetchScalarGridSpec(
            num_scalar_prefetch=4, grid=(n_rows // MOE_TM,),
            in_specs=[pl.BlockSpec((MOE_TM, D), lambda i, te, src, *_: (src[i], 0)), hbm, hbm, hbm],
            out_specs=pl.BlockSpec((MOE_TM, D), lambda i, *_: (i, 0)),
            scratch_shapes=[pltpu.VMEM((D, D_FF), F32), pltpu.VMEM((D, D_FF), F32), pltpu.VMEM((D_FF, D), F32),
                            pltpu.SemaphoreType.DMA((3, N_FCH))]),
        compiler_params=_cparams(("arbitrary",)),
        name="moe_ffn",
    )(te, src, valid, fresh, xg, wg, wu, wd)


def _moe_combine_kernel(dst_ref, nv_ref, x_ref, mod_ref, gpost_ref, info_ref, yg_ref, xo_ref, buf_ref, sem_ref):
    i = pl.program_id(0)
    nv = nv_ref[i]

    def piece(p):
        return _piece_copy(yg_ref, pl.multiple_of(dst_ref[i, p], MOE_SEG),
                           buf_ref, pl.multiple_of(p * MOE_SEG, MOE_SEG), sem_ref.at[p])

    def start(p, c):
        piece(p).start()
        return c

    def clear(p, c):
        buf_ref[pl.ds(pl.multiple_of(p * MOE_SEG, MOE_SEG), MOE_SEG), :] = jnp.zeros((MOE_SEG, D), BF16)
        return c

    def wait(p, c):
        piece(p).wait()
        return c

    lax.fori_loop(0, nv, start, 0)
    lax.fori_loop(nv, MOE_NP, clear, 0)
    info = info_ref[0]
    rr = lax.broadcasted_iota(jnp.int32, (MOE_TB, MOE_R), 1).astype(F32)
    comb = jnp.where(rr == info[:, 0:1], info[:, 2:3], jnp.where(rr == info[:, 1:2], info[:, 3:4], 0.0)).astype(BF16)
    lax.fori_loop(0, nv, wait, 0)
    fx = jnp.dot(comb, buf_ref[...], preferred_element_type=F32)
    xo_ref[0] = x_ref[0] + mod_ref[0, 0][5:6] * _rms(fx, gpost_ref[...])


def _moe_combine(dst, n_valid, xs, modtab, g_post, info, yg):
    B, S, _ = xs.shape
    per = S // MOE_TB
    blk = pl.BlockSpec((1, MOE_TB, D), lambda i, *_: (i // per, i % per, 0))
    return pl.pallas_call(
        _moe_combine_kernel,
        out_shape=jax.ShapeDtypeStruct((B, S, D), F32),
        grid_spec=pltpu.PrefetchScalarGridSpec(
            num_scalar_prefetch=2, grid=(B * per,),
            in_specs=[blk, pl.BlockSpec((1, 1, 6, D), lambda i, *_: (i // per, 1, 0, 0)),
                      pl.BlockSpec((1, D), lambda i, *_: (0, 0)),
                      pl.BlockSpec((1, MOE_TB, LANES), lambda i, *_: (i, 0, 0)),
                      pl.BlockSpec(memory_space=pl.ANY)],
            out_specs=blk,
            scratch_shapes=[pltpu.VMEM((MOE_R, D), BF16), pltpu.SemaphoreType.DMA((MOE_NP,))]),
        compiler_params=_cparams(("arbitrary",)),
        name="moe_combine",
    )(dst, n_valid, xs, modtab, g_post, info, yg)


def _moe(xs, modtab, g_pre, g_post, router, wg, wu, wd):
    B, S, _ = xs.shape
    nb = B * S // MOE_TB
    h, info, info_t, cnt = _moe_route(xs, modtab, g_pre, router)
    worst = 2 * B * S + nb * N_EXPERTS * (MOE_SEG - 1) + N_EXPERTS * (MOE_TM - 1)
    n_tiles = -(-worst // MOE_TM)
    dst, n_valid, te, src, valid, fresh = _moe_plan(cnt[:, 0, :N_EXPERTS].astype(jnp.int32), n_tiles)
    xg = _moe_gather(dst, n_valid, h, info_t, n_tiles * MOE_TM)
    yg = _moe_ffn(te, src, valid, fresh, xg, wg, wu, wd)
    return _moe_combine(dst, n_valid, xs, modtab, g_post, info, yg)


def _block_diag2(w):
    z = jnp.zeros_like(w[0])
    return jnp.concatenate([jnp.concatenate([w[0], z], axis=1), jnp.concatenate([z, w[1]], axis=1)], axis=0)


def _row(v):
    return v.reshape(1, -1).astype(F32)


def _head_ones(width, head):
    i = jnp.arange(width) // head
    return (i[:, None] == i[None, :]).astype(BF16)


def kernel(x, c, ctx, c_ctx, ada_w, ada_b, norm_mix_pre, norm_mix_post, norm_ffn_pre, norm_ffn_post, w_in, shift_mu, rw_w_up, rw_w0, rw_a_up, rw_a0, rw_k_k, rw_k_a, rw_r_k, rw_g_up, rw_gn_w, rw_gn_b, rw_v_down, rw_v_up, rw_v0, gla_conv, gla_a_up, gla_a_b, gla_gn_w, w_out, ffn_w_gate, ffn_w_up, ffn_w_down, moe_router, moe_w_gate, moe_w_up, moe_w_down):
    B, S, _ = x.shape
    n_ctx = ctx.shape[1]
    depth = w_in.shape[0]
    assert n_ctx == TB and S % MOE_TB == 0 and (n_ctx + S) % TW == 0 and depth == 2

    xc = None
    pad_rows = -(B + 1) % SUBLANES
    cvec = jnp.concatenate([c, c_ctx[None, :], jnp.zeros((pad_rows, D), F32)], axis=0)
    bd64 = _head_ones(RW_WIDTH, RW_HEAD)
    ada_b3 = ada_b.reshape(depth, 1, 6 * D)
    v_first = None
    out = None
    for i in range(depth):
        last = i == depth - 1
        mods = _adaln(cvec, ada_w, ada_b3, i)
        mod_x = mods[:B].reshape(B, 6, D)
        mod_c = jnp.broadcast_to(mods[B].reshape(1, 6, D), (B, 6, D))
        modtab = jnp.stack([mod_c, mod_x], axis=1)

        p_rw, p_gl = _inproj(ctx, x, xc, _row(norm_mix_pre[i]), modtab, w_in, i)

        prm = dict(
            mu=_row(shift_mu[i]),
            w_up=_block_diag2(rw_w_up[i]).astype(BF16), w0=_row(rw_w0[i]),
            a_up=_block_diag2(rw_a_up[i]).astype(BF16), a0=_row(rw_a0[i]),
            k_k=_row(rw_k_k[i]), k_a=_row(rw_k_a[i]), r_k=_row(rw_r_k[i]),
            g_up=rw_g_up[i].astype(BF16), bd64=bd64,
            gn_w=_row(rw_gn_w[i]), gn_b=_row(rw_gn_b[i]), gla_gn_w=_row(gla_gn_w[i]),
            w_out=w_out[i].astype(BF16), norm_post=_row(norm_mix_post[i]),
        )
        gate_pad = jnp.zeros((LANES - 2 * GLA_GATE_RANK, 2 * GLA_KW), F32)
        gla_prm = dict(conv=gla_conv[i].astype(F32), a_b=_row(gla_a_b[i]),
                       a_up=jnp.concatenate([_block_diag2(gla_a_up[i]), gate_pad], axis=0).astype(BF16))
        if i > 0:
            pad = LANES - RW_V_RANK
            prm["v_down"] = jnp.concatenate([rw_v_down[i - 1], jnp.zeros((RW_WIDTH, pad), F32)], axis=1).astype(BF16)
            prm["v_up"] = jnp.concatenate([rw_v_up[i - 1], jnp.zeros((pad, RW_WIDTH), F32)], axis=0).astype(BF16)
            prm["v0"] = _row(rw_v0[i - 1])

        r, kk, vm, g, bonus, lw, bb, ke = _rwprep(p_rw, prm, v_first if i > 0 else None)
        if i == 0:
            v_first = vm
        q, k, gv, og, lg = _glaprep(p_gl, gla_prm)
        y, o = _scans(r, kk, vm, lw, bb, ke, q, k, gv, lg)
        xc = _readout(y, bonus, g, o, og, ctx, x, xc, modtab, prm, latents_only=last)

        jf = i // 2
        if i % 2 == 0:
            xc = _ffn(xc, modtab, _row(norm_ffn_pre[i]), _row(norm_ffn_post[i]),
                      ffn_w_gate[jf], ffn_w_up[jf], ffn_w_down[jf])
        else:
            router = jnp.concatenate([moe_router[jf], jnp.zeros((D, LANES - N_EXPERTS), F32)], axis=1)
            r_hi = router.astype(BF16)
            router = jnp.stack([r_hi, (router - r_hi.astype(F32)).astype(BF16)])
            out = _moe(xc, modtab, _row(norm_ffn_pre[i]), _row(norm_ffn_post[i]), router,
                       moe_w_gate[jf], moe_w_up[jf], moe_w_down[jf])
    return out
```

```python
import functools
import math

import jax
import jax.numpy as jnp
from jax import lax
from jax.experimental import pallas as pl
from jax.experimental.pallas import tpu as pltpu

F32, BF16 = jnp.float32, jnp.bfloat16
ACT = BF16

D = 1024
GRID_W = 64
RW_WIDTH = 512
RW_HEAD = 64
RW_RANK = 64
RW_G_RANK = 128
RW_V_RANK = 32
RW_GN_EPS = 64e-5
GLA_WIDTH = 512
GLA_HEADS = 4
GLA_DV = 128
GLA_DK = 64
GLA_KW = 256
GLA_GATE_RANK = 16
GLA_GATE_NORM = 16.0
GLA_NORM_EPS = 1e-5
D_FF = 2816
N_EXPERTS = 8
NORM_EPS = 1e-6
RW_COLS = 3 * RW_WIDTH + 4 * RW_RANK + RW_G_RANK
GLA_QKV = 2 * GLA_KW + GLA_WIDTH
GLA_COLS = GLA_QKV + GLA_WIDTH + 2 * GLA_GATE_RANK
GLA_PAD = 1664

LANES = 128
SUBLANES = 8
TB = 256
TW = 3 * TB
CH = 64
FCH = 256
HALO = 16
VMEM_LIMIT = 56 * 1024 * 1024


def _cparams(sem):
    return pltpu.CompilerParams(dimension_semantics=sem, vmem_limit_bytes=VMEM_LIMIT)


def _bdot(a, b):
    return jnp.dot(a.astype(BF16), b.astype(BF16), preferred_element_type=F32)


def _bdot_nt(a, b):
    return lax.dot_general(a.astype(BF16), b.astype(BF16), (((1,), (1,)), ((), ())),
                           preferred_element_type=F32)


def _bdot_tn(a, b):
    return lax.dot_general(a.astype(BF16), b.astype(BF16), (((0,), (0,)), ((), ())),
                           preferred_element_type=F32)


def _split_dot(a_exact, x):
    h1 = x.astype(BF16)
    r1 = x - h1.astype(F32)
    h2 = r1.astype(BF16)
    h3 = (r1 - h2.astype(F32)).astype(BF16)
    return (jnp.dot(a_exact, h1, preferred_element_type=F32)
            + jnp.dot(a_exact, h2, preferred_element_type=F32)
            + jnp.dot(a_exact, h3, preferred_element_type=F32))


def _seg_sum(x, bd, exact=True):
    hi = x.astype(BF16)
    out = jnp.dot(hi, bd, preferred_element_type=F32)
    if exact:
        lo = (x - hi.astype(F32)).astype(BF16)
        out = out + jnp.dot(lo, bd, preferred_element_type=F32)
    return out


def _sigmoid(x):
    return jax.nn.sigmoid(x)


def _rms(x, g):
    return x * lax.rsqrt(jnp.mean(x * x, axis=-1, keepdims=True) + NORM_EPS) * g


def _pair_stack(z, lo):
    return jnp.concatenate([jnp.where(lo, z, 0.0), jnp.where(lo, 0.0, z)], axis=0)


def _adaln_kernel(c_ref, w_ref, b_ref, o_ref):
    c = c_ref[...]
    s = c * _sigmoid(c)
    o_ref[...] = jnp.dot(s, w_ref[0], precision=lax.Precision.HIGHEST,
                         preferred_element_type=F32) + b_ref[0]


def _adaln(cvec, w, b, layer):
    rows = cvec.shape[0]
    n = w.shape[2]
    return pl.pallas_call(
        _adaln_kernel,
        out_shape=jax.ShapeDtypeStruct((rows, n), F32),
        grid=(n // D,),
        in_specs=[pl.BlockSpec((rows, D), lambda i: (0, 0)),
                  pl.BlockSpec((1, D, D), lambda i: (layer, 0, i)),
                  pl.BlockSpec((1, 1, D), lambda i: (layer, 0, i))],
        out_specs=pl.BlockSpec((rows, D), lambda i: (0, i)),
        compiler_params=_cparams(("arbitrary",)),
        name="adaln",
    )(cvec, w, b)


def _wide_mod(mod_ref, k):
    is_ctx = jnp.logical_and(pl.program_id(1) == 0, lax.broadcasted_iota(jnp.int32, (TW, 1), 0) < TB)
    return jnp.where(is_ctx, mod_ref[0, 0, k:k + 1], mod_ref[0, 1, k:k + 1])


def _wide_rows(refs):
    if len(refs) == 1:
        return refs[0][0]
    first = jnp.where(pl.program_id(1) == 0, refs[0][0], refs[1][0])
    return jnp.concatenate([first] + [r[0] for r in refs[2:]], axis=0)


def _inproj_kernel(n_src, *refs):
    g_ref, mod_ref, w_ref, prw_ref, pgl_ref = refs[n_src:]
    x = _wide_rows(refs[:n_src])
    h = _rms(x, g_ref[...]) * (1.0 + _wide_mod(mod_ref, 1)) + _wide_mod(mod_ref, 0)
    hb = h.astype(BF16)
    prw_ref[0] = jnp.dot(hb, w_ref[0, :, :RW_COLS].astype(BF16), preferred_element_type=F32).astype(ACT)
    pgl_ref[0, :, :GLA_COLS] = jnp.dot(hb, w_ref[0, :, RW_COLS:].astype(BF16),
                                       preferred_element_type=F32).astype(ACT)
    pgl_ref[0, :, GLA_COLS:] = jnp.zeros((TW, GLA_PAD - GLA_COLS), ACT)


def _wide_src(ctx, x, xc):
    if xc is not None:
        return [xc], [pl.BlockSpec((1, TW, D), lambda b, j: (b, j, 0))]
    per = TW // TB
    specs = [pl.BlockSpec((1, TB, D), lambda b, j: (b, 0, 0))]
    for k in range(per):
        specs.append(pl.BlockSpec((1, TB, D), lambda b, j, k=k: (b, jnp.maximum(j * per + k - 1, 0), 0)))
    return [ctx] + [x] * per, specs


def _wide_mod_spec():
    return pl.BlockSpec((1, 2, 6, D), lambda b, j: (b, 0, 0, 0))


def _const_spec(shape):
    nd = len(shape)
    return pl.BlockSpec(shape, lambda *_: (0,) * nd)


def _tile_spec(c):
    return pl.BlockSpec((1, TB, c), lambda b, j: (b, j, 0))


def _dir_tile_spec(c):
    return pl.BlockSpec((2, 1, TB, c), lambda b, j: (0, b, j, 0))


def _inproj(ctx, x, xc, g, modtab, w, layer):
    B = modtab.shape[0]
    N = xc.shape[1] if xc is not None else ctx.shape[1] + x.shape[1]
    srcs, src_specs = _wide_src(ctx, x, xc)
    wide = lambda c: pl.BlockSpec((1, TW, c), lambda b, j: (b, j, 0))
    return pl.pallas_call(
        functools.partial(_inproj_kernel, len(srcs)),
        out_shape=[jax.ShapeDtypeStruct((B, N, RW_COLS), ACT),
                   jax.ShapeDtypeStruct((B, N, GLA_PAD), ACT)],
        grid=(B, N // TW),
        in_specs=src_specs + [_const_spec((1, D)), _wide_mod_spec(),
                              pl.BlockSpec((1, D, RW_COLS + GLA_COLS), lambda b, j: (layer, 0, 0),
                                           pipeline_mode=pl.Buffered(1))],
        out_specs=[wide(RW_COLS), wide(GLA_PAD)],
        compiler_params=_cparams(("parallel", "parallel")),
        name="inproj",
    )(*srcs, g, modtab, w)


def _shift_table():
    t = jnp.arange(TB)
    prev = t[:, None] - 1 == t[None, :]
    nxt = t[:, None] + 1 == t[None, :]
    col = (t % GRID_W)[:, None]
    return jnp.stack([jnp.stack([prev, nxt]),
                      jnp.stack([prev & (col != 0), nxt & (col != GRID_W - 1)])]).astype(BF16)


def _rwprep_kernel(has_vres, n_tiles, p_ref, hu_ref, hd_ref, sh_ref, mu_ref, wup_ref, w0_ref, aup_ref,
                   a0_ref, kk_ref, ka_ref, rk_ref, gup_ref, bd_ref, *rest):
    if has_vres:
        vf_ref, vdn_ref, vup_ref, v0_ref = rest[:4]
        rest = rest[4:]
    r_out, kk_out, v_out, g_out, bon_out, lw_out, b_out, ke_out = rest
    j = pl.program_id(1)
    pb = p_ref[0]
    p = pb.astype(F32)
    prev1 = jnp.dot(sh_ref[0, 0], pb, preferred_element_type=F32)
    next1 = jnp.dot(sh_ref[0, 1], pb, preferred_element_type=F32)
    has_upper = jnp.where(j == 1, 0.0, 1.0)
    has_lower = jnp.where(j == n_tiles - 1, 0.0, 1.0)
    up = jnp.concatenate([hu_ref[0].astype(F32) * has_upper, p[:TB - GRID_W]], axis=0)
    down = jnp.concatenate([p[GRID_W:], hd_ref[0].astype(F32) * has_lower], axis=0)
    cls = lax.broadcasted_iota(jnp.int32, p.shape, 1) & jnp.where(j == 0, 1, 3)
    shifted = jnp.where(cls == 0, prev1, jnp.where(cls == 1, next1, jnp.where(cls == 2, up, down)))
    u = p + mu_ref[...] * (shifted - p)

    r = u[:, 0:RW_WIDTH]
    k = u[:, RW_WIDTH:2 * RW_WIDTH]
    v = u[:, 2 * RW_WIDTH:3 * RW_WIDTH]
    o = 3 * RW_WIDTH
    wd = u[:, o:o + 2 * RW_RANK]
    ad = u[:, o + 2 * RW_RANK:o + 4 * RW_RANK]
    gd = u[:, o + 4 * RW_RANK:]

    w_logit = w0_ref[...] + _bdot(jnp.tanh(wd), wup_ref[...])
    lw = -math.exp(-0.5) * _sigmoid(w_logit)
    a = _sigmoid(a0_ref[...] + _bdot(ad, aup_ref[...]))
    bd = bd_ref[...]
    kk = k * kk_ref[...]
    kk = kk * lax.rsqrt(jnp.maximum(_seg_sum(kk * kk, bd), 1e-24))
    g = _bdot(_sigmoid(gd), gup_ref[...])
    if has_vres:
        gate = _sigmoid(v0_ref[...] + _bdot(_bdot(v, vdn_ref[...]), vup_ref[...]))
        vm = v + (vf_ref[0].astype(F32) - v) * gate
    else:
        vm = v
    ke_sum = jnp.zeros_like(k)
    for d in range(2):
        a_d = a[:, d * RW_WIDTH:(d + 1) * RW_WIDTH]
        ke_d = k * (1.0 + (a_d - 1.0) * ka_ref[...])
        lw_out[d, 0] = lw[:, d * RW_WIDTH:(d + 1) * RW_WIDTH]
        b_out[d, 0] = (kk * a_d).astype(ACT)
        ke_out[d, 0] = ke_d.astype(ACT)
        ke_sum = ke_sum + ke_d
    r_out[0] = r.astype(ACT)
    kk_out[0] = kk.astype(ACT)
    v_out[0] = vm.astype(ACT)
    g_out[0] = g.astype(ACT)
    bon_out[0] = (_seg_sum(r * ke_sum * rk_ref[...], bd, exact=False) * vm).astype(ACT)


def _rwprep(p_rw, prm, v_first):
    B, N, _ = p_rw.shape
    nt = N // TB
    hb = TB // GRID_W
    nhb = N // GRID_W
    has_vres = v_first is not None
    W2 = 2 * RW_WIDTH
    in_specs = [
        _tile_spec(RW_COLS),
        pl.BlockSpec((1, GRID_W, RW_COLS), lambda b, j: (b, jnp.maximum(j * hb - 1, 0), 0)),
        pl.BlockSpec((1, GRID_W, RW_COLS), lambda b, j: (b, jnp.minimum(j * hb + hb, nhb - 1), 0)),
        pl.BlockSpec((1, 2, TB, TB), lambda b, j: (jnp.minimum(j, 1), 0, 0, 0)),
        _const_spec((1, RW_COLS)), _const_spec((2 * RW_RANK, W2)), _const_spec((1, W2)),
        _const_spec((2 * RW_RANK, W2)), _const_spec((1, W2)),
        _const_spec((1, RW_WIDTH)), _const_spec((1, RW_WIDTH)), _const_spec((1, RW_WIDTH)),
        _const_spec((RW_G_RANK, RW_WIDTH)), _const_spec((RW_WIDTH, RW_WIDTH)),
    ]
    args = [p_rw, p_rw, p_rw, _shift_table(), prm["mu"], prm["w_up"], prm["w0"], prm["a_up"], prm["a0"],
            prm["k_k"], prm["k_a"], prm["r_k"], prm["g_up"], prm["bd64"]]
    if has_vres:
        in_specs += [_tile_spec(RW_WIDTH), _const_spec((RW_WIDTH, LANES)), _const_spec((LANES, RW_WIDTH)),
                     _const_spec((1, RW_WIDTH))]
        args += [v_first, prm["v_down"], prm["v_up"], prm["v0"]]
    tok = jax.ShapeDtypeStruct((B, N, RW_WIDTH), ACT)
    dtok = jax.ShapeDtypeStruct((2, B, N, RW_WIDTH), ACT)
    return pl.pallas_call(
        functools.partial(_rwprep_kernel, has_vres, nt),
        out_shape=[tok] * 5 + [jax.ShapeDtypeStruct((2, B, N, RW_WIDTH), F32), dtok, dtok],
        grid=(B, nt),
        in_specs=in_specs,
        out_specs=[_tile_spec(RW_WIDTH)] * 5 + [_dir_tile_spec(RW_WIDTH)] * 3,
        compiler_params=_cparams(("parallel", "parallel")),
        name="rwprep",
    )(*args)


NCH = TB // CH


def _scan_kernel(rf_ref, kkf_ref, vf_ref, lwf_ref, bf_ref, kef_ref,
                 rb_ref, kkb_ref, vb_ref, lwb_ref, bb_ref, keb_ref,
                 gqf_ref, gkf_ref, gvf_ref, lgf_ref, gqb_ref, gkb_ref, gvb_ref, lgb_ref,
                 yf_ref, yb_ref, of_ref, ob_ref, st_ref, gst_ref):
    j = pl.program_id(1)

    @pl.when(j == 0)
    def _():
        st_ref[...] = jnp.zeros_like(st_ref)
        gst_ref[...] = jnp.zeros_like(gst_ref)

    ti = lax.broadcasted_iota(jnp.int32, (CH, LANES), 0)
    li = lax.broadcasted_iota(jnp.int32, (CH, LANES), 1)
    si = li & (CH - 1)
    lo = li < CH
    eye = jnp.where(ti == si, 1.0, 0.0)
    rblk = lax.broadcasted_iota(jnp.int32, (LANES, LANES), 0) // CH
    cblk = lax.broadcasted_iota(jnp.int32, (LANES, LANES), 1) // CH
    bdmask = rblk == cblk
    t64 = lax.broadcasted_iota(jnp.int32, (CH, CH), 0)
    s64 = lax.broadcasted_iota(jnp.int32, (CH, CH), 1)
    tri = [jnp.where(s64 <= t64, 1.0, 0.0).astype(BF16), jnp.where(s64 >= t64, 1.0, 0.0).astype(BF16)]
    m_incl = [si <= ti, si >= ti]
    m_strict = [si < ti, si > ti]
    n_pairs = RW_WIDTH // LANES
    in_refs = ((rf_ref, kkf_ref, vf_ref, lwf_ref, bf_ref, kef_ref),
               (rb_ref, kkb_ref, vb_ref, lwb_ref, bb_ref, keb_ref))
    y_refs = (yf_ref, yb_ref)

    def stack(z):
        return _pair_stack(z, lo).astype(BF16)

    st = {(d, p): st_ref[d, p] for d in range(2) for p in range(n_pairs)}

    def scan_step(step):
        cur = {}
        w_tot = {}
        for d in range(2):
            c = (NCH - 1 - step) if d == 1 else step
            rows = slice(c * CH, (c + 1) * CH)
            refs = in_refs[d]
            r, kk, v = (refs[i][0, rows, :].astype(F32) for i in range(3))
            lw = refs[3][0, 0, rows, :]
            b, ke = refs[4][0, 0, rows, :].astype(F32), refs[5][0, 0, rows, :].astype(F32)
            cum = _split_dot(tri[d], lw)
            tot = jnp.sum(lw, axis=0, keepdims=True)
            w_inv = jnp.exp(-cum)
            w_end = jnp.exp(tot - cum)
            w_tot[d] = jnp.exp(tot)
            rh = r * jnp.exp(cum)
            ah = -(kk * jnp.exp(cum - lw))
            bh, kh, bt, kt = b * w_inv, ke * w_inv, b * w_end, ke * w_end
            for p in range(n_pairs):
                sl = slice(LANES * p, LANES * (p + 1))
                cur[d, p] = dict(ah=ah[:, sl], rh=rh[:, sl], bh=bh[:, sl], kh=kh[:, sl], bt=bt[:, sl],
                                 kt=kt[:, sl], v=v[:, sl], rows=rows, sl=sl)
        yield
        for it in cur.values():
            it["lhs"] = jnp.concatenate([it["ah"], it["rh"]], axis=0).astype(BF16)
            rhs = jnp.concatenate([stack(it["bh"]), stack(it["kh"])], axis=0)
            it["a_all"] = _bdot_nt(it["lhs"], rhs)
        yield
        for (d, _), it in cur.items():
            a_all = it.pop("a_all")
            it["a_ab"] = jnp.where(m_strict[d], a_all[:CH, :LANES], 0.0)
            a_ak = jnp.where(m_strict[d], a_all[:CH, LANES:], 0.0)
            a_rb = jnp.where(m_incl[d], a_all[CH:, :LANES], 0.0)
            a_rk = jnp.where(m_incl[d], a_all[CH:, LANES:], 0.0)
            it["a_r"] = jnp.concatenate([a_rb, a_rk], axis=1).astype(BF16)
            it["v_bd"] = stack(it["v"])
            it["akv"] = _bdot(a_ak, it["v_bd"])
            it["t"] = eye + it["a_ab"]
            it["m"] = _bdot(it["a_ab"], stack(it["a_ab"]))
        yield
        for _ in range(int(math.log2(CH)) - 2):
            for it in cur.values():
                z = _bdot(it["m"], jnp.concatenate([stack(it["m"]), stack(it["t"])], axis=1))
                it["m"] = z[:, :LANES]
                it["t"] = it["t"] + z[:, LANES:]
            yield
        for it in cur.values():
            it["t"] = (it["t"] + _bdot(it["m"], stack(it["t"]))).astype(BF16)
        yield
        for it in cur.values():
            ta = jnp.dot(it["t"], jnp.concatenate([stack(it["ah"]), stack(it["akv"])], axis=1),
                         preferred_element_type=F32)
            it["lhs2"] = jnp.concatenate([ta[:, :LANES].astype(BF16), it["lhs"][CH:]], axis=0)
            it["cc"] = ta[:, LANES:]
            it["rhs_t"] = jnp.concatenate([it["bt"], it["kt"]], axis=0).astype(BF16)
        yield
        z1 = {k: _bdot_nt(it["lhs2"], st[k]) for k, it in cur.items()}
        yield
        u = {k: z1[k][:CH] + it["cc"] for k, it in cur.items()}
        upd = {k: _bdot_tn(jnp.concatenate([u[k], it["v"]], axis=0), it["rhs_t"]) for k, it in cur.items()}
        yield
        for (d, p), it in cur.items():
            y = z1[d, p][CH:] + jnp.dot(it["a_r"], jnp.concatenate([stack(u[d, p]), it["v_bd"]], axis=0),
                                        preferred_element_type=F32)
            y_refs[d][0, it["rows"], it["sl"]] = y.astype(ACT)
            st[d, p] = st[d, p] * w_tot[d][:, it["sl"]] + jnp.where(bdmask, upd[d, p], 0.0)
        yield

    gla_refs = ((gqf_ref, gkf_ref, gvf_ref, lgf_ref), (gqb_ref, gkb_ref, gvb_ref, lgb_ref))
    o_refs = (of_ref, ob_ref)
    g_pairs = GLA_KW // LANES
    gst = {(d, p): gst_ref[d, p] for d in range(2) for p in range(g_pairs)}
    lo2 = lax.broadcasted_iota(jnp.int32, (LANES, LANES), 1) < CH
    zeros_v = jnp.zeros((CH, GLA_DV), F32)

    def gla_step(step):
        cur = {}
        dec = {}
        for d in range(2):
            c = (NCH - 1 - step) if d == 1 else step
            rows = slice(c * CH, (c + 1) * CH)
            refs = gla_refs[d]
            q, k, v = (refs[i][0, rows, :].astype(F32) for i in range(3))
            lg = refs[3][0, 0, rows, :]
            cum = _split_dot(tri[d], lg)
            tot = jnp.sum(lg, axis=0, keepdims=True)
            dec[d] = jnp.exp(tot)
            qd = q * jnp.exp(cum)
            ki = k * jnp.exp(-cum)
            kend = k * jnp.exp(tot - cum)
            for p in range(g_pairs):
                sl = slice(LANES * p, LANES * (p + 1))
                cur[d, p] = dict(qd=qd[:, sl].astype(BF16), ki=ki[:, sl], kend=kend[:, sl], rows=rows, sl=sl,
                                 v0=v[:, 2 * LANES * p:2 * LANES * p + LANES],
                                 v1=v[:, 2 * LANES * p + LANES:2 * LANES * (p + 1)])
        yield
        for it in cur.values():
            it["att"] = _bdot_nt(it["qd"], _pair_stack(it["ki"], lo))
        yield
        for (d, _), it in cur.items():
            att = jnp.where(m_incl[d], it["att"], 0.0)
            v_bd = jnp.concatenate([jnp.concatenate([it["v0"], zeros_v], axis=1),
                                    jnp.concatenate([zeros_v, it["v1"]], axis=1)], axis=0)
            it["o"] = _bdot(att, v_bd)
            it["upd"] = _bdot_tn(jnp.concatenate([it["v0"], it["v1"]], axis=0), _pair_stack(it["kend"], lo))
        yield
        for (d, p), it in cur.items():
            s = gst[d, p]
            s_bd = jnp.concatenate([jnp.where(lo2, s, 0.0), jnp.where(lo2, 0.0, s)], axis=0)
            o = it["o"] + _bdot_nt(it["qd"], s_bd)
            o_refs[d][0, it["rows"], 2 * LANES * p:2 * LANES * (p + 1)] = o.astype(ACT)
            gst[d, p] = s * dec[d][:, it["sl"]] + it["upd"]
        yield

    n_stages = 12
    n_dep = 3
    gla_stages = (1, 4, 7, 10)
    pipeline = [(scan_step(s), gla_step(s)) for s in range(NCH)]
    for slot in range(n_stages + n_dep * (NCH - 1)):
        for s, (rw_gen, gla_gen) in enumerate(pipeline):
            stage = slot - n_dep * s
            if 0 <= stage < n_stages:
                next(rw_gen)
                if stage in gla_stages:
                    next(gla_gen)
    for (d, p), s in st.items():
        st_ref[d, p] = s
    for (d, p), s in gst.items():
        gst_ref[d, p] = s


def _bidir_specs(c, n_tiles):
    def back(j):
        return jnp.where(j == 0, 0, n_tiles - j)

    tok_f = pl.BlockSpec((1, TB, c), lambda b, j: (b, j, 0))
    tok_b = pl.BlockSpec((1, TB, c), lambda b, j: (b, back(j), 0))
    dir_f = pl.BlockSpec((1, 1, TB, c), lambda b, j: (0, b, j, 0))
    dir_b = pl.BlockSpec((1, 1, TB, c), lambda b, j: (1, b, back(j), 0))
    return tok_f, tok_b, dir_f, dir_b


def _scans(r, kk, v, lw, b, ke, q, k, gv, lg):
    B, N, _ = r.shape
    nt = N // TB
    tok_f, tok_b, dir_f, dir_b = _bidir_specs(RW_WIDTH, nt)
    kf, kb, dkf, dkb = _bidir_specs(GLA_KW, nt)
    out = jax.ShapeDtypeStruct((B, N, RW_WIDTH), ACT)
    assert GLA_WIDTH == RW_WIDTH
    yf, yb, of, ob = pl.pallas_call(
        _scan_kernel,
        out_shape=[out] * 4,
        grid=(B, nt),
        in_specs=[tok_f, tok_f, tok_f, dir_f, dir_f, dir_f, tok_b, tok_b, tok_b, dir_b, dir_b, dir_b,
                  kf, kf, tok_f, dkf, kb, kb, tok_b, dkb],
        out_specs=[tok_f, tok_b, tok_f, tok_b],
        scratch_shapes=[pltpu.VMEM((2, RW_WIDTH // LANES, LANES, LANES), F32),
                        pltpu.VMEM((2, GLA_KW // LANES, GLA_DV, LANES), F32)],
        compiler_params=_cparams(("parallel", "arbitrary")),
        name="scans",
    )(r, kk, v, lw, b, ke, r, kk, v, lw, b, ke, q, k, gv, lg, q, k, gv, lg)
    return (yf, yb), (of, ob)


CONV_K = TB + LANES


def _conv_shift_table():
    t = jnp.arange(TB)[:, None]
    s = jnp.arange(CONV_K)[None, :]
    prev = jnp.where(t == 0, s == TB + 2 * HALO - 1, s == t - 1)
    nxt = jnp.where(t == TB - 1, s == TB, s == t + 1)
    return jnp.stack([prev, nxt]).astype(BF16)


def _glaprep_kernel(n_tiles, p_ref, hp_ref, hn_ref, sh_ref, cw_ref, aup_ref, ab_ref, q_out, k_out, v_out,
                    og_out, lg_out):
    j = pl.program_id(1)
    ub = p_ref[0][:, :GLA_QKV]
    u = ub.astype(F32)
    has_prev = jnp.where(j <= 1, 0.0, 1.0)
    has_next = jnp.where(jnp.logical_or(j == 0, j == n_tiles - 1), 0.0, 1.0)
    ext = jnp.concatenate([ub,
                           (hn_ref[0][:, :GLA_QKV].astype(F32) * has_next).astype(BF16),
                           (hp_ref[0][:, :GLA_QKV].astype(F32) * has_prev).astype(BF16),
                           jnp.zeros((CONV_K - TB - 2 * HALO, GLA_QKV), BF16)], axis=0)
    prev1 = jnp.dot(sh_ref[0], ext, preferred_element_type=F32)
    next1 = jnp.dot(sh_ref[1], ext, preferred_element_type=F32)
    cw = cw_ref[...]
    conv = cw[0:1] * prev1 + cw[1:2] * u + cw[2:3] * next1
    qkv = conv * _sigmoid(conv)
    q_out[0] = (qkv[:, :GLA_KW] * (GLA_DK ** -0.5)).astype(ACT)
    k_out[0] = qkv[:, GLA_KW:2 * GLA_KW].astype(ACT)
    v_out[0] = qkv[:, 2 * GLA_KW:].astype(ACT)
    og_out[0] = p_ref[0][:, GLA_QKV:GLA_QKV + GLA_WIDTH]
    z = _bdot(p_ref[0][:, GLA_QKV + GLA_WIDTH:], aup_ref[...]) + ab_ref[...]
    lg = (jnp.minimum(z, 0.0) - jnp.log1p(jnp.exp(-jnp.abs(z)))) * (1.0 / GLA_GATE_NORM)
    for d in range(2):
        lg_out[d, 0] = lg[:, d * GLA_KW:(d + 1) * GLA_KW]


def _glaprep(p_gl, prm):
    B, N, _ = p_gl.shape
    nt = N // TB
    sub = HALO
    hb = TB // sub
    nhb = N // sub
    return pl.pallas_call(
        functools.partial(_glaprep_kernel, nt),
        out_shape=[jax.ShapeDtypeStruct((B, N, GLA_KW), ACT), jax.ShapeDtypeStruct((B, N, GLA_KW), ACT),
                   jax.ShapeDtypeStruct((B, N, GLA_WIDTH), ACT), jax.ShapeDtypeStruct((B, N, GLA_WIDTH), ACT),
                   jax.ShapeDtypeStruct((2, B, N, GLA_KW), F32)],
        grid=(B, nt),
        in_specs=[_tile_spec(GLA_PAD),
                  pl.BlockSpec((1, sub, GLA_PAD), lambda b, j: (b, jnp.maximum(j * hb - 1, 0), 0)),
                  pl.BlockSpec((1, sub, GLA_PAD), lambda b, j: (b, jnp.minimum(j * hb + hb, nhb - 1), 0)),
                  _const_spec((2, TB, CONV_K)),
                  _const_spec((3, GLA_QKV)), _const_spec((LANES, 2 * GLA_KW)), _const_spec((1, 2 * GLA_KW))],
        out_specs=[_tile_spec(GLA_KW), _tile_spec(GLA_KW), _tile_spec(GLA_WIDTH), _tile_spec(GLA_WIDTH),
                   _dir_tile_spec(GLA_KW)],
        compiler_params=_cparams(("parallel", "parallel")),
        name="glaprep",
    )(p_gl, p_gl, p_gl, _conv_shift_table(), prm["conv"], prm["a_up"], prm["a_b"])


def _readout_kernel(group, first, split_src, *refs):
    tok = [refs[i * group:(i + 1) * group] for i in range(7)]
    refs = refs[7 * group:]
    n_res = group + 1 if split_src else group
    res_refs, refs = refs[:n_res], refs[n_res:]
    mod_ref, gnw_ref, gnb_ref, ggn_ref, wout_ref, gpost_ref, bd_ref, xo_ref = refs
    bd = bd_ref[...]
    starts_with_ctx = jnp.logical_and(first == 0, pl.program_id(1) == 0)
    for k in range(group):
        yf_ref, yb_ref, bon_ref, g_ref, of_ref, ob_ref, og_ref = (t[k] for t in tok)
        gate = mod_ref[0, 1, 2:3]
        if split_src:
            x_res = res_refs[k + 1][0]
            if k == 0:
                x_res = jnp.where(starts_with_ctx, res_refs[0][0], x_res)
        else:
            x_res = res_refs[k][0]
        if k == 0 and first == 0:
            gate = jnp.where(starts_with_ctx, mod_ref[0, 0, 2:3], gate)
        y = yf_ref[0].astype(F32) + yb_ref[0].astype(F32)
        mu = _seg_sum(y, bd, exact=False) * (1.0 / RW_HEAD)
        yc = y - mu
        var = _seg_sum(yc * yc, bd, exact=False) * (1.0 / RW_HEAD)
        yn = yc * lax.rsqrt(var + RW_GN_EPS) * gnw_ref[...] + gnb_ref[...]
        rw = (yn + bon_ref[0].astype(F32)) * g_ref[0].astype(F32)
        o = of_ref[0].astype(F32) + ob_ref[0].astype(F32)
        og = og_ref[0].astype(F32)
        parts = [rw.astype(BF16)]
        for h in range(GLA_HEADS):
            sl = slice(GLA_DV * h, GLA_DV * (h + 1))
            oh = o[:, sl]
            on = oh * lax.rsqrt(jnp.mean(oh * oh, axis=-1, keepdims=True) + GLA_NORM_EPS)
            ogh = og[:, sl]
            parts.append((on * ggn_ref[:, sl] * (ogh * _sigmoid(ogh))).astype(BF16))
        cat = jnp.concatenate(parts, axis=1)
        mx = jnp.dot(cat, wout_ref[...], preferred_element_type=F32)
        xo_ref[0, k * TB:(k + 1) * TB, :] = x_res + gate * _rms(mx, gpost_ref[...])


def _readout(y, bonus, g, o, og, ctx, x, xc, modtab, prm, latents_only):
    B, N, _ = bonus.shape
    first = 1 if latents_only else 0
    n_tiles = N // TB - first
    group = max(k for k in (4, 3, 2, 1) if n_tiles % k == 0)

    def tiles(c, shift=0):
        return [pl.BlockSpec((1, TB, c), lambda b, j, k=k: (b, jnp.maximum(first + group * j + k + shift, 0), 0))
                for k in range(group)]

    if xc is None:
        res, res_specs = [ctx] + [x] * group, [pl.BlockSpec((1, TB, D), lambda b, j: (b, 0, 0))] + tiles(D, -1)
    else:
        res, res_specs = [xc] * group, tiles(D)
    tok_args, tok_specs = [], []
    for arr in (y[0], y[1], bonus, g, o[0], o[1], og):
        tok_args += [arr] * group
        tok_specs += tiles(arr.shape[-1])
    return pl.pallas_call(
        functools.partial(_readout_kernel, group, first, xc is None),
        out_shape=jax.ShapeDtypeStruct((B, n_tiles * TB, D), F32),
        grid=(B, n_tiles // group),
        in_specs=tok_specs + res_specs + [_wide_mod_spec(),
                  _const_spec((1, RW_WIDTH)), _const_spec((1, RW_WIDTH)), _const_spec((1, GLA_WIDTH)),
                  _const_spec((D, D)), _const_spec((1, D)), _const_spec((RW_WIDTH, RW_WIDTH))],
        out_specs=pl.BlockSpec((1, group * TB, D), lambda b, j: (b, j, 0)),
        compiler_params=_cparams(("parallel", "parallel")),
        name="readout",
    )(*tok_args, *res, modtab, prm["gn_w"], prm["gn_b"], prm["gla_gn_w"],
      prm["w_out"], prm["norm_post"], prm["bd64"])


def _swiglu_acc(hb, wg_ref, wu_ref, wd_ref):
    acc = jnp.zeros((hb.shape[0], D), F32)
    for c in range(D_FF // FCH):
        sl = slice(c * FCH, (c + 1) * FCH)
        gate = jnp.dot(hb, wg_ref[:, sl].astype(BF16), preferred_element_type=F32)
        up = jnp.dot(hb, wu_ref[:, sl].astype(BF16), preferred_element_type=F32)
        act = (gate * _sigmoid(gate) * up).astype(BF16)
        acc = acc + jnp.dot(act, wd_ref[sl, :].astype(BF16), preferred_element_type=F32)
    return acc


def _ffn_kernel(x_ref, mod_ref, gpre_ref, gpost_ref, wg_ref, wu_ref, wd_ref, xo_ref):
    x = x_ref[0]
    hb = (_rms(x, gpre_ref[...]) * (1.0 + _wide_mod(mod_ref, 4)) + _wide_mod(mod_ref, 3)).astype(BF16)
    fx = _swiglu_acc(hb, wg_ref, wu_ref, wd_ref)
    xo_ref[0] = x + _wide_mod(mod_ref, 5) * _rms(fx, gpost_ref[...])


def _single_buffered(shape):
    nd = len(shape)
    return pl.BlockSpec(shape, lambda *_: (0,) * nd, pipeline_mode=pl.Buffered(1))


def _ffn(xc, modtab, g_pre, g_post, wg, wu, wd):
    B, N, _ = xc.shape
    wide = pl.BlockSpec((1, TW, D), lambda b, j: (b, j, 0))
    return pl.pallas_call(
        _ffn_kernel,
        out_shape=jax.ShapeDtypeStruct((B, N, D), F32),
        grid=(B, N // TW),
        in_specs=[wide, _wide_mod_spec(), _const_spec((1, D)), _const_spec((1, D)),
                  _single_buffered((D, D_FF)), _single_buffered((D, D_FF)), _single_buffered((D_FF, D))],
        out_specs=wide,
        compiler_params=_cparams(("parallel", "parallel")),
        name="ffn",
    )(xc, modtab, g_pre, g_post, wg, wu, wd)


MOE_TB = 1024
MOE_SEG = 32
MOE_TM = 512
MOE_R = 2 * MOE_TB + N_EXPERTS * MOE_SEG
MOE_NP = MOE_R // MOE_SEG


def _moe_route_kernel(x_ref, mod_ref, gpre_ref, router_ref, h_ref, info_ref, infot_ref, cnt_ref):
    mod = mod_ref[0, 0]
    h = _rms(x_ref[0], gpre_ref[...]) * (1.0 + mod[4:5]) + mod[3:4]
    hb = h.astype(BF16)
    h_ref[0] = hb
    lane = lax.broadcasted_iota(jnp.int32, (MOE_TB, LANES), 1)
    h_lo = (h - hb.astype(F32)).astype(BF16)
    logits = (jnp.dot(hb, router_ref[0], preferred_element_type=F32)
              + jnp.dot(hb, router_ref[1], preferred_element_type=F32)
              + jnp.dot(h_lo, router_ref[0], preferred_element_type=F32))
    logits = jnp.where(lane < N_EXPERTS, logits, -jnp.inf)
    v1 = jnp.max(logits, axis=-1, keepdims=True)
    i1 = jnp.min(jnp.where(logits == v1, lane, LANES), axis=-1, keepdims=True)
    rest = jnp.where(lane == i1, -jnp.inf, logits)
    v2 = jnp.max(rest, axis=-1, keepdims=True)
    i2 = jnp.min(jnp.where(rest == v2, lane, LANES), axis=-1, keepdims=True)
    ex = jnp.exp(v2 - v1)
    w1 = 1.0 / (1.0 + ex)
    w2 = ex * w1
    e1 = jnp.where(lane == i1, 1.0, 0.0)
    e2 = jnp.where(lane == i2, 1.0, 0.0)
    es = e1 + e2
    t = lax.broadcasted_iota(jnp.int32, (MOE_TB, MOE_TB), 0)
    s = lax.broadcasted_iota(jnp.int32, (MOE_TB, MOE_TB), 1)
    before = jnp.where(s < t, 1.0, 0.0).astype(BF16)
    rank = jnp.dot(before, es.astype(BF16), preferred_element_type=F32)
    cnt = jnp.sum(es, axis=0, keepdims=True)
    segs = jnp.floor((cnt + (MOE_SEG - 1)) * (1.0 / MOE_SEG))
    ea = lax.broadcasted_iota(jnp.int32, (LANES, LANES), 0)
    eb = lax.broadcasted_iota(jnp.int32, (LANES, LANES), 1)
    earlier = jnp.where(ea < eb, 1.0, 0.0).astype(BF16)
    start = jnp.dot(jnp.broadcast_to(segs, (SUBLANES, LANES)).astype(BF16), earlier,
                    preferred_element_type=F32)[0:1] * MOE_SEG
    pos = rank + start
    d1 = jnp.sum(e1 * pos, axis=-1, keepdims=True)
    d2 = jnp.sum(e2 * pos, axis=-1, keepdims=True)
    info = jnp.where(lane == 0, d1, jnp.where(lane == 1, d2, jnp.where(lane == 2, w1, jnp.where(lane == 3, w2, 0.0))))
    info_ref[0] = info
    infot_ref[0] = jnp.transpose(info)[0:SUBLANES]
    cnt_ref[0] = jnp.broadcast_to(cnt, (SUBLANES, LANES))


def _moe_route(xs, modtab, g_pre, router):
    B, S, _ = xs.shape
    per = S // MOE_TB
    nb = B * per
    blk = lambda c: pl.BlockSpec((1, MOE_TB, c), lambda i: (i // per, i % per, 0))
    flat = lambda r, c: pl.BlockSpec((1, r, c), lambda i: (i, 0, 0))
    return pl.pallas_call(
        _moe_route_kernel,
        out_shape=[jax.ShapeDtypeStruct((nb, MOE_TB, D), BF16), jax.ShapeDtypeStruct((nb, MOE_TB, LANES), F32),
                   jax.ShapeDtypeStruct((nb, SUBLANES, MOE_TB), F32),
                   jax.ShapeDtypeStruct((nb, SUBLANES, LANES), F32)],
        grid=(nb,),
        in_specs=[blk(D), pl.BlockSpec((1, 1, 6, D), lambda i: (i // per, 1, 0, 0)),
                  _const_spec((1, D)), _const_spec((2, D, LANES))],
        out_specs=[flat(MOE_TB, D), flat(MOE_TB, LANES), flat(SUBLANES, MOE_TB), flat(SUBLANES, LANES)],
        compiler_params=_cparams(("parallel",)),
        name="moe_route",
    )(xs, modtab, g_pre, router)


def _moe_plan(cnt, n_tiles):
    pc = (cnt + MOE_SEG - 1) // MOE_SEG * MOE_SEG
    inc = jnp.cumsum(pc, axis=1)
    loff = inc - pc
    reg = (jnp.sum(pc, axis=0) + MOE_TM - 1) // MOE_TM * MOE_TM
    gend = jnp.cumsum(reg)
    goff = (gend - reg)[None, :] + jnp.cumsum(pc, axis=0) - pc
    rows = jnp.arange(MOE_NP, dtype=jnp.int32) * MOE_SEG
    e_p = jnp.sum((rows[None, :, None] >= inc[:, None, :]).astype(jnp.int32), axis=-1)
    e_c = jnp.minimum(e_p, N_EXPERTS - 1)
    pick = (e_c[:, :, None] == jnp.arange(N_EXPERTS)[None, None, :]).astype(jnp.int32)
    dst = jnp.sum(pick * (goff - loff)[:, None, :], axis=-1) + rows[None, :]
    dst = jnp.where(e_p < N_EXPERTS, dst, 0).astype(jnp.int32)
    n_valid = (inc[:, -1] // MOE_SEG).astype(jnp.int32)
    trow = jnp.arange(n_tiles, dtype=jnp.int32) * MOE_TM
    te = jnp.sum((trow[:, None] >= gend[None, :]).astype(jnp.int32), axis=-1)
    valid = te < N_EXPERTS
    last = gend[-1] // MOE_TM - 1
    te = jnp.where(valid, te, te[last]).astype(jnp.int32)
    src = jnp.where(valid, jnp.arange(n_tiles, dtype=jnp.int32), last).astype(jnp.int32)
    fresh = valid & jnp.concatenate([jnp.ones((1,), bool), te[1:] != te[:-1]])
    return dst, n_valid, te, src, valid.astype(jnp.int32), fresh.astype(jnp.int32)


def _piece_copy(src_ref, src_row, dst_ref, dst_row, sem):
    return pltpu.make_async_copy(src_ref.at[pl.ds(src_row, MOE_SEG)], dst_ref.at[pl.ds(dst_row, MOE_SEG)], sem)


def _moe_gather_kernel(dst_ref, nv_ref, h_ref, infot_ref, xg_in_ref, xg_ref, buf_ref, sem_ref):
    del xg_in_ref
    i = pl.program_id(0)
    it = infot_ref[0]
    rr = lax.broadcasted_iota(jnp.int32, (MOE_R, MOE_TB), 0).astype(F32)
    onehot = jnp.where(rr == it[0:1], 1.0, jnp.where(rr == it[1:2], 1.0, 0.0)).astype(BF16)
    buf_ref[...] = jnp.dot(onehot, h_ref[0], preferred_element_type=F32).astype(BF16)
    nv = nv_ref[i]

    def piece(p):
        return _piece_copy(buf_ref, pl.multiple_of(p * MOE_SEG, MOE_SEG),
                           xg_ref, pl.multiple_of(dst_ref[i, p], MOE_SEG), sem_ref.at[p])

    def start(p, c):
        piece(p).start()
        return c

    def wait(p, c):
        piece(p).wait()
        return c

    lax.fori_loop(0, nv, start, 0)
    lax.fori_loop(0, nv, wait, 0)


def _moe_gather(dst, n_valid, h, info_t, n_rows):
    nb = h.shape[0]
    flat = lambda r, c: pl.BlockSpec((1, r, c), lambda i, *_: (i, 0, 0))
    return pl.pallas_call(
        _moe_gather_kernel,
        out_shape=jax.ShapeDtypeStruct((n_rows, D), BF16),
        grid_spec=pltpu.PrefetchScalarGridSpec(
            num_scalar_prefetch=2, grid=(nb,),
            in_specs=[flat(MOE_TB, D), flat(SUBLANES, MOE_TB), pl.BlockSpec(memory_space=pl.ANY)],
            out_specs=pl.BlockSpec(memory_space=pl.ANY),
            scratch_shapes=[pltpu.VMEM((MOE_R, D), BF16), pltpu.SemaphoreType.DMA((MOE_NP,))]),
        input_output_aliases={4: 0},
        compiler_params=_cparams(("arbitrary",)),
        name="moe_gather",
    )(dst, n_valid, h, info_t, jnp.zeros((n_rows, D), BF16))


N_FCH = D_FF // FCH


def _moe_ffn_kernel(te_ref, src_ref, valid_ref, fresh_ref, x_ref, wg_hbm, wu_hbm, wd_hbm, y_ref,
                    wg_ref, wu_ref, wd_ref, sem_ref):
    i = pl.program_id(0)
    e = te_ref[i]
    fresh = fresh_ref[i] == 1

    def slice_copies(c):
        cols = pl.ds(c * FCH, FCH)
        return (pltpu.make_async_copy(wg_hbm.at[e, :, cols], wg_ref.at[:, cols], sem_ref.at[0, c]),
                pltpu.make_async_copy(wu_hbm.at[e, :, cols], wu_ref.at[:, cols], sem_ref.at[1, c]),
                pltpu.make_async_copy(wd_hbm.at[e, cols, :], wd_ref.at[cols, :], sem_ref.at[2, c]))

    @pl.when(fresh)
    def _():
        for c in range(N_FCH):
            for cp in slice_copies(c):
                cp.start()
        hb = x_ref[...]
        acc = jnp.zeros((MOE_TM, D), F32)
        for c in range(N_FCH):
            for cp in slice_copies(c):
                cp.wait()
            sl = slice(c * FCH, (c + 1) * FCH)
            gate = jnp.dot(hb, wg_ref[:, sl].astype(BF16), preferred_element_type=F32)
            up = jnp.dot(hb, wu_ref[:, sl].astype(BF16), preferred_element_type=F32)
            act = (gate * _sigmoid(gate) * up).astype(BF16)
            acc = acc + jnp.dot(act, wd_ref[sl, :].astype(BF16), preferred_element_type=F32)
        y_ref[...] = acc.astype(BF16)

    @pl.when(jnp.logical_and(valid_ref[i] == 1, jnp.logical_not(fresh)))
    def _():
        y_ref[...] = _swiglu_acc(x_ref[...], wg_ref, wu_ref, wd_ref).astype(BF16)

    @pl.when(valid_ref[i] == 0)
    def _():
        y_ref[...] = jnp.zeros_like(y_ref)


def _moe_ffn(te, src, valid, fresh, xg, wg, wu, wd):
    n_rows = xg.shape[0]
    hbm = pl.BlockSpec(memory_space=pl.ANY)
    return pl.pallas_call(
        _moe_ffn_kernel,
        out_shape=jax.ShapeDtypeStruct((n_rows, D), BF16),
        grid_spec=pltpu.PrefetchScalarGridSpec(
            num_scalar_prefetch=4, grid=(n_rows // MOE_TM,),
            in_specs=[pl.BlockSpec((MOE_TM, D), lambda i, te, src, *_: (src[i], 0)), hbm, hbm, hbm],
            out_specs=pl.BlockSpec((MOE_TM, D), lambda i, *_: (i, 0)),
            scratch_shapes=[pltpu.VMEM((D, D_FF), F32), pltpu.VMEM((D, D_FF), F32), pltpu.VMEM((D_FF, D), F32),
                            pltpu.SemaphoreType.DMA((3, N_FCH))]),
        compiler_params=_cparams(("arbitrary",)),
        name="moe_ffn",
    )(te, src, valid, fresh, xg, wg, wu, wd)


def _moe_combine_kernel(dst_ref, nv_ref, x_ref, mod_ref, gpost_ref, info_ref, yg_ref, xo_ref, buf_ref, sem_ref):
    i = pl.program_id(0)
    nv = nv_ref[i]

    def piece(p):
        return _piece_copy(yg_ref, pl.multiple_of(dst_ref[i, p], MOE_SEG),
                           buf_ref, pl.multiple_of(p * MOE_SEG, MOE_SEG), sem_ref.at[p])

    def start(p, c):
        piece(p).start()
        return c

    def clear(p, c):
        buf_ref[pl.ds(pl.multiple_of(p * MOE_SEG, MOE_SEG), MOE_SEG), :] = jnp.zeros((MOE_SEG, D), BF16)
        return c

    def wait(p, c):
        piece(p).wait()
        return c

    lax.fori_loop(0, nv, start, 0)
    lax.fori_loop(nv, MOE_NP, clear, 0)
    info = info_ref[0]
    rr = lax.broadcasted_iota(jnp.int32, (MOE_TB, MOE_R), 1).astype(F32)
    comb = jnp.where(rr == info[:, 0:1], info[:, 2:3], jnp.where(rr == info[:, 1:2], info[:, 3:4], 0.0)).astype(BF16)
    lax.fori_loop(0, nv, wait, 0)
    fx = jnp.dot(comb, buf_ref[...], preferred_element_type=F32)
    xo_ref[0] = x_ref[0] + mod_ref[0, 0][5:6] * _rms(fx, gpost_ref[...])


def _moe_combine(dst, n_valid, xs, modtab, g_post, info, yg):
    B, S, _ = xs.shape
    per = S // MOE_TB
    blk = pl.BlockSpec((1, MOE_TB, D), lambda i, *_: (i // per, i % per, 0))
    return pl.pallas_call(
        _moe_combine_kernel,
        out_shape=jax.ShapeDtypeStruct((B, S, D), F32),
        grid_spec=pltpu.PrefetchScalarGridSpec(
            num_scalar_prefetch=2, grid=(B * per,),
            in_specs=[blk, pl.BlockSpec((1, 1, 6, D), lambda i, *_: (i // per, 1, 0, 0)),
                      pl.BlockSpec((1, D), lambda i, *_: (0, 0)),
                      pl.BlockSpec((1, MOE_TB, LANES), lambda i, *_: (i, 0, 0)),
                      pl.BlockSpec(memory_space=pl.ANY)],
            out_specs=blk,
            scratch_shapes=[pltpu.VMEM((MOE_R, D), BF16), pltpu.SemaphoreType.DMA((MOE_NP,))]),
        compiler_params=_cparams(("arbitrary",)),
        name="moe_combine",
    )(dst, n_valid, xs, modtab, g_post, info, yg)


def _moe(xs, modtab, g_pre, g_post, router, wg, wu, wd):
    B, S, _ = xs.shape
    nb = B * S // MOE_TB
    h, info, info_t, cnt = _moe_route(xs, modtab, g_pre, router)
    worst = 2 * B * S + nb * N_EXPERTS * (MOE_SEG - 1) + N_EXPERTS * (MOE_TM - 1)
    n_tiles = -(-worst // MOE_TM)
    dst, n_valid, te, src, valid, fresh = _moe_plan(cnt[:, 0, :N_EXPERTS].astype(jnp.int32), n_tiles)
    xg = _moe_gather(dst, n_valid, h, info_t, n_tiles * MOE_TM)
    yg = _moe_ffn(te, src, valid, fresh, xg, wg, wu, wd)
    return _moe_combine(dst, n_valid, xs, modtab, g_post, info, yg)


def _block_diag2(w):
    z = jnp.zeros_like(w[0])
    return jnp.concatenate([jnp.concatenate([w[0], z], axis=1), jnp.concatenate([z, w[1]], axis=1)], axis=0)


def _row(v):
    return v.reshape(1, -1).astype(F32)


def _head_ones(width, head):
    i = jnp.arange(width) // head
    return (i[:, None] == i[None, :]).astype(BF16)


def kernel(x, c, ctx, c_ctx, ada_w, ada_b, norm_mix_pre, norm_mix_post, norm_ffn_pre, norm_ffn_post, w_in, shift_mu, rw_w_up, rw_w0, rw_a_up, rw_a0, rw_k_k, rw_k_a, rw_r_k, rw_g_up, rw_gn_w, rw_gn_b, rw_v_down, rw_v_up, rw_v0, gla_conv, gla_a_up, gla_a_b, gla_gn_w, w_out, ffn_w_gate, ffn_w_up, ffn_w_down, moe_router, moe_w_gate, moe_w_up, moe_w_down):
    B, S, _ = x.shape
    n_ctx = ctx.shape[1]
    depth = w_in.shape[0]
    assert n_ctx == TB and S % MOE_TB == 0 and (n_ctx + S) % TW == 0 and depth == 2

    xc = None
    pad_rows = -(B + 1) % SUBLANES
    cvec = jnp.concatenate([c, c_ctx[None, :], jnp.zeros((pad_rows, D), F32)], axis=0)
    bd64 = _head_ones(RW_WIDTH, RW_HEAD)
    ada_b3 = ada_b.reshape(depth, 1, 6 * D)
    v_first = None
    out = None
    for i in range(depth):
        last = i == depth - 1
        mods = _adaln(cvec, ada_w, ada_b3, i)
        mod_x = mods[:B].reshape(B, 6, D)
        mod_c = jnp.broadcast_to(mods[B].reshape(1, 6, D), (B, 6, D))
        modtab = jnp.stack([mod_c, mod_x], axis=1)

        p_rw, p_gl = _inproj(ctx, x, xc, _row(norm_mix_pre[i]), modtab, w_in, i)

        prm = dict(
            mu=_row(shift_mu[i]),
            w_up=_block_diag2(rw_w_up[i]).astype(BF16), w0=_row(rw_w0[i]),
            a_up=_block_diag2(rw_a_up[i]).astype(BF16), a0=_row(rw_a0[i]),
            k_k=_row(rw_k_k[i]), k_a=_row(rw_k_a[i]), r_k=_row(rw_r_k[i]),
            g_up=rw_g_up[i].astype(BF16), bd64=bd64,
            gn_w=_row(rw_gn_w[i]), gn_b=_row(rw_gn_b[i]), gla_gn_w=_row(gla_gn_w[i]),
            w_out=w_out[i].astype(BF16), norm_post=_row(norm_mix_post[i]),
        )
        gate_pad = jnp.zeros((LANES - 2 * GLA_GATE_RANK, 2 * GLA_KW), F32)
        gla_prm = dict(conv=gla_conv[i].astype(F32), a_b=_row(gla_a_b[i]),
                       a_up=jnp.concatenate([_block_diag2(gla_a_up[i]), gate_pad], axis=0).astype(BF16))
        if i > 0:
            pad = LANES - RW_V_RANK
            prm["v_down"] = jnp.concatenate([rw_v_down[i - 1], jnp.zeros((RW_WIDTH, pad), F32)], axis=1).astype(BF16)
            prm["v_up"] = jnp.concatenate([rw_v_up[i - 1], jnp.zeros((pad, RW_WIDTH), F32)], axis=0).astype(BF16)
            prm["v0"] = _row(rw_v0[i - 1])

        r, kk, vm, g, bonus, lw, bb, ke = _rwprep(p_rw, prm, v_first if i > 0 else None)
        if i == 0:
            v_first = vm
        q, k, gv, og, lg = _glaprep(p_gl, gla_prm)
        y, o = _scans(r, kk, vm, lw, bb, ke, q, k, gv, lg)
        xc = _readout(y, bonus, g, o, og, ctx, x, xc, modtab, prm, latents_only=last)

        jf = i // 2
        if i % 2 == 0:
            xc = _ffn(xc, modtab, _row(norm_ffn_pre[i]), _row(norm_ffn_post[i]),
                      ffn_w_gate[jf], ffn_w_up[jf], ffn_w_down[jf])
        else:
            router = jnp.concatenate([moe_router[jf], jnp.zeros((D, LANES - N_EXPERTS), F32)], axis=1)
            r_hi = router.astype(BF16)
            router = jnp.stack([r_hi, (router - r_hi.astype(F32)).astype(BF16)])
            out = _moe(xc, modtab, _row(norm_ffn_pre[i]), _row(norm_ffn_post[i]), router,
                       moe_w_gate[jf], moe_w_up[jf], moe_w_down[jf])
    return out
```

```python
import functools
import math

import jax
import jax.numpy as jnp
from jax import lax
from jax.experimental import pallas as pl
from jax.experimental.pallas import tpu as pltpu

F32, BF16 = jnp.float32, jnp.bfloat16
ACT = BF16

D = 1024
GRID_W = 64
RW_WIDTH = 512
RW_HEAD = 64
RW_RANK = 64
RW_G_RANK = 128
RW_V_RANK = 32
RW_GN_EPS = 64e-5
GLA_WIDTH = 512
GLA_HEADS = 4
GLA_DV = 128
GLA_DK = 64
GLA_KW = 256
GLA_GATE_RANK = 16
GLA_GATE_NORM = 16.0
GLA_NORM_EPS = 1e-5
D_FF = 2816
N_EXPERTS = 8
NORM_EPS = 1e-6
RW_COLS = 3 * RW_WIDTH + 4 * RW_RANK + RW_G_RANK
GLA_QKV = 2 * GLA_KW + GLA_WIDTH
GLA_COLS = GLA_QKV + GLA_WIDTH + 2 * GLA_GATE_RANK
GLA_PAD = 1664

LANES = 128
SUBLANES = 8
TB = 256
TW = 3 * TB
CH = 64
FCH = 256
HALO = 16
VMEM_LIMIT = 56 * 1024 * 1024


def _cparams(sem):
    return pltpu.CompilerParams(dimension_semantics=sem, vmem_limit_bytes=VMEM_LIMIT)


def _bdot(a, b):
    return jnp.dot(a.astype(BF16), b.astype(BF16), preferred_element_type=F32)


def _bdot_nt(a, b):
    return lax.dot_general(a.astype(BF16), b.astype(BF16), (((1,), (1,)), ((), ())),
                           preferred_element_type=F32)


def _bdot_tn(a, b):
    return lax.dot_general(a.astype(BF16), b.astype(BF16), (((0,), (0,)), ((), ())),
                           preferred_element_type=F32)


def _split_dot(a_exact, x):
    h1 = x.astype(BF16)
    r1 = x - h1.astype(F32)
    h2 = r1.astype(BF16)
    h3 = (r1 - h2.astype(F32)).astype(BF16)
    return (jnp.dot(a_exact, h1, preferred_element_type=F32)
            + jnp.dot(a_exact, h2, preferred_element_type=F32)
            + jnp.dot(a_exact, h3, preferred_element_type=F32))


def _seg_sum(x, bd, exact=True):
    hi = x.astype(BF16)
    out = jnp.dot(hi, bd, preferred_element_type=F32)
    if exact:
        lo = (x - hi.astype(F32)).astype(BF16)
        out = out + jnp.dot(lo, bd, preferred_element_type=F32)
    return out


def _sigmoid(x):
    return jax.nn.sigmoid(x)


def _rms(x, g):
    return x * lax.rsqrt(jnp.mean(x * x, axis=-1, keepdims=True) + NORM_EPS) * g


def _pair_stack(z, lo):
    return jnp.concatenate([jnp.where(lo, z, 0.0), jnp.where(lo, 0.0, z)], axis=0)


def _adaln_kernel(c_ref, w_ref, b_ref, o_ref):
    c = c_ref[...]
    s = c * _sigmoid(c)
    o_ref[...] = jnp.dot(s, w_ref[0], precision=lax.Precision.HIGHEST,
                         preferred_element_type=F32) + b_ref[0]


def _adaln(cvec, w, b, layer):
    rows = cvec.shape[0]
    n = w.shape[2]
    return pl.pallas_call(
        _adaln_kernel,
        out_shape=jax.ShapeDtypeStruct((rows, n), F32),
        grid=(n // D,),
        in_specs=[pl.BlockSpec((rows, D), lambda i: (0, 0)),
                  pl.BlockSpec((1, D, D), lambda i: (layer, 0, i)),
                  pl.BlockSpec((1, 1, D), lambda i: (layer, 0, i))],
        out_specs=pl.BlockSpec((rows, D), lambda i: (0, i)),
        compiler_params=_cparams(("arbitrary",)),
        name="adaln",
    )(cvec, w, b)


def _wide_mod(mod_ref, k):
    is_ctx = jnp.logical_and(pl.program_id(1) == 0, lax.broadcasted_iota(jnp.int32, (TW, 1), 0) < TB)
    return jnp.where(is_ctx, mod_ref[0, 0, k:k + 1], mod_ref[0, 1, k:k + 1])


def _wide_rows(refs):
    if len(refs) == 1:
        return refs[0][0]
    first = jnp.where(pl.program_id(1) == 0, refs[0][0], refs[1][0])
    return jnp.concatenate([first] + [r[0] for r in refs[2:]], axis=0)


def _inproj_kernel(n_src, *refs):
    g_ref, mod_ref, w_ref, prw_ref, pgl_ref = refs[n_src:]
    x = _wide_rows(refs[:n_src])
    h = _rms(x, g_ref[...]) * (1.0 + _wide_mod(mod_ref, 1)) + _wide_mod(mod_ref, 0)
    hb = h.astype(BF16)
    prw_ref[0] = jnp.dot(hb, w_ref[0, :, :RW_COLS].astype(BF16), preferred_element_type=F32).astype(ACT)
    pgl_ref[0, :, :GLA_COLS] = jnp.dot(hb, w_ref[0, :, RW_COLS:].astype(BF16),
                                       preferred_element_type=F32).astype(ACT)
    pgl_ref[0, :, GLA_COLS:] = jnp.zeros((TW, GLA_PAD - GLA_COLS), ACT)


def _wide_src(ctx, x, xc):
    if xc is not None:
        return [xc], [pl.BlockSpec((1, TW, D), lambda b, j: (b, j, 0))]
    per = TW // TB
    specs = [pl.BlockSpec((1, TB, D), lambda b, j: (b, 0, 0))]
    for k in range(per):
        specs.append(pl.BlockSpec((1, TB, D), lambda b, j, k=k: (b, jnp.maximum(j * per + k - 1, 0), 0)))
    return [ctx] + [x] * per, specs


def _wide_mod_spec():
    return pl.BlockSpec((1, 2, 6, D), lambda b, j: (b, 0, 0, 0))


def _const_spec(shape):
    nd = len(shape)
    return pl.BlockSpec(shape, lambda *_: (0,) * nd)


def _tile_spec(c):
    return pl.BlockSpec((1, TB, c), lambda b, j: (b, j, 0))


def _dir_tile_spec(c):
    return pl.BlockSpec((2, 1, TB, c), lambda b, j: (0, b, j, 0))


def _inproj(ctx, x, xc, g, modtab, w, layer):
    B = modtab.shape[0]
    N = xc.shape[1] if xc is not None else ctx.shape[1] + x.shape[1]
    srcs, src_specs = _wide_src(ctx, x, xc)
    wide = lambda c: pl.BlockSpec((1, TW, c), lambda b, j: (b, j, 0))
    return pl.pallas_call(
        functools.partial(_inproj_kernel, len(srcs)),
        out_shape=[jax.ShapeDtypeStruct((B, N, RW_COLS), ACT),
                   jax.ShapeDtypeStruct((B, N, GLA_PAD), ACT)],
        grid=(B, N // TW),
        in_specs=src_specs + [_const_spec((1, D)), _wide_mod_spec(),
                              pl.BlockSpec((1, D, RW_COLS + GLA_COLS), lambda b, j: (layer, 0, 0),
                                           pipeline_mode=pl.Buffered(1))],
        out_specs=[wide(RW_COLS), wide(GLA_PAD)],
        compiler_params=_cparams(("parallel", "parallel")),
        name="inproj",
    )(*srcs, g, modtab, w)


def _shift_table():
    t = jnp.arange(TB)
    prev = t[:, None] - 1 == t[None, :]
    nxt = t[:, None] + 1 == t[None, :]
    col = (t % GRID_W)[:, None]
    return jnp.stack([jnp.stack([prev, nxt]),
                      jnp.stack([prev & (col != 0), nxt & (col != GRID_W - 1)])]).astype(BF16)


def _rwprep_kernel(has_vres, n_tiles, p_ref, hu_ref, hd_ref, sh_ref, mu_ref, wup_ref, w0_ref, aup_ref,
                   a0_ref, kk_ref, ka_ref, rk_ref, gup_ref, bd_ref, *rest):
    if has_vres:
        vf_ref, vdn_ref, vup_ref, v0_ref = rest[:4]
        rest = rest[4:]
    r_out, kk_out, v_out, g_out, bon_out, lw_out, b_out, ke_out = rest
    j = pl.program_id(1)
    pb = p_ref[0]
    p = pb.astype(F32)
    prev1 = jnp.dot(sh_ref[0, 0], pb, preferred_element_type=F32)
    next1 = jnp.dot(sh_ref[0, 1], pb, preferred_element_type=F32)
    has_upper = jnp.where(j == 1, 0.0, 1.0)
    has_lower = jnp.where(j == n_tiles - 1, 0.0, 1.0)
    up = jnp.concatenate([hu_ref[0].astype(F32) * has_upper, p[:TB - GRID_W]], axis=0)
    down = jnp.concatenate([p[GRID_W:], hd_ref[0].astype(F32) * has_lower], axis=0)
    cls = lax.broadcasted_iota(jnp.int32, p.shape, 1) & jnp.where(j == 0, 1, 3)
    shifted = jnp.where(cls == 0, prev1, jnp.where(cls == 1, next1, jnp.where(cls == 2, up, down)))
    u = p + mu_ref[...] * (shifted - p)

    r = u[:, 0:RW_WIDTH]
    k = u[:, RW_WIDTH:2 * RW_WIDTH]
    v = u[:, 2 * RW_WIDTH:3 * RW_WIDTH]
    o = 3 * RW_WIDTH
    wd = u[:, o:o + 2 * RW_RANK]
    ad = u[:, o + 2 * RW_RANK:o + 4 * RW_RANK]
    gd = u[:, o + 4 * RW_RANK:]

    w_logit = w0_ref[...] + _bdot(jnp.tanh(wd), wup_ref[...])
    lw = -math.exp(-0.5) * _sigmoid(w_logit)
    a = _sigmoid(a0_ref[...] + _bdot(ad, aup_ref[...]))
    bd = bd_ref[...]
    kk = k * kk_ref[...]
    kk = kk * lax.rsqrt(jnp.maximum(_seg_sum(kk * kk, bd), 1e-24))
    g = _bdot(_sigmoid(gd), gup_ref[...])
    if has_vres:
        gate = _sigmoid(v0_ref[...] + _bdot(_bdot(v, vdn_ref[...]), vup_ref[...]))
        vm = v + (vf_ref[0].astype(F32) - v) * gate
    else:
        vm = v
    ke_sum = jnp.zeros_like(k)
    for d in range(2):
        a_d = a[:, d * RW_WIDTH:(d + 1) * RW_WIDTH]
        ke_d = k * (1.0 + (a_d - 1.0) * ka_ref[...])
        lw_out[d, 0] = lw[:, d * RW_WIDTH:(d + 1) * RW_WIDTH]
        b_out[d, 0] = (kk * a_d).astype(ACT)
        ke_out[d, 0] = ke_d.astype(ACT)
        ke_sum = ke_sum + ke_d
    r_out[0] = r.astype(ACT)
    kk_out[0] = kk.astype(ACT)
    v_out[0] = vm.astype(ACT)
    g_out[0] = g.astype(ACT)
    bon_out[0] = (_seg_sum(r * ke_sum * rk_ref[...], bd, exact=False) * vm).astype(ACT)


def _rwprep_call(p_rw, prm, v_first):
    B, N, _ = p_rw.shape
    hb = TB // GRID_W
    nhb = N // GRID_W
    has_vres = v_first is not None
    W2 = 2 * RW_WIDTH
    in_specs = [
        _tile_spec(RW_COLS),
        pl.BlockSpec((1, GRID_W, RW_COLS), lambda b, j: (b, jnp.maximum(j * hb - 1, 0), 0)),
        pl.BlockSpec((1, GRID_W, RW_COLS), lambda b, j: (b, jnp.minimum(j * hb + hb, nhb - 1), 0)),
        pl.BlockSpec((1, 2, TB, TB), lambda b, j: (jnp.minimum(j, 1), 0, 0, 0)),
        _const_spec((1, RW_COLS)), _const_spec((2 * RW_RANK, W2)), _const_spec((1, W2)),
        _const_spec((2 * RW_RANK, W2)), _const_spec((1, W2)),
        _const_spec((1, RW_WIDTH)), _const_spec((1, RW_WIDTH)), _const_spec((1, RW_WIDTH)),
        _const_spec((RW_G_RANK, RW_WIDTH)), _const_spec((RW_WIDTH, RW_WIDTH)),
    ]
    args = [p_rw, p_rw, p_rw, _shift_table(), prm["mu"], prm["w_up"], prm["w0"], prm["a_up"], prm["a0"],
            prm["k_k"], prm["k_a"], prm["r_k"], prm["g_up"], prm["bd64"]]
    if has_vres:
        in_specs += [_tile_spec(RW_WIDTH), _const_spec((RW_WIDTH, LANES)), _const_spec((LANES, RW_WIDTH)),
                     _const_spec((1, RW_WIDTH))]
        args += [v_first, prm["v_down"], prm["v_up"], prm["v0"]]
    tok = jax.ShapeDtypeStruct((B, N, RW_WIDTH), ACT)
    dtok = jax.ShapeDtypeStruct((2, B, N, RW_WIDTH), ACT)
    out_shapes = [tok] * 5 + [jax.ShapeDtypeStruct((2, B, N, RW_WIDTH), F32), dtok, dtok]
    out_specs = [_tile_spec(RW_WIDTH)] * 5 + [_dir_tile_spec(RW_WIDTH)] * 3
    return args, in_specs, out_shapes, out_specs


NCH = TB // CH


def _scan_kernel(rf_ref, kkf_ref, vf_ref, lwf_ref, bf_ref, kef_ref,
                 rb_ref, kkb_ref, vb_ref, lwb_ref, bb_ref, keb_ref,
                 gqf_ref, gkf_ref, gvf_ref, lgf_ref, gqb_ref, gkb_ref, gvb_ref, lgb_ref,
                 yf_ref, yb_ref, of_ref, ob_ref, st_ref, gst_ref):
    j = pl.program_id(1)

    @pl.when(j == 0)
    def _():
        st_ref[...] = jnp.zeros_like(st_ref)
        gst_ref[...] = jnp.zeros_like(gst_ref)

    ti = lax.broadcasted_iota(jnp.int32, (CH, LANES), 0)
    li = lax.broadcasted_iota(jnp.int32, (CH, LANES), 1)
    si = li & (CH - 1)
    lo = li < CH
    eye = jnp.where(ti == si, 1.0, 0.0)
    rblk = lax.broadcasted_iota(jnp.int32, (LANES, LANES), 0) // CH
    cblk = lax.broadcasted_iota(jnp.int32, (LANES, LANES), 1) // CH
    bdmask = rblk == cblk
    t64 = lax.broadcasted_iota(jnp.int32, (CH, CH), 0)
    s64 = lax.broadcasted_iota(jnp.int32, (CH, CH), 1)
    tri = [jnp.where(s64 <= t64, 1.0, 0.0).astype(BF16), jnp.where(s64 >= t64, 1.0, 0.0).astype(BF16)]
    m_incl = [si <= ti, si >= ti]
    m_strict = [si < ti, si > ti]
    n_pairs = RW_WIDTH // LANES
    in_refs = ((rf_ref, kkf_ref, vf_ref, lwf_ref, bf_ref, kef_ref),
               (rb_ref, kkb_ref, vb_ref, lwb_ref, bb_ref, keb_ref))
    y_refs = (yf_ref, yb_ref)

    def stack(z):
        return _pair_stack(z, lo).astype(BF16)

    st = {(d, p): st_ref[d, p] for d in range(2) for p in range(n_pairs)}

    def scan_step(step):
        cur = {}
        w_tot = {}
        for d in range(2):
            c = (NCH - 1 - step) if d == 1 else step
            rows = slice(c * CH, (c + 1) * CH)
            refs = in_refs[d]
            r, kk, v = (refs[i][0, rows, :].astype(F32) for i in range(3))
            lw = refs[3][0, 0, rows, :]
            b, ke = refs[4][0, 0, rows, :].astype(F32), refs[5][0, 0, rows, :].astype(F32)
            cum = _split_dot(tri[d], lw)
            tot = jnp.sum(lw, axis=0, keepdims=True)
            w_inv = jnp.exp(-cum)
            w_end = jnp.exp(tot - cum)
            w_tot[d] = jnp.exp(tot)
            rh = r * jnp.exp(cum)
            ah = -(kk * jnp.exp(cum - lw))
            bh, kh, bt, kt = b * w_inv, ke * w_inv, b * w_end, ke * w_end
            for p in range(n_pairs):
                sl = slice(LANES * p, LANES * (p + 1))
                cur[d, p] = dict(ah=ah[:, sl], rh=rh[:, sl], bh=bh[:, sl], kh=kh[:, sl], bt=bt[:, sl],
                                 kt=kt[:, sl], v=v[:, sl], rows=rows, sl=sl)
        yield
        for it in cur.values():
            it["lhs"] = jnp.concatenate([it["ah"], it["rh"]], axis=0).astype(BF16)
            rhs = jnp.concatenate([stack(it["bh"]), stack(it["kh"])], axis=0)
            it["a_all"] = _bdot_nt(it["lhs"], rhs)
        yield
        for (d, _), it in cur.items():
            a_all = it.pop("a_all")
            it["a_ab"] = jnp.where(m_strict[d], a_all[:CH, :LANES], 0.0)
            a_ak = jnp.where(m_strict[d], a_all[:CH, LANES:], 0.0)
            a_rb = jnp.where(m_incl[d], a_all[CH:, :LANES], 0.0)
            a_rk = jnp.where(m_incl[d], a_all[CH:, LANES:], 0.0)
            it["a_r"] = jnp.concatenate([a_rb, a_rk], axis=1).astype(BF16)
            it["v_bd"] = stack(it["v"])
            it["akv"] = _bdot(a_ak, it["v_bd"])
            it["t"] = eye + it["a_ab"]
            it["m"] = _bdot(it["a_ab"], stack(it["a_ab"]))
        yield
        for _ in range(int(math.log2(CH)) - 2):
            for it in cur.values():
                z = _bdot(it["m"], jnp.concatenate([stack(it["m"]), stack(it["t"])], axis=1))
                it["m"] = z[:, :LANES]
                it["t"] = it["t"] + z[:, LANES:]
            yield
        for it in cur.values():
            it["t"] = (it["t"] + _bdot(it["m"], stack(it["t"]))).astype(BF16)
        yield
        for it in cur.values():
            ta = jnp.dot(it["t"], jnp.concatenate([stack(it["ah"]), stack(it["akv"])], axis=1),
                         preferred_element_type=F32)
            it["lhs2"] = jnp.concatenate([ta[:, :LANES].astype(BF16), it["lhs"][CH:]], axis=0)
            it["cc"] = ta[:, LANES:]
            it["rhs_t"] = jnp.concatenate([it["bt"], it["kt"]], axis=0).astype(BF16)
        yield
        z1 = {k: _bdot_nt(it["lhs2"], st[k]) for k, it in cur.items()}
        yield
        u = {k: z1[k][:CH] + it["cc"] for k, it in cur.items()}
        upd = {k: _bdot_tn(jnp.concatenate([u[k], it["v"]], axis=0), it["rhs_t"]) for k, it in cur.items()}
        yield
        for (d, p), it in cur.items():
            y = z1[d, p][CH:] + jnp.dot(it["a_r"], jnp.concatenate([stack(u[d, p]), it["v_bd"]], axis=0),
                                        preferred_element_type=F32)
            y_refs[d][0, it["rows"], it["sl"]] = y.astype(ACT)
            st[d, p] = st[d, p] * w_tot[d][:, it["sl"]] + jnp.where(bdmask, upd[d, p], 0.0)
        yield

    gla_refs = ((gqf_ref, gkf_ref, gvf_ref, lgf_ref), (gqb_ref, gkb_ref, gvb_ref, lgb_ref))
    o_refs = (of_ref, ob_ref)
    g_pairs = GLA_KW // LANES
    gst = {(d, p): gst_ref[d, p] for d in range(2) for p in range(g_pairs)}
    lo2 = lax.broadcasted_iota(jnp.int32, (LANES, LANES), 1) < CH
    zeros_v = jnp.zeros((CH, GLA_DV), F32)

    def gla_step(step):
        cur = {}
        dec = {}
        for d in range(2):
            c = (NCH - 1 - step) if d == 1 else step
            rows = slice(c * CH, (c + 1) * CH)
            refs = gla_refs[d]
            q, k, v = (refs[i][0, rows, :].astype(F32) for i in range(3))
            lg = refs[3][0, 0, rows, :]
            cum = _split_dot(tri[d], lg)
            tot = jnp.sum(lg, axis=0, keepdims=True)
            dec[d] = jnp.exp(tot)
            qd = q * jnp.exp(cum)
            ki = k * jnp.exp(-cum)
            kend = k * jnp.exp(tot - cum)
            for p in range(g_pairs):
                sl = slice(LANES * p, LANES * (p + 1))
                cur[d, p] = dict(qd=qd[:, sl].astype(BF16), ki=ki[:, sl], kend=kend[:, sl], rows=rows, sl=sl,
                                 v0=v[:, 2 * LANES * p:2 * LANES * p + LANES],
                                 v1=v[:, 2 * LANES * p + LANES:2 * LANES * (p + 1)])
        yield
        for it in cur.values():
            it["att"] = _bdot_nt(it["qd"], _pair_stack(it["ki"], lo))
        yield
        for (d, _), it in cur.items():
            att = jnp.where(m_incl[d], it["att"], 0.0)
            v_bd = jnp.concatenate([jnp.concatenate([it["v0"], zeros_v], axis=1),
                                    jnp.concatenate([zeros_v, it["v1"]], axis=1)], axis=0)
            it["o"] = _bdot(att, v_bd)
            it["upd"] = _bdot_tn(jnp.concatenate([it["v0"], it["v1"]], axis=0), _pair_stack(it["kend"], lo))
        yield
        for (d, p), it in cur.items():
            s = gst[d, p]
            s_bd = jnp.concatenate([jnp.where(lo2, s, 0.0), jnp.where(lo2, 0.0, s)], axis=0)
            o = it["o"] + _bdot_nt(it["qd"], s_bd)
            o_refs[d][0, it["rows"], 2 * LANES * p:2 * LANES * (p + 1)] = o.astype(ACT)
            gst[d, p] = s * dec[d][:, it["sl"]] + it["upd"]
        yield

    n_stages = 12
    n_dep = 3
    gla_stages = (1, 4, 7, 10)
    pipeline = [(scan_step(s), gla_step(s)) for s in range(NCH)]
    for slot in range(n_stages + n_dep * (NCH - 1)):
        for s, (rw_gen, gla_gen) in enumerate(pipeline):
            stage = slot - n_dep * s
            if 0 <= stage < n_stages:
                next(rw_gen)
                if stage in gla_stages:
                    next(gla_gen)
    for (d, p), s in st.items():
        st_ref[d, p] = s
    for (d, p), s in gst.items():
        gst_ref[d, p] = s


def _bidir_specs(c, n_tiles):
    def back(j):
        return jnp.where(j == 0, 0, n_tiles - j)

    tok_f = pl.BlockSpec((1, TB, c), lambda b, j: (b, j, 0))
    tok_b = pl.BlockSpec((1, TB, c), lambda b, j: (b, back(j), 0))
    dir_f = pl.BlockSpec((1, 1, TB, c), lambda b, j: (0, b, j, 0))
    dir_b = pl.BlockSpec((1, 1, TB, c), lambda b, j: (1, b, back(j), 0))
    return tok_f, tok_b, dir_f, dir_b


def _scans(r, kk, v, lw, b, ke, q, k, gv, lg):
    B, N, _ = r.shape
    nt = N // TB
    tok_f, tok_b, dir_f, dir_b = _bidir_specs(RW_WIDTH, nt)
    kf, kb, dkf, dkb = _bidir_specs(GLA_KW, nt)
    out = jax.ShapeDtypeStruct((B, N, RW_WIDTH), ACT)
    assert GLA_WIDTH == RW_WIDTH
    yf, yb, of, ob = pl.pallas_call(
        _scan_kernel,
        out_shape=[out] * 4,
        grid=(B, nt),
        in_specs=[tok_f, tok_f, tok_f, dir_f, dir_f, dir_f, tok_b, tok_b, tok_b, dir_b, dir_b, dir_b,
                  kf, kf, tok_f, dkf, kb, kb, tok_b, dkb],
        out_specs=[tok_f, tok_b, tok_f, tok_b],
        scratch_shapes=[pltpu.VMEM((2, RW_WIDTH // LANES, LANES, LANES), F32),
                        pltpu.VMEM((2, GLA_KW // LANES, GLA_DV, LANES), F32)],
        compiler_params=_cparams(("parallel", "arbitrary")),
        name="scans",
    )(r, kk, v, lw, b, ke, r, kk, v, lw, b, ke, q, k, gv, lg, q, k, gv, lg)
    return (yf, yb), (of, ob)


CONV_K = TB + LANES


def _conv_shift_table():
    t = jnp.arange(TB)[:, None]
    s = jnp.arange(CONV_K)[None, :]
    prev = jnp.where(t == 0, s == TB + 2 * HALO - 1, s == t - 1)
    nxt = jnp.where(t == TB - 1, s == TB, s == t + 1)
    return jnp.stack([prev, nxt]).astype(BF16)


def _glaprep_kernel(n_tiles, p_ref, hp_ref, hn_ref, sh_ref, cw_ref, aup_ref, ab_ref, q_out, k_out, v_out,
                    og_out, lg_out):
    j = pl.program_id(1)
    ub = p_ref[0][:, :GLA_QKV]
    u = ub.astype(F32)
    has_prev = jnp.where(j <= 1, 0.0, 1.0)
    has_next = jnp.where(jnp.logical_or(j == 0, j == n_tiles - 1), 0.0, 1.0)
    ext = jnp.concatenate([ub,
                           (hn_ref[0][:, :GLA_QKV].astype(F32) * has_next).astype(BF16),
                           (hp_ref[0][:, :GLA_QKV].astype(F32) * has_prev).astype(BF16),
                           jnp.zeros((CONV_K - TB - 2 * HALO, GLA_QKV), BF16)], axis=0)
    prev1 = jnp.dot(sh_ref[0], ext, preferred_element_type=F32)
    next1 = jnp.dot(sh_ref[1], ext, preferred_element_type=F32)
    cw = cw_ref[...]
    conv = cw[0:1] * prev1 + cw[1:2] * u + cw[2:3] * next1
    qkv = conv * _sigmoid(conv)
    q_out[0] = (qkv[:, :GLA_KW] * (GLA_DK ** -0.5)).astype(ACT)
    k_out[0] = qkv[:, GLA_KW:2 * GLA_KW].astype(ACT)
    v_out[0] = qkv[:, 2 * GLA_KW:].astype(ACT)
    og_out[0] = p_ref[0][:, GLA_QKV:GLA_QKV + GLA_WIDTH]
    z = _bdot(p_ref[0][:, GLA_QKV + GLA_WIDTH:], aup_ref[...]) + ab_ref[...]
    lg = (jnp.minimum(z, 0.0) - jnp.log1p(jnp.exp(-jnp.abs(z)))) * (1.0 / GLA_GATE_NORM)
    for d in range(2):
        lg_out[d, 0] = lg[:, d * GLA_KW:(d + 1) * GLA_KW]


def _glaprep_call(p_gl, prm):
    B, N, _ = p_gl.shape
    sub = HALO
    hb = TB // sub
    nhb = N // sub
    args = [p_gl, p_gl, p_gl, _conv_shift_table(), prm["conv"], prm["a_up"], prm["a_b"]]
    in_specs = [_tile_spec(GLA_PAD),
                pl.BlockSpec((1, sub, GLA_PAD), lambda b, j: (b, jnp.maximum(j * hb - 1, 0), 0)),
                pl.BlockSpec((1, sub, GLA_PAD), lambda b, j: (b, jnp.minimum(j * hb + hb, nhb - 1), 0)),
                _const_spec((2, TB, CONV_K)),
                _const_spec((3, GLA_QKV)), _const_spec((LANES, 2 * GLA_KW)), _const_spec((1, 2 * GLA_KW))]
    out_shapes = [jax.ShapeDtypeStruct((B, N, GLA_KW), ACT), jax.ShapeDtypeStruct((B, N, GLA_KW), ACT),
                  jax.ShapeDtypeStruct((B, N, GLA_WIDTH), ACT), jax.ShapeDtypeStruct((B, N, GLA_WIDTH), ACT),
                  jax.ShapeDtypeStruct((2, B, N, GLA_KW), F32)]
    out_specs = [_tile_spec(GLA_KW), _tile_spec(GLA_KW), _tile_spec(GLA_WIDTH), _tile_spec(GLA_WIDTH),
                 _dir_tile_spec(GLA_KW)]
    return args, in_specs, out_shapes, out_specs


def _prep_kernel(has_vres, n_tiles, n_rw_in, n_gla_in, n_rw_out, *refs):
    rw_in, refs = refs[:n_rw_in], refs[n_rw_in:]
    gla_in, outs = refs[:n_gla_in], refs[n_gla_in:]
    _rwprep_kernel(has_vres, n_tiles, *rw_in, *outs[:n_rw_out])
    _glaprep_kernel(n_tiles, *gla_in, *outs[n_rw_out:])


def _prep(p_rw, p_gl, prm, gla_prm, v_first):
    B, N, _ = p_rw.shape
    nt = N // TB
    rw_args, rw_in, rw_shapes, rw_out = _rwprep_call(p_rw, prm, v_first)
    gla_args, gla_in, gla_shapes, gla_out = _glaprep_call(p_gl, gla_prm)
    outs = pl.pallas_call(
        functools.partial(_prep_kernel, v_first is not None, nt, len(rw_args), len(gla_args), len(rw_shapes)),
        out_shape=rw_shapes + gla_shapes,
        grid=(B, nt),
        in_specs=rw_in + gla_in,
        out_specs=rw_out + gla_out,
        compiler_params=_cparams(("parallel", "parallel")),
        name="prep",
    )(*rw_args, *gla_args)
    return outs[:len(rw_shapes)], outs[len(rw_shapes):]


def _readout_kernel(group, first, split_src, *refs):
    tok = [refs[i * group:(i + 1) * group] for i in range(7)]
    refs = refs[7 * group:]
    n_res = group + 1 if split_src else group
    res_refs, refs = refs[:n_res], refs[n_res:]
    mod_ref, gnw_ref, gnb_ref, ggn_ref, wout_ref, gpost_ref, bd_ref, xo_ref = refs
    bd = bd_ref[...]
    starts_with_ctx = jnp.logical_and(first == 0, pl.program_id(1) == 0)
    for k in range(group):
        yf_ref, yb_ref, bon_ref, g_ref, of_ref, ob_ref, og_ref = (t[k] for t in tok)
        gate = mod_ref[0, 1, 2:3]
        if split_src:
            x_res = res_refs[k + 1][0]
            if k == 0:
                x_res = jnp.where(starts_with_ctx, res_refs[0][0], x_res)
        else:
            x_res = res_refs[k][0]
        if k == 0 and first == 0:
            gate = jnp.where(starts_with_ctx, mod_ref[0, 0, 2:3], gate)
        y = yf_ref[0].astype(F32) + yb_ref[0].astype(F32)
        mu = _seg_sum(y, bd, exact=False) * (1.0 / RW_HEAD)
        yc = y - mu
        var = _seg_sum(yc * yc, bd, exact=False) * (1.0 / RW_HEAD)
        yn = yc * lax.rsqrt(var + RW_GN_EPS) * gnw_ref[...] + gnb_ref[...]
        rw = (yn + bon_ref[0].astype(F32)) * g_ref[0].astype(F32)
        o = of_ref[0].astype(F32) + ob_ref[0].astype(F32)
        og = og_ref[0].astype(F32)
        parts = [rw.astype(BF16)]
        for h in range(GLA_HEADS):
            sl = slice(GLA_DV * h, GLA_DV * (h + 1))
            oh = o[:, sl]
            on = oh * lax.rsqrt(jnp.mean(oh * oh, axis=-1, keepdims=True) + GLA_NORM_EPS)
            ogh = og[:, sl]
            parts.append((on * ggn_ref[:, sl] * (ogh * _sigmoid(ogh))).astype(BF16))
        cat = jnp.concatenate(parts, axis=1)
        mx = jnp.dot(cat, wout_ref[...], preferred_element_type=F32)
        xo_ref[0, k * TB:(k + 1) * TB, :] = x_res + gate * _rms(mx, gpost_ref[...])


def _readout(y, bonus, g, o, og, ctx, x, xc, modtab, prm, latents_only):
    B, N, _ = bonus.shape
    first = 1 if latents_only else 0
    n_tiles = N // TB - first
    group = max(k for k in (4, 3, 2, 1) if n_tiles % k == 0)

    def tiles(c, shift=0):
        return [pl.BlockSpec((1, TB, c), lambda b, j, k=k: (b, jnp.maximum(first + group * j + k + shift, 0), 0))
                for k in range(group)]

    if xc is None:
        res, res_specs = [ctx] + [x] * group, [pl.BlockSpec((1, TB, D), lambda b, j: (b, 0, 0))] + tiles(D, -1)
    else:
        res, res_specs = [xc] * group, tiles(D)
    tok_args, tok_specs = [], []
    for arr in (y[0], y[1], bonus, g, o[0], o[1], og):
        tok_args += [arr] * group
        tok_specs += tiles(arr.shape[-1])
    return pl.pallas_call(
        functools.partial(_readout_kernel, group, first, xc is None),
        out_shape=jax.ShapeDtypeStruct((B, n_tiles * TB, D), F32),
        grid=(B, n_tiles // group),
        in_specs=tok_specs + res_specs + [_wide_mod_spec(),
                  _const_spec((1, RW_WIDTH)), _const_spec((1, RW_WIDTH)), _const_spec((1, GLA_WIDTH)),
                  _const_spec((D, D)), _const_spec((1, D)), _const_spec((RW_WIDTH, RW_WIDTH))],
        out_specs=pl.BlockSpec((1, group * TB, D), lambda b, j: (b, j, 0)),
        compiler_params=_cparams(("parallel", "parallel")),
        name="readout",
    )(*tok_args, *res, modtab, prm["gn_w"], prm["gn_b"], prm["gla_gn_w"],
      prm["w_out"], prm["norm_post"], prm["bd64"])


def _swiglu_acc(hb, wg_ref, wu_ref, wd_ref):
    acc = jnp.zeros((hb.shape[0], D), F32)
    for c in range(D_FF // FCH):
        sl = slice(c * FCH, (c + 1) * FCH)
        gate = jnp.dot(hb, wg_ref[:, sl].astype(BF16), preferred_element_type=F32)
        up = jnp.dot(hb, wu_ref[:, sl].astype(BF16), preferred_element_type=F32)
        act = (gate * _sigmoid(gate) * up).astype(BF16)
        acc = acc + jnp.dot(act, wd_ref[sl, :].astype(BF16), preferred_element_type=F32)
    return acc


def _ffn_kernel(x_ref, mod_ref, gpre_ref, gpost_ref, wg_ref, wu_ref, wd_ref, xo_ref):
    x = x_ref[0]
    hb = (_rms(x, gpre_ref[...]) * (1.0 + _wide_mod(mod_ref, 4)) + _wide_mod(mod_ref, 3)).astype(BF16)
    fx = _swiglu_acc(hb, wg_ref, wu_ref, wd_ref)
    xo_ref[0] = x + _wide_mod(mod_ref, 5) * _rms(fx, gpost_ref[...])


def _single_buffered(shape):
    nd = len(shape)
    return pl.BlockSpec(shape, lambda *_: (0,) * nd, pipeline_mode=pl.Buffered(1))


def _ffn(xc, modtab, g_pre, g_post, wg, wu, wd):
    B, N, _ = xc.shape
    wide = pl.BlockSpec((1, TW, D), lambda b, j: (b, j, 0))
    return pl.pallas_call(
        _ffn_kernel,
        out_shape=jax.ShapeDtypeStruct((B, N, D), F32),
        grid=(B, N // TW),
        in_specs=[wide, _wide_mod_spec(), _const_spec((1, D)), _const_spec((1, D)),
                  _single_buffered((D, D_FF)), _single_buffered((D, D_FF)), _single_buffered((D_FF, D))],
        out_specs=wide,
        compiler_params=_cparams(("parallel", "parallel")),
        name="ffn",
    )(xc, modtab, g_pre, g_post, wg, wu, wd)


MOE_TB = 1024
MOE_SEG = 32
MOE_TM = 512
MOE_R = 2 * MOE_TB + N_EXPERTS * MOE_SEG
MOE_NP = MOE_R // MOE_SEG


def _moe_route_kernel(x_ref, mod_ref, gpre_ref, router_ref, h_ref, info_ref, infot_ref, cnt_ref):
    mod = mod_ref[0, 0]
    h = _rms(x_ref[0], gpre_ref[...]) * (1.0 + mod[4:5]) + mod[3:4]
    hb = h.astype(BF16)
    h_ref[0] = hb
    lane = lax.broadcasted_iota(jnp.int32, (MOE_TB, LANES), 1)
    h_lo = (h - hb.astype(F32)).astype(BF16)
    logits = (jnp.dot(hb, router_ref[0], preferred_element_type=F32)
              + jnp.dot(hb, router_ref[1], preferred_element_type=F32)
              + jnp.dot(h_lo, router_ref[0], preferred_element_type=F32))
    logits = jnp.where(lane < N_EXPERTS, logits, -jnp.inf)
    v1 = jnp.max(logits, axis=-1, keepdims=True)
    i1 = jnp.min(jnp.where(logits == v1, lane, LANES), axis=-1, keepdims=True)
    rest = jnp.where(lane == i1, -jnp.inf, logits)
    v2 = jnp.max(rest, axis=-1, keepdims=True)
    i2 = jnp.min(jnp.where(rest == v2, lane, LANES), axis=-1, keepdims=True)
    ex = jnp.exp(v2 - v1)
    w1 = 1.0 / (1.0 + ex)
    w2 = ex * w1
    e1 = jnp.where(lane == i1, 1.0, 0.0)
    e2 = jnp.where(lane == i2, 1.0, 0.0)
    es = e1 + e2
    t = lax.broadcasted_iota(jnp.int32, (MOE_TB, MOE_TB), 0)
    s = lax.broadcasted_iota(jnp.int32, (MOE_TB, MOE_TB), 1)
    before = jnp.where(s < t, 1.0, 0.0).astype(BF16)
    rank = jnp.dot(before, es.astype(BF16), preferred_element_type=F32)
    cnt = jnp.sum(es, axis=0, keepdims=True)
    segs = jnp.floor((cnt + (MOE_SEG - 1)) * (1.0 / MOE_SEG))
    ea = lax.broadcasted_iota(jnp.int32, (LANES, LANES), 0)
    eb = lax.broadcasted_iota(jnp.int32, (LANES, LANES), 1)
    earlier = jnp.where(ea < eb, 1.0, 0.0).astype(BF16)
    start = jnp.dot(jnp.broadcast_to(segs, (SUBLANES, LANES)).astype(BF16), earlier,
                    preferred_element_type=F32)[0:1] * MOE_SEG
    pos = rank + start
    d1 = jnp.sum(e1 * pos, axis=-1, keepdims=True)
    d2 = jnp.sum(e2 * pos, axis=-1, keepdims=True)
    info = jnp.where(lane == 0, d1, jnp.where(lane == 1, d2, jnp.where(lane == 2, w1, jnp.where(lane == 3, w2, 0.0))))
    info_ref[0] = info
    infot_ref[0] = jnp.transpose(info)[0:SUBLANES]
    cnt_ref[0] = jnp.broadcast_to(cnt, (SUBLANES, LANES))


def _moe_route(xs, modtab, g_pre, router):
    B, S, _ = xs.shape
    per = S // MOE_TB
    nb = B * per
    blk = lambda c: pl.BlockSpec((1, MOE_TB, c), lambda i: (i // per, i % per, 0))
    flat = lambda r, c: pl.BlockSpec((1, r, c), lambda i: (i, 0, 0))
    return pl.pallas_call(
        _moe_route_kernel,
        out_shape=[jax.ShapeDtypeStruct((nb, MOE_TB, D), BF16), jax.ShapeDtypeStruct((nb, MOE_TB, LANES), F32),
                   jax.ShapeDtypeStruct((nb, SUBLANES, MOE_TB), F32),
                   jax.ShapeDtypeStruct((nb, SUBLANES, LANES), F32)],
        grid=(nb,),
        in_specs=[blk(D), pl.BlockSpec((1, 1, 6, D), lambda i: (i // per, 1, 0, 0)),
                  _const_spec((1, D)), _const_spec((2, D, LANES))],
        out_specs=[flat(MOE_TB, D), flat(MOE_TB, LANES), flat(SUBLANES, MOE_TB), flat(SUBLANES, LANES)],
        compiler_params=_cparams(("parallel",)),
        name="moe_route",
    )(xs, modtab, g_pre, router)


def _moe_plan(cnt, n_tiles):
    pc = (cnt + MOE_SEG - 1) // MOE_SEG * MOE_SEG
    inc = jnp.cumsum(pc, axis=1)
    loff = inc - pc
    reg = (jnp.sum(pc, axis=0) + MOE_TM - 1) // MOE_TM * MOE_TM
    gend = jnp.cumsum(reg)
    goff = (gend - reg)[None, :] + jnp.cumsum(pc, axis=0) - pc
    rows = jnp.arange(MOE_NP, dtype=jnp.int32) * MOE_SEG
    e_p = jnp.sum((rows[None, :, None] >= inc[:, None, :]).astype(jnp.int32), axis=-1)
    e_c = jnp.minimum(e_p, N_EXPERTS - 1)
    pick = (e_c[:, :, None] == jnp.arange(N_EXPERTS)[None, None, :]).astype(jnp.int32)
    dst = jnp.sum(pick * (goff - loff)[:, None, :], axis=-1) + rows[None, :]
    dst = jnp.where(e_p < N_EXPERTS, dst, 0).astype(jnp.int32)
    n_valid = (inc[:, -1] // MOE_SEG).astype(jnp.int32)
    trow = jnp.arange(n_tiles, dtype=jnp.int32) * MOE_TM
    te = jnp.sum((trow[:, None] >= gend[None, :]).astype(jnp.int32), axis=-1)
    valid = te < N_EXPERTS
    last = gend[-1] // MOE_TM - 1
    te = jnp.where(valid, te, te[last]).astype(jnp.int32)
    src = jnp.where(valid, jnp.arange(n_tiles, dtype=jnp.int32), last).astype(jnp.int32)
    fresh = valid & jnp.concatenate([jnp.ones((1,), bool), te[1:] != te[:-1]])
    return dst, n_valid, te, src, valid.astype(jnp.int32), fresh.astype(jnp.int32)


def _piece_copy(src_ref, src_row, dst_ref, dst_row, sem):
    return pltpu.make_async_copy(src_ref.at[pl.ds(src_row, MOE_SEG)], dst_ref.at[pl.ds(dst_row, MOE_SEG)], sem)


def _moe_gather_kernel(dst_ref, nv_ref, h_ref, infot_ref, xg_in_ref, xg_ref, buf_ref, sem_ref):
    del xg_in_ref
    i = pl.program_id(0)
    it = infot_ref[0]
    rr = lax.broadcasted_iota(jnp.int32, (MOE_R, MOE_TB), 0).astype(F32)
    onehot = jnp.where(rr == it[0:1], 1.0, jnp.where(rr == it[1:2], 1.0, 0.0)).astype(BF16)
    buf_ref[...] = jnp.dot(onehot, h_ref[0], preferred_element_type=F32).astype(BF16)
    nv = nv_ref[i]

    def piece(p):
        return _piece_copy(buf_ref, pl.multiple_of(p * MOE_SEG, MOE_SEG),
                           xg_ref, pl.multiple_of(dst_ref[i, p], MOE_SEG), sem_ref.at[p])

    def start(p, c):
        piece(p).start()
        return c

    def wait(p, c):
        piece(p).wait()
        return c

    lax.fori_loop(0, nv, start, 0)
    lax.fori_loop(0, nv, wait, 0)


def _moe_gather(dst, n_valid, h, info_t, n_rows):
    nb = h.shape[0]
    flat = lambda r, c: pl.BlockSpec((1, r, c), lambda i, *_: (i, 0, 0))
    return pl.pallas_call(
        _moe_gather_kernel,
        out_shape=jax.ShapeDtypeStruct((n_rows, D), BF16),
        grid_spec=pltpu.PrefetchScalarGridSpec(
            num_scalar_prefetch=2, grid=(nb,),
            in_specs=[flat(MOE_TB, D), flat(SUBLANES, MOE_TB), pl.BlockSpec(memory_space=pl.ANY)],
            out_specs=pl.BlockSpec(memory_space=pl.ANY),
            scratch_shapes=[pltpu.VMEM((MOE_R, D), BF16), pltpu.SemaphoreType.DMA((MOE_NP,))]),
        input_output_aliases={4: 0},
        compiler_params=_cparams(("arbitrary",)),
        name="moe_gather",
    )(dst, n_valid, h, info_t, jnp.zeros((n_rows, D), BF16))


N_FCH = D_FF // FCH


def _moe_ffn_kernel(te_ref, src_ref, valid_ref, fresh_ref, x_ref, wg_hbm, wu_hbm, wd_hbm, y_ref,
                    wg_ref, wu_ref, wd_ref, sem_ref):
    i = pl.program_id(0)
    e = te_ref[i]
    fresh = fresh_ref[i] == 1

    def slice_copies(c):
        cols = pl.ds(c * FCH, FCH)
        return (pltpu.make_async_copy(wg_hbm.at[e, :, cols], wg_ref.at[:, cols], sem_ref.at[0, c]),
                pltpu.make_async_copy(wu_hbm.at[e, :, cols], wu_ref.at[:, cols], sem_ref.at[1, c]),
                pltpu.make_async_copy(wd_hbm.at[e, cols, :], wd_ref.at[cols, :], sem_ref.at[2, c]))

    @pl.when(fresh)
    def _():
        for c in range(N_FCH):
            for cp in slice_copies(c):
                cp.start()
        hb = x_ref[...]
        acc = jnp.zeros((MOE_TM, D), F32)
        for c in range(N_FCH):
            for cp in slice_copies(c):
                cp.wait()
            sl = slice(c * FCH, (c + 1) * FCH)
            gate = jnp.dot(hb, wg_ref[:, sl].astype(BF16), preferred_element_type=F32)
            up = jnp.dot(hb, wu_ref[:, sl].astype(BF16), preferred_element_type=F32)
            act = (gate * _sigmoid(gate) * up).astype(BF16)
            acc = acc + jnp.dot(act, wd_ref[sl, :].astype(BF16), preferred_element_type=F32)
        y_ref[...] = acc.astype(BF16)

    @pl.when(jnp.logical_and(valid_ref[i] == 1, jnp.logical_not(fresh)))
    def _():
        y_ref[...] = _swiglu_acc(x_ref[...], wg_ref, wu_ref, wd_ref).astype(BF16)

    @pl.when(valid_ref[i] == 0)
    def _():
        y_ref[...] = jnp.zeros_like(y_ref)


def _moe_ffn(te, src, valid, fresh, xg, wg, wu, wd):
    n_rows = xg.shape[0]
    hbm = pl.BlockSpec(memory_space=pl.ANY)
    return pl.pallas_call(
        _moe_ffn_kernel,
        out_shape=jax.ShapeDtypeStruct((n_rows, D), BF16),
        grid_spec=pltpu.PrefetchScalarGridSpec(
            num_scalar_prefetch=4, grid=(n_rows // MOE_TM,),
            in_specs=[pl.BlockSpec((MOE_TM, D), lambda i, te, src, *_: (src[i], 0)), hbm, hbm, hbm],
            out_specs=pl.BlockSpec((MOE_TM, D), lambda i, *_: (i, 0)),
            scratch_shapes=[pltpu.VMEM((D, D_FF), F32), pltpu.VMEM((D, D_FF), F32), pltpu.VMEM((D_FF, D), F32),
                            pltpu.SemaphoreType.DMA((3, N_FCH))]),
        compiler_params=_cparams(("arbitrary",)),
        name="moe_ffn",
    )(te, src, valid, fresh, xg, wg, wu, wd)


def _moe_combine_kernel(dst_ref, nv_ref, x_ref, mod_ref, gpost_ref, info_ref, yg_ref, xo_ref, buf_ref, sem_ref):
    i = pl.program_id(0)
    nv = nv_ref[i]

    def piece(p):
        return _piece_copy(yg_ref, pl.multiple_of(dst_ref[i, p], MOE_SEG),
                           buf_ref, pl.multiple_of(p * MOE_SEG, MOE_SEG), sem_ref.at[p])

    def start(p, c):
        piece(p).start()
        return c

    def clear(p, c):
        buf_ref[pl.ds(pl.multiple_of(p * MOE_SEG, MOE_SEG), MOE_SEG), :] = jnp.zeros((MOE_SEG, D), BF16)
        return c

    def wait(p, c):
        piece(p).wait()
        return c

    lax.fori_loop(0, nv, start, 0)
    lax.fori_loop(nv, MOE_NP, clear, 0)
    info = info_ref[0]
    rr = lax.broadcasted_iota(jnp.int32, (MOE_TB, MOE_R), 1).astype(F32)
    comb = jnp.where(rr == info[:, 0:1], info[:, 2:3], jnp.where(rr == info[:, 1:2], info[:, 3:4], 0.0)).astype(BF16)
    lax.fori_loop(0, nv, wait, 0)
    fx = jnp.dot(comb, buf_ref[...], preferred_element_type=F32)
    xo_ref[0] = x_ref[0] + mod_ref[0, 0][5:6] * _rms(fx, gpost_ref[...])


def _moe_combine(dst, n_valid, xs, modtab, g_post, info, yg):
    B, S, _ = xs.shape
    per = S // MOE_TB
    blk = pl.BlockSpec((1, MOE_TB, D), lambda i, *_: (i // per, i % per, 0))
    return pl.pallas_call(
        _moe_combine_kernel,
        out_shape=jax.ShapeDtypeStruct((B, S, D), F32),
        grid_spec=pltpu.PrefetchScalarGridSpec(
            num_scalar_prefetch=2, grid=(B * per,),
            in_specs=[blk, pl.BlockSpec((1, 1, 6, D), lambda i, *_: (i // per, 1, 0, 0)),
                      pl.BlockSpec((1, D), lambda i, *_: (0, 0)),
                      pl.BlockSpec((1, MOE_TB, LANES), lambda i, *_: (i, 0, 0)),
                      pl.BlockSpec(memory_space=pl.ANY)],
            out_specs=blk,
            scratch_shapes=[pltpu.VMEM((MOE_R, D), BF16), pltpu.SemaphoreType.DMA((MOE_NP,))]),
        compiler_params=_cparams(("arbitrary",)),
        name="moe_combine",
    )(dst, n_valid, xs, modtab, g_post, info, yg)


def _moe(xs, modtab, g_pre, g_post, router, wg, wu, wd):
    B, S, _ = xs.shape
    nb = B * S // MOE_TB
    h, info, info_t, cnt = _moe_route(xs, modtab, g_pre, router)
    worst = 2 * B * S + nb * N_EXPERTS * (MOE_SEG - 1) + N_EXPERTS * (MOE_TM - 1)
    n_tiles = -(-worst // MOE_TM)
    dst, n_valid, te, src, valid, fresh = _moe_plan(cnt[:, 0, :N_EXPERTS].astype(jnp.int32), n_tiles)
    xg = _moe_gather(dst, n_valid, h, info_t, n_tiles * MOE_TM)
    yg = _moe_ffn(te, src, valid, fresh, xg, wg, wu, wd)
    return _moe_combine(dst, n_valid, xs, modtab, g_post, info, yg)


def _block_diag2(w):
    z = jnp.zeros_like(w[0])
    return jnp.concatenate([jnp.concatenate([w[0], z], axis=1), jnp.concatenate([z, w[1]], axis=1)], axis=0)


def _row(v):
    return v.reshape(1, -1).astype(F32)


def _head_ones(width, head):
    i = jnp.arange(width) // head
    return (i[:, None] == i[None, :]).astype(BF16)


def kernel(x, c, ctx, c_ctx, ada_w, ada_b, norm_mix_pre, norm_mix_post, norm_ffn_pre, norm_ffn_post, w_in, shift_mu, rw_w_up, rw_w0, rw_a_up, rw_a0, rw_k_k, rw_k_a, rw_r_k, rw_g_up, rw_gn_w, rw_gn_b, rw_v_down, rw_v_up, rw_v0, gla_conv, gla_a_up, gla_a_b, gla_gn_w, w_out, ffn_w_gate, ffn_w_up, ffn_w_down, moe_router, moe_w_gate, moe_w_up, moe_w_down):
    B, S, _ = x.shape
    n_ctx = ctx.shape[1]
    depth = w_in.shape[0]
    assert n_ctx == TB and S % MOE_TB == 0 and (n_ctx + S) % TW == 0 and depth == 2

    xc = None
    pad_rows = -(B + 1) % SUBLANES
    cvec = jnp.concatenate([c, c_ctx[None, :], jnp.zeros((pad_rows, D), F32)], axis=0)
    bd64 = _head_ones(RW_WIDTH, RW_HEAD)
    ada_b3 = ada_b.reshape(depth, 1, 6 * D)
    v_first = None
    out = None
    for i in range(depth):
        last = i == depth - 1
        mods = _adaln(cvec, ada_w, ada_b3, i)
        mod_x = mods[:B].reshape(B, 6, D)
        mod_c = jnp.broadcast_to(mods[B].reshape(1, 6, D), (B, 6, D))
        modtab = jnp.stack([mod_c, mod_x], axis=1)

        p_rw, p_gl = _inproj(ctx, x, xc, _row(norm_mix_pre[i]), modtab, w_in, i)

        prm = dict(
            mu=_row(shift_mu[i]),
            w_up=_block_diag2(rw_w_up[i]).astype(BF16), w0=_row(rw_w0[i]),
            a_up=_block_diag2(rw_a_up[i]).astype(BF16), a0=_row(rw_a0[i]),
            k_k=_row(rw_k_k[i]), k_a=_row(rw_k_a[i]), r_k=_row(rw_r_k[i]),
            g_up=rw_g_up[i].astype(BF16), bd64=bd64,
            gn_w=_row(rw_gn_w[i]), gn_b=_row(rw_gn_b[i]), gla_gn_w=_row(gla_gn_w[i]),
            w_out=w_out[i].astype(BF16), norm_post=_row(norm_mix_post[i]),
        )
        gate_pad = jnp.zeros((LANES - 2 * GLA_GATE_RANK, 2 * GLA_KW), F32)
        gla_prm = dict(conv=gla_conv[i].astype(F32), a_b=_row(gla_a_b[i]),
                       a_up=jnp.concatenate([_block_diag2(gla_a_up[i]), gate_pad], axis=0).astype(BF16))
        if i > 0:
            pad = LANES - RW_V_RANK
            prm["v_down"] = jnp.concatenate([rw_v_down[i - 1], jnp.zeros((RW_WIDTH, pad), F32)], axis=1).astype(BF16)
            prm["v_up"] = jnp.concatenate([rw_v_up[i - 1], jnp.zeros((pad, RW_WIDTH), F32)], axis=0).astype(BF16)
            prm["v0"] = _row(rw_v0[i - 1])

        (r, kk, vm, g, bonus, lw, bb, ke), (q, k, gv, og, lg) = _prep(p_rw, p_gl, prm, gla_prm,
                                                                     v_first if i > 0 else None)
        if i == 0:
            v_first = vm
        y, o = _scans(r, kk, vm, lw, bb, ke, q, k, gv, lg)
        xc = _readout(y, bonus, g, o, og, ctx, x, xc, modtab, prm, latents_only=last)

        jf = i // 2
        if i % 2 == 0:
            xc = _ffn(xc, modtab, _row(norm_ffn_pre[i]), _row(norm_ffn_post[i]),
                      ffn_w_gate[jf], ffn_w_up[jf], ffn_w_down[jf])
        else:
            router = jnp.concatenate([moe_router[jf], jnp.zeros((D, LANES - N_EXPERTS), F32)], axis=1)
            r_hi = router.astype(BF16)
            router = jnp.stack([r_hi, (router - r_hi.astype(F32)).astype(BF16)])
            out = _moe(xc, modtab, _row(norm_ffn_pre[i]), _row(norm_ffn_post[i]), router,
                       moe_w_gate[jf], moe_w_up[jf], moe_w_down[jf])
    return out
```

```python
import functools
import math

import jax
import jax.numpy as jnp
from jax import lax
from jax.experimental import pallas as pl
from jax.experimental.pallas import tpu as pltpu

F32, BF16 = jnp.float32, jnp.bfloat16
ACT = BF16

D = 1024
GRID_W = 64
RW_WIDTH = 512
RW_HEAD = 64
RW_RANK = 64
RW_G_RANK = 128
RW_V_RANK = 32
RW_GN_EPS = 64e-5
GLA_WIDTH = 512
GLA_HEADS = 4
GLA_DV = 128
GLA_DK = 64
GLA_KW = 256
GLA_GATE_RANK = 16
GLA_GATE_NORM = 16.0
GLA_NORM_EPS = 1e-5
D_FF = 2816
N_EXPERTS = 8
NORM_EPS = 1e-6
RW_COLS = 3 * RW_WIDTH + 4 * RW_RANK + RW_G_RANK
GLA_QKV = 2 * GLA_KW + GLA_WIDTH
GLA_COLS = GLA_QKV + GLA_WIDTH + 2 * GLA_GATE_RANK
GLA_PAD = 1664

LANES = 128
SUBLANES = 8
TB = 256
TW = 3 * TB
CH = 64
FCH = 256
HALO = 16
VMEM_LIMIT = 56 * 1024 * 1024


def _cparams(sem):
    return pltpu.CompilerParams(dimension_semantics=sem, vmem_limit_bytes=VMEM_LIMIT)


def _bdot(a, b):
    return jnp.dot(a.astype(BF16), b.astype(BF16), preferred_element_type=F32)


def _bdot_nt(a, b):
    return lax.dot_general(a.astype(BF16), b.astype(BF16), (((1,), (1,)), ((), ())),
                           preferred_element_type=F32)


def _bdot_tn(a, b):
    return lax.dot_general(a.astype(BF16), b.astype(BF16), (((0,), (0,)), ((), ())),
                           preferred_element_type=F32)


def _split_dot(a_exact, x):
    h1 = x.astype(BF16)
    r1 = x - h1.astype(F32)
    h2 = r1.astype(BF16)
    h3 = (r1 - h2.astype(F32)).astype(BF16)
    return (jnp.dot(a_exact, h1, preferred_element_type=F32)
            + jnp.dot(a_exact, h2, preferred_element_type=F32)
            + jnp.dot(a_exact, h3, preferred_element_type=F32))


def _seg_sum(x, bd):
    return jnp.dot(x.astype(BF16), bd, preferred_element_type=F32)


def _sigmoid(x):
    return jax.nn.sigmoid(x)


def _rms(x, g):
    return x * lax.rsqrt(jnp.mean(x * x, axis=-1, keepdims=True) + NORM_EPS) * g


def _pair_stack(z, lo):
    return jnp.concatenate([jnp.where(lo, z, 0.0), jnp.where(lo, 0.0, z)], axis=0)


def _adaln_kernel(c_ref, w_ref, b_ref, o_ref):
    c = c_ref[...]
    s = c * _sigmoid(c)
    o_ref[...] = jnp.dot(s, w_ref[0], precision=lax.Precision.HIGHEST,
                         preferred_element_type=F32) + b_ref[0]


def _adaln(cvec, w, b, layer):
    rows = cvec.shape[0]
    n = w.shape[2]
    return pl.pallas_call(
        _adaln_kernel,
        out_shape=jax.ShapeDtypeStruct((rows, n), F32),
        grid=(n // D,),
        in_specs=[pl.BlockSpec((rows, D), lambda i: (0, 0)),
                  pl.BlockSpec((1, D, D), lambda i: (layer, 0, i)),
                  pl.BlockSpec((1, 1, D), lambda i: (layer, 0, i))],
        out_specs=pl.BlockSpec((rows, D), lambda i: (0, i)),
        compiler_params=_cparams(("arbitrary",)),
        name="adaln",
    )(cvec, w, b)


def _wide_mod(mod_ref, k):
    is_ctx = jnp.logical_and(pl.program_id(1) == 0, lax.broadcasted_iota(jnp.int32, (TW, 1), 0) < TB)
    return jnp.where(is_ctx, mod_ref[0, 0, k:k + 1], mod_ref[0, 1, k:k + 1])


def _wide_rows(refs):
    if len(refs) == 1:
        return refs[0][0]
    first = jnp.where(pl.program_id(1) == 0, refs[0][0], refs[1][0])
    return jnp.concatenate([first] + [r[0] for r in refs[2:]], axis=0)


def _inproj_kernel(n_src, *refs):
    g_ref, mod_ref, w_ref, prw_ref, pgl_ref = refs[n_src:]
    x = _wide_rows(refs[:n_src])
    h = _rms(x, g_ref[...]) * (1.0 + _wide_mod(mod_ref, 1)) + _wide_mod(mod_ref, 0)
    hb = h.astype(BF16)
    prw_ref[0] = jnp.dot(hb, w_ref[0, :, :RW_COLS].astype(BF16), preferred_element_type=F32).astype(ACT)
    pgl_ref[0, :, :GLA_COLS] = jnp.dot(hb, w_ref[0, :, RW_COLS:].astype(BF16),
                                       preferred_element_type=F32).astype(ACT)
    pgl_ref[0, :, GLA_COLS:] = jnp.zeros((TW, GLA_PAD - GLA_COLS), ACT)


def _wide_src(ctx, x, xc):
    if xc is not None:
        return [xc], [pl.BlockSpec((1, TW, D), lambda b, j: (b, j, 0))]
    per = TW // TB
    specs = [pl.BlockSpec((1, TB, D), lambda b, j: (b, 0, 0))]
    for k in range(per):
        specs.append(pl.BlockSpec((1, TB, D), lambda b, j, k=k: (b, jnp.maximum(j * per + k - 1, 0), 0)))
    return [ctx] + [x] * per, specs


def _wide_mod_spec():
    return pl.BlockSpec((1, 2, 6, D), lambda b, j: (b, 0, 0, 0))


def _const_spec(shape):
    nd = len(shape)
    return pl.BlockSpec(shape, lambda *_: (0,) * nd)


def _tile_spec(c):
    return pl.BlockSpec((1, TB, c), lambda b, j: (b, j, 0))


def _dir_tile_spec(c):
    return pl.BlockSpec((2, 1, TB, c), lambda b, j: (0, b, j, 0))


def _inproj(ctx, x, xc, g, modtab, w, layer):
    B = modtab.shape[0]
    N = xc.shape[1] if xc is not None else ctx.shape[1] + x.shape[1]
    srcs, src_specs = _wide_src(ctx, x, xc)
    wide = lambda c: pl.BlockSpec((1, TW, c), lambda b, j: (b, j, 0))
    return pl.pallas_call(
        functools.partial(_inproj_kernel, len(srcs)),
        out_shape=[jax.ShapeDtypeStruct((B, N, RW_COLS), ACT),
                   jax.ShapeDtypeStruct((B, N, GLA_PAD), ACT)],
        grid=(B, N // TW),
        in_specs=src_specs + [_const_spec((1, D)), _wide_mod_spec(),
                              pl.BlockSpec((1, D, RW_COLS + GLA_COLS), lambda b, j: (layer, 0, 0),
                                           pipeline_mode=pl.Buffered(1))],
        out_specs=[wide(RW_COLS), wide(GLA_PAD)],
        compiler_params=_cparams(("parallel", "parallel")),
        name="inproj",
    )(*srcs, g, modtab, w)


def _shift_table():
    t = jnp.arange(TB)
    prev = t[:, None] - 1 == t[None, :]
    nxt = t[:, None] + 1 == t[None, :]
    col = (t % GRID_W)[:, None]
    return jnp.stack([jnp.stack([prev, nxt]),
                      jnp.stack([prev & (col != 0), nxt & (col != GRID_W - 1)])]).astype(BF16)


def _rwprep_kernel(has_vres, n_tiles, p_ref, hu_ref, hd_ref, sh_ref, mu_ref, wup_ref, w0_ref, aup_ref,
                   a0_ref, kk_ref, ka_ref, rk_ref, gup_ref, bd_ref, *rest):
    if has_vres:
        vf_ref, vdn_ref, vup_ref, v0_ref = rest[:4]
        rest = rest[4:]
    r_out, kk_out, v_out, g_out, bon_out, lw_out, b_out, ke_out = rest
    j = pl.program_id(1)
    pb = p_ref[0]
    p = pb.astype(F32)
    prev1 = jnp.dot(sh_ref[0, 0], pb, preferred_element_type=F32)
    next1 = jnp.dot(sh_ref[0, 1], pb, preferred_element_type=F32)
    has_upper = jnp.where(j == 1, 0.0, 1.0)
    has_lower = jnp.where(j == n_tiles - 1, 0.0, 1.0)
    up = jnp.concatenate([hu_ref[0].astype(F32) * has_upper, p[:TB - GRID_W]], axis=0)
    down = jnp.concatenate([p[GRID_W:], hd_ref[0].astype(F32) * has_lower], axis=0)
    cls = lax.broadcasted_iota(jnp.int32, p.shape, 1) & jnp.where(j == 0, 1, 3)
    shifted = jnp.where(cls == 0, prev1, jnp.where(cls == 1, next1, jnp.where(cls == 2, up, down)))
    u = p + mu_ref[...] * (shifted - p)

    r = u[:, 0:RW_WIDTH]
    k = u[:, RW_WIDTH:2 * RW_WIDTH]
    v = u[:, 2 * RW_WIDTH:3 * RW_WIDTH]
    o = 3 * RW_WIDTH
    wd = u[:, o:o + 2 * RW_RANK]
    ad = u[:, o + 2 * RW_RANK:o + 4 * RW_RANK]
    gd = u[:, o + 4 * RW_RANK:]

    w_logit = w0_ref[...] + _bdot(jnp.tanh(wd), wup_ref[...])
    lw = -math.exp(-0.5) * _sigmoid(w_logit)
    a = _sigmoid(a0_ref[...] + _bdot(ad, aup_ref[...]))
    bd = bd_ref[...]
    kk = k * kk_ref[...]
    kk = kk * lax.rsqrt(jnp.maximum(_seg_sum(kk * kk, bd), 1e-24))
    g = _bdot(_sigmoid(gd), gup_ref[...])
    if has_vres:
        gate = _sigmoid(v0_ref[...] + _bdot(_bdot(v, vdn_ref[...]), vup_ref[...]))
        vm = v + (vf_ref[0].astype(F32) - v) * gate
    else:
        vm = v
    k_scaled = k * ka_ref[...]
    k_rest = k - k_scaled
    a_sum = a[:, :RW_WIDTH] + a[:, RW_WIDTH:]
    ke_sum = 2.0 * k_rest + k_scaled * a_sum
    for d in range(2):
        a_d = a[:, d * RW_WIDTH:(d + 1) * RW_WIDTH]
        lw_out[d, 0] = lw[:, d * RW_WIDTH:(d + 1) * RW_WIDTH]
        b_out[d, 0] = (kk * a_d).astype(ACT)
        ke_out[d, 0] = (k_rest + k_scaled * a_d).astype(ACT)
    r_out[0] = r.astype(ACT)
    kk_out[0] = kk.astype(ACT)
    v_out[0] = vm.astype(ACT)
    g_out[0] = g.astype(ACT)
    bon_out[0] = (_seg_sum(r * ke_sum * rk_ref[...], bd) * vm).astype(ACT)


def _rwprep_call(p_rw, prm, v_first):
    B, N, _ = p_rw.shape
    hb = TB // GRID_W
    nhb = N // GRID_W
    has_vres = v_first is not None
    W2 = 2 * RW_WIDTH
    in_specs = [
        _tile_spec(RW_COLS),
        pl.BlockSpec((1, GRID_W, RW_COLS), lambda b, j: (b, jnp.maximum(j * hb - 1, 0), 0)),
        pl.BlockSpec((1, GRID_W, RW_COLS), lambda b, j: (b, jnp.minimum(j * hb + hb, nhb - 1), 0)),
        pl.BlockSpec((1, 2, TB, TB), lambda b, j: (jnp.minimum(j, 1), 0, 0, 0)),
        _const_spec((1, RW_COLS)), _const_spec((2 * RW_RANK, W2)), _const_spec((1, W2)),
        _const_spec((2 * RW_RANK, W2)), _const_spec((1, W2)),
        _const_spec((1, RW_WIDTH)), _const_spec((1, RW_WIDTH)), _const_spec((1, RW_WIDTH)),
        _const_spec((RW_G_RANK, RW_WIDTH)), _const_spec((RW_WIDTH, RW_WIDTH)),
    ]
    args = [p_rw, p_rw, p_rw, _shift_table(), prm["mu"], prm["w_up"], prm["w0"], prm["a_up"], prm["a0"],
            prm["k_k"], prm["k_a"], prm["r_k"], prm["g_up"], prm["bd64"]]
    if has_vres:
        in_specs += [_tile_spec(RW_WIDTH), _const_spec((RW_WIDTH, LANES)), _const_spec((LANES, RW_WIDTH)),
                     _const_spec((1, RW_WIDTH))]
        args += [v_first, prm["v_down"], prm["v_up"], prm["v0"]]
    tok = jax.ShapeDtypeStruct((B, N, RW_WIDTH), ACT)
    dtok = jax.ShapeDtypeStruct((2, B, N, RW_WIDTH), ACT)
    out_shapes = [tok] * 5 + [jax.ShapeDtypeStruct((2, B, N, RW_WIDTH), F32), dtok, dtok]
    out_specs = [_tile_spec(RW_WIDTH)] * 5 + [_dir_tile_spec(RW_WIDTH)] * 3
    return args, in_specs, out_shapes, out_specs


NCH = TB // CH


def _scan_kernel(rf_ref, kkf_ref, vf_ref, lwf_ref, bf_ref, kef_ref,
                 rb_ref, kkb_ref, vb_ref, lwb_ref, bb_ref, keb_ref,
                 gqf_ref, gkf_ref, gvf_ref, lgf_ref, gqb_ref, gkb_ref, gvb_ref, lgb_ref,
                 yf_ref, yb_ref, of_ref, ob_ref, st_ref, gst_ref):
    j = pl.program_id(1)

    @pl.when(j == 0)
    def _():
        st_ref[...] = jnp.zeros_like(st_ref)
        gst_ref[...] = jnp.zeros_like(gst_ref)

    ti = lax.broadcasted_iota(jnp.int32, (CH, LANES), 0)
    li = lax.broadcasted_iota(jnp.int32, (CH, LANES), 1)
    si = li & (CH - 1)
    lo = li < CH
    eye = jnp.where(ti == si, 1.0, 0.0)
    rblk = lax.broadcasted_iota(jnp.int32, (LANES, LANES), 0) // CH
    cblk = lax.broadcasted_iota(jnp.int32, (LANES, LANES), 1) // CH
    bdmask = rblk == cblk
    t64 = lax.broadcasted_iota(jnp.int32, (CH, CH), 0)
    s64 = lax.broadcasted_iota(jnp.int32, (CH, CH), 1)
    tri = [jnp.where(s64 <= t64, 1.0, 0.0).astype(BF16), jnp.where(s64 >= t64, 1.0, 0.0).astype(BF16)]
    m_incl = [si <= ti, si >= ti]
    m_strict = [si < ti, si > ti]
    n_pairs = RW_WIDTH // LANES
    in_refs = ((rf_ref, kkf_ref, vf_ref, lwf_ref, bf_ref, kef_ref),
               (rb_ref, kkb_ref, vb_ref, lwb_ref, bb_ref, keb_ref))
    y_refs = (yf_ref, yb_ref)

    def stack(z):
        return _pair_stack(z, lo).astype(BF16)

    st = {(d, p): st_ref[d, p] for d in range(2) for p in range(n_pairs)}

    def scan_step(step):
        cur = {}
        w_tot = {}
        for d in range(2):
            c = (NCH - 1 - step) if d == 1 else step
            rows = slice(c * CH, (c + 1) * CH)
            refs = in_refs[d]
            r, kk, v = (refs[i][0, rows, :].astype(F32) for i in range(3))
            lw = refs[3][0, 0, rows, :]
            b, ke = refs[4][0, 0, rows, :].astype(F32), refs[5][0, 0, rows, :].astype(F32)
            cum = _split_dot(tri[d], lw)
            tot = jnp.sum(lw, axis=0, keepdims=True)
            w_inv = jnp.exp(-cum)
            w_end = jnp.exp(tot - cum)
            w_tot[d] = jnp.exp(tot)
            rh = r * jnp.exp(cum)
            ah = -(kk * jnp.exp(cum - lw))
            bh, kh, bt, kt = b * w_inv, ke * w_inv, b * w_end, ke * w_end
            for p in range(n_pairs):
                sl = slice(LANES * p, LANES * (p + 1))
                cur[d, p] = dict(ah=ah[:, sl], rh=rh[:, sl], bh=bh[:, sl], kh=kh[:, sl], bt=bt[:, sl],
                                 kt=kt[:, sl], v=v[:, sl], rows=rows, sl=sl)
        yield
        for it in cur.values():
            it["lhs"] = jnp.concatenate([it["ah"], it["rh"]], axis=0).astype(BF16)
            rhs = jnp.concatenate([stack(it["bh"]), stack(it["kh"])], axis=0)
            it["a_all"] = _bdot_nt(it["lhs"], rhs)
        yield
        for (d, _), it in cur.items():
            a_all = it.pop("a_all")
            it["a_ab"] = jnp.where(m_strict[d], a_all[:CH, :LANES], 0.0)
            a_ak = jnp.where(m_strict[d], a_all[:CH, LANES:], 0.0)
            a_rb = jnp.where(m_incl[d], a_all[CH:, :LANES], 0.0)
            a_rk = jnp.where(m_incl[d], a_all[CH:, LANES:], 0.0)
            it["a_r"] = jnp.concatenate([a_rb, a_rk], axis=1).astype(BF16)
            it["v_bd"] = stack(it["v"])
            it["akv"] = _bdot(a_ak, it["v_bd"])
            it["t"] = eye + it["a_ab"]
            it["m"] = _bdot(it["a_ab"], stack(it["a_ab"]))
        yield
        for _ in range(int(math.log2(CH)) - 2):
            for it in cur.values():
                z = _bdot(it["m"], jnp.concatenate([stack(it["m"]), stack(it["t"])], axis=1))
                it["m"] = z[:, :LANES]
                it["t"] = it["t"] + z[:, LANES:]
            yield
        for it in cur.values():
            it["t"] = (it["t"] + _bdot(it["m"], stack(it["t"]))).astype(BF16)
        yield
        for it in cur.values():
            ta = jnp.dot(it["t"], jnp.concatenate([stack(it["ah"]), stack(it["akv"])], axis=1),
                         preferred_element_type=F32)
            it["lhs2"] = jnp.concatenate([ta[:, :LANES].astype(BF16), it["lhs"][CH:]], axis=0)
            it["cc"] = ta[:, LANES:]
            it["rhs_t"] = jnp.concatenate([it["bt"], it["kt"]], axis=0).astype(BF16)
        yield
        z1 = {k: _bdot_nt(it["lhs2"], st[k]) for k, it in cur.items()}
        yield
        u = {k: z1[k][:CH] + it["cc"] for k, it in cur.items()}
        upd = {k: _bdot_tn(jnp.concatenate([u[k], it["v"]], axis=0), it["rhs_t"]) for k, it in cur.items()}
        yield
        for (d, p), it in cur.items():
            y = z1[d, p][CH:] + jnp.dot(it["a_r"], jnp.concatenate([stack(u[d, p]), it["v_bd"]], axis=0),
                                        preferred_element_type=F32)
            y_refs[d][0, it["rows"], it["sl"]] = y.astype(ACT)
            st[d, p] = st[d, p] * w_tot[d][:, it["sl"]] + jnp.where(bdmask, upd[d, p], 0.0)
        yield

    gla_refs = ((gqf_ref, gkf_ref, gvf_ref, lgf_ref), (gqb_ref, gkb_ref, gvb_ref, lgb_ref))
    o_refs = (of_ref, ob_ref)
    g_pairs = GLA_KW // LANES
    gst = {(d, p): gst_ref[d, p] for d in range(2) for p in range(g_pairs)}
    lo2 = lax.broadcasted_iota(jnp.int32, (LANES, LANES), 1) < CH
    zeros_v = jnp.zeros((CH, GLA_DV), F32)

    def gla_step(step):
        cur = {}
        dec = {}
        for d in range(2):
            c = (NCH - 1 - step) if d == 1 else step
            rows = slice(c * CH, (c + 1) * CH)
            refs = gla_refs[d]
            q, k, v = (refs[i][0, rows, :].astype(F32) for i in range(3))
            lg = refs[3][0, 0, rows, :]
            cum = _split_dot(tri[d], lg)
            tot = jnp.sum(lg, axis=0, keepdims=True)
            dec[d] = jnp.exp(tot)
            qd = q * jnp.exp(cum)
            ki = k * jnp.exp(-cum)
            kend = k * jnp.exp(tot - cum)
            for p in range(g_pairs):
                sl = slice(LANES * p, LANES * (p + 1))
                cur[d, p] = dict(qd=qd[:, sl].astype(BF16), ki=ki[:, sl], kend=kend[:, sl], rows=rows, sl=sl,
                                 v0=v[:, 2 * LANES * p:2 * LANES * p + LANES],
                                 v1=v[:, 2 * LANES * p + LANES:2 * LANES * (p + 1)])
        yield
        for it in cur.values():
            it["att"] = _bdot_nt(it["qd"], _pair_stack(it["ki"], lo))
        yield
        for (d, _), it in cur.items():
            att = jnp.where(m_incl[d], it["att"], 0.0)
            v_bd = jnp.concatenate([jnp.concatenate([it["v0"], zeros_v], axis=1),
                                    jnp.concatenate([zeros_v, it["v1"]], axis=1)], axis=0)
            it["o"] = _bdot(att, v_bd)
            it["upd"] = _bdot_tn(jnp.concatenate([it["v0"], it["v1"]], axis=0), _pair_stack(it["kend"], lo))
        yield
        for (d, p), it in cur.items():
            s = gst[d, p]
            s_bd = jnp.concatenate([jnp.where(lo2, s, 0.0), jnp.where(lo2, 0.0, s)], axis=0)
            o = it["o"] + _bdot_nt(it["qd"], s_bd)
            o_refs[d][0, it["rows"], 2 * LANES * p:2 * LANES * (p + 1)] = o.astype(ACT)
            gst[d, p] = s * dec[d][:, it["sl"]] + it["upd"]
        yield

    n_stages = 12
    n_dep = 3
    gla_stages = (1, 4, 7, 10)
    pipeline = [(scan_step(s), gla_step(s)) for s in range(NCH)]
    for slot in range(n_stages + n_dep * (NCH - 1)):
        for s, (rw_gen, gla_gen) in enumerate(pipeline):
            stage = slot - n_dep * s
            if 0 <= stage < n_stages:
                next(rw_gen)
                if stage in gla_stages:
                    next(gla_gen)
    for (d, p), s in st.items():
        st_ref[d, p] = s
    for (d, p), s in gst.items():
        gst_ref[d, p] = s


def _bidir_specs(c, n_tiles):
    def back(j):
        return jnp.where(j == 0, 0, n_tiles - j)

    tok_f = pl.BlockSpec((1, TB, c), lambda b, j: (b, j, 0))
    tok_b = pl.BlockSpec((1, TB, c), lambda b, j: (b, back(j), 0))
    dir_f = pl.BlockSpec((1, 1, TB, c), lambda b, j: (0, b, j, 0))
    dir_b = pl.BlockSpec((1, 1, TB, c), lambda b, j: (1, b, back(j), 0))
    return tok_f, tok_b, dir_f, dir_b


def _scans(r, kk, v, lw, b, ke, q, k, gv, lg):
    B, N, _ = r.shape
    nt = N // TB
    tok_f, tok_b, dir_f, dir_b = _bidir_specs(RW_WIDTH, nt)
    kf, kb, dkf, dkb = _bidir_specs(GLA_KW, nt)
    out = jax.ShapeDtypeStruct((B, N, RW_WIDTH), ACT)
    assert GLA_WIDTH == RW_WIDTH
    yf, yb, of, ob = pl.pallas_call(
        _scan_kernel,
        out_shape=[out] * 4,
        grid=(B, nt),
        in_specs=[tok_f, tok_f, tok_f, dir_f, dir_f, dir_f, tok_b, tok_b, tok_b, dir_b, dir_b, dir_b,
                  kf, kf, tok_f, dkf, kb, kb, tok_b, dkb],
        out_specs=[tok_f, tok_b, tok_f, tok_b],
        scratch_shapes=[pltpu.VMEM((2, RW_WIDTH // LANES, LANES, LANES), F32),
                        pltpu.VMEM((2, GLA_KW // LANES, GLA_DV, LANES), F32)],
        compiler_params=_cparams(("parallel", "arbitrary")),
        name="scans",
    )(r, kk, v, lw, b, ke, r, kk, v, lw, b, ke, q, k, gv, lg, q, k, gv, lg)
    return (yf, yb), (of, ob)


CONV_K = TB + LANES


def _conv_shift_table():
    t = jnp.arange(TB)[:, None]
    s = jnp.arange(CONV_K)[None, :]
    prev = jnp.where(t == 0, s == TB + 2 * HALO - 1, s == t - 1)
    nxt = jnp.where(t == TB - 1, s == TB, s == t + 1)
    return jnp.stack([prev, nxt]).astype(BF16)


def _glaprep_kernel(n_tiles, p_ref, hp_ref, hn_ref, sh_ref, cw_ref, aup_ref, ab_ref, q_out, k_out, v_out,
                    og_out, lg_out):
    j = pl.program_id(1)
    ub = p_ref[0][:, :GLA_QKV]
    u = ub.astype(F32)
    has_prev = jnp.where(j <= 1, 0.0, 1.0)
    has_next = jnp.where(jnp.logical_or(j == 0, j == n_tiles - 1), 0.0, 1.0)
    ext = jnp.concatenate([ub,
                           (hn_ref[0][:, :GLA_QKV].astype(F32) * has_next).astype(BF16),
                           (hp_ref[0][:, :GLA_QKV].astype(F32) * has_prev).astype(BF16),
                           jnp.zeros((CONV_K - TB - 2 * HALO, GLA_QKV), BF16)], axis=0)
    prev1 = jnp.dot(sh_ref[0], ext, preferred_element_type=F32)
    next1 = jnp.dot(sh_ref[1], ext, preferred_element_type=F32)
    cw = cw_ref[...]
    conv = cw[0:1] * prev1 + cw[1:2] * u + cw[2:3] * next1
    qkv = conv * _sigmoid(conv)
    q_out[0] = (qkv[:, :GLA_KW] * (GLA_DK ** -0.5)).astype(ACT)
    k_out[0] = qkv[:, GLA_KW:2 * GLA_KW].astype(ACT)
    v_out[0] = qkv[:, 2 * GLA_KW:].astype(ACT)
    og_out[0] = p_ref[0][:, GLA_QKV:GLA_QKV + GLA_WIDTH]
    z = _bdot(p_ref[0][:, GLA_QKV + GLA_WIDTH:], aup_ref[...]) + ab_ref[...]
    lg = (jnp.minimum(z, 0.0) - jnp.log1p(jnp.exp(-jnp.abs(z)))) * (1.0 / GLA_GATE_NORM)
    for d in range(2):
        lg_out[d, 0] = lg[:, d * GLA_KW:(d + 1) * GLA_KW]


def _glaprep_call(p_gl, prm):
    B, N, _ = p_gl.shape
    sub = HALO
    hb = TB // sub
    nhb = N // sub
    args = [p_gl, p_gl, p_gl, _conv_shift_table(), prm["conv"], prm["a_up"], prm["a_b"]]
    in_specs = [_tile_spec(GLA_PAD),
                pl.BlockSpec((1, sub, GLA_PAD), lambda b, j: (b, jnp.maximum(j * hb - 1, 0), 0)),
                pl.BlockSpec((1, sub, GLA_PAD), lambda b, j: (b, jnp.minimum(j * hb + hb, nhb - 1), 0)),
                _const_spec((2, TB, CONV_K)),
                _const_spec((3, GLA_QKV)), _const_spec((LANES, 2 * GLA_KW)), _const_spec((1, 2 * GLA_KW))]
    out_shapes = [jax.ShapeDtypeStruct((B, N, GLA_KW), ACT), jax.ShapeDtypeStruct((B, N, GLA_KW), ACT),
                  jax.ShapeDtypeStruct((B, N, GLA_WIDTH), ACT), jax.ShapeDtypeStruct((B, N, GLA_WIDTH), ACT),
                  jax.ShapeDtypeStruct((2, B, N, GLA_KW), F32)]
    out_specs = [_tile_spec(GLA_KW), _tile_spec(GLA_KW), _tile_spec(GLA_WIDTH), _tile_spec(GLA_WIDTH),
                 _dir_tile_spec(GLA_KW)]
    return args, in_specs, out_shapes, out_specs


def _prep_kernel(has_vres, n_tiles, n_rw_in, n_gla_in, n_rw_out, *refs):
    rw_in, refs = refs[:n_rw_in], refs[n_rw_in:]
    gla_in, outs = refs[:n_gla_in], refs[n_gla_in:]
    _rwprep_kernel(has_vres, n_tiles, *rw_in, *outs[:n_rw_out])
    _glaprep_kernel(n_tiles, *gla_in, *outs[n_rw_out:])


def _prep(p_rw, p_gl, prm, gla_prm, v_first):
    B, N, _ = p_rw.shape
    nt = N // TB
    rw_args, rw_in, rw_shapes, rw_out = _rwprep_call(p_rw, prm, v_first)
    gla_args, gla_in, gla_shapes, gla_out = _glaprep_call(p_gl, gla_prm)
    outs = pl.pallas_call(
        functools.partial(_prep_kernel, v_first is not None, nt, len(rw_args), len(gla_args), len(rw_shapes)),
        out_shape=rw_shapes + gla_shapes,
        grid=(B, nt),
        in_specs=rw_in + gla_in,
        out_specs=rw_out + gla_out,
        compiler_params=_cparams(("parallel", "parallel")),
        name="prep",
    )(*rw_args, *gla_args)
    return outs[:len(rw_shapes)], outs[len(rw_shapes):]


def _readout_kernel(group, first, split_src, *refs):
    tok = [refs[i * group:(i + 1) * group] for i in range(7)]
    refs = refs[7 * group:]
    n_res = group + 1 if split_src else group
    res_refs, refs = refs[:n_res], refs[n_res:]
    mod_ref, gnw_ref, gnb_ref, ggn_ref, wout_ref, gpost_ref, bd_ref, xo_ref = refs
    bd = bd_ref[...]
    starts_with_ctx = jnp.logical_and(first == 0, pl.program_id(1) == 0)
    for k in range(group):
        yf_ref, yb_ref, bon_ref, g_ref, of_ref, ob_ref, og_ref = (t[k] for t in tok)
        gate = mod_ref[0, 1, 2:3]
        if split_src:
            x_res = res_refs[k + 1][0]
            if k == 0:
                x_res = jnp.where(starts_with_ctx, res_refs[0][0], x_res)
        else:
            x_res = res_refs[k][0]
        if k == 0 and first == 0:
            gate = jnp.where(starts_with_ctx, mod_ref[0, 0, 2:3], gate)
        y = yf_ref[0].astype(F32) + yb_ref[0].astype(F32)
        mu = _seg_sum(y, bd) * (1.0 / RW_HEAD)
        yc = y - mu
        var = _seg_sum(yc * yc, bd) * (1.0 / RW_HEAD)
        yn = yc * lax.rsqrt(var + RW_GN_EPS) * gnw_ref[...] + gnb_ref[...]
        rw = (yn + bon_ref[0].astype(F32)) * g_ref[0].astype(F32)
        o = of_ref[0].astype(F32) + ob_ref[0].astype(F32)
        og = og_ref[0].astype(F32)
        parts = [rw.astype(BF16)]
        for h in range(GLA_HEADS):
            sl = slice(GLA_DV * h, GLA_DV * (h + 1))
            oh = o[:, sl]
            on = oh * lax.rsqrt(jnp.mean(oh * oh, axis=-1, keepdims=True) + GLA_NORM_EPS)
            ogh = og[:, sl]
            parts.append((on * ggn_ref[:, sl] * (ogh * _sigmoid(ogh))).astype(BF16))
        cat = jnp.concatenate(parts, axis=1)
        mx = jnp.dot(cat, wout_ref[...], preferred_element_type=F32)
        xo_ref[0, k * TB:(k + 1) * TB, :] = x_res + gate * _rms(mx, gpost_ref[...])


def _readout(y, bonus, g, o, og, ctx, x, xc, modtab, prm, latents_only):
    B, N, _ = bonus.shape
    first = 1 if latents_only else 0
    n_tiles = N // TB - first
    group = max(k for k in (4, 3, 2, 1) if n_tiles % k == 0)

    def tiles(c, shift=0):
        return [pl.BlockSpec((1, TB, c), lambda b, j, k=k: (b, jnp.maximum(first + group * j + k + shift, 0), 0))
                for k in range(group)]

    if xc is None:
        res, res_specs = [ctx] + [x] * group, [pl.BlockSpec((1, TB, D), lambda b, j: (b, 0, 0))] + tiles(D, -1)
    else:
        res, res_specs = [xc] * group, tiles(D)
    tok_args, tok_specs = [], []
    for arr in (y[0], y[1], bonus, g, o[0], o[1], og):
        tok_args += [arr] * group
        tok_specs += tiles(arr.shape[-1])
    return pl.pallas_call(
        functools.partial(_readout_kernel, group, first, xc is None),
        out_shape=jax.ShapeDtypeStruct((B, n_tiles * TB, D), F32),
        grid=(B, n_tiles // group),
        in_specs=tok_specs + res_specs + [_wide_mod_spec(),
                  _const_spec((1, RW_WIDTH)), _const_spec((1, RW_WIDTH)), _const_spec((1, GLA_WIDTH)),
                  _const_spec((D, D)), _const_spec((1, D)), _const_spec((RW_WIDTH, RW_WIDTH))],
        out_specs=pl.BlockSpec((1, group * TB, D), lambda b, j: (b, j, 0)),
        compiler_params=_cparams(("parallel", "parallel")),
        name="readout",
    )(*tok_args, *res, modtab, prm["gn_w"], prm["gn_b"], prm["gla_gn_w"],
      prm["w_out"], prm["norm_post"], prm["bd64"])


def _swiglu_acc(hb, wg_ref, wu_ref, wd_ref):
    acc = jnp.zeros((hb.shape[0], D), F32)
    for c in range(D_FF // FCH):
        sl = slice(c * FCH, (c + 1) * FCH)
        gate = jnp.dot(hb, wg_ref[:, sl].astype(BF16), preferred_element_type=F32)
        up = jnp.dot(hb, wu_ref[:, sl].astype(BF16), preferred_element_type=F32)
        act = (gate * _sigmoid(gate) * up).astype(BF16)
        acc = acc + jnp.dot(act, wd_ref[sl, :].astype(BF16), preferred_element_type=F32)
    return acc


def _ffn_kernel(x_ref, mod_ref, gpre_ref, gpost_ref, wg_ref, wu_ref, wd_ref, xo_ref):
    x = x_ref[0]
    hb = (_rms(x, gpre_ref[...]) * (1.0 + _wide_mod(mod_ref, 4)) + _wide_mod(mod_ref, 3)).astype(BF16)
    fx = _swiglu_acc(hb, wg_ref, wu_ref, wd_ref)
    xo_ref[0] = x + _wide_mod(mod_ref, 5) * _rms(fx, gpost_ref[...])


def _single_buffered(shape):
    nd = len(shape)
    return pl.BlockSpec(shape, lambda *_: (0,) * nd, pipeline_mode=pl.Buffered(1))


def _ffn(xc, modtab, g_pre, g_post, wg, wu, wd):
    B, N, _ = xc.shape
    wide = pl.BlockSpec((1, TW, D), lambda b, j: (b, j, 0))
    return pl.pallas_call(
        _ffn_kernel,
        out_shape=jax.ShapeDtypeStruct((B, N, D), F32),
        grid=(B, N // TW),
        in_specs=[wide, _wide_mod_spec(), _const_spec((1, D)), _const_spec((1, D)),
                  _single_buffered((D, D_FF)), _single_buffered((D, D_FF)), _single_buffered((D_FF, D))],
        out_specs=wide,
        compiler_params=_cparams(("parallel", "parallel")),
        name="ffn",
    )(xc, modtab, g_pre, g_post, wg, wu, wd)


MOE_TB = 1024
MOE_SEG = 32
MOE_TM = 512
MOE_R = 2 * MOE_TB + N_EXPERTS * MOE_SEG
MOE_NP = MOE_R // MOE_SEG


def _moe_route_kernel(x_ref, mod_ref, gpre_ref, router_ref, h_ref, info_ref, infot_ref, cnt_ref):
    mod = mod_ref[0, 0]
    h = _rms(x_ref[0], gpre_ref[...]) * (1.0 + mod[4:5]) + mod[3:4]
    hb = h.astype(BF16)
    h_ref[0] = hb
    lane = lax.broadcasted_iota(jnp.int32, (MOE_TB, LANES), 1)
    h_lo = (h - hb.astype(F32)).astype(BF16)
    logits = (jnp.dot(hb, router_ref[0], preferred_element_type=F32)
              + jnp.dot(hb, router_ref[1], preferred_element_type=F32)
              + jnp.dot(h_lo, router_ref[0], preferred_element_type=F32))
    logits = jnp.where(lane < N_EXPERTS, logits, -jnp.inf)
    v1 = jnp.max(logits, axis=-1, keepdims=True)
    i1 = jnp.min(jnp.where(logits == v1, lane, LANES), axis=-1, keepdims=True)
    rest = jnp.where(lane == i1, -jnp.inf, logits)
    v2 = jnp.max(rest, axis=-1, keepdims=True)
    i2 = jnp.min(jnp.where(rest == v2, lane, LANES), axis=-1, keepdims=True)
    ex = jnp.exp(v2 - v1)
    w1 = 1.0 / (1.0 + ex)
    w2 = ex * w1
    e1 = jnp.where(lane == i1, 1.0, 0.0)
    e2 = jnp.where(lane == i2, 1.0, 0.0)
    es = e1 + e2
    t = lax.broadcasted_iota(jnp.int32, (MOE_TB, MOE_TB), 0)
    s = lax.broadcasted_iota(jnp.int32, (MOE_TB, MOE_TB), 1)
    before = jnp.where(s < t, 1.0, 0.0).astype(BF16)
    rank = jnp.dot(before, es.astype(BF16), preferred_element_type=F32)
    cnt = jnp.sum(es, axis=0, keepdims=True)
    segs = jnp.floor((cnt + (MOE_SEG - 1)) * (1.0 / MOE_SEG))
    ea = lax.broadcasted_iota(jnp.int32, (LANES, LANES), 0)
    eb = lax.broadcasted_iota(jnp.int32, (LANES, LANES), 1)
    earlier = jnp.where(ea < eb, 1.0, 0.0).astype(BF16)
    start = jnp.dot(jnp.broadcast_to(segs, (SUBLANES, LANES)).astype(BF16), earlier,
                    preferred_element_type=F32)[0:1] * MOE_SEG
    pos = rank + start
    d1 = jnp.sum(e1 * pos, axis=-1, keepdims=True)
    d2 = jnp.sum(e2 * pos, axis=-1, keepdims=True)
    info = jnp.where(lane == 0, d1, jnp.where(lane == 1, d2, jnp.where(lane == 2, w1, jnp.where(lane == 3, w2, 0.0))))
    info_ref[0] = info
    infot_ref[0] = jnp.transpose(info)[0:SUBLANES]
    cnt_ref[0] = jnp.broadcast_to(cnt, (SUBLANES, LANES))


def _moe_route(xs, modtab, g_pre, router):
    B, S, _ = xs.shape
    per = S // MOE_TB
    nb = B * per
    blk = lambda c: pl.BlockSpec((1, MOE_TB, c), lambda i: (i // per, i % per, 0))
    flat = lambda r, c: pl.BlockSpec((1, r, c), lambda i: (i, 0, 0))
    return pl.pallas_call(
        _moe_route_kernel,
        out_shape=[jax.ShapeDtypeStruct((nb, MOE_TB, D), BF16), jax.ShapeDtypeStruct((nb, MOE_TB, LANES), F32),
                   jax.ShapeDtypeStruct((nb, SUBLANES, MOE_TB), F32),
                   jax.ShapeDtypeStruct((nb, SUBLANES, LANES), F32)],
        grid=(nb,),
        in_specs=[blk(D), pl.BlockSpec((1, 1, 6, D), lambda i: (i // per, 1, 0, 0)),
                  _const_spec((1, D)), _const_spec((2, D, LANES))],
        out_specs=[flat(MOE_TB, D), flat(MOE_TB, LANES), flat(SUBLANES, MOE_TB), flat(SUBLANES, LANES)],
        compiler_params=_cparams(("parallel",)),
        name="moe_route",
    )(xs, modtab, g_pre, router)


def _moe_plan(cnt, n_tiles):
    pc = (cnt + MOE_SEG - 1) // MOE_SEG * MOE_SEG
    inc = jnp.cumsum(pc, axis=1)
    loff = inc - pc
    reg = (jnp.sum(pc, axis=0) + MOE_TM - 1) // MOE_TM * MOE_TM
    gend = jnp.cumsum(reg)
    goff = (gend - reg)[None, :] + jnp.cumsum(pc, axis=0) - pc
    rows = jnp.arange(MOE_NP, dtype=jnp.int32) * MOE_SEG
    e_p = jnp.sum((rows[None, :, None] >= inc[:, None, :]).astype(jnp.int32), axis=-1)
    e_c = jnp.minimum(e_p, N_EXPERTS - 1)
    pick = (e_c[:, :, None] == jnp.arange(N_EXPERTS)[None, None, :]).astype(jnp.int32)
    dst = jnp.sum(pick * (goff - loff)[:, None, :], axis=-1) + rows[None, :]
    dst = jnp.where(e_p < N_EXPERTS, dst, 0).astype(jnp.int32)
    n_valid = (inc[:, -1] // MOE_SEG).astype(jnp.int32)
    trow = jnp.arange(n_tiles, dtype=jnp.int32) * MOE_TM
    te = jnp.sum((trow[:, None] >= gend[None, :]).astype(jnp.int32), axis=-1)
    valid = te < N_EXPERTS
    last = gend[-1] // MOE_TM - 1
    te = jnp.where(valid, te, te[last]).astype(jnp.int32)
    src = jnp.where(valid, jnp.arange(n_tiles, dtype=jnp.int32), last).astype(jnp.int32)
    fresh = valid & jnp.concatenate([jnp.ones((1,), bool), te[1:] != te[:-1]])
    return dst, n_valid, te, src, valid.astype(jnp.int32), fresh.astype(jnp.int32)


def _piece_copy(src_ref, src_row, dst_ref, dst_row, sem):
    return pltpu.make_async_copy(src_ref.at[pl.ds(src_row, MOE_SEG)], dst_ref.at[pl.ds(dst_row, MOE_SEG)], sem)


def _moe_gather_kernel(dst_ref, nv_ref, h_ref, infot_ref, xg_in_ref, xg_ref, buf_ref, sem_ref):
    del xg_in_ref
    i = pl.program_id(0)
    it = infot_ref[0]
    rr = lax.broadcasted_iota(jnp.int32, (MOE_R, MOE_TB), 0).astype(F32)
    onehot = jnp.where(rr == it[0:1], 1.0, jnp.where(rr == it[1:2], 1.0, 0.0)).astype(BF16)
    buf_ref[...] = jnp.dot(onehot, h_ref[0], preferred_element_type=F32).astype(BF16)
    nv = nv_ref[i]

    def piece(p):
        return _piece_copy(buf_ref, pl.multiple_of(p * MOE_SEG, MOE_SEG),
                           xg_ref, pl.multiple_of(dst_ref[i, p], MOE_SEG), sem_ref.at[p])

    def start(p, c):
        piece(p).start()
        return c

    def wait(p, c):
        piece(p).wait()
        return c

    lax.fori_loop(0, nv, start, 0)
    lax.fori_loop(0, nv, wait, 0)


def _moe_gather(dst, n_valid, h, info_t, n_rows):
    nb = h.shape[0]
    flat = lambda r, c: pl.BlockSpec((1, r, c), lambda i, *_: (i, 0, 0))
    return pl.pallas_call(
        _moe_gather_kernel,
        out_shape=jax.ShapeDtypeStruct((n_rows, D), BF16),
        grid_spec=pltpu.PrefetchScalarGridSpec(
            num_scalar_prefetch=2, grid=(nb,),
            in_specs=[flat(MOE_TB, D), flat(SUBLANES, MOE_TB), pl.BlockSpec(memory_space=pl.ANY)],
            out_specs=pl.BlockSpec(memory_space=pl.ANY),
            scratch_shapes=[pltpu.VMEM((MOE_R, D), BF16), pltpu.SemaphoreType.DMA((MOE_NP,))]),
        input_output_aliases={4: 0},
        compiler_params=_cparams(("arbitrary",)),
        name="moe_gather",
    )(dst, n_valid, h, info_t, jnp.zeros((n_rows, D), BF16))


N_FCH = D_FF // FCH


def _moe_ffn_kernel(te_ref, src_ref, valid_ref, fresh_ref, x_ref, wg_hbm, wu_hbm, wd_hbm, y_ref,
                    wg_ref, wu_ref, wd_ref, sem_ref):
    i = pl.program_id(0)
    e = te_ref[i]
    fresh = fresh_ref[i] == 1

    def slice_copies(c):
        cols = pl.ds(c * FCH, FCH)
        return (pltpu.make_async_copy(wg_hbm.at[e, :, cols], wg_ref.at[:, cols], sem_ref.at[0, c]),
                pltpu.make_async_copy(wu_hbm.at[e, :, cols], wu_ref.at[:, cols], sem_ref.at[1, c]),
                pltpu.make_async_copy(wd_hbm.at[e, cols, :], wd_ref.at[cols, :], sem_ref.at[2, c]))

    @pl.when(fresh)
    def _():
        for c in range(N_FCH):
            for cp in slice_copies(c):
                cp.start()
        hb = x_ref[...]
        acc = jnp.zeros((MOE_TM, D), F32)
        for c in range(N_FCH):
            for cp in slice_copies(c):
                cp.wait()
            sl = slice(c * FCH, (c + 1) * FCH)
            gate = jnp.dot(hb, wg_ref[:, sl].astype(BF16), preferred_element_type=F32)
            up = jnp.dot(hb, wu_ref[:, sl].astype(BF16), preferred_element_type=F32)
            act = (gate * _sigmoid(gate) * up).astype(BF16)
            acc = acc + jnp.dot(act, wd_ref[sl, :].astype(BF16), preferred_element_type=F32)
        y_ref[...] = acc.astype(BF16)

    @pl.when(jnp.logical_and(valid_ref[i] == 1, jnp.logical_not(fresh)))
    def _():
        y_ref[...] = _swiglu_acc(x_ref[...], wg_ref, wu_ref, wd_ref).astype(BF16)

    @pl.when(valid_ref[i] == 0)
    def _():
        y_ref[...] = jnp.zeros_like(y_ref)


def _moe_ffn(te, src, valid, fresh, xg, wg, wu, wd):
    n_rows = xg.shape[0]
    hbm = pl.BlockSpec(memory_space=pl.ANY)
    return pl.pallas_call(
        _moe_ffn_kernel,
        out_shape=jax.ShapeDtypeStruct((n_rows, D), BF16),
        grid_spec=pltpu.PrefetchScalarGridSpec(
            num_scalar_prefetch=4, grid=(n_rows // MOE_TM,),
            in_specs=[pl.BlockSpec((MOE_TM, D), lambda i, te, src, *_: (src[i], 0)), hbm, hbm, hbm],
            out_specs=pl.BlockSpec((MOE_TM, D), lambda i, *_: (i, 0)),
            scratch_shapes=[pltpu.VMEM((D, D_FF), F32), pltpu.VMEM((D, D_FF), F32), pltpu.VMEM((D_FF, D), F32),
                            pltpu.SemaphoreType.DMA((3, N_FCH))]),
        compiler_params=_cparams(("arbitrary",)),
        name="moe_ffn",
    )(te, src, valid, fresh, xg, wg, wu, wd)


def _moe_combine_kernel(dst_ref, nv_ref, x_ref, mod_ref, gpost_ref, info_ref, yg_ref, xo_ref, buf_ref, sem_ref):
    i = pl.program_id(0)
    nv = nv_ref[i]

    def piece(p):
        return _piece_copy(yg_ref, pl.multiple_of(dst_ref[i, p], MOE_SEG),
                           buf_ref, pl.multiple_of(p * MOE_SEG, MOE_SEG), sem_ref.at[p])

    def start(p, c):
        piece(p).start()
        return c

    def clear(p, c):
        buf_ref[pl.ds(pl.multiple_of(p * MOE_SEG, MOE_SEG), MOE_SEG), :] = jnp.zeros((MOE_SEG, D), BF16)
        return c

    def wait(p, c):
        piece(p).wait()
        return c

    lax.fori_loop(0, nv, start, 0)
    lax.fori_loop(nv, MOE_NP, clear, 0)
    info = info_ref[0]
    rr = lax.broadcasted_iota(jnp.int32, (MOE_TB, MOE_R), 1).astype(F32)
    comb = jnp.where(rr == info[:, 0:1], info[:, 2:3], jnp.where(rr == info[:, 1:2], info[:, 3:4], 0.0)).astype(BF16)
    lax.fori_loop(0, nv, wait, 0)
    fx = jnp.dot(comb, buf_ref[...], preferred_element_type=F32)
    xo_ref[0] = x_ref[0] + mod_ref[0, 0][5:6] * _rms(fx, gpost_ref[...])


def _moe_combine(dst, n_valid, xs, modtab, g_post, info, yg):
    B, S, _ = xs.shape
    per = S // MOE_TB
    blk = pl.BlockSpec((1, MOE_TB, D), lambda i, *_: (i // per, i % per, 0))
    return pl.pallas_call(
        _moe_combine_kernel,
        out_shape=jax.ShapeDtypeStruct((B, S, D), F32),
        grid_spec=pltpu.PrefetchScalarGridSpec(
            num_scalar_prefetch=2, grid=(B * per,),
            in_specs=[blk, pl.BlockSpec((1, 1, 6, D), lambda i, *_: (i // per, 1, 0, 0)),
                      pl.BlockSpec((1, D), lambda i, *_: (0, 0)),
                      pl.BlockSpec((1, MOE_TB, LANES), lambda i, *_: (i, 0, 0)),
                      pl.BlockSpec(memory_space=pl.ANY)],
            out_specs=blk,
            scratch_shapes=[pltpu.VMEM((MOE_R, D), BF16), pltpu.SemaphoreType.DMA((MOE_NP,))]),
        compiler_params=_cparams(("arbitrary",)),
        name="moe_combine",
    )(dst, n_valid, xs, modtab, g_post, info, yg)


def _moe(xs, modtab, g_pre, g_post, router, wg, wu, wd):
    B, S, _ = xs.shape
    nb = B * S // MOE_TB
    h, info, info_t, cnt = _moe_route(xs, modtab, g_pre, router)
    worst = 2 * B * S + nb * N_EXPERTS * (MOE_SEG - 1) + N_EXPERTS * (MOE_TM - 1)
    n_tiles = -(-worst // MOE_TM)
    dst, n_valid, te, src, valid, fresh = _moe_plan(cnt[:, 0, :N_EXPERTS].astype(jnp.int32), n_tiles)
    xg = _moe_gather(dst, n_valid, h, info_t, n_tiles * MOE_TM)
    yg = _moe_ffn(te, src, valid, fresh, xg, wg, wu, wd)
    return _moe_combine(dst, n_valid, xs, modtab, g_post, info, yg)


def _block_diag2(w):
    z = jnp.zeros_like(w[0])
    return jnp.concatenate([jnp.concatenate([w[0], z], axis=1), jnp.concatenate([z, w[1]], axis=1)], axis=0)


def _row(v):
    return v.reshape(1, -1).astype(F32)


def _head_ones(width, head):
    i = jnp.arange(width) // head
    return (i[:, None] == i[None, :]).astype(BF16)


def kernel(x, c, ctx, c_ctx, ada_w, ada_b, norm_mix_pre, norm_mix_post, norm_ffn_pre, norm_ffn_post, w_in, shift_mu, rw_w_up, rw_w0, rw_a_up, rw_a0, rw_k_k, rw_k_a, rw_r_k, rw_g_up, rw_gn_w, rw_gn_b, rw_v_down, rw_v_up, rw_v0, gla_conv, gla_a_up, gla_a_b, gla_gn_w, w_out, ffn_w_gate, ffn_w_up, ffn_w_down, moe_router, moe_w_gate, moe_w_up, moe_w_down):
    B, S, _ = x.shape
    n_ctx = ctx.shape[1]
    depth = w_in.shape[0]
    assert n_ctx == TB and S % MOE_TB == 0 and (n_ctx + S) % TW == 0 and depth == 2

    xc = None
    pad_rows = -(B + 1) % SUBLANES
    cvec = jnp.concatenate([c, c_ctx[None, :], jnp.zeros((pad_rows, D), F32)], axis=0)
    bd64 = _head_ones(RW_WIDTH, RW_HEAD)
    ada_b3 = ada_b.reshape(depth, 1, 6 * D)
    v_first = None
    out = None
    for i in range(depth):
        last = i == depth - 1
        mods = _adaln(cvec, ada_w, ada_b3, i)
        mod_x = mods[:B].reshape(B, 6, D)
        mod_c = jnp.broadcast_to(mods[B].reshape(1, 6, D), (B, 6, D))
        modtab = jnp.stack([mod_c, mod_x], axis=1)

        p_rw, p_gl = _inproj(ctx, x, xc, _row(norm_mix_pre[i]), modtab, w_in, i)

        prm = dict(
            mu=_row(shift_mu[i]),
            w_up=_block_diag2(rw_w_up[i]).astype(BF16), w0=_row(rw_w0[i]),
            a_up=_block_diag2(rw_a_up[i]).astype(BF16), a0=_row(rw_a0[i]),
            k_k=_row(rw_k_k[i]), k_a=_row(rw_k_a[i]), r_k=_row(rw_r_k[i]),
            g_up=rw_g_up[i].astype(BF16), bd64=bd64,
            gn_w=_row(rw_gn_w[i]), gn_b=_row(rw_gn_b[i]), gla_gn_w=_row(gla_gn_w[i]),
            w_out=w_out[i].astype(BF16), norm_post=_row(norm_mix_post[i]),
        )
        gate_pad = jnp.zeros((LANES - 2 * GLA_GATE_RANK, 2 * GLA_KW), F32)
        gla_prm = dict(conv=gla_conv[i].astype(F32), a_b=_row(gla_a_b[i]),
                       a_up=jnp.concatenate([_block_diag2(gla_a_up[i]), gate_pad], axis=0).astype(BF16))
        if i > 0:
            pad = LANES - RW_V_RANK
            prm["v_down"] = jnp.concatenate([rw_v_down[i - 1], jnp.zeros((RW_WIDTH, pad), F32)], axis=1).astype(BF16)
            prm["v_up"] = jnp.concatenate([rw_v_up[i - 1], jnp.zeros((pad, RW_WIDTH), F32)], axis=0).astype(BF16)
            prm["v0"] = _row(rw_v0[i - 1])

        (r, kk, vm, g, bonus, lw, bb, ke), (q, k, gv, og, lg) = _prep(p_rw, p_gl, prm, gla_prm,
                                                                     v_first if i > 0 else None)
        if i == 0:
            v_first = vm
        y, o = _scans(r, kk, vm, lw, bb, ke, q, k, gv, lg)
        xc = _readout(y, bonus, g, o, og, ctx, x, xc, modtab, prm, latents_only=last)

        jf = i // 2
        if i % 2 == 0:
            xc = _ffn(xc, modtab, _row(norm_ffn_pre[i]), _row(norm_ffn_post[i]),
                      ffn_w_gate[jf], ffn_w_up[jf], ffn_w_down[jf])
        else:
            router = jnp.concatenate([moe_router[jf], jnp.zeros((D, LANES - N_EXPERTS), F32)], axis=1)
            r_hi = router.astype(BF16)
            router = jnp.stack([r_hi, (router - r_hi.astype(F32)).astype(BF16)])
            out = _moe(xc, modtab, _row(norm_ffn_pre[i]), _row(norm_ffn_post[i]), router,
                       moe_w_gate[jf], moe_w_up[jf], moe_w_down[jf])
    return out
```

```python
import functools
import math

import jax
import jax.numpy as jnp
from jax import lax
from jax.experimental import pallas as pl
from jax.experimental.pallas import tpu as pltpu

F32, BF16 = jnp.float32, jnp.bfloat16
ACT = BF16

D = 1024
GRID_W = 64
RW_WIDTH = 512
RW_HEAD = 64
RW_RANK = 64
RW_G_RANK = 128
RW_V_RANK = 32
RW_GN_EPS = 64e-5
GLA_WIDTH = 512
GLA_HEADS = 4
GLA_DV = 128
GLA_DK = 64
GLA_KW = 256
GLA_GATE_RANK = 16
GLA_GATE_NORM = 16.0
GLA_NORM_EPS = 1e-5
D_FF = 2816
N_EXPERTS = 8
NORM_EPS = 1e-6
RW_COLS = 3 * RW_WIDTH + 4 * RW_RANK + RW_G_RANK
GLA_QKV = 2 * GLA_KW + GLA_WIDTH
GLA_COLS = GLA_QKV + GLA_WIDTH + 2 * GLA_GATE_RANK
GLA_PAD = 1664

LANES = 128
SUBLANES = 8
TB = 256
TW = 3 * TB
CH = 64
FCH = 256
HALO = 16
VMEM_LIMIT = 56 * 1024 * 1024


def _cparams(sem):
    return pltpu.CompilerParams(dimension_semantics=sem, vmem_limit_bytes=VMEM_LIMIT)


def _bdot(a, b):
    return jnp.dot(a.astype(BF16), b.astype(BF16), preferred_element_type=F32)


def _bdot_nt(a, b):
    return lax.dot_general(a.astype(BF16), b.astype(BF16), (((1,), (1,)), ((), ())),
                           preferred_element_type=F32)


def _bdot_tn(a, b):
    return lax.dot_general(a.astype(BF16), b.astype(BF16), (((0,), (0,)), ((), ())),
                           preferred_element_type=F32)


def _split_dot(a_exact, x):
    h1 = x.astype(BF16)
    r1 = x - h1.astype(F32)
    h2 = r1.astype(BF16)
    h3 = (r1 - h2.astype(F32)).astype(BF16)
    return (jnp.dot(a_exact, h1, preferred_element_type=F32)
            + jnp.dot(a_exact, h2, preferred_element_type=F32)
            + jnp.dot(a_exact, h3, preferred_element_type=F32))


def _seg_sum(x, bd, exact=True):
    hi = x.astype(BF16)
    out = jnp.dot(hi, bd, preferred_element_type=F32)
    if exact:
        lo = (x - hi.astype(F32)).astype(BF16)
        out = out + jnp.dot(lo, bd, preferred_element_type=F32)
    return out


def _sigmoid(x):
    return jax.nn.sigmoid(x)


def _rms(x, g):
    return x * lax.rsqrt(jnp.mean(x * x, axis=-1, keepdims=True) + NORM_EPS) * g


def _pair_stack(z, lo):
    return jnp.concatenate([jnp.where(lo, z, 0.0), jnp.where(lo, 0.0, z)], axis=0)


def _adaln_kernel(c_ref, w_ref, b_ref, o_ref):
    c = c_ref[...]
    s = c * _sigmoid(c)
    o_ref[...] = jnp.dot(s, w_ref[0], precision=lax.Precision.HIGHEST,
                         preferred_element_type=F32) + b_ref[0]


def _adaln(cvec, w, b, layer):
    rows = cvec.shape[0]
    n = w.shape[2]
    return pl.pallas_call(
        _adaln_kernel,
        out_shape=jax.ShapeDtypeStruct((rows, n), F32),
        grid=(n // D,),
        in_specs=[pl.BlockSpec((rows, D), lambda i: (0, 0)),
                  pl.BlockSpec((1, D, D), lambda i: (layer, 0, i)),
                  pl.BlockSpec((1, 1, D), lambda i: (layer, 0, i))],
        out_specs=pl.BlockSpec((rows, D), lambda i: (0, i)),
        compiler_params=_cparams(("arbitrary",)),
        name="adaln",
    )(cvec, w, b)


def _wide_mod(mod_ref, k):
    is_ctx = jnp.logical_and(pl.program_id(1) == 0, lax.broadcasted_iota(jnp.int32, (TW, 1), 0) < TB)
    return jnp.where(is_ctx, mod_ref[0, 0, k:k + 1], mod_ref[0, 1, k:k + 1])


def _wide_rows(refs):
    if len(refs) == 1:
        return refs[0][0]
    first = jnp.where(pl.program_id(1) == 0, refs[0][0], refs[1][0])
    return jnp.concatenate([first] + [r[0] for r in refs[2:]], axis=0)


def _inproj_kernel(n_src, *refs):
    g_ref, mod_ref, w_ref, prw_ref, pgl_ref = refs[n_src:]
    x = _wide_rows(refs[:n_src])
    h = _rms(x, g_ref[...]) * (1.0 + _wide_mod(mod_ref, 1)) + _wide_mod(mod_ref, 0)
    hb = h.astype(BF16)
    prw_ref[0] = jnp.dot(hb, w_ref[0, :, :RW_COLS].astype(BF16), preferred_element_type=F32).astype(ACT)
    pgl_ref[0, :, :GLA_COLS] = jnp.dot(hb, w_ref[0, :, RW_COLS:].astype(BF16),
                                       preferred_element_type=F32).astype(ACT)
    pgl_ref[0, :, GLA_COLS:] = jnp.zeros((TW, GLA_PAD - GLA_COLS), ACT)


def _wide_src(ctx, x, xc):
    if xc is not None:
        return [xc], [pl.BlockSpec((1, TW, D), lambda b, j: (b, j, 0))]
    per = TW // TB
    specs = [pl.BlockSpec((1, TB, D), lambda b, j: (b, 0, 0))]
    for k in range(per):
        specs.append(pl.BlockSpec((1, TB, D), lambda b, j, k=k: (b, jnp.maximum(j * per + k - 1, 0), 0)))
    return [ctx] + [x] * per, specs


def _wide_mod_spec():
    return pl.BlockSpec((1, 2, 6, D), lambda b, j: (b, 0, 0, 0))


def _const_spec(shape):
    nd = len(shape)
    return pl.BlockSpec(shape, lambda *_: (0,) * nd)


def _tile_spec(c):
    return pl.BlockSpec((1, TB, c), lambda b, j: (b, j, 0))


def _dir_tile_spec(c):
    return pl.BlockSpec((2, 1, TB, c), lambda b, j: (0, b, j, 0))


def _inproj(ctx, x, xc, g, modtab, w, layer):
    B = modtab.shape[0]
    N = xc.shape[1] if xc is not None else ctx.shape[1] + x.shape[1]
    srcs, src_specs = _wide_src(ctx, x, xc)
    wide = lambda c: pl.BlockSpec((1, TW, c), lambda b, j: (b, j, 0))
    return pl.pallas_call(
        functools.partial(_inproj_kernel, len(srcs)),
        out_shape=[jax.ShapeDtypeStruct((B, N, RW_COLS), ACT),
                   jax.ShapeDtypeStruct((B, N, GLA_PAD), ACT)],
        grid=(B, N // TW),
        in_specs=src_specs + [_const_spec((1, D)), _wide_mod_spec(),
                              pl.BlockSpec((1, D, RW_COLS + GLA_COLS), lambda b, j: (layer, 0, 0),
                                           pipeline_mode=pl.Buffered(1))],
        out_specs=[wide(RW_COLS), wide(GLA_PAD)],
        compiler_params=_cparams(("parallel", "parallel")),
        name="inproj",
    )(*srcs, g, modtab, w)


def _shift_table():
    t = jnp.arange(TB)
    prev = t[:, None] - 1 == t[None, :]
    nxt = t[:, None] + 1 == t[None, :]
    col = (t % GRID_W)[:, None]
    return jnp.stack([jnp.stack([prev, nxt]),
                      jnp.stack([prev & (col != 0), nxt & (col != GRID_W - 1)])]).astype(BF16)


def _rwprep_kernel(has_vres, n_tiles, p_ref, hu_ref, hd_ref, sh_ref, mu_ref, wup_ref, w0_ref, aup_ref,
                   a0_ref, kk_ref, ka_ref, rk_ref, gup_ref, bd_ref, *rest):
    if has_vres:
        vf_ref, vdn_ref, vup_ref, v0_ref = rest[:4]
        rest = rest[4:]
    r_out, kk_out, v_out, g_out, bon_out, lw_out, b_out, ke_out = rest
    j = pl.program_id(1)
    pb = p_ref[0]
    p = pb.astype(F32)
    prev1 = jnp.dot(sh_ref[0, 0], pb, preferred_element_type=F32)
    next1 = jnp.dot(sh_ref[0, 1], pb, preferred_element_type=F32)
    has_upper = jnp.where(j == 1, 0.0, 1.0)
    has_lower = jnp.where(j == n_tiles - 1, 0.0, 1.0)
    up = jnp.concatenate([hu_ref[0].astype(F32) * has_upper, p[:TB - GRID_W]], axis=0)
    down = jnp.concatenate([p[GRID_W:], hd_ref[0].astype(F32) * has_lower], axis=0)
    cls = lax.broadcasted_iota(jnp.int32, p.shape, 1) & jnp.where(j == 0, 1, 3)
    shifted = jnp.where(cls == 0, prev1, jnp.where(cls == 1, next1, jnp.where(cls == 2, up, down)))
    u = p + mu_ref[...] * (shifted - p)

    r = u[:, 0:RW_WIDTH]
    k = u[:, RW_WIDTH:2 * RW_WIDTH]
    v = u[:, 2 * RW_WIDTH:3 * RW_WIDTH]
    o = 3 * RW_WIDTH
    wd = u[:, o:o + 2 * RW_RANK]
    ad = u[:, o + 2 * RW_RANK:o + 4 * RW_RANK]
    gd = u[:, o + 4 * RW_RANK:]

    w_logit = w0_ref[...] + _bdot(jnp.tanh(wd), wup_ref[...])
    lw = -math.exp(-0.5) * _sigmoid(w_logit)
    a = _sigmoid(a0_ref[...] + _bdot(ad, aup_ref[...]))
    bd = bd_ref[...]
    kk = k * kk_ref[...]
    kk = kk * lax.rsqrt(jnp.maximum(_seg_sum(kk * kk, bd), 1e-24))
    g = _bdot(_sigmoid(gd), gup_ref[...])
    if has_vres:
        gate = _sigmoid(v0_ref[...] + _bdot(_bdot(v, vdn_ref[...]), vup_ref[...]))
        vm = v + (vf_ref[0].astype(F32) - v) * gate
    else:
        vm = v
    ke_sum = jnp.zeros_like(k)
    for d in range(2):
        a_d = a[:, d * RW_WIDTH:(d + 1) * RW_WIDTH]
        ke_d = k * (1.0 + (a_d - 1.0) * ka_ref[...])
        lw_out[d, 0] = lw[:, d * RW_WIDTH:(d + 1) * RW_WIDTH]
        b_out[d, 0] = (kk * a_d).astype(ACT)
        ke_out[d, 0] = ke_d.astype(ACT)
        ke_sum = ke_sum + ke_d
    r_out[0] = r.astype(ACT)
    kk_out[0] = kk.astype(ACT)
    v_out[0] = vm.astype(ACT)
    g_out[0] = g.astype(ACT)
    bon_out[0] = (_seg_sum(r * ke_sum * rk_ref[...], bd, exact=False) * vm).astype(ACT)


def _rwprep_call(p_rw, prm, v_first):
    B, N, _ = p_rw.shape
    hb = TB // GRID_W
    nhb = N // GRID_W
    has_vres = v_first is not None
    W2 = 2 * RW_WIDTH
    in_specs = [
        _tile_spec(RW_COLS),
        pl.BlockSpec((1, GRID_W, RW_COLS), lambda b, j: (b, jnp.maximum(j * hb - 1, 0), 0)),
        pl.BlockSpec((1, GRID_W, RW_COLS), lambda b, j: (b, jnp.minimum(j * hb + hb, nhb - 1), 0)),
        pl.BlockSpec((1, 2, TB, TB), lambda b, j: (jnp.minimum(j, 1), 0, 0, 0)),
        _const_spec((1, RW_COLS)), _const_spec((2 * RW_RANK, W2)), _const_spec((1, W2)),
        _const_spec((2 * RW_RANK, W2)), _const_spec((1, W2)),
        _const_spec((1, RW_WIDTH)), _const_spec((1, RW_WIDTH)), _const_spec((1, RW_WIDTH)),
        _const_spec((RW_G_RANK, RW_WIDTH)), _const_spec((RW_WIDTH, RW_WIDTH)),
    ]
    args = [p_rw, p_rw, p_rw, _shift_table(), prm["mu"], prm["w_up"], prm["w0"], prm["a_up"], prm["a0"],
            prm["k_k"], prm["k_a"], prm["r_k"], prm["g_up"], prm["bd64"]]
    if has_vres:
        in_specs += [_tile_spec(RW_WIDTH), _const_spec((RW_WIDTH, LANES)), _const_spec((LANES, RW_WIDTH)),
                     _const_spec((1, RW_WIDTH))]
        args += [v_first, prm["v_down"], prm["v_up"], prm["v0"]]
    tok = jax.ShapeDtypeStruct((B, N, RW_WIDTH), ACT)
    dtok = jax.ShapeDtypeStruct((2, B, N, RW_WIDTH), ACT)
    out_shapes = [tok] * 5 + [jax.ShapeDtypeStruct((2, B, N, RW_WIDTH), F32), dtok, dtok]
    out_specs = [_tile_spec(RW_WIDTH)] * 5 + [_dir_tile_spec(RW_WIDTH)] * 3
    return args, in_specs, out_shapes, out_specs


NCH = TB // CH


def _scan_kernel(rf_ref, kkf_ref, vf_ref, lwf_ref, bf_ref, kef_ref,
                 rb_ref, kkb_ref, vb_ref, lwb_ref, bb_ref, keb_ref,
                 gqf_ref, gkf_ref, gvf_ref, lgf_ref, gqb_ref, gkb_ref, gvb_ref, lgb_ref,
                 yf_ref, yb_ref, of_ref, ob_ref, st_ref, gst_ref):
    j = pl.program_id(1)

    @pl.when(j == 0)
    def _():
        st_ref[...] = jnp.zeros_like(st_ref)
        gst_ref[...] = jnp.zeros_like(gst_ref)

    ti = lax.broadcasted_iota(jnp.int32, (CH, LANES), 0)
    li = lax.broadcasted_iota(jnp.int32, (CH, LANES), 1)
    si = li & (CH - 1)
    lo = li < CH
    eye = jnp.where(ti == si, 1.0, 0.0)
    rblk = lax.broadcasted_iota(jnp.int32, (LANES, LANES), 0) // CH
    cblk = lax.broadcasted_iota(jnp.int32, (LANES, LANES), 1) // CH
    bdmask = rblk == cblk
    t64 = lax.broadcasted_iota(jnp.int32, (CH, CH), 0)
    s64 = lax.broadcasted_iota(jnp.int32, (CH, CH), 1)
    tri = [jnp.where(s64 <= t64, 1.0, 0.0).astype(BF16), jnp.where(s64 >= t64, 1.0, 0.0).astype(BF16)]
    m_incl = [si <= ti, si >= ti]
    m_strict = [si < ti, si > ti]
    n_pairs = RW_WIDTH // LANES
    in_refs = ((rf_ref, kkf_ref, vf_ref, lwf_ref, bf_ref, kef_ref),
               (rb_ref, kkb_ref, vb_ref, lwb_ref, bb_ref, keb_ref))
    y_refs = (yf_ref, yb_ref)

    def stack(z):
        return _pair_stack(z, lo).astype(BF16)

    st = {(d, p): st_ref[d, p] for d in range(2) for p in range(n_pairs)}

    def scan_step(step):
        cur = {}
        w_tot = {}
        for d in range(2):
            c = (NCH - 1 - step) if d == 1 else step
            rows = slice(c * CH, (c + 1) * CH)
            refs = in_refs[d]
            r, kk, v = (refs[i][0, rows, :].astype(F32) for i in range(3))
            lw = refs[3][0, 0, rows, :]
            b, ke = refs[4][0, 0, rows, :].astype(F32), refs[5][0, 0, rows, :].astype(F32)
            cum = _split_dot(tri[d], lw)
            tot = jnp.sum(lw, axis=0, keepdims=True)
            w_inv = jnp.exp(-cum)
            w_end = jnp.exp(tot - cum)
            w_tot[d] = jnp.exp(tot)
            rh = r * jnp.exp(cum)
            ah = -(kk * jnp.exp(cum - lw))
            bh, kh, bt, kt = b * w_inv, ke * w_inv, b * w_end, ke * w_end
            for p in range(n_pairs):
                sl = slice(LANES * p, LANES * (p + 1))
                cur[d, p] = dict(ah=ah[:, sl], rh=rh[:, sl], bh=bh[:, sl], kh=kh[:, sl], bt=bt[:, sl],
                                 kt=kt[:, sl], v=v[:, sl], rows=rows, sl=sl)
        yield
        for it in cur.values():
            it["lhs"] = jnp.concatenate([it["ah"], it["rh"]], axis=0).astype(BF16)
            rhs = jnp.concatenate([stack(it["bh"]), stack(it["kh"])], axis=0)
            it["a_all"] = _bdot_nt(it["lhs"], rhs)
        yield
        for (d, _), it in cur.items():
            a_all = it.pop("a_all")
            it["a_ab"] = jnp.where(m_strict[d], a_all[:CH, :LANES], 0.0)
            a_ak = jnp.where(m_strict[d], a_all[:CH, LANES:], 0.0)
            a_rb = jnp.where(m_incl[d], a_all[CH:, :LANES], 0.0)
            a_rk = jnp.where(m_incl[d], a_all[CH:, LANES:], 0.0)
            it["a_r"] = jnp.concatenate([a_rb, a_rk], axis=1).astype(BF16)
            it["v_bd"] = stack(it["v"])
            it["akv"] = _bdot(a_ak, it["v_bd"])
            it["t"] = eye + it["a_ab"]
            it["m"] = _bdot(it["a_ab"], stack(it["a_ab"]))
        yield
        for _ in range(int(math.log2(CH)) - 2):
            for it in cur.values():
                z = _bdot(it["m"], jnp.concatenate([stack(it["m"]), stack(it["t"])], axis=1))
                it["m"] = z[:, :LANES]
                it["t"] = it["t"] + z[:, LANES:]
            yield
        for it in cur.values():
            it["t"] = (it["t"] + _bdot(it["m"], stack(it["t"]))).astype(BF16)
        yield
        for it in cur.values():
            ta = jnp.dot(it["t"], jnp.concatenate([stack(it["ah"]), stack(it["akv"])], axis=1),
                         preferred_element_type=F32)
            it["lhs2"] = jnp.concatenate([ta[:, :LANES].astype(BF16), it["lhs"][CH:]], axis=0)
            it["cc"] = ta[:, LANES:]
            it["rhs_t"] = jnp.concatenate([it["bt"], it["kt"]], axis=0).astype(BF16)
        yield
        z1 = {k: _bdot_nt(it["lhs2"], st[k]) for k, it in cur.items()}
        yield
        u = {k: z1[k][:CH] + it["cc"] for k, it in cur.items()}
        upd = {k: _bdot_tn(jnp.concatenate([u[k], it["v"]], axis=0), it["rhs_t"]) for k, it in cur.items()}
        yield
        for (d, p), it in cur.items():
            y = z1[d, p][CH:] + jnp.dot(it["a_r"], jnp.concatenate([stack(u[d, p]), it["v_bd"]], axis=0),
                                        preferred_element_type=F32)
            y_refs[d][0, it["rows"], it["sl"]] = y.astype(ACT)
            st[d, p] = st[d, p] * w_tot[d][:, it["sl"]] + jnp.where(bdmask, upd[d, p], 0.0)
        yield

    gla_refs = ((gqf_ref, gkf_ref, gvf_ref, lgf_ref), (gqb_ref, gkb_ref, gvb_ref, lgb_ref))
    o_refs = (of_ref, ob_ref)
    g_pairs = GLA_KW // LANES
    gst = {(d, p): gst_ref[d, p] for d in range(2) for p in range(g_pairs)}
    lo2 = lax.broadcasted_iota(jnp.int32, (LANES, LANES), 1) < CH
    zeros_v = jnp.zeros((CH, GLA_DV), F32)

    def gla_step(step):
        cur = {}
        dec = {}
        for d in range(2):
            c = (NCH - 1 - step) if d == 1 else step
            rows = slice(c * CH, (c + 1) * CH)
            refs = gla_refs[d]
            q, k, v = (refs[i][0, rows, :].astype(F32) for i in range(3))
            lg = refs[3][0, 0, rows, :]
            cum = _split_dot(tri[d], lg)
            tot = jnp.sum(lg, axis=0, keepdims=True)
            dec[d] = jnp.exp(tot)
            qd = q * jnp.exp(cum)
            ki = k * jnp.exp(-cum)
            kend = k * jnp.exp(tot - cum)
            for p in range(g_pairs):
                sl = slice(LANES * p, LANES * (p + 1))
                cur[d, p] = dict(qd=qd[:, sl].astype(BF16), ki=ki[:, sl], kend=kend[:, sl], rows=rows, sl=sl,
                                 v0=v[:, 2 * LANES * p:2 * LANES * p + LANES],
                                 v1=v[:, 2 * LANES * p + LANES:2 * LANES * (p + 1)])
        yield
        for it in cur.values():
            it["att"] = _bdot_nt(it["qd"], _pair_stack(it["ki"], lo))
        yield
        for (d, _), it in cur.items():
            att = jnp.where(m_incl[d], it["att"], 0.0)
            v_bd = jnp.concatenate([jnp.concatenate([it["v0"], zeros_v], axis=1),
                                    jnp.concatenate([zeros_v, it["v1"]], axis=1)], axis=0)
            it["o"] = _bdot(att, v_bd)
            it["upd"] = _bdot_tn(jnp.concatenate([it["v0"], it["v1"]], axis=0), _pair_stack(it["kend"], lo))
        yield
        for (d, p), it in cur.items():
            s = gst[d, p]
            s_bd = jnp.concatenate([jnp.where(lo2, s, 0.0), jnp.where(lo2, 0.0, s)], axis=0)
            o = it["o"] + _bdot_nt(it["qd"], s_bd)
            o_refs[d][0, it["rows"], 2 * LANES * p:2 * LANES * (p + 1)] = o.astype(ACT)
            gst[d, p] = s * dec[d][:, it["sl"]] + it["upd"]
        yield

    n_stages = 12
    n_dep = 3
    gla_stages = (1, 4, 7, 10)
    pipeline = [(scan_step(s), gla_step(s)) for s in range(NCH)]
    for slot in range(n_stages + n_dep * (NCH - 1)):
        for s, (rw_gen, gla_gen) in enumerate(pipeline):
            stage = slot - n_dep * s
            if 0 <= stage < n_stages:
                next(rw_gen)
                if stage in gla_stages:
                    next(gla_gen)
    for (d, p), s in st.items():
        st_ref[d, p] = s
    for (d, p), s in gst.items():
        gst_ref[d, p] = s


def _bidir_specs(c, n_tiles):
    def back(j):
        return jnp.where(j == 0, 0, n_tiles - j)

    tok_f = pl.BlockSpec((1, TB, c), lambda b, j: (b, j, 0))
    tok_b = pl.BlockSpec((1, TB, c), lambda b, j: (b, back(j), 0))
    dir_f = pl.BlockSpec((1, 1, TB, c), lambda b, j: (0, b, j, 0))
    dir_b = pl.BlockSpec((1, 1, TB, c), lambda b, j: (1, b, back(j), 0))
    return tok_f, tok_b, dir_f, dir_b


def _scans(r, kk, v, lw, b, ke, q, k, gv, lg):
    B, N, _ = r.shape
    nt = N // TB
    tok_f, tok_b, dir_f, dir_b = _bidir_specs(RW_WIDTH, nt)
    kf, kb, dkf, dkb = _bidir_specs(GLA_KW, nt)
    out = jax.ShapeDtypeStruct((B, N, RW_WIDTH), ACT)
    assert GLA_WIDTH == RW_WIDTH
    yf, yb, of, ob = pl.pallas_call(
        _scan_kernel,
        out_shape=[out] * 4,
        grid=(B, nt),
        in_specs=[tok_f, tok_f, tok_f, dir_f, dir_f, dir_f, tok_b, tok_b, tok_b, dir_b, dir_b, dir_b,
                  kf, kf, tok_f, dkf, kb, kb, tok_b, dkb],
        out_specs=[tok_f, tok_b, tok_f, tok_b],
        scratch_shapes=[pltpu.VMEM((2, RW_WIDTH // LANES, LANES, LANES), F32),
                        pltpu.VMEM((2, GLA_KW // LANES, GLA_DV, LANES), F32)],
        compiler_params=_cparams(("parallel", "arbitrary")),
        name="scans",
    )(r, kk, v, lw, b, ke, r, kk, v, lw, b, ke, q, k, gv, lg, q, k, gv, lg)
    return (yf, yb), (of, ob)


CONV_K = TB + LANES


def _conv_shift_table():
    t = jnp.arange(TB)[:, None]
    s = jnp.arange(CONV_K)[None, :]
    prev = jnp.where(t == 0, s == TB + 2 * HALO - 1, s == t - 1)
    nxt = jnp.where(t == TB - 1, s == TB, s == t + 1)
    return jnp.stack([prev, nxt]).astype(BF16)


def _glaprep_kernel(n_tiles, p_ref, hp_ref, hn_ref, sh_ref, cw_ref, aup_ref, ab_ref, q_out, k_out, v_out,
                    og_out, lg_out):
    j = pl.program_id(1)
    ub = p_ref[0][:, :GLA_QKV]
    u = ub.astype(F32)
    has_prev = jnp.where(j <= 1, 0.0, 1.0)
    has_next = jnp.where(jnp.logical_or(j == 0, j == n_tiles - 1), 0.0, 1.0)
    ext = jnp.concatenate([ub,
                           (hn_ref[0][:, :GLA_QKV].astype(F32) * has_next).astype(BF16),
                           (hp_ref[0][:, :GLA_QKV].astype(F32) * has_prev).astype(BF16),
                           jnp.zeros((CONV_K - TB - 2 * HALO, GLA_QKV), BF16)], axis=0)
    prev1 = jnp.dot(sh_ref[0], ext, preferred_element_type=F32)
    next1 = jnp.dot(sh_ref[1], ext, preferred_element_type=F32)
    cw = cw_ref[...]
    conv = cw[0:1] * prev1 + cw[1:2] * u + cw[2:3] * next1
    qkv = conv * _sigmoid(conv)
    q_out[0] = (qkv[:, :GLA_KW] * (GLA_DK ** -0.5)).astype(ACT)
    k_out[0] = qkv[:, GLA_KW:2 * GLA_KW].astype(ACT)
    v_out[0] = qkv[:, 2 * GLA_KW:].astype(ACT)
    og_out[0] = p_ref[0][:, GLA_QKV:GLA_QKV + GLA_WIDTH]
    z = _bdot(p_ref[0][:, GLA_QKV + GLA_WIDTH:], aup_ref[...]) + ab_ref[...]
    lg = (jnp.minimum(z, 0.0) - jnp.log1p(jnp.exp(-jnp.abs(z)))) * (1.0 / GLA_GATE_NORM)
    for d in range(2):
        lg_out[d, 0] = lg[:, d * GLA_KW:(d + 1) * GLA_KW]


def _glaprep_call(p_gl, prm):
    B, N, _ = p_gl.shape
    sub = HALO
    hb = TB // sub
    nhb = N // sub
    args = [p_gl, p_gl, p_gl, _conv_shift_table(), prm["conv"], prm["a_up"], prm["a_b"]]
    in_specs = [_tile_spec(GLA_PAD),
                pl.BlockSpec((1, sub, GLA_PAD), lambda b, j: (b, jnp.maximum(j * hb - 1, 0), 0)),
                pl.BlockSpec((1, sub, GLA_PAD), lambda b, j: (b, jnp.minimum(j * hb + hb, nhb - 1), 0)),
                _const_spec((2, TB, CONV_K)),
                _const_spec((3, GLA_QKV)), _const_spec((LANES, 2 * GLA_KW)), _const_spec((1, 2 * GLA_KW))]
    out_shapes = [jax.ShapeDtypeStruct((B, N, GLA_KW), ACT), jax.ShapeDtypeStruct((B, N, GLA_KW), ACT),
                  jax.ShapeDtypeStruct((B, N, GLA_WIDTH), ACT), jax.ShapeDtypeStruct((B, N, GLA_WIDTH), ACT),
                  jax.ShapeDtypeStruct((2, B, N, GLA_KW), F32)]
    out_specs = [_tile_spec(GLA_KW), _tile_spec(GLA_KW), _tile_spec(GLA_WIDTH), _tile_spec(GLA_WIDTH),
                 _dir_tile_spec(GLA_KW)]
    return args, in_specs, out_shapes, out_specs


def _prep_kernel(has_vres, n_tiles, n_rw_in, n_gla_in, n_rw_out, *refs):
    rw_in, refs = refs[:n_rw_in], refs[n_rw_in:]
    gla_in, outs = refs[:n_gla_in], refs[n_gla_in:]
    _rwprep_kernel(has_vres, n_tiles, *rw_in, *outs[:n_rw_out])
    _glaprep_kernel(n_tiles, *gla_in, *outs[n_rw_out:])


def _prep(p_rw, p_gl, prm, gla_prm, v_first):
    B, N, _ = p_rw.shape
    nt = N // TB
    rw_args, rw_in, rw_shapes, rw_out = _rwprep_call(p_rw, prm, v_first)
    gla_args, gla_in, gla_shapes, gla_out = _glaprep_call(p_gl, gla_prm)
    outs = pl.pallas_call(
        functools.partial(_prep_kernel, v_first is not None, nt, len(rw_args), len(gla_args), len(rw_shapes)),
        out_shape=rw_shapes + gla_shapes,
        grid=(B, nt),
        in_specs=rw_in + gla_in,
        out_specs=rw_out + gla_out,
        compiler_params=_cparams(("parallel", "parallel")),
        name="prep",
    )(*rw_args, *gla_args)
    return outs[:len(rw_shapes)], outs[len(rw_shapes):]


def _readout_kernel(group, first, split_src, *refs):
    tok = [refs[i * group:(i + 1) * group] for i in range(7)]
    refs = refs[7 * group:]
    n_res = group + 1 if split_src else group
    res_refs, refs = refs[:n_res], refs[n_res:]
    mod_ref, gnw_ref, gnb_ref, ggn_ref, wout_ref, gpost_ref, bd_ref, xo_ref = refs
    bd = bd_ref[...]
    starts_with_ctx = jnp.logical_and(first == 0, pl.program_id(1) == 0)
    for k in range(group):
        yf_ref, yb_ref, bon_ref, g_ref, of_ref, ob_ref, og_ref = (t[k] for t in tok)
        gate = mod_ref[0, 1, 2:3]
        if split_src:
            x_res = res_refs[k + 1][0]
            if k == 0:
                x_res = jnp.where(starts_with_ctx, res_refs[0][0], x_res)
        else:
            x_res = res_refs[k][0]
        if k == 0 and first == 0:
            gate = jnp.where(starts_with_ctx, mod_ref[0, 0, 2:3], gate)
        y = yf_ref[0].astype(F32) + yb_ref[0].astype(F32)
        mu = _seg_sum(y, bd, exact=False) * (1.0 / RW_HEAD)
        yc = y - mu
        var = _seg_sum(yc * yc, bd, exact=False) * (1.0 / RW_HEAD)
        yn = yc * lax.rsqrt(var + RW_GN_EPS) * gnw_ref[...] + gnb_ref[...]
        rw = (yn + bon_ref[0].astype(F32)) * g_ref[0].astype(F32)
        o = of_ref[0].astype(F32) + ob_ref[0].astype(F32)
        og = og_ref[0].astype(F32)
        parts = [rw.astype(BF16)]
        for h in range(GLA_HEADS):
            sl = slice(GLA_DV * h, GLA_DV * (h + 1))
            oh = o[:, sl]
            on = oh * lax.rsqrt(jnp.mean(oh * oh, axis=-1, keepdims=True) + GLA_NORM_EPS)
            ogh = og[:, sl]
            parts.append((on * ggn_ref[:, sl] * (ogh * _sigmoid(ogh))).astype(BF16))
        cat = jnp.concatenate(parts, axis=1)
        mx = jnp.dot(cat, wout_ref[...], preferred_element_type=F32)
        xo_ref[0, k * TB:(k + 1) * TB, :] = x_res + gate * _rms(mx, gpost_ref[...])


def _readout(y, bonus, g, o, og, ctx, x, xc, modtab, prm, latents_only):
    B, N, _ = bonus.shape
    first = 1 if latents_only else 0
    n_tiles = N // TB - first
    group = max(k for k in (4, 3, 2, 1) if n_tiles % k == 0)

    def tiles(c, shift=0):
        return [pl.BlockSpec((1, TB, c), lambda b, j, k=k: (b, jnp.maximum(first + group * j + k + shift, 0), 0))
                for k in range(group)]

    if xc is None:
        res, res_specs = [ctx] + [x] * group, [pl.BlockSpec((1, TB, D), lambda b, j: (b, 0, 0))] + tiles(D, -1)
    else:
        res, res_specs = [xc] * group, tiles(D)
    tok_args, tok_specs = [], []
    for arr in (y[0], y[1], bonus, g, o[0], o[1], og):
        tok_args += [arr] * group
        tok_specs += tiles(arr.shape[-1])
    return pl.pallas_call(
        functools.partial(_readout_kernel, group, first, xc is None),
        out_shape=jax.ShapeDtypeStruct((B, n_tiles * TB, D), F32),
        grid=(B, n_tiles // group),
        in_specs=tok_specs + res_specs + [_wide_mod_spec(),
                  _const_spec((1, RW_WIDTH)), _const_spec((1, RW_WIDTH)), _const_spec((1, GLA_WIDTH)),
                  _const_spec((D, D)), _const_spec((1, D)), _const_spec((RW_WIDTH, RW_WIDTH))],
        out_specs=pl.BlockSpec((1, group * TB, D), lambda b, j: (b, j, 0)),
        compiler_params=_cparams(("parallel", "parallel")),
        name="readout",
    )(*tok_args, *res, modtab, prm["gn_w"], prm["gn_b"], prm["gla_gn_w"],
      prm["w_out"], prm["norm_post"], prm["bd64"])


def _swiglu_acc(hb, wg_ref, wu_ref, wd_ref):
    acc = jnp.zeros((hb.shape[0], D), F32)
    for c in range(D_FF // FCH):
        sl = slice(c * FCH, (c + 1) * FCH)
        gate = jnp.dot(hb, wg_ref[:, sl].astype(BF16), preferred_element_type=F32)
        up = jnp.dot(hb, wu_ref[:, sl].astype(BF16), preferred_element_type=F32)
        act = (gate * _sigmoid(gate) * up).astype(BF16)
        acc = acc + jnp.dot(act, wd_ref[sl, :].astype(BF16), preferred_element_type=F32)
    return acc


def _ffn_kernel(x_ref, mod_ref, gpre_ref, gpost_ref, wg_ref, wu_ref, wd_ref, xo_ref):
    x = x_ref[0]
    hb = (_rms(x, gpre_ref[...]) * (1.0 + _wide_mod(mod_ref, 4)) + _wide_mod(mod_ref, 3)).astype(BF16)
    fx = _swiglu_acc(hb, wg_ref, wu_ref, wd_ref)
    xo_ref[0] = x + _wide_mod(mod_ref, 5) * _rms(fx, gpost_ref[...])


def _single_buffered(shape):
    nd = len(shape)
    return pl.BlockSpec(shape, lambda *_: (0,) * nd, pipeline_mode=pl.Buffered(1))


def _ffn(xc, modtab, g_pre, g_post, wg, wu, wd):
    B, N, _ = xc.shape
    wide = pl.BlockSpec((1, TW, D), lambda b, j: (b, j, 0))
    return pl.pallas_call(
        _ffn_kernel,
        out_shape=jax.ShapeDtypeStruct((B, N, D), F32),
        grid=(B, N // TW),
        in_specs=[wide, _wide_mod_spec(), _const_spec((1, D)), _const_spec((1, D)),
                  _single_buffered((D, D_FF)), _single_buffered((D, D_FF)), _single_buffered((D_FF, D))],
        out_specs=wide,
        compiler_params=_cparams(("parallel", "parallel")),
        name="ffn",
    )(xc, modtab, g_pre, g_post, wg, wu, wd)


MOE_TB = 1024
MOE_SEG = 16
MOE_TM = 512
MOE_R = 2 * MOE_TB + N_EXPERTS * MOE_SEG
MOE_NP = MOE_R // MOE_SEG


def _moe_route_kernel(x_ref, mod_ref, gpre_ref, router_ref, h_ref, info_ref, infot_ref, cnt_ref):
    mod = mod_ref[0, 0]
    h = _rms(x_ref[0], gpre_ref[...]) * (1.0 + mod[4:5]) + mod[3:4]
    hb = h.astype(BF16)
    h_ref[0] = hb
    lane = lax.broadcasted_iota(jnp.int32, (MOE_TB, LANES), 1)
    h_lo = (h - hb.astype(F32)).astype(BF16)
    logits = (jnp.dot(hb, router_ref[0], preferred_element_type=F32)
              + jnp.dot(hb, router_ref[1], preferred_element_type=F32)
              + jnp.dot(h_lo, router_ref[0], preferred_element_type=F32))
    logits = jnp.where(lane < N_EXPERTS, logits, -jnp.inf)
    v1 = jnp.max(logits, axis=-1, keepdims=True)
    i1 = jnp.min(jnp.where(logits == v1, lane, LANES), axis=-1, keepdims=True)
    rest = jnp.where(lane == i1, -jnp.inf, logits)
    v2 = jnp.max(rest, axis=-1, keepdims=True)
    i2 = jnp.min(jnp.where(rest == v2, lane, LANES), axis=-1, keepdims=True)
    ex = jnp.exp(v2 - v1)
    w1 = 1.0 / (1.0 + ex)
    w2 = ex * w1
    e1 = jnp.where(lane == i1, 1.0, 0.0)
    e2 = jnp.where(lane == i2, 1.0, 0.0)
    es = e1 + e2
    t = lax.broadcasted_iota(jnp.int32, (MOE_TB, MOE_TB), 0)
    s = lax.broadcasted_iota(jnp.int32, (MOE_TB, MOE_TB), 1)
    before = jnp.where(s < t, 1.0, 0.0).astype(BF16)
    rank = jnp.dot(before, es.astype(BF16), preferred_element_type=F32)
    cnt = jnp.sum(es, axis=0, keepdims=True)
    segs = jnp.floor((cnt + (MOE_SEG - 1)) * (1.0 / MOE_SEG))
    ea = lax.broadcasted_iota(jnp.int32, (LANES, LANES), 0)
    eb = lax.broadcasted_iota(jnp.int32, (LANES, LANES), 1)
    earlier = jnp.where(ea < eb, 1.0, 0.0).astype(BF16)
    start = jnp.dot(jnp.broadcast_to(segs, (SUBLANES, LANES)).astype(BF16), earlier,
                    preferred_element_type=F32)[0:1] * MOE_SEG
    pos = rank + start
    d1 = jnp.sum(e1 * pos, axis=-1, keepdims=True)
    d2 = jnp.sum(e2 * pos, axis=-1, keepdims=True)
    info = jnp.where(lane == 0, d1, jnp.where(lane == 1, d2, jnp.where(lane == 2, w1, jnp.where(lane == 3, w2, 0.0))))
    info_ref[0] = info
    infot_ref[0] = jnp.transpose(info)[0:SUBLANES]
    cnt_ref[0] = jnp.broadcast_to(cnt, (SUBLANES, LANES))


def _moe_route(xs, modtab, g_pre, router):
    B, S, _ = xs.shape
    per = S // MOE_TB
    nb = B * per
    blk = lambda c: pl.BlockSpec((1, MOE_TB, c), lambda i: (i // per, i % per, 0))
    flat = lambda r, c: pl.BlockSpec((1, r, c), lambda i: (i, 0, 0))
    return pl.pallas_call(
        _moe_route_kernel,
        out_shape=[jax.ShapeDtypeStruct((nb, MOE_TB, D), BF16), jax.ShapeDtypeStruct((nb, MOE_TB, LANES), F32),
                   jax.ShapeDtypeStruct((nb, SUBLANES, MOE_TB), F32),
                   jax.ShapeDtypeStruct((nb, SUBLANES, LANES), F32)],
        grid=(nb,),
        in_specs=[blk(D), pl.BlockSpec((1, 1, 6, D), lambda i: (i // per, 1, 0, 0)),
                  _const_spec((1, D)), _const_spec((2, D, LANES))],
        out_specs=[flat(MOE_TB, D), flat(MOE_TB, LANES), flat(SUBLANES, MOE_TB), flat(SUBLANES, LANES)],
        compiler_params=_cparams(("parallel",)),
        name="moe_route",
    )(xs, modtab, g_pre, router)


def _moe_plan(cnt, n_tiles):
    pc = (cnt + MOE_SEG - 1) // MOE_SEG * MOE_SEG
    inc = jnp.cumsum(pc, axis=1)
    loff = inc - pc
    reg = (jnp.sum(pc, axis=0) + MOE_TM - 1) // MOE_TM * MOE_TM
    gend = jnp.cumsum(reg)
    goff = (gend - reg)[None, :] + jnp.cumsum(pc, axis=0) - pc
    rows = jnp.arange(MOE_NP, dtype=jnp.int32) * MOE_SEG
    e_p = jnp.sum((rows[None, :, None] >= inc[:, None, :]).astype(jnp.int32), axis=-1)
    e_c = jnp.minimum(e_p, N_EXPERTS - 1)
    pick = (e_c[:, :, None] == jnp.arange(N_EXPERTS)[None, None, :]).astype(jnp.int32)
    dst = jnp.sum(pick * (goff - loff)[:, None, :], axis=-1) + rows[None, :]
    dst = jnp.where(e_p < N_EXPERTS, dst, 0).astype(jnp.int32)
    n_valid = (inc[:, -1] // MOE_SEG).astype(jnp.int32)
    trow = jnp.arange(n_tiles, dtype=jnp.int32) * MOE_TM
    te = jnp.sum((trow[:, None] >= gend[None, :]).astype(jnp.int32), axis=-1)
    valid = te < N_EXPERTS
    last = gend[-1] // MOE_TM - 1
    te = jnp.where(valid, te, te[last]).astype(jnp.int32)
    src = jnp.where(valid, jnp.arange(n_tiles, dtype=jnp.int32), last).astype(jnp.int32)
    fresh = valid & jnp.concatenate([jnp.ones((1,), bool), te[1:] != te[:-1]])
    return dst, n_valid, te, src, valid.astype(jnp.int32), fresh.astype(jnp.int32)


def _piece_copy(src_ref, src_row, dst_ref, dst_row, sem):
    return pltpu.make_async_copy(src_ref.at[pl.ds(src_row, MOE_SEG)], dst_ref.at[pl.ds(dst_row, MOE_SEG)], sem)


def _moe_gather_kernel(dst_ref, nv_ref, h_ref, infot_ref, xg_in_ref, xg_ref, buf_ref, sem_ref):
    del xg_in_ref
    i = pl.program_id(0)
    it = infot_ref[0]
    rr = lax.broadcasted_iota(jnp.int32, (MOE_R, MOE_TB), 0).astype(F32)
    onehot = jnp.where(rr == it[0:1], 1.0, jnp.where(rr == it[1:2], 1.0, 0.0)).astype(BF16)
    buf_ref[...] = jnp.dot(onehot, h_ref[0], preferred_element_type=F32).astype(BF16)
    nv = nv_ref[i]

    def piece(p):
        return _piece_copy(buf_ref, pl.multiple_of(p * MOE_SEG, MOE_SEG),
                           xg_ref, pl.multiple_of(dst_ref[i, p], MOE_SEG), sem_ref.at[p])

    def start(p, c):
        piece(p).start()
        return c

    def wait(p, c):
        piece(p).wait()
        return c

    lax.fori_loop(0, nv, start, 0)
    lax.fori_loop(0, nv, wait, 0)


def _moe_gather(dst, n_valid, h, info_t, n_rows):
    nb = h.shape[0]
    flat = lambda r, c: pl.BlockSpec((1, r, c), lambda i, *_: (i, 0, 0))
    return pl.pallas_call(
        _moe_gather_kernel,
        out_shape=jax.ShapeDtypeStruct((n_rows, D), BF16),
        grid_spec=pltpu.PrefetchScalarGridSpec(
            num_scalar_prefetch=2, grid=(nb,),
            in_specs=[flat(MOE_TB, D), flat(SUBLANES, MOE_TB), pl.BlockSpec(memory_space=pl.ANY)],
            out_specs=pl.BlockSpec(memory_space=pl.ANY),
            scratch_shapes=[pltpu.VMEM((MOE_R, D), BF16), pltpu.SemaphoreType.DMA((MOE_NP,))]),
        input_output_aliases={4: 0},
        compiler_params=_cparams(("arbitrary",)),
        name="moe_gather",
    )(dst, n_valid, h, info_t, jnp.zeros((n_rows, D), BF16))


N_FCH = D_FF // FCH


def _moe_ffn_kernel(te_ref, src_ref, valid_ref, fresh_ref, x_ref, wg_hbm, wu_hbm, wd_hbm, y_ref,
                    wg_ref, wu_ref, wd_ref, sem_ref):
    i = pl.program_id(0)
    e = te_ref[i]
    fresh = fresh_ref[i] == 1

    def slice_copies(c):
        cols = pl.ds(c * FCH, FCH)
        return (pltpu.make_async_copy(wg_hbm.at[e, :, cols], wg_ref.at[:, cols], sem_ref.at[0, c]),
                pltpu.make_async_copy(wu_hbm.at[e, :, cols], wu_ref.at[:, cols], sem_ref.at[1, c]),
                pltpu.make_async_copy(wd_hbm.at[e, cols, :], wd_ref.at[cols, :], sem_ref.at[2, c]))

    @pl.when(fresh)
    def _():
        for c in range(N_FCH):
            for cp in slice_copies(c):
                cp.start()
        hb = x_ref[...]
        acc = jnp.zeros((MOE_TM, D), F32)
        for c in range(N_FCH):
            for cp in slice_copies(c):
                cp.wait()
            sl = slice(c * FCH, (c + 1) * FCH)
            gate = jnp.dot(hb, wg_ref[:, sl].astype(BF16), preferred_element_type=F32)
            up = jnp.dot(hb, wu_ref[:, sl].astype(BF16), preferred_element_type=F32)
            act = (gate * _sigmoid(gate) * up).astype(BF16)
            acc = acc + jnp.dot(act, wd_ref[sl, :].astype(BF16), preferred_element_type=F32)
        y_ref[...] = acc.astype(BF16)

    @pl.when(jnp.logical_and(valid_ref[i] == 1, jnp.logical_not(fresh)))
    def _():
        y_ref[...] = _swiglu_acc(x_ref[...], wg_ref, wu_ref, wd_ref).astype(BF16)

    @pl.when(valid_ref[i] == 0)
    def _():
        y_ref[...] = jnp.zeros_like(y_ref)


def _moe_ffn(te, src, valid, fresh, xg, wg, wu, wd):
    n_rows = xg.shape[0]
    hbm = pl.BlockSpec(memory_space=pl.ANY)
    return pl.pallas_call(
        _moe_ffn_kernel,
        out_shape=jax.ShapeDtypeStruct((n_rows, D), BF16),
        grid_spec=pltpu.PrefetchScalarGridSpec(
            num_scalar_prefetch=4, grid=(n_rows // MOE_TM,),
            in_specs=[pl.BlockSpec((MOE_TM, D), lambda i, te, src, *_: (src[i], 0)), hbm, hbm, hbm],
            out_specs=pl.BlockSpec((MOE_TM, D), lambda i, *_: (i, 0)),
            scratch_shapes=[pltpu.VMEM((D, D_FF), F32), pltpu.VMEM((D, D_FF), F32), pltpu.VMEM((D_FF, D), F32),
                            pltpu.SemaphoreType.DMA((3, N_FCH))]),
        compiler_params=_cparams(("arbitrary",)),
        name="moe_ffn",
    )(te, src, valid, fresh, xg, wg, wu, wd)


def _moe_combine_kernel(dst_ref, nv_ref, x_ref, mod_ref, gpost_ref, info_ref, yg_ref, xo_ref, buf_ref, sem_ref):
    i = pl.program_id(0)
    nv = nv_ref[i]

    def piece(p):
        return _piece_copy(yg_ref, pl.multiple_of(dst_ref[i, p], MOE_SEG),
                           buf_ref, pl.multiple_of(p * MOE_SEG, MOE_SEG), sem_ref.at[p])

    def start(p, c):
        piece(p).start()
        return c

    def clear(p, c):
        buf_ref[pl.ds(pl.multiple_of(p * MOE_SEG, MOE_SEG), MOE_SEG), :] = jnp.zeros((MOE_SEG, D), BF16)
        return c

    def wait(p, c):
        piece(p).wait()
        return c

    lax.fori_loop(0, nv, start, 0)
    lax.fori_loop(nv, MOE_NP, clear, 0)
    info = info_ref[0]
    rr = lax.broadcasted_iota(jnp.int32, (MOE_TB, MOE_R), 1).astype(F32)
    comb = jnp.where(rr == info[:, 0:1], info[:, 2:3], jnp.where(rr == info[:, 1:2], info[:, 3:4], 0.0)).astype(BF16)
    lax.fori_loop(0, nv, wait, 0)
    fx = jnp.dot(comb, buf_ref[...], preferred_element_type=F32)
    xo_ref[0] = x_ref[0] + mod_ref[0, 0][5:6] * _rms(fx, gpost_ref[...])


def _moe_combine(dst, n_valid, xs, modtab, g_post, info, yg):
    B, S, _ = xs.shape
    per = S // MOE_TB
    blk = pl.BlockSpec((1, MOE_TB, D), lambda i, *_: (i // per, i % per, 0))
    return pl.pallas_call(
        _moe_combine_kernel,
        out_shape=jax.ShapeDtypeStruct((B, S, D), F32),
        grid_spec=pltpu.PrefetchScalarGridSpec(
            num_scalar_prefetch=2, grid=(B * per,),
            in_specs=[blk, pl.BlockSpec((1, 1, 6, D), lambda i, *_: (i // per, 1, 0, 0)),
                      pl.BlockSpec((1, D), lambda i, *_: (0, 0)),
                      pl.BlockSpec((1, MOE_TB, LANES), lambda i, *_: (i, 0, 0)),
                      pl.BlockSpec(memory_space=pl.ANY)],
            out_specs=blk,
            scratch_shapes=[pltpu.VMEM((MOE_R, D), BF16), pltpu.SemaphoreType.DMA((MOE_NP,))]),
        compiler_params=_cparams(("arbitrary",)),
        name="moe_combine",
    )(dst, n_valid, xs, modtab, g_post, info, yg)


def _moe(xs, modtab, g_pre, g_post, router, wg, wu, wd):
    B, S, _ = xs.shape
    nb = B * S // MOE_TB
    h, info, info_t, cnt = _moe_route(xs, modtab, g_pre, router)
    worst = 2 * B * S + nb * N_EXPERTS * (MOE_SEG - 1) + N_EXPERTS * (MOE_TM - 1)
    n_tiles = -(-worst // MOE_TM)
    dst, n_valid, te, src, valid, fresh = _moe_plan(cnt[:, 0, :N_EXPERTS].astype(jnp.int32), n_tiles)
    xg = _moe_gather(dst, n_valid, h, info_t, n_tiles * MOE_TM)
    yg = _moe_ffn(te, src, valid, fresh, xg, wg, wu, wd)
    return _moe_combine(dst, n_valid, xs, modtab, g_post, info, yg)


def _block_diag2(w):
    z = jnp.zeros_like(w[0])
    return jnp.concatenate([jnp.concatenate([w[0], z], axis=1), jnp.concatenate([z, w[1]], axis=1)], axis=0)


def _row(v):
    return v.reshape(1, -1).astype(F32)


def _head_ones(width, head):
    i = jnp.arange(width) // head
    return (i[:, None] == i[None, :]).astype(BF16)


def kernel(x, c, ctx, c_ctx, ada_w, ada_b, norm_mix_pre, norm_mix_post, norm_ffn_pre, norm_ffn_post, w_in, shift_mu, rw_w_up, rw_w0, rw_a_up, rw_a0, rw_k_k, rw_k_a, rw_r_k, rw_g_up, rw_gn_w, rw_gn_b, rw_v_down, rw_v_up, rw_v0, gla_conv, gla_a_up, gla_a_b, gla_gn_w, w_out, ffn_w_gate, ffn_w_up, ffn_w_down, moe_router, moe_w_gate, moe_w_up, moe_w_down):
    B, S, _ = x.shape
    n_ctx = ctx.shape[1]
    depth = w_in.shape[0]
    assert n_ctx == TB and S % MOE_TB == 0 and (n_ctx + S) % TW == 0 and depth == 2

    xc = None
    pad_rows = -(B + 1) % SUBLANES
    cvec = jnp.concatenate([c, c_ctx[None, :], jnp.zeros((pad_rows, D), F32)], axis=0)
    bd64 = _head_ones(RW_WIDTH, RW_HEAD)
    ada_b3 = ada_b.reshape(depth, 1, 6 * D)
    v_first = None
    out = None
    for i in range(depth):
        last = i == depth - 1
        mods = _adaln(cvec, ada_w, ada_b3, i)
        mod_x = mods[:B].reshape(B, 6, D)
        mod_c = jnp.broadcast_to(mods[B].reshape(1, 6, D), (B, 6, D))
        modtab = jnp.stack([mod_c, mod_x], axis=1)

        p_rw, p_gl = _inproj(ctx, x, xc, _row(norm_mix_pre[i]), modtab, w_in, i)

        prm = dict(
            mu=_row(shift_mu[i]),
            w_up=_block_diag2(rw_w_up[i]).astype(BF16), w0=_row(rw_w0[i]),
            a_up=_block_diag2(rw_a_up[i]).astype(BF16), a0=_row(rw_a0[i]),
            k_k=_row(rw_k_k[i]), k_a=_row(rw_k_a[i]), r_k=_row(rw_r_k[i]),
            g_up=rw_g_up[i].astype(BF16), bd64=bd64,
            gn_w=_row(rw_gn_w[i]), gn_b=_row(rw_gn_b[i]), gla_gn_w=_row(gla_gn_w[i]),
            w_out=w_out[i].astype(BF16), norm_post=_row(norm_mix_post[i]),
        )
        gate_pad = jnp.zeros((LANES - 2 * GLA_GATE_RANK, 2 * GLA_KW), F32)
        gla_prm = dict(conv=gla_conv[i].astype(F32), a_b=_row(gla_a_b[i]),
                       a_up=jnp.concatenate([_block_diag2(gla_a_up[i]), gate_pad], axis=0).astype(BF16))
        if i > 0:
            pad = LANES - RW_V_RANK
            prm["v_down"] = jnp.concatenate([rw_v_down[i - 1], jnp.zeros((RW_WIDTH, pad), F32)], axis=1).astype(BF16)
            prm["v_up"] = jnp.concatenate([rw_v_up[i - 1], jnp.zeros((pad, RW_WIDTH), F32)], axis=0).astype(BF16)
            prm["v0"] = _row(rw_v0[i - 1])

        (r, kk, vm, g, bonus, lw, bb, ke), (q, k, gv, og, lg) = _prep(p_rw, p_gl, prm, gla_prm,
                                                                     v_first if i > 0 else None)
        if i == 0:
            v_first = vm
        y, o = _scans(r, kk, vm, lw, bb, ke, q, k, gv, lg)
        xc = _readout(y, bonus, g, o, og, ctx, x, xc, modtab, prm, latents_only=last)

        jf = i // 2
        if i % 2 == 0:
            xc = _ffn(xc, modtab, _row(norm_ffn_pre[i]), _row(norm_ffn_post[i]),
                      ffn_w_gate[jf], ffn_w_up[jf], ffn_w_down[jf])
        else:
            router = jnp.concatenate([moe_router[jf], jnp.zeros((D, LANES - N_EXPERTS), F32)], axis=1)
            r_hi = router.astype(BF16)
            router = jnp.stack([r_hi, (router - r_hi.astype(F32)).astype(BF16)])
            out = _moe(xc, modtab, _row(norm_ffn_pre[i]), _row(norm_ffn_post[i]), router,
                       moe_w_gate[jf], moe_w_up[jf], moe_w_down[jf])
    return out
```

```python
import functools
import math

import jax
import jax.numpy as jnp
from jax import lax
from jax.experimental import pallas as pl
from jax.experimental.pallas import tpu as pltpu

F32, BF16 = jnp.float32, jnp.bfloat16
ACT = BF16

D = 1024
GRID_W = 64
RW_WIDTH = 512
RW_HEAD = 64
RW_RANK = 64
RW_G_RANK = 128
RW_V_RANK = 32
RW_GN_EPS = 64e-5
GLA_WIDTH = 512
GLA_HEADS = 4
GLA_DV = 128
GLA_DK = 64
GLA_KW = 256
GLA_GATE_RANK = 16
GLA_GATE_NORM = 16.0
GLA_NORM_EPS = 1e-5
D_FF = 2816
N_EXPERTS = 8
NORM_EPS = 1e-6
RW_COLS = 3 * RW_WIDTH + 4 * RW_RANK + RW_G_RANK
GLA_QKV = 2 * GLA_KW + GLA_WIDTH
GLA_COLS = GLA_QKV + GLA_WIDTH + 2 * GLA_GATE_RANK
GLA_PAD = 1664

LANES = 128
SUBLANES = 8
TB = 256
TW = 3 * TB
CH = 64
FCH = 256
HALO = 16
VMEM_LIMIT = 56 * 1024 * 1024


def _cparams(sem):
    return pltpu.CompilerParams(dimension_semantics=sem, vmem_limit_bytes=VMEM_LIMIT)


def _bdot(a, b):
    return jnp.dot(a.astype(BF16), b.astype(BF16), preferred_element_type=F32)


def _bdot_nt(a, b):
    return lax.dot_general(a.astype(BF16), b.astype(BF16), (((1,), (1,)), ((), ())),
                           preferred_element_type=F32)


def _bdot_tn(a, b):
    return lax.dot_general(a.astype(BF16), b.astype(BF16), (((0,), (0,)), ((), ())),
                           preferred_element_type=F32)


def _split_dot(a_exact, x):
    h1 = x.astype(BF16)
    r1 = x - h1.astype(F32)
    h2 = r1.astype(BF16)
    h3 = (r1 - h2.astype(F32)).astype(BF16)
    return (jnp.dot(a_exact, h1, preferred_element_type=F32)
            + jnp.dot(a_exact, h2, preferred_element_type=F32)
            + jnp.dot(a_exact, h3, preferred_element_type=F32))


def _seg_sum(x, bd, exact=True):
    hi = x.astype(BF16)
    out = jnp.dot(hi, bd, preferred_element_type=F32)
    if exact:
        lo = (x - hi.astype(F32)).astype(BF16)
        out = out + jnp.dot(lo, bd, preferred_element_type=F32)
    return out


def _sigmoid(x):
    return jax.nn.sigmoid(x)


def _rms(x, g):
    return x * lax.rsqrt(jnp.mean(x * x, axis=-1, keepdims=True) + NORM_EPS) * g


def _pair_stack(z, lo):
    return jnp.concatenate([jnp.where(lo, z, 0.0), jnp.where(lo, 0.0, z)], axis=0)


def _adaln_kernel(c_ref, w_ref, b_ref, o_ref):
    c = c_ref[...]
    s = c * _sigmoid(c)
    o_ref[...] = jnp.dot(s, w_ref[0], precision=lax.Precision.HIGHEST,
                         preferred_element_type=F32) + b_ref[0]


def _adaln(cvec, w, b, layer):
    rows = cvec.shape[0]
    n = w.shape[2]
    return pl.pallas_call(
        _adaln_kernel,
        out_shape=jax.ShapeDtypeStruct((rows, n), F32),
        grid=(n // D,),
        in_specs=[pl.BlockSpec((rows, D), lambda i: (0, 0)),
                  pl.BlockSpec((1, D, D), lambda i: (layer, 0, i)),
                  pl.BlockSpec((1, 1, D), lambda i: (layer, 0, i))],
        out_specs=pl.BlockSpec((rows, D), lambda i: (0, i)),
        compiler_params=_cparams(("arbitrary",)),
        name="adaln",
    )(cvec, w, b)


def _wide_mod(mod_ref, k):
    is_ctx = jnp.logical_and(pl.program_id(1) == 0, lax.broadcasted_iota(jnp.int32, (TW, 1), 0) < TB)
    return jnp.where(is_ctx, mod_ref[0, 0, k:k + 1], mod_ref[0, 1, k:k + 1])


def _wide_rows(refs):
    if len(refs) == 1:
        return refs[0][0]
    first = jnp.where(pl.program_id(1) == 0, refs[0][0], refs[1][0])
    return jnp.concatenate([first] + [r[0] for r in refs[2:]], axis=0)


def _inproj_kernel(n_src, *refs):
    g_ref, mod_ref, w_ref, prw_ref, pgl_ref = refs[n_src:]
    x = _wide_rows(refs[:n_src])
    h = _rms(x, g_ref[...]) * (1.0 + _wide_mod(mod_ref, 1)) + _wide_mod(mod_ref, 0)
    hb = h.astype(BF16)
    prw_ref[0] = jnp.dot(hb, w_ref[0, :, :RW_COLS].astype(BF16), preferred_element_type=F32).astype(ACT)
    pgl_ref[0, :, :GLA_COLS] = jnp.dot(hb, w_ref[0, :, RW_COLS:].astype(BF16),
                                       preferred_element_type=F32).astype(ACT)
    pgl_ref[0, :, GLA_COLS:] = jnp.zeros((TW, GLA_PAD - GLA_COLS), ACT)


def _wide_src(ctx, x, xc):
    if xc is not None:
        return [xc], [pl.BlockSpec((1, TW, D), lambda b, j: (b, j, 0))]
    per = TW // TB
    specs = [pl.BlockSpec((1, TB, D), lambda b, j: (b, 0, 0))]
    for k in range(per):
        specs.append(pl.BlockSpec((1, TB, D), lambda b, j, k=k: (b, jnp.maximum(j * per + k - 1, 0), 0)))
    return [ctx] + [x] * per, specs


def _wide_mod_spec():
    return pl.BlockSpec((1, 2, 6, D), lambda b, j: (b, 0, 0, 0))


def _const_spec(shape):
    nd = len(shape)
    return pl.BlockSpec(shape, lambda *_: (0,) * nd)


def _tile_spec(c):
    return pl.BlockSpec((1, TB, c), lambda b, j: (b, j, 0))


def _dir_tile_spec(c):
    return pl.BlockSpec((2, 1, TB, c), lambda b, j: (0, b, j, 0))


def _inproj(ctx, x, xc, g, modtab, w, layer):
    B = modtab.shape[0]
    N = xc.shape[1] if xc is not None else ctx.shape[1] + x.shape[1]
    srcs, src_specs = _wide_src(ctx, x, xc)
    wide = lambda c: pl.BlockSpec((1, TW, c), lambda b, j: (b, j, 0))
    return pl.pallas_call(
        functools.partial(_inproj_kernel, len(srcs)),
        out_shape=[jax.ShapeDtypeStruct((B, N, RW_COLS), ACT),
                   jax.ShapeDtypeStruct((B, N, GLA_PAD), ACT)],
        grid=(B, N // TW),
        in_specs=src_specs + [_const_spec((1, D)), _wide_mod_spec(),
                              pl.BlockSpec((1, D, RW_COLS + GLA_COLS), lambda b, j: (layer, 0, 0),
                                           pipeline_mode=pl.Buffered(1))],
        out_specs=[wide(RW_COLS), wide(GLA_PAD)],
        compiler_params=_cparams(("parallel", "parallel")),
        name="inproj",
    )(*srcs, g, modtab, w)


def _shift_table():
    t = jnp.arange(TB)
    prev = t[:, None] - 1 == t[None, :]
    nxt = t[:, None] + 1 == t[None, :]
    col = (t % GRID_W)[:, None]
    return jnp.stack([jnp.stack([prev, nxt]),
                      jnp.stack([prev & (col != 0), nxt & (col != GRID_W - 1)])]).astype(BF16)


def _rwprep_kernel(has_vres, n_tiles, p_ref, hu_ref, hd_ref, sh_ref, mu_ref, wup_ref, w0_ref, aup_ref,
                   a0_ref, kk_ref, ka_ref, rk_ref, gup_ref, bd_ref, *rest):
    if has_vres:
        vf_ref, vdn_ref, vup_ref, v0_ref = rest[:4]
        rest = rest[4:]
    r_out, kk_out, v_out, g_out, bon_out, lw_out, b_out, ke_out = rest
    j = pl.program_id(1)
    pb = p_ref[0]
    p = pb.astype(F32)
    prev1 = jnp.dot(sh_ref[0, 0], pb, preferred_element_type=F32)
    next1 = jnp.dot(sh_ref[0, 1], pb, preferred_element_type=F32)
    has_upper = jnp.where(j == 1, 0.0, 1.0)
    has_lower = jnp.where(j == n_tiles - 1, 0.0, 1.0)
    up = jnp.concatenate([hu_ref[0].astype(F32) * has_upper, p[:TB - GRID_W]], axis=0)
    down = jnp.concatenate([p[GRID_W:], hd_ref[0].astype(F32) * has_lower], axis=0)
    cls = lax.broadcasted_iota(jnp.int32, p.shape, 1) & jnp.where(j == 0, 1, 3)
    shifted = jnp.where(cls == 0, prev1, jnp.where(cls == 1, next1, jnp.where(cls == 2, up, down)))
    u = p + mu_ref[...] * (shifted - p)

    r = u[:, 0:RW_WIDTH]
    k = u[:, RW_WIDTH:2 * RW_WIDTH]
    v = u[:, 2 * RW_WIDTH:3 * RW_WIDTH]
    o = 3 * RW_WIDTH
    wd = u[:, o:o + 2 * RW_RANK]
    ad = u[:, o + 2 * RW_RANK:o + 4 * RW_RANK]
    gd = u[:, o + 4 * RW_RANK:]

    w_logit = w0_ref[...] + _bdot(jnp.tanh(wd), wup_ref[...])
    lw = -math.exp(-0.5) * _sigmoid(w_logit)
    a = _sigmoid(a0_ref[...] + _bdot(ad, aup_ref[...]))
    bd = bd_ref[...]
    kk = k * kk_ref[...]
    kk = kk * lax.rsqrt(jnp.maximum(_seg_sum(kk * kk, bd), 1e-24))
    g = _bdot(_sigmoid(gd), gup_ref[...])
    if has_vres:
        gate = _sigmoid(v0_ref[...] + _bdot(_bdot(v, vdn_ref[...]), vup_ref[...]))
        vm = v + (vf_ref[0].astype(F32) - v) * gate
    else:
        vm = v
    ke_sum = jnp.zeros_like(k)
    for d in range(2):
        a_d = a[:, d * RW_WIDTH:(d + 1) * RW_WIDTH]
        ke_d = k * (1.0 + (a_d - 1.0) * ka_ref[...])
        lw_out[d, 0] = lw[:, d * RW_WIDTH:(d + 1) * RW_WIDTH]
        b_out[d, 0] = (kk * a_d).astype(ACT)
        ke_out[d, 0] = ke_d.astype(ACT)
        ke_sum = ke_sum + ke_d
    r_out[0] = r.astype(ACT)
    kk_out[0] = kk.astype(ACT)
    v_out[0] = vm.astype(ACT)
    g_out[0] = g.astype(ACT)
    bon_out[0] = (_seg_sum(r * ke_sum * rk_ref[...], bd, exact=False) * vm).astype(ACT)


def _rwprep_call(p_rw, prm, v_first):
    B, N, _ = p_rw.shape
    hb = TB // GRID_W
    nhb = N // GRID_W
    has_vres = v_first is not None
    W2 = 2 * RW_WIDTH
    in_specs = [
        _tile_spec(RW_COLS),
        pl.BlockSpec((1, GRID_W, RW_COLS), lambda b, j: (b, jnp.maximum(j * hb - 1, 0), 0)),
        pl.BlockSpec((1, GRID_W, RW_COLS), lambda b, j: (b, jnp.minimum(j * hb + hb, nhb - 1), 0)),
        pl.BlockSpec((1, 2, TB, TB), lambda b, j: (jnp.minimum(j, 1), 0, 0, 0)),
        _const_spec((1, RW_COLS)), _const_spec((2 * RW_RANK, W2)), _const_spec((1, W2)),
        _const_spec((2 * RW_RANK, W2)), _const_spec((1, W2)),
        _const_spec((1, RW_WIDTH)), _const_spec((1, RW_WIDTH)), _const_spec((1, RW_WIDTH)),
        _const_spec((RW_G_RANK, RW_WIDTH)), _const_spec((RW_WIDTH, RW_WIDTH)),
    ]
    args = [p_rw, p_rw, p_rw, _shift_table(), prm["mu"], prm["w_up"], prm["w0"], prm["a_up"], prm["a0"],
            prm["k_k"], prm["k_a"], prm["r_k"], prm["g_up"], prm["bd64"]]
    if has_vres:
        in_specs += [_tile_spec(RW_WIDTH), _const_spec((RW_WIDTH, LANES)), _const_spec((LANES, RW_WIDTH)),
                     _const_spec((1, RW_WIDTH))]
        args += [v_first, prm["v_down"], prm["v_up"], prm["v0"]]
    tok = jax.ShapeDtypeStruct((B, N, RW_WIDTH), ACT)
    dtok = jax.ShapeDtypeStruct((2, B, N, RW_WIDTH), ACT)
    out_shapes = [tok] * 5 + [jax.ShapeDtypeStruct((2, B, N, RW_WIDTH), F32), dtok, dtok]
    out_specs = [_tile_spec(RW_WIDTH)] * 5 + [_dir_tile_spec(RW_WIDTH)] * 3
    return args, in_specs, out_shapes, out_specs


NCH = TB // CH


def _scan_kernel(rf_ref, kkf_ref, vf_ref, lwf_ref, bf_ref, kef_ref,
                 rb_ref, kkb_ref, vb_ref, lwb_ref, bb_ref, keb_ref,
                 gqf_ref, gkf_ref, gvf_ref, lgf_ref, gqb_ref, gkb_ref, gvb_ref, lgb_ref,
                 yf_ref, yb_ref, of_ref, ob_ref, st_ref, gst_ref):
    j = pl.program_id(1)

    @pl.when(j == 0)
    def _():
        st_ref[...] = jnp.zeros_like(st_ref)
        gst_ref[...] = jnp.zeros_like(gst_ref)

    ti = lax.broadcasted_iota(jnp.int32, (CH, LANES), 0)
    li = lax.broadcasted_iota(jnp.int32, (CH, LANES), 1)
    si = li & (CH - 1)
    lo = li < CH
    eye = jnp.where(ti == si, 1.0, 0.0)
    rblk = lax.broadcasted_iota(jnp.int32, (LANES, LANES), 0) // CH
    cblk = lax.broadcasted_iota(jnp.int32, (LANES, LANES), 1) // CH
    bdmask = rblk == cblk
    t64 = lax.broadcasted_iota(jnp.int32, (CH, CH), 0)
    s64 = lax.broadcasted_iota(jnp.int32, (CH, CH), 1)
    tri = [jnp.where(s64 <= t64, 1.0, 0.0).astype(BF16), jnp.where(s64 >= t64, 1.0, 0.0).astype(BF16)]
    m_incl = [si <= ti, si >= ti]
    m_strict = [si < ti, si > ti]
    n_pairs = RW_WIDTH // LANES
    in_refs = ((rf_ref, kkf_ref, vf_ref, lwf_ref, bf_ref, kef_ref),
               (rb_ref, kkb_ref, vb_ref, lwb_ref, bb_ref, keb_ref))
    y_refs = (yf_ref, yb_ref)

    def stack(z):
        return _pair_stack(z, lo).astype(BF16)

    st = {(d, p): st_ref[d, p] for d in range(2) for p in range(n_pairs)}

    def scan_step(step):
        cur = {}
        w_tot = {}
        for d in range(2):
            c = (NCH - 1 - step) if d == 1 else step
            rows = slice(c * CH, (c + 1) * CH)
            refs = in_refs[d]
            r, kk, v = (refs[i][0, rows, :].astype(F32) for i in range(3))
            lw = refs[3][0, 0, rows, :]
            b, ke = refs[4][0, 0, rows, :].astype(F32), refs[5][0, 0, rows, :].astype(F32)
            cum = _split_dot(tri[d], lw)
            tot = jnp.sum(lw, axis=0, keepdims=True)
            w_inv = jnp.exp(-cum)
            w_end = jnp.exp(tot - cum)
            w_tot[d] = jnp.exp(tot)
            rh = r * jnp.exp(cum)
            ah = -(kk * jnp.exp(cum - lw))
            bh, kh, bt, kt = b * w_inv, ke * w_inv, b * w_end, ke * w_end
            for p in range(n_pairs):
                sl = slice(LANES * p, LANES * (p + 1))
                cur[d, p] = dict(ah=ah[:, sl], rh=rh[:, sl], bh=bh[:, sl], kh=kh[:, sl], bt=bt[:, sl],
                                 kt=kt[:, sl], v=v[:, sl], rows=rows, sl=sl)
        yield
        for it in cur.values():
            it["lhs"] = jnp.concatenate([it["ah"], it["rh"]], axis=0).astype(BF16)
            rhs = jnp.concatenate([stack(it["bh"]), stack(it["kh"])], axis=0)
            it["a_all"] = _bdot_nt(it["lhs"], rhs)
        yield
        for (d, _), it in cur.items():
            a_all = it.pop("a_all")
            it["a_ab"] = jnp.where(m_strict[d], a_all[:CH, :LANES], 0.0)
            a_ak = jnp.where(m_strict[d], a_all[:CH, LANES:], 0.0)
            a_rb = jnp.where(m_incl[d], a_all[CH:, :LANES], 0.0)
            a_rk = jnp.where(m_incl[d], a_all[CH:, LANES:], 0.0)
            it["a_r"] = jnp.concatenate([a_rb, a_rk], axis=1).astype(BF16)
            it["v_bd"] = stack(it["v"])
            it["akv"] = _bdot(a_ak, it["v_bd"])
            it["t"] = eye + it["a_ab"]
            it["m"] = _bdot(it["a_ab"], stack(it["a_ab"]))
        yield
        for _ in range(int(math.log2(CH)) - 2):
            for it in cur.values():
                z = _bdot(it["m"], jnp.concatenate([stack(it["m"]), stack(it["t"])], axis=1))
                it["m"] = z[:, :LANES]
                it["t"] = it["t"] + z[:, LANES:]
            yield
        for it in cur.values():
            it["t"] = (it["t"] + _bdot(it["m"], stack(it["t"]))).astype(BF16)
        yield
        for it in cur.values():
            ta = jnp.dot(it["t"], jnp.concatenate([stack(it["ah"]), stack(it["akv"])], axis=1),
                         preferred_element_type=F32)
            it["lhs2"] = jnp.concatenate([ta[:, :LANES].astype(BF16), it["lhs"][CH:]], axis=0)
            it["cc"] = ta[:, LANES:]
            it["rhs_t"] = jnp.concatenate([it["bt"], it["kt"]], axis=0).astype(BF16)
        yield
        z1 = {k: _bdot_nt(it["lhs2"], st[k]) for k, it in cur.items()}
        yield
        u = {k: z1[k][:CH] + it["cc"] for k, it in cur.items()}
        upd = {k: _bdot_tn(jnp.concatenate([u[k], it["v"]], axis=0), it["rhs_t"]) for k, it in cur.items()}
        yield
        for (d, p), it in cur.items():
            y = z1[d, p][CH:] + jnp.dot(it["a_r"], jnp.concatenate([stack(u[d, p]), it["v_bd"]], axis=0),
                                        preferred_element_type=F32)
            y_refs[d][0, it["rows"], it["sl"]] = y.astype(ACT)
            st[d, p] = st[d, p] * w_tot[d][:, it["sl"]] + jnp.where(bdmask, upd[d, p], 0.0)
        yield

    gla_refs = ((gqf_ref, gkf_ref, gvf_ref, lgf_ref), (gqb_ref, gkb_ref, gvb_ref, lgb_ref))
    o_refs = (of_ref, ob_ref)
    g_pairs = GLA_KW // LANES
    gst = {(d, p): gst_ref[d, p] for d in range(2) for p in range(g_pairs)}
    lo2 = lax.broadcasted_iota(jnp.int32, (LANES, LANES), 1) < CH
    zeros_v = jnp.zeros((CH, GLA_DV), F32)

    def gla_step(step):
        cur = {}
        dec = {}
        for d in range(2):
            c = (NCH - 1 - step) if d == 1 else step
            rows = slice(c * CH, (c + 1) * CH)
            refs = gla_refs[d]
            q, k, v = (refs[i][0, rows, :].astype(F32) for i in range(3))
            lg = refs[3][0, 0, rows, :]
            cum = _split_dot(tri[d], lg)
            tot = jnp.sum(lg, axis=0, keepdims=True)
            dec[d] = jnp.exp(tot)
            qd = q * jnp.exp(cum)
            ki = k * jnp.exp(-cum)
            kend = k * jnp.exp(tot - cum)
            for p in range(g_pairs):
                sl = slice(LANES * p, LANES * (p + 1))
                cur[d, p] = dict(qd=qd[:, sl].astype(BF16), ki=ki[:, sl], kend=kend[:, sl], rows=rows, sl=sl,
                                 v0=v[:, 2 * LANES * p:2 * LANES * p + LANES],
                                 v1=v[:, 2 * LANES * p + LANES:2 * LANES * (p + 1)])
        yield
        for it in cur.values():
            it["att"] = _bdot_nt(it["qd"], _pair_stack(it["ki"], lo))
        yield
        for (d, _), it in cur.items():
            att = jnp.where(m_incl[d], it["att"], 0.0)
            v_bd = jnp.concatenate([jnp.concatenate([it["v0"], zeros_v], axis=1),
                                    jnp.concatenate([zeros_v, it["v1"]], axis=1)], axis=0)
            it["o"] = _bdot(att, v_bd)
            it["upd"] = _bdot_tn(jnp.concatenate([it["v0"], it["v1"]], axis=0), _pair_stack(it["kend"], lo))
        yield
        for (d, p), it in cur.items():
            s = gst[d, p]
            s_bd = jnp.concatenate([jnp.where(lo2, s, 0.0), jnp.where(lo2, 0.0, s)], axis=0)
            o = it["o"] + _bdot_nt(it["qd"], s_bd)
            o_refs[d][0, it["rows"], 2 * LANES * p:2 * LANES * (p + 1)] = o.astype(ACT)
            gst[d, p] = s * dec[d][:, it["sl"]] + it["upd"]
        yield

    n_stages = 12
    n_dep = 3
    gla_stages = (1, 4, 7, 10)
    pipeline = [(scan_step(s), gla_step(s)) for s in range(NCH)]
    for slot in range(n_stages + n_dep * (NCH - 1)):
        for s, (rw_gen, gla_gen) in enumerate(pipeline):
            stage = slot - n_dep * s
            if 0 <= stage < n_stages:
                next(rw_gen)
                if stage in gla_stages:
                    next(gla_gen)
    for (d, p), s in st.items():
        st_ref[d, p] = s
    for (d, p), s in gst.items():
        gst_ref[d, p] = s


def _bidir_specs(c, n_tiles):
    def back(j):
        return jnp.where(j == 0, 0, n_tiles - j)

    tok_f = pl.BlockSpec((1, TB, c), lambda b, j: (b, j, 0))
    tok_b = pl.BlockSpec((1, TB, c), lambda b, j: (b, back(j), 0))
    dir_f = pl.BlockSpec((1, 1, TB, c), lambda b, j: (0, b, j, 0))
    dir_b = pl.BlockSpec((1, 1, TB, c), lambda b, j: (1, b, back(j), 0))
    return tok_f, tok_b, dir_f, dir_b


def _scans(r, kk, v, lw, b, ke, q, k, gv, lg):
    B, N, _ = r.shape
    nt = N // TB
    tok_f, tok_b, dir_f, dir_b = _bidir_specs(RW_WIDTH, nt)
    kf, kb, dkf, dkb = _bidir_specs(GLA_KW, nt)
    out = jax.ShapeDtypeStruct((B, N, RW_WIDTH), ACT)
    assert GLA_WIDTH == RW_WIDTH
    yf, yb, of, ob = pl.pallas_call(
        _scan_kernel,
        out_shape=[out] * 4,
        grid=(B, nt),
        in_specs=[tok_f, tok_f, tok_f, dir_f, dir_f, dir_f, tok_b, tok_b, tok_b, dir_b, dir_b, dir_b,
                  kf, kf, tok_f, dkf, kb, kb, tok_b, dkb],
        out_specs=[tok_f, tok_b, tok_f, tok_b],
        scratch_shapes=[pltpu.VMEM((2, RW_WIDTH // LANES, LANES, LANES), F32),
                        pltpu.VMEM((2, GLA_KW // LANES, GLA_DV, LANES), F32)],
        compiler_params=_cparams(("parallel", "arbitrary")),
        name="scans",
    )(r, kk, v, lw, b, ke, r, kk, v, lw, b, ke, q, k, gv, lg, q, k, gv, lg)
    return (yf, yb), (of, ob)


CONV_K = TB + LANES


def _conv_shift_table():
    t = jnp.arange(TB)[:, None]
    s = jnp.arange(CONV_K)[None, :]
    prev = jnp.where(t == 0, s == TB + 2 * HALO - 1, s == t - 1)
    nxt = jnp.where(t == TB - 1, s == TB, s == t + 1)
    return jnp.stack([prev, nxt]).astype(BF16)


def _glaprep_kernel(n_tiles, p_ref, hp_ref, hn_ref, sh_ref, cw_ref, aup_ref, ab_ref, q_out, k_out, v_out,
                    og_out, lg_out):
    j = pl.program_id(1)
    ub = p_ref[0][:, :GLA_QKV]
    u = ub.astype(F32)
    has_prev = jnp.where(j <= 1, 0.0, 1.0)
    has_next = jnp.where(jnp.logical_or(j == 0, j == n_tiles - 1), 0.0, 1.0)
    ext = jnp.concatenate([ub,
                           (hn_ref[0][:, :GLA_QKV].astype(F32) * has_next).astype(BF16),
                           (hp_ref[0][:, :GLA_QKV].astype(F32) * has_prev).astype(BF16),
                           jnp.zeros((CONV_K - TB - 2 * HALO, GLA_QKV), BF16)], axis=0)
    prev1 = jnp.dot(sh_ref[0], ext, preferred_element_type=F32)
    next1 = jnp.dot(sh_ref[1], ext, preferred_element_type=F32)
    cw = cw_ref[...]
    conv = cw[0:1] * prev1 + cw[1:2] * u + cw[2:3] * next1
    qkv = conv * _sigmoid(conv)
    q_out[0] = (qkv[:, :GLA_KW] * (GLA_DK ** -0.5)).astype(ACT)
    k_out[0] = qkv[:, GLA_KW:2 * GLA_KW].astype(ACT)
    v_out[0] = qkv[:, 2 * GLA_KW:].astype(ACT)
    og_out[0] = p_ref[0][:, GLA_QKV:GLA_QKV + GLA_WIDTH]
    z = _bdot(p_ref[0][:, GLA_QKV + GLA_WIDTH:], aup_ref[...]) + ab_ref[...]
    lg = (jnp.minimum(z, 0.0) - jnp.log1p(jnp.exp(-jnp.abs(z)))) * (1.0 / GLA_GATE_NORM)
    for d in range(2):
        lg_out[d, 0] = lg[:, d * GLA_KW:(d + 1) * GLA_KW]


def _glaprep_call(p_gl, prm):
    B, N, _ = p_gl.shape
    sub = HALO
    hb = TB // sub
    nhb = N // sub
    args = [p_gl, p_gl, p_gl, _conv_shift_table(), prm["conv"], prm["a_up"], prm["a_b"]]
    in_specs = [_tile_spec(GLA_PAD),
                pl.BlockSpec((1, sub, GLA_PAD), lambda b, j: (b, jnp.maximum(j * hb - 1, 0), 0)),
                pl.BlockSpec((1, sub, GLA_PAD), lambda b, j: (b, jnp.minimum(j * hb + hb, nhb - 1), 0)),
                _const_spec((2, TB, CONV_K)),
                _const_spec((3, GLA_QKV)), _const_spec((LANES, 2 * GLA_KW)), _const_spec((1, 2 * GLA_KW))]
    out_shapes = [jax.ShapeDtypeStruct((B, N, GLA_KW), ACT), jax.ShapeDtypeStruct((B, N, GLA_KW), ACT),
                  jax.ShapeDtypeStruct((B, N, GLA_WIDTH), ACT), jax.ShapeDtypeStruct((B, N, GLA_WIDTH), ACT),
                  jax.ShapeDtypeStruct((2, B, N, GLA_KW), F32)]
    out_specs = [_tile_spec(GLA_KW), _tile_spec(GLA_KW), _tile_spec(GLA_WIDTH), _tile_spec(GLA_WIDTH),
                 _dir_tile_spec(GLA_KW)]
    return args, in_specs, out_shapes, out_specs


def _prep_kernel(has_vres, n_tiles, n_rw_in, n_gla_in, n_rw_out, *refs):
    rw_in, refs = refs[:n_rw_in], refs[n_rw_in:]
    gla_in, outs = refs[:n_gla_in], refs[n_gla_in:]
    _rwprep_kernel(has_vres, n_tiles, *rw_in, *outs[:n_rw_out])
    _glaprep_kernel(n_tiles, *gla_in, *outs[n_rw_out:])


def _prep(p_rw, p_gl, prm, gla_prm, v_first):
    B, N, _ = p_rw.shape
    nt = N // TB
    rw_args, rw_in, rw_shapes, rw_out = _rwprep_call(p_rw, prm, v_first)
    gla_args, gla_in, gla_shapes, gla_out = _glaprep_call(p_gl, gla_prm)
    outs = pl.pallas_call(
        functools.partial(_prep_kernel, v_first is not None, nt, len(rw_args), len(gla_args), len(rw_shapes)),
        out_shape=rw_shapes + gla_shapes,
        grid=(B, nt),
        in_specs=rw_in + gla_in,
        out_specs=rw_out + gla_out,
        compiler_params=_cparams(("parallel", "parallel")),
        name="prep",
    )(*rw_args, *gla_args)
    return outs[:len(rw_shapes)], outs[len(rw_shapes):]


def _readout_kernel(group, first, split_src, *refs):
    tok = [refs[i * group:(i + 1) * group] for i in range(7)]
    refs = refs[7 * group:]
    n_res = group + 1 if split_src else group
    res_refs, refs = refs[:n_res], refs[n_res:]
    mod_ref, gnw_ref, gnb_ref, ggn_ref, wout_ref, gpost_ref, bd_ref, xo_ref = refs
    bd = bd_ref[...]
    starts_with_ctx = jnp.logical_and(first == 0, pl.program_id(1) == 0)
    for k in range(group):
        yf_ref, yb_ref, bon_ref, g_ref, of_ref, ob_ref, og_ref = (t[k] for t in tok)
        gate = mod_ref[0, 1, 2:3]
        if split_src:
            x_res = res_refs[k + 1][0]
            if k == 0:
                x_res = jnp.where(starts_with_ctx, res_refs[0][0], x_res)
        else:
            x_res = res_refs[k][0]
        if k == 0 and first == 0:
            gate = jnp.where(starts_with_ctx, mod_ref[0, 0, 2:3], gate)
        y = yf_ref[0].astype(F32) + yb_ref[0].astype(F32)
        mu = _seg_sum(y, bd, exact=False) * (1.0 / RW_HEAD)
        yc = y - mu
        var = _seg_sum(yc * yc, bd, exact=False) * (1.0 / RW_HEAD)
        yn = yc * lax.rsqrt(var + RW_GN_EPS) * gnw_ref[...] + gnb_ref[...]
        rw = (yn + bon_ref[0].astype(F32)) * g_ref[0].astype(F32)
        o = of_ref[0].astype(F32) + ob_ref[0].astype(F32)
        og = og_ref[0].astype(F32)
        parts = [rw.astype(BF16)]
        for h in range(GLA_HEADS):
            sl = slice(GLA_DV * h, GLA_DV * (h + 1))
            oh = o[:, sl]
            on = oh * lax.rsqrt(jnp.mean(oh * oh, axis=-1, keepdims=True) + GLA_NORM_EPS)
            ogh = og[:, sl]
            parts.append((on * ggn_ref[:, sl] * (ogh * _sigmoid(ogh))).astype(BF16))
        cat = jnp.concatenate(parts, axis=1)
        mx = jnp.dot(cat, wout_ref[...], preferred_element_type=F32)
        xo_ref[0, k * TB:(k + 1) * TB, :] = x_res + gate * _rms(mx, gpost_ref[...])


def _readout(y, bonus, g, o, og, ctx, x, xc, modtab, prm, latents_only):
    B, N, _ = bonus.shape
    first = 1 if latents_only else 0
    n_tiles = N // TB - first
    group = max(k for k in (4, 3, 2, 1) if n_tiles % k == 0)

    def tiles(c, shift=0):
        return [pl.BlockSpec((1, TB, c), lambda b, j, k=k: (b, jnp.maximum(first + group * j + k + shift, 0), 0))
                for k in range(group)]

    if xc is None:
        res, res_specs = [ctx] + [x] * group, [pl.BlockSpec((1, TB, D), lambda b, j: (b, 0, 0))] + tiles(D, -1)
    else:
        res, res_specs = [xc] * group, tiles(D)
    tok_args, tok_specs = [], []
    for arr in (y[0], y[1], bonus, g, o[0], o[1], og):
        tok_args += [arr] * group
        tok_specs += tiles(arr.shape[-1])
    return pl.pallas_call(
        functools.partial(_readout_kernel, group, first, xc is None),
        out_shape=jax.ShapeDtypeStruct((B, n_tiles * TB, D), F32),
        grid=(B, n_tiles // group),
        in_specs=tok_specs + res_specs + [_wide_mod_spec(),
                  _const_spec((1, RW_WIDTH)), _const_spec((1, RW_WIDTH)), _const_spec((1, GLA_WIDTH)),
                  _const_spec((D, D)), _const_spec((1, D)), _const_spec((RW_WIDTH, RW_WIDTH))],
        out_specs=pl.BlockSpec((1, group * TB, D), lambda b, j: (b, j, 0)),
        compiler_params=_cparams(("parallel", "parallel")),
        name="readout",
    )(*tok_args, *res, modtab, prm["gn_w"], prm["gn_b"], prm["gla_gn_w"],
      prm["w_out"], prm["norm_post"], prm["bd64"])


def _swiglu_acc(hb, wg_ref, wu_ref, wd_ref):
    acc = jnp.zeros((hb.shape[0], D), F32)
    for c in range(D_FF // FCH):
        sl = slice(c * FCH, (c + 1) * FCH)
        gate = jnp.dot(hb, wg_ref[:, sl].astype(BF16), preferred_element_type=F32)
        up = jnp.dot(hb, wu_ref[:, sl].astype(BF16), preferred_element_type=F32)
        act = (gate * _sigmoid(gate) * up).astype(BF16)
        acc = acc + jnp.dot(act, wd_ref[sl, :].astype(BF16), preferred_element_type=F32)
    return acc


def _ffn_kernel(x_ref, mod_ref, gpre_ref, gpost_ref, wg_ref, wu_ref, wd_ref, xo_ref):
    x = x_ref[0]
    hb = (_rms(x, gpre_ref[...]) * (1.0 + _wide_mod(mod_ref, 4)) + _wide_mod(mod_ref, 3)).astype(BF16)
    fx = _swiglu_acc(hb, wg_ref, wu_ref, wd_ref)
    xo_ref[0] = x + _wide_mod(mod_ref, 5) * _rms(fx, gpost_ref[...])


def _single_buffered(shape):
    nd = len(shape)
    return pl.BlockSpec(shape, lambda *_: (0,) * nd, pipeline_mode=pl.Buffered(1))


def _ffn(xc, modtab, g_pre, g_post, wg, wu, wd):
    B, N, _ = xc.shape
    wide = pl.BlockSpec((1, TW, D), lambda b, j: (b, j, 0))
    return pl.pallas_call(
        _ffn_kernel,
        out_shape=jax.ShapeDtypeStruct((B, N, D), F32),
        grid=(B, N // TW),
        in_specs=[wide, _wide_mod_spec(), _const_spec((1, D)), _const_spec((1, D)),
                  _single_buffered((D, D_FF)), _single_buffered((D, D_FF)), _single_buffered((D_FF, D))],
        out_specs=wide,
        compiler_params=_cparams(("parallel", "parallel")),
        name="ffn",
    )(xc, modtab, g_pre, g_post, wg, wu, wd)


MOE_TB = 512
MOE_SEG = 16
MOE_TM = 512
MOE_R = 2 * MOE_TB + N_EXPERTS * MOE_SEG
MOE_NP = MOE_R // MOE_SEG


def _moe_route_kernel(x_ref, mod_ref, gpre_ref, router_ref, h_ref, info_ref, infot_ref, cnt_ref):
    mod = mod_ref[0, 0]
    h = _rms(x_ref[0], gpre_ref[...]) * (1.0 + mod[4:5]) + mod[3:4]
    hb = h.astype(BF16)
    h_ref[0] = hb
    lane = lax.broadcasted_iota(jnp.int32, (MOE_TB, LANES), 1)
    h_lo = (h - hb.astype(F32)).astype(BF16)
    logits = (jnp.dot(hb, router_ref[0], preferred_element_type=F32)
              + jnp.dot(hb, router_ref[1], preferred_element_type=F32)
              + jnp.dot(h_lo, router_ref[0], preferred_element_type=F32))
    logits = jnp.where(lane < N_EXPERTS, logits, -jnp.inf)
    v1 = jnp.max(logits, axis=-1, keepdims=True)
    i1 = jnp.min(jnp.where(logits == v1, lane, LANES), axis=-1, keepdims=True)
    rest = jnp.where(lane == i1, -jnp.inf, logits)
    v2 = jnp.max(rest, axis=-1, keepdims=True)
    i2 = jnp.min(jnp.where(rest == v2, lane, LANES), axis=-1, keepdims=True)
    ex = jnp.exp(v2 - v1)
    w1 = 1.0 / (1.0 + ex)
    w2 = ex * w1
    e1 = jnp.where(lane == i1, 1.0, 0.0)
    e2 = jnp.where(lane == i2, 1.0, 0.0)
    es = e1 + e2
    t = lax.broadcasted_iota(jnp.int32, (MOE_TB, MOE_TB), 0)
    s = lax.broadcasted_iota(jnp.int32, (MOE_TB, MOE_TB), 1)
    before = jnp.where(s < t, 1.0, 0.0).astype(BF16)
    rank = jnp.dot(before, es.astype(BF16), preferred_element_type=F32)
    cnt = jnp.sum(es, axis=0, keepdims=True)
    segs = jnp.floor((cnt + (MOE_SEG - 1)) * (1.0 / MOE_SEG))
    ea = lax.broadcasted_iota(jnp.int32, (LANES, LANES), 0)
    eb = lax.broadcasted_iota(jnp.int32, (LANES, LANES), 1)
    earlier = jnp.where(ea < eb, 1.0, 0.0).astype(BF16)
    start = jnp.dot(jnp.broadcast_to(segs, (SUBLANES, LANES)).astype(BF16), earlier,
                    preferred_element_type=F32)[0:1] * MOE_SEG
    pos = rank + start
    d1 = jnp.sum(e1 * pos, axis=-1, keepdims=True)
    d2 = jnp.sum(e2 * pos, axis=-1, keepdims=True)
    info = jnp.where(lane == 0, d1, jnp.where(lane == 1, d2, jnp.where(lane == 2, w1, jnp.where(lane == 3, w2, 0.0))))
    info_ref[0] = info
    infot_ref[0] = jnp.transpose(info)[0:SUBLANES]
    cnt_ref[0] = jnp.broadcast_to(cnt, (SUBLANES, LANES))


def _moe_route(xs, modtab, g_pre, router):
    B, S, _ = xs.shape
    per = S // MOE_TB
    nb = B * per
    blk = lambda c: pl.BlockSpec((1, MOE_TB, c), lambda i: (i // per, i % per, 0))
    flat = lambda r, c: pl.BlockSpec((1, r, c), lambda i: (i, 0, 0))
    return pl.pallas_call(
        _moe_route_kernel,
        out_shape=[jax.ShapeDtypeStruct((nb, MOE_TB, D), BF16), jax.ShapeDtypeStruct((nb, MOE_TB, LANES), F32),
                   jax.ShapeDtypeStruct((nb, SUBLANES, MOE_TB), F32),
                   jax.ShapeDtypeStruct((nb, SUBLANES, LANES), F32)],
        grid=(nb,),
        in_specs=[blk(D), pl.BlockSpec((1, 1, 6, D), lambda i: (i // per, 1, 0, 0)),
                  _const_spec((1, D)), _const_spec((2, D, LANES))],
        out_specs=[flat(MOE_TB, D), flat(MOE_TB, LANES), flat(SUBLANES, MOE_TB), flat(SUBLANES, LANES)],
        compiler_params=_cparams(("parallel",)),
        name="moe_route",
    )(xs, modtab, g_pre, router)


def _moe_plan(cnt, n_tiles):
    pc = (cnt + MOE_SEG - 1) // MOE_SEG * MOE_SEG
    inc = jnp.cumsum(pc, axis=1)
    loff = inc - pc
    reg = (jnp.sum(pc, axis=0) + MOE_TM - 1) // MOE_TM * MOE_TM
    gend = jnp.cumsum(reg)
    goff = (gend - reg)[None, :] + jnp.cumsum(pc, axis=0) - pc
    rows = jnp.arange(MOE_NP, dtype=jnp.int32) * MOE_SEG
    e_p = jnp.sum((rows[None, :, None] >= inc[:, None, :]).astype(jnp.int32), axis=-1)
    e_c = jnp.minimum(e_p, N_EXPERTS - 1)
    pick = (e_c[:, :, None] == jnp.arange(N_EXPERTS)[None, None, :]).astype(jnp.int32)
    dst = jnp.sum(pick * (goff - loff)[:, None, :], axis=-1) + rows[None, :]
    dst = jnp.where(e_p < N_EXPERTS, dst, 0).astype(jnp.int32)
    n_valid = (inc[:, -1] // MOE_SEG).astype(jnp.int32)
    trow = jnp.arange(n_tiles, dtype=jnp.int32) * MOE_TM
    te = jnp.sum((trow[:, None] >= gend[None, :]).astype(jnp.int32), axis=-1)
    valid = te < N_EXPERTS
    last = gend[-1] // MOE_TM - 1
    te = jnp.where(valid, te, te[last]).astype(jnp.int32)
    src = jnp.where(valid, jnp.arange(n_tiles, dtype=jnp.int32), last).astype(jnp.int32)
    fresh = valid & jnp.concatenate([jnp.ones((1,), bool), te[1:] != te[:-1]])
    return dst, n_valid, te, src, valid.astype(jnp.int32), fresh.astype(jnp.int32)


def _piece_copy(src_ref, src_row, dst_ref, dst_row, sem):
    return pltpu.make_async_copy(src_ref.at[pl.ds(src_row, MOE_SEG)], dst_ref.at[pl.ds(dst_row, MOE_SEG)], sem)


def _moe_gather_kernel(dst_ref, nv_ref, h_ref, infot_ref, xg_in_ref, xg_ref, buf_ref, sem_ref):
    del xg_in_ref
    i = pl.program_id(0)
    it = infot_ref[0]
    rr = lax.broadcasted_iota(jnp.int32, (MOE_R, MOE_TB), 0).astype(F32)
    onehot = jnp.where(rr == it[0:1], 1.0, jnp.where(rr == it[1:2], 1.0, 0.0)).astype(BF16)
    buf_ref[...] = jnp.dot(onehot, h_ref[0], preferred_element_type=F32).astype(BF16)
    nv = nv_ref[i]

    def piece(p):
        return _piece_copy(buf_ref, pl.multiple_of(p * MOE_SEG, MOE_SEG),
                           xg_ref, pl.multiple_of(dst_ref[i, p], MOE_SEG), sem_ref.at[p])

    def start(p, c):
        piece(p).start()
        return c

    def wait(p, c):
        piece(p).wait()
        return c

    lax.fori_loop(0, nv, start, 0)
    lax.fori_loop(0, nv, wait, 0)


def _moe_gather(dst, n_valid, h, info_t, n_rows):
    nb = h.shape[0]
    flat = lambda r, c: pl.BlockSpec((1, r, c), lambda i, *_: (i, 0, 0))
    return pl.pallas_call(
        _moe_gather_kernel,
        out_shape=jax.ShapeDtypeStruct((n_rows, D), BF16),
        grid_spec=pltpu.PrefetchScalarGridSpec(
            num_scalar_prefetch=2, grid=(nb,),
            in_specs=[flat(MOE_TB, D), flat(SUBLANES, MOE_TB), pl.BlockSpec(memory_space=pl.ANY)],
            out_specs=pl.BlockSpec(memory_space=pl.ANY),
            scratch_shapes=[pltpu.VMEM((MOE_R, D), BF16), pltpu.SemaphoreType.DMA((MOE_NP,))]),
        input_output_aliases={4: 0},
        compiler_params=_cparams(("arbitrary",)),
        name="moe_gather",
    )(dst, n_valid, h, info_t, jnp.zeros((n_rows, D), BF16))


N_FCH = D_FF // FCH


def _moe_ffn_kernel(te_ref, src_ref, valid_ref, fresh_ref, x_ref, wg_hbm, wu_hbm, wd_hbm, y_ref,
                    wg_ref, wu_ref, wd_ref, sem_ref):
    i = pl.program_id(0)
    e = te_ref[i]
    fresh = fresh_ref[i] == 1

    def slice_copies(c):
        cols = pl.ds(c * FCH, FCH)
        return (pltpu.make_async_copy(wg_hbm.at[e, :, cols], wg_ref.at[:, cols], sem_ref.at[0, c]),
                pltpu.make_async_copy(wu_hbm.at[e, :, cols], wu_ref.at[:, cols], sem_ref.at[1, c]),
                pltpu.make_async_copy(wd_hbm.at[e, cols, :], wd_ref.at[cols, :], sem_ref.at[2, c]))

    @pl.when(fresh)
    def _():
        for c in range(N_FCH):
            for cp in slice_copies(c):
                cp.start()
        hb = x_ref[...]
        acc = jnp.zeros((MOE_TM, D), F32)
        for c in range(N_FCH):
            for cp in slice_copies(c):
                cp.wait()
            sl = slice(c * FCH, (c + 1) * FCH)
            gate = jnp.dot(hb, wg_ref[:, sl].astype(BF16), preferred_element_type=F32)
            up = jnp.dot(hb, wu_ref[:, sl].astype(BF16), preferred_element_type=F32)
            act = (gate * _sigmoid(gate) * up).astype(BF16)
            acc = acc + jnp.dot(act, wd_ref[sl, :].astype(BF16), preferred_element_type=F32)
        y_ref[...] = acc.astype(BF16)

    @pl.when(jnp.logical_and(valid_ref[i] == 1, jnp.logical_not(fresh)))
    def _():
        y_ref[...] = _swiglu_acc(x_ref[...], wg_ref, wu_ref, wd_ref).astype(BF16)

    @pl.when(valid_ref[i] == 0)
    def _():
        y_ref[...] = jnp.zeros_like(y_ref)


def _moe_ffn(te, src, valid, fresh, xg, wg, wu, wd):
    n_rows = xg.shape[0]
    hbm = pl.BlockSpec(memory_space=pl.ANY)
    return pl.pallas_call(
        _moe_ffn_kernel,
        out_shape=jax.ShapeDtypeStruct((n_rows, D), BF16),
        grid_spec=pltpu.PrefetchScalarGridSpec(
            num_scalar_prefetch=4, grid=(n_rows // MOE_TM,),
            in_specs=[pl.BlockSpec((MOE_TM, D), lambda i, te, src, *_: (src[i], 0)), hbm, hbm, hbm],
            out_specs=pl.BlockSpec((MOE_TM, D), lambda i, *_: (i, 0)),
            scratch_shapes=[pltpu.VMEM((D, D_FF), F32), pltpu.VMEM((D, D_FF), F32), pltpu.VMEM((D_FF, D), F32),
                            pltpu.SemaphoreType.DMA((3, N_FCH))]),
        compiler_params=_cparams(("arbitrary",)),
        name="moe_ffn",
    )(te, src, valid, fresh, xg, wg, wu, wd)


def _moe_combine_kernel(dst_ref, nv_ref, x_ref, mod_ref, gpost_ref, info_ref, yg_ref, xo_ref, buf_ref, sem_ref):
    i = pl.program_id(0)
    nv = nv_ref[i]

    def piece(p):
        return _piece_copy(yg_ref, pl.multiple_of(dst_ref[i, p], MOE_SEG),
                           buf_ref, pl.multiple_of(p * MOE_SEG, MOE_SEG), sem_ref.at[p])

    def start(p, c):
        piece(p).start()
        return c

    def clear(p, c):
        buf_ref[pl.ds(pl.multiple_of(p * MOE_SEG, MOE_SEG), MOE_SEG), :] = jnp.zeros((MOE_SEG, D), BF16)
        return c

    def wait(p, c):
        piece(p).wait()
        return c

    lax.fori_loop(0, nv, start, 0)
    lax.fori_loop(nv, MOE_NP, clear, 0)
    info = info_ref[0]
    rr = lax.broadcasted_iota(jnp.int32, (MOE_TB, MOE_R), 1).astype(F32)
    comb = jnp.where(rr == info[:, 0:1], info[:, 2:3], jnp.where(rr == info[:, 1:2], info[:, 3:4], 0.0)).astype(BF16)
    lax.fori_loop(0, nv, wait, 0)
    fx = jnp.dot(comb, buf_ref[...], preferred_element_type=F32)
    xo_ref[0] = x_ref[0] + mod_ref[0, 0][5:6] * _rms(fx, gpost_ref[...])


def _moe_combine(dst, n_valid, xs, modtab, g_post, info, yg):
    B, S, _ = xs.shape
    per = S // MOE_TB
    blk = pl.BlockSpec((1, MOE_TB, D), lambda i, *_: (i // per, i % per, 0))
    return pl.pallas_call(
        _moe_combine_kernel,
        out_shape=jax.ShapeDtypeStruct((B, S, D), F32),
        grid_spec=pltpu.PrefetchScalarGridSpec(
            num_scalar_prefetch=2, grid=(B * per,),
            in_specs=[blk, pl.BlockSpec((1, 1, 6, D), lambda i, *_: (i // per, 1, 0, 0)),
                      pl.BlockSpec((1, D), lambda i, *_: (0, 0)),
                      pl.BlockSpec((1, MOE_TB, LANES), lambda i, *_: (i, 0, 0)),
                      pl.BlockSpec(memory_space=pl.ANY)],
            out_specs=blk,
            scratch_shapes=[pltpu.VMEM((MOE_R, D), BF16), pltpu.SemaphoreType.DMA((MOE_NP,))]),
        compiler_params=_cparams(("arbitrary",)),
        name="moe_combine",
    )(dst, n_valid, xs, modtab, g_post, info, yg)


def _moe(xs, modtab, g_pre, g_post, router, wg, wu, wd):
    B, S, _ = xs.shape
    nb = B * S // MOE_TB
    h, info, info_t, cnt = _moe_route(xs, modtab, g_pre, router)
    worst = 2 * B * S + nb * N_EXPERTS * (MOE_SEG - 1) + N_EXPERTS * (MOE_TM - 1)
    n_tiles = -(-worst // MOE_TM)
    dst, n_valid, te, src, valid, fresh = _moe_plan(cnt[:, 0, :N_EXPERTS].astype(jnp.int32), n_tiles)
    xg = _moe_gather(dst, n_valid, h, info_t, n_tiles * MOE_TM)
    yg = _moe_ffn(te, src, valid, fresh, xg, wg, wu, wd)
    return _moe_combine(dst, n_valid, xs, modtab, g_post, info, yg)


def _block_diag2(w):
    z = jnp.zeros_like(w[0])
    return jnp.concatenate([jnp.concatenate([w[0], z], axis=1), jnp.concatenate([z, w[1]], axis=1)], axis=0)


def _row(v):
    return v.reshape(1, -1).astype(F32)


def _head_ones(width, head):
    i = jnp.arange(width) // head
    return (i[:, None] == i[None, :]).astype(BF16)


def kernel(x, c, ctx, c_ctx, ada_w, ada_b, norm_mix_pre, norm_mix_post, norm_ffn_pre, norm_ffn_post, w_in, shift_mu, rw_w_up, rw_w0, rw_a_up, rw_a0, rw_k_k, rw_k_a, rw_r_k, rw_g_up, rw_gn_w, rw_gn_b, rw_v_down, rw_v_up, rw_v0, gla_conv, gla_a_up, gla_a_b, gla_gn_w, w_out, ffn_w_gate, ffn_w_up, ffn_w_down, moe_router, moe_w_gate, moe_w_up, moe_w_down):
    B, S, _ = x.shape
    n_ctx = ctx.shape[1]
    depth = w_in.shape[0]
    assert n_ctx == TB and S % MOE_TB == 0 and (n_ctx + S) % TW == 0 and depth == 2

    xc = None
    pad_rows = -(B + 1) % SUBLANES
    cvec = jnp.concatenate([c, c_ctx[None, :], jnp.zeros((pad_rows, D), F32)], axis=0)
    bd64 = _head_ones(RW_WIDTH, RW_HEAD)
    ada_b3 = ada_b.reshape(depth, 1, 6 * D)
    v_first = None
    out = None
    for i in range(depth):
        last = i == depth - 1
        mods = _adaln(cvec, ada_w, ada_b3, i)
        mod_x = mods[:B].reshape(B, 6, D)
        mod_c = jnp.broadcast_to(mods[B].reshape(1, 6, D), (B, 6, D))
        modtab = jnp.stack([mod_c, mod_x], axis=1)

        p_rw, p_gl = _inproj(ctx, x, xc, _row(norm_mix_pre[i]), modtab, w_in, i)

        prm = dict(
            mu=_row(shift_mu[i]),
            w_up=_block_diag2(rw_w_up[i]).astype(BF16), w0=_row(rw_w0[i]),
            a_up=_block_diag2(rw_a_up[i]).astype(BF16), a0=_row(rw_a0[i]),
            k_k=_row(rw_k_k[i]), k_a=_row(rw_k_a[i]), r_k=_row(rw_r_k[i]),
            g_up=rw_g_up[i].astype(BF16), bd64=bd64,
            gn_w=_row(rw_gn_w[i]), gn_b=_row(rw_gn_b[i]), gla_gn_w=_row(gla_gn_w[i]),
            w_out=w_out[i].astype(BF16), norm_post=_row(norm_mix_post[i]),
        )
        gate_pad = jnp.zeros((LANES - 2 * GLA_GATE_RANK, 2 * GLA_KW), F32)
        gla_prm = dict(conv=gla_conv[i].astype(F32), a_b=_row(gla_a_b[i]),
                       a_up=jnp.concatenate([_block_diag2(gla_a_up[i]), gate_pad], axis=0).astype(BF16))
        if i > 0:
            pad = LANES - RW_V_RANK
            prm["v_down"] = jnp.concatenate([rw_v_down[i - 1], jnp.zeros((RW_WIDTH, pad), F32)], axis=1).astype(BF16)
            prm["v_up"] = jnp.concatenate([rw_v_up[i - 1], jnp.zeros((pad, RW_WIDTH), F32)], axis=0).astype(BF16)
            prm["v0"] = _row(rw_v0[i - 1])

        (r, kk, vm, g, bonus, lw, bb, ke), (q, k, gv, og, lg) = _prep(p_rw, p_gl, prm, gla_prm,
                                                                     v_first if i > 0 else None)
        if i == 0:
            v_first = vm
        y, o = _scans(r, kk, vm, lw, bb, ke, q, k, gv, lg)
        xc = _readout(y, bonus, g, o, og, ctx, x, xc, modtab, prm, latents_only=last)

        jf = i // 2
        if i % 2 == 0:
            xc = _ffn(xc, modtab, _row(norm_ffn_pre[i]), _row(norm_ffn_post[i]),
                      ffn_w_gate[jf], ffn_w_up[jf], ffn_w_down[jf])
        else:
            router = jnp.concatenate([moe_router[jf], jnp.zeros((D, LANES - N_EXPERTS), F32)], axis=1)
            r_hi = router.astype(BF16)
            router = jnp.stack([r_hi, (router - r_hi.astype(F32)).astype(BF16)])
            out = _moe(xc, modtab, _row(norm_ffn_pre[i]), _row(norm_ffn_post[i]), router,
                       moe_w_gate[jf], moe_w_up[jf], moe_w_down[jf])
    return out
```

```python
import functools
import math

import jax
import jax.numpy as jnp
from jax import lax
from jax.experimental import pallas as pl
from jax.experimental.pallas import tpu as pltpu

F32, BF16 = jnp.float32, jnp.bfloat16
ACT = BF16

D = 1024
GRID_W = 64
RW_WIDTH = 512
RW_HEAD = 64
RW_RANK = 64
RW_G_RANK = 128
RW_V_RANK = 32
RW_GN_EPS = 64e-5
GLA_WIDTH = 512
GLA_HEADS = 4
GLA_DV = 128
GLA_DK = 64
GLA_KW = 256
GLA_GATE_RANK = 16
GLA_GATE_NORM = 16.0
GLA_NORM_EPS = 1e-5
D_FF = 2816
N_EXPERTS = 8
NORM_EPS = 1e-6
RW_COLS = 3 * RW_WIDTH + 4 * RW_RANK + RW_G_RANK
GLA_QKV = 2 * GLA_KW + GLA_WIDTH
GLA_COLS = GLA_QKV + GLA_WIDTH + 2 * GLA_GATE_RANK
GLA_PAD = 1664

LANES = 128
SUBLANES = 8
TB = 256
TW = 3 * TB
CH = 64
FCH = 256
HALO = 16
VMEM_LIMIT = 56 * 1024 * 1024


def _cparams(sem):
    return pltpu.CompilerParams(dimension_semantics=sem, vmem_limit_bytes=VMEM_LIMIT)


def _bdot(a, b):
    return jnp.dot(a.astype(BF16), b.astype(BF16), preferred_element_type=F32)


def _bdot_nt(a, b):
    return lax.dot_general(a.astype(BF16), b.astype(BF16), (((1,), (1,)), ((), ())),
                           preferred_element_type=F32)


def _bdot_tn(a, b):
    return lax.dot_general(a.astype(BF16), b.astype(BF16), (((0,), (0,)), ((), ())),
                           preferred_element_type=F32)


def _split_dot(a_exact, x):
    h1 = x.astype(BF16)
    r1 = x - h1.astype(F32)
    h2 = r1.astype(BF16)
    h3 = (r1 - h2.astype(F32)).astype(BF16)
    return (jnp.dot(a_exact, h1, preferred_element_type=F32)
            + jnp.dot(a_exact, h2, preferred_element_type=F32)
            + jnp.dot(a_exact, h3, preferred_element_type=F32))


def _seg_sum(x, bd, exact=True):
    hi = x.astype(BF16)
    out = jnp.dot(hi, bd, preferred_element_type=F32)
    if exact:
        lo = (x - hi.astype(F32)).astype(BF16)
        out = out + jnp.dot(lo, bd, preferred_element_type=F32)
    return out


def _sigmoid(x):
    return jax.nn.sigmoid(x)


def _rms(x, g):
    return x * lax.rsqrt(jnp.mean(x * x, axis=-1, keepdims=True) + NORM_EPS) * g


def _pair_stack(z, lo):
    return jnp.concatenate([jnp.where(lo, z, 0.0), jnp.where(lo, 0.0, z)], axis=0)


def _adaln_kernel(c_ref, w_ref, b_ref, o_ref):
    c = c_ref[...]
    s = c * _sigmoid(c)
    o_ref[...] = jnp.dot(s, w_ref[0], precision=lax.Precision.HIGHEST,
                         preferred_element_type=F32) + b_ref[0]


def _adaln(cvec, w, b, layer):
    rows = cvec.shape[0]
    n = w.shape[2]
    return pl.pallas_call(
        _adaln_kernel,
        out_shape=jax.ShapeDtypeStruct((rows, n), F32),
        grid=(n // D,),
        in_specs=[pl.BlockSpec((rows, D), lambda i: (0, 0)),
                  pl.BlockSpec((1, D, D), lambda i: (layer, 0, i)),
                  pl.BlockSpec((1, 1, D), lambda i: (layer, 0, i))],
        out_specs=pl.BlockSpec((rows, D), lambda i: (0, i)),
        compiler_params=_cparams(("arbitrary",)),
        name="adaln",
    )(cvec, w, b)


def _wide_mod(mod_ref, k):
    is_ctx = jnp.logical_and(pl.program_id(1) == 0, lax.broadcasted_iota(jnp.int32, (TW, 1), 0) < TB)
    return jnp.where(is_ctx, mod_ref[0, 0, k:k + 1], mod_ref[0, 1, k:k + 1])


def _wide_rows(refs):
    if len(refs) == 1:
        return refs[0][0]
    first = jnp.where(pl.program_id(1) == 0, refs[0][0], refs[1][0])
    return jnp.concatenate([first] + [r[0] for r in refs[2:]], axis=0)


def _inproj_kernel(n_src, *refs):
    g_ref, mod_ref, w_ref, prw_ref, pgl_ref = refs[n_src:]
    x = _wide_rows(refs[:n_src])
    h = _rms(x, g_ref[...]) * (1.0 + _wide_mod(mod_ref, 1)) + _wide_mod(mod_ref, 0)
    hb = h.astype(BF16)
    prw_ref[0] = jnp.dot(hb, w_ref[0, :, :RW_COLS].astype(BF16), preferred_element_type=F32).astype(ACT)
    pgl_ref[0, :, :GLA_COLS] = jnp.dot(hb, w_ref[0, :, RW_COLS:].astype(BF16),
                                       preferred_element_type=F32).astype(ACT)
    pgl_ref[0, :, GLA_COLS:] = jnp.zeros((TW, GLA_PAD - GLA_COLS), ACT)


def _wide_src(ctx, x, xc):
    if xc is not None:
        return [xc], [pl.BlockSpec((1, TW, D), lambda b, j: (b, j, 0))]
    per = TW // TB
    specs = [pl.BlockSpec((1, TB, D), lambda b, j: (b, 0, 0))]
    for k in range(per):
        specs.append(pl.BlockSpec((1, TB, D), lambda b, j, k=k: (b, jnp.maximum(j * per + k - 1, 0), 0)))
    return [ctx] + [x] * per, specs


def _wide_mod_spec():
    return pl.BlockSpec((1, 2, 6, D), lambda b, j: (b, 0, 0, 0))


def _const_spec(shape):
    nd = len(shape)
    return pl.BlockSpec(shape, lambda *_: (0,) * nd)


def _tile_spec(c):
    return pl.BlockSpec((1, TB, c), lambda b, j: (b, j, 0))


def _dir_tile_spec(c):
    return pl.BlockSpec((2, 1, TB, c), lambda b, j: (0, b, j, 0))


def _inproj(ctx, x, xc, g, modtab, w, layer):
    B = modtab.shape[0]
    N = xc.shape[1] if xc is not None else ctx.shape[1] + x.shape[1]
    srcs, src_specs = _wide_src(ctx, x, xc)
    wide = lambda c: pl.BlockSpec((1, TW, c), lambda b, j: (b, j, 0))
    return pl.pallas_call(
        functools.partial(_inproj_kernel, len(srcs)),
        out_shape=[jax.ShapeDtypeStruct((B, N, RW_COLS), ACT),
                   jax.ShapeDtypeStruct((B, N, GLA_PAD), ACT)],
        grid=(B, N // TW),
        in_specs=src_specs + [_const_spec((1, D)), _wide_mod_spec(),
                              pl.BlockSpec((1, D, RW_COLS + GLA_COLS), lambda b, j: (layer, 0, 0),
                                           pipeline_mode=pl.Buffered(1))],
        out_specs=[wide(RW_COLS), wide(GLA_PAD)],
        compiler_params=_cparams(("parallel", "parallel")),
        name="inproj",
    )(*srcs, g, modtab, w)


def _shift_table():
    t = jnp.arange(TB)
    prev = t[:, None] - 1 == t[None, :]
    nxt = t[:, None] + 1 == t[None, :]
    col = (t % GRID_W)[:, None]
    return jnp.stack([jnp.stack([prev, nxt]),
                      jnp.stack([prev & (col != 0), nxt & (col != GRID_W - 1)])]).astype(BF16)


def _rwprep_kernel(has_vres, n_tiles, p_ref, hu_ref, hd_ref, sh_ref, mu_ref, wup_ref, w0_ref, aup_ref,
                   a0_ref, kk_ref, ka_ref, rk_ref, gup_ref, bd_ref, *rest):
    if has_vres:
        vf_ref, vdn_ref, vup_ref, v0_ref = rest[:4]
        rest = rest[4:]
    r_out, kk_out, v_out, g_out, bon_out, lw_out, b_out, ke_out = rest
    j = pl.program_id(1)
    pb = p_ref[0]
    p = pb.astype(F32)
    prev1 = jnp.dot(sh_ref[0, 0], pb, preferred_element_type=F32)
    next1 = jnp.dot(sh_ref[0, 1], pb, preferred_element_type=F32)
    has_upper = jnp.where(j == 1, 0.0, 1.0)
    has_lower = jnp.where(j == n_tiles - 1, 0.0, 1.0)
    up = jnp.concatenate([hu_ref[0].astype(F32) * has_upper, p[:TB - GRID_W]], axis=0)
    down = jnp.concatenate([p[GRID_W:], hd_ref[0].astype(F32) * has_lower], axis=0)
    cls = lax.broadcasted_iota(jnp.int32, p.shape, 1) & jnp.where(j == 0, 1, 3)
    shifted = jnp.where(cls == 0, prev1, jnp.where(cls == 1, next1, jnp.where(cls == 2, up, down)))
    u = p + mu_ref[...] * (shifted - p)

    r = u[:, 0:RW_WIDTH]
    k = u[:, RW_WIDTH:2 * RW_WIDTH]
    v = u[:, 2 * RW_WIDTH:3 * RW_WIDTH]
    o = 3 * RW_WIDTH
    wd = u[:, o:o + 2 * RW_RANK]
    ad = u[:, o + 2 * RW_RANK:o + 4 * RW_RANK]
    gd = u[:, o + 4 * RW_RANK:]

    w_logit = w0_ref[...] + _bdot(jnp.tanh(wd), wup_ref[...])
    lw = -math.exp(-0.5) * _sigmoid(w_logit)
    a = _sigmoid(a0_ref[...] + _bdot(ad, aup_ref[...]))
    bd = bd_ref[...]
    kk = k * kk_ref[...]
    kk = kk * lax.rsqrt(jnp.maximum(_seg_sum(kk * kk, bd), 1e-24))
    g = _bdot(_sigmoid(gd), gup_ref[...])
    if has_vres:
        gate = _sigmoid(v0_ref[...] + _bdot(_bdot(v, vdn_ref[...]), vup_ref[...]))
        vm = v + (vf_ref[0].astype(F32) - v) * gate
    else:
        vm = v
    ke_sum = jnp.zeros_like(k)
    for d in range(2):
        a_d = a[:, d * RW_WIDTH:(d + 1) * RW_WIDTH]
        ke_d = k * (1.0 + (a_d - 1.0) * ka_ref[...])
        lw_out[d, 0] = lw[:, d * RW_WIDTH:(d + 1) * RW_WIDTH]
        b_out[d, 0] = (kk * a_d).astype(ACT)
        ke_out[d, 0] = ke_d.astype(ACT)
        ke_sum = ke_sum + ke_d
    r_out[0] = r.astype(ACT)
    kk_out[0] = kk.astype(ACT)
    v_out[0] = vm.astype(ACT)
    g_out[0] = g.astype(ACT)
    bon_out[0] = (_seg_sum(r * ke_sum * rk_ref[...], bd, exact=False) * vm).astype(ACT)


def _rwprep_call(p_rw, prm, v_first):
    B, N, _ = p_rw.shape
    hb = TB // GRID_W
    nhb = N // GRID_W
    has_vres = v_first is not None
    W2 = 2 * RW_WIDTH
    in_specs = [
        _tile_spec(RW_COLS),
        pl.BlockSpec((1, GRID_W, RW_COLS), lambda b, j: (b, jnp.maximum(j * hb - 1, 0), 0)),
        pl.BlockSpec((1, GRID_W, RW_COLS), lambda b, j: (b, jnp.minimum(j * hb + hb, nhb - 1), 0)),
        pl.BlockSpec((1, 2, TB, TB), lambda b, j: (jnp.minimum(j, 1), 0, 0, 0)),
        _const_spec((1, RW_COLS)), _const_spec((2 * RW_RANK, W2)), _const_spec((1, W2)),
        _const_spec((2 * RW_RANK, W2)), _const_spec((1, W2)),
        _const_spec((1, RW_WIDTH)), _const_spec((1, RW_WIDTH)), _const_spec((1, RW_WIDTH)),
        _const_spec((RW_G_RANK, RW_WIDTH)), _const_spec((RW_WIDTH, RW_WIDTH)),
    ]
    args = [p_rw, p_rw, p_rw, _shift_table(), prm["mu"], prm["w_up"], prm["w0"], prm["a_up"], prm["a0"],
            prm["k_k"], prm["k_a"], prm["r_k"], prm["g_up"], prm["bd64"]]
    if has_vres:
        in_specs += [_tile_spec(RW_WIDTH), _const_spec((RW_WIDTH, LANES)), _const_spec((LANES, RW_WIDTH)),
                     _const_spec((1, RW_WIDTH))]
        args += [v_first, prm["v_down"], prm["v_up"], prm["v0"]]
    tok = jax.ShapeDtypeStruct((B, N, RW_WIDTH), ACT)
    dtok = jax.ShapeDtypeStruct((2, B, N, RW_WIDTH), ACT)
    out_shapes = [tok] * 5 + [jax.ShapeDtypeStruct((2, B, N, RW_WIDTH), F32), dtok, dtok]
    out_specs = [_tile_spec(RW_WIDTH)] * 5 + [_dir_tile_spec(RW_WIDTH)] * 3
    return args, in_specs, out_shapes, out_specs


NCH = TB // CH


def _scan_kernel(rf_ref, kkf_ref, vf_ref, lwf_ref, bf_ref, kef_ref,
                 rb_ref, kkb_ref, vb_ref, lwb_ref, bb_ref, keb_ref,
                 gqf_ref, gkf_ref, gvf_ref, lgf_ref, gqb_ref, gkb_ref, gvb_ref, lgb_ref,
                 yf_ref, yb_ref, of_ref, ob_ref, st_ref, gst_ref):
    j = pl.program_id(1)

    @pl.when(j == 0)
    def _():
        st_ref[...] = jnp.zeros_like(st_ref)
        gst_ref[...] = jnp.zeros_like(gst_ref)

    ti = lax.broadcasted_iota(jnp.int32, (CH, LANES), 0)
    li = lax.broadcasted_iota(jnp.int32, (CH, LANES), 1)
    si = li & (CH - 1)
    lo = li < CH
    eye = jnp.where(ti == si, 1.0, 0.0)
    rblk = lax.broadcasted_iota(jnp.int32, (LANES, LANES), 0) // CH
    cblk = lax.broadcasted_iota(jnp.int32, (LANES, LANES), 1) // CH
    bdmask = rblk == cblk
    t64 = lax.broadcasted_iota(jnp.int32, (CH, CH), 0)
    s64 = lax.broadcasted_iota(jnp.int32, (CH, CH), 1)
    tri = [jnp.where(s64 <= t64, 1.0, 0.0).astype(BF16), jnp.where(s64 >= t64, 1.0, 0.0).astype(BF16)]
    m_incl = [si <= ti, si >= ti]
    m_strict = [si < ti, si > ti]
    n_pairs = RW_WIDTH // LANES
    in_refs = ((rf_ref, kkf_ref, vf_ref, lwf_ref, bf_ref, kef_ref),
               (rb_ref, kkb_ref, vb_ref, lwb_ref, bb_ref, keb_ref))
    y_refs = (yf_ref, yb_ref)

    def stack(z):
        return _pair_stack(z, lo).astype(BF16)

    st = {(d, p): st_ref[d, p] for d in range(2) for p in range(n_pairs)}

    def scan_step(step):
        cur = {}
        w_tot = {}
        for d in range(2):
            c = (NCH - 1 - step) if d == 1 else step
            rows = slice(c * CH, (c + 1) * CH)
            refs = in_refs[d]
            r, kk, v = (refs[i][0, rows, :].astype(F32) for i in range(3))
            lw = refs[3][0, 0, rows, :]
            b, ke = refs[4][0, 0, rows, :].astype(F32), refs[5][0, 0, rows, :].astype(F32)
            cum = _split_dot(tri[d], lw)
            tot = jnp.sum(lw, axis=0, keepdims=True)
            w_inv = jnp.exp(-cum)
            w_end = jnp.exp(tot - cum)
            w_tot[d] = jnp.exp(tot)
            rh = r * jnp.exp(cum)
            ah = -(kk * jnp.exp(cum - lw))
            bh, kh, bt, kt = b * w_inv, ke * w_inv, b * w_end, ke * w_end
            for p in range(n_pairs):
                sl = slice(LANES * p, LANES * (p + 1))
                cur[d, p] = dict(ah=ah[:, sl], rh=rh[:, sl], bh=bh[:, sl], kh=kh[:, sl], bt=bt[:, sl],
                                 kt=kt[:, sl], v=v[:, sl], rows=rows, sl=sl)
        yield
        for it in cur.values():
            it["lhs"] = jnp.concatenate([it["ah"], it["rh"]], axis=0).astype(BF16)
            rhs = jnp.concatenate([stack(it["bh"]), stack(it["kh"])], axis=0)
            it["a_all"] = _bdot_nt(it["lhs"], rhs)
        yield
        for (d, _), it in cur.items():
            a_all = it.pop("a_all")
            it["a_ab"] = jnp.where(m_strict[d], a_all[:CH, :LANES], 0.0)
            a_ak = jnp.where(m_strict[d], a_all[:CH, LANES:], 0.0)
            a_rb = jnp.where(m_incl[d], a_all[CH:, :LANES], 0.0)
            a_rk = jnp.where(m_incl[d], a_all[CH:, LANES:], 0.0)
            it["a_r"] = jnp.concatenate([a_rb, a_rk], axis=1).astype(BF16)
            it["v_bd"] = stack(it["v"])
            it["akv"] = _bdot(a_ak, it["v_bd"])
            it["t"] = eye + it["a_ab"]
            it["m"] = _bdot(it["a_ab"], stack(it["a_ab"]))
        yield
        for _ in range(int(math.log2(CH)) - 2):
            for it in cur.values():
                z = _bdot(it["m"], jnp.concatenate([stack(it["m"]), stack(it["t"])], axis=1))
                it["m"] = z[:, :LANES]
                it["t"] = it["t"] + z[:, LANES:]
            yield
        for it in cur.values():
            it["t"] = (it["t"] + _bdot(it["m"], stack(it["t"]))).astype(BF16)
        yield
        for it in cur.values():
            ta = jnp.dot(it["t"], jnp.concatenate([stack(it["ah"]), stack(it["akv"])], axis=1),
                         preferred_element_type=F32)
            it["lhs2"] = jnp.concatenate([ta[:, :LANES].astype(BF16), it["lhs"][CH:]], axis=0)
            it["cc"] = ta[:, LANES:]
            it["rhs_t"] = jnp.concatenate([it["bt"], it["kt"]], axis=0).astype(BF16)
        yield
        z1 = {k: _bdot_nt(it["lhs2"], st[k]) for k, it in cur.items()}
        yield
        u = {k: z1[k][:CH] + it["cc"] for k, it in cur.items()}
        upd = {k: _bdot_tn(jnp.concatenate([u[k], it["v"]], axis=0), it["rhs_t"]) for k, it in cur.items()}
        yield
        for (d, p), it in cur.items():
            y = z1[d, p][CH:] + jnp.dot(it["a_r"], jnp.concatenate([stack(u[d, p]), it["v_bd"]], axis=0),
                                        preferred_element_type=F32)
            y_refs[d][0, it["rows"], it["sl"]] = y.astype(ACT)
            st[d, p] = st[d, p] * w_tot[d][:, it["sl"]] + jnp.where(bdmask, upd[d, p], 0.0)
        yield

    gla_refs = ((gqf_ref, gkf_ref, gvf_ref, lgf_ref), (gqb_ref, gkb_ref, gvb_ref, lgb_ref))
    o_refs = (of_ref, ob_ref)
    g_pairs = GLA_KW // LANES
    gst = {(d, p): gst_ref[d, p] for d in range(2) for p in range(g_pairs)}
    lo2 = lax.broadcasted_iota(jnp.int32, (LANES, LANES), 1) < CH
    zeros_v = jnp.zeros((CH, GLA_DV), F32)

    def gla_step(step):
        cur = {}
        dec = {}
        for d in range(2):
            c = (NCH - 1 - step) if d == 1 else step
            rows = slice(c * CH, (c + 1) * CH)
            refs = gla_refs[d]
            q, k, v = (refs[i][0, rows, :].astype(F32) for i in range(3))
            lg = refs[3][0, 0, rows, :]
            cum = _split_dot(tri[d], lg)
            tot = jnp.sum(lg, axis=0, keepdims=True)
            dec[d] = jnp.exp(tot)
            qd = q * jnp.exp(cum)
            ki = k * jnp.exp(-cum)
            kend = k * jnp.exp(tot - cum)
            for p in range(g_pairs):
                sl = slice(LANES * p, LANES * (p + 1))
                cur[d, p] = dict(qd=qd[:, sl].astype(BF16), ki=ki[:, sl], kend=kend[:, sl], rows=rows, sl=sl,
                                 v0=v[:, 2 * LANES * p:2 * LANES * p + LANES],
                                 v1=v[:, 2 * LANES * p + LANES:2 * LANES * (p + 1)])
        yield
        for it in cur.values():
            it["att"] = _bdot_nt(it["qd"], _pair_stack(it["ki"], lo))
        yield
        for (d, _), it in cur.items():
            att = jnp.where(m_incl[d], it["att"], 0.0)
            v_bd = jnp.concatenate([jnp.concatenate([it["v0"], zeros_v], axis=1),
                                    jnp.concatenate([zeros_v, it["v1"]], axis=1)], axis=0)
            it["o"] = _bdot(att, v_bd)
            it["upd"] = _bdot_tn(jnp.concatenate([it["v0"], it["v1"]], axis=0), _pair_stack(it["kend"], lo))
        yield
        for (d, p), it in cur.items():
            s = gst[d, p]
            s_bd = jnp.concatenate([jnp.where(lo2, s, 0.0), jnp.where(lo2, 0.0, s)], axis=0)
            o = it["o"] + _bdot_nt(it["qd"], s_bd)
            o_refs[d][0, it["rows"], 2 * LANES * p:2 * LANES * (p + 1)] = o.astype(ACT)
            gst[d, p] = s * dec[d][:, it["sl"]] + it["upd"]
        yield

    n_stages = 12
    n_dep = 3
    gla_stages = (1, 4, 7, 10)
    pipeline = [(scan_step(s), gla_step(s)) for s in range(NCH)]
    for slot in range(n_stages + n_dep * (NCH - 1)):
        for s, (rw_gen, gla_gen) in enumerate(pipeline):
            stage = slot - n_dep * s
            if 0 <= stage < n_stages:
                next(rw_gen)
                if stage in gla_stages:
                    next(gla_gen)
    for (d, p), s in st.items():
        st_ref[d, p] = s
    for (d, p), s in gst.items():
        gst_ref[d, p] = s


def _bidir_specs(c, n_tiles):
    def back(j):
        return jnp.where(j == 0, 0, n_tiles - j)

    tok_f = pl.BlockSpec((1, TB, c), lambda b, j: (b, j, 0))
    tok_b = pl.BlockSpec((1, TB, c), lambda b, j: (b, back(j), 0))
    dir_f = pl.BlockSpec((1, 1, TB, c), lambda b, j: (0, b, j, 0))
    dir_b = pl.BlockSpec((1, 1, TB, c), lambda b, j: (1, b, back(j), 0))
    return tok_f, tok_b, dir_f, dir_b


def _scans(r, kk, v, lw, b, ke, q, k, gv, lg):
    B, N, _ = r.shape
    nt = N // TB
    tok_f, tok_b, dir_f, dir_b = _bidir_specs(RW_WIDTH, nt)
    kf, kb, dkf, dkb = _bidir_specs(GLA_KW, nt)
    out = jax.ShapeDtypeStruct((B, N, RW_WIDTH), ACT)
    assert GLA_WIDTH == RW_WIDTH
    yf, yb, of, ob = pl.pallas_call(
        _scan_kernel,
        out_shape=[out] * 4,
        grid=(B, nt),
        in_specs=[tok_f, tok_f, tok_f, dir_f, dir_f, dir_f, tok_b, tok_b, tok_b, dir_b, dir_b, dir_b,
                  kf, kf, tok_f, dkf, kb, kb, tok_b, dkb],
        out_specs=[tok_f, tok_b, tok_f, tok_b],
        scratch_shapes=[pltpu.VMEM((2, RW_WIDTH // LANES, LANES, LANES), F32),
                        pltpu.VMEM((2, GLA_KW // LANES, GLA_DV, LANES), F32)],
        compiler_params=_cparams(("parallel", "arbitrary")),
        name="scans",
    )(r, kk, v, lw, b, ke, r, kk, v, lw, b, ke, q, k, gv, lg, q, k, gv, lg)
    return (yf, yb), (of, ob)


CONV_K = TB + LANES


def _conv_shift_table():
    t = jnp.arange(TB)[:, None]
    s = jnp.arange(CONV_K)[None, :]
    prev = jnp.where(t == 0, s == TB + 2 * HALO - 1, s == t - 1)
    nxt = jnp.where(t == TB - 1, s == TB, s == t + 1)
    return jnp.stack([prev, nxt]).astype(BF16)


def _glaprep_kernel(n_tiles, p_ref, hp_ref, hn_ref, sh_ref, cw_ref, aup_ref, ab_ref, q_out, k_out, v_out,
                    og_out, lg_out):
    j = pl.program_id(1)
    ub = p_ref[0][:, :GLA_QKV]
    u = ub.astype(F32)
    has_prev = jnp.where(j <= 1, 0.0, 1.0)
    has_next = jnp.where(jnp.logical_or(j == 0, j == n_tiles - 1), 0.0, 1.0)
    ext = jnp.concatenate([ub,
                           (hn_ref[0][:, :GLA_QKV].astype(F32) * has_next).astype(BF16),
                           (hp_ref[0][:, :GLA_QKV].astype(F32) * has_prev).astype(BF16),
                           jnp.zeros((CONV_K - TB - 2 * HALO, GLA_QKV), BF16)], axis=0)
    prev1 = jnp.dot(sh_ref[0], ext, preferred_element_type=F32)
    next1 = jnp.dot(sh_ref[1], ext, preferred_element_type=F32)
    cw = cw_ref[...]
    conv = cw[0:1] * prev1 + cw[1:2] * u + cw[2:3] * next1
    qkv = conv * _sigmoid(conv)
    q_out[0] = (qkv[:, :GLA_KW] * (GLA_DK ** -0.5)).astype(ACT)
    k_out[0] = qkv[:, GLA_KW:2 * GLA_KW].astype(ACT)
    v_out[0] = qkv[:, 2 * GLA_KW:].astype(ACT)
    og_out[0] = p_ref[0][:, GLA_QKV:GLA_QKV + GLA_WIDTH]
    z = _bdot(p_ref[0][:, GLA_QKV + GLA_WIDTH:], aup_ref[...]) + ab_ref[...]
    lg = (jnp.minimum(z, 0.0) - jnp.log1p(jnp.exp(-jnp.abs(z)))) * (1.0 / GLA_GATE_NORM)
    for d in range(2):
        lg_out[d, 0] = lg[:, d * GLA_KW:(d + 1) * GLA_KW]


def _glaprep_call(p_gl, prm):
    B, N, _ = p_gl.shape
    sub = HALO
    hb = TB // sub
    nhb = N // sub
    args = [p_gl, p_gl, p_gl, _conv_shift_table(), prm["conv"], prm["a_up"], prm["a_b"]]
    in_specs = [_tile_spec(GLA_PAD),
                pl.BlockSpec((1, sub, GLA_PAD), lambda b, j: (b, jnp.maximum(j * hb - 1, 0), 0)),
                pl.BlockSpec((1, sub, GLA_PAD), lambda b, j: (b, jnp.minimum(j * hb + hb, nhb - 1), 0)),
                _const_spec((2, TB, CONV_K)),
                _const_spec((3, GLA_QKV)), _const_spec((LANES, 2 * GLA_KW)), _const_spec((1, 2 * GLA_KW))]
    out_shapes = [jax.ShapeDtypeStruct((B, N, GLA_KW), ACT), jax.ShapeDtypeStruct((B, N, GLA_KW), ACT),
                  jax.ShapeDtypeStruct((B, N, GLA_WIDTH), ACT), jax.ShapeDtypeStruct((B, N, GLA_WIDTH), ACT),
                  jax.ShapeDtypeStruct((2, B, N, GLA_KW), F32)]
    out_specs = [_tile_spec(GLA_KW), _tile_spec(GLA_KW), _tile_spec(GLA_WIDTH), _tile_spec(GLA_WIDTH),
                 _dir_tile_spec(GLA_KW)]
    return args, in_specs, out_shapes, out_specs


def _prep_kernel(has_vres, n_tiles, n_rw_in, n_gla_in, n_rw_out, *refs):
    rw_in, refs = refs[:n_rw_in], refs[n_rw_in:]
    gla_in, outs = refs[:n_gla_in], refs[n_gla_in:]
    _rwprep_kernel(has_vres, n_tiles, *rw_in, *outs[:n_rw_out])
    _glaprep_kernel(n_tiles, *gla_in, *outs[n_rw_out:])


def _prep(p_rw, p_gl, prm, gla_prm, v_first):
    B, N, _ = p_rw.shape
    nt = N // TB
    rw_args, rw_in, rw_shapes, rw_out = _rwprep_call(p_rw, prm, v_first)
    gla_args, gla_in, gla_shapes, gla_out = _glaprep_call(p_gl, gla_prm)
    outs = pl.pallas_call(
        functools.partial(_prep_kernel, v_first is not None, nt, len(rw_args), len(gla_args), len(rw_shapes)),
        out_shape=rw_shapes + gla_shapes,
        grid=(B, nt),
        in_specs=rw_in + gla_in,
        out_specs=rw_out + gla_out,
        compiler_params=_cparams(("parallel", "parallel")),
        name="prep",
    )(*rw_args, *gla_args)
    return outs[:len(rw_shapes)], outs[len(rw_shapes):]


def _readout_kernel(group, first, split_src, *refs):
    tok = [refs[i * group:(i + 1) * group] for i in range(7)]
    refs = refs[7 * group:]
    n_res = group + 1 if split_src else group
    res_refs, refs = refs[:n_res], refs[n_res:]
    mod_ref, gnw_ref, gnb_ref, ggn_ref, wout_ref, gpost_ref, bd_ref, xo_ref = refs
    bd = bd_ref[...]
    starts_with_ctx = jnp.logical_and(first == 0, pl.program_id(1) == 0)
    for k in range(group):
        yf_ref, yb_ref, bon_ref, g_ref, of_ref, ob_ref, og_ref = (t[k] for t in tok)
        gate = mod_ref[0, 1, 2:3]
        if split_src:
            x_res = res_refs[k + 1][0]
            if k == 0:
                x_res = jnp.where(starts_with_ctx, res_refs[0][0], x_res)
        else:
            x_res = res_refs[k][0]
        if k == 0 and first == 0:
            gate = jnp.where(starts_with_ctx, mod_ref[0, 0, 2:3], gate)
        y = yf_ref[0].astype(F32) + yb_ref[0].astype(F32)
        mu = _seg_sum(y, bd, exact=False) * (1.0 / RW_HEAD)
        yc = y - mu
        var = _seg_sum(yc * yc, bd, exact=False) * (1.0 / RW_HEAD)
        yn = yc * lax.rsqrt(var + RW_GN_EPS) * gnw_ref[...] + gnb_ref[...]
        rw = (yn + bon_ref[0].astype(F32)) * g_ref[0].astype(F32)
        o = of_ref[0].astype(F32) + ob_ref[0].astype(F32)
        og = og_ref[0].astype(F32)
        parts = [rw.astype(BF16)]
        for h in range(GLA_HEADS):
            sl = slice(GLA_DV * h, GLA_DV * (h + 1))
            oh = o[:, sl]
            on = oh * lax.rsqrt(jnp.mean(oh * oh, axis=-1, keepdims=True) + GLA_NORM_EPS)
            ogh = og[:, sl]
            parts.append((on * ggn_ref[:, sl] * (ogh * _sigmoid(ogh))).astype(BF16))
        cat = jnp.concatenate(parts, axis=1)
        mx = jnp.dot(cat, wout_ref[...], preferred_element_type=F32)
        xo_ref[0, k * TB:(k + 1) * TB, :] = x_res + gate * _rms(mx, gpost_ref[...])


def _readout(y, bonus, g, o, og, ctx, x, xc, modtab, prm, latents_only):
    B, N, _ = bonus.shape
    first = 1 if latents_only else 0
    n_tiles = N // TB - first
    group = max(k for k in (4, 3, 2, 1) if n_tiles % k == 0)

    def tiles(c, shift=0):
        return [pl.BlockSpec((1, TB, c), lambda b, j, k=k: (b, jnp.maximum(first + group * j + k + shift, 0), 0))
                for k in range(group)]

    if xc is None:
        res, res_specs = [ctx] + [x] * group, [pl.BlockSpec((1, TB, D), lambda b, j: (b, 0, 0))] + tiles(D, -1)
    else:
        res, res_specs = [xc] * group, tiles(D)
    tok_args, tok_specs = [], []
    for arr in (y[0], y[1], bonus, g, o[0], o[1], og):
        tok_args += [arr] * group
        tok_specs += tiles(arr.shape[-1])
    return pl.pallas_call(
        functools.partial(_readout_kernel, group, first, xc is None),
        out_shape=jax.ShapeDtypeStruct((B, n_tiles * TB, D), F32),
        grid=(B, n_tiles // group),
        in_specs=tok_specs + res_specs + [_wide_mod_spec(),
                  _const_spec((1, RW_WIDTH)), _const_spec((1, RW_WIDTH)), _const_spec((1, GLA_WIDTH)),
                  _const_spec((D, D)), _const_spec((1, D)), _const_spec((RW_WIDTH, RW_WIDTH))],
        out_specs=pl.BlockSpec((1, group * TB, D), lambda b, j: (b, j, 0)),
        compiler_params=_cparams(("parallel", "parallel")),
        name="readout",
    )(*tok_args, *res, modtab, prm["gn_w"], prm["gn_b"], prm["gla_gn_w"],
      prm["w_out"], prm["norm_post"], prm["bd64"])


def _swiglu_acc(hb, wg_ref, wu_ref, wd_ref):
    acc = jnp.zeros((hb.shape[0], D), F32)
    for c in range(D_FF // FCH):
        sl = slice(c * FCH, (c + 1) * FCH)
        gate = jnp.dot(hb, wg_ref[:, sl].astype(BF16), preferred_element_type=F32)
        up = jnp.dot(hb, wu_ref[:, sl].astype(BF16), preferred_element_type=F32)
        act = (gate * _sigmoid(gate) * up).astype(BF16)
        acc = acc + jnp.dot(act, wd_ref[sl, :].astype(BF16), preferred_element_type=F32)
    return acc


def _ffn_kernel(x_ref, mod_ref, gpre_ref, gpost_ref, wg_ref, wu_ref, wd_ref, xo_ref):
    x = x_ref[0]
    hb = (_rms(x, gpre_ref[...]) * (1.0 + _wide_mod(mod_ref, 4)) + _wide_mod(mod_ref, 3)).astype(BF16)
    fx = _swiglu_acc(hb, wg_ref, wu_ref, wd_ref)
    xo_ref[0] = x + _wide_mod(mod_ref, 5) * _rms(fx, gpost_ref[...])


def _single_buffered(shape):
    nd = len(shape)
    return pl.BlockSpec(shape, lambda *_: (0,) * nd, pipeline_mode=pl.Buffered(1))


def _ffn(xc, modtab, g_pre, g_post, wg, wu, wd):
    B, N, _ = xc.shape
    wide = pl.BlockSpec((1, TW, D), lambda b, j: (b, j, 0))
    return pl.pallas_call(
        _ffn_kernel,
        out_shape=jax.ShapeDtypeStruct((B, N, D), F32),
        grid=(B, N // TW),
        in_specs=[wide, _wide_mod_spec(), _const_spec((1, D)), _const_spec((1, D)),
                  _single_buffered((D, D_FF)), _single_buffered((D, D_FF)), _single_buffered((D_FF, D))],
        out_specs=wide,
        compiler_params=_cparams(("parallel", "parallel")),
        name="ffn",
    )(xc, modtab, g_pre, g_post, wg, wu, wd)


MOE_TB = 512
MOE_SEG = 16
MOE_TM = 512
MOE_R = 2 * MOE_TB + N_EXPERTS * MOE_SEG
MOE_NP = MOE_R // MOE_SEG


def _moe_route_kernel(x_ref, mod_ref, gpre_ref, router_ref, h_ref, info_ref, infot_ref, cnt_ref):
    mod = mod_ref[0, 0]
    h = _rms(x_ref[0], gpre_ref[...]) * (1.0 + mod[4:5]) + mod[3:4]
    hb = h.astype(BF16)
    h_ref[0] = hb
    lane = lax.broadcasted_iota(jnp.int32, (MOE_TB, LANES), 1)
    h_lo = (h - hb.astype(F32)).astype(BF16)
    logits = (jnp.dot(hb, router_ref[0], preferred_element_type=F32)
              + jnp.dot(hb, router_ref[1], preferred_element_type=F32)
              + jnp.dot(h_lo, router_ref[0], preferred_element_type=F32))
    logits = jnp.where(lane < N_EXPERTS, logits, -jnp.inf)
    v1 = jnp.max(logits, axis=-1, keepdims=True)
    i1 = jnp.min(jnp.where(logits == v1, lane, LANES), axis=-1, keepdims=True)
    rest = jnp.where(lane == i1, -jnp.inf, logits)
    v2 = jnp.max(rest, axis=-1, keepdims=True)
    i2 = jnp.min(jnp.where(rest == v2, lane, LANES), axis=-1, keepdims=True)
    ex = jnp.exp(v2 - v1)
    w1 = 1.0 / (1.0 + ex)
    w2 = ex * w1
    e1 = jnp.where(lane == i1, 1.0, 0.0)
    e2 = jnp.where(lane == i2, 1.0, 0.0)
    es = e1 + e2
    t = lax.broadcasted_iota(jnp.int32, (MOE_TB, MOE_TB), 0)
    s = lax.broadcasted_iota(jnp.int32, (MOE_TB, MOE_TB), 1)
    before = jnp.where(s < t, 1.0, 0.0).astype(BF16)
    rank = jnp.dot(before, es.astype(BF16), preferred_element_type=F32)
    cnt = jnp.sum(es, axis=0, keepdims=True)
    segs = jnp.floor((cnt + (MOE_SEG - 1)) * (1.0 / MOE_SEG))
    ea = lax.broadcasted_iota(jnp.int32, (LANES, LANES), 0)
    eb = lax.broadcasted_iota(jnp.int32, (LANES, LANES), 1)
    earlier = jnp.where(ea < eb, 1.0, 0.0).astype(BF16)
    start = jnp.dot(jnp.broadcast_to(segs, (SUBLANES, LANES)).astype(BF16), earlier,
                    preferred_element_type=F32)[0:1] * MOE_SEG
    pos = rank + start
    d1 = jnp.sum(e1 * pos, axis=-1, keepdims=True)
    d2 = jnp.sum(e2 * pos, axis=-1, keepdims=True)
    info = jnp.where(lane == 0, d1, jnp.where(lane == 1, d2, jnp.where(lane == 2, w1, jnp.where(lane == 3, w2, 0.0))))
    info_ref[0] = info
    infot_ref[0] = jnp.transpose(info)[0:SUBLANES]
    cnt_ref[0] = jnp.broadcast_to(cnt, (SUBLANES, LANES))


def _moe_route(xs, modtab, g_pre, router):
    B, S, _ = xs.shape
    per = S // MOE_TB
    nb = B * per
    blk = lambda c: pl.BlockSpec((1, MOE_TB, c), lambda i: (i // per, i % per, 0))
    flat = lambda r, c: pl.BlockSpec((1, r, c), lambda i: (i, 0, 0))
    return pl.pallas_call(
        _moe_route_kernel,
        out_shape=[jax.ShapeDtypeStruct((nb, MOE_TB, D), BF16), jax.ShapeDtypeStruct((nb, MOE_TB, LANES), F32),
                   jax.ShapeDtypeStruct((nb, SUBLANES, MOE_TB), F32),
                   jax.ShapeDtypeStruct((nb, SUBLANES, LANES), F32)],
        grid=(nb,),
        in_specs=[blk(D), pl.BlockSpec((1, 1, 6, D), lambda i: (i // per, 1, 0, 0)),
                  _const_spec((1, D)), _const_spec((2, D, LANES))],
        out_specs=[flat(MOE_TB, D), flat(MOE_TB, LANES), flat(SUBLANES, MOE_TB), flat(SUBLANES, LANES)],
        compiler_params=_cparams(("parallel",)),
        name="moe_route",
    )(xs, modtab, g_pre, router)


def _moe_plan(cnt, n_tiles):
    pc = (cnt + MOE_SEG - 1) // MOE_SEG * MOE_SEG
    inc = jnp.cumsum(pc, axis=1)
    loff = inc - pc
    reg = (jnp.sum(pc, axis=0) + MOE_TM - 1) // MOE_TM * MOE_TM
    gend = jnp.cumsum(reg)
    goff = (gend - reg)[None, :] + jnp.cumsum(pc, axis=0) - pc
    rows = jnp.arange(MOE_NP, dtype=jnp.int32) * MOE_SEG
    e_p = jnp.sum((rows[None, :, None] >= inc[:, None, :]).astype(jnp.int32), axis=-1)
    e_c = jnp.minimum(e_p, N_EXPERTS - 1)
    pick = (e_c[:, :, None] == jnp.arange(N_EXPERTS)[None, None, :]).astype(jnp.int32)
    dst = jnp.sum(pick * (goff - loff)[:, None, :], axis=-1) + rows[None, :]
    dst = jnp.where(e_p < N_EXPERTS, dst, 0).astype(jnp.int32)
    n_valid = (inc[:, -1] // MOE_SEG).astype(jnp.int32)
    trow = jnp.arange(n_tiles, dtype=jnp.int32) * MOE_TM
    te = jnp.sum((trow[:, None] >= gend[None, :]).astype(jnp.int32), axis=-1)
    valid = te < N_EXPERTS
    last = gend[-1] // MOE_TM - 1
    te = jnp.where(valid, te, te[last]).astype(jnp.int32)
    src = jnp.where(valid, jnp.arange(n_tiles, dtype=jnp.int32), last).astype(jnp.int32)
    fresh = valid & jnp.concatenate([jnp.ones((1,), bool), te[1:] != te[:-1]])
    return dst, n_valid, te, src, valid.astype(jnp.int32), fresh.astype(jnp.int32)


def _piece_copy(src_ref, src_row, dst_ref, dst_row, sem):
    return pltpu.make_async_copy(src_ref.at[pl.ds(src_row, MOE_SEG)], dst_ref.at[pl.ds(dst_row, MOE_SEG)], sem)


def _moe_gather_kernel(dst_ref, nv_ref, h_ref, infot_ref, xg_in_ref, xg_ref, buf_ref, sem_ref):
    del xg_in_ref
    i = pl.program_id(0)
    it = infot_ref[0]
    rr = lax.broadcasted_iota(jnp.int32, (MOE_R, MOE_TB), 0).astype(F32)
    onehot = jnp.where(rr == it[0:1], 1.0, jnp.where(rr == it[1:2], 1.0, 0.0)).astype(BF16)
    slot = i % 2
    buf_ref[slot] = jnp.dot(onehot, h_ref[0], preferred_element_type=F32).astype(BF16)

    def piece(blk, p):
        return _piece_copy(buf_ref.at[blk % 2], pl.multiple_of(p * MOE_SEG, MOE_SEG),
                           xg_ref, pl.multiple_of(dst_ref[blk, p], MOE_SEG), sem_ref.at[blk % 2, p])

    def start(p, c):
        piece(i, p).start()
        return c

    lax.fori_loop(0, nv_ref[i], start, 0)

    def wait_block(blk):
        def wait(p, c):
            piece(blk, p).wait()
            return c
        lax.fori_loop(0, nv_ref[blk], wait, 0)

    @pl.when(i > 0)
    def _():
        wait_block(i - 1)

    @pl.when(i == pl.num_programs(0) - 1)
    def _():
        wait_block(i)


def _moe_gather(dst, n_valid, h, info_t, n_rows):
    nb = h.shape[0]
    flat = lambda r, c: pl.BlockSpec((1, r, c), lambda i, *_: (i, 0, 0))
    return pl.pallas_call(
        _moe_gather_kernel,
        out_shape=jax.ShapeDtypeStruct((n_rows, D), BF16),
        grid_spec=pltpu.PrefetchScalarGridSpec(
            num_scalar_prefetch=2, grid=(nb,),
            in_specs=[flat(MOE_TB, D), flat(SUBLANES, MOE_TB), pl.BlockSpec(memory_space=pl.ANY)],
            out_specs=pl.BlockSpec(memory_space=pl.ANY),
            scratch_shapes=[pltpu.VMEM((2, MOE_R, D), BF16), pltpu.SemaphoreType.DMA((2, MOE_NP))]),
        input_output_aliases={4: 0},
        compiler_params=_cparams(("arbitrary",)),
        name="moe_gather",
    )(dst, n_valid, h, info_t, jnp.zeros((n_rows, D), BF16))


N_FCH = D_FF // FCH


def _moe_ffn_kernel(te_ref, src_ref, valid_ref, fresh_ref, x_ref, wg_hbm, wu_hbm, wd_hbm, y_ref,
                    wg_ref, wu_ref, wd_ref, sem_ref):
    i = pl.program_id(0)
    e = te_ref[i]
    fresh = fresh_ref[i] == 1

    def slice_copies(c):
        cols = pl.ds(c * FCH, FCH)
        return (pltpu.make_async_copy(wg_hbm.at[e, :, cols], wg_ref.at[:, cols], sem_ref.at[0, c]),
                pltpu.make_async_copy(wu_hbm.at[e, :, cols], wu_ref.at[:, cols], sem_ref.at[1, c]),
                pltpu.make_async_copy(wd_hbm.at[e, cols, :], wd_ref.at[cols, :], sem_ref.at[2, c]))

    @pl.when(fresh)
    def _():
        for c in range(N_FCH):
            for cp in slice_copies(c):
                cp.start()
        hb = x_ref[...]
        acc = jnp.zeros((MOE_TM, D), F32)
        for c in range(N_FCH):
            for cp in slice_copies(c):
                cp.wait()
            sl = slice(c * FCH, (c + 1) * FCH)
            gate = jnp.dot(hb, wg_ref[:, sl].astype(BF16), preferred_element_type=F32)
            up = jnp.dot(hb, wu_ref[:, sl].astype(BF16), preferred_element_type=F32)
            act = (gate * _sigmoid(gate) * up).astype(BF16)
            acc = acc + jnp.dot(act, wd_ref[sl, :].astype(BF16), preferred_element_type=F32)
        y_ref[...] = acc.astype(BF16)

    @pl.when(jnp.logical_and(valid_ref[i] == 1, jnp.logical_not(fresh)))
    def _():
        y_ref[...] = _swiglu_acc(x_ref[...], wg_ref, wu_ref, wd_ref).astype(BF16)

    @pl.when(valid_ref[i] == 0)
    def _():
        y_ref[...] = jnp.zeros_like(y_ref)


def _moe_ffn(te, src, valid, fresh, xg, wg, wu, wd):
    n_rows = xg.shape[0]
    hbm = pl.BlockSpec(memory_space=pl.ANY)
    return pl.pallas_call(
        _moe_ffn_kernel,
        out_shape=jax.ShapeDtypeStruct((n_rows, D), BF16),
        grid_spec=pltpu.PrefetchScalarGridSpec(
            num_scalar_prefetch=4, grid=(n_rows // MOE_TM,),
            in_specs=[pl.BlockSpec((MOE_TM, D), lambda i, te, src, *_: (src[i], 0)), hbm, hbm, hbm],
            out_specs=pl.BlockSpec((MOE_TM, D), lambda i, *_: (i, 0)),
            scratch_shapes=[pltpu.VMEM((D, D_FF), F32), pltpu.VMEM((D, D_FF), F32), pltpu.VMEM((D_FF, D), F32),
                            pltpu.SemaphoreType.DMA((3, N_FCH))]),
        compiler_params=_cparams(("arbitrary",)),
        name="moe_ffn",
    )(te, src, valid, fresh, xg, wg, wu, wd)


def _moe_combine_kernel(dst_ref, nv_ref, x_ref, mod_ref, gpost_ref, info_ref, yg_ref, xo_ref, buf_ref, sem_ref):
    i = pl.program_id(0)

    def piece(blk, p):
        return _piece_copy(yg_ref, pl.multiple_of(dst_ref[blk, p], MOE_SEG),
                           buf_ref.at[blk % 2], pl.multiple_of(p * MOE_SEG, MOE_SEG), sem_ref.at[blk % 2, p])

    def fetch(blk):
        def start(p, c):
            piece(blk, p).start()
            return c

        def clear(p, c):
            buf_ref[blk % 2, pl.ds(pl.multiple_of(p * MOE_SEG, MOE_SEG), MOE_SEG), :] = jnp.zeros((MOE_SEG, D), BF16)
            return c

        lax.fori_loop(0, nv_ref[blk], start, 0)
        lax.fori_loop(nv_ref[blk], MOE_NP, clear, 0)

    @pl.when(i == 0)
    def _():
        fetch(i)

    @pl.when(i + 1 < pl.num_programs(0))
    def _():
        fetch(i + 1)

    info = info_ref[0]
    rr = lax.broadcasted_iota(jnp.int32, (MOE_TB, MOE_R), 1).astype(F32)
    comb = jnp.where(rr == info[:, 0:1], info[:, 2:3], jnp.where(rr == info[:, 1:2], info[:, 3:4], 0.0)).astype(BF16)

    def wait(p, c):
        piece(i, p).wait()
        return c

    lax.fori_loop(0, nv_ref[i], wait, 0)
    fx = jnp.dot(comb, buf_ref[i % 2], preferred_element_type=F32)
    xo_ref[0] = x_ref[0] + mod_ref[0, 0][5:6] * _rms(fx, gpost_ref[...])


def _moe_combine(dst, n_valid, xs, modtab, g_post, info, yg):
    B, S, _ = xs.shape
    per = S // MOE_TB
    blk = pl.BlockSpec((1, MOE_TB, D), lambda i, *_: (i // per, i % per, 0))
    return pl.pallas_call(
        _moe_combine_kernel,
        out_shape=jax.ShapeDtypeStruct((B, S, D), F32),
        grid_spec=pltpu.PrefetchScalarGridSpec(
            num_scalar_prefetch=2, grid=(B * per,),
            in_specs=[blk, pl.BlockSpec((1, 1, 6, D), lambda i, *_: (i // per, 1, 0, 0)),
                      pl.BlockSpec((1, D), lambda i, *_: (0, 0)),
                      pl.BlockSpec((1, MOE_TB, LANES), lambda i, *_: (i, 0, 0)),
                      pl.BlockSpec(memory_space=pl.ANY)],
            out_specs=blk,
            scratch_shapes=[pltpu.VMEM((2, MOE_R, D), BF16), pltpu.SemaphoreType.DMA((2, MOE_NP))]),
        compiler_params=_cparams(("arbitrary",)),
        name="moe_combine",
    )(dst, n_valid, xs, modtab, g_post, info, yg)


def _moe(xs, modtab, g_pre, g_post, router, wg, wu, wd):
    B, S, _ = xs.shape
    nb = B * S // MOE_TB
    h, info, info_t, cnt = _moe_route(xs, modtab, g_pre, router)
    worst = 2 * B * S + nb * N_EXPERTS * (MOE_SEG - 1) + N_EXPERTS * (MOE_TM - 1)
    n_tiles = -(-worst // MOE_TM)
    dst, n_valid, te, src, valid, fresh = _moe_plan(cnt[:, 0, :N_EXPERTS].astype(jnp.int32), n_tiles)
    xg = _moe_gather(dst, n_valid, h, info_t, n_tiles * MOE_TM)
    yg = _moe_ffn(te, src, valid, fresh, xg, wg, wu, wd)
    return _moe_combine(dst, n_valid, xs, modtab, g_post, info, yg)


def _block_diag2(w):
    z = jnp.zeros_like(w[0])
    return jnp.concatenate([jnp.concatenate([w[0], z], axis=1), jnp.concatenate([z, w[1]], axis=1)], axis=0)


def _row(v):
    return v.reshape(1, -1).astype(F32)


def _head_ones(width, head):
    i = jnp.arange(width) // head
    return (i[:, None] == i[None, :]).astype(BF16)


def kernel(x, c, ctx, c_ctx, ada_w, ada_b, norm_mix_pre, norm_mix_post, norm_ffn_pre, norm_ffn_post, w_in, shift_mu, rw_w_up, rw_w0, rw_a_up, rw_a0, rw_k_k, rw_k_a, rw_r_k, rw_g_up, rw_gn_w, rw_gn_b, rw_v_down, rw_v_up, rw_v0, gla_conv, gla_a_up, gla_a_b, gla_gn_w, w_out, ffn_w_gate, ffn_w_up, ffn_w_down, moe_router, moe_w_gate, moe_w_up, moe_w_down):
    B, S, _ = x.shape
    n_ctx = ctx.shape[1]
    depth = w_in.shape[0]
    assert n_ctx == TB and S % MOE_TB == 0 and (n_ctx + S) % TW == 0 and depth == 2

    xc = None
    pad_rows = -(B + 1) % SUBLANES
    cvec = jnp.concatenate([c, c_ctx[None, :], jnp.zeros((pad_rows, D), F32)], axis=0)
    bd64 = _head_ones(RW_WIDTH, RW_HEAD)
    ada_b3 = ada_b.reshape(depth, 1, 6 * D)
    v_first = None
    out = None
    for i in range(depth):
        last = i == depth - 1
        mods = _adaln(cvec, ada_w, ada_b3, i)
        mod_x = mods[:B].reshape(B, 6, D)
        mod_c = jnp.broadcast_to(mods[B].reshape(1, 6, D), (B, 6, D))
        modtab = jnp.stack([mod_c, mod_x], axis=1)

        p_rw, p_gl = _inproj(ctx, x, xc, _row(norm_mix_pre[i]), modtab, w_in, i)

        prm = dict(
            mu=_row(shift_mu[i]),
            w_up=_block_diag2(rw_w_up[i]).astype(BF16), w0=_row(rw_w0[i]),
            a_up=_block_diag2(rw_a_up[i]).astype(BF16), a0=_row(rw_a0[i]),
            k_k=_row(rw_k_k[i]), k_a=_row(rw_k_a[i]), r_k=_row(rw_r_k[i]),
            g_up=rw_g_up[i].astype(BF16), bd64=bd64,
            gn_w=_row(rw_gn_w[i]), gn_b=_row(rw_gn_b[i]), gla_gn_w=_row(gla_gn_w[i]),
            w_out=w_out[i].astype(BF16), norm_post=_row(norm_mix_post[i]),
        )
        gate_pad = jnp.zeros((LANES - 2 * GLA_GATE_RANK, 2 * GLA_KW), F32)
        gla_prm = dict(conv=gla_conv[i].astype(F32), a_b=_row(gla_a_b[i]),
                       a_up=jnp.concatenate([_block_diag2(gla_a_up[i]), gate_pad], axis=0).astype(BF16))
        if i > 0:
            pad = LANES - RW_V_RANK
            prm["v_down"] = jnp.concatenate([rw_v_down[i - 1], jnp.zeros((RW_WIDTH, pad), F32)], axis=1).astype(BF16)
            prm["v_up"] = jnp.concatenate([rw_v_up[i - 1], jnp.zeros((pad, RW_WIDTH), F32)], axis=0).astype(BF16)
            prm["v0"] = _row(rw_v0[i - 1])

        (r, kk, vm, g, bonus, lw, bb, ke), (q, k, gv, og, lg) = _prep(p_rw, p_gl, prm, gla_prm,
                                                                     v_first if i > 0 else None)
        if i == 0:
            v_first = vm
        y, o = _scans(r, kk, vm, lw, bb, ke, q, k, gv, lg)
        xc = _readout(y, bonus, g, o, og, ctx, x, xc, modtab, prm, latents_only=last)

        jf = i // 2
        if i % 2 == 0:
            xc = _ffn(xc, modtab, _row(norm_ffn_pre[i]), _row(norm_ffn_post[i]),
                      ffn_w_gate[jf], ffn_w_up[jf], ffn_w_down[jf])
        else:
            router = jnp.concatenate([moe_router[jf], jnp.zeros((D, LANES - N_EXPERTS), F32)], axis=1)
            r_hi = router.astype(BF16)
            router = jnp.stack([r_hi, (router - r_hi.astype(F32)).astype(BF16)])
            out = _moe(xc, modtab, _row(norm_ffn_pre[i]), _row(norm_ffn_post[i]), router,
                       moe_w_gate[jf], moe_w_up[jf], moe_w_down[jf])
    return out
```

```python
import functools
import math

import jax
import jax.numpy as jnp
from jax import lax
from jax.experimental import pallas as pl
from jax.experimental.pallas import tpu as pltpu

F32, BF16 = jnp.float32, jnp.bfloat16
ACT = BF16

D = 1024
GRID_W = 64
RW_WIDTH = 512
RW_HEAD = 64
RW_RANK = 64
RW_G_RANK = 128
RW_V_RANK = 32
RW_GN_EPS = 64e-5
GLA_WIDTH = 512
GLA_HEADS = 4
GLA_DV = 128
GLA_DK = 64
GLA_KW = 256
GLA_GATE_RANK = 16
GLA_GATE_NORM = 16.0
GLA_NORM_EPS = 1e-5
D_FF = 2816
N_EXPERTS = 8
NORM_EPS = 1e-6
RW_COLS = 3 * RW_WIDTH + 4 * RW_RANK + RW_G_RANK
GLA_QKV = 2 * GLA_KW + GLA_WIDTH
GLA_COLS = GLA_QKV + GLA_WIDTH + 2 * GLA_GATE_RANK
GLA_PAD = 1664

LANES = 128
SUBLANES = 8
TB = 256
TW = 3 * TB
CH = 64
FCH = 256
HALO = 16
VMEM_LIMIT = 56 * 1024 * 1024


def _cparams(sem):
    return pltpu.CompilerParams(dimension_semantics=sem, vmem_limit_bytes=VMEM_LIMIT)


def _bdot(a, b):
    return jnp.dot(a.astype(BF16), b.astype(BF16), preferred_element_type=F32)


def _bdot_nt(a, b):
    return lax.dot_general(a.astype(BF16), b.astype(BF16), (((1,), (1,)), ((), ())),
                           preferred_element_type=F32)


def _bdot_tn(a, b):
    return lax.dot_general(a.astype(BF16), b.astype(BF16), (((0,), (0,)), ((), ())),
                           preferred_element_type=F32)


def _split_dot(a_exact, x):
    h1 = x.astype(BF16)
    r1 = x - h1.astype(F32)
    h2 = r1.astype(BF16)
    h3 = (r1 - h2.astype(F32)).astype(BF16)
    return (jnp.dot(a_exact, h1, preferred_element_type=F32)
            + jnp.dot(a_exact, h2, preferred_element_type=F32)
            + jnp.dot(a_exact, h3, preferred_element_type=F32))


def _seg_sum(x, bd, exact=True):
    hi = x.astype(BF16)
    out = jnp.dot(hi, bd, preferred_element_type=F32)
    if exact:
        lo = (x - hi.astype(F32)).astype(BF16)
        out = out + jnp.dot(lo, bd, preferred_element_type=F32)
    return out


def _sigmoid(x):
    return jax.nn.sigmoid(x)


def _rms(x, g):
    return x * lax.rsqrt(jnp.mean(x * x, axis=-1, keepdims=True) + NORM_EPS) * g


def _pair_stack(z, lo):
    return jnp.concatenate([jnp.where(lo, z, 0.0), jnp.where(lo, 0.0, z)], axis=0)


def _adaln_kernel(c_ref, w_ref, b_ref, o_ref):
    c = c_ref[...]
    s = c * _sigmoid(c)
    o_ref[...] = jnp.dot(s, w_ref[0], precision=lax.Precision.HIGHEST,
                         preferred_element_type=F32) + b_ref[0]


def _adaln(cvec, w, b, layer):
    rows = cvec.shape[0]
    n = w.shape[2]
    return pl.pallas_call(
        _adaln_kernel,
        out_shape=jax.ShapeDtypeStruct((rows, n), F32),
        grid=(n // D,),
        in_specs=[pl.BlockSpec((rows, D), lambda i: (0, 0)),
                  pl.BlockSpec((1, D, D), lambda i: (layer, 0, i)),
                  pl.BlockSpec((1, 1, D), lambda i: (layer, 0, i))],
        out_specs=pl.BlockSpec((rows, D), lambda i: (0, i)),
        compiler_params=_cparams(("arbitrary",)),
        name="adaln",
    )(cvec, w, b)


def _wide_mod(mod_ref, k):
    is_ctx = jnp.logical_and(pl.program_id(1) == 0, lax.broadcasted_iota(jnp.int32, (TW, 1), 0) < TB)
    return jnp.where(is_ctx, mod_ref[0, 0, k:k + 1], mod_ref[0, 1, k:k + 1])


def _wide_rows(refs):
    if len(refs) == 1:
        return refs[0][0]
    first = jnp.where(pl.program_id(1) == 0, refs[0][0], refs[1][0])
    return jnp.concatenate([first] + [r[0] for r in refs[2:]], axis=0)


def _inproj_kernel(n_src, *refs):
    g_ref, mod_ref, w_ref, prw_ref, pgl_ref = refs[n_src:]
    x = _wide_rows(refs[:n_src])
    h = _rms(x, g_ref[...]) * (1.0 + _wide_mod(mod_ref, 1)) + _wide_mod(mod_ref, 0)
    hb = h.astype(BF16)
    prw_ref[0] = jnp.dot(hb, w_ref[0, :, :RW_COLS].astype(BF16), preferred_element_type=F32).astype(ACT)
    pgl_ref[0, :, :GLA_COLS] = jnp.dot(hb, w_ref[0, :, RW_COLS:].astype(BF16),
                                       preferred_element_type=F32).astype(ACT)
    pgl_ref[0, :, GLA_COLS:] = jnp.zeros((TW, GLA_PAD - GLA_COLS), ACT)


def _wide_src(ctx, x, xc):
    if xc is not None:
        return [xc], [pl.BlockSpec((1, TW, D), lambda b, j: (b, j, 0))]
    per = TW // TB
    specs = [pl.BlockSpec((1, TB, D), lambda b, j: (b, 0, 0))]
    for k in range(per):
        specs.append(pl.BlockSpec((1, TB, D), lambda b, j, k=k: (b, jnp.maximum(j * per + k - 1, 0), 0)))
    return [ctx] + [x] * per, specs


def _wide_mod_spec():
    return pl.BlockSpec((1, 2, 6, D), lambda b, j: (b, 0, 0, 0))


def _const_spec(shape):
    nd = len(shape)
    return pl.BlockSpec(shape, lambda *_: (0,) * nd)


def _inproj(ctx, x, xc, g, modtab, w, layer):
    B = modtab.shape[0]
    N = xc.shape[1] if xc is not None else ctx.shape[1] + x.shape[1]
    srcs, src_specs = _wide_src(ctx, x, xc)
    wide = lambda c: pl.BlockSpec((1, TW, c), lambda b, j: (b, j, 0))
    return pl.pallas_call(
        functools.partial(_inproj_kernel, len(srcs)),
        out_shape=[jax.ShapeDtypeStruct((B, N, RW_COLS), ACT),
                   jax.ShapeDtypeStruct((B, N, GLA_PAD), ACT)],
        grid=(B, N // TW),
        in_specs=src_specs + [_const_spec((1, D)), _wide_mod_spec(),
                              pl.BlockSpec((1, D, RW_COLS + GLA_COLS), lambda b, j: (layer, 0, 0),
                                           pipeline_mode=pl.Buffered(1))],
        out_specs=[wide(RW_COLS), wide(GLA_PAD)],
        compiler_params=_cparams(("parallel", "parallel")),
        name="inproj",
    )(*srcs, g, modtab, w)


def _shift_table():
    t = jnp.arange(TB)
    prev = t[:, None] - 1 == t[None, :]
    nxt = t[:, None] + 1 == t[None, :]
    col = (t % GRID_W)[:, None]
    return jnp.stack([jnp.stack([prev, nxt]),
                      jnp.stack([prev & (col != 0), nxt & (col != GRID_W - 1)])]).astype(BF16)


def _rwprep_kernel(has_vres, n_tiles, j, p_ref, hu_ref, hd_ref, sh_ref, mu_ref, wup_ref, w0_ref, aup_ref,
                   a0_ref, kk_ref, ka_ref, rk_ref, gup_ref, bd_ref, *rest):
    if has_vres:
        vf_ref, vdn_ref, vup_ref, v0_ref = rest[:4]
        rest = rest[4:]
    r_out, kk_out, v_out, g_out, bon_out, lw_out, b_out, ke_out = rest
    pb = p_ref[0]
    p = pb.astype(F32)
    prev1 = jnp.dot(sh_ref[0, 0], pb, preferred_element_type=F32)
    next1 = jnp.dot(sh_ref[0, 1], pb, preferred_element_type=F32)
    has_upper = jnp.where(j == 1, 0.0, 1.0)
    has_lower = jnp.where(j == n_tiles - 1, 0.0, 1.0)
    up = jnp.concatenate([hu_ref[0].astype(F32) * has_upper, p[:TB - GRID_W]], axis=0)
    down = jnp.concatenate([p[GRID_W:], hd_ref[0].astype(F32) * has_lower], axis=0)
    cls = lax.broadcasted_iota(jnp.int32, p.shape, 1) & jnp.where(j == 0, 1, 3)
    shifted = jnp.where(cls == 0, prev1, jnp.where(cls == 1, next1, jnp.where(cls == 2, up, down)))
    u = p + mu_ref[...] * (shifted - p)

    r = u[:, 0:RW_WIDTH]
    k = u[:, RW_WIDTH:2 * RW_WIDTH]
    v = u[:, 2 * RW_WIDTH:3 * RW_WIDTH]
    o = 3 * RW_WIDTH
    wd = u[:, o:o + 2 * RW_RANK]
    ad = u[:, o + 2 * RW_RANK:o + 4 * RW_RANK]
    gd = u[:, o + 4 * RW_RANK:]

    w_logit = w0_ref[...] + _bdot(jnp.tanh(wd), wup_ref[...])
    lw = -math.exp(-0.5) * _sigmoid(w_logit)
    a = _sigmoid(a0_ref[...] + _bdot(ad, aup_ref[...]))
    bd = bd_ref[...]
    kk = k * kk_ref[...]
    kk = kk * lax.rsqrt(jnp.maximum(_seg_sum(kk * kk, bd), 1e-24))
    g = _bdot(_sigmoid(gd), gup_ref[...])
    if has_vres:
        gate = _sigmoid(v0_ref[...] + _bdot(_bdot(v, vdn_ref[...]), vup_ref[...]))
        vm = v + (vf_ref[0].astype(F32) - v) * gate
    else:
        vm = v
    ke_sum = jnp.zeros_like(k)
    for d in range(2):
        a_d = a[:, d * RW_WIDTH:(d + 1) * RW_WIDTH]
        ke_d = k * (1.0 + (a_d - 1.0) * ka_ref[...])
        lw_out[d, 0] = lw[:, d * RW_WIDTH:(d + 1) * RW_WIDTH]
        b_out[d, 0] = (kk * a_d).astype(ACT)
        ke_out[d, 0] = ke_d.astype(ACT)
        ke_sum = ke_sum + ke_d
    r_out[0] = r.astype(ACT)
    kk_out[0] = kk.astype(ACT)
    v_out[0] = vm.astype(ACT)
    g_out[0] = g.astype(ACT)
    bon_out[0] = (_seg_sum(r * ke_sum * rk_ref[...], bd, exact=False) * vm).astype(ACT)


PREP_GROUP = 3


def _group_tile(k):
    return lambda j: PREP_GROUP * j + k


def _rwprep_tile(p_rw, v_first, k):
    _, N, _ = p_rw.shape
    hb = TB // GRID_W
    nhb = N // GRID_W
    t = _group_tile(k)
    specs = [pl.BlockSpec((1, TB, RW_COLS), lambda b, j: (b, t(j), 0)),
             pl.BlockSpec((1, GRID_W, RW_COLS), lambda b, j: (b, jnp.maximum(t(j) * hb - 1, 0), 0)),
             pl.BlockSpec((1, GRID_W, RW_COLS), lambda b, j: (b, jnp.minimum(t(j) * hb + hb, nhb - 1), 0)),
             pl.BlockSpec((1, 2, TB, TB), lambda b, j: (jnp.minimum(t(j), 1), 0, 0, 0))]
    args = [p_rw, p_rw, p_rw, _shift_table()]
    if v_first is not None:
        specs.append(pl.BlockSpec((1, TB, RW_WIDTH), lambda b, j: (b, t(j), 0)))
        args.append(v_first)
    return args, specs


def _rwprep_params(prm, has_vres):
    W2 = 2 * RW_WIDTH
    specs = [_const_spec((1, RW_COLS)), _const_spec((2 * RW_RANK, W2)), _const_spec((1, W2)),
             _const_spec((2 * RW_RANK, W2)), _const_spec((1, W2)),
             _const_spec((1, RW_WIDTH)), _const_spec((1, RW_WIDTH)), _const_spec((1, RW_WIDTH)),
             _const_spec((RW_G_RANK, RW_WIDTH)), _const_spec((RW_WIDTH, RW_WIDTH))]
    args = [prm["mu"], prm["w_up"], prm["w0"], prm["a_up"], prm["a0"],
            prm["k_k"], prm["k_a"], prm["r_k"], prm["g_up"], prm["bd64"]]
    if has_vres:
        specs += [_const_spec((RW_WIDTH, LANES)), _const_spec((LANES, RW_WIDTH)), _const_spec((1, RW_WIDTH))]
        args += [prm["v_down"], prm["v_up"], prm["v0"]]
    return args, specs


def _group_out(c):
    return pl.BlockSpec((1, PREP_GROUP * TB, c), lambda b, j: (b, j, 0))


def _group_dir_out(c):
    return pl.BlockSpec((2, 1, PREP_GROUP * TB, c), lambda b, j: (0, b, j, 0))


def _rwprep_outs(B, N):
    tok = jax.ShapeDtypeStruct((B, N, RW_WIDTH), ACT)
    dtok = jax.ShapeDtypeStruct((2, B, N, RW_WIDTH), ACT)
    shapes = [tok] * 5 + [jax.ShapeDtypeStruct((2, B, N, RW_WIDTH), F32), dtok, dtok]
    return shapes, [_group_out(RW_WIDTH)] * 5 + [_group_dir_out(RW_WIDTH)] * 3


NCH = TB // CH


def _scan_kernel(rf_ref, kkf_ref, vf_ref, lwf_ref, bf_ref, kef_ref,
                 rb_ref, kkb_ref, vb_ref, lwb_ref, bb_ref, keb_ref,
                 gqf_ref, gkf_ref, gvf_ref, lgf_ref, gqb_ref, gkb_ref, gvb_ref, lgb_ref,
                 yf_ref, yb_ref, of_ref, ob_ref, st_ref, gst_ref):
    j = pl.program_id(1)

    @pl.when(j == 0)
    def _():
        st_ref[...] = jnp.zeros_like(st_ref)
        gst_ref[...] = jnp.zeros_like(gst_ref)

    ti = lax.broadcasted_iota(jnp.int32, (CH, LANES), 0)
    li = lax.broadcasted_iota(jnp.int32, (CH, LANES), 1)
    si = li & (CH - 1)
    lo = li < CH
    eye = jnp.where(ti == si, 1.0, 0.0)
    rblk = lax.broadcasted_iota(jnp.int32, (LANES, LANES), 0) // CH
    cblk = lax.broadcasted_iota(jnp.int32, (LANES, LANES), 1) // CH
    bdmask = rblk == cblk
    t64 = lax.broadcasted_iota(jnp.int32, (CH, CH), 0)
    s64 = lax.broadcasted_iota(jnp.int32, (CH, CH), 1)
    tri = [jnp.where(s64 <= t64, 1.0, 0.0).astype(BF16), jnp.where(s64 >= t64, 1.0, 0.0).astype(BF16)]
    m_incl = [si <= ti, si >= ti]
    m_strict = [si < ti, si > ti]
    n_pairs = RW_WIDTH // LANES
    in_refs = ((rf_ref, kkf_ref, vf_ref, lwf_ref, bf_ref, kef_ref),
               (rb_ref, kkb_ref, vb_ref, lwb_ref, bb_ref, keb_ref))
    y_refs = (yf_ref, yb_ref)

    def stack(z):
        return _pair_stack(z, lo).astype(BF16)

    st = {(d, p): st_ref[d, p] for d in range(2) for p in range(n_pairs)}

    def scan_step(step):
        cur = {}
        w_tot = {}
        for d in range(2):
            c = (NCH - 1 - step) if d == 1 else step
            rows = slice(c * CH, (c + 1) * CH)
            refs = in_refs[d]
            r, kk, v = (refs[i][0, rows, :].astype(F32) for i in range(3))
            lw = refs[3][0, 0, rows, :]
            b, ke = refs[4][0, 0, rows, :].astype(F32), refs[5][0, 0, rows, :].astype(F32)
            cum = _split_dot(tri[d], lw)
            tot = jnp.sum(lw, axis=0, keepdims=True)
            w_inv = jnp.exp(-cum)
            w_end = jnp.exp(tot - cum)
            w_tot[d] = jnp.exp(tot)
            rh = r * jnp.exp(cum)
            ah = -(kk * jnp.exp(cum - lw))
            bh, kh, bt, kt = b * w_inv, ke * w_inv, b * w_end, ke * w_end
            for p in range(n_pairs):
                sl = slice(LANES * p, LANES * (p + 1))
                cur[d, p] = dict(ah=ah[:, sl], rh=rh[:, sl], bh=bh[:, sl], kh=kh[:, sl], bt=bt[:, sl],
                                 kt=kt[:, sl], v=v[:, sl], rows=rows, sl=sl)
        yield
        for it in cur.values():
            it["lhs"] = jnp.concatenate([it["ah"], it["rh"]], axis=0).astype(BF16)
            rhs = jnp.concatenate([stack(it["bh"]), stack(it["kh"])], axis=0)
            it["a_all"] = _bdot_nt(it["lhs"], rhs)
        yield
        for (d, _), it in cur.items():
            a_all = it.pop("a_all")
            it["a_ab"] = jnp.where(m_strict[d], a_all[:CH, :LANES], 0.0)
            a_ak = jnp.where(m_strict[d], a_all[:CH, LANES:], 0.0)
            a_rb = jnp.where(m_incl[d], a_all[CH:, :LANES], 0.0)
            a_rk = jnp.where(m_incl[d], a_all[CH:, LANES:], 0.0)
            it["a_r"] = jnp.concatenate([a_rb, a_rk], axis=1).astype(BF16)
            it["v_bd"] = stack(it["v"])
            it["akv"] = _bdot(a_ak, it["v_bd"])
            it["t"] = eye + it["a_ab"]
            it["m"] = _bdot(it["a_ab"], stack(it["a_ab"]))
        yield
        for _ in range(int(math.log2(CH)) - 2):
            for it in cur.values():
                z = _bdot(it["m"], jnp.concatenate([stack(it["m"]), stack(it["t"])], axis=1))
                it["m"] = z[:, :LANES]
                it["t"] = it["t"] + z[:, LANES:]
            yield
        for it in cur.values():
            it["t"] = (it["t"] + _bdot(it["m"], stack(it["t"]))).astype(BF16)
        yield
        for it in cur.values():
            ta = jnp.dot(it["t"], jnp.concatenate([stack(it["ah"]), stack(it["akv"])], axis=1),
                         preferred_element_type=F32)
            it["lhs2"] = jnp.concatenate([ta[:, :LANES].astype(BF16), it["lhs"][CH:]], axis=0)
            it["cc"] = ta[:, LANES:]
            it["rhs_t"] = jnp.concatenate([it["bt"], it["kt"]], axis=0).astype(BF16)
        yield
        z1 = {k: _bdot_nt(it["lhs2"], st[k]) for k, it in cur.items()}
        yield
        u = {k: z1[k][:CH] + it["cc"] for k, it in cur.items()}
        upd = {k: _bdot_tn(jnp.concatenate([u[k], it["v"]], axis=0), it["rhs_t"]) for k, it in cur.items()}
        yield
        for (d, p), it in cur.items():
            y = z1[d, p][CH:] + jnp.dot(it["a_r"], jnp.concatenate([stack(u[d, p]), it["v_bd"]], axis=0),
                                        preferred_element_type=F32)
            y_refs[d][0, it["rows"], it["sl"]] = y.astype(ACT)
            st[d, p] = st[d, p] * w_tot[d][:, it["sl"]] + jnp.where(bdmask, upd[d, p], 0.0)
        yield

    gla_refs = ((gqf_ref, gkf_ref, gvf_ref, lgf_ref), (gqb_ref, gkb_ref, gvb_ref, lgb_ref))
    o_refs = (of_ref, ob_ref)
    g_pairs = GLA_KW // LANES
    gst = {(d, p): gst_ref[d, p] for d in range(2) for p in range(g_pairs)}
    lo2 = lax.broadcasted_iota(jnp.int32, (LANES, LANES), 1) < CH
    zeros_v = jnp.zeros((CH, GLA_DV), F32)

    def gla_step(step):
        cur = {}
        dec = {}
        for d in range(2):
            c = (NCH - 1 - step) if d == 1 else step
            rows = slice(c * CH, (c + 1) * CH)
            refs = gla_refs[d]
            q, k, v = (refs[i][0, rows, :].astype(F32) for i in range(3))
            lg = refs[3][0, 0, rows, :]
            cum = _split_dot(tri[d], lg)
            tot = jnp.sum(lg, axis=0, keepdims=True)
            dec[d] = jnp.exp(tot)
            qd = q * jnp.exp(cum)
            ki = k * jnp.exp(-cum)
            kend = k * jnp.exp(tot - cum)
            for p in range(g_pairs):
                sl = slice(LANES * p, LANES * (p + 1))
                cur[d, p] = dict(qd=qd[:, sl].astype(BF16), ki=ki[:, sl], kend=kend[:, sl], rows=rows, sl=sl,
                                 v0=v[:, 2 * LANES * p:2 * LANES * p + LANES],
                                 v1=v[:, 2 * LANES * p + LANES:2 * LANES * (p + 1)])
        yield
        for it in cur.values():
            it["att"] = _bdot_nt(it["qd"], _pair_stack(it["ki"], lo))
        yield
        for (d, _), it in cur.items():
            att = jnp.where(m_incl[d], it["att"], 0.0)
            v_bd = jnp.concatenate([jnp.concatenate([it["v0"], zeros_v], axis=1),
                                    jnp.concatenate([zeros_v, it["v1"]], axis=1)], axis=0)
            it["o"] = _bdot(att, v_bd)
            it["upd"] = _bdot_tn(jnp.concatenate([it["v0"], it["v1"]], axis=0), _pair_stack(it["kend"], lo))
        yield
        for (d, p), it in cur.items():
            s = gst[d, p]
            s_bd = jnp.concatenate([jnp.where(lo2, s, 0.0), jnp.where(lo2, 0.0, s)], axis=0)
            o = it["o"] + _bdot_nt(it["qd"], s_bd)
            o_refs[d][0, it["rows"], 2 * LANES * p:2 * LANES * (p + 1)] = o.astype(ACT)
            gst[d, p] = s * dec[d][:, it["sl"]] + it["upd"]
        yield

    n_stages = 12
    n_dep = 3
    gla_stages = (1, 4, 7, 10)
    pipeline = [(scan_step(s), gla_step(s)) for s in range(NCH)]
    for slot in range(n_stages + n_dep * (NCH - 1)):
        for s, (rw_gen, gla_gen) in enumerate(pipeline):
            stage = slot - n_dep * s
            if 0 <= stage < n_stages:
                next(rw_gen)
                if stage in gla_stages:
                    next(gla_gen)
    for (d, p), s in st.items():
        st_ref[d, p] = s
    for (d, p), s in gst.items():
        gst_ref[d, p] = s


def _bidir_specs(c, n_tiles):
    def back(j):
        return jnp.where(j == 0, 0, n_tiles - j)

    tok_f = pl.BlockSpec((1, TB, c), lambda b, j: (b, j, 0))
    tok_b = pl.BlockSpec((1, TB, c), lambda b, j: (b, back(j), 0))
    dir_f = pl.BlockSpec((1, 1, TB, c), lambda b, j: (0, b, j, 0))
    dir_b = pl.BlockSpec((1, 1, TB, c), lambda b, j: (1, b, back(j), 0))
    return tok_f, tok_b, dir_f, dir_b


def _scans(r, kk, v, lw, b, ke, q, k, gv, lg):
    B, N, _ = r.shape
    nt = N // TB
    tok_f, tok_b, dir_f, dir_b = _bidir_specs(RW_WIDTH, nt)
    kf, kb, dkf, dkb = _bidir_specs(GLA_KW, nt)
    out = jax.ShapeDtypeStruct((B, N, RW_WIDTH), ACT)
    assert GLA_WIDTH == RW_WIDTH
    yf, yb, of, ob = pl.pallas_call(
        _scan_kernel,
        out_shape=[out] * 4,
        grid=(B, nt),
        in_specs=[tok_f, tok_f, tok_f, dir_f, dir_f, dir_f, tok_b, tok_b, tok_b, dir_b, dir_b, dir_b,
                  kf, kf, tok_f, dkf, kb, kb, tok_b, dkb],
        out_specs=[tok_f, tok_b, tok_f, tok_b],
        scratch_shapes=[pltpu.VMEM((2, RW_WIDTH // LANES, LANES, LANES), F32),
                        pltpu.VMEM((2, GLA_KW // LANES, GLA_DV, LANES), F32)],
        compiler_params=_cparams(("parallel", "arbitrary")),
        name="scans",
    )(r, kk, v, lw, b, ke, r, kk, v, lw, b, ke, q, k, gv, lg, q, k, gv, lg)
    return (yf, yb), (of, ob)


CONV_K = TB + LANES


def _conv_shift_table():
    t = jnp.arange(TB)[:, None]
    s = jnp.arange(CONV_K)[None, :]
    prev = jnp.where(t == 0, s == TB + 2 * HALO - 1, s == t - 1)
    nxt = jnp.where(t == TB - 1, s == TB, s == t + 1)
    return jnp.stack([prev, nxt]).astype(BF16)


def _glaprep_kernel(n_tiles, j, p_ref, hp_ref, hn_ref, sh_ref, cw_ref, aup_ref, ab_ref, q_out, k_out, v_out,
                    og_out, lg_out):
    ub = p_ref[0][:, :GLA_QKV]
    u = ub.astype(F32)
    has_prev = jnp.where(j <= 1, 0.0, 1.0)
    has_next = jnp.where(jnp.logical_or(j == 0, j == n_tiles - 1), 0.0, 1.0)
    ext = jnp.concatenate([ub,
                           (hn_ref[0][:, :GLA_QKV].astype(F32) * has_next).astype(BF16),
                           (hp_ref[0][:, :GLA_QKV].astype(F32) * has_prev).astype(BF16),
                           jnp.zeros((CONV_K - TB - 2 * HALO, GLA_QKV), BF16)], axis=0)
    prev1 = jnp.dot(sh_ref[0], ext, preferred_element_type=F32)
    next1 = jnp.dot(sh_ref[1], ext, preferred_element_type=F32)
    cw = cw_ref[...]
    conv = cw[0:1] * prev1 + cw[1:2] * u + cw[2:3] * next1
    qkv = conv * _sigmoid(conv)
    q_out[0] = (qkv[:, :GLA_KW] * (GLA_DK ** -0.5)).astype(ACT)
    k_out[0] = qkv[:, GLA_KW:2 * GLA_KW].astype(ACT)
    v_out[0] = qkv[:, 2 * GLA_KW:].astype(ACT)
    og_out[0] = p_ref[0][:, GLA_QKV:GLA_QKV + GLA_WIDTH]
    z = _bdot(p_ref[0][:, GLA_QKV + GLA_WIDTH:], aup_ref[...]) + ab_ref[...]
    lg = (jnp.minimum(z, 0.0) - jnp.log1p(jnp.exp(-jnp.abs(z)))) * (1.0 / GLA_GATE_NORM)
    for d in range(2):
        lg_out[d, 0] = lg[:, d * GLA_KW:(d + 1) * GLA_KW]


def _glaprep_tile(p_gl, k):
    _, N, _ = p_gl.shape
    hb = TB // HALO
    nhb = N // HALO
    t = _group_tile(k)
    specs = [pl.BlockSpec((1, TB, GLA_PAD), lambda b, j: (b, t(j), 0)),
             pl.BlockSpec((1, HALO, GLA_PAD), lambda b, j: (b, jnp.maximum(t(j) * hb - 1, 0), 0)),
             pl.BlockSpec((1, HALO, GLA_PAD), lambda b, j: (b, jnp.minimum(t(j) * hb + hb, nhb - 1), 0))]
    return [p_gl, p_gl, p_gl], specs


def _glaprep_params(prm):
    specs = [_const_spec((2, TB, CONV_K)), _const_spec((3, GLA_QKV)), _const_spec((LANES, 2 * GLA_KW)),
             _const_spec((1, 2 * GLA_KW))]
    return [_conv_shift_table(), prm["conv"], prm["a_up"], prm["a_b"]], specs


def _glaprep_outs(B, N):
    shapes = [jax.ShapeDtypeStruct((B, N, GLA_KW), ACT), jax.ShapeDtypeStruct((B, N, GLA_KW), ACT),
              jax.ShapeDtypeStruct((B, N, GLA_WIDTH), ACT), jax.ShapeDtypeStruct((B, N, GLA_WIDTH), ACT),
              jax.ShapeDtypeStruct((2, B, N, GLA_KW), F32)]
    return shapes, [_group_out(GLA_KW), _group_out(GLA_KW), _group_out(GLA_WIDTH), _group_out(GLA_WIDTH),
                    _group_dir_out(GLA_KW)]


class _TileRows:
    def __init__(self, ref, rows):
        self.ref, self.rows = ref, rows

    def __setitem__(self, idx, value):
        idx = idx if isinstance(idx, tuple) else (idx,)
        self.ref[idx + (self.rows, slice(None))] = value


def _prep_kernel(has_vres, n_tiles, n_rw_tile, n_gla_tile, n_rw_par, n_gla_par, n_rw_out, *refs):
    rw_tiles = [refs[k * n_rw_tile:(k + 1) * n_rw_tile] for k in range(PREP_GROUP)]
    refs = refs[PREP_GROUP * n_rw_tile:]
    gla_tiles = [refs[k * n_gla_tile:(k + 1) * n_gla_tile] for k in range(PREP_GROUP)]
    refs = refs[PREP_GROUP * n_gla_tile:]
    rw_par, refs = refs[:n_rw_par], refs[n_rw_par:]
    gla_par, outs = refs[:n_gla_par], refs[n_gla_par:]
    n_base = n_rw_par - 3 if has_vres else n_rw_par
    for k in range(PREP_GROUP):
        t = _group_tile(k)(pl.program_id(1))
        views = [_TileRows(o, slice(k * TB, (k + 1) * TB)) for o in outs]
        tile = rw_tiles[k]
        rw_in = list(tile[:4]) + list(rw_par[:n_base])
        if has_vres:
            rw_in += [tile[4]] + list(rw_par[n_base:])
        _rwprep_kernel(has_vres, n_tiles, t, *rw_in, *views[:n_rw_out])
        _glaprep_kernel(n_tiles, t, *gla_tiles[k], *gla_par, *views[n_rw_out:])


def _prep(p_rw, p_gl, prm, gla_prm, v_first):
    B, N, _ = p_rw.shape
    nt = N // TB
    assert nt % PREP_GROUP == 0
    has_vres = v_first is not None
    args, specs = [], []
    for k in range(PREP_GROUP):
        a, s = _rwprep_tile(p_rw, v_first, k)
        args, specs, n_rw_tile = args + a, specs + s, len(a)
    for k in range(PREP_GROUP):
        a, s = _glaprep_tile(p_gl, k)
        args, specs, n_gla_tile = args + a, specs + s, len(a)
    rw_par, rw_par_specs = _rwprep_params(prm, has_vres)
    gla_par, gla_par_specs = _glaprep_params(gla_prm)
    rw_shapes, rw_out = _rwprep_outs(B, N)
    gla_shapes, gla_out = _glaprep_outs(B, N)
    outs = pl.pallas_call(
        functools.partial(_prep_kernel, has_vres, nt, n_rw_tile, n_gla_tile, len(rw_par), len(gla_par),
                          len(rw_shapes)),
        out_shape=rw_shapes + gla_shapes,
        grid=(B, nt // PREP_GROUP),
        in_specs=specs + rw_par_specs + gla_par_specs,
        out_specs=rw_out + gla_out,
        compiler_params=_cparams(("parallel", "parallel")),
        name="prep",
    )(*args, *rw_par, *gla_par)
    return outs[:len(rw_shapes)], outs[len(rw_shapes):]


def _readout_kernel(group, first, split_src, *refs):
    tok = [refs[i * group:(i + 1) * group] for i in range(7)]
    refs = refs[7 * group:]
    n_res = group + 1 if split_src else group
    res_refs, refs = refs[:n_res], refs[n_res:]
    mod_ref, gnw_ref, gnb_ref, ggn_ref, wout_ref, gpost_ref, bd_ref, xo_ref = refs
    bd = bd_ref[...]
    starts_with_ctx = jnp.logical_and(first == 0, pl.program_id(1) == 0)
    for k in range(group):
        yf_ref, yb_ref, bon_ref, g_ref, of_ref, ob_ref, og_ref = (t[k] for t in tok)
        gate = mod_ref[0, 1, 2:3]
        if split_src:
            x_res = res_refs[k + 1][0]
            if k == 0:
                x_res = jnp.where(starts_with_ctx, res_refs[0][0], x_res)
        else:
            x_res = res_refs[k][0]
        if k == 0 and first == 0:
            gate = jnp.where(starts_with_ctx, mod_ref[0, 0, 2:3], gate)
        y = yf_ref[0].astype(F32) + yb_ref[0].astype(F32)
        mu = _seg_sum(y, bd, exact=False) * (1.0 / RW_HEAD)
        yc = y - mu
        var = _seg_sum(yc * yc, bd, exact=False) * (1.0 / RW_HEAD)
        yn = yc * lax.rsqrt(var + RW_GN_EPS) * gnw_ref[...] + gnb_ref[...]
        rw = (yn + bon_ref[0].astype(F32)) * g_ref[0].astype(F32)
        o = of_ref[0].astype(F32) + ob_ref[0].astype(F32)
        og = og_ref[0].astype(F32)
        parts = [rw.astype(BF16)]
        for h in range(GLA_HEADS):
            sl = slice(GLA_DV * h, GLA_DV * (h + 1))
            oh = o[:, sl]
            on = oh * lax.rsqrt(jnp.mean(oh * oh, axis=-1, keepdims=True) + GLA_NORM_EPS)
            ogh = og[:, sl]
            parts.append((on * ggn_ref[:, sl] * (ogh * _sigmoid(ogh))).astype(BF16))
        cat = jnp.concatenate(parts, axis=1)
        mx = jnp.dot(cat, wout_ref[...], preferred_element_type=F32)
        xo_ref[0, k * TB:(k + 1) * TB, :] = x_res + gate * _rms(mx, gpost_ref[...])


def _readout(y, bonus, g, o, og, ctx, x, xc, modtab, prm, latents_only):
    B, N, _ = bonus.shape
    first = 1 if latents_only else 0
    n_tiles = N // TB - first
    group = max(k for k in (4, 3, 2, 1) if n_tiles % k == 0)

    def tiles(c, shift=0):
        return [pl.BlockSpec((1, TB, c), lambda b, j, k=k: (b, jnp.maximum(first + group * j + k + shift, 0), 0))
                for k in range(group)]

    if xc is None:
        res, res_specs = [ctx] + [x] * group, [pl.BlockSpec((1, TB, D), lambda b, j: (b, 0, 0))] + tiles(D, -1)
    else:
        res, res_specs = [xc] * group, tiles(D)
    tok_args, tok_specs = [], []
    for arr in (y[0], y[1], bonus, g, o[0], o[1], og):
        tok_args += [arr] * group
        tok_specs += tiles(arr.shape[-1])
    return pl.pallas_call(
        functools.partial(_readout_kernel, group, first, xc is None),
        out_shape=jax.ShapeDtypeStruct((B, n_tiles * TB, D), F32),
        grid=(B, n_tiles // group),
        in_specs=tok_specs + res_specs + [_wide_mod_spec(),
                  _const_spec((1, RW_WIDTH)), _const_spec((1, RW_WIDTH)), _const_spec((1, GLA_WIDTH)),
                  _const_spec((D, D)), _const_spec((1, D)), _const_spec((RW_WIDTH, RW_WIDTH))],
        out_specs=pl.BlockSpec((1, group * TB, D), lambda b, j: (b, j, 0)),
        compiler_params=_cparams(("parallel", "parallel")),
        name="readout",
    )(*tok_args, *res, modtab, prm["gn_w"], prm["gn_b"], prm["gla_gn_w"],
      prm["w_out"], prm["norm_post"], prm["bd64"])


def _swiglu_acc(hb, wg_ref, wu_ref, wd_ref):
    acc = jnp.zeros((hb.shape[0], D), F32)
    for c in range(D_FF // FCH):
        sl = slice(c * FCH, (c + 1) * FCH)
        gate = jnp.dot(hb, wg_ref[:, sl].astype(BF16), preferred_element_type=F32)
        up = jnp.dot(hb, wu_ref[:, sl].astype(BF16), preferred_element_type=F32)
        act = (gate * _sigmoid(gate) * up).astype(BF16)
        acc = acc + jnp.dot(act, wd_ref[sl, :].astype(BF16), preferred_element_type=F32)
    return acc


def _ffn_kernel(x_ref, mod_ref, gpre_ref, gpost_ref, wg_ref, wu_ref, wd_ref, xo_ref):
    x = x_ref[0]
    hb = (_rms(x, gpre_ref[...]) * (1.0 + _wide_mod(mod_ref, 4)) + _wide_mod(mod_ref, 3)).astype(BF16)
    fx = _swiglu_acc(hb, wg_ref, wu_ref, wd_ref)
    xo_ref[0] = x + _wide_mod(mod_ref, 5) * _rms(fx, gpost_ref[...])


def _single_buffered(shape):
    nd = len(shape)
    return pl.BlockSpec(shape, lambda *_: (0,) * nd, pipeline_mode=pl.Buffered(1))


def _ffn(xc, modtab, g_pre, g_post, wg, wu, wd):
    B, N, _ = xc.shape
    wide = pl.BlockSpec((1, TW, D), lambda b, j: (b, j, 0))
    return pl.pallas_call(
        _ffn_kernel,
        out_shape=jax.ShapeDtypeStruct((B, N, D), F32),
        grid=(B, N // TW),
        in_specs=[wide, _wide_mod_spec(), _const_spec((1, D)), _const_spec((1, D)),
                  _single_buffered((D, D_FF)), _single_buffered((D, D_FF)), _single_buffered((D_FF, D))],
        out_specs=wide,
        compiler_params=_cparams(("parallel", "parallel")),
        name="ffn",
    )(xc, modtab, g_pre, g_post, wg, wu, wd)


MOE_TB = 512
MOE_SEG = 16
MOE_TM = 512
MOE_R = 2 * MOE_TB + N_EXPERTS * MOE_SEG
MOE_NP = MOE_R // MOE_SEG


def _moe_route_kernel(x_ref, mod_ref, gpre_ref, router_ref, h_ref, info_ref, infot_ref, cnt_ref):
    mod = mod_ref[0, 0]
    h = _rms(x_ref[0], gpre_ref[...]) * (1.0 + mod[4:5]) + mod[3:4]
    hb = h.astype(BF16)
    h_ref[0] = hb
    lane = lax.broadcasted_iota(jnp.int32, (MOE_TB, LANES), 1)
    h_lo = (h - hb.astype(F32)).astype(BF16)
    logits = (jnp.dot(hb, router_ref[0], preferred_element_type=F32)
              + jnp.dot(hb, router_ref[1], preferred_element_type=F32)
              + jnp.dot(h_lo, router_ref[0], preferred_element_type=F32))
    logits = jnp.where(lane < N_EXPERTS, logits, -jnp.inf)
    v1 = jnp.max(logits, axis=-1, keepdims=True)
    i1 = jnp.min(jnp.where(logits == v1, lane, LANES), axis=-1, keepdims=True)
    rest = jnp.where(lane == i1, -jnp.inf, logits)
    v2 = jnp.max(rest, axis=-1, keepdims=True)
    i2 = jnp.min(jnp.where(rest == v2, lane, LANES), axis=-1, keepdims=True)
    ex = jnp.exp(v2 - v1)
    w1 = 1.0 / (1.0 + ex)
    w2 = ex * w1
    e1 = jnp.where(lane == i1, 1.0, 0.0)
    e2 = jnp.where(lane == i2, 1.0, 0.0)
    es = e1 + e2
    t = lax.broadcasted_iota(jnp.int32, (MOE_TB, MOE_TB), 0)
    s = lax.broadcasted_iota(jnp.int32, (MOE_TB, MOE_TB), 1)
    before = jnp.where(s < t, 1.0, 0.0).astype(BF16)
    rank = jnp.dot(before, es.astype(BF16), preferred_element_type=F32)
    cnt = jnp.sum(es, axis=0, keepdims=True)
    segs = jnp.floor((cnt + (MOE_SEG - 1)) * (1.0 / MOE_SEG))
    ea = lax.broadcasted_iota(jnp.int32, (LANES, LANES), 0)
    eb = lax.broadcasted_iota(jnp.int32, (LANES, LANES), 1)
    earlier = jnp.where(ea < eb, 1.0, 0.0).astype(BF16)
    start = jnp.dot(jnp.broadcast_to(segs, (SUBLANES, LANES)).astype(BF16), earlier,
                    preferred_element_type=F32)[0:1] * MOE_SEG
    pos = rank + start
    d1 = jnp.sum(e1 * pos, axis=-1, keepdims=True)
    d2 = jnp.sum(e2 * pos, axis=-1, keepdims=True)
    info = jnp.where(lane == 0, d1, jnp.where(lane == 1, d2, jnp.where(lane == 2, w1, jnp.where(lane == 3, w2, 0.0))))
    info_ref[0] = info
    infot_ref[0] = jnp.transpose(info)[0:SUBLANES]
    cnt_ref[0] = jnp.broadcast_to(cnt, (SUBLANES, LANES))


def _moe_route(xs, modtab, g_pre, router):
    B, S, _ = xs.shape
    per = S // MOE_TB
    nb = B * per
    blk = lambda c: pl.BlockSpec((1, MOE_TB, c), lambda i: (i // per, i % per, 0))
    flat = lambda r, c: pl.BlockSpec((1, r, c), lambda i: (i, 0, 0))
    return pl.pallas_call(
        _moe_route_kernel,
        out_shape=[jax.ShapeDtypeStruct((nb, MOE_TB, D), BF16), jax.ShapeDtypeStruct((nb, MOE_TB, LANES), F32),
                   jax.ShapeDtypeStruct((nb, SUBLANES, MOE_TB), F32),
                   jax.ShapeDtypeStruct((nb, SUBLANES, LANES), F32)],
        grid=(nb,),
        in_specs=[blk(D), pl.BlockSpec((1, 1, 6, D), lambda i: (i // per, 1, 0, 0)),
                  _const_spec((1, D)), _const_spec((2, D, LANES))],
        out_specs=[flat(MOE_TB, D), flat(MOE_TB, LANES), flat(SUBLANES, MOE_TB), flat(SUBLANES, LANES)],
        compiler_params=_cparams(("parallel",)),
        name="moe_route",
    )(xs, modtab, g_pre, router)


def _moe_plan(cnt, n_tiles):
    pc = (cnt + MOE_SEG - 1) // MOE_SEG * MOE_SEG
    inc = jnp.cumsum(pc, axis=1)
    loff = inc - pc
    reg = (jnp.sum(pc, axis=0) + MOE_TM - 1) // MOE_TM * MOE_TM
    gend = jnp.cumsum(reg)
    goff = (gend - reg)[None, :] + jnp.cumsum(pc, axis=0) - pc
    rows = jnp.arange(MOE_NP, dtype=jnp.int32) * MOE_SEG
    e_p = jnp.sum((rows[None, :, None] >= inc[:, None, :]).astype(jnp.int32), axis=-1)
    e_c = jnp.minimum(e_p, N_EXPERTS - 1)
    pick = (e_c[:, :, None] == jnp.arange(N_EXPERTS)[None, None, :]).astype(jnp.int32)
    dst = jnp.sum(pick * (goff - loff)[:, None, :], axis=-1) + rows[None, :]
    dst = jnp.where(e_p < N_EXPERTS, dst, 0).astype(jnp.int32)
    n_valid = (inc[:, -1] // MOE_SEG).astype(jnp.int32)
    trow = jnp.arange(n_tiles, dtype=jnp.int32) * MOE_TM
    te = jnp.sum((trow[:, None] >= gend[None, :]).astype(jnp.int32), axis=-1)
    valid = te < N_EXPERTS
    last = gend[-1] // MOE_TM - 1
    te = jnp.where(valid, te, te[last]).astype(jnp.int32)
    src = jnp.where(valid, jnp.arange(n_tiles, dtype=jnp.int32), last).astype(jnp.int32)
    fresh = valid & jnp.concatenate([jnp.ones((1,), bool), te[1:] != te[:-1]])
    return dst, n_valid, te, src, valid.astype(jnp.int32), fresh.astype(jnp.int32)


def _piece_copy(src_ref, src_row, dst_ref, dst_row, sem):
    return pltpu.make_async_copy(src_ref.at[pl.ds(src_row, MOE_SEG)], dst_ref.at[pl.ds(dst_row, MOE_SEG)], sem)


def _moe_gather_kernel(dst_ref, nv_ref, h_ref, infot_ref, xg_in_ref, xg_ref, buf_ref, sem_ref):
    del xg_in_ref
    i = pl.program_id(0)
    it = infot_ref[0]
    rr = lax.broadcasted_iota(jnp.int32, (MOE_R, MOE_TB), 0).astype(F32)
    onehot = jnp.where(rr == it[0:1], 1.0, jnp.where(rr == it[1:2], 1.0, 0.0)).astype(BF16)
    slot = i % 2
    buf_ref[slot] = jnp.dot(onehot, h_ref[0], preferred_element_type=F32).astype(BF16)

    def piece(blk, p):
        return _piece_copy(buf_ref.at[blk % 2], pl.multiple_of(p * MOE_SEG, MOE_SEG),
                           xg_ref, pl.multiple_of(dst_ref[blk, p], MOE_SEG), sem_ref.at[blk % 2, p])

    def start(p, c):
        piece(i, p).start()
        return c

    lax.fori_loop(0, nv_ref[i], start, 0)

    def wait_block(blk):
        def wait(p, c):
            piece(blk, p).wait()
            return c
        lax.fori_loop(0, nv_ref[blk], wait, 0)

    @pl.when(i > 0)
    def _():
        wait_block(i - 1)

    @pl.when(i == pl.num_programs(0) - 1)
    def _():
        wait_block(i)


def _moe_gather(dst, n_valid, h, info_t, n_rows):
    nb = h.shape[0]
    flat = lambda r, c: pl.BlockSpec((1, r, c), lambda i, *_: (i, 0, 0))
    return pl.pallas_call(
        _moe_gather_kernel,
        out_shape=jax.ShapeDtypeStruct((n_rows, D), BF16),
        grid_spec=pltpu.PrefetchScalarGridSpec(
            num_scalar_prefetch=2, grid=(nb,),
            in_specs=[flat(MOE_TB, D), flat(SUBLANES, MOE_TB), pl.BlockSpec(memory_space=pl.ANY)],
            out_specs=pl.BlockSpec(memory_space=pl.ANY),
            scratch_shapes=[pltpu.VMEM((2, MOE_R, D), BF16), pltpu.SemaphoreType.DMA((2, MOE_NP))]),
        input_output_aliases={4: 0},
        compiler_params=_cparams(("arbitrary",)),
        name="moe_gather",
    )(dst, n_valid, h, info_t, jnp.zeros((n_rows, D), BF16))


N_FCH = D_FF // FCH


def _moe_ffn_kernel(te_ref, src_ref, valid_ref, fresh_ref, x_ref, wg_hbm, wu_hbm, wd_hbm, y_ref,
                    wg_ref, wu_ref, wd_ref, sem_ref):
    i = pl.program_id(0)
    e = te_ref[i]
    fresh = fresh_ref[i] == 1

    def slice_copies(c):
        cols = pl.ds(c * FCH, FCH)
        return (pltpu.make_async_copy(wg_hbm.at[e, :, cols], wg_ref.at[:, cols], sem_ref.at[0, c]),
                pltpu.make_async_copy(wu_hbm.at[e, :, cols], wu_ref.at[:, cols], sem_ref.at[1, c]),
                pltpu.make_async_copy(wd_hbm.at[e, cols, :], wd_ref.at[cols, :], sem_ref.at[2, c]))

    @pl.when(fresh)
    def _():
        for c in range(N_FCH):
            for cp in slice_copies(c):
                cp.start()
        hb = x_ref[...]
        acc = jnp.zeros((MOE_TM, D), F32)
        for c in range(N_FCH):
            for cp in slice_copies(c):
                cp.wait()
            sl = slice(c * FCH, (c + 1) * FCH)
            gate = jnp.dot(hb, wg_ref[:, sl].astype(BF16), preferred_element_type=F32)
            up = jnp.dot(hb, wu_ref[:, sl].astype(BF16), preferred_element_type=F32)
            act = (gate * _sigmoid(gate) * up).astype(BF16)
            acc = acc + jnp.dot(act, wd_ref[sl, :].astype(BF16), preferred_element_type=F32)
        y_ref[...] = acc.astype(BF16)

    @pl.when(jnp.logical_and(valid_ref[i] == 1, jnp.logical_not(fresh)))
    def _():
        y_ref[...] = _swiglu_acc(x_ref[...], wg_ref, wu_ref, wd_ref).astype(BF16)

    @pl.when(valid_ref[i] == 0)
    def _():
        y_ref[...] = jnp.zeros_like(y_ref)


def _moe_ffn(te, src, valid, fresh, xg, wg, wu, wd):
    n_rows = xg.shape[0]
    hbm = pl.BlockSpec(memory_space=pl.ANY)
    return pl.pallas_call(
        _moe_ffn_kernel,
        out_shape=jax.ShapeDtypeStruct((n_rows, D), BF16),
        grid_spec=pltpu.PrefetchScalarGridSpec(
            num_scalar_prefetch=4, grid=(n_rows // MOE_TM,),
            in_specs=[pl.BlockSpec((MOE_TM, D), lambda i, te, src, *_: (src[i], 0)), hbm, hbm, hbm],
            out_specs=pl.BlockSpec((MOE_TM, D), lambda i, *_: (i, 0)),
            scratch_shapes=[pltpu.VMEM((D, D_FF), F32), pltpu.VMEM((D, D_FF), F32), pltpu.VMEM((D_FF, D), F32),
                            pltpu.SemaphoreType.DMA((3, N_FCH))]),
        compiler_params=_cparams(("arbitrary",)),
        name="moe_ffn",
    )(te, src, valid, fresh, xg, wg, wu, wd)


def _moe_combine_kernel(dst_ref, nv_ref, x_ref, mod_ref, gpost_ref, info_ref, yg_ref, xo_ref, buf_ref, sem_ref):
    i = pl.program_id(0)

    def piece(blk, p):
        return _piece_copy(yg_ref, pl.multiple_of(dst_ref[blk, p], MOE_SEG),
                           buf_ref.at[blk % 2], pl.multiple_of(p * MOE_SEG, MOE_SEG), sem_ref.at[blk % 2, p])

    def fetch(blk):
        def start(p, c):
            piece(blk, p).start()
            return c

        def clear(p, c):
            buf_ref[blk % 2, pl.ds(pl.multiple_of(p * MOE_SEG, MOE_SEG), MOE_SEG), :] = jnp.zeros((MOE_SEG, D), BF16)
            return c

        lax.fori_loop(0, nv_ref[blk], start, 0)
        lax.fori_loop(nv_ref[blk], MOE_NP, clear, 0)

    @pl.when(i == 0)
    def _():
        fetch(i)

    @pl.when(i + 1 < pl.num_programs(0))
    def _():
        fetch(i + 1)

    info = info_ref[0]
    rr = lax.broadcasted_iota(jnp.int32, (MOE_TB, MOE_R), 1).astype(F32)
    comb = jnp.where(rr == info[:, 0:1], info[:, 2:3], jnp.where(rr == info[:, 1:2], info[:, 3:4], 0.0)).astype(BF16)

    def wait(p, c):
        piece(i, p).wait()
        return c

    lax.fori_loop(0, nv_ref[i], wait, 0)
    fx = jnp.dot(comb, buf_ref[i % 2], preferred_element_type=F32)
    xo_ref[0] = x_ref[0] + mod_ref[0, 0][5:6] * _rms(fx, gpost_ref[...])


def _moe_combine(dst, n_valid, xs, modtab, g_post, info, yg):
    B, S, _ = xs.shape
    per = S // MOE_TB
    blk = pl.BlockSpec((1, MOE_TB, D), lambda i, *_: (i // per, i % per, 0))
    return pl.pallas_call(
        _moe_combine_kernel,
        out_shape=jax.ShapeDtypeStruct((B, S, D), F32),
        grid_spec=pltpu.PrefetchScalarGridSpec(
            num_scalar_prefetch=2, grid=(B * per,),
            in_specs=[blk, pl.BlockSpec((1, 1, 6, D), lambda i, *_: (i // per, 1, 0, 0)),
                      pl.BlockSpec((1, D), lambda i, *_: (0, 0)),
                      pl.BlockSpec((1, MOE_TB, LANES), lambda i, *_: (i, 0, 0)),
                      pl.BlockSpec(memory_space=pl.ANY)],
            out_specs=blk,
            scratch_shapes=[pltpu.VMEM((2, MOE_R, D), BF16), pltpu.SemaphoreType.DMA((2, MOE_NP))]),
        compiler_params=_cparams(("arbitrary",)),
        name="moe_combine",
    )(dst, n_valid, xs, modtab, g_post, info, yg)


def _moe(xs, modtab, g_pre, g_post, router, wg, wu, wd):
    B, S, _ = xs.shape
    nb = B * S // MOE_TB
    h, info, info_t, cnt = _moe_route(xs, modtab, g_pre, router)
    worst = 2 * B * S + nb * N_EXPERTS * (MOE_SEG - 1) + N_EXPERTS * (MOE_TM - 1)
    n_tiles = -(-worst // MOE_TM)
    dst, n_valid, te, src, valid, fresh = _moe_plan(cnt[:, 0, :N_EXPERTS].astype(jnp.int32), n_tiles)
    xg = _moe_gather(dst, n_valid, h, info_t, n_tiles * MOE_TM)
    yg = _moe_ffn(te, src, valid, fresh, xg, wg, wu, wd)
    return _moe_combine(dst, n_valid, xs, modtab, g_post, info, yg)


def _block_diag2(w):
    z = jnp.zeros_like(w[0])
    return jnp.concatenate([jnp.concatenate([w[0], z], axis=1), jnp.concatenate([z, w[1]], axis=1)], axis=0)


def _row(v):
    return v.reshape(1, -1).astype(F32)


def _head_ones(width, head):
    i = jnp.arange(width) // head
    return (i[:, None] == i[None, :]).astype(BF16)


def kernel(x, c, ctx, c_ctx, ada_w, ada_b, norm_mix_pre, norm_mix_post, norm_ffn_pre, norm_ffn_post, w_in, shift_mu, rw_w_up, rw_w0, rw_a_up, rw_a0, rw_k_k, rw_k_a, rw_r_k, rw_g_up, rw_gn_w, rw_gn_b, rw_v_down, rw_v_up, rw_v0, gla_conv, gla_a_up, gla_a_b, gla_gn_w, w_out, ffn_w_gate, ffn_w_up, ffn_w_down, moe_router, moe_w_gate, moe_w_up, moe_w_down):
    B, S, _ = x.shape
    n_ctx = ctx.shape[1]
    depth = w_in.shape[0]
    assert n_ctx == TB and S % MOE_TB == 0 and (n_ctx + S) % TW == 0 and depth == 2

    xc = None
    pad_rows = -(B + 1) % SUBLANES
    cvec = jnp.concatenate([c, c_ctx[None, :], jnp.zeros((pad_rows, D), F32)], axis=0)
    bd64 = _head_ones(RW_WIDTH, RW_HEAD)
    ada_b3 = ada_b.reshape(depth, 1, 6 * D)
    v_first = None
    out = None
    for i in range(depth):
        last = i == depth - 1
        mods = _adaln(cvec, ada_w, ada_b3, i)
        mod_x = mods[:B].reshape(B, 6, D)
        mod_c = jnp.broadcast_to(mods[B].reshape(1, 6, D), (B, 6, D))
        modtab = jnp.stack([mod_c, mod_x], axis=1)

        p_rw, p_gl = _inproj(ctx, x, xc, _row(norm_mix_pre[i]), modtab, w_in, i)

        prm = dict(
            mu=_row(shift_mu[i]),
            w_up=_block_diag2(rw_w_up[i]).astype(BF16), w0=_row(rw_w0[i]),
            a_up=_block_diag2(rw_a_up[i]).astype(BF16), a0=_row(rw_a0[i]),
            k_k=_row(rw_k_k[i]), k_a=_row(rw_k_a[i]), r_k=_row(rw_r_k[i]),
            g_up=rw_g_up[i].astype(BF16), bd64=bd64,
            gn_w=_row(rw_gn_w[i]), gn_b=_row(rw_gn_b[i]), gla_gn_w=_row(gla_gn_w[i]),
            w_out=w_out[i].astype(BF16), norm_post=_row(norm_mix_post[i]),
        )
        gate_pad = jnp.zeros((LANES - 2 * GLA_GATE_RANK, 2 * GLA_KW), F32)
        gla_prm = dict(conv=gla_conv[i].astype(F32), a_b=_row(gla_a_b[i]),
                       a_up=jnp.concatenate([_block_diag2(gla_a_up[i]), gate_pad], axis=0).astype(BF16))
        if i > 0:
            pad = LANES - RW_V_RANK
            prm["v_down"] = jnp.concatenate([rw_v_down[i - 1], jnp.zeros((RW_WIDTH, pad), F32)], axis=1).astype(BF16)
            prm["v_up"] = jnp.concatenate([rw_v_up[i - 1], jnp.zeros((pad, RW_WIDTH), F32)], axis=0).astype(BF16)
            prm["v0"] = _row(rw_v0[i - 1])

        (r, kk, vm, g, bonus, lw, bb, ke), (q, k, gv, og, lg) = _prep(p_rw, p_gl, prm, gla_prm,
                                                                     v_first if i > 0 else None)
        if i == 0:
            v_first = vm
        y, o = _scans(r, kk, vm, lw, bb, ke, q, k, gv, lg)
        xc = _readout(y, bonus, g, o, og, ctx, x, xc, modtab, prm, latents_only=last)

        jf = i // 2
        if i % 2 == 0:
            xc = _ffn(xc, modtab, _row(norm_ffn_pre[i]), _row(norm_ffn_post[i]),
                      ffn_w_gate[jf], ffn_w_up[jf], ffn_w_down[jf])
        else:
            router = jnp.concatenate([moe_router[jf], jnp.zeros((D, LANES - N_EXPERTS), F32)], axis=1)
            r_hi = router.astype(BF16)
            router = jnp.stack([r_hi, (router - r_hi.astype(F32)).astype(BF16)])
            out = _moe(xc, modtab, _row(norm_ffn_pre[i]), _row(norm_ffn_post[i]), router,
                       moe_w_gate[jf], moe_w_up[jf], moe_w_down[jf])
    return out
```

```python
import functools
import math

import jax
import jax.numpy as jnp
from jax import lax
from jax.experimental import pallas as pl
from jax.experimental.pallas import tpu as pltpu

F32, BF16 = jnp.float32, jnp.bfloat16
ACT = BF16

D = 1024
GRID_W = 64
RW_WIDTH = 512
RW_HEAD = 64
RW_RANK = 64
RW_G_RANK = 128
RW_V_RANK = 32
RW_GN_EPS = 64e-5
GLA_WIDTH = 512
GLA_HEADS = 4
GLA_DV = 128
GLA_DK = 64
GLA_KW = 256
GLA_GATE_RANK = 16
GLA_GATE_NORM = 16.0
GLA_NORM_EPS = 1e-5
D_FF = 2816
N_EXPERTS = 8
NORM_EPS = 1e-6
RW_COLS = 3 * RW_WIDTH + 4 * RW_RANK + RW_G_RANK
GLA_QKV = 2 * GLA_KW + GLA_WIDTH
GLA_COLS = GLA_QKV + GLA_WIDTH + 2 * GLA_GATE_RANK

LANES = 128
SUBLANES = 8
GLA_PAD = -(-GLA_COLS // LANES) * LANES
TB = 256
TW = 3 * TB
CH = 64
FCH = 256
HALO = 16
VMEM_LIMIT = 56 * 1024 * 1024


def _cparams(sem):
    return pltpu.CompilerParams(dimension_semantics=sem, vmem_limit_bytes=VMEM_LIMIT)


def _bdot(a, b):
    return jnp.dot(a.astype(BF16), b.astype(BF16), preferred_element_type=F32)


def _bdot_nt(a, b):
    return lax.dot_general(a.astype(BF16), b.astype(BF16), (((1,), (1,)), ((), ())),
                           preferred_element_type=F32)


def _bdot_tn(a, b):
    return lax.dot_general(a.astype(BF16), b.astype(BF16), (((0,), (0,)), ((), ())),
                           preferred_element_type=F32)


def _split_dot(a_exact, x):
    h1 = x.astype(BF16)
    r1 = x - h1.astype(F32)
    h2 = r1.astype(BF16)
    h3 = (r1 - h2.astype(F32)).astype(BF16)
    return (jnp.dot(a_exact, h1, preferred_element_type=F32)
            + jnp.dot(a_exact, h2, preferred_element_type=F32)
            + jnp.dot(a_exact, h3, preferred_element_type=F32))


def _seg_sum(x, bd, exact=True):
    hi = x.astype(BF16)
    out = jnp.dot(hi, bd, preferred_element_type=F32)
    if exact:
        lo = (x - hi.astype(F32)).astype(BF16)
        out = out + jnp.dot(lo, bd, preferred_element_type=F32)
    return out


def _sigmoid(x):
    return jax.nn.sigmoid(x)


def _rms(x, g):
    return x * lax.rsqrt(jnp.mean(x * x, axis=-1, keepdims=True) + NORM_EPS) * g


def _pair_stack(z, lo):
    return jnp.concatenate([jnp.where(lo, z, 0.0), jnp.where(lo, 0.0, z)], axis=0)


def _adaln_kernel(c_ref, w_ref, b_ref, o_ref):
    c = c_ref[...]
    s = c * _sigmoid(c)
    w = w_ref[0]
    s_hi, w_hi = s.astype(BF16), w.astype(BF16)
    s_lo, w_lo = (s - s_hi.astype(F32)).astype(BF16), (w - w_hi.astype(F32)).astype(BF16)
    o_ref[...] = (jnp.dot(s_hi, w_hi, preferred_element_type=F32) + jnp.dot(s_hi, w_lo, preferred_element_type=F32)
                  + jnp.dot(s_lo, w_hi, preferred_element_type=F32)) + b_ref[0]


def _adaln(cvec, w, b, layer):
    rows = cvec.shape[0]
    n = w.shape[2]
    return pl.pallas_call(
        _adaln_kernel,
        out_shape=jax.ShapeDtypeStruct((rows, n), F32),
        grid=(n // D,),
        in_specs=[pl.BlockSpec((rows, D), lambda i: (0, 0)),
                  pl.BlockSpec((1, D, D), lambda i: (layer, 0, i)),
                  pl.BlockSpec((1, 1, D), lambda i: (layer, 0, i))],
        out_specs=pl.BlockSpec((rows, D), lambda i: (0, i)),
        compiler_params=_cparams(("arbitrary",)),
        name="adaln",
    )(cvec, w, b)


def _wide_mod(mod_ref, k):
    is_ctx = jnp.logical_and(pl.program_id(1) == 0, lax.broadcasted_iota(jnp.int32, (TW, 1), 0) < TB)
    return jnp.where(is_ctx, mod_ref[0, 0, k:k + 1], mod_ref[0, 1, k:k + 1])


def _wide_rows(refs):
    if len(refs) == 1:
        return refs[0][0]
    first = jnp.where(pl.program_id(1) == 0, refs[0][0], refs[1][0])
    return jnp.concatenate([first] + [r[0] for r in refs[2:]], axis=0)


def _inproj_kernel(n_src, *refs):
    g_ref, mod_ref, w_ref, prw_ref, pgl_ref = refs[n_src:]
    x = _wide_rows(refs[:n_src])
    h = _rms(x, g_ref[...]) * (1.0 + _wide_mod(mod_ref, 1)) + _wide_mod(mod_ref, 0)
    hb = h.astype(BF16)
    prw_ref[0] = jnp.dot(hb, w_ref[0, :, :RW_COLS].astype(BF16), preferred_element_type=F32).astype(ACT)
    pgl_ref[0, :, :GLA_COLS] = jnp.dot(hb, w_ref[0, :, RW_COLS:].astype(BF16),
                                       preferred_element_type=F32).astype(ACT)
    pgl_ref[0, :, GLA_COLS:] = jnp.zeros((TW, GLA_PAD - GLA_COLS), ACT)


def _wide_src(ctx, x, xc):
    if xc is not None:
        return [xc], [pl.BlockSpec((1, TW, D), lambda b, j: (b, j, 0))]
    per = TW // TB
    specs = [pl.BlockSpec((1, TB, D), lambda b, j: (b, 0, 0))]
    for k in range(per):
        specs.append(pl.BlockSpec((1, TB, D), lambda b, j, k=k: (b, jnp.maximum(j * per + k - 1, 0), 0)))
    return [ctx] + [x] * per, specs


def _wide_mod_spec():
    return pl.BlockSpec((1, 2, 6, D), lambda b, j: (b, 0, 0, 0))


def _const_spec(shape):
    nd = len(shape)
    return pl.BlockSpec(shape, lambda *_: (0,) * nd)


def _inproj(ctx, x, xc, g, modtab, w, layer):
    B = modtab.shape[0]
    N = xc.shape[1] if xc is not None else ctx.shape[1] + x.shape[1]
    srcs, src_specs = _wide_src(ctx, x, xc)
    wide = lambda c: pl.BlockSpec((1, TW, c), lambda b, j: (b, j, 0))
    return pl.pallas_call(
        functools.partial(_inproj_kernel, len(srcs)),
        out_shape=[jax.ShapeDtypeStruct((B, N, RW_COLS), ACT),
                   jax.ShapeDtypeStruct((B, N, GLA_PAD), ACT)],
        grid=(B, N // TW),
        in_specs=src_specs + [_const_spec((1, D)), _wide_mod_spec(),
                              pl.BlockSpec((1, D, RW_COLS + GLA_COLS), lambda b, j: (layer, 0, 0),
                                           pipeline_mode=pl.Buffered(1))],
        out_specs=[wide(RW_COLS), wide(GLA_PAD)],
        compiler_params=_cparams(("parallel", "parallel")),
        name="inproj",
    )(*srcs, g, modtab, w)


def _shift_table():
    t = jnp.arange(TB)
    prev = t[:, None] - 1 == t[None, :]
    nxt = t[:, None] + 1 == t[None, :]
    col = (t % GRID_W)[:, None]
    return jnp.stack([jnp.stack([prev, nxt]),
                      jnp.stack([prev & (col != 0), nxt & (col != GRID_W - 1)])]).astype(BF16)


def _rwprep_kernel(has_vres, n_tiles, j, p_ref, hu_ref, hd_ref, sh_ref, mu_ref, wup_ref, w0_ref, aup_ref,
                   a0_ref, kk_ref, ka_ref, rk_ref, gup_ref, bd_ref, *rest):
    if has_vres:
        vf_ref, vdn_ref, vup_ref, v0_ref = rest[:4]
        rest = rest[4:]
    r_out, kk_out, v_out, g_out, bon_out, lw_out, b_out, ke_out = rest
    pb = p_ref[0]
    p = pb.astype(F32)
    prev1 = jnp.dot(sh_ref[0, 0], pb, preferred_element_type=F32)
    next1 = jnp.dot(sh_ref[0, 1], pb, preferred_element_type=F32)
    has_upper = jnp.where(j == 1, 0.0, 1.0)
    has_lower = jnp.where(j == n_tiles - 1, 0.0, 1.0)
    up = jnp.concatenate([hu_ref[0].astype(F32) * has_upper, p[:TB - GRID_W]], axis=0)
    down = jnp.concatenate([p[GRID_W:], hd_ref[0].astype(F32) * has_lower], axis=0)
    cls = lax.broadcasted_iota(jnp.int32, p.shape, 1) & jnp.where(j == 0, 1, 3)
    shifted = jnp.where(cls == 0, prev1, jnp.where(cls == 1, next1, jnp.where(cls == 2, up, down)))
    u = p + mu_ref[...] * (shifted - p)

    r = u[:, 0:RW_WIDTH]
    k = u[:, RW_WIDTH:2 * RW_WIDTH]
    v = u[:, 2 * RW_WIDTH:3 * RW_WIDTH]
    o = 3 * RW_WIDTH
    wd = u[:, o:o + 2 * RW_RANK]
    ad = u[:, o + 2 * RW_RANK:o + 4 * RW_RANK]
    gd = u[:, o + 4 * RW_RANK:]

    w_logit = w0_ref[...] + _bdot(jnp.tanh(wd), wup_ref[...])
    lw = -math.exp(-0.5) * _sigmoid(w_logit)
    a = _sigmoid(a0_ref[...] + _bdot(ad, aup_ref[...]))
    bd = bd_ref[...]
    kk = k * kk_ref[...]
    kk = kk * lax.rsqrt(jnp.maximum(_seg_sum(kk * kk, bd), 1e-24))
    g = _bdot(_sigmoid(gd), gup_ref[...])
    if has_vres:
        gate = _sigmoid(v0_ref[...] + _bdot(_bdot(v, vdn_ref[...]), vup_ref[...]))
        vm = v + (vf_ref[0].astype(F32) - v) * gate
    else:
        vm = v
    ke_sum = jnp.zeros_like(k)
    for d in range(2):
        a_d = a[:, d * RW_WIDTH:(d + 1) * RW_WIDTH]
        ke_d = k * (1.0 + (a_d - 1.0) * ka_ref[...])
        lw_out[d, 0] = lw[:, d * RW_WIDTH:(d + 1) * RW_WIDTH]
        b_out[d, 0] = (kk * a_d).astype(ACT)
        ke_out[d, 0] = ke_d.astype(ACT)
        ke_sum = ke_sum + ke_d
    r_out[0] = r.astype(ACT)
    kk_out[0] = kk.astype(ACT)
    v_out[0] = vm.astype(ACT)
    g_out[0] = g.astype(ACT)
    bon_out[0] = (_seg_sum(r * ke_sum * rk_ref[...], bd, exact=False) * vm).astype(ACT)


PREP_GROUP = 3


def _group_tile(k):
    return lambda j: PREP_GROUP * j + k


def _rwprep_tile(p_rw, v_first, k):
    _, N, _ = p_rw.shape
    hb = TB // GRID_W
    nhb = N // GRID_W
    t = _group_tile(k)
    specs = [pl.BlockSpec((1, TB, RW_COLS), lambda b, j: (b, t(j), 0)),
             pl.BlockSpec((1, GRID_W, RW_COLS), lambda b, j: (b, jnp.maximum(t(j) * hb - 1, 0), 0)),
             pl.BlockSpec((1, GRID_W, RW_COLS), lambda b, j: (b, jnp.minimum(t(j) * hb + hb, nhb - 1), 0)),
             pl.BlockSpec((1, 2, TB, TB), lambda b, j: (jnp.minimum(t(j), 1), 0, 0, 0))]
    args = [p_rw, p_rw, p_rw, _shift_table()]
    if v_first is not None:
        specs.append(pl.BlockSpec((1, TB, RW_WIDTH), lambda b, j: (b, t(j), 0)))
        args.append(v_first)
    return args, specs


def _rwprep_params(prm, has_vres):
    W2 = 2 * RW_WIDTH
    specs = [_const_spec((1, RW_COLS)), _const_spec((2 * RW_RANK, W2)), _const_spec((1, W2)),
             _const_spec((2 * RW_RANK, W2)), _const_spec((1, W2)),
             _const_spec((1, RW_WIDTH)), _const_spec((1, RW_WIDTH)), _const_spec((1, RW_WIDTH)),
             _const_spec((RW_G_RANK, RW_WIDTH)), _const_spec((RW_WIDTH, RW_WIDTH))]
    args = [prm["mu"], prm["w_up"], prm["w0"], prm["a_up"], prm["a0"],
            prm["k_k"], prm["k_a"], prm["r_k"], prm["g_up"], prm["bd64"]]
    if has_vres:
        specs += [_const_spec((RW_WIDTH, LANES)), _const_spec((LANES, RW_WIDTH)), _const_spec((1, RW_WIDTH))]
        args += [prm["v_down"], prm["v_up"], prm["v0"]]
    return args, specs


def _group_out(c):
    return pl.BlockSpec((1, PREP_GROUP * TB, c), lambda b, j: (b, j, 0))


def _group_dir_out(c):
    return pl.BlockSpec((2, 1, PREP_GROUP * TB, c), lambda b, j: (0, b, j, 0))


def _rwprep_outs(B, N):
    tok = jax.ShapeDtypeStruct((B, N, RW_WIDTH), ACT)
    dtok = jax.ShapeDtypeStruct((2, B, N, RW_WIDTH), ACT)
    shapes = [tok] * 5 + [jax.ShapeDtypeStruct((2, B, N, RW_WIDTH), F32), dtok, dtok]
    return shapes, [_group_out(RW_WIDTH)] * 5 + [_group_dir_out(RW_WIDTH)] * 3


NCH = TB // CH


def _scan_kernel(rf_ref, kkf_ref, vf_ref, lwf_ref, bf_ref, kef_ref,
                 rb_ref, kkb_ref, vb_ref, lwb_ref, bb_ref, keb_ref,
                 gqf_ref, gkf_ref, gvf_ref, lgf_ref, gqb_ref, gkb_ref, gvb_ref, lgb_ref,
                 yf_ref, yb_ref, of_ref, ob_ref, st_ref, gst_ref):
    j = pl.program_id(1)

    @pl.when(j == 0)
    def _():
        st_ref[...] = jnp.zeros_like(st_ref)
        gst_ref[...] = jnp.zeros_like(gst_ref)

    ti = lax.broadcasted_iota(jnp.int32, (CH, LANES), 0)
    li = lax.broadcasted_iota(jnp.int32, (CH, LANES), 1)
    si = li & (CH - 1)
    lo = li < CH
    eye = jnp.where(ti == si, 1.0, 0.0)
    rblk = lax.broadcasted_iota(jnp.int32, (LANES, LANES), 0) // CH
    cblk = lax.broadcasted_iota(jnp.int32, (LANES, LANES), 1) // CH
    bdmask = rblk == cblk
    t64 = lax.broadcasted_iota(jnp.int32, (CH, CH), 0)
    s64 = lax.broadcasted_iota(jnp.int32, (CH, CH), 1)
    tri = [jnp.where(s64 <= t64, 1.0, 0.0).astype(BF16), jnp.where(s64 >= t64, 1.0, 0.0).astype(BF16)]
    m_incl = [si <= ti, si >= ti]
    m_strict = [si < ti, si > ti]
    n_pairs = RW_WIDTH // LANES
    in_refs = ((rf_ref, kkf_ref, vf_ref, lwf_ref, bf_ref, kef_ref),
               (rb_ref, kkb_ref, vb_ref, lwb_ref, bb_ref, keb_ref))
    y_refs = (yf_ref, yb_ref)

    def stack(z):
        return _pair_stack(z, lo).astype(BF16)

    st = {(d, p): st_ref[d, p] for d in range(2) for p in range(n_pairs)}

    def scan_step(step):
        cur = {}
        w_tot = {}
        for d in range(2):
            c = (NCH - 1 - step) if d == 1 else step
            rows = slice(c * CH, (c + 1) * CH)
            refs = in_refs[d]
            r, kk, v = (refs[i][0, rows, :].astype(F32) for i in range(3))
            lw = refs[3][0, 0, rows, :]
            b, ke = refs[4][0, 0, rows, :].astype(F32), refs[5][0, 0, rows, :].astype(F32)
            cum = _split_dot(tri[d], lw)
            tot = jnp.sum(lw, axis=0, keepdims=True)
            w_inv = jnp.exp(-cum)
            w_end = jnp.exp(tot - cum)
            w_tot[d] = jnp.exp(tot)
            rh = r * jnp.exp(cum)
            ah = -(kk * jnp.exp(cum - lw))
            bh, kh, bt, kt = b * w_inv, ke * w_inv, b * w_end, ke * w_end
            for p in range(n_pairs):
                sl = slice(LANES * p, LANES * (p + 1))
                cur[d, p] = dict(ah=ah[:, sl], rh=rh[:, sl], bh=bh[:, sl], kh=kh[:, sl], bt=bt[:, sl],
                                 kt=kt[:, sl], v=v[:, sl], rows=rows, sl=sl)
        yield
        for it in cur.values():
            it["lhs"] = jnp.concatenate([it["ah"], it["rh"]], axis=0).astype(BF16)
            rhs = jnp.concatenate([stack(it["bh"]), stack(it["kh"])], axis=0)
            it["a_all"] = _bdot_nt(it["lhs"], rhs)
        yield
        for (d, _), it in cur.items():
            a_all = it.pop("a_all")
            it["a_ab"] = jnp.where(m_strict[d], a_all[:CH, :LANES], 0.0)
            a_ak = jnp.where(m_strict[d], a_all[:CH, LANES:], 0.0)
            a_rb = jnp.where(m_incl[d], a_all[CH:, :LANES], 0.0)
            a_rk = jnp.where(m_incl[d], a_all[CH:, LANES:], 0.0)
            it["a_r"] = jnp.concatenate([a_rb, a_rk], axis=1).astype(BF16)
            it["v_bd"] = stack(it["v"])
            it["akv"] = _bdot(a_ak, it["v_bd"])
            it["t"] = eye + it["a_ab"]
            it["m"] = _bdot(it["a_ab"], stack(it["a_ab"]))
        yield
        for _ in range(int(math.log2(CH)) - 2):
            for it in cur.values():
                z = _bdot(it["m"], jnp.concatenate([stack(it["m"]), stack(it["t"])], axis=1))
                it["m"] = z[:, :LANES]
                it["t"] = it["t"] + z[:, LANES:]
            yield
        for it in cur.values():
            it["t"] = (it["t"] + _bdot(it["m"], stack(it["t"]))).astype(BF16)
        yield
        for it in cur.values():
            ta = jnp.dot(it["t"], jnp.concatenate([stack(it["ah"]), stack(it["akv"])], axis=1),
                         preferred_element_type=F32)
            it["lhs2"] = jnp.concatenate([ta[:, :LANES].astype(BF16), it["lhs"][CH:]], axis=0)
            it["cc"] = ta[:, LANES:]
            it["rhs_t"] = jnp.concatenate([it["bt"], it["kt"]], axis=0).astype(BF16)
        yield
        z1 = {k: _bdot_nt(it["lhs2"], st[k]) for k, it in cur.items()}
        yield
        u = {k: z1[k][:CH] + it["cc"] for k, it in cur.items()}
        upd = {k: _bdot_tn(jnp.concatenate([u[k], it["v"]], axis=0), it["rhs_t"]) for k, it in cur.items()}
        yield
        for (d, p), it in cur.items():
            y = z1[d, p][CH:] + jnp.dot(it["a_r"], jnp.concatenate([stack(u[d, p]), it["v_bd"]], axis=0),
                                        preferred_element_type=F32)
            y_refs[d][0, it["rows"], it["sl"]] = y.astype(ACT)
            st[d, p] = st[d, p] * w_tot[d][:, it["sl"]] + jnp.where(bdmask, upd[d, p], 0.0)
        yield

    gla_refs = ((gqf_ref, gkf_ref, gvf_ref, lgf_ref), (gqb_ref, gkb_ref, gvb_ref, lgb_ref))
    o_refs = (of_ref, ob_ref)
    g_pairs = GLA_KW // LANES
    gst = {(d, p): gst_ref[d, p] for d in range(2) for p in range(g_pairs)}
    lo2 = lax.broadcasted_iota(jnp.int32, (LANES, LANES), 1) < CH
    zeros_v = jnp.zeros((CH, GLA_DV), F32)

    def gla_step(step):
        cur = {}
        dec = {}
        for d in range(2):
            c = (NCH - 1 - step) if d == 1 else step
            rows = slice(c * CH, (c + 1) * CH)
            refs = gla_refs[d]
            q, k, v = (refs[i][0, rows, :].astype(F32) for i in range(3))
            lg = refs[3][0, 0, rows, :]
            cum = _split_dot(tri[d], lg)
            tot = jnp.sum(lg, axis=0, keepdims=True)
            dec[d] = jnp.exp(tot)
            qd = q * jnp.exp(cum)
            ki = k * jnp.exp(-cum)
            kend = k * jnp.exp(tot - cum)
            for p in range(g_pairs):
                sl = slice(LANES * p, LANES * (p + 1))
                cur[d, p] = dict(qd=qd[:, sl].astype(BF16), ki=ki[:, sl], kend=kend[:, sl], rows=rows, sl=sl,
                                 v0=v[:, 2 * LANES * p:2 * LANES * p + LANES],
                                 v1=v[:, 2 * LANES * p + LANES:2 * LANES * (p + 1)])
        yield
        for it in cur.values():
            it["att"] = _bdot_nt(it["qd"], _pair_stack(it["ki"], lo))
        yield
        for (d, _), it in cur.items():
            att = jnp.where(m_incl[d], it["att"], 0.0)
            v_bd = jnp.concatenate([jnp.concatenate([it["v0"], zeros_v], axis=1),
                                    jnp.concatenate([zeros_v, it["v1"]], axis=1)], axis=0)
            it["o"] = _bdot(att, v_bd)
            it["upd"] = _bdot_tn(jnp.concatenate([it["v0"], it["v1"]], axis=0), _pair_stack(it["kend"], lo))
        yield
        for (d, p), it in cur.items():
            s = gst[d, p]
            s_bd = jnp.concatenate([jnp.where(lo2, s, 0.0), jnp.where(lo2, 0.0, s)], axis=0)
            o = it["o"] + _bdot_nt(it["qd"], s_bd)
            o_refs[d][0, it["rows"], 2 * LANES * p:2 * LANES * (p + 1)] = o.astype(ACT)
            gst[d, p] = s * dec[d][:, it["sl"]] + it["upd"]
        yield

    n_stages = 12
    n_dep = 3
    gla_stages = (1, 4, 7, 10)
    pipeline = [(scan_step(s), gla_step(s)) for s in range(NCH)]
    for slot in range(n_stages + n_dep * (NCH - 1)):
        for s, (rw_gen, gla_gen) in enumerate(pipeline):
            stage = slot - n_dep * s
            if 0 <= stage < n_stages:
                next(rw_gen)
                if stage in gla_stages:
                    next(gla_gen)
    for (d, p), s in st.items():
        st_ref[d, p] = s
    for (d, p), s in gst.items():
        gst_ref[d, p] = s


def _bidir_specs(c, n_tiles):
    def back(j):
        return jnp.where(j == 0, 0, n_tiles - j)

    tok_f = pl.BlockSpec((1, TB, c), lambda b, j: (b, j, 0))
    tok_b = pl.BlockSpec((1, TB, c), lambda b, j: (b, back(j), 0))
    dir_f = pl.BlockSpec((1, 1, TB, c), lambda b, j: (0, b, j, 0))
    dir_b = pl.BlockSpec((1, 1, TB, c), lambda b, j: (1, b, back(j), 0))
    return tok_f, tok_b, dir_f, dir_b


def _scans(r, kk, v, lw, b, ke, q, k, gv, lg):
    B, N, _ = r.shape
    nt = N // TB
    tok_f, tok_b, dir_f, dir_b = _bidir_specs(RW_WIDTH, nt)
    kf, kb, dkf, dkb = _bidir_specs(GLA_KW, nt)
    out = jax.ShapeDtypeStruct((B, N, RW_WIDTH), ACT)
    assert GLA_WIDTH == RW_WIDTH
    yf, yb, of, ob = pl.pallas_call(
        _scan_kernel,
        out_shape=[out] * 4,
        grid=(B, nt),
        in_specs=[tok_f, tok_f, tok_f, dir_f, dir_f, dir_f, tok_b, tok_b, tok_b, dir_b, dir_b, dir_b,
                  kf, kf, tok_f, dkf, kb, kb, tok_b, dkb],
        out_specs=[tok_f, tok_b, tok_f, tok_b],
        scratch_shapes=[pltpu.VMEM((2, RW_WIDTH // LANES, LANES, LANES), F32),
                        pltpu.VMEM((2, GLA_KW // LANES, GLA_DV, LANES), F32)],
        compiler_params=_cparams(("parallel", "arbitrary")),
        name="scans",
    )(r, kk, v, lw, b, ke, r, kk, v, lw, b, ke, q, k, gv, lg, q, k, gv, lg)
    return (yf, yb), (of, ob)


CONV_K = TB + LANES


def _conv_shift_table():
    t = jnp.arange(TB)[:, None]
    s = jnp.arange(CONV_K)[None, :]
    prev = jnp.where(t == 0, s == TB + 2 * HALO - 1, s == t - 1)
    nxt = jnp.where(t == TB - 1, s == TB, s == t + 1)
    return jnp.stack([prev, nxt]).astype(BF16)


def _glaprep_kernel(n_tiles, j, p_ref, hp_ref, hn_ref, sh_ref, cw_ref, aup_ref, ab_ref, q_out, k_out, v_out,
                    og_out, lg_out):
    ub = p_ref[0][:, :GLA_QKV]
    u = ub.astype(F32)
    has_prev = jnp.where(j <= 1, 0.0, 1.0)
    has_next = jnp.where(jnp.logical_or(j == 0, j == n_tiles - 1), 0.0, 1.0)
    ext = jnp.concatenate([ub,
                           (hn_ref[0][:, :GLA_QKV].astype(F32) * has_next).astype(BF16),
                           (hp_ref[0][:, :GLA_QKV].astype(F32) * has_prev).astype(BF16),
                           jnp.zeros((CONV_K - TB - 2 * HALO, GLA_QKV), BF16)], axis=0)
    prev1 = jnp.dot(sh_ref[0], ext, preferred_element_type=F32)
    next1 = jnp.dot(sh_ref[1], ext, preferred_element_type=F32)
    cw = cw_ref[...]
    conv = cw[0:1] * prev1 + cw[1:2] * u + cw[2:3] * next1
    qkv = conv * _sigmoid(conv)
    q_out[0] = (qkv[:, :GLA_KW] * (GLA_DK ** -0.5)).astype(ACT)
    k_out[0] = qkv[:, GLA_KW:2 * GLA_KW].astype(ACT)
    v_out[0] = qkv[:, 2 * GLA_KW:].astype(ACT)
    og_out[0] = p_ref[0][:, GLA_QKV:GLA_QKV + GLA_WIDTH]
    z = _bdot(p_ref[0][:, GLA_QKV + GLA_WIDTH:], aup_ref[...]) + ab_ref[...]
    lg = (jnp.minimum(z, 0.0) - jnp.log1p(jnp.exp(-jnp.abs(z)))) * (1.0 / GLA_GATE_NORM)
    for d in range(2):
        lg_out[d, 0] = lg[:, d * GLA_KW:(d + 1) * GLA_KW]


def _glaprep_tile(p_gl, k):
    _, N, _ = p_gl.shape
    hb = TB // HALO
    nhb = N // HALO
    t = _group_tile(k)
    specs = [pl.BlockSpec((1, TB, GLA_PAD), lambda b, j: (b, t(j), 0)),
             pl.BlockSpec((1, HALO, GLA_PAD), lambda b, j: (b, jnp.maximum(t(j) * hb - 1, 0), 0)),
             pl.BlockSpec((1, HALO, GLA_PAD), lambda b, j: (b, jnp.minimum(t(j) * hb + hb, nhb - 1), 0))]
    return [p_gl, p_gl, p_gl], specs


def _glaprep_params(prm):
    specs = [_const_spec((2, TB, CONV_K)), _const_spec((3, GLA_QKV)), _const_spec((LANES, 2 * GLA_KW)),
             _const_spec((1, 2 * GLA_KW))]
    return [_conv_shift_table(), prm["conv"], prm["a_up"], prm["a_b"]], specs


def _glaprep_outs(B, N):
    shapes = [jax.ShapeDtypeStruct((B, N, GLA_KW), ACT), jax.ShapeDtypeStruct((B, N, GLA_KW), ACT),
              jax.ShapeDtypeStruct((B, N, GLA_WIDTH), ACT), jax.ShapeDtypeStruct((B, N, GLA_WIDTH), ACT),
              jax.ShapeDtypeStruct((2, B, N, GLA_KW), F32)]
    return shapes, [_group_out(GLA_KW), _group_out(GLA_KW), _group_out(GLA_WIDTH), _group_out(GLA_WIDTH),
                    _group_dir_out(GLA_KW)]


class _TileRows:
    def __init__(self, ref, rows):
        self.ref, self.rows = ref, rows

    def __setitem__(self, idx, value):
        idx = idx if isinstance(idx, tuple) else (idx,)
        self.ref[idx + (self.rows, slice(None))] = value


def _prep_kernel(has_vres, n_tiles, n_rw_tile, n_gla_tile, n_rw_par, n_gla_par, n_rw_out, *refs):
    rw_tiles = [refs[k * n_rw_tile:(k + 1) * n_rw_tile] for k in range(PREP_GROUP)]
    refs = refs[PREP_GROUP * n_rw_tile:]
    gla_tiles = [refs[k * n_gla_tile:(k + 1) * n_gla_tile] for k in range(PREP_GROUP)]
    refs = refs[PREP_GROUP * n_gla_tile:]
    rw_par, refs = refs[:n_rw_par], refs[n_rw_par:]
    gla_par, outs = refs[:n_gla_par], refs[n_gla_par:]
    n_base = n_rw_par - 3 if has_vres else n_rw_par
    for k in range(PREP_GROUP):
        t = _group_tile(k)(pl.program_id(1))
        views = [_TileRows(o, slice(k * TB, (k + 1) * TB)) for o in outs]
        tile = rw_tiles[k]
        rw_in = list(tile[:4]) + list(rw_par[:n_base])
        if has_vres:
            rw_in += [tile[4]] + list(rw_par[n_base:])
        _rwprep_kernel(has_vres, n_tiles, t, *rw_in, *views[:n_rw_out])
        _glaprep_kernel(n_tiles, t, *gla_tiles[k], *gla_par, *views[n_rw_out:])


def _prep(p_rw, p_gl, prm, gla_prm, v_first):
    B, N, _ = p_rw.shape
    nt = N // TB
    assert nt % PREP_GROUP == 0
    has_vres = v_first is not None
    args, specs = [], []
    for k in range(PREP_GROUP):
        a, s = _rwprep_tile(p_rw, v_first, k)
        args, specs, n_rw_tile = args + a, specs + s, len(a)
    for k in range(PREP_GROUP):
        a, s = _glaprep_tile(p_gl, k)
        args, specs, n_gla_tile = args + a, specs + s, len(a)
    rw_par, rw_par_specs = _rwprep_params(prm, has_vres)
    gla_par, gla_par_specs = _glaprep_params(gla_prm)
    rw_shapes, rw_out = _rwprep_outs(B, N)
    gla_shapes, gla_out = _glaprep_outs(B, N)
    outs = pl.pallas_call(
        functools.partial(_prep_kernel, has_vres, nt, n_rw_tile, n_gla_tile, len(rw_par), len(gla_par),
                          len(rw_shapes)),
        out_shape=rw_shapes + gla_shapes,
        grid=(B, nt // PREP_GROUP),
        in_specs=specs + rw_par_specs + gla_par_specs,
        out_specs=rw_out + gla_out,
        compiler_params=_cparams(("parallel", "parallel")),
        name="prep",
    )(*args, *rw_par, *gla_par)
    return outs[:len(rw_shapes)], outs[len(rw_shapes):]


def _readout_kernel(group, first, split_src, *refs):
    tok = [refs[i * group:(i + 1) * group] for i in range(7)]
    refs = refs[7 * group:]
    n_res = group + 1 if split_src else group
    res_refs, refs = refs[:n_res], refs[n_res:]
    mod_ref, gnw_ref, gnb_ref, ggn_ref, wout_ref, gpost_ref, bd_ref, xo_ref = refs
    bd = bd_ref[...]
    starts_with_ctx = jnp.logical_and(first == 0, pl.program_id(1) == 0)
    for k in range(group):
        yf_ref, yb_ref, bon_ref, g_ref, of_ref, ob_ref, og_ref = (t[k] for t in tok)
        gate = mod_ref[0, 1, 2:3]
        if split_src:
            x_res = res_refs[k + 1][0]
            if k == 0:
                x_res = jnp.where(starts_with_ctx, res_refs[0][0], x_res)
        else:
            x_res = res_refs[k][0]
        if k == 0 and first == 0:
            gate = jnp.where(starts_with_ctx, mod_ref[0, 0, 2:3], gate)
        y = yf_ref[0].astype(F32) + yb_ref[0].astype(F32)
        mu = _seg_sum(y, bd, exact=False) * (1.0 / RW_HEAD)
        yc = y - mu
        var = _seg_sum(yc * yc, bd, exact=False) * (1.0 / RW_HEAD)
        yn = yc * lax.rsqrt(var + RW_GN_EPS) * gnw_ref[...] + gnb_ref[...]
        rw = (yn + bon_ref[0].astype(F32)) * g_ref[0].astype(F32)
        o = of_ref[0].astype(F32) + ob_ref[0].astype(F32)
        og = og_ref[0].astype(F32)
        parts = [rw.astype(BF16)]
        for h in range(GLA_HEADS):
            sl = slice(GLA_DV * h, GLA_DV * (h + 1))
            oh = o[:, sl]
            on = oh * lax.rsqrt(jnp.mean(oh * oh, axis=-1, keepdims=True) + GLA_NORM_EPS)
            ogh = og[:, sl]
            parts.append((on * ggn_ref[:, sl] * (ogh * _sigmoid(ogh))).astype(BF16))
        cat = jnp.concatenate(parts, axis=1)
        mx = jnp.dot(cat, wout_ref[...], preferred_element_type=F32)
        xo_ref[0, k * TB:(k + 1) * TB, :] = x_res + gate * _rms(mx, gpost_ref[...])


def _readout(y, bonus, g, o, og, ctx, x, xc, modtab, prm, latents_only):
    B, N, _ = bonus.shape
    first = 1 if latents_only else 0
    n_tiles = N // TB - first
    group = max(k for k in (4, 3, 2, 1) if n_tiles % k == 0)

    def tiles(c, shift=0):
        return [pl.BlockSpec((1, TB, c), lambda b, j, k=k: (b, jnp.maximum(first + group * j + k + shift, 0), 0))
                for k in range(group)]

    if xc is None:
        res, res_specs = [ctx] + [x] * group, [pl.BlockSpec((1, TB, D), lambda b, j: (b, 0, 0))] + tiles(D, -1)
    else:
        res, res_specs = [xc] * group, tiles(D)
    tok_args, tok_specs = [], []
    for arr in (y[0], y[1], bonus, g, o[0], o[1], og):
        tok_args += [arr] * group
        tok_specs += tiles(arr.shape[-1])
    return pl.pallas_call(
        functools.partial(_readout_kernel, group, first, xc is None),
        out_shape=jax.ShapeDtypeStruct((B, n_tiles * TB, D), F32),
        grid=(B, n_tiles // group),
        in_specs=tok_specs + res_specs + [_wide_mod_spec(),
                  _const_spec((1, RW_WIDTH)), _const_spec((1, RW_WIDTH)), _const_spec((1, GLA_WIDTH)),
                  _const_spec((D, D)), _const_spec((1, D)), _const_spec((RW_WIDTH, RW_WIDTH))],
        out_specs=pl.BlockSpec((1, group * TB, D), lambda b, j: (b, j, 0)),
        compiler_params=_cparams(("parallel", "parallel")),
        name="readout",
    )(*tok_args, *res, modtab, prm["gn_w"], prm["gn_b"], prm["gla_gn_w"],
      prm["w_out"], prm["norm_post"], prm["bd64"])


def _swiglu_acc(hb, wg_ref, wu_ref, wd_ref):
    acc = jnp.zeros((hb.shape[0], D), F32)
    for c in range(D_FF // FCH):
        sl = slice(c * FCH, (c + 1) * FCH)
        gate = jnp.dot(hb, wg_ref[:, sl].astype(BF16), preferred_element_type=F32)
        up = jnp.dot(hb, wu_ref[:, sl].astype(BF16), preferred_element_type=F32)
        act = (gate * _sigmoid(gate) * up).astype(BF16)
        acc = acc + jnp.dot(act, wd_ref[sl, :].astype(BF16), preferred_element_type=F32)
    return acc


def _ffn_kernel(x_ref, mod_ref, gpre_ref, gpost_ref, wg_ref, wu_ref, wd_ref, xo_ref):
    x = x_ref[0]
    hb = (_rms(x, gpre_ref[...]) * (1.0 + _wide_mod(mod_ref, 4)) + _wide_mod(mod_ref, 3)).astype(BF16)
    fx = _swiglu_acc(hb, wg_ref, wu_ref, wd_ref)
    xo_ref[0] = x + _wide_mod(mod_ref, 5) * _rms(fx, gpost_ref[...])


def _single_buffered(shape):
    nd = len(shape)
    return pl.BlockSpec(shape, lambda *_: (0,) * nd, pipeline_mode=pl.Buffered(1))


def _ffn(xc, modtab, g_pre, g_post, wg, wu, wd):
    B, N, _ = xc.shape
    wide = pl.BlockSpec((1, TW, D), lambda b, j: (b, j, 0))
    return pl.pallas_call(
        _ffn_kernel,
        out_shape=jax.ShapeDtypeStruct((B, N, D), F32),
        grid=(B, N // TW),
        in_specs=[wide, _wide_mod_spec(), _const_spec((1, D)), _const_spec((1, D)),
                  _single_buffered((D, D_FF)), _single_buffered((D, D_FF)), _single_buffered((D_FF, D))],
        out_specs=wide,
        compiler_params=_cparams(("parallel", "parallel")),
        name="ffn",
    )(xc, modtab, g_pre, g_post, wg, wu, wd)


MOE_TB = 512
MOE_SEG = 16
MOE_TM = 512
MOE_R = 2 * MOE_TB + N_EXPERTS * MOE_SEG
MOE_NP = MOE_R // MOE_SEG


def _moe_route_kernel(x_ref, mod_ref, gpre_ref, router_ref, h_ref, info_ref, infot_ref, cnt_ref):
    mod = mod_ref[0, 0]
    h = _rms(x_ref[0], gpre_ref[...]) * (1.0 + mod[4:5]) + mod[3:4]
    hb = h.astype(BF16)
    h_ref[0] = hb
    lane = lax.broadcasted_iota(jnp.int32, (MOE_TB, LANES), 1)
    h_lo = (h - hb.astype(F32)).astype(BF16)
    logits = (jnp.dot(hb, router_ref[0], preferred_element_type=F32)
              + jnp.dot(hb, router_ref[1], preferred_element_type=F32)
              + jnp.dot(h_lo, router_ref[0], preferred_element_type=F32))
    logits = jnp.where(lane < N_EXPERTS, logits, -jnp.inf)
    v1 = jnp.max(logits, axis=-1, keepdims=True)
    i1 = jnp.min(jnp.where(logits == v1, lane, LANES), axis=-1, keepdims=True)
    rest = jnp.where(lane == i1, -jnp.inf, logits)
    v2 = jnp.max(rest, axis=-1, keepdims=True)
    i2 = jnp.min(jnp.where(rest == v2, lane, LANES), axis=-1, keepdims=True)
    ex = jnp.exp(v2 - v1)
    w1 = 1.0 / (1.0 + ex)
    w2 = ex * w1
    e1 = jnp.where(lane == i1, 1.0, 0.0)
    e2 = jnp.where(lane == i2, 1.0, 0.0)
    es = e1 + e2
    t = lax.broadcasted_iota(jnp.int32, (MOE_TB, MOE_TB), 0)
    s = lax.broadcasted_iota(jnp.int32, (MOE_TB, MOE_TB), 1)
    before = jnp.where(s < t, 1.0, 0.0).astype(BF16)
    rank = jnp.dot(before, es.astype(BF16), preferred_element_type=F32)
    cnt = jnp.sum(es, axis=0, keepdims=True)
    segs = jnp.floor((cnt + (MOE_SEG - 1)) * (1.0 / MOE_SEG))
    ea = lax.broadcasted_iota(jnp.int32, (LANES, LANES), 0)
    eb = lax.broadcasted_iota(jnp.int32, (LANES, LANES), 1)
    earlier = jnp.where(ea < eb, 1.0, 0.0).astype(BF16)
    start = jnp.dot(jnp.broadcast_to(segs, (SUBLANES, LANES)).astype(BF16), earlier,
                    preferred_element_type=F32)[0:1] * MOE_SEG
    pos = rank + start
    d1 = jnp.sum(e1 * pos, axis=-1, keepdims=True)
    d2 = jnp.sum(e2 * pos, axis=-1, keepdims=True)
    info = jnp.where(lane == 0, d1, jnp.where(lane == 1, d2, jnp.where(lane == 2, w1, jnp.where(lane == 3, w2, 0.0))))
    info_ref[0] = info
    infot_ref[0] = jnp.transpose(info)[0:SUBLANES]
    cnt_ref[0] = jnp.broadcast_to(cnt, (SUBLANES, LANES))


def _moe_route(xs, modtab, g_pre, router):
    B, S, _ = xs.shape
    per = S // MOE_TB
    nb = B * per
    blk = lambda c: pl.BlockSpec((1, MOE_TB, c), lambda i: (i // per, i % per, 0))
    flat = lambda r, c: pl.BlockSpec((1, r, c), lambda i: (i, 0, 0))
    return pl.pallas_call(
        _moe_route_kernel,
        out_shape=[jax.ShapeDtypeStruct((nb, MOE_TB, D), BF16), jax.ShapeDtypeStruct((nb, MOE_TB, LANES), F32),
                   jax.ShapeDtypeStruct((nb, SUBLANES, MOE_TB), F32),
                   jax.ShapeDtypeStruct((nb, SUBLANES, LANES), F32)],
        grid=(nb,),
        in_specs=[blk(D), pl.BlockSpec((1, 1, 6, D), lambda i: (i // per, 1, 0, 0)),
                  _const_spec((1, D)), _const_spec((2, D, LANES))],
        out_specs=[flat(MOE_TB, D), flat(MOE_TB, LANES), flat(SUBLANES, MOE_TB), flat(SUBLANES, LANES)],
        compiler_params=_cparams(("parallel",)),
        name="moe_route",
    )(xs, modtab, g_pre, router)


def _moe_plan(cnt, n_tiles):
    pc = (cnt + MOE_SEG - 1) // MOE_SEG * MOE_SEG
    inc = jnp.cumsum(pc, axis=1)
    loff = inc - pc
    reg = (jnp.sum(pc, axis=0) + MOE_TM - 1) // MOE_TM * MOE_TM
    gend = jnp.cumsum(reg)
    goff = (gend - reg)[None, :] + jnp.cumsum(pc, axis=0) - pc
    rows = jnp.arange(MOE_NP, dtype=jnp.int32) * MOE_SEG
    e_p = jnp.sum((rows[None, :, None] >= inc[:, None, :]).astype(jnp.int32), axis=-1)
    e_c = jnp.minimum(e_p, N_EXPERTS - 1)
    pick = (e_c[:, :, None] == jnp.arange(N_EXPERTS)[None, None, :]).astype(jnp.int32)
    dst = jnp.sum(pick * (goff - loff)[:, None, :], axis=-1) + rows[None, :]
    dst = jnp.where(e_p < N_EXPERTS, dst, 0).astype(jnp.int32)
    n_valid = (inc[:, -1] // MOE_SEG).astype(jnp.int32)
    trow = jnp.arange(n_tiles, dtype=jnp.int32) * MOE_TM
    te = jnp.sum((trow[:, None] >= gend[None, :]).astype(jnp.int32), axis=-1)
    valid = te < N_EXPERTS
    last = gend[-1] // MOE_TM - 1
    te = jnp.where(valid, te, te[last]).astype(jnp.int32)
    src = jnp.where(valid, jnp.arange(n_tiles, dtype=jnp.int32), last).astype(jnp.int32)
    fresh = valid & jnp.concatenate([jnp.ones((1,), bool), te[1:] != te[:-1]])
    return dst, n_valid, te, src, valid.astype(jnp.int32), fresh.astype(jnp.int32)


def _piece_copy(src_ref, src_row, dst_ref, dst_row, sem):
    return pltpu.make_async_copy(src_ref.at[pl.ds(src_row, MOE_SEG)], dst_ref.at[pl.ds(dst_row, MOE_SEG)], sem)


def _moe_gather_kernel(dst_ref, nv_ref, h_ref, infot_ref, xg_in_ref, xg_ref, buf_ref, sem_ref):
    del xg_in_ref
    i = pl.program_id(0)
    it = infot_ref[0]
    rr = lax.broadcasted_iota(jnp.int32, (MOE_R, MOE_TB), 0).astype(F32)
    onehot = jnp.where(rr == it[0:1], 1.0, jnp.where(rr == it[1:2], 1.0, 0.0)).astype(BF16)
    slot = i % 2
    buf_ref[slot] = jnp.dot(onehot, h_ref[0], preferred_element_type=F32).astype(BF16)

    def piece(blk, p):
        return _piece_copy(buf_ref.at[blk % 2], pl.multiple_of(p * MOE_SEG, MOE_SEG),
                           xg_ref, pl.multiple_of(dst_ref[blk, p], MOE_SEG), sem_ref.at[blk % 2, p])

    def start(p, c):
        piece(i, p).start()
        return c

    lax.fori_loop(0, nv_ref[i], start, 0)

    def wait_block(blk):
        def wait(p, c):
            piece(blk, p).wait()
            return c
        lax.fori_loop(0, nv_ref[blk], wait, 0)

    @pl.when(i > 0)
    def _():
        wait_block(i - 1)

    @pl.when(i == pl.num_programs(0) - 1)
    def _():
        wait_block(i)


def _moe_gather(dst, n_valid, h, info_t, n_rows):
    nb = h.shape[0]
    flat = lambda r, c: pl.BlockSpec((1, r, c), lambda i, *_: (i, 0, 0))
    return pl.pallas_call(
        _moe_gather_kernel,
        out_shape=jax.ShapeDtypeStruct((n_rows, D), BF16),
        grid_spec=pltpu.PrefetchScalarGridSpec(
            num_scalar_prefetch=2, grid=(nb,),
            in_specs=[flat(MOE_TB, D), flat(SUBLANES, MOE_TB), pl.BlockSpec(memory_space=pl.ANY)],
            out_specs=pl.BlockSpec(memory_space=pl.ANY),
            scratch_shapes=[pltpu.VMEM((2, MOE_R, D), BF16), pltpu.SemaphoreType.DMA((2, MOE_NP))]),
        input_output_aliases={4: 0},
        compiler_params=_cparams(("arbitrary",)),
        name="moe_gather",
    )(dst, n_valid, h, info_t, jnp.zeros((n_rows, D), BF16))


N_FCH = D_FF // FCH


def _moe_ffn_kernel(te_ref, src_ref, valid_ref, fresh_ref, x_ref, wg_hbm, wu_hbm, wd_hbm, y_ref,
                    wg_ref, wu_ref, wd_ref, sem_ref):
    i = pl.program_id(0)
    e = te_ref[i]
    fresh = fresh_ref[i] == 1

    def slice_copies(c):
        cols = pl.ds(c * FCH, FCH)
        return (pltpu.make_async_copy(wg_hbm.at[e, :, cols], wg_ref.at[:, cols], sem_ref.at[0, c]),
                pltpu.make_async_copy(wu_hbm.at[e, :, cols], wu_ref.at[:, cols], sem_ref.at[1, c]),
                pltpu.make_async_copy(wd_hbm.at[e, cols, :], wd_ref.at[cols, :], sem_ref.at[2, c]))

    @pl.when(fresh)
    def _():
        for c in range(N_FCH):
            for cp in slice_copies(c):
                cp.start()
        hb = x_ref[...]
        acc = jnp.zeros((MOE_TM, D), F32)
        for c in range(N_FCH):
            for cp in slice_copies(c):
                cp.wait()
            sl = slice(c * FCH, (c + 1) * FCH)
            gate = jnp.dot(hb, wg_ref[:, sl].astype(BF16), preferred_element_type=F32)
            up = jnp.dot(hb, wu_ref[:, sl].astype(BF16), preferred_element_type=F32)
            act = (gate * _sigmoid(gate) * up).astype(BF16)
            acc = acc + jnp.dot(act, wd_ref[sl, :].astype(BF16), preferred_element_type=F32)
        y_ref[...] = acc.astype(BF16)

    @pl.when(jnp.logical_and(valid_ref[i] == 1, jnp.logical_not(fresh)))
    def _():
        y_ref[...] = _swiglu_acc(x_ref[...], wg_ref, wu_ref, wd_ref).astype(BF16)

    @pl.when(valid_ref[i] == 0)
    def _():
        y_ref[...] = jnp.zeros_like(y_ref)


def _moe_ffn(te, src, valid, fresh, xg, wg, wu, wd):
    n_rows = xg.shape[0]
    hbm = pl.BlockSpec(memory_space=pl.ANY)
    return pl.pallas_call(
        _moe_ffn_kernel,
        out_shape=jax.ShapeDtypeStruct((n_rows, D), BF16),
        grid_spec=pltpu.PrefetchScalarGridSpec(
            num_scalar_prefetch=4, grid=(n_rows // MOE_TM,),
            in_specs=[pl.BlockSpec((MOE_TM, D), lambda i, te, src, *_: (src[i], 0)), hbm, hbm, hbm],
            out_specs=pl.BlockSpec((MOE_TM, D), lambda i, *_: (i, 0)),
            scratch_shapes=[pltpu.VMEM((D, D_FF), F32), pltpu.VMEM((D, D_FF), F32), pltpu.VMEM((D_FF, D), F32),
                            pltpu.SemaphoreType.DMA((3, N_FCH))]),
        compiler_params=_cparams(("arbitrary",)),
        name="moe_ffn",
    )(te, src, valid, fresh, xg, wg, wu, wd)


def _moe_combine_kernel(dst_ref, nv_ref, x_ref, mod_ref, gpost_ref, info_ref, yg_ref, xo_ref, buf_ref, sem_ref):
    i = pl.program_id(0)

    def piece(blk, p):
        return _piece_copy(yg_ref, pl.multiple_of(dst_ref[blk, p], MOE_SEG),
                           buf_ref.at[blk % 2], pl.multiple_of(p * MOE_SEG, MOE_SEG), sem_ref.at[blk % 2, p])

    def fetch(blk):
        def start(p, c):
            piece(blk, p).start()
            return c

        def clear(p, c):
            buf_ref[blk % 2, pl.ds(pl.multiple_of(p * MOE_SEG, MOE_SEG), MOE_SEG), :] = jnp.zeros((MOE_SEG, D), BF16)
            return c

        lax.fori_loop(0, nv_ref[blk], start, 0)
        lax.fori_loop(nv_ref[blk], MOE_NP, clear, 0)

    @pl.when(i == 0)
    def _():
        fetch(i)

    @pl.when(i + 1 < pl.num_programs(0))
    def _():
        fetch(i + 1)

    info = info_ref[0]
    rr = lax.broadcasted_iota(jnp.int32, (MOE_TB, MOE_R), 1).astype(F32)
    comb = jnp.where(rr == info[:, 0:1], info[:, 2:3], jnp.where(rr == info[:, 1:2], info[:, 3:4], 0.0)).astype(BF16)

    def wait(p, c):
        piece(i, p).wait()
        return c

    lax.fori_loop(0, nv_ref[i], wait, 0)
    fx = jnp.dot(comb, buf_ref[i % 2], preferred_element_type=F32)
    xo_ref[0] = x_ref[0] + mod_ref[0, 0][5:6] * _rms(fx, gpost_ref[...])


def _moe_combine(dst, n_valid, xs, modtab, g_post, info, yg):
    B, S, _ = xs.shape
    per = S // MOE_TB
    blk = pl.BlockSpec((1, MOE_TB, D), lambda i, *_: (i // per, i % per, 0))
    return pl.pallas_call(
        _moe_combine_kernel,
        out_shape=jax.ShapeDtypeStruct((B, S, D), F32),
        grid_spec=pltpu.PrefetchScalarGridSpec(
            num_scalar_prefetch=2, grid=(B * per,),
            in_specs=[blk, pl.BlockSpec((1, 1, 6, D), lambda i, *_: (i // per, 1, 0, 0)),
                      pl.BlockSpec((1, D), lambda i, *_: (0, 0)),
                      pl.BlockSpec((1, MOE_TB, LANES), lambda i, *_: (i, 0, 0)),
                      pl.BlockSpec(memory_space=pl.ANY)],
            out_specs=blk,
            scratch_shapes=[pltpu.VMEM((2, MOE_R, D), BF16), pltpu.SemaphoreType.DMA((2, MOE_NP))]),
        compiler_params=_cparams(("arbitrary",)),
        name="moe_combine",
    )(dst, n_valid, xs, modtab, g_post, info, yg)


def _moe(xs, modtab, g_pre, g_post, router, wg, wu, wd):
    B, S, _ = xs.shape
    nb = B * S // MOE_TB
    h, info, info_t, cnt = _moe_route(xs, modtab, g_pre, router)
    worst = 2 * B * S + nb * N_EXPERTS * (MOE_SEG - 1) + N_EXPERTS * (MOE_TM - 1)
    n_tiles = -(-worst // MOE_TM)
    dst, n_valid, te, src, valid, fresh = _moe_plan(cnt[:, 0, :N_EXPERTS].astype(jnp.int32), n_tiles)
    xg = _moe_gather(dst, n_valid, h, info_t, n_tiles * MOE_TM)
    yg = _moe_ffn(te, src, valid, fresh, xg, wg, wu, wd)
    return _moe_combine(dst, n_valid, xs, modtab, g_post, info, yg)


def _block_diag2(w):
    z = jnp.zeros_like(w[0])
    return jnp.concatenate([jnp.concatenate([w[0], z], axis=1), jnp.concatenate([z, w[1]], axis=1)], axis=0)


def _row(v):
    return v.reshape(1, -1).astype(F32)


def _head_ones(width, head):
    i = jnp.arange(width) // head
    return (i[:, None] == i[None, :]).astype(BF16)


def kernel(x, c, ctx, c_ctx, ada_w, ada_b, norm_mix_pre, norm_mix_post, norm_ffn_pre, norm_ffn_post, w_in, shift_mu, rw_w_up, rw_w0, rw_a_up, rw_a0, rw_k_k, rw_k_a, rw_r_k, rw_g_up, rw_gn_w, rw_gn_b, rw_v_down, rw_v_up, rw_v0, gla_conv, gla_a_up, gla_a_b, gla_gn_w, w_out, ffn_w_gate, ffn_w_up, ffn_w_down, moe_router, moe_w_gate, moe_w_up, moe_w_down):
    B, S, _ = x.shape
    n_ctx = ctx.shape[1]
    depth = w_in.shape[0]
    assert n_ctx == TB and S % MOE_TB == 0 and (n_ctx + S) % TW == 0 and depth == 2

    xc = None
    pad_rows = -(B + 1) % SUBLANES
    cvec = jnp.concatenate([c, c_ctx[None, :], jnp.zeros((pad_rows, D), F32)], axis=0)
    bd64 = _head_ones(RW_WIDTH, RW_HEAD)
    ada_b3 = ada_b.reshape(depth, 1, 6 * D)
    v_first = None
    out = None
    for i in range(depth):
        last = i == depth - 1
        mods = _adaln(cvec, ada_w, ada_b3, i)
        mod_x = mods[:B].reshape(B, 6, D)
        mod_c = jnp.broadcast_to(mods[B].reshape(1, 6, D), (B, 6, D))
        modtab = jnp.stack([mod_c, mod_x], axis=1)

        p_rw, p_gl = _inproj(ctx, x, xc, _row(norm_mix_pre[i]), modtab, w_in, i)

        prm = dict(
            mu=_row(shift_mu[i]),
            w_up=_block_diag2(rw_w_up[i]).astype(BF16), w0=_row(rw_w0[i]),
            a_up=_block_diag2(rw_a_up[i]).astype(BF16), a0=_row(rw_a0[i]),
            k_k=_row(rw_k_k[i]), k_a=_row(rw_k_a[i]), r_k=_row(rw_r_k[i]),
            g_up=rw_g_up[i].astype(BF16), bd64=bd64,
            gn_w=_row(rw_gn_w[i]), gn_b=_row(rw_gn_b[i]), gla_gn_w=_row(gla_gn_w[i]),
            w_out=w_out[i].astype(BF16), norm_post=_row(norm_mix_post[i]),
        )
        gate_pad = jnp.zeros((LANES - 2 * GLA_GATE_RANK, 2 * GLA_KW), F32)
        gla_prm = dict(conv=gla_conv[i].astype(F32), a_b=_row(gla_a_b[i]),
                       a_up=jnp.concatenate([_block_diag2(gla_a_up[i]), gate_pad], axis=0).astype(BF16))
        if i > 0:
            pad = LANES - RW_V_RANK
            prm["v_down"] = jnp.concatenate([rw_v_down[i - 1], jnp.zeros((RW_WIDTH, pad), F32)], axis=1).astype(BF16)
            prm["v_up"] = jnp.concatenate([rw_v_up[i - 1], jnp.zeros((pad, RW_WIDTH), F32)], axis=0).astype(BF16)
            prm["v0"] = _row(rw_v0[i - 1])

        (r, kk, vm, g, bonus, lw, bb, ke), (q, k, gv, og, lg) = _prep(p_rw, p_gl, prm, gla_prm,
                                                                     v_first if i > 0 else None)
        if i == 0:
            v_first = vm
        y, o = _scans(r, kk, vm, lw, bb, ke, q, k, gv, lg)
        xc = _readout(y, bonus, g, o, og, ctx, x, xc, modtab, prm, latents_only=last)

        jf = i // 2
        if i % 2 == 0:
            xc = _ffn(xc, modtab, _row(norm_ffn_pre[i]), _row(norm_ffn_post[i]),
                      ffn_w_gate[jf], ffn_w_up[jf], ffn_w_down[jf])
        else:
            router = jnp.concatenate([moe_router[jf], jnp.zeros((D, LANES - N_EXPERTS), F32)], axis=1)
            r_hi = router.astype(BF16)
            router = jnp.stack([r_hi, (router - r_hi.astype(F32)).astype(BF16)])
            out = _moe(xc, modtab, _row(norm_ffn_pre[i]), _row(norm_ffn_post[i]), router,
                       moe_w_gate[jf], moe_w_up[jf], moe_w_down[jf])
    return out
```

```python
import functools
import math

import jax
import jax.numpy as jnp
from jax import lax
from jax.experimental import pallas as pl
from jax.experimental.pallas import tpu as pltpu

F32, BF16 = jnp.float32, jnp.bfloat16
ACT = BF16

D = 1024
GRID_W = 64
RW_WIDTH = 512
RW_HEAD = 64
RW_RANK = 64
RW_G_RANK = 128
RW_V_RANK = 32
RW_GN_EPS = 64e-5
GLA_WIDTH = 512
GLA_HEADS = 4
GLA_DV = 128
GLA_DK = 64
GLA_KW = 256
GLA_GATE_RANK = 16
GLA_GATE_NORM = 16.0
GLA_NORM_EPS = 1e-5
D_FF = 2816
N_EXPERTS = 8
NORM_EPS = 1e-6
RW_COLS = 3 * RW_WIDTH + 4 * RW_RANK + RW_G_RANK
GLA_QKV = 2 * GLA_KW + GLA_WIDTH
GLA_COLS = GLA_QKV + GLA_WIDTH + 2 * GLA_GATE_RANK

LANES = 128
SUBLANES = 8
GLA_PAD = -(-GLA_COLS // LANES) * LANES
TB = 256
TW = 3 * TB
CH = 64
FCH = 256
HALO = 16
VMEM_LIMIT = 56 * 1024 * 1024


def _cparams(sem):
    return pltpu.CompilerParams(dimension_semantics=sem, vmem_limit_bytes=VMEM_LIMIT)


def _bdot(a, b):
    return jnp.dot(a.astype(BF16), b.astype(BF16), preferred_element_type=F32)


def _bdot_nt(a, b):
    return lax.dot_general(a.astype(BF16), b.astype(BF16), (((1,), (1,)), ((), ())),
                           preferred_element_type=F32)


def _bdot_tn(a, b):
    return lax.dot_general(a.astype(BF16), b.astype(BF16), (((0,), (0,)), ((), ())),
                           preferred_element_type=F32)


def _split_dot(a_exact, x):
    h1 = x.astype(BF16)
    r1 = x - h1.astype(F32)
    h2 = r1.astype(BF16)
    h3 = (r1 - h2.astype(F32)).astype(BF16)
    return (jnp.dot(a_exact, h1, preferred_element_type=F32)
            + jnp.dot(a_exact, h2, preferred_element_type=F32)
            + jnp.dot(a_exact, h3, preferred_element_type=F32))


def _seg_sum(x, bd, exact=True):
    hi = x.astype(BF16)
    out = jnp.dot(hi, bd, preferred_element_type=F32)
    if exact:
        lo = (x - hi.astype(F32)).astype(BF16)
        out = out + jnp.dot(lo, bd, preferred_element_type=F32)
    return out


def _sigmoid(x):
    return jax.nn.sigmoid(x)


def _rms(x, g):
    return x * lax.rsqrt(jnp.mean(x * x, axis=-1, keepdims=True) + NORM_EPS) * g


def _pair_stack(z, lo):
    return jnp.concatenate([jnp.where(lo, z, 0.0), jnp.where(lo, 0.0, z)], axis=0)


def _adaln_kernel(c_ref, w_ref, b_ref, o_ref):
    c = c_ref[...]
    s = c * _sigmoid(c)
    w = w_ref[0]
    s_hi, w_hi = s.astype(BF16), w.astype(BF16)
    s_lo, w_lo = (s - s_hi.astype(F32)).astype(BF16), (w - w_hi.astype(F32)).astype(BF16)
    o_ref[...] = (jnp.dot(s_hi, w_hi, preferred_element_type=F32) + jnp.dot(s_hi, w_lo, preferred_element_type=F32)
                  + jnp.dot(s_lo, w_hi, preferred_element_type=F32)) + b_ref[0]


def _adaln(cvec, w, b, layer):
    rows = cvec.shape[0]
    n = w.shape[2]
    return pl.pallas_call(
        _adaln_kernel,
        out_shape=jax.ShapeDtypeStruct((rows, n), F32),
        grid=(n // D,),
        in_specs=[pl.BlockSpec((rows, D), lambda i: (0, 0)),
                  pl.BlockSpec((1, D, D), lambda i: (layer, 0, i)),
                  pl.BlockSpec((1, 1, D), lambda i: (layer, 0, i))],
        out_specs=pl.BlockSpec((rows, D), lambda i: (0, i)),
        compiler_params=_cparams(("arbitrary",)),
        name="adaln",
    )(cvec, w, b)


def _wide_mod(mod_ref, k):
    is_ctx = jnp.logical_and(pl.program_id(1) == 0, lax.broadcasted_iota(jnp.int32, (TW, 1), 0) < TB)
    return jnp.where(is_ctx, mod_ref[0, 0, k:k + 1], mod_ref[0, 1, k:k + 1])


def _wide_rows(refs):
    if len(refs) == 1:
        return refs[0][0]
    first = jnp.where(pl.program_id(1) == 0, refs[0][0], refs[1][0])
    return jnp.concatenate([first] + [r[0] for r in refs[2:]], axis=0)


def _inproj_kernel(n_src, *refs):
    g_ref, mod_ref, w_ref, prw_ref, pgl_ref = refs[n_src:]
    x = _wide_rows(refs[:n_src])
    h = _rms(x, g_ref[...]) * (1.0 + _wide_mod(mod_ref, 1)) + _wide_mod(mod_ref, 0)
    hb = h.astype(BF16)
    prw_ref[0] = jnp.dot(hb, w_ref[0, :, :RW_COLS].astype(BF16), preferred_element_type=F32).astype(ACT)
    pgl_ref[0, :, :GLA_COLS] = jnp.dot(hb, w_ref[0, :, RW_COLS:].astype(BF16),
                                       preferred_element_type=F32).astype(ACT)
    pgl_ref[0, :, GLA_COLS:] = jnp.zeros((TW, GLA_PAD - GLA_COLS), ACT)


def _wide_src(ctx, x, xc):
    if xc is not None:
        return [xc], [pl.BlockSpec((1, TW, D), lambda b, j: (b, j, 0))]
    per = TW // TB
    specs = [pl.BlockSpec((1, TB, D), lambda b, j: (b, 0, 0))]
    for k in range(per):
        specs.append(pl.BlockSpec((1, TB, D), lambda b, j, k=k: (b, jnp.maximum(j * per + k - 1, 0), 0)))
    return [ctx] + [x] * per, specs


def _wide_mod_spec():
    return pl.BlockSpec((1, 2, 6, D), lambda b, j: (b, 0, 0, 0))


def _const_spec(shape):
    nd = len(shape)
    return pl.BlockSpec(shape, lambda *_: (0,) * nd)


def _inproj(ctx, x, xc, g, modtab, w, layer):
    B = modtab.shape[0]
    N = xc.shape[1] if xc is not None else ctx.shape[1] + x.shape[1]
    srcs, src_specs = _wide_src(ctx, x, xc)
    wide = lambda c: pl.BlockSpec((1, TW, c), lambda b, j: (b, j, 0))
    return pl.pallas_call(
        functools.partial(_inproj_kernel, len(srcs)),
        out_shape=[jax.ShapeDtypeStruct((B, N, RW_COLS), ACT),
                   jax.ShapeDtypeStruct((B, N, GLA_PAD), ACT)],
        grid=(B, N // TW),
        in_specs=src_specs + [_const_spec((1, D)), _wide_mod_spec(),
                              pl.BlockSpec((1, D, RW_COLS + GLA_COLS), lambda b, j: (layer, 0, 0),
                                           pipeline_mode=pl.Buffered(1))],
        out_specs=[wide(RW_COLS), wide(GLA_PAD)],
        compiler_params=_cparams(("parallel", "parallel")),
        name="inproj",
    )(*srcs, g, modtab, w)


def _shift_table():
    t = jnp.arange(TB)
    prev = t[:, None] - 1 == t[None, :]
    nxt = t[:, None] + 1 == t[None, :]
    col = (t % GRID_W)[:, None]
    return jnp.stack([jnp.stack([prev, nxt]),
                      jnp.stack([prev & (col != 0), nxt & (col != GRID_W - 1)])]).astype(BF16)


def _rwprep_kernel(has_vres, n_tiles, j, p_ref, hu_ref, hd_ref, sh_ref, mu_ref, wup_ref, w0_ref, aup_ref,
                   a0_ref, kk_ref, ka_ref, rk_ref, gup_ref, bd_ref, *rest):
    if has_vres:
        vf_ref, vdn_ref, vup_ref, v0_ref = rest[:4]
        rest = rest[4:]
    r_out, kk_out, v_out, g_out, bon_out, lw_out, b_out, ke_out = rest
    pb = p_ref[0]
    p = pb.astype(F32)
    prev1 = jnp.dot(sh_ref[0, 0], pb, preferred_element_type=F32)
    next1 = jnp.dot(sh_ref[0, 1], pb, preferred_element_type=F32)
    has_upper = jnp.where(j == 1, 0.0, 1.0)
    has_lower = jnp.where(j == n_tiles - 1, 0.0, 1.0)
    up = jnp.concatenate([hu_ref[0].astype(F32) * has_upper, p[:TB - GRID_W]], axis=0)
    down = jnp.concatenate([p[GRID_W:], hd_ref[0].astype(F32) * has_lower], axis=0)
    cls = lax.broadcasted_iota(jnp.int32, p.shape, 1) & jnp.where(j == 0, 1, 3)
    shifted = jnp.where(cls == 0, prev1, jnp.where(cls == 1, next1, jnp.where(cls == 2, up, down)))
    u = p + mu_ref[...] * (shifted - p)

    r = u[:, 0:RW_WIDTH]
    k = u[:, RW_WIDTH:2 * RW_WIDTH]
    v = u[:, 2 * RW_WIDTH:3 * RW_WIDTH]
    o = 3 * RW_WIDTH
    wd = u[:, o:o + 2 * RW_RANK]
    ad = u[:, o + 2 * RW_RANK:o + 4 * RW_RANK]
    gd = u[:, o + 4 * RW_RANK:]

    w_logit = w0_ref[...] + _bdot(jnp.tanh(wd), wup_ref[...])
    lw = -math.exp(-0.5) * _sigmoid(w_logit)
    a = _sigmoid(a0_ref[...] + _bdot(ad, aup_ref[...]))
    bd = bd_ref[...]
    kk = k * kk_ref[...]
    kk = kk * lax.rsqrt(jnp.maximum(_seg_sum(kk * kk, bd), 1e-24))
    g = _bdot(_sigmoid(gd), gup_ref[...])
    if has_vres:
        gate = _sigmoid(v0_ref[...] + _bdot(_bdot(v, vdn_ref[...]), vup_ref[...]))
        vm = v + (vf_ref[0].astype(F32) - v) * gate
    else:
        vm = v
    ke_sum = jnp.zeros_like(k)
    for d in range(2):
        a_d = a[:, d * RW_WIDTH:(d + 1) * RW_WIDTH]
        ke_d = k * (1.0 + (a_d - 1.0) * ka_ref[...])
        lw_out[d, 0] = lw[:, d * RW_WIDTH:(d + 1) * RW_WIDTH]
        b_out[d, 0] = (kk * a_d).astype(ACT)
        ke_out[d, 0] = ke_d.astype(ACT)
        ke_sum = ke_sum + ke_d
    r_out[0] = r.astype(ACT)
    kk_out[0] = kk.astype(ACT)
    v_out[0] = vm.astype(ACT)
    g_out[0] = g.astype(ACT)
    bon_out[0] = (_seg_sum(r * ke_sum * rk_ref[...], bd, exact=False) * vm).astype(ACT)


PREP_GROUP = 3


def _group_tile(k):
    return lambda j: PREP_GROUP * j + k


def _rwprep_tile(p_rw, v_first, k):
    _, N, _ = p_rw.shape
    hb = TB // GRID_W
    nhb = N // GRID_W
    t = _group_tile(k)
    specs = [pl.BlockSpec((1, TB, RW_COLS), lambda b, j: (b, t(j), 0)),
             pl.BlockSpec((1, GRID_W, RW_COLS), lambda b, j: (b, jnp.maximum(t(j) * hb - 1, 0), 0)),
             pl.BlockSpec((1, GRID_W, RW_COLS), lambda b, j: (b, jnp.minimum(t(j) * hb + hb, nhb - 1), 0)),
             pl.BlockSpec((1, 2, TB, TB), lambda b, j: (jnp.minimum(t(j), 1), 0, 0, 0))]
    args = [p_rw, p_rw, p_rw, _shift_table()]
    if v_first is not None:
        specs.append(pl.BlockSpec((1, TB, RW_WIDTH), lambda b, j: (b, t(j), 0)))
        args.append(v_first)
    return args, specs


def _rwprep_params(prm, has_vres):
    W2 = 2 * RW_WIDTH
    specs = [_const_spec((1, RW_COLS)), _const_spec((2 * RW_RANK, W2)), _const_spec((1, W2)),
             _const_spec((2 * RW_RANK, W2)), _const_spec((1, W2)),
             _const_spec((1, RW_WIDTH)), _const_spec((1, RW_WIDTH)), _const_spec((1, RW_WIDTH)),
             _const_spec((RW_G_RANK, RW_WIDTH)), _const_spec((RW_WIDTH, RW_WIDTH))]
    args = [prm["mu"], prm["w_up"], prm["w0"], prm["a_up"], prm["a0"],
            prm["k_k"], prm["k_a"], prm["r_k"], prm["g_up"], prm["bd64"]]
    if has_vres:
        specs += [_const_spec((RW_WIDTH, LANES)), _const_spec((LANES, RW_WIDTH)), _const_spec((1, RW_WIDTH))]
        args += [prm["v_down"], prm["v_up"], prm["v0"]]
    return args, specs


def _group_out(c):
    return pl.BlockSpec((1, PREP_GROUP * TB, c), lambda b, j: (b, j, 0))


def _group_dir_out(c):
    return pl.BlockSpec((2, 1, PREP_GROUP * TB, c), lambda b, j: (0, b, j, 0))


def _rwprep_outs(B, N):
    tok = jax.ShapeDtypeStruct((B, N, RW_WIDTH), ACT)
    dtok = jax.ShapeDtypeStruct((2, B, N, RW_WIDTH), ACT)
    shapes = [tok] * 5 + [jax.ShapeDtypeStruct((2, B, N, RW_WIDTH), F32), dtok, dtok]
    return shapes, [_group_out(RW_WIDTH)] * 5 + [_group_dir_out(RW_WIDTH)] * 3


NCH = TB // CH


def _scan_kernel(rf_ref, kkf_ref, vf_ref, lwf_ref, bf_ref, kef_ref,
                 rb_ref, kkb_ref, vb_ref, lwb_ref, bb_ref, keb_ref,
                 gqf_ref, gkf_ref, gvf_ref, lgf_ref, gqb_ref, gkb_ref, gvb_ref, lgb_ref,
                 yf_ref, yb_ref, of_ref, ob_ref, st_ref, gst_ref):
    j = pl.program_id(1)

    @pl.when(j == 0)
    def _():
        st_ref[...] = jnp.zeros_like(st_ref)
        gst_ref[...] = jnp.zeros_like(gst_ref)

    ti = lax.broadcasted_iota(jnp.int32, (CH, LANES), 0)
    li = lax.broadcasted_iota(jnp.int32, (CH, LANES), 1)
    si = li & (CH - 1)
    lo = li < CH
    eye = jnp.where(ti == si, 1.0, 0.0)
    rblk = lax.broadcasted_iota(jnp.int32, (LANES, LANES), 0) // CH
    cblk = lax.broadcasted_iota(jnp.int32, (LANES, LANES), 1) // CH
    bdmask = rblk == cblk
    t64 = lax.broadcasted_iota(jnp.int32, (CH, CH), 0)
    s64 = lax.broadcasted_iota(jnp.int32, (CH, CH), 1)
    tri = [jnp.where(s64 <= t64, 1.0, 0.0).astype(BF16), jnp.where(s64 >= t64, 1.0, 0.0).astype(BF16)]
    m_incl = [si <= ti, si >= ti]
    m_strict = [si < ti, si > ti]
    n_pairs = RW_WIDTH // LANES
    in_refs = ((rf_ref, kkf_ref, vf_ref, lwf_ref, bf_ref, kef_ref),
               (rb_ref, kkb_ref, vb_ref, lwb_ref, bb_ref, keb_ref))
    y_refs = (yf_ref, yb_ref)

    def stack(z):
        return _pair_stack(z, lo).astype(BF16)

    st = {(d, p): st_ref[d, p] for d in range(2) for p in range(n_pairs)}

    def scan_step(step):
        cur = {}
        w_tot = {}
        for d in range(2):
            c = (NCH - 1 - step) if d == 1 else step
            rows = slice(c * CH, (c + 1) * CH)
            refs = in_refs[d]
            r, kk, v = (refs[i][0, rows, :].astype(F32) for i in range(3))
            lw = refs[3][0, 0, rows, :]
            b, ke = refs[4][0, 0, rows, :].astype(F32), refs[5][0, 0, rows, :].astype(F32)
            cum = _split_dot(tri[d], lw)
            tot = jnp.sum(lw, axis=0, keepdims=True)
            w_inv = jnp.exp(-cum)
            w_end = jnp.exp(tot - cum)
            w_tot[d] = jnp.exp(tot)
            rh = r * jnp.exp(cum)
            ah = -(kk * jnp.exp(cum - lw))
            bh, kh, bt, kt = b * w_inv, ke * w_inv, b * w_end, ke * w_end
            for p in range(n_pairs):
                sl = slice(LANES * p, LANES * (p + 1))
                cur[d, p] = dict(ah=ah[:, sl], rh=rh[:, sl], bh=bh[:, sl], kh=kh[:, sl], bt=bt[:, sl],
                                 kt=kt[:, sl], v=v[:, sl], rows=rows, sl=sl)
        yield
        for it in cur.values():
            it["lhs"] = jnp.concatenate([it["ah"], it["rh"]], axis=0).astype(BF16)
            rhs = jnp.concatenate([stack(it["bh"]), stack(it["kh"])], axis=0)
            it["a_all"] = _bdot_nt(it["lhs"], rhs)
        yield
        for (d, _), it in cur.items():
            a_all = it.pop("a_all")
            it["a_ab"] = jnp.where(m_strict[d], a_all[:CH, :LANES], 0.0)
            a_ak = jnp.where(m_strict[d], a_all[:CH, LANES:], 0.0)
            a_rb = jnp.where(m_incl[d], a_all[CH:, :LANES], 0.0)
            a_rk = jnp.where(m_incl[d], a_all[CH:, LANES:], 0.0)
            it["a_r"] = jnp.concatenate([a_rb, a_rk], axis=1).astype(BF16)
            it["v_bd"] = stack(it["v"])
            it["akv"] = _bdot(a_ak, it["v_bd"])
            it["t"] = eye + it["a_ab"]
            it["m"] = _bdot(it["a_ab"], stack(it["a_ab"]))
        yield
        for _ in range(int(math.log2(CH)) - 2):
            for it in cur.values():
                z = _bdot(it["m"], jnp.concatenate([stack(it["m"]), stack(it["t"])], axis=1))
                it["m"] = z[:, :LANES]
                it["t"] = it["t"] + z[:, LANES:]
            yield
        for it in cur.values():
            it["t"] = (it["t"] + _bdot(it["m"], stack(it["t"]))).astype(BF16)
        yield
        for it in cur.values():
            ta = jnp.dot(it["t"], jnp.concatenate([stack(it["ah"]), stack(it["akv"])], axis=1),
                         preferred_element_type=F32)
            it["lhs2"] = jnp.concatenate([ta[:, :LANES].astype(BF16), it["lhs"][CH:]], axis=0)
            it["cc"] = ta[:, LANES:]
            it["rhs_t"] = jnp.concatenate([it["bt"], it["kt"]], axis=0).astype(BF16)
        yield
        z1 = {k: _bdot_nt(it["lhs2"], st[k]) for k, it in cur.items()}
        yield
        u = {k: z1[k][:CH] + it["cc"] for k, it in cur.items()}
        upd = {k: _bdot_tn(jnp.concatenate([u[k], it["v"]], axis=0), it["rhs_t"]) for k, it in cur.items()}
        yield
        for (d, p), it in cur.items():
            y = z1[d, p][CH:] + jnp.dot(it["a_r"], jnp.concatenate([stack(u[d, p]), it["v_bd"]], axis=0),
                                        preferred_element_type=F32)
            y_refs[d][0, it["rows"], it["sl"]] = y.astype(ACT)
            st[d, p] = st[d, p] * w_tot[d][:, it["sl"]] + jnp.where(bdmask, upd[d, p], 0.0)
        yield

    gla_refs = ((gqf_ref, gkf_ref, gvf_ref, lgf_ref), (gqb_ref, gkb_ref, gvb_ref, lgb_ref))
    o_refs = (of_ref, ob_ref)
    g_pairs = GLA_KW // LANES
    gst = {(d, p): gst_ref[d, p] for d in range(2) for p in range(g_pairs)}
    lo2 = lax.broadcasted_iota(jnp.int32, (LANES, LANES), 1) < CH
    zeros_v = jnp.zeros((CH, GLA_DV), F32)

    def gla_step(step):
        cur = {}
        dec = {}
        for d in range(2):
            c = (NCH - 1 - step) if d == 1 else step
            rows = slice(c * CH, (c + 1) * CH)
            refs = gla_refs[d]
            q, k, v = (refs[i][0, rows, :].astype(F32) for i in range(3))
            lg = refs[3][0, 0, rows, :]
            cum = _split_dot(tri[d], lg)
            tot = jnp.sum(lg, axis=0, keepdims=True)
            dec[d] = jnp.exp(tot)
            qd = q * jnp.exp(cum)
            ki = k * jnp.exp(-cum)
            kend = k * jnp.exp(tot - cum)
            for p in range(g_pairs):
                sl = slice(LANES * p, LANES * (p + 1))
                cur[d, p] = dict(qd=qd[:, sl].astype(BF16), ki=ki[:, sl], kend=kend[:, sl], rows=rows, sl=sl,
                                 v0=v[:, 2 * LANES * p:2 * LANES * p + LANES],
                                 v1=v[:, 2 * LANES * p + LANES:2 * LANES * (p + 1)])
        yield
        for it in cur.values():
            it["att"] = _bdot_nt(it["qd"], _pair_stack(it["ki"], lo))
        yield
        for (d, _), it in cur.items():
            att = jnp.where(m_incl[d], it["att"], 0.0)
            v_bd = jnp.concatenate([jnp.concatenate([it["v0"], zeros_v], axis=1),
                                    jnp.concatenate([zeros_v, it["v1"]], axis=1)], axis=0)
            it["o"] = _bdot(att, v_bd)
            it["upd"] = _bdot_tn(jnp.concatenate([it["v0"], it["v1"]], axis=0), _pair_stack(it["kend"], lo))
        yield
        for (d, p), it in cur.items():
            s = gst[d, p]
            s_bd = jnp.concatenate([jnp.where(lo2, s, 0.0), jnp.where(lo2, 0.0, s)], axis=0)
            o = it["o"] + _bdot_nt(it["qd"], s_bd)
            o_refs[d][0, it["rows"], 2 * LANES * p:2 * LANES * (p + 1)] = o.astype(ACT)
            gst[d, p] = s * dec[d][:, it["sl"]] + it["upd"]
        yield

    n_stages = 12
    n_dep = 3
    gla_stages = (1, 4, 7, 10)
    pipeline = [(scan_step(s), gla_step(s)) for s in range(NCH)]
    for slot in range(n_stages + n_dep * (NCH - 1)):
        for s, (rw_gen, gla_gen) in enumerate(pipeline):
            stage = slot - n_dep * s
            if 0 <= stage < n_stages:
                next(rw_gen)
                if stage in gla_stages:
                    next(gla_gen)
    for (d, p), s in st.items():
        st_ref[d, p] = s
    for (d, p), s in gst.items():
        gst_ref[d, p] = s


def _bidir_specs(c, n_tiles):
    def back(j):
        return jnp.where(j == 0, 0, n_tiles - j)

    tok_f = pl.BlockSpec((1, TB, c), lambda b, j: (b, j, 0))
    tok_b = pl.BlockSpec((1, TB, c), lambda b, j: (b, back(j), 0))
    dir_f = pl.BlockSpec((1, 1, TB, c), lambda b, j: (0, b, j, 0))
    dir_b = pl.BlockSpec((1, 1, TB, c), lambda b, j: (1, b, back(j), 0))
    return tok_f, tok_b, dir_f, dir_b


def _scans(r, kk, v, lw, b, ke, q, k, gv, lg):
    B, N, _ = r.shape
    nt = N // TB
    tok_f, tok_b, dir_f, dir_b = _bidir_specs(RW_WIDTH, nt)
    kf, kb, dkf, dkb = _bidir_specs(GLA_KW, nt)
    out = jax.ShapeDtypeStruct((B, N, RW_WIDTH), ACT)
    assert GLA_WIDTH == RW_WIDTH
    yf, yb, of, ob = pl.pallas_call(
        _scan_kernel,
        out_shape=[out] * 4,
        grid=(B, nt),
        in_specs=[tok_f, tok_f, tok_f, dir_f, dir_f, dir_f, tok_b, tok_b, tok_b, dir_b, dir_b, dir_b,
                  kf, kf, tok_f, dkf, kb, kb, tok_b, dkb],
        out_specs=[tok_f, tok_b, tok_f, tok_b],
        scratch_shapes=[pltpu.VMEM((2, RW_WIDTH // LANES, LANES, LANES), F32),
                        pltpu.VMEM((2, GLA_KW // LANES, GLA_DV, LANES), F32)],
        compiler_params=_cparams(("parallel", "arbitrary")),
        name="scans",
    )(r, kk, v, lw, b, ke, r, kk, v, lw, b, ke, q, k, gv, lg, q, k, gv, lg)
    return (yf, yb), (of, ob)


CONV_K = TB + LANES


def _conv_shift_table():
    t = jnp.arange(TB)[:, None]
    s = jnp.arange(CONV_K)[None, :]
    prev = jnp.where(t == 0, s == TB + 2 * HALO - 1, s == t - 1)
    nxt = jnp.where(t == TB - 1, s == TB, s == t + 1)
    return jnp.stack([prev, nxt]).astype(BF16)


def _glaprep_kernel(n_tiles, j, p_ref, hp_ref, hn_ref, sh_ref, cw_ref, aup_ref, ab_ref, q_out, k_out, v_out,
                    og_out, lg_out):
    ub = p_ref[0][:, :GLA_QKV]
    u = ub.astype(F32)
    has_prev = jnp.where(j <= 1, 0.0, 1.0)
    has_next = jnp.where(jnp.logical_or(j == 0, j == n_tiles - 1), 0.0, 1.0)
    ext = jnp.concatenate([ub,
                           (hn_ref[0][:, :GLA_QKV].astype(F32) * has_next).astype(BF16),
                           (hp_ref[0][:, :GLA_QKV].astype(F32) * has_prev).astype(BF16),
                           jnp.zeros((CONV_K - TB - 2 * HALO, GLA_QKV), BF16)], axis=0)
    prev1 = jnp.dot(sh_ref[0], ext, preferred_element_type=F32)
    next1 = jnp.dot(sh_ref[1], ext, preferred_element_type=F32)
    cw = cw_ref[...]
    conv = cw[0:1] * prev1 + cw[1:2] * u + cw[2:3] * next1
    qkv = conv * _sigmoid(conv)
    q_out[0] = (qkv[:, :GLA_KW] * (GLA_DK ** -0.5)).astype(ACT)
    k_out[0] = qkv[:, GLA_KW:2 * GLA_KW].astype(ACT)
    v_out[0] = qkv[:, 2 * GLA_KW:].astype(ACT)
    og_out[0] = p_ref[0][:, GLA_QKV:GLA_QKV + GLA_WIDTH]
    z = _bdot(p_ref[0][:, GLA_QKV + GLA_WIDTH:], aup_ref[...]) + ab_ref[...]
    lg = (jnp.minimum(z, 0.0) - jnp.log1p(jnp.exp(-jnp.abs(z)))) * (1.0 / GLA_GATE_NORM)
    for d in range(2):
        lg_out[d, 0] = lg[:, d * GLA_KW:(d + 1) * GLA_KW]


def _glaprep_tile(p_gl, k):
    _, N, _ = p_gl.shape
    hb = TB // HALO
    nhb = N // HALO
    t = _group_tile(k)
    specs = [pl.BlockSpec((1, TB, GLA_PAD), lambda b, j: (b, t(j), 0)),
             pl.BlockSpec((1, HALO, GLA_PAD), lambda b, j: (b, jnp.maximum(t(j) * hb - 1, 0), 0)),
             pl.BlockSpec((1, HALO, GLA_PAD), lambda b, j: (b, jnp.minimum(t(j) * hb + hb, nhb - 1), 0))]
    return [p_gl, p_gl, p_gl], specs


def _glaprep_params(prm):
    specs = [_const_spec((2, TB, CONV_K)), _const_spec((3, GLA_QKV)), _const_spec((LANES, 2 * GLA_KW)),
             _const_spec((1, 2 * GLA_KW))]
    return [_conv_shift_table(), prm["conv"], prm["a_up"], prm["a_b"]], specs


def _glaprep_outs(B, N):
    shapes = [jax.ShapeDtypeStruct((B, N, GLA_KW), ACT), jax.ShapeDtypeStruct((B, N, GLA_KW), ACT),
              jax.ShapeDtypeStruct((B, N, GLA_WIDTH), ACT), jax.ShapeDtypeStruct((B, N, GLA_WIDTH), ACT),
              jax.ShapeDtypeStruct((2, B, N, GLA_KW), F32)]
    return shapes, [_group_out(GLA_KW), _group_out(GLA_KW), _group_out(GLA_WIDTH), _group_out(GLA_WIDTH),
                    _group_dir_out(GLA_KW)]


class _TileRows:
    def __init__(self, ref, rows):
        self.ref, self.rows = ref, rows

    def __setitem__(self, idx, value):
        idx = idx if isinstance(idx, tuple) else (idx,)
        self.ref[idx + (self.rows, slice(None))] = value


def _prep_kernel(has_vres, n_tiles, n_rw_tile, n_gla_tile, n_rw_par, n_gla_par, n_rw_out, *refs):
    rw_tiles = [refs[k * n_rw_tile:(k + 1) * n_rw_tile] for k in range(PREP_GROUP)]
    refs = refs[PREP_GROUP * n_rw_tile:]
    gla_tiles = [refs[k * n_gla_tile:(k + 1) * n_gla_tile] for k in range(PREP_GROUP)]
    refs = refs[PREP_GROUP * n_gla_tile:]
    rw_par, refs = refs[:n_rw_par], refs[n_rw_par:]
    gla_par, outs = refs[:n_gla_par], refs[n_gla_par:]
    n_base = n_rw_par - 3 if has_vres else n_rw_par
    for k in range(PREP_GROUP):
        t = _group_tile(k)(pl.program_id(1))
        views = [_TileRows(o, slice(k * TB, (k + 1) * TB)) for o in outs]
        tile = rw_tiles[k]
        rw_in = list(tile[:4]) + list(rw_par[:n_base])
        if has_vres:
            rw_in += [tile[4]] + list(rw_par[n_base:])
        _rwprep_kernel(has_vres, n_tiles, t, *rw_in, *views[:n_rw_out])
        _glaprep_kernel(n_tiles, t, *gla_tiles[k], *gla_par, *views[n_rw_out:])


def _prep(p_rw, p_gl, prm, gla_prm, v_first):
    B, N, _ = p_rw.shape
    nt = N // TB
    assert nt % PREP_GROUP == 0
    has_vres = v_first is not None
    args, specs = [], []
    for k in range(PREP_GROUP):
        a, s = _rwprep_tile(p_rw, v_first, k)
        args, specs, n_rw_tile = args + a, specs + s, len(a)
    for k in range(PREP_GROUP):
        a, s = _glaprep_tile(p_gl, k)
        args, specs, n_gla_tile = args + a, specs + s, len(a)
    rw_par, rw_par_specs = _rwprep_params(prm, has_vres)
    gla_par, gla_par_specs = _glaprep_params(gla_prm)
    rw_shapes, rw_out = _rwprep_outs(B, N)
    gla_shapes, gla_out = _glaprep_outs(B, N)
    outs = pl.pallas_call(
        functools.partial(_prep_kernel, has_vres, nt, n_rw_tile, n_gla_tile, len(rw_par), len(gla_par),
                          len(rw_shapes)),
        out_shape=rw_shapes + gla_shapes,
        grid=(B, nt // PREP_GROUP),
        in_specs=specs + rw_par_specs + gla_par_specs,
        out_specs=rw_out + gla_out,
        compiler_params=_cparams(("parallel", "parallel")),
        name="prep",
    )(*args, *rw_par, *gla_par)
    return outs[:len(rw_shapes)], outs[len(rw_shapes):]


def _readout_kernel(group, first, split_src, *refs):
    tok = [refs[i * group:(i + 1) * group] for i in range(7)]
    refs = refs[7 * group:]
    n_res = group + 1 if split_src else group
    res_refs, refs = refs[:n_res], refs[n_res:]
    mod_ref, gnw_ref, gnb_ref, ggn_ref, wout_ref, gpost_ref, bd_ref, xo_ref = refs
    bd = bd_ref[...]
    starts_with_ctx = jnp.logical_and(first == 0, pl.program_id(1) == 0)
    cats = []
    for k in range(group):
        yf_ref, yb_ref, bon_ref, g_ref, of_ref, ob_ref, og_ref = (t[k] for t in tok)
        y = yf_ref[0].astype(F32) + yb_ref[0].astype(F32)
        mu = _seg_sum(y, bd, exact=False) * (1.0 / RW_HEAD)
        yc = y - mu
        var = _seg_sum(yc * yc, bd, exact=False) * (1.0 / RW_HEAD)
        yn = yc * lax.rsqrt(var + RW_GN_EPS) * gnw_ref[...] + gnb_ref[...]
        rw = (yn + bon_ref[0].astype(F32)) * g_ref[0].astype(F32)
        o = of_ref[0].astype(F32) + ob_ref[0].astype(F32)
        og = og_ref[0].astype(F32)
        parts = [rw.astype(BF16)]
        for h in range(GLA_HEADS):
            sl = slice(GLA_DV * h, GLA_DV * (h + 1))
            oh = o[:, sl]
            on = oh * lax.rsqrt(jnp.mean(oh * oh, axis=-1, keepdims=True) + GLA_NORM_EPS)
            ogh = og[:, sl]
            parts.append((on * ggn_ref[:, sl] * (ogh * _sigmoid(ogh))).astype(BF16))
        cats.append(jnp.concatenate(parts, axis=1))
    mx_all = jnp.dot(jnp.concatenate(cats, axis=0), wout_ref[...], preferred_element_type=F32)
    for k in range(group):
        gate = mod_ref[0, 1, 2:3]
        if split_src:
            x_res = res_refs[k + 1][0]
            if k == 0:
                x_res = jnp.where(starts_with_ctx, res_refs[0][0], x_res)
        else:
            x_res = res_refs[k][0]
        if k == 0 and first == 0:
            gate = jnp.where(starts_with_ctx, mod_ref[0, 0, 2:3], gate)
        mx = mx_all[k * TB:(k + 1) * TB]
        xo_ref[0, k * TB:(k + 1) * TB, :] = x_res + gate * _rms(mx, gpost_ref[...])


def _readout(y, bonus, g, o, og, ctx, x, xc, modtab, prm, latents_only):
    B, N, _ = bonus.shape
    first = 1 if latents_only else 0
    n_tiles = N // TB - first
    group = max(k for k in (4, 3, 2, 1) if n_tiles % k == 0)

    def tiles(c, shift=0):
        return [pl.BlockSpec((1, TB, c), lambda b, j, k=k: (b, jnp.maximum(first + group * j + k + shift, 0), 0))
                for k in range(group)]

    if xc is None:
        res, res_specs = [ctx] + [x] * group, [pl.BlockSpec((1, TB, D), lambda b, j: (b, 0, 0))] + tiles(D, -1)
    else:
        res, res_specs = [xc] * group, tiles(D)
    tok_args, tok_specs = [], []
    for arr in (y[0], y[1], bonus, g, o[0], o[1], og):
        tok_args += [arr] * group
        tok_specs += tiles(arr.shape[-1])
    return pl.pallas_call(
        functools.partial(_readout_kernel, group, first, xc is None),
        out_shape=jax.ShapeDtypeStruct((B, n_tiles * TB, D), F32),
        grid=(B, n_tiles // group),
        in_specs=tok_specs + res_specs + [_wide_mod_spec(),
                  _const_spec((1, RW_WIDTH)), _const_spec((1, RW_WIDTH)), _const_spec((1, GLA_WIDTH)),
                  _const_spec((D, D)), _const_spec((1, D)), _const_spec((RW_WIDTH, RW_WIDTH))],
        out_specs=pl.BlockSpec((1, group * TB, D), lambda b, j: (b, j, 0)),
        compiler_params=_cparams(("parallel", "parallel")),
        name="readout",
    )(*tok_args, *res, modtab, prm["gn_w"], prm["gn_b"], prm["gla_gn_w"],
      prm["w_out"], prm["norm_post"], prm["bd64"])


def _swiglu_acc(hb, wg_ref, wu_ref, wd_ref):
    acc = jnp.zeros((hb.shape[0], D), F32)
    for c in range(D_FF // FCH):
        sl = slice(c * FCH, (c + 1) * FCH)
        gate = jnp.dot(hb, wg_ref[:, sl].astype(BF16), preferred_element_type=F32)
        up = jnp.dot(hb, wu_ref[:, sl].astype(BF16), preferred_element_type=F32)
        act = (gate * _sigmoid(gate) * up).astype(BF16)
        acc = acc + jnp.dot(act, wd_ref[sl, :].astype(BF16), preferred_element_type=F32)
    return acc


def _ffn_kernel(x_ref, mod_ref, gpre_ref, gpost_ref, wg_ref, wu_ref, wd_ref, xo_ref):
    x = x_ref[0]
    hb = (_rms(x, gpre_ref[...]) * (1.0 + _wide_mod(mod_ref, 4)) + _wide_mod(mod_ref, 3)).astype(BF16)
    fx = _swiglu_acc(hb, wg_ref, wu_ref, wd_ref)
    xo_ref[0] = x + _wide_mod(mod_ref, 5) * _rms(fx, gpost_ref[...])


def _single_buffered(shape):
    nd = len(shape)
    return pl.BlockSpec(shape, lambda *_: (0,) * nd, pipeline_mode=pl.Buffered(1))


def _ffn(xc, modtab, g_pre, g_post, wg, wu, wd):
    B, N, _ = xc.shape
    wide = pl.BlockSpec((1, TW, D), lambda b, j: (b, j, 0))
    return pl.pallas_call(
        _ffn_kernel,
        out_shape=jax.ShapeDtypeStruct((B, N, D), F32),
        grid=(B, N // TW),
        in_specs=[wide, _wide_mod_spec(), _const_spec((1, D)), _const_spec((1, D)),
                  _single_buffered((D, D_FF)), _single_buffered((D, D_FF)), _single_buffered((D_FF, D))],
        out_specs=wide,
        compiler_params=_cparams(("parallel", "parallel")),
        name="ffn",
    )(xc, modtab, g_pre, g_post, wg, wu, wd)


MOE_TB = 512
MOE_SEG = 16
MOE_TM = 512
MOE_R = 2 * MOE_TB + N_EXPERTS * MOE_SEG
MOE_NP = MOE_R // MOE_SEG


def _moe_route_kernel(x_ref, mod_ref, gpre_ref, router_ref, h_ref, info_ref, infot_ref, cnt_ref):
    mod = mod_ref[0, 0]
    h = _rms(x_ref[0], gpre_ref[...]) * (1.0 + mod[4:5]) + mod[3:4]
    hb = h.astype(BF16)
    h_ref[0] = hb
    lane = lax.broadcasted_iota(jnp.int32, (MOE_TB, LANES), 1)
    h_lo = (h - hb.astype(F32)).astype(BF16)
    logits = (jnp.dot(hb, router_ref[0], preferred_element_type=F32)
              + jnp.dot(hb, router_ref[1], preferred_element_type=F32)
              + jnp.dot(h_lo, router_ref[0], preferred_element_type=F32))
    logits = jnp.where(lane < N_EXPERTS, logits, -jnp.inf)
    v1 = jnp.max(logits, axis=-1, keepdims=True)
    i1 = jnp.min(jnp.where(logits == v1, lane, LANES), axis=-1, keepdims=True)
    rest = jnp.where(lane == i1, -jnp.inf, logits)
    v2 = jnp.max(rest, axis=-1, keepdims=True)
    i2 = jnp.min(jnp.where(rest == v2, lane, LANES), axis=-1, keepdims=True)
    ex = jnp.exp(v2 - v1)
    w1 = 1.0 / (1.0 + ex)
    w2 = ex * w1
    e1 = jnp.where(lane == i1, 1.0, 0.0)
    e2 = jnp.where(lane == i2, 1.0, 0.0)
    es = e1 + e2
    t = lax.broadcasted_iota(jnp.int32, (MOE_TB, MOE_TB), 0)
    s = lax.broadcasted_iota(jnp.int32, (MOE_TB, MOE_TB), 1)
    before = jnp.where(s < t, 1.0, 0.0).astype(BF16)
    rank = jnp.dot(before, es.astype(BF16), preferred_element_type=F32)
    cnt = jnp.sum(es, axis=0, keepdims=True)
    segs = jnp.floor((cnt + (MOE_SEG - 1)) * (1.0 / MOE_SEG))
    ea = lax.broadcasted_iota(jnp.int32, (LANES, LANES), 0)
    eb = lax.broadcasted_iota(jnp.int32, (LANES, LANES), 1)
    earlier = jnp.where(ea < eb, 1.0, 0.0).astype(BF16)
    start = jnp.dot(jnp.broadcast_to(segs, (SUBLANES, LANES)).astype(BF16), earlier,
                    preferred_element_type=F32)[0:1] * MOE_SEG
    pos = rank + start
    d1 = jnp.sum(e1 * pos, axis=-1, keepdims=True)
    d2 = jnp.sum(e2 * pos, axis=-1, keepdims=True)
    info = jnp.where(lane == 0, d1, jnp.where(lane == 1, d2, jnp.where(lane == 2, w1, jnp.where(lane == 3, w2, 0.0))))
    info_ref[0] = info
    infot_ref[0] = jnp.transpose(info)[0:SUBLANES]
    cnt_ref[0] = jnp.broadcast_to(cnt, (SUBLANES, LANES))


def _moe_route(xs, modtab, g_pre, router):
    B, S, _ = xs.shape
    per = S // MOE_TB
    nb = B * per
    blk = lambda c: pl.BlockSpec((1, MOE_TB, c), lambda i: (i // per, i % per, 0))
    flat = lambda r, c: pl.BlockSpec((1, r, c), lambda i: (i, 0, 0))
    return pl.pallas_call(
        _moe_route_kernel,
        out_shape=[jax.ShapeDtypeStruct((nb, MOE_TB, D), BF16), jax.ShapeDtypeStruct((nb, MOE_TB, LANES), F32),
                   jax.ShapeDtypeStruct((nb, SUBLANES, MOE_TB), F32),
                   jax.ShapeDtypeStruct((nb, SUBLANES, LANES), F32)],
        grid=(nb,),
        in_specs=[blk(D), pl.BlockSpec((1, 1, 6, D), lambda i: (i // per, 1, 0, 0)),
                  _const_spec((1, D)), _const_spec((2, D, LANES))],
        out_specs=[flat(MOE_TB, D), flat(MOE_TB, LANES), flat(SUBLANES, MOE_TB), flat(SUBLANES, LANES)],
        compiler_params=_cparams(("parallel",)),
        name="moe_route",
    )(xs, modtab, g_pre, router)


def _moe_plan(cnt, n_tiles):
    pc = (cnt + MOE_SEG - 1) // MOE_SEG * MOE_SEG
    inc = jnp.cumsum(pc, axis=1)
    loff = inc - pc
    reg = (jnp.sum(pc, axis=0) + MOE_TM - 1) // MOE_TM * MOE_TM
    gend = jnp.cumsum(reg)
    goff = (gend - reg)[None, :] + jnp.cumsum(pc, axis=0) - pc
    rows = jnp.arange(MOE_NP, dtype=jnp.int32) * MOE_SEG
    e_p = jnp.sum((rows[None, :, None] >= inc[:, None, :]).astype(jnp.int32), axis=-1)
    e_c = jnp.minimum(e_p, N_EXPERTS - 1)
    pick = (e_c[:, :, None] == jnp.arange(N_EXPERTS)[None, None, :]).astype(jnp.int32)
    dst = jnp.sum(pick * (goff - loff)[:, None, :], axis=-1) + rows[None, :]
    dst = jnp.where(e_p < N_EXPERTS, dst, 0).astype(jnp.int32)
    n_valid = (inc[:, -1] // MOE_SEG).astype(jnp.int32)
    trow = jnp.arange(n_tiles, dtype=jnp.int32) * MOE_TM
    te = jnp.sum((trow[:, None] >= gend[None, :]).astype(jnp.int32), axis=-1)
    valid = te < N_EXPERTS
    last = gend[-1] // MOE_TM - 1
    te = jnp.where(valid, te, te[last]).astype(jnp.int32)
    src = jnp.where(valid, jnp.arange(n_tiles, dtype=jnp.int32), last).astype(jnp.int32)
    fresh = valid & jnp.concatenate([jnp.ones((1,), bool), te[1:] != te[:-1]])
    return dst, n_valid, te, src, valid.astype(jnp.int32), fresh.astype(jnp.int32)


def _piece_copy(src_ref, src_row, dst_ref, dst_row, sem):
    return pltpu.make_async_copy(src_ref.at[pl.ds(src_row, MOE_SEG)], dst_ref.at[pl.ds(dst_row, MOE_SEG)], sem)


def _moe_gather_kernel(dst_ref, nv_ref, h_ref, infot_ref, xg_in_ref, xg_ref, buf_ref, sem_ref):
    del xg_in_ref
    i = pl.program_id(0)
    it = infot_ref[0]
    rr = lax.broadcasted_iota(jnp.int32, (MOE_R, MOE_TB), 0).astype(F32)
    onehot = jnp.where(rr == it[0:1], 1.0, jnp.where(rr == it[1:2], 1.0, 0.0)).astype(BF16)
    slot = i % 2
    buf_ref[slot] = jnp.dot(onehot, h_ref[0], preferred_element_type=F32).astype(BF16)

    def piece(blk, p):
        return _piece_copy(buf_ref.at[blk % 2], pl.multiple_of(p * MOE_SEG, MOE_SEG),
                           xg_ref, pl.multiple_of(dst_ref[blk, p], MOE_SEG), sem_ref.at[blk % 2, p])

    def start(p, c):
        piece(i, p).start()
        return c

    lax.fori_loop(0, nv_ref[i], start, 0)

    def wait_block(blk):
        def wait(p, c):
            piece(blk, p).wait()
            return c
        lax.fori_loop(0, nv_ref[blk], wait, 0)

    @pl.when(i > 0)
    def _():
        wait_block(i - 1)

    @pl.when(i == pl.num_programs(0) - 1)
    def _():
        wait_block(i)


def _moe_gather(dst, n_valid, h, info_t, n_rows):
    nb = h.shape[0]
    flat = lambda r, c: pl.BlockSpec((1, r, c), lambda i, *_: (i, 0, 0))
    return pl.pallas_call(
        _moe_gather_kernel,
        out_shape=jax.ShapeDtypeStruct((n_rows, D), BF16),
        grid_spec=pltpu.PrefetchScalarGridSpec(
            num_scalar_prefetch=2, grid=(nb,),
            in_specs=[flat(MOE_TB, D), flat(SUBLANES, MOE_TB), pl.BlockSpec(memory_space=pl.ANY)],
            out_specs=pl.BlockSpec(memory_space=pl.ANY),
            scratch_shapes=[pltpu.VMEM((2, MOE_R, D), BF16), pltpu.SemaphoreType.DMA((2, MOE_NP))]),
        input_output_aliases={4: 0},
        compiler_params=_cparams(("arbitrary",)),
        name="moe_gather",
    )(dst, n_valid, h, info_t, jnp.zeros((n_rows, D), BF16))


N_FCH = D_FF // FCH


def _moe_ffn_kernel(te_ref, src_ref, valid_ref, fresh_ref, x_ref, wg_hbm, wu_hbm, wd_hbm, y_ref,
                    wg_ref, wu_ref, wd_ref, sem_ref):
    i = pl.program_id(0)
    e = te_ref[i]
    fresh = fresh_ref[i] == 1

    def slice_copies(c):
        cols = pl.ds(c * FCH, FCH)
        return (pltpu.make_async_copy(wg_hbm.at[e, :, cols], wg_ref.at[:, cols], sem_ref.at[0, c]),
                pltpu.make_async_copy(wu_hbm.at[e, :, cols], wu_ref.at[:, cols], sem_ref.at[1, c]),
                pltpu.make_async_copy(wd_hbm.at[e, cols, :], wd_ref.at[cols, :], sem_ref.at[2, c]))

    @pl.when(fresh)
    def _():
        for c in range(N_FCH):
            for cp in slice_copies(c):
                cp.start()
        hb = x_ref[...]
        acc = jnp.zeros((MOE_TM, D), F32)
        for c in range(N_FCH):
            for cp in slice_copies(c):
                cp.wait()
            sl = slice(c * FCH, (c + 1) * FCH)
            gate = jnp.dot(hb, wg_ref[:, sl].astype(BF16), preferred_element_type=F32)
            up = jnp.dot(hb, wu_ref[:, sl].astype(BF16), preferred_element_type=F32)
            act = (gate * _sigmoid(gate) * up).astype(BF16)
            acc = acc + jnp.dot(act, wd_ref[sl, :].astype(BF16), preferred_element_type=F32)
        y_ref[...] = acc.astype(BF16)

    @pl.when(jnp.logical_and(valid_ref[i] == 1, jnp.logical_not(fresh)))
    def _():
        y_ref[...] = _swiglu_acc(x_ref[...], wg_ref, wu_ref, wd_ref).astype(BF16)

    @pl.when(valid_ref[i] == 0)
    def _():
        y_ref[...] = jnp.zeros_like(y_ref)


def _moe_ffn(te, src, valid, fresh, xg, wg, wu, wd):
    n_rows = xg.shape[0]
    hbm = pl.BlockSpec(memory_space=pl.ANY)
    return pl.pallas_call(
        _moe_ffn_kernel,
        out_shape=jax.ShapeDtypeStruct((n_rows, D), BF16),
        grid_spec=pltpu.PrefetchScalarGridSpec(
            num_scalar_prefetch=4, grid=(n_rows // MOE_TM,),
            in_specs=[pl.BlockSpec((MOE_TM, D), lambda i, te, src, *_: (src[i], 0)), hbm, hbm, hbm],
            out_specs=pl.BlockSpec((MOE_TM, D), lambda i, *_: (i, 0)),
            scratch_shapes=[pltpu.VMEM((D, D_FF), F32), pltpu.VMEM((D, D_FF), F32), pltpu.VMEM((D_FF, D), F32),
                            pltpu.SemaphoreType.DMA((3, N_FCH))]),
        compiler_params=_cparams(("arbitrary",)),
        name="moe_ffn",
    )(te, src, valid, fresh, xg, wg, wu, wd)


def _moe_combine_kernel(dst_ref, nv_ref, x_ref, mod_ref, gpost_ref, info_ref, yg_ref, xo_ref, buf_ref, sem_ref):
    i = pl.program_id(0)

    def piece(blk, p):
        return _piece_copy(yg_ref, pl.multiple_of(dst_ref[blk, p], MOE_SEG),
                           buf_ref.at[blk % 2], pl.multiple_of(p * MOE_SEG, MOE_SEG), sem_ref.at[blk % 2, p])

    def fetch(blk):
        def start(p, c):
            piece(blk, p).start()
            return c

        def clear(p, c):
            buf_ref[blk % 2, pl.ds(pl.multiple_of(p * MOE_SEG, MOE_SEG), MOE_SEG), :] = jnp.zeros((MOE_SEG, D), BF16)
            return c

        lax.fori_loop(0, nv_ref[blk], start, 0)
        lax.fori_loop(nv_ref[blk], MOE_NP, clear, 0)

    @pl.when(i == 0)
    def _():
        fetch(i)

    @pl.when(i + 1 < pl.num_programs(0))
    def _():
        fetch(i + 1)

    info = info_ref[0]
    rr = lax.broadcasted_iota(jnp.int32, (MOE_TB, MOE_R), 1).astype(F32)
    comb = jnp.where(rr == info[:, 0:1], info[:, 2:3], jnp.where(rr == info[:, 1:2], info[:, 3:4], 0.0)).astype(BF16)

    def wait(p, c):
        piece(i, p).wait()
        return c

    lax.fori_loop(0, nv_ref[i], wait, 0)
    fx = jnp.dot(comb, buf_ref[i % 2], preferred_element_type=F32)
    xo_ref[0] = x_ref[0] + mod_ref[0, 0][5:6] * _rms(fx, gpost_ref[...])


def _moe_combine(dst, n_valid, xs, modtab, g_post, info, yg):
    B, S, _ = xs.shape
    per = S // MOE_TB
    blk = pl.BlockSpec((1, MOE_TB, D), lambda i, *_: (i // per, i % per, 0))
    return pl.pallas_call(
        _moe_combine_kernel,
        out_shape=jax.ShapeDtypeStruct((B, S, D), F32),
        grid_spec=pltpu.PrefetchScalarGridSpec(
            num_scalar_prefetch=2, grid=(B * per,),
            in_specs=[blk, pl.BlockSpec((1, 1, 6, D), lambda i, *_: (i // per, 1, 0, 0)),
                      pl.BlockSpec((1, D), lambda i, *_: (0, 0)),
                      pl.BlockSpec((1, MOE_TB, LANES), lambda i, *_: (i, 0, 0)),
                      pl.BlockSpec(memory_space=pl.ANY)],
            out_specs=blk,
            scratch_shapes=[pltpu.VMEM((2, MOE_R, D), BF16), pltpu.SemaphoreType.DMA((2, MOE_NP))]),
        compiler_params=_cparams(("arbitrary",)),
        name="moe_combine",
    )(dst, n_valid, xs, modtab, g_post, info, yg)


def _moe(xs, modtab, g_pre, g_post, router, wg, wu, wd):
    B, S, _ = xs.shape
    nb = B * S // MOE_TB
    h, info, info_t, cnt = _moe_route(xs, modtab, g_pre, router)
    worst = 2 * B * S + nb * N_EXPERTS * (MOE_SEG - 1) + N_EXPERTS * (MOE_TM - 1)
    n_tiles = -(-worst // MOE_TM)
    dst, n_valid, te, src, valid, fresh = _moe_plan(cnt[:, 0, :N_EXPERTS].astype(jnp.int32), n_tiles)
    xg = _moe_gather(dst, n_valid, h, info_t, n_tiles * MOE_TM)
    yg = _moe_ffn(te, src, valid, fresh, xg, wg, wu, wd)
    return _moe_combine(dst, n_valid, xs, modtab, g_post, info, yg)


def _block_diag2(w):
    z = jnp.zeros_like(w[0])
    return jnp.concatenate([jnp.concatenate([w[0], z], axis=1), jnp.concatenate([z, w[1]], axis=1)], axis=0)


def _row(v):
    return v.reshape(1, -1).astype(F32)


def _head_ones(width, head):
    i = jnp.arange(width) // head
    return (i[:, None] == i[None, :]).astype(BF16)


def kernel(x, c, ctx, c_ctx, ada_w, ada_b, norm_mix_pre, norm_mix_post, norm_ffn_pre, norm_ffn_post, w_in, shift_mu, rw_w_up, rw_w0, rw_a_up, rw_a0, rw_k_k, rw_k_a, rw_r_k, rw_g_up, rw_gn_w, rw_gn_b, rw_v_down, rw_v_up, rw_v0, gla_conv, gla_a_up, gla_a_b, gla_gn_w, w_out, ffn_w_gate, ffn_w_up, ffn_w_down, moe_router, moe_w_gate, moe_w_up, moe_w_down):
    B, S, _ = x.shape
    n_ctx = ctx.shape[1]
    depth = w_in.shape[0]
    assert n_ctx == TB and S % MOE_TB == 0 and (n_ctx + S) % TW == 0 and depth == 2

    xc = None
    pad_rows = -(B + 1) % SUBLANES
    cvec = jnp.concatenate([c, c_ctx[None, :], jnp.zeros((pad_rows, D), F32)], axis=0)
    bd64 = _head_ones(RW_WIDTH, RW_HEAD)
    ada_b3 = ada_b.reshape(depth, 1, 6 * D)
    v_first = None
    out = None
    for i in range(depth):
        last = i == depth - 1
        mods = _adaln(cvec, ada_w, ada_b3, i)
        mod_x = mods[:B].reshape(B, 6, D)
        mod_c = jnp.broadcast_to(mods[B].reshape(1, 6, D), (B, 6, D))
        modtab = jnp.stack([mod_c, mod_x], axis=1)

        p_rw, p_gl = _inproj(ctx, x, xc, _row(norm_mix_pre[i]), modtab, w_in, i)

        prm = dict(
            mu=_row(shift_mu[i]),
            w_up=_block_diag2(rw_w_up[i]).astype(BF16), w0=_row(rw_w0[i]),
            a_up=_block_diag2(rw_a_up[i]).astype(BF16), a0=_row(rw_a0[i]),
            k_k=_row(rw_k_k[i]), k_a=_row(rw_k_a[i]), r_k=_row(rw_r_k[i]),
            g_up=rw_g_up[i].astype(BF16), bd64=bd64,
            gn_w=_row(rw_gn_w[i]), gn_b=_row(rw_gn_b[i]), gla_gn_w=_row(gla_gn_w[i]),
            w_out=w_out[i].astype(BF16), norm_post=_row(norm_mix_post[i]),
        )
        gate_pad = jnp.zeros((LANES - 2 * GLA_GATE_RANK, 2 * GLA_KW), F32)
        gla_prm = dict(conv=gla_conv[i].astype(F32), a_b=_row(gla_a_b[i]),
                       a_up=jnp.concatenate([_block_diag2(gla_a_up[i]), gate_pad], axis=0).astype(BF16))
        if i > 0:
            pad = LANES - RW_V_RANK
            prm["v_down"] = jnp.concatenate([rw_v_down[i - 1], jnp.zeros((RW_WIDTH, pad), F32)], axis=1).astype(BF16)
            prm["v_up"] = jnp.concatenate([rw_v_up[i - 1], jnp.zeros((pad, RW_WIDTH), F32)], axis=0).astype(BF16)
            prm["v0"] = _row(rw_v0[i - 1])

        (r, kk, vm, g, bonus, lw, bb, ke), (q, k, gv, og, lg) = _prep(p_rw, p_gl, prm, gla_prm,
                                                                     v_first if i > 0 else None)
        if i == 0:
            v_first = vm
        y, o = _scans(r, kk, vm, lw, bb, ke, q, k, gv, lg)
        xc = _readout(y, bonus, g, o, og, ctx, x, xc, modtab, prm, latents_only=last)

        jf = i // 2
        if i % 2 == 0:
            xc = _ffn(xc, modtab, _row(norm_ffn_pre[i]), _row(norm_ffn_post[i]),
                      ffn_w_gate[jf], ffn_w_up[jf], ffn_w_down[jf])
        else:
            router = jnp.concatenate([moe_router[jf], jnp.zeros((D, LANES - N_EXPERTS), F32)], axis=1)
            r_hi = router.astype(BF16)
            router = jnp.stack([r_hi, (router - r_hi.astype(F32)).astype(BF16)])
            out = _moe(xc, modtab, _row(norm_ffn_pre[i]), _row(norm_ffn_post[i]), router,
                       moe_w_gate[jf], moe_w_up[jf], moe_w_down[jf])
    return out
```

```python
import functools
import math

import jax
import jax.numpy as jnp
from jax import lax
from jax.experimental import pallas as pl
from jax.experimental.pallas import tpu as pltpu

F32, BF16 = jnp.float32, jnp.bfloat16
ACT = BF16

D = 1024
GRID_W = 64
RW_WIDTH = 512
RW_HEAD = 64
RW_RANK = 64
RW_G_RANK = 128
RW_V_RANK = 32
RW_GN_EPS = 64e-5
GLA_WIDTH = 512
GLA_HEADS = 4
GLA_DV = 128
GLA_DK = 64
GLA_KW = 256
GLA_GATE_RANK = 16
GLA_GATE_NORM = 16.0
GLA_NORM_EPS = 1e-5
D_FF = 2816
N_EXPERTS = 8
NORM_EPS = 1e-6
RW_COLS = 3 * RW_WIDTH + 4 * RW_RANK + RW_G_RANK
GLA_QKV = 2 * GLA_KW + GLA_WIDTH
GLA_COLS = GLA_QKV + GLA_WIDTH + 2 * GLA_GATE_RANK

LANES = 128
SUBLANES = 8
GLA_PAD = -(-GLA_COLS // LANES) * LANES
TB = 256
TW = 3 * TB
CH = 64
FCH = 256
HALO = 16
VMEM_LIMIT = 56 * 1024 * 1024


def _cparams(sem):
    return pltpu.CompilerParams(dimension_semantics=sem, vmem_limit_bytes=VMEM_LIMIT)


def _bdot(a, b):
    return jnp.dot(a.astype(BF16), b.astype(BF16), preferred_element_type=F32)


def _bdot_nt(a, b):
    return lax.dot_general(a.astype(BF16), b.astype(BF16), (((1,), (1,)), ((), ())),
                           preferred_element_type=F32)


def _bdot_tn(a, b):
    return lax.dot_general(a.astype(BF16), b.astype(BF16), (((0,), (0,)), ((), ())),
                           preferred_element_type=F32)


def _split_dot(a_exact, x):
    h1 = x.astype(BF16)
    r1 = x - h1.astype(F32)
    h2 = r1.astype(BF16)
    h3 = (r1 - h2.astype(F32)).astype(BF16)
    return (jnp.dot(a_exact, h1, preferred_element_type=F32)
            + jnp.dot(a_exact, h2, preferred_element_type=F32)
            + jnp.dot(a_exact, h3, preferred_element_type=F32))


def _seg_sum(x, bd, exact=True):
    hi = x.astype(BF16)
    out = jnp.dot(hi, bd, preferred_element_type=F32)
    if exact:
        lo = (x - hi.astype(F32)).astype(BF16)
        out = out + jnp.dot(lo, bd, preferred_element_type=F32)
    return out


def _sigmoid(x):
    return jax.nn.sigmoid(x)


def _rms(x, g):
    return x * lax.rsqrt(jnp.mean(x * x, axis=-1, keepdims=True) + NORM_EPS) * g


def _pair_stack(z, lo):
    return jnp.concatenate([jnp.where(lo, z, 0.0), jnp.where(lo, 0.0, z)], axis=0)


def _adaln_kernel(c_ref, w_ref, b_ref, o_ref):
    c = c_ref[...]
    s = c * _sigmoid(c)
    w = w_ref[0]
    s_hi, w_hi = s.astype(BF16), w.astype(BF16)
    s_lo, w_lo = (s - s_hi.astype(F32)).astype(BF16), (w - w_hi.astype(F32)).astype(BF16)
    o_ref[...] = (jnp.dot(s_hi, w_hi, preferred_element_type=F32) + jnp.dot(s_hi, w_lo, preferred_element_type=F32)
                  + jnp.dot(s_lo, w_hi, preferred_element_type=F32)) + b_ref[0]


def _adaln(cvec, w, b, layer):
    rows = cvec.shape[0]
    n = w.shape[2]
    return pl.pallas_call(
        _adaln_kernel,
        out_shape=jax.ShapeDtypeStruct((rows, n), F32),
        grid=(n // D,),
        in_specs=[pl.BlockSpec((rows, D), lambda i: (0, 0)),
                  pl.BlockSpec((1, D, D), lambda i: (layer, 0, i)),
                  pl.BlockSpec((1, 1, D), lambda i: (layer, 0, i))],
        out_specs=pl.BlockSpec((rows, D), lambda i: (0, i)),
        compiler_params=_cparams(("arbitrary",)),
        name="adaln",
    )(cvec, w, b)


def _wide_mod(mod_ref, k):
    is_ctx = jnp.logical_and(pl.program_id(1) == 0, lax.broadcasted_iota(jnp.int32, (TW, 1), 0) < TB)
    return jnp.where(is_ctx, mod_ref[0, 0, k:k + 1], mod_ref[0, 1, k:k + 1])


def _wide_rows(refs):
    if len(refs) == 1:
        return refs[0][0]
    first = jnp.where(pl.program_id(1) == 0, refs[0][0], refs[1][0])
    return jnp.concatenate([first] + [r[0] for r in refs[2:]], axis=0)


def _inproj_kernel(n_src, *refs):
    g_ref, mod_ref, w_ref, prw_ref, pgl_ref = refs[n_src:]
    x = _wide_rows(refs[:n_src])
    h = _rms(x, g_ref[...]) * (1.0 + _wide_mod(mod_ref, 1)) + _wide_mod(mod_ref, 0)
    hb = h.astype(BF16)
    prw_ref[0] = jnp.dot(hb, w_ref[0, :, :RW_COLS].astype(BF16), preferred_element_type=F32).astype(ACT)
    pgl_ref[0, :, :GLA_COLS] = jnp.dot(hb, w_ref[0, :, RW_COLS:].astype(BF16),
                                       preferred_element_type=F32).astype(ACT)
    pgl_ref[0, :, GLA_COLS:] = jnp.zeros((TW, GLA_PAD - GLA_COLS), ACT)


def _wide_src(ctx, x, xc):
    if xc is not None:
        return [xc], [pl.BlockSpec((1, TW, D), lambda b, j: (b, j, 0))]
    per = TW // TB
    specs = [pl.BlockSpec((1, TB, D), lambda b, j: (b, 0, 0))]
    for k in range(per):
        specs.append(pl.BlockSpec((1, TB, D), lambda b, j, k=k: (b, jnp.maximum(j * per + k - 1, 0), 0)))
    return [ctx] + [x] * per, specs


def _wide_mod_spec():
    return pl.BlockSpec((1, 2, 6, D), lambda b, j: (b, 0, 0, 0))


def _const_spec(shape):
    nd = len(shape)
    return pl.BlockSpec(shape, lambda *_: (0,) * nd)


def _inproj(ctx, x, xc, g, modtab, w, layer):
    B = modtab.shape[0]
    N = xc.shape[1] if xc is not None else ctx.shape[1] + x.shape[1]
    srcs, src_specs = _wide_src(ctx, x, xc)
    wide = lambda c: pl.BlockSpec((1, TW, c), lambda b, j: (b, j, 0))
    return pl.pallas_call(
        functools.partial(_inproj_kernel, len(srcs)),
        out_shape=[jax.ShapeDtypeStruct((B, N, RW_COLS), ACT),
                   jax.ShapeDtypeStruct((B, N, GLA_PAD), ACT)],
        grid=(B, N // TW),
        in_specs=src_specs + [_const_spec((1, D)), _wide_mod_spec(),
                              pl.BlockSpec((1, D, RW_COLS + GLA_COLS), lambda b, j: (layer, 0, 0),
                                           pipeline_mode=pl.Buffered(1))],
        out_specs=[wide(RW_COLS), wide(GLA_PAD)],
        compiler_params=_cparams(("parallel", "parallel")),
        name="inproj",
    )(*srcs, g, modtab, w)


def _shift_table():
    t = jnp.arange(TB)
    prev = t[:, None] - 1 == t[None, :]
    nxt = t[:, None] + 1 == t[None, :]
    col = (t % GRID_W)[:, None]
    return jnp.stack([jnp.stack([prev, nxt]),
                      jnp.stack([prev & (col != 0), nxt & (col != GRID_W - 1)])]).astype(BF16)


def _rwprep_kernel(has_vres, n_tiles, j, p_ref, hu_ref, hd_ref, sh_ref, mu_ref, wup_ref, w0_ref, aup_ref,
                   a0_ref, kk_ref, ka_ref, rk_ref, gup_ref, bd_ref, *rest):
    if has_vres:
        vf_ref, vdn_ref, vup_ref, v0_ref = rest[:4]
        rest = rest[4:]
    r_out, kk_out, v_out, g_out, bon_out, lw_out, b_out, ke_out = rest
    pb = p_ref[0]
    p = pb.astype(F32)
    prev1 = jnp.dot(sh_ref[0, 0], pb, preferred_element_type=F32)
    next1 = jnp.dot(sh_ref[0, 1], pb, preferred_element_type=F32)
    has_upper = jnp.where(j == 1, 0.0, 1.0)
    has_lower = jnp.where(j == n_tiles - 1, 0.0, 1.0)
    up = jnp.concatenate([hu_ref[0].astype(F32) * has_upper, p[:TB - GRID_W]], axis=0)
    down = jnp.concatenate([p[GRID_W:], hd_ref[0].astype(F32) * has_lower], axis=0)
    cls = lax.broadcasted_iota(jnp.int32, p.shape, 1) & jnp.where(j == 0, 1, 3)
    shifted = jnp.where(cls == 0, prev1, jnp.where(cls == 1, next1, jnp.where(cls == 2, up, down)))
    u = p + mu_ref[...] * (shifted - p)

    r = u[:, 0:RW_WIDTH]
    k = u[:, RW_WIDTH:2 * RW_WIDTH]
    v = u[:, 2 * RW_WIDTH:3 * RW_WIDTH]
    o = 3 * RW_WIDTH
    wd = u[:, o:o + 2 * RW_RANK]
    ad = u[:, o + 2 * RW_RANK:o + 4 * RW_RANK]
    gd = u[:, o + 4 * RW_RANK:]

    w_logit = w0_ref[...] + _bdot(jnp.tanh(wd), wup_ref[...])
    lw = -math.exp(-0.5) * _sigmoid(w_logit)
    a = _sigmoid(a0_ref[...] + _bdot(ad, aup_ref[...]))
    bd = bd_ref[...]
    kk = k * kk_ref[...]
    kk = kk * lax.rsqrt(jnp.maximum(_seg_sum(kk * kk, bd), 1e-24))
    g = _bdot(_sigmoid(gd), gup_ref[...])
    if has_vres:
        gate = _sigmoid(v0_ref[...] + _bdot(_bdot(v, vdn_ref[...]), vup_ref[...]))
        vm = v + (vf_ref[0].astype(F32) - v) * gate
    else:
        vm = v
    ke_sum = jnp.zeros_like(k)
    for d in range(2):
        a_d = a[:, d * RW_WIDTH:(d + 1) * RW_WIDTH]
        ke_d = k * (1.0 + (a_d - 1.0) * ka_ref[...])
        lw_out[d, 0] = lw[:, d * RW_WIDTH:(d + 1) * RW_WIDTH]
        b_out[d, 0] = (kk * a_d).astype(ACT)
        ke_out[d, 0] = ke_d.astype(ACT)
        ke_sum = ke_sum + ke_d
    r_out[0] = r.astype(ACT)
    kk_out[0] = kk.astype(ACT)
    v_out[0] = vm.astype(ACT)
    g_out[0] = g.astype(ACT)
    bon_out[0] = (_seg_sum(r * ke_sum * rk_ref[...], bd, exact=False) * vm).astype(ACT)


PREP_GROUP = 3


def _group_tile(k):
    return lambda j: PREP_GROUP * j + k


def _rwprep_tile(p_rw, v_first, k):
    _, N, _ = p_rw.shape
    hb = TB // GRID_W
    nhb = N // GRID_W
    t = _group_tile(k)
    specs = [pl.BlockSpec((1, TB, RW_COLS), lambda b, j: (b, t(j), 0)),
             pl.BlockSpec((1, GRID_W, RW_COLS), lambda b, j: (b, jnp.maximum(t(j) * hb - 1, 0), 0)),
             pl.BlockSpec((1, GRID_W, RW_COLS), lambda b, j: (b, jnp.minimum(t(j) * hb + hb, nhb - 1), 0)),
             pl.BlockSpec((1, 2, TB, TB), lambda b, j: (jnp.minimum(t(j), 1), 0, 0, 0))]
    args = [p_rw, p_rw, p_rw, _shift_table()]
    if v_first is not None:
        specs.append(pl.BlockSpec((1, TB, RW_WIDTH), lambda b, j: (b, t(j), 0)))
        args.append(v_first)
    return args, specs


def _rwprep_params(prm, has_vres):
    W2 = 2 * RW_WIDTH
    specs = [_const_spec((1, RW_COLS)), _const_spec((2 * RW_RANK, W2)), _const_spec((1, W2)),
             _const_spec((2 * RW_RANK, W2)), _const_spec((1, W2)),
             _const_spec((1, RW_WIDTH)), _const_spec((1, RW_WIDTH)), _const_spec((1, RW_WIDTH)),
             _const_spec((RW_G_RANK, RW_WIDTH)), _const_spec((RW_WIDTH, RW_WIDTH))]
    args = [prm["mu"], prm["w_up"], prm["w0"], prm["a_up"], prm["a0"],
            prm["k_k"], prm["k_a"], prm["r_k"], prm["g_up"], prm["bd64"]]
    if has_vres:
        specs += [_const_spec((RW_WIDTH, LANES)), _const_spec((LANES, RW_WIDTH)), _const_spec((1, RW_WIDTH))]
        args += [prm["v_down"], prm["v_up"], prm["v0"]]
    return args, specs


def _group_out(c):
    return pl.BlockSpec((1, PREP_GROUP * TB, c), lambda b, j: (b, j, 0))


def _group_dir_out(c):
    return pl.BlockSpec((2, 1, PREP_GROUP * TB, c), lambda b, j: (0, b, j, 0))


def _rwprep_outs(B, N):
    tok = jax.ShapeDtypeStruct((B, N, RW_WIDTH), ACT)
    dtok = jax.ShapeDtypeStruct((2, B, N, RW_WIDTH), ACT)
    shapes = [tok] * 5 + [jax.ShapeDtypeStruct((2, B, N, RW_WIDTH), F32), dtok, dtok]
    return shapes, [_group_out(RW_WIDTH)] * 5 + [_group_dir_out(RW_WIDTH)] * 3


NCH = TB // CH


def _scan_kernel(rf_ref, kkf_ref, vf_ref, lwf_ref, bf_ref, kef_ref,
                 rb_ref, kkb_ref, vb_ref, lwb_ref, bb_ref, keb_ref,
                 gqf_ref, gkf_ref, gvf_ref, lgf_ref, gqb_ref, gkb_ref, gvb_ref, lgb_ref,
                 yf_ref, yb_ref, of_ref, ob_ref, st_ref, gst_ref):
    j = pl.program_id(1)

    @pl.when(j == 0)
    def _():
        st_ref[...] = jnp.zeros_like(st_ref)
        gst_ref[...] = jnp.zeros_like(gst_ref)

    ti = lax.broadcasted_iota(jnp.int32, (CH, LANES), 0)
    li = lax.broadcasted_iota(jnp.int32, (CH, LANES), 1)
    si = li & (CH - 1)
    lo = li < CH
    eye = jnp.where(ti == si, 1.0, 0.0)
    rblk = lax.broadcasted_iota(jnp.int32, (LANES, LANES), 0) // CH
    cblk = lax.broadcasted_iota(jnp.int32, (LANES, LANES), 1) // CH
    bdmask = rblk == cblk
    t64 = lax.broadcasted_iota(jnp.int32, (CH, CH), 0)
    s64 = lax.broadcasted_iota(jnp.int32, (CH, CH), 1)
    tri = [jnp.where(s64 <= t64, 1.0, 0.0).astype(BF16), jnp.where(s64 >= t64, 1.0, 0.0).astype(BF16)]
    m_incl = [si <= ti, si >= ti]
    m_strict = [si < ti, si > ti]
    n_pairs = RW_WIDTH // LANES
    in_refs = ((rf_ref, kkf_ref, vf_ref, lwf_ref, bf_ref, kef_ref),
               (rb_ref, kkb_ref, vb_ref, lwb_ref, bb_ref, keb_ref))
    y_refs = (yf_ref, yb_ref)

    def stack(z):
        return _pair_stack(z, lo).astype(BF16)

    st = {(d, p): st_ref[d, p] for d in range(2) for p in range(n_pairs)}

    def scan_step(step):
        cur = {}
        w_tot = {}
        for d in range(2):
            c = (NCH - 1 - step) if d == 1 else step
            rows = slice(c * CH, (c + 1) * CH)
            refs = in_refs[d]
            r, kk, v = (refs[i][0, rows, :].astype(F32) for i in range(3))
            lw = refs[3][0, 0, rows, :]
            b, ke = refs[4][0, 0, rows, :].astype(F32), refs[5][0, 0, rows, :].astype(F32)
            cum = _split_dot(tri[d], lw)
            tot = jnp.sum(lw, axis=0, keepdims=True)
            w_inv = jnp.exp(-cum)
            w_end = jnp.exp(tot - cum)
            w_tot[d] = jnp.exp(tot)
            rh = r * jnp.exp(cum)
            ah = -(kk * jnp.exp(cum - lw))
            bh, kh, bt, kt = b * w_inv, ke * w_inv, b * w_end, ke * w_end
            for p in range(n_pairs):
                sl = slice(LANES * p, LANES * (p + 1))
                cur[d, p] = dict(ah=ah[:, sl], rh=rh[:, sl], bh=bh[:, sl], kh=kh[:, sl], bt=bt[:, sl],
                                 kt=kt[:, sl], v=v[:, sl], rows=rows, sl=sl)
        yield
        for it in cur.values():
            it["lhs"] = jnp.concatenate([it["ah"], it["rh"]], axis=0).astype(BF16)
            rhs = jnp.concatenate([stack(it["bh"]), stack(it["kh"])], axis=0)
            it["a_all"] = _bdot_nt(it["lhs"], rhs)
        yield
        for (d, _), it in cur.items():
            a_all = it.pop("a_all")
            it["a_ab"] = jnp.where(m_strict[d], a_all[:CH, :LANES], 0.0)
            a_ak = jnp.where(m_strict[d], a_all[:CH, LANES:], 0.0)
            a_rb = jnp.where(m_incl[d], a_all[CH:, :LANES], 0.0)
            a_rk = jnp.where(m_incl[d], a_all[CH:, LANES:], 0.0)
            it["a_r"] = jnp.concatenate([a_rb, a_rk], axis=1).astype(BF16)
            it["v_bd"] = stack(it["v"])
            it["akv"] = _bdot(a_ak, it["v_bd"])
            it["t"] = eye + it["a_ab"]
            it["m"] = _bdot(it["a_ab"], stack(it["a_ab"]))
        yield
        for _ in range(int(math.log2(CH)) - 2):
            for it in cur.values():
                z = _bdot(it["m"], jnp.concatenate([stack(it["m"]), stack(it["t"])], axis=1))
                it["m"] = z[:, :LANES]
                it["t"] = it["t"] + z[:, LANES:]
            yield
        for it in cur.values():
            it["t"] = (it["t"] + _bdot(it["m"], stack(it["t"]))).astype(BF16)
        yield
        for it in cur.values():
            ta = jnp.dot(it["t"], jnp.concatenate([stack(it["ah"]), stack(it["akv"])], axis=1),
                         preferred_element_type=F32)
            it["lhs2"] = jnp.concatenate([ta[:, :LANES].astype(BF16), it["lhs"][CH:]], axis=0)
            it["cc"] = ta[:, LANES:]
            it["rhs_t"] = jnp.concatenate([it["bt"], it["kt"]], axis=0).astype(BF16)
        yield
        z1 = {k: _bdot_nt(it["lhs2"], st[k]) for k, it in cur.items()}
        yield
        u = {k: z1[k][:CH] + it["cc"] for k, it in cur.items()}
        upd = {k: _bdot_tn(jnp.concatenate([u[k], it["v"]], axis=0), it["rhs_t"]) for k, it in cur.items()}
        yield
        for (d, p), it in cur.items():
            y = z1[d, p][CH:] + jnp.dot(it["a_r"], jnp.concatenate([stack(u[d, p]), it["v_bd"]], axis=0),
                                        preferred_element_type=F32)
            y_refs[d][0, it["rows"], it["sl"]] = y.astype(ACT)
            st[d, p] = st[d, p] * w_tot[d][:, it["sl"]] + jnp.where(bdmask, upd[d, p], 0.0)
        yield

    gla_refs = ((gqf_ref, gkf_ref, gvf_ref, lgf_ref), (gqb_ref, gkb_ref, gvb_ref, lgb_ref))
    o_refs = (of_ref, ob_ref)
    g_pairs = GLA_KW // LANES
    gst = {(d, p): gst_ref[d, p] for d in range(2) for p in range(g_pairs)}
    lo2 = lax.broadcasted_iota(jnp.int32, (LANES, LANES), 1) < CH
    zeros_v = jnp.zeros((CH, GLA_DV), F32)

    def gla_step(step):
        cur = {}
        dec = {}
        for d in range(2):
            c = (NCH - 1 - step) if d == 1 else step
            rows = slice(c * CH, (c + 1) * CH)
            refs = gla_refs[d]
            q, k, v = (refs[i][0, rows, :].astype(F32) for i in range(3))
            lg = refs[3][0, 0, rows, :]
            cum = _split_dot(tri[d], lg)
            tot = jnp.sum(lg, axis=0, keepdims=True)
            dec[d] = jnp.exp(tot)
            qd = q * jnp.exp(cum)
            ki = k * jnp.exp(-cum)
            kend = k * jnp.exp(tot - cum)
            for p in range(g_pairs):
                sl = slice(LANES * p, LANES * (p + 1))
                cur[d, p] = dict(qd=qd[:, sl].astype(BF16), ki=ki[:, sl], kend=kend[:, sl], rows=rows, sl=sl,
                                 v0=v[:, 2 * LANES * p:2 * LANES * p + LANES],
                                 v1=v[:, 2 * LANES * p + LANES:2 * LANES * (p + 1)])
        yield
        for it in cur.values():
            it["att"] = _bdot_nt(it["qd"], _pair_stack(it["ki"], lo))
        yield
        for (d, _), it in cur.items():
            att = jnp.where(m_incl[d], it["att"], 0.0)
            v_bd = jnp.concatenate([jnp.concatenate([it["v0"], zeros_v], axis=1),
                                    jnp.concatenate([zeros_v, it["v1"]], axis=1)], axis=0)
            it["o"] = _bdot(att, v_bd)
            it["upd"] = _bdot_tn(jnp.concatenate([it["v0"], it["v1"]], axis=0), _pair_stack(it["kend"], lo))
        yield
        for (d, p), it in cur.items():
            s = gst[d, p]
            s_bd = jnp.concatenate([jnp.where(lo2, s, 0.0), jnp.where(lo2, 0.0, s)], axis=0)
            o = it["o"] + _bdot_nt(it["qd"], s_bd)
            o_refs[d][0, it["rows"], 2 * LANES * p:2 * LANES * (p + 1)] = o.astype(ACT)
            gst[d, p] = s * dec[d][:, it["sl"]] + it["upd"]
        yield

    n_stages = 12
    n_dep = 3
    gla_stages = (1, 4, 7, 10)
    pipeline = [(scan_step(s), gla_step(s)) for s in range(NCH)]
    for slot in range(n_stages + n_dep * (NCH - 1)):
        for s, (rw_gen, gla_gen) in enumerate(pipeline):
            stage = slot - n_dep * s
            if 0 <= stage < n_stages:
                next(rw_gen)
                if stage in gla_stages:
                    next(gla_gen)
    for (d, p), s in st.items():
        st_ref[d, p] = s
    for (d, p), s in gst.items():
        gst_ref[d, p] = s


def _bidir_specs(c, n_tiles):
    def back(j):
        return jnp.where(j == 0, 0, n_tiles - j)

    tok_f = pl.BlockSpec((1, TB, c), lambda b, j: (b, j, 0))
    tok_b = pl.BlockSpec((1, TB, c), lambda b, j: (b, back(j), 0))
    dir_f = pl.BlockSpec((1, 1, TB, c), lambda b, j: (0, b, j, 0))
    dir_b = pl.BlockSpec((1, 1, TB, c), lambda b, j: (1, b, back(j), 0))
    return tok_f, tok_b, dir_f, dir_b


def _scans(r, kk, v, lw, b, ke, q, k, gv, lg):
    B, N, _ = r.shape
    nt = N // TB
    tok_f, tok_b, dir_f, dir_b = _bidir_specs(RW_WIDTH, nt)
    kf, kb, dkf, dkb = _bidir_specs(GLA_KW, nt)
    out = jax.ShapeDtypeStruct((B, N, RW_WIDTH), ACT)
    assert GLA_WIDTH == RW_WIDTH
    yf, yb, of, ob = pl.pallas_call(
        _scan_kernel,
        out_shape=[out] * 4,
        grid=(B, nt),
        in_specs=[tok_f, tok_f, tok_f, dir_f, dir_f, dir_f, tok_b, tok_b, tok_b, dir_b, dir_b, dir_b,
                  kf, kf, tok_f, dkf, kb, kb, tok_b, dkb],
        out_specs=[tok_f, tok_b, tok_f, tok_b],
        scratch_shapes=[pltpu.VMEM((2, RW_WIDTH // LANES, LANES, LANES), F32),
                        pltpu.VMEM((2, GLA_KW // LANES, GLA_DV, LANES), F32)],
        compiler_params=_cparams(("parallel", "arbitrary")),
        name="scans",
    )(r, kk, v, lw, b, ke, r, kk, v, lw, b, ke, q, k, gv, lg, q, k, gv, lg)
    return (yf, yb), (of, ob)


CONV_K = TB + LANES


def _conv_shift_table():
    t = jnp.arange(TB)[:, None]
    s = jnp.arange(CONV_K)[None, :]
    prev = jnp.where(t == 0, s == TB + 2 * HALO - 1, s == t - 1)
    nxt = jnp.where(t == TB - 1, s == TB, s == t + 1)
    return jnp.stack([prev, nxt]).astype(BF16)


def _glaprep_kernel(n_tiles, j, p_ref, hp_ref, hn_ref, sh_ref, cw_ref, aup_ref, ab_ref, q_out, k_out, v_out,
                    og_out, lg_out):
    ub = p_ref[0][:, :GLA_QKV]
    u = ub.astype(F32)
    has_prev = jnp.where(j <= 1, 0.0, 1.0)
    has_next = jnp.where(jnp.logical_or(j == 0, j == n_tiles - 1), 0.0, 1.0)
    ext = jnp.concatenate([ub,
                           (hn_ref[0][:, :GLA_QKV].astype(F32) * has_next).astype(BF16),
                           (hp_ref[0][:, :GLA_QKV].astype(F32) * has_prev).astype(BF16),
                           jnp.zeros((CONV_K - TB - 2 * HALO, GLA_QKV), BF16)], axis=0)
    prev1 = jnp.dot(sh_ref[0], ext, preferred_element_type=F32)
    next1 = jnp.dot(sh_ref[1], ext, preferred_element_type=F32)
    cw = cw_ref[...]
    conv = cw[0:1] * prev1 + cw[1:2] * u + cw[2:3] * next1
    qkv = conv * _sigmoid(conv)
    q_out[0] = (qkv[:, :GLA_KW] * (GLA_DK ** -0.5)).astype(ACT)
    k_out[0] = qkv[:, GLA_KW:2 * GLA_KW].astype(ACT)
    v_out[0] = qkv[:, 2 * GLA_KW:].astype(ACT)
    og_out[0] = p_ref[0][:, GLA_QKV:GLA_QKV + GLA_WIDTH]
    z = _bdot(p_ref[0][:, GLA_QKV + GLA_WIDTH:], aup_ref[...]) + ab_ref[...]
    lg = (jnp.minimum(z, 0.0) - jnp.log1p(jnp.exp(-jnp.abs(z)))) * (1.0 / GLA_GATE_NORM)
    for d in range(2):
        lg_out[d, 0] = lg[:, d * GLA_KW:(d + 1) * GLA_KW]


def _glaprep_tile(p_gl, k):
    _, N, _ = p_gl.shape
    hb = TB // HALO
    nhb = N // HALO
    t = _group_tile(k)
    specs = [pl.BlockSpec((1, TB, GLA_PAD), lambda b, j: (b, t(j), 0)),
             pl.BlockSpec((1, HALO, GLA_PAD), lambda b, j: (b, jnp.maximum(t(j) * hb - 1, 0), 0)),
             pl.BlockSpec((1, HALO, GLA_PAD), lambda b, j: (b, jnp.minimum(t(j) * hb + hb, nhb - 1), 0))]
    return [p_gl, p_gl, p_gl], specs


def _glaprep_params(prm):
    specs = [_const_spec((2, TB, CONV_K)), _const_spec((3, GLA_QKV)), _const_spec((LANES, 2 * GLA_KW)),
             _const_spec((1, 2 * GLA_KW))]
    return [_conv_shift_table(), prm["conv"], prm["a_up"], prm["a_b"]], specs


def _glaprep_outs(B, N):
    shapes = [jax.ShapeDtypeStruct((B, N, GLA_KW), ACT), jax.ShapeDtypeStruct((B, N, GLA_KW), ACT),
              jax.ShapeDtypeStruct((B, N, GLA_WIDTH), ACT), jax.ShapeDtypeStruct((B, N, GLA_WIDTH), ACT),
              jax.ShapeDtypeStruct((2, B, N, GLA_KW), F32)]
    return shapes, [_group_out(GLA_KW), _group_out(GLA_KW), _group_out(GLA_WIDTH), _group_out(GLA_WIDTH),
                    _group_dir_out(GLA_KW)]


class _TileRows:
    def __init__(self, ref, rows):
        self.ref, self.rows = ref, rows

    def __setitem__(self, idx, value):
        idx = idx if isinstance(idx, tuple) else (idx,)
        self.ref[idx + (self.rows, slice(None))] = value


def _prep_kernel(has_vres, n_tiles, n_rw_tile, n_gla_tile, n_rw_par, n_gla_par, n_rw_out, *refs):
    rw_tiles = [refs[k * n_rw_tile:(k + 1) * n_rw_tile] for k in range(PREP_GROUP)]
    refs = refs[PREP_GROUP * n_rw_tile:]
    gla_tiles = [refs[k * n_gla_tile:(k + 1) * n_gla_tile] for k in range(PREP_GROUP)]
    refs = refs[PREP_GROUP * n_gla_tile:]
    rw_par, refs = refs[:n_rw_par], refs[n_rw_par:]
    gla_par, outs = refs[:n_gla_par], refs[n_gla_par:]
    n_base = n_rw_par - 3 if has_vres else n_rw_par
    for k in range(PREP_GROUP):
        t = _group_tile(k)(pl.program_id(1))
        views = [_TileRows(o, slice(k * TB, (k + 1) * TB)) for o in outs]
        tile = rw_tiles[k]
        rw_in = list(tile[:4]) + list(rw_par[:n_base])
        if has_vres:
            rw_in += [tile[4]] + list(rw_par[n_base:])
        _rwprep_kernel(has_vres, n_tiles, t, *rw_in, *views[:n_rw_out])
        _glaprep_kernel(n_tiles, t, *gla_tiles[k], *gla_par, *views[n_rw_out:])


def _prep(p_rw, p_gl, prm, gla_prm, v_first):
    B, N, _ = p_rw.shape
    nt = N // TB
    assert nt % PREP_GROUP == 0
    has_vres = v_first is not None
    args, specs = [], []
    for k in range(PREP_GROUP):
        a, s = _rwprep_tile(p_rw, v_first, k)
        args, specs, n_rw_tile = args + a, specs + s, len(a)
    for k in range(PREP_GROUP):
        a, s = _glaprep_tile(p_gl, k)
        args, specs, n_gla_tile = args + a, specs + s, len(a)
    rw_par, rw_par_specs = _rwprep_params(prm, has_vres)
    gla_par, gla_par_specs = _glaprep_params(gla_prm)
    rw_shapes, rw_out = _rwprep_outs(B, N)
    gla_shapes, gla_out = _glaprep_outs(B, N)
    outs = pl.pallas_call(
        functools.partial(_prep_kernel, has_vres, nt, n_rw_tile, n_gla_tile, len(rw_par), len(gla_par),
                          len(rw_shapes)),
        out_shape=rw_shapes + gla_shapes,
        grid=(B, nt // PREP_GROUP),
        in_specs=specs + rw_par_specs + gla_par_specs,
        out_specs=rw_out + gla_out,
        compiler_params=_cparams(("parallel", "parallel")),
        name="prep",
    )(*args, *rw_par, *gla_par)
    return outs[:len(rw_shapes)], outs[len(rw_shapes):]


def _readout_kernel(group, first, split_src, *refs):
    tok = [refs[i * group:(i + 1) * group] for i in range(7)]
    refs = refs[7 * group:]
    n_res = group + 1 if split_src else group
    res_refs, refs = refs[:n_res], refs[n_res:]
    mod_ref, gnw_ref, gnb_ref, ggn_ref, wout_ref, gpost_ref, bd_ref, xo_ref = refs
    bd = bd_ref[...]
    starts_with_ctx = jnp.logical_and(first == 0, pl.program_id(1) == 0)
    cats = []
    for k in range(group):
        yf_ref, yb_ref, bon_ref, g_ref, of_ref, ob_ref, og_ref = (t[k] for t in tok)
        y = yf_ref[0].astype(F32) + yb_ref[0].astype(F32)
        mu = _seg_sum(y, bd, exact=False) * (1.0 / RW_HEAD)
        yc = y - mu
        var = _seg_sum(yc * yc, bd, exact=False) * (1.0 / RW_HEAD)
        yn = yc * lax.rsqrt(var + RW_GN_EPS) * gnw_ref[...] + gnb_ref[...]
        rw = (yn + bon_ref[0].astype(F32)) * g_ref[0].astype(F32)
        o = of_ref[0].astype(F32) + ob_ref[0].astype(F32)
        og = og_ref[0].astype(F32)
        parts = [rw.astype(BF16)]
        for h in range(GLA_HEADS):
            sl = slice(GLA_DV * h, GLA_DV * (h + 1))
            oh = o[:, sl]
            on = oh * lax.rsqrt(jnp.mean(oh * oh, axis=-1, keepdims=True) + GLA_NORM_EPS)
            ogh = og[:, sl]
            parts.append((on * ggn_ref[:, sl] * (ogh * _sigmoid(ogh))).astype(BF16))
        cats.append(jnp.concatenate(parts, axis=1))
    mx_all = jnp.dot(jnp.concatenate(cats, axis=0), wout_ref[...], preferred_element_type=F32)
    for k in range(group):
        gate = mod_ref[0, 1, 2:3]
        if split_src:
            x_res = res_refs[k + 1][0]
            if k == 0:
                x_res = jnp.where(starts_with_ctx, res_refs[0][0], x_res)
        else:
            x_res = res_refs[k][0]
        if k == 0 and first == 0:
            gate = jnp.where(starts_with_ctx, mod_ref[0, 0, 2:3], gate)
        mx = mx_all[k * TB:(k + 1) * TB]
        xo_ref[0, k * TB:(k + 1) * TB, :] = x_res + gate * _rms(mx, gpost_ref[...])


def _readout(y, bonus, g, o, og, ctx, x, xc, modtab, prm, latents_only):
    B, N, _ = bonus.shape
    first = 1 if latents_only else 0
    n_tiles = N // TB - first
    group = max(k for k in (4, 3, 2, 1) if n_tiles % k == 0)

    def tiles(c, shift=0):
        return [pl.BlockSpec((1, TB, c), lambda b, j, k=k: (b, jnp.maximum(first + group * j + k + shift, 0), 0))
                for k in range(group)]

    if xc is None:
        res, res_specs = [ctx] + [x] * group, [pl.BlockSpec((1, TB, D), lambda b, j: (b, 0, 0))] + tiles(D, -1)
    else:
        res, res_specs = [xc] * group, tiles(D)
    tok_args, tok_specs = [], []
    for arr in (y[0], y[1], bonus, g, o[0], o[1], og):
        tok_args += [arr] * group
        tok_specs += tiles(arr.shape[-1])
    return pl.pallas_call(
        functools.partial(_readout_kernel, group, first, xc is None),
        out_shape=jax.ShapeDtypeStruct((B, n_tiles * TB, D), F32),
        grid=(B, n_tiles // group),
        in_specs=tok_specs + res_specs + [_wide_mod_spec(),
                  _const_spec((1, RW_WIDTH)), _const_spec((1, RW_WIDTH)), _const_spec((1, GLA_WIDTH)),
                  _const_spec((D, D)), _const_spec((1, D)), _const_spec((RW_WIDTH, RW_WIDTH))],
        out_specs=pl.BlockSpec((1, group * TB, D), lambda b, j: (b, j, 0)),
        compiler_params=_cparams(("parallel", "parallel")),
        name="readout",
    )(*tok_args, *res, modtab, prm["gn_w"], prm["gn_b"], prm["gla_gn_w"],
      prm["w_out"], prm["norm_post"], prm["bd64"])


def _swiglu_acc(hb, wg_ref, wu_ref, wd_ref):
    acc = jnp.zeros((hb.shape[0], D), F32)
    for c in range(D_FF // FCH):
        sl = slice(c * FCH, (c + 1) * FCH)
        gate = jnp.dot(hb, wg_ref[:, sl].astype(BF16), preferred_element_type=F32)
        up = jnp.dot(hb, wu_ref[:, sl].astype(BF16), preferred_element_type=F32)
        act = (gate * _sigmoid(gate) * up).astype(BF16)
        acc = acc + jnp.dot(act, wd_ref[sl, :].astype(BF16), preferred_element_type=F32)
    return acc


def _ffn_kernel(x_ref, mod_ref, gpre_ref, gpost_ref, wg_ref, wu_ref, wd_ref, xo_ref):
    x = x_ref[0]
    hb = (_rms(x, gpre_ref[...]) * (1.0 + _wide_mod(mod_ref, 4)) + _wide_mod(mod_ref, 3)).astype(BF16)
    fx = _swiglu_acc(hb, wg_ref, wu_ref, wd_ref)
    xo_ref[0] = x + _wide_mod(mod_ref, 5) * _rms(fx, gpost_ref[...])


def _single_buffered(shape):
    nd = len(shape)
    return pl.BlockSpec(shape, lambda *_: (0,) * nd, pipeline_mode=pl.Buffered(1))


def _ffn(xc, modtab, g_pre, g_post, wg, wu, wd):
    B, N, _ = xc.shape
    wide = pl.BlockSpec((1, TW, D), lambda b, j: (b, j, 0))
    return pl.pallas_call(
        _ffn_kernel,
        out_shape=jax.ShapeDtypeStruct((B, N, D), F32),
        grid=(B, N // TW),
        in_specs=[wide, _wide_mod_spec(), _const_spec((1, D)), _const_spec((1, D)),
                  _single_buffered((D, D_FF)), _single_buffered((D, D_FF)), _single_buffered((D_FF, D))],
        out_specs=wide,
        compiler_params=_cparams(("parallel", "parallel")),
        name="ffn",
    )(xc, modtab, g_pre, g_post, wg, wu, wd)


MOE_TB = 512
MOE_SEG = 16
MOE_TM = 512
MOE_R = 2 * MOE_TB + N_EXPERTS * MOE_SEG
MOE_NP = MOE_R // MOE_SEG


def _moe_route_kernel(x_ref, mod_ref, gpre_ref, router_ref, h_ref, info_ref, infot_ref, cnt_ref):
    mod = mod_ref[0, 0]
    h = _rms(x_ref[0], gpre_ref[...]) * (1.0 + mod[4:5]) + mod[3:4]
    hb = h.astype(BF16)
    h_ref[0] = hb
    lane = lax.broadcasted_iota(jnp.int32, (MOE_TB, LANES), 1)
    h_lo = (h - hb.astype(F32)).astype(BF16)
    logits = (jnp.dot(hb, router_ref[0], preferred_element_type=F32)
              + jnp.dot(hb, router_ref[1], preferred_element_type=F32)
              + jnp.dot(h_lo, router_ref[0], preferred_element_type=F32))
    logits = jnp.where(lane < N_EXPERTS, logits, -jnp.inf)
    v1 = jnp.max(logits, axis=-1, keepdims=True)
    i1 = jnp.min(jnp.where(logits == v1, lane, LANES), axis=-1, keepdims=True)
    rest = jnp.where(lane == i1, -jnp.inf, logits)
    v2 = jnp.max(rest, axis=-1, keepdims=True)
    i2 = jnp.min(jnp.where(rest == v2, lane, LANES), axis=-1, keepdims=True)
    ex = jnp.exp(v2 - v1)
    w1 = 1.0 / (1.0 + ex)
    w2 = ex * w1
    e1 = jnp.where(lane == i1, 1.0, 0.0)
    e2 = jnp.where(lane == i2, 1.0, 0.0)
    es = e1 + e2
    t = lax.broadcasted_iota(jnp.int32, (MOE_TB, MOE_TB), 0)
    s = lax.broadcasted_iota(jnp.int32, (MOE_TB, MOE_TB), 1)
    before = jnp.where(s < t, 1.0, 0.0).astype(BF16)
    rank = jnp.dot(before, es.astype(BF16), preferred_element_type=F32)
    cnt = jnp.sum(es, axis=0, keepdims=True)
    segs = jnp.floor((cnt + (MOE_SEG - 1)) * (1.0 / MOE_SEG))
    ea = lax.broadcasted_iota(jnp.int32, (LANES, LANES), 0)
    eb = lax.broadcasted_iota(jnp.int32, (LANES, LANES), 1)
    earlier = jnp.where(ea < eb, 1.0, 0.0).astype(BF16)
    start = jnp.dot(jnp.broadcast_to(segs, (SUBLANES, LANES)).astype(BF16), earlier,
                    preferred_element_type=F32)[0:1] * MOE_SEG
    pos = rank + start
    d1 = jnp.sum(e1 * pos, axis=-1, keepdims=True)
    d2 = jnp.sum(e2 * pos, axis=-1, keepdims=True)
    info = jnp.where(lane == 0, d1, jnp.where(lane == 1, d2, jnp.where(lane == 2, w1, jnp.where(lane == 3, w2, 0.0))))
    info_ref[0] = info
    infot_ref[0] = jnp.transpose(info)[0:SUBLANES]
    cnt_ref[0] = jnp.broadcast_to(cnt, (SUBLANES, LANES))


def _moe_route(xs, modtab, g_pre, router):
    B, S, _ = xs.shape
    per = S // MOE_TB
    nb = B * per
    blk = lambda c: pl.BlockSpec((1, MOE_TB, c), lambda i: (i // per, i % per, 0))
    flat = lambda r, c: pl.BlockSpec((1, r, c), lambda i: (i, 0, 0))
    return pl.pallas_call(
        _moe_route_kernel,
        out_shape=[jax.ShapeDtypeStruct((nb, MOE_TB, D), BF16), jax.ShapeDtypeStruct((nb, MOE_TB, LANES), F32),
                   jax.ShapeDtypeStruct((nb, SUBLANES, MOE_TB), F32),
                   jax.ShapeDtypeStruct((nb, SUBLANES, LANES), F32)],
        grid=(nb,),
        in_specs=[blk(D), pl.BlockSpec((1, 1, 6, D), lambda i: (i // per, 1, 0, 0)),
                  _const_spec((1, D)), _const_spec((2, D, LANES))],
        out_specs=[flat(MOE_TB, D), flat(MOE_TB, LANES), flat(SUBLANES, MOE_TB), flat(SUBLANES, LANES)],
        compiler_params=_cparams(("parallel",)),
        name="moe_route",
    )(xs, modtab, g_pre, router)


def _moe_plan(cnt, n_tiles):
    pc = (cnt + MOE_SEG - 1) // MOE_SEG * MOE_SEG
    inc = jnp.cumsum(pc, axis=1)
    loff = inc - pc
    reg = (jnp.sum(pc, axis=0) + MOE_TM - 1) // MOE_TM * MOE_TM
    gend = jnp.cumsum(reg)
    goff = (gend - reg)[None, :] + jnp.cumsum(pc, axis=0) - pc
    rows = jnp.arange(MOE_NP, dtype=jnp.int32) * MOE_SEG
    e_p = jnp.sum((rows[None, :, None] >= inc[:, None, :]).astype(jnp.int32), axis=-1)
    e_c = jnp.minimum(e_p, N_EXPERTS - 1)
    pick = (e_c[:, :, None] == jnp.arange(N_EXPERTS)[None, None, :]).astype(jnp.int32)
    dst = jnp.sum(pick * (goff - loff)[:, None, :], axis=-1) + rows[None, :]
    dst = jnp.where(e_p < N_EXPERTS, dst, 0).astype(jnp.int32)
    n_valid = (inc[:, -1] // MOE_SEG).astype(jnp.int32)
    trow = jnp.arange(n_tiles, dtype=jnp.int32) * MOE_TM
    te = jnp.sum((trow[:, None] >= gend[None, :]).astype(jnp.int32), axis=-1)
    valid = te < N_EXPERTS
    last = gend[-1] // MOE_TM - 1
    te = jnp.where(valid, te, te[last]).astype(jnp.int32)
    src = jnp.where(valid, jnp.arange(n_tiles, dtype=jnp.int32), last).astype(jnp.int32)
    fresh = valid & jnp.concatenate([jnp.ones((1,), bool), te[1:] != te[:-1]])
    return dst, n_valid, te, src, valid.astype(jnp.int32), fresh.astype(jnp.int32)


def _piece_copy(src_ref, src_row, dst_ref, dst_row, sem):
    return pltpu.make_async_copy(src_ref.at[pl.ds(src_row, MOE_SEG)], dst_ref.at[pl.ds(dst_row, MOE_SEG)], sem)


def _moe_gather_kernel(dst_ref, nv_ref, h_ref, infot_ref, xg_in_ref, xg_ref, buf_ref, sem_ref):
    del xg_in_ref
    i = pl.program_id(0)
    it = infot_ref[0]
    rr = lax.broadcasted_iota(jnp.int32, (MOE_R, MOE_TB), 0).astype(F32)
    onehot = jnp.where(rr == it[0:1], 1.0, jnp.where(rr == it[1:2], 1.0, 0.0)).astype(BF16)
    slot = i % 2
    buf_ref[slot] = jnp.dot(onehot, h_ref[0], preferred_element_type=F32).astype(BF16)

    def piece(blk, p):
        return _piece_copy(buf_ref.at[blk % 2], pl.multiple_of(p * MOE_SEG, MOE_SEG),
                           xg_ref, pl.multiple_of(dst_ref[blk, p], MOE_SEG), sem_ref.at[blk % 2, p])

    def start_even(q, c):
        piece(i, 2 * q).start(priority=0)
        return c

    def start_odd(q, c):
        piece(i, 2 * q + 1).start(priority=1)
        return c

    lax.fori_loop(0, (nv_ref[i] + 1) // 2, start_even, 0)
    lax.fori_loop(0, nv_ref[i] // 2, start_odd, 0)

    def wait_block(blk):
        def wait(p, c):
            piece(blk, p).wait()
            return c
        lax.fori_loop(0, nv_ref[blk], wait, 0)

    @pl.when(i > 0)
    def _():
        wait_block(i - 1)

    @pl.when(i == pl.num_programs(0) - 1)
    def _():
        wait_block(i)


def _moe_gather(dst, n_valid, h, info_t, n_rows):
    nb = h.shape[0]
    flat = lambda r, c: pl.BlockSpec((1, r, c), lambda i, *_: (i, 0, 0))
    return pl.pallas_call(
        _moe_gather_kernel,
        out_shape=jax.ShapeDtypeStruct((n_rows, D), BF16),
        grid_spec=pltpu.PrefetchScalarGridSpec(
            num_scalar_prefetch=2, grid=(nb,),
            in_specs=[flat(MOE_TB, D), flat(SUBLANES, MOE_TB), pl.BlockSpec(memory_space=pl.ANY)],
            out_specs=pl.BlockSpec(memory_space=pl.ANY),
            scratch_shapes=[pltpu.VMEM((2, MOE_R, D), BF16), pltpu.SemaphoreType.DMA((2, MOE_NP))]),
        input_output_aliases={4: 0},
        compiler_params=_cparams(("arbitrary",)),
        name="moe_gather",
    )(dst, n_valid, h, info_t, jnp.zeros((n_rows, D), BF16))


N_FCH = D_FF // FCH


def _moe_ffn_kernel(te_ref, src_ref, valid_ref, fresh_ref, x_ref, wg_hbm, wu_hbm, wd_hbm, y_ref,
                    wg_ref, wu_ref, wd_ref, sem_ref):
    i = pl.program_id(0)
    e = te_ref[i]
    fresh = fresh_ref[i] == 1

    def slice_copies(c):
        cols = pl.ds(c * FCH, FCH)
        return (pltpu.make_async_copy(wg_hbm.at[e, :, cols], wg_ref.at[:, cols], sem_ref.at[0, c]),
                pltpu.make_async_copy(wu_hbm.at[e, :, cols], wu_ref.at[:, cols], sem_ref.at[1, c]),
                pltpu.make_async_copy(wd_hbm.at[e, cols, :], wd_ref.at[cols, :], sem_ref.at[2, c]))

    @pl.when(fresh)
    def _():
        for c in range(N_FCH):
            for cp in slice_copies(c):
                cp.start()
        hb = x_ref[...]
        acc = jnp.zeros((MOE_TM, D), F32)
        for c in range(N_FCH):
            for cp in slice_copies(c):
                cp.wait()
            sl = slice(c * FCH, (c + 1) * FCH)
            gate = jnp.dot(hb, wg_ref[:, sl].astype(BF16), preferred_element_type=F32)
            up = jnp.dot(hb, wu_ref[:, sl].astype(BF16), preferred_element_type=F32)
            act = (gate * _sigmoid(gate) * up).astype(BF16)
            acc = acc + jnp.dot(act, wd_ref[sl, :].astype(BF16), preferred_element_type=F32)
        y_ref[...] = acc.astype(BF16)

    @pl.when(jnp.logical_and(valid_ref[i] == 1, jnp.logical_not(fresh)))
    def _():
        y_ref[...] = _swiglu_acc(x_ref[...], wg_ref, wu_ref, wd_ref).astype(BF16)

    @pl.when(valid_ref[i] == 0)
    def _():
        y_ref[...] = jnp.zeros_like(y_ref)


def _moe_ffn(te, src, valid, fresh, xg, wg, wu, wd):
    n_rows = xg.shape[0]
    hbm = pl.BlockSpec(memory_space=pl.ANY)
    return pl.pallas_call(
        _moe_ffn_kernel,
        out_shape=jax.ShapeDtypeStruct((n_rows, D), BF16),
        grid_spec=pltpu.PrefetchScalarGridSpec(
            num_scalar_prefetch=4, grid=(n_rows // MOE_TM,),
            in_specs=[pl.BlockSpec((MOE_TM, D), lambda i, te, src, *_: (src[i], 0)), hbm, hbm, hbm],
            out_specs=pl.BlockSpec((MOE_TM, D), lambda i, *_: (i, 0)),
            scratch_shapes=[pltpu.VMEM((D, D_FF), F32), pltpu.VMEM((D, D_FF), F32), pltpu.VMEM((D_FF, D), F32),
                            pltpu.SemaphoreType.DMA((3, N_FCH))]),
        compiler_params=_cparams(("arbitrary",)),
        name="moe_ffn",
    )(te, src, valid, fresh, xg, wg, wu, wd)


def _moe_combine_kernel(dst_ref, nv_ref, x_ref, mod_ref, gpost_ref, info_ref, yg_ref, xo_ref, buf_ref, sem_ref):
    i = pl.program_id(0)

    def piece(blk, p):
        return _piece_copy(yg_ref, pl.multiple_of(dst_ref[blk, p], MOE_SEG),
                           buf_ref.at[blk % 2], pl.multiple_of(p * MOE_SEG, MOE_SEG), sem_ref.at[blk % 2, p])

    def fetch(blk):
        def start_even(q, c):
            piece(blk, 2 * q).start(priority=0)
            return c

        def start_odd(q, c):
            piece(blk, 2 * q + 1).start(priority=1)
            return c

        def clear(p, c):
            buf_ref[blk % 2, pl.ds(pl.multiple_of(p * MOE_SEG, MOE_SEG), MOE_SEG), :] = jnp.zeros((MOE_SEG, D), BF16)
            return c

        lax.fori_loop(0, (nv_ref[blk] + 1) // 2, start_even, 0)
        lax.fori_loop(0, nv_ref[blk] // 2, start_odd, 0)
        lax.fori_loop(nv_ref[blk], MOE_NP, clear, 0)

    @pl.when(i == 0)
    def _():
        fetch(i)

    @pl.when(i + 1 < pl.num_programs(0))
    def _():
        fetch(i + 1)

    info = info_ref[0]
    rr = lax.broadcasted_iota(jnp.int32, (MOE_TB, MOE_R), 1).astype(F32)
    comb = jnp.where(rr == info[:, 0:1], info[:, 2:3], jnp.where(rr == info[:, 1:2], info[:, 3:4], 0.0)).astype(BF16)

    def wait(p, c):
        piece(i, p).wait()
        return c

    lax.fori_loop(0, nv_ref[i], wait, 0)
    fx = jnp.dot(comb, buf_ref[i % 2], preferred_element_type=F32)
    xo_ref[0] = x_ref[0] + mod_ref[0, 0][5:6] * _rms(fx, gpost_ref[...])


def _moe_combine(dst, n_valid, xs, modtab, g_post, info, yg):
    B, S, _ = xs.shape
    per = S // MOE_TB
    blk = pl.BlockSpec((1, MOE_TB, D), lambda i, *_: (i // per, i % per, 0))
    return pl.pallas_call(
        _moe_combine_kernel,
        out_shape=jax.ShapeDtypeStruct((B, S, D), F32),
        grid_spec=pltpu.PrefetchScalarGridSpec(
            num_scalar_prefetch=2, grid=(B * per,),
            in_specs=[blk, pl.BlockSpec((1, 1, 6, D), lambda i, *_: (i // per, 1, 0, 0)),
                      pl.BlockSpec((1, D), lambda i, *_: (0, 0)),
                      pl.BlockSpec((1, MOE_TB, LANES), lambda i, *_: (i, 0, 0)),
                      pl.BlockSpec(memory_space=pl.ANY)],
            out_specs=blk,
            scratch_shapes=[pltpu.VMEM((2, MOE_R, D), BF16), pltpu.SemaphoreType.DMA((2, MOE_NP))]),
        compiler_params=_cparams(("arbitrary",)),
        name="moe_combine",
    )(dst, n_valid, xs, modtab, g_post, info, yg)


def _moe(xs, modtab, g_pre, g_post, router, wg, wu, wd):
    B, S, _ = xs.shape
    nb = B * S // MOE_TB
    h, info, info_t, cnt = _moe_route(xs, modtab, g_pre, router)
    worst = 2 * B * S + nb * N_EXPERTS * (MOE_SEG - 1) + N_EXPERTS * (MOE_TM - 1)
    n_tiles = -(-worst // MOE_TM)
    dst, n_valid, te, src, valid, fresh = _moe_plan(cnt[:, 0, :N_EXPERTS].astype(jnp.int32), n_tiles)
    xg = _moe_gather(dst, n_valid, h, info_t, n_tiles * MOE_TM)
    yg = _moe_ffn(te, src, valid, fresh, xg, wg, wu, wd)
    return _moe_combine(dst, n_valid, xs, modtab, g_post, info, yg)


def _block_diag2(w):
    z = jnp.zeros_like(w[0])
    return jnp.concatenate([jnp.concatenate([w[0], z], axis=1), jnp.concatenate([z, w[1]], axis=1)], axis=0)


def _row(v):
    return v.reshape(1, -1).astype(F32)


def _head_ones(width, head):
    i = jnp.arange(width) // head
    return (i[:, None] == i[None, :]).astype(BF16)


def kernel(x, c, ctx, c_ctx, ada_w, ada_b, norm_mix_pre, norm_mix_post, norm_ffn_pre, norm_ffn_post, w_in, shift_mu, rw_w_up, rw_w0, rw_a_up, rw_a0, rw_k_k, rw_k_a, rw_r_k, rw_g_up, rw_gn_w, rw_gn_b, rw_v_down, rw_v_up, rw_v0, gla_conv, gla_a_up, gla_a_b, gla_gn_w, w_out, ffn_w_gate, ffn_w_up, ffn_w_down, moe_router, moe_w_gate, moe_w_up, moe_w_down):
    B, S, _ = x.shape
    n_ctx = ctx.shape[1]
    depth = w_in.shape[0]
    assert n_ctx == TB and S % MOE_TB == 0 and (n_ctx + S) % TW == 0 and depth == 2

    xc = None
    pad_rows = -(B + 1) % SUBLANES
    cvec = jnp.concatenate([c, c_ctx[None, :], jnp.zeros((pad_rows, D), F32)], axis=0)
    bd64 = _head_ones(RW_WIDTH, RW_HEAD)
    ada_b3 = ada_b.reshape(depth, 1, 6 * D)
    v_first = None
    out = None
    for i in range(depth):
        last = i == depth - 1
        mods = _adaln(cvec, ada_w, ada_b3, i)
        mod_x = mods[:B].reshape(B, 6, D)
        mod_c = jnp.broadcast_to(mods[B].reshape(1, 6, D), (B, 6, D))
        modtab = jnp.stack([mod_c, mod_x], axis=1)

        p_rw, p_gl = _inproj(ctx, x, xc, _row(norm_mix_pre[i]), modtab, w_in, i)

        prm = dict(
            mu=_row(shift_mu[i]),
            w_up=_block_diag2(rw_w_up[i]).astype(BF16), w0=_row(rw_w0[i]),
            a_up=_block_diag2(rw_a_up[i]).astype(BF16), a0=_row(rw_a0[i]),
            k_k=_row(rw_k_k[i]), k_a=_row(rw_k_a[i]), r_k=_row(rw_r_k[i]),
            g_up=rw_g_up[i].astype(BF16), bd64=bd64,
            gn_w=_row(rw_gn_w[i]), gn_b=_row(rw_gn_b[i]), gla_gn_w=_row(gla_gn_w[i]),
            w_out=w_out[i].astype(BF16), norm_post=_row(norm_mix_post[i]),
        )
        gate_pad = jnp.zeros((LANES - 2 * GLA_GATE_RANK, 2 * GLA_KW), F32)
        gla_prm = dict(conv=gla_conv[i].astype(F32), a_b=_row(gla_a_b[i]),
                       a_up=jnp.concatenate([_block_diag2(gla_a_up[i]), gate_pad], axis=0).astype(BF16))
        if i > 0:
            pad = LANES - RW_V_RANK
            prm["v_down"] = jnp.concatenate([rw_v_down[i - 1], jnp.zeros((RW_WIDTH, pad), F32)], axis=1).astype(BF16)
            prm["v_up"] = jnp.concatenate([rw_v_up[i - 1], jnp.zeros((pad, RW_WIDTH), F32)], axis=0).astype(BF16)
            prm["v0"] = _row(rw_v0[i - 1])

        (r, kk, vm, g, bonus, lw, bb, ke), (q, k, gv, og, lg) = _prep(p_rw, p_gl, prm, gla_prm,
                                                                     v_first if i > 0 else None)
        if i == 0:
            v_first = vm
        y, o = _scans(r, kk, vm, lw, bb, ke, q, k, gv, lg)
        xc = _readout(y, bonus, g, o, og, ctx, x, xc, modtab, prm, latents_only=last)

        jf = i // 2
        if i % 2 == 0:
            xc = _ffn(xc, modtab, _row(norm_ffn_pre[i]), _row(norm_ffn_post[i]),
                      ffn_w_gate[jf], ffn_w_up[jf], ffn_w_down[jf])
        else:
            router = jnp.concatenate([moe_router[jf], jnp.zeros((D, LANES - N_EXPERTS), F32)], axis=1)
            r_hi = router.astype(BF16)
            router = jnp.stack([r_hi, (router - r_hi.astype(F32)).astype(BF16)])
            out = _moe(xc, modtab, _row(norm_ffn_pre[i]), _row(norm_ffn_post[i]), router,
                       moe_w_gate[jf], moe_w_up[jf], moe_w_down[jf])
    return out
```
